```python
import math
import jax, jax.numpy as jnp
from jax import lax
import numpy as np

D_MODEL = 1024
BATCH = 8
SEQ = 4096
DEPTH = 1

D_MIX = D_MODEL
HEAD_DIM = 64
ATTN_HEADS = 8
ATTN_WIDTH = ATTN_HEADS * HEAD_DIM
CONV_WIDTH = D_MIX - ATTN_WIDTH
CONV_GROUPS = 8
CONV_GROUP_DIM = CONV_WIDTH // CONV_GROUPS
CONV_K = 3
IDX_HEADS = 8
IDX_DIM = 64
TOPK_MAX = 256
Q_BLOCK = 128
N_BUCKETS = 32
MAX_DISTANCE = 128
N_GROUPS = 4
EXPERTS_PER_GROUP = 8
N_EXPERTS = N_GROUPS * EXPERTS_PER_GROUP
EXPERT_FF = 256
TOP_K_EXPERTS = 2
EPS = 1e-6
SPLITS = (ATTN_WIDTH, ATTN_WIDTH, ATTN_WIDTH, IDX_HEADS * IDX_DIM, IDX_DIM, IDX_HEADS,
          CONV_WIDTH, CONV_WIDTH, CONV_WIDTH)
N_IN = ATTN_WIDTH * 3 + IDX_HEADS * IDX_DIM + IDX_DIM + IDX_HEADS + CONV_WIDTH * 3

kernel_name = "hybrid_dsa_shortconv_hmoe_block"


def rmsnorm(x, g):
    xf = x.astype(jnp.float32)
    y = xf * lax.rsqrt(jnp.mean(xf * xf, axis=-1, keepdims=True) + EPS)
    return (y * g.astype(jnp.float32)).astype(x.dtype)


def t5_bucket(rel):
    max_exact = N_BUCKETS // 2
    n = jnp.maximum(rel, 0)
    nf = jnp.maximum(n, 1).astype(jnp.float32)
    large = max_exact + (jnp.log(nf / max_exact) / math.log(MAX_DISTANCE / max_exact)
                         * (N_BUCKETS - max_exact)).astype(jnp.int32)
    large = jnp.minimum(large, N_BUCKETS - 1)
    return jnp.where(n < max_exact, n, large)


def split_cols(p):
    offs = [int(o) for o in np.cumsum(np.array(SPLITS))[:-1]]
    return jnp.split(p, offs, axis=-1)


def dsa_attention(q, k, v, qi, ki, wi, rel_bias):
    B, L = q.shape[0], q.shape[1]
    topk = min(TOPK_MAX, L // 4)
    qb = min(Q_BLOCK, L)
    n_blocks = L // qb
    scale = HEAD_DIM ** -0.5
    idx_scale = (IDX_DIM ** -0.5) * (IDX_HEADS ** -0.5)
    key_pos = jnp.arange(L, dtype=jnp.int32)
    kif = ki.astype(jnp.float32)
    gather = jax.vmap(lambda arr, ids: arr[ids])

    def block(b):
        t0 = b * qb
        q_b = lax.dynamic_slice_in_dim(q, t0, qb, axis=1)
        qi_b = lax.dynamic_slice_in_dim(qi, t0, qb, axis=1).astype(jnp.float32)
        wi_b = lax.dynamic_slice_in_dim(wi, t0, qb, axis=1).astype(jnp.float32)
        q_pos = t0 + jnp.arange(qb, dtype=jnp.int32)
        dots = jnp.einsum('bqhd,bsd->bqhs', qi_b, kif)
        score = jnp.einsum('bqhs,bqh->bqs', jax.nn.relu(dots), wi_b) * idx_scale
        causal = key_pos[None, :] <= q_pos[:, None]
        score = jnp.where(causal[None], score, -jnp.inf)
        _, sel = lax.top_k(score, topk)
        k_sel = gather(k, sel)
        v_sel = gather(v, sel)
        logits = jnp.einsum('bqhd,bqkhd->bhqk', q_b, k_sel).astype(jnp.float32) * scale
        rel = q_pos[None, :, None] - sel
        bias = rel_bias[t5_bucket(rel)].astype(jnp.float32)
        logits = logits + jnp.transpose(bias, (0, 3, 1, 2))
        logits = jnp.where((rel >= 0)[:, None], logits, -jnp.inf)
        p = jax.nn.softmax(logits, axis=-1).astype(v.dtype)
        return jnp.einsum('bhqk,bqkhd->bqhd', p, v_sel)

    out = lax.map(block, jnp.arange(n_blocks, dtype=jnp.int32))
    return jnp.transpose(out, (1, 0, 2, 3, 4)).reshape(B, L, ATTN_HEADS, HEAD_DIM)


def hybrid_mixer(h, w_in, q_norm, k_norm, conv_w, attn_out_norm, conv_out_norm, w_out, rel_bias):
    B, L, _ = h.shape
    proj = jnp.einsum('btd,dn->btn', h, w_in)
    q, k, v, qi, ki, wi, gate_b, gate_c, u = split_cols(proj)
    q = rmsnorm(q.reshape(B, L, ATTN_HEADS, HEAD_DIM), q_norm)
    k = rmsnorm(k.reshape(B, L, ATTN_HEADS, HEAD_DIM), k_norm)
    v = v.reshape(B, L, ATTN_HEADS, HEAD_DIM)
    qi = qi.reshape(B, L, IDX_HEADS, IDX_DIM)
    attn = dsa_attention(q, k, v, qi, ki, wi, rel_bias)
    z = gate_c * u
    conv = lax.conv_general_dilated(
        z, conv_w[:, None, :].astype(z.dtype), window_strides=(1,),
        padding=[(CONV_K - 1, 0)], dimension_numbers=('NWC', 'WIO', 'NWC'),
        feature_group_count=CONV_WIDTH)
    y_conv = (gate_b * conv).reshape(B, L, CONV_GROUPS, CONV_GROUP_DIM)
    merged = jnp.concatenate([
        rmsnorm(attn, attn_out_norm).reshape(B, L, ATTN_WIDTH),
        rmsnorm(y_conv, conv_out_norm).reshape(B, L, CONV_WIDTH)], axis=-1)
    return jnp.einsum('btm,md->btd', merged, w_out)


def hier_moe(h, w_group_router, b_group_router, w_expert_router, b_expert_router,
             w_gate, w_up, w_down):
    B, L, D = h.shape
    xt = h.reshape(-1, D)
    g_logits = (xt @ w_group_router + b_group_router).astype(jnp.float32)
    g_prob = jax.nn.softmax(g_logits, axis=-1)
    g_sel = jnp.argmax(g_logits, axis=-1)
    p_g = jnp.take_along_axis(g_prob, g_sel[:, None], axis=1)[:, 0]
    e_logits = (xt @ w_expert_router + b_expert_router).astype(jnp.float32)
    e_logits = e_logits.reshape(-1, N_GROUPS, EXPERTS_PER_GROUP)
    e_in = jnp.take_along_axis(e_logits, g_sel[:, None, None], axis=1)[:, 0]
    e_prob = jax.nn.softmax(e_in, axis=-1)
    top_w, top_i = lax.top_k(e_prob, TOP_K_EXPERTS)
    top_w = top_w / jnp.sum(top_w, axis=-1, keepdims=True)
    w_e = jnp.einsum('nk,nke->ne', top_w, jax.nn.one_hot(top_i, EXPERTS_PER_GROUP, dtype=jnp.float32))
    combine = (jax.nn.one_hot(g_sel, N_GROUPS, dtype=jnp.float32)[:, :, None]
               * (p_g[:, None] * w_e)[:, None, :]).astype(xt.dtype)
    wg = w_gate.reshape(N_GROUPS, EXPERTS_PER_GROUP, D, EXPERT_FF)
    wu = w_up.reshape(N_GROUPS, EXPERTS_PER_GROUP, D, EXPERT_FF)
    wd = w_down.reshape(N_GROUPS, EXPERTS_PER_GROUP, EXPERT_FF, D)
    y = jnp.zeros_like(xt)
    for g in range(N_GROUPS):
        a = jnp.einsum('nd,edf->nef', xt, wg[g])
        b = jnp.einsum('nd,edf->nef', xt, wu[g])
        hid = jax.nn.silu(a) * b * combine[:, g, :, None]
        y = y + jnp.einsum('nef,efd->nd', hid, wd[g])
    return y.reshape(B, L, D)


def setup_inputs(seed: int = 0) -> dict:
    key = jax.random.key(seed)
    ks = jax.random.split(key, 24)
    f32 = jnp.float32

    def nrm(k, shape, scale):
        return jax.random.normal(k, shape, f32) * scale

    def gain(k, shape):
        return jnp.ones(shape, f32) + 0.02 * jax.random.normal(k, shape, f32)

    D, Dp = D_MODEL, DEPTH
    return {
        "x": nrm(ks[0], (BATCH, SEQ, D), 1.0),
        "c": nrm(ks[1], (BATCH, D), 1.0),
        "rel_bias": nrm(ks[2], (N_BUCKETS, ATTN_HEADS), 0.5),
        "w_ada": nrm(ks[3], (Dp, D, 6 * D), 0.5 * D ** -0.5),
        "b_ada": nrm(ks[4], (Dp, 6 * D), 0.02),
        "norm1": gain(ks[5], (Dp, D)),
        "w_in": nrm(ks[6], (Dp, D, N_IN), D ** -0.5),
        "q_norm": gain(ks[7], (Dp, HEAD_DIM)),
        "k_norm": gain(ks[8], (Dp, HEAD_DIM)),
        "conv_w": nrm(ks[9], (Dp, CONV_K, CONV_WIDTH), CONV_K ** -0.5),
        "attn_out_norm": gain(ks[10], (Dp, ATTN_HEADS, HEAD_DIM)),
        "conv_out_norm": gain(ks[11], (Dp, CONV_GROUPS, CONV_GROUP_DIM)),
        "w_out": nrm(ks[12], (Dp, D_MIX, D), D_MIX ** -0.5),
        "norm2": gain(ks[13], (Dp, D)),
        "w_group_router": nrm(ks[14], (Dp, D, N_GROUPS), D ** -0.5),
        "b_group_router": nrm(ks[15], (Dp, N_GROUPS), 0.01),
        "w_expert_router": nrm(ks[16], (Dp, D, N_EXPERTS), D ** -0.5),
        "b_expert_router": nrm(ks[17], (Dp, N_EXPERTS), 0.01),
        "w_gate": nrm(ks[18], (Dp, N_EXPERTS, D, EXPERT_FF), D ** -0.5),
        "w_up": nrm(ks[19], (Dp, N_EXPERTS, D, EXPERT_FF), D ** -0.5),
        "w_down": nrm(ks[20], (Dp, N_EXPERTS, EXPERT_FF, D), EXPERT_FF ** -0.5),
    }


def reference(x, c, rel_bias, w_ada, b_ada, norm1, w_in, q_norm, k_norm, conv_w,
              attn_out_norm, conv_out_norm, w_out, norm2, w_group_router, b_group_router,
              w_expert_router, b_expert_router, w_gate, w_up, w_down):
    c_act = jax.nn.silu(c)
    for l in range(DEPTH):
        mod = jnp.einsum('bd,dm->bm', c_act, w_ada[l]) + b_ada[l]
        shift1, scale1, gate1, shift2, scale2, gate2 = jnp.split(mod[:, None, :], 6, axis=-1)
        h = rmsnorm(x, norm1[l]) * (1 + scale1) + shift1
        mix = hybrid_mixer(h, w_in[l], q_norm[l], k_norm[l], conv_w[l], attn_out_norm[l],
                           conv_out_norm[l], w_out[l], rel_bias)
        x = x + gate1 * mix
        h2 = rmsnorm(x, norm2[l]) * (1 + scale2) + shift2
        ffn = hier_moe(h2, w_group_router[l], b_group_router[l], w_expert_router[l],
                       b_expert_router[l], w_gate[l], w_up[l], w_down[l])
        x = x + gate2 * ffn
    return x
```

```python
import functools
import math

import jax
import jax.numpy as jnp
import numpy as np
from jax import lax
from jax.experimental import pallas as pl
from jax.experimental.pallas import tpu as pltpu

F32 = jnp.float32
BF16 = jnp.bfloat16

D_MODEL = 1024
HEAD_DIM = 64
ATTN_HEADS = 8
ATTN_WIDTH = ATTN_HEADS * HEAD_DIM
CONV_WIDTH = D_MODEL - ATTN_WIDTH
CONV_GROUP_DIM = 64
CONV_K = 3
IDX_HEADS = 8
IDX_DIM = 64
TOPK_MAX = 256
IDX_SCALE = (IDX_DIM ** -0.5) * (IDX_HEADS ** -0.5)
N_BUCKETS = 32
MAX_DISTANCE = 128
N_GROUPS = 4
EXPERTS_PER_GROUP = 8
N_EXPERTS = N_GROUPS * EXPERTS_PER_GROUP
EXPERT_FF = 256
EPS = 1e-6
LOG2E = 1.4426950408889634
NEG_BIG = -1e30
BISECT_MAX_STEPS = 300

LANES = 128
SUBLANES = 8
VMEM_LIMIT_BYTES = 56 * 1024 * 1024

PRE_TM = 512
ATT_TQ = 256
ATT_TK = 256
POST_TM = 512
MOE_TM = 1024
MOD_TN = 1536

_NT_DIMS = (((1,), (1,)), ((), ()))


def _bucket_boundaries():
    max_exact = N_BUCKETS // 2
    d = np.arange(0, 4 * MAX_DISTANCE, dtype=np.int64)
    nf = np.maximum(d, 1).astype(np.float32)
    large = max_exact + (np.log(nf / np.float32(max_exact)) / np.float32(math.log(MAX_DISTANCE / max_exact))
                         * np.float32(N_BUCKETS - max_exact)).astype(np.int32)
    large = np.minimum(large, N_BUCKETS - 1)
    bucket = np.where(d < max_exact, d, large)
    assert np.all(np.diff(bucket) >= 0) and bucket[-1] == N_BUCKETS - 1
    bounds = [int(np.argmax(bucket >= j)) for j in range(1, N_BUCKETS)]
    return np.asarray([0] + bounds, dtype=np.int32)


def _mod_kernel(c_ref, w_ref, b_ref, o_ref):
    c = c_ref[...]
    act = c * jax.nn.sigmoid(c)
    o_ref[...] = jnp.dot(act, w_ref[...], preferred_element_type=F32,
                         precision=lax.Precision.HIGHEST) + b_ref[...]


def _mod_call(c, w_ada, b_ada):
    bsz, d = c.shape
    n = w_ada.shape[1]
    return pl.pallas_call(
        _mod_kernel,
        out_shape=jax.ShapeDtypeStruct((bsz, n), F32),
        grid=(n // MOD_TN,),
        in_specs=[pl.BlockSpec((bsz, d), lambda j: (0, 0)),
                  pl.BlockSpec((d, MOD_TN), lambda j: (0, j)),
                  pl.BlockSpec((1, MOD_TN), lambda j: (0, j))],
        out_specs=pl.BlockSpec((bsz, MOD_TN), lambda j: (0, j)),
        compiler_params=pltpu.CompilerParams(dimension_semantics=("arbitrary",),
                                             vmem_limit_bytes=VMEM_LIMIT_BYTES),
        name="adaln_mod",
    )(c, w_ada, b_ada)


def _group_rms(y, g_ref):
    ms = jnp.dot((y * y).astype(BF16), g_ref[...], preferred_element_type=F32)
    return y * lax.rsqrt(ms + EPS)


def _pre_kernel(x_ref, mod_ref, n1_ref, wm_ref, wvt_ref, wki_ref, wwit_ref, qg_ref, kg_ref,
                cw_ref, cg_ref, g_ref,
                q_ref, k_ref, vt_ref, qi_ref, ki_ref, wit_ref, cn_ref, carry_ref):
    j = pl.program_id(1)
    tm = x_ref.shape[0]
    aw = ATTN_WIDTH

    x = x_ref[...]
    ms = jnp.mean(x * x, axis=-1, keepdims=True)
    y = x * lax.rsqrt(ms + EPS) * n1_ref[...]
    h = y * (1.0 + mod_ref[1:2, :]) + mod_ref[0:1, :]
    hb = h.astype(BF16)

    def proj(lo):
        return jnp.dot(hb, wm_ref[:, lo:lo + aw], preferred_element_type=F32)

    q = _group_rms(proj(0), g_ref) * qg_ref[...]
    q_ref[...] = q.astype(BF16)
    k = _group_rms(proj(aw), g_ref) * kg_ref[...]
    k_ref[...] = k.astype(BF16)

    vt = lax.dot_general(wvt_ref[...], hb, _NT_DIMS, preferred_element_type=F32).astype(BF16)
    for cc in range(tm // ATT_TK):
        vt_ref[cc] = vt[:, cc * ATT_TK:(cc + 1) * ATT_TK]

    qi_ref[...] = proj(2 * aw).astype(BF16)
    ki_ref[...] = jnp.dot(hb, wki_ref[...], preferred_element_type=F32).astype(BF16)
    wit_ref[...] = lax.dot_general(wwit_ref[...], hb, _NT_DIMS, preferred_element_type=F32) * IDX_SCALE

    gate_b = proj(3 * aw)
    z = proj(4 * aw) * proj(5 * aw)

    @pl.when(j == 0)
    def _():
        carry_ref[...] = jnp.zeros_like(carry_ref)

    prev = carry_ref[...]
    row = lax.broadcasted_iota(jnp.int32, z.shape, 0)
    z1 = jnp.where(row == 0, prev[SUBLANES - 1:SUBLANES, :], pltpu.roll(z, 1, 0))
    z2 = pltpu.roll(z, 2, 0)
    z2 = jnp.where(row == 0, prev[SUBLANES - 2:SUBLANES - 1, :], z2)
    z2 = jnp.where(row == 1, prev[SUBLANES - 1:SUBLANES, :], z2)
    carry_ref[...] = z[tm - SUBLANES:, :]
    conv = cw_ref[2:3, :] * z + cw_ref[1:2, :] * z1 + cw_ref[0:1, :] * z2
    yc = gate_b * conv
    cn_ref[...] = (_group_rms(yc, g_ref) * cg_ref[...]).astype(BF16)


def _pre_call(x, mod, n1, wm, wvt, wki, wwit, qg, kg, cw, cg, gmat):
    bsz, seq, d = x.shape
    tm = PRE_TM
    nck = tm // ATT_TK
    aw = ATTN_WIDTH
    const = lambda b, j: (0, 0)
    tok = lambda b, j: (b, j, 0)
    out_shape = (
        jax.ShapeDtypeStruct((bsz, seq, aw), BF16),
        jax.ShapeDtypeStruct((bsz, seq, aw), BF16),
        jax.ShapeDtypeStruct((bsz, seq // ATT_TK, aw, ATT_TK), BF16),
        jax.ShapeDtypeStruct((bsz, seq, aw), BF16),
        jax.ShapeDtypeStruct((bsz, seq, LANES), BF16),
        jax.ShapeDtypeStruct((bsz, IDX_HEADS, seq), F32),
        jax.ShapeDtypeStruct((bsz, seq, CONV_WIDTH), BF16),
    )
    out_specs = (
        pl.BlockSpec((None, tm, aw), tok),
        pl.BlockSpec((None, tm, aw), tok),
        pl.BlockSpec((None, nck, aw, ATT_TK), lambda b, j: (b, j, 0, 0)),
        pl.BlockSpec((None, tm, aw), tok),
        pl.BlockSpec((None, tm, LANES), tok),
        pl.BlockSpec((None, IDX_HEADS, tm), lambda b, j: (b, 0, j)),
        pl.BlockSpec((None, tm, CONV_WIDTH), tok),
    )
    in_specs = [
        pl.BlockSpec((None, tm, d), tok),
        pl.BlockSpec((None, 6, d), lambda b, j: (b, 0, 0)),
        pl.BlockSpec(n1.shape, const),
        pl.BlockSpec(wm.shape, const),
        pl.BlockSpec(wvt.shape, const),
        pl.BlockSpec(wki.shape, const),
        pl.BlockSpec(wwit.shape, const),
        pl.BlockSpec(qg.shape, const),
        pl.BlockSpec(kg.shape, const),
        pl.BlockSpec(cw.shape, const),
        pl.BlockSpec(cg.shape, const),
        pl.BlockSpec(gmat.shape, const),
    ]
    return pl.pallas_call(
        _pre_kernel,
        out_shape=out_shape,
        grid=(bsz, seq // tm),
        in_specs=in_specs,
        out_specs=out_specs,
        scratch_shapes=[pltpu.VMEM((SUBLANES, CONV_WIDTH), F32)],
        compiler_params=pltpu.CompilerParams(dimension_semantics=("arbitrary", "arbitrary"),
                                             vmem_limit_bytes=VMEM_LIMIT_BYTES),
        name="pre_proj",
    )(x, mod, n1, wm, wvt, wki, wwit, qg, kg, cw, cg, gmat)


def _attn_kernel(rb_ref, bnd_ref, q_ref, qi_ref, wit_ref, k_ref, ki_ref, vt_ref, og_ref,
                 o_ref,
                 s_ref, bias_ref, qpad_ref, qipad_ref, acc_ref, m_ref, l_ref, *, topk):
    b = pl.program_id(0)
    i = pl.program_id(1)
    tq, tk = ATT_TQ, ATT_TK
    nh, hd = ATTN_HEADS, HEAD_DIM

    t_loc = lax.broadcasted_iota(jnp.int32, (tk, tq), 1)
    s_loc = lax.broadcasted_iota(jnp.int32, (tk, tq), 0)

    @pl.when((b == 0) & (i == 0))
    def _():
        for idx in range(2):
            dist = t_loc - s_loc + idx * tq
            for h in range(nh):
                bias_ref[idx, h] = jnp.full((tk, tq), (rb_ref[0, h] - rb_ref[N_BUCKETS - 1, h]) * LOG2E, F32)

            def fill(jb, carry):
                reached = dist >= bnd_ref[jb]
                for h in range(nh):
                    val = (rb_ref[jb, h] - rb_ref[N_BUCKETS - 1, h]) * LOG2E
                    bias_ref[idx, h] = jnp.where(reached, val, bias_ref[idx, h])
                return carry

            lax.fori_loop(1, N_BUCKETS, fill, 0)

    lane = lax.broadcasted_iota(jnp.int32, (tq, LANES), 1)
    for h in range(nh):
        pair = slice((h // 2) * LANES, (h // 2 + 1) * LANES)
        keep = (lane // hd) == (h % 2)
        qpad_ref[h] = jnp.where(keep, q_ref[:, pair], jnp.zeros((), BF16))
        qipad_ref[h] = jnp.where(keep, qi_ref[:, pair], jnp.zeros((), BF16))

    def score_chunk(c):
        kic = ki_ref[pl.ds(pl.multiple_of(c * tk, tk), tk), :]
        acc = jnp.zeros((tk, tq), F32)
        for h in range(nh):
            e = lax.dot_general(kic, qipad_ref[h], _NT_DIMS, preferred_element_type=F32)
            acc = acc + wit_ref[h:h + 1, :] * jnp.maximum(e, 0.0)
        return acc

    def a_body(c, carry):
        rmin, rmax = carry
        sc = score_chunk(c)
        s_ref[c] = sc
        return (jnp.minimum(rmin, jnp.min(sc, axis=0, keepdims=True)),
                jnp.maximum(rmax, jnp.max(sc, axis=0, keepdims=True)))

    rmin0 = jnp.full((1, tq), jnp.inf, F32)
    rmax0 = jnp.full((1, tq), -jnp.inf, F32)
    rmin, rmax = lax.fori_loop(0, i, a_body, (rmin0, rmax0))
    sc = score_chunk(i)
    causal = s_loc <= t_loc
    s_ref[i] = jnp.where(causal, sc, -jnp.inf)
    rmin = jnp.minimum(rmin, jnp.min(jnp.where(causal, sc, jnp.inf), axis=0, keepdims=True))
    rmax = jnp.maximum(rmax, jnp.max(jnp.where(causal, sc, -jnp.inf), axis=0, keepdims=True))

    def count_ge(thr):
        def body(c, acc):
            hit = jnp.where(s_ref[c] >= thr, 1.0, 0.0)
            return acc + jnp.sum(hit.reshape(tk // SUBLANES, SUBLANES, tq), axis=0)
        acc = lax.fori_loop(0, i + 1, body, jnp.zeros((SUBLANES, tq), F32))
        return jnp.sum(acc, axis=0, keepdims=True)

    t_glob = (i * tq + lax.broadcasted_iota(jnp.int32, (1, tq), 1)).astype(F32)
    n_causal = t_glob + 1.0
    kf = jnp.minimum(float(topk), n_causal)
    c_max = count_ge(rmax)
    all_sel = n_causal <= kf
    max_ge = c_max >= kf
    active0 = jnp.where(all_sel | max_ge, 0.0, 1.0)
    thr0 = jnp.where(all_sel, rmin, rmax)
    tie0 = jnp.where(jnp.logical_not(all_sel) & (c_max > kf), 1.0, 0.0)
    hif0 = jnp.full((1, tq), jnp.inf, F32)
    need0 = kf

    def b_cond(st):
        return (jnp.max(st[0]) > 0.0) & (st[8] <= BISECT_MAX_STEPS)

    def b_body(st):
        active, lo, hi, fhi, thr, tie, hif, need, step = st
        mid = lo + (hi - lo) * 0.5
        collapsed = (mid <= lo) | (mid >= hi) | (step >= BISECT_MAX_STEPS)
        cm = count_ge(mid)
        act = active > 0.0
        live = act & jnp.logical_not(collapsed)
        found = live & (cm == kf)
        go_up = live & (cm > kf)
        go_dn = live & (cm < kf)
        ends_tie = act & collapsed
        thr = jnp.where(found, mid, jnp.where(ends_tie, lo, thr))
        tie = jnp.where(ends_tie, 1.0, tie)
        hif = jnp.where(ends_tie, hi, hif)
        need = jnp.where(ends_tie, kf - fhi, need)
        lo = jnp.where(go_up, mid, lo)
        fhi = jnp.where(go_dn, cm, fhi)
        hi = jnp.where(go_dn, mid, hi)
        active = jnp.where(found | ends_tie, 0.0, active)
        return active, lo, hi, fhi, thr, tie, hif, need, step + 1

    _, _, _, _, thr, tie, hif, need, _ = lax.while_loop(
        b_cond, b_body, (active0, rmin, rmax, c_max, thr0, tie0, hif0, need0, jnp.int32(0)))

    @pl.when(jnp.max(tie) > 0.0)
    def _():
        tri = jnp.where(lax.broadcasted_iota(jnp.int32, (tk, tk), 1)
                        <= lax.broadcasted_iota(jnp.int32, (tk, tk), 0), 1.0, 0.0).astype(BF16)

        def body(c, seen):
            sc_c = s_ref[c]
            tied = (sc_c >= thr) & (sc_c < hif) & (tie > 0.0)
            rank = jnp.dot(tri, jnp.where(tied, 1.0, 0.0).astype(BF16), preferred_element_type=F32) + seen
            s_ref[c] = jnp.where(tied & (rank > need), -jnp.inf, sc_c)
            return rank[tk - 1:tk, :]

        lax.fori_loop(0, i + 1, body, jnp.zeros((1, tq), F32))

    m_ref[...] = jnp.full(m_ref.shape, NEG_BIG, F32)
    l_ref[...] = jnp.zeros(l_ref.shape, F32)
    acc_ref[...] = jnp.zeros(acc_ref.shape, F32)

    def attend(c, bias_idx):
        mask = s_ref[c] >= thr
        row0 = pl.multiple_of(c * tk, tk)
        for h in range(nh):
            kc = k_ref[pl.ds(row0, tk), (h // 2) * LANES:(h // 2 + 1) * LANES]
            lt = lax.dot_general(kc, qpad_ref[h], _NT_DIMS, preferred_element_type=F32)
            if bias_idx is not None:
                lt = lt + bias_ref[bias_idx, h]
            lm = jnp.where(mask, lt, NEG_BIG)
            m_old = m_ref[h:h + 1, :]
            m_new = jnp.maximum(m_old, jnp.max(lm, axis=0, keepdims=True))
            p = jnp.exp2(lm - m_new)
            alpha = jnp.exp2(m_old - m_new)
            l_ref[h:h + 1, :] = alpha * l_ref[h:h + 1, :] + jnp.sum(p, axis=0, keepdims=True)
            m_ref[h:h + 1, :] = m_new
            pv = jnp.dot(vt_ref[c, h * hd:(h + 1) * hd, :], p.astype(BF16), preferred_element_type=F32)
            acc_ref[h * hd:(h + 1) * hd, :] = alpha * acc_ref[h * hd:(h + 1) * hd, :] + pv

    def far_body(c, carry):
        attend(c, None)
        return carry

    lax.fori_loop(0, jnp.maximum(i - 1, 0), far_body, 0)

    @pl.when(i >= 1)
    def _():
        attend(i - 1, 1)

    attend(i, 0)

    for h in range(nh):
        o = acc_ref[h * hd:(h + 1) * hd, :] / l_ref[h:h + 1, :]
        ms = jnp.mean(o * o, axis=0, keepdims=True)
        acc_ref[h * hd:(h + 1) * hd, :] = o * lax.rsqrt(ms + EPS)
    o_ref[...] = (acc_ref[...].T * og_ref[...]).astype(BF16)


def _attn_call(rel_bias, bounds, q, qi, wit, k, ki, vt, og, topk):
    bsz, seq, aw = q.shape
    tq, tk = ATT_TQ, ATT_TK
    nck = seq // tk
    blk_q = lambda b, i: (b, i, 0)
    whole = lambda b, i: (b, 0, 0)
    smem = pl.BlockSpec(memory_space=pltpu.SMEM)
    return pl.pallas_call(
        functools.partial(_attn_kernel, topk=topk),
        out_shape=jax.ShapeDtypeStruct((bsz, seq, aw), BF16),
        grid=(bsz, seq // tq),
        in_specs=[
            smem, smem,
            pl.BlockSpec((None, tq, aw), blk_q),
            pl.BlockSpec((None, tq, aw), blk_q),
            pl.BlockSpec((None, IDX_HEADS, tq), lambda b, i: (b, 0, i)),
            pl.BlockSpec((None, seq, aw), whole),
            pl.BlockSpec((None, seq, LANES), whole),
            pl.BlockSpec((None, nck, aw, tk), lambda b, i: (b, 0, 0, 0)),
            pl.BlockSpec(og.shape, lambda b, i: (0, 0)),
        ],
        out_specs=pl.BlockSpec((None, tq, aw), blk_q),
        scratch_shapes=[
            pltpu.VMEM((nck, tk, tq), F32),
            pltpu.VMEM((2, ATTN_HEADS, tk, tq), F32),
            pltpu.VMEM((ATTN_HEADS, tq, LANES), BF16),
            pltpu.VMEM((IDX_HEADS, tq, LANES), BF16),
            pltpu.VMEM((aw, tq), F32),
            pltpu.VMEM((ATTN_HEADS, tq), F32),
            pltpu.VMEM((ATTN_HEADS, tq), F32),
        ],
        compiler_params=pltpu.CompilerParams(dimension_semantics=("arbitrary", "arbitrary"),
                                             vmem_limit_bytes=VMEM_LIMIT_BYTES),
        name="dsa_attention",
    )(rel_bias, bounds, q, qi, wit, k, ki, vt, og)


def _post_kernel(an_ref, cn_ref, x_ref, mod_ref, n2_ref, woa_ref, woc_ref, wr_ref, br_ref,
                 x1_ref, h2_ref, comb_ref):
    mix = (jnp.dot(an_ref[...], woa_ref[...], preferred_element_type=F32)
           + jnp.dot(cn_ref[...], woc_ref[...], preferred_element_type=F32))
    x1 = x_ref[...] + mod_ref[2:3, :] * mix
    x1_ref[...] = x1
    ms = jnp.mean(x1 * x1, axis=-1, keepdims=True)
    h2 = x1 * lax.rsqrt(ms + EPS) * n2_ref[...] * (1.0 + mod_ref[4:5, :]) + mod_ref[3:4, :]
    h2b = h2.astype(BF16)
    h2_ref[...] = h2b

    logits = jnp.dot(h2b, wr_ref[...], preferred_element_type=F32) + br_ref[...]
    lane = lax.broadcasted_iota(jnp.int32, logits.shape, 1)
    lane_f = lane.astype(F32)
    far = float(LANES)
    is_g = (lane >= N_EXPERTS) & (lane < N_EXPERTS + N_GROUPS)
    gl = jnp.where(is_g, logits, -jnp.inf)
    gmax = jnp.max(gl, axis=-1, keepdims=True)
    g_sel = jnp.min(jnp.where(is_g & (gl == gmax), lane_f, far), axis=-1, keepdims=True) - float(N_EXPERTS)
    p_g = 1.0 / jnp.sum(jnp.exp(gl - gmax), axis=-1, keepdims=True)

    in_grp = (lane < N_EXPERTS) & ((lane // EXPERTS_PER_GROUP).astype(F32) == g_sel)
    e1 = jnp.where(in_grp, logits, -jnp.inf)
    l1 = jnp.max(e1, axis=-1, keepdims=True)
    i1 = jnp.min(jnp.where(in_grp & (e1 == l1), lane_f, far), axis=-1, keepdims=True)
    rest = in_grp & (lane_f != i1)
    e2 = jnp.where(rest, logits, -jnp.inf)
    l2 = jnp.max(e2, axis=-1, keepdims=True)
    i2 = jnp.min(jnp.where(rest & (e2 == l2), lane_f, far), axis=-1, keepdims=True)
    r = jnp.exp(l2 - l1)
    w1 = 1.0 / (1.0 + r)
    w2 = r / (1.0 + r)
    comb_ref[...] = jnp.where(lane_f == i1, p_g * w1, 0.0) + jnp.where(lane_f == i2, p_g * w2, 0.0)


def _post_call(an, cn, x, mod, n2, woa, woc, wr, br):
    bsz, seq, d = x.shape
    tm = POST_TM
    tok = lambda b, j: (b, j, 0)
    const = lambda b, j: (0, 0)
    return pl.pallas_call(
        _post_kernel,
        out_shape=(jax.ShapeDtypeStruct((bsz, seq, d), F32),
                   jax.ShapeDtypeStruct((bsz, seq, d), BF16),
                   jax.ShapeDtypeStruct((bsz, seq, LANES), F32)),
        grid=(bsz, seq // tm),
        in_specs=[
            pl.BlockSpec((None, tm, ATTN_WIDTH), tok),
            pl.BlockSpec((None, tm, CONV_WIDTH), tok),
            pl.BlockSpec((None, tm, d), tok),
            pl.BlockSpec((None, 6, d), lambda b, j: (b, 0, 0)),
            pl.BlockSpec(n2.shape, const),
            pl.BlockSpec(woa.shape, const),
            pl.BlockSpec(woc.shape, const),
            pl.BlockSpec(wr.shape, const),
            pl.BlockSpec(br.shape, const),
        ],
        out_specs=(pl.BlockSpec((None, tm, d), tok),
                   pl.BlockSpec((None, tm, d), tok),
                   pl.BlockSpec((None, tm, LANES), tok)),
        compiler_params=pltpu.CompilerParams(dimension_semantics=("arbitrary", "arbitrary"),
                                             vmem_limit_bytes=VMEM_LIMIT_BYTES),
        name="post_router",
    )(an, cn, x, mod, n2, woa, woc, wr, br)


def _moe_kernel(h2_ref, comb_ref, x1_ref, mod_ref, wgu_ref, wd_ref, o_ref, acc_ref):
    e = pl.program_id(2)
    ab = jnp.dot(h2_ref[...], wgu_ref[...], preferred_element_type=F32)
    a = ab[:, :EXPERT_FF]
    up = ab[:, EXPERT_FF:]
    comb = comb_ref[...]
    lane = lax.broadcasted_iota(jnp.int32, comb.shape, 1)
    cw = jnp.sum(jnp.where(lane == e, comb, 0.0), axis=-1, keepdims=True)
    hid = (a * jax.nn.sigmoid(a)) * up * cw
    y = jnp.dot(hid.astype(BF16), wd_ref[...], preferred_element_type=F32)

    @pl.when(e == 0)
    def _():
        acc_ref[...] = y

    @pl.when(e > 0)
    def _():
        acc_ref[...] += y

    @pl.when(e == N_EXPERTS - 1)
    def _():
        o_ref[...] = x1_ref[...] + mod_ref[5:6, :] * acc_ref[...]


def _moe_call(h2, comb, x1, mod, wgu, wd):
    bsz, seq, d = x1.shape
    tm = MOE_TM
    tok = lambda b, j, e: (b, j, 0)
    return pl.pallas_call(
        _moe_kernel,
        out_shape=jax.ShapeDtypeStruct((bsz, seq, d), F32),
        grid=(bsz, seq // tm, N_EXPERTS),
        in_specs=[
            pl.BlockSpec((None, tm, d), tok),
            pl.BlockSpec((None, tm, LANES), tok),
            pl.BlockSpec((None, tm, d), tok),
            pl.BlockSpec((None, 6, d), lambda b, j, e: (b, 0, 0)),
            pl.BlockSpec((None, d, 2 * EXPERT_FF), lambda b, j, e: (e, 0, 0)),
            pl.BlockSpec((None, EXPERT_FF, d), lambda b, j, e: (e, 0, 0)),
        ],
        out_specs=pl.BlockSpec((None, tm, d), tok),
        scratch_shapes=[pltpu.VMEM((tm, d), F32)],
        compiler_params=pltpu.CompilerParams(dimension_semantics=("arbitrary", "arbitrary", "arbitrary"),
                                             vmem_limit_bytes=VMEM_LIMIT_BYTES),
        name="moe_experts",
    )(h2, comb, x1, mod, wgu, wd)


def _layer(x, mod, rel_bias, norm1, w_in, q_norm, k_norm, conv_w, attn_out_norm, conv_out_norm, w_out,
           norm2, w_group_router, b_group_router, w_expert_router, b_expert_router, w_gate, w_up, w_down):
    bsz, seq, d = x.shape
    aw = ATTN_WIDTH
    topk = min(TOPK_MAX, seq // 4)

    offs = np.cumsum([0, aw, aw, aw, IDX_HEADS * IDX_DIM, IDX_DIM, IDX_HEADS, CONV_WIDTH, CONV_WIDTH, CONV_WIDTH])
    col = lambda n: w_in[:, int(offs[n]):int(offs[n + 1])]
    wm = jnp.concatenate([col(0), col(1), col(3), col(6), col(7), col(8)], axis=1).astype(BF16)
    wvt = col(2).T.astype(BF16)
    wki = jnp.concatenate([col(4), col(4)], axis=1).astype(BF16)
    wwit = col(5).T.astype(BF16)
    qg = (jnp.tile(q_norm, ATTN_HEADS) * ((HEAD_DIM ** -0.5) * LOG2E))[None, :]
    kg = jnp.tile(k_norm, ATTN_HEADS)[None, :]
    grp = np.arange(aw) // CONV_GROUP_DIM
    gmat = jnp.asarray((grp[:, None] == grp[None, :]).astype(np.float32) / CONV_GROUP_DIM, dtype=BF16)

    q, k, vt, qi, ki, wit, cn = _pre_call(
        x, mod, norm1[None, :], wm, wvt, wki, wwit, qg, kg, conv_w, conv_out_norm.reshape(1, -1), gmat)

    bounds = jnp.asarray(_bucket_boundaries())
    an = _attn_call(rel_bias, bounds, q, qi, wit, k, ki, vt, attn_out_norm.reshape(1, -1), topk)

    wr = jnp.concatenate([w_expert_router, w_group_router,
                          jnp.zeros((d, LANES - N_EXPERTS - N_GROUPS), F32)], axis=1).astype(BF16)
    br = jnp.concatenate([b_expert_router, b_group_router,
                          jnp.zeros((LANES - N_EXPERTS - N_GROUPS,), F32)])[None, :]
    x1, h2, comb = _post_call(an, cn, x, mod, norm2[None, :], w_out[:aw].astype(BF16), w_out[aw:].astype(BF16),
                              wr, br)

    wgu = jnp.concatenate([w_gate, w_up], axis=-1).astype(BF16)
    return _moe_call(h2, comb, x1, mod, wgu, w_down.astype(BF16))


def kernel(x, c, rel_bias, w_ada, b_ada, norm1, w_in, q_norm, k_norm, conv_w, attn_out_norm, conv_out_norm,
           w_out, norm2, w_group_router, b_group_router, w_expert_router, b_expert_router, w_gate, w_up,
           w_down):
    bsz, seq, d = x.shape
    assert d == D_MODEL and seq % max(PRE_TM, POST_TM, MOE_TM) == 0 and ATT_TQ == ATT_TK
    depth = w_ada.shape[0]
    for l in range(depth):
        mod = _mod_call(c, w_ada[l], b_ada[l][None, :]).reshape(bsz, 6, d)
        x = _layer(x, mod, rel_bias, norm1[l], w_in[l], q_norm[l], k_norm[l], conv_w[l], attn_out_norm[l],
                   conv_out_norm[l], w_out[l], norm2[l], w_group_router[l], b_group_router[l],
                   w_expert_router[l], b_expert_router[l], w_gate[l], w_up[l], w_down[l])
    return x
```

```python
import functools
import math

import jax
import jax.numpy as jnp
import numpy as np
from jax import lax
from jax.experimental import pallas as pl
from jax.experimental.pallas import tpu as pltpu

F32 = jnp.float32
BF16 = jnp.bfloat16

D_MODEL = 1024
HEAD_DIM = 64
ATTN_HEADS = 8
ATTN_WIDTH = ATTN_HEADS * HEAD_DIM
CONV_WIDTH = D_MODEL - ATTN_WIDTH
CONV_GROUP_DIM = 64
CONV_K = 3
IDX_HEADS = 8
IDX_DIM = 64
TOPK_MAX = 256
IDX_SCALE = (IDX_DIM ** -0.5) * (IDX_HEADS ** -0.5)
N_BUCKETS = 32
MAX_DISTANCE = 128
N_GROUPS = 4
EXPERTS_PER_GROUP = 8
N_EXPERTS = N_GROUPS * EXPERTS_PER_GROUP
EXPERT_FF = 256
EPS = 1e-6
LOG2E = 1.4426950408889634
NEG_BIG = -1e30
BISECT_VALUE_STEPS = 8
BISECT_MAX_STEPS = 64

LANES = 128
SUBLANES = 8
BF16_SUBLANES = 16
V_SLAB = HEAD_DIM + BF16_SUBLANES
VMEM_LIMIT_BYTES = 56 * 1024 * 1024

PRE_TM = 512
ATT_TQ = 256
ATT_TK = 256
POST_TM = 512
MOE_TM = 1024
MOD_TN = 1536

_NT_DIMS = (((1,), (1,)), ((), ()))


def _tree_sum(parts):
    while len(parts) > 1:
        nxt = [parts[j] + parts[j + 1] for j in range(0, len(parts) - 1, 2)]
        if len(parts) % 2:
            nxt.append(parts[-1])
        parts = nxt
    return parts[0]


def _bucket_boundaries():
    max_exact = N_BUCKETS // 2
    d = np.arange(0, 4 * MAX_DISTANCE, dtype=np.int64)
    nf = np.maximum(d, 1).astype(np.float32)
    large = max_exact + (np.log(nf / np.float32(max_exact)) / np.float32(math.log(MAX_DISTANCE / max_exact))
                         * np.float32(N_BUCKETS - max_exact)).astype(np.int32)
    large = np.minimum(large, N_BUCKETS - 1)
    bucket = np.where(d < max_exact, d, large)
    assert np.all(np.diff(bucket) >= 0) and bucket[-1] == N_BUCKETS - 1
    bounds = [int(np.argmax(bucket >= j)) for j in range(1, N_BUCKETS)]
    return np.asarray([0] + bounds, dtype=np.int32)


def _mod_kernel(c_ref, w_ref, b_ref, o_ref):
    c = c_ref[...]
    act = c * jax.nn.sigmoid(c)
    o_ref[...] = jnp.dot(act, w_ref[...], preferred_element_type=F32,
                         precision=lax.Precision.HIGHEST) + b_ref[...]


def _mod_call(c, w_ada, b_ada):
    bsz, d = c.shape
    n = w_ada.shape[1]
    return pl.pallas_call(
        _mod_kernel,
        out_shape=jax.ShapeDtypeStruct((bsz, n), F32),
        grid=(n // MOD_TN,),
        in_specs=[pl.BlockSpec((bsz, d), lambda j: (0, 0)),
                  pl.BlockSpec((d, MOD_TN), lambda j: (0, j)),
                  pl.BlockSpec((1, MOD_TN), lambda j: (0, j))],
        out_specs=pl.BlockSpec((bsz, MOD_TN), lambda j: (0, j)),
        compiler_params=pltpu.CompilerParams(dimension_semantics=("arbitrary",),
                                             vmem_limit_bytes=VMEM_LIMIT_BYTES),
        name="adaln_mod",
    )(c, w_ada, b_ada)


def _group_rms(y, g_ref):
    ms = jnp.dot((y * y).astype(BF16), g_ref[...], preferred_element_type=F32)
    return y * lax.rsqrt(ms + EPS)


def _pre_kernel(x_ref, mod_ref, n1_ref, wm_ref, wvt_ref, wki_ref, wwit_ref, qg_ref, kg_ref,
                cw_ref, cg_ref, g_ref,
                q_ref, k_ref, vt_ref, qi_ref, ki_ref, wit_ref, cn_ref, carry_ref):
    j = pl.program_id(1)
    tm = x_ref.shape[0]
    aw = ATTN_WIDTH

    x = x_ref[...]
    ms = jnp.mean(x * x, axis=-1, keepdims=True)
    y = x * lax.rsqrt(ms + EPS) * n1_ref[...]
    h = y * (1.0 + mod_ref[1:2, :]) + mod_ref[0:1, :]
    hb = h.astype(BF16)

    def proj(lo):
        return jnp.dot(hb, wm_ref[:, lo:lo + aw], preferred_element_type=F32)

    q = _group_rms(proj(0), g_ref) * qg_ref[...]
    q_ref[...] = q.astype(BF16)
    k = _group_rms(proj(aw), g_ref) * kg_ref[...]
    k_ref[...] = k.astype(BF16)

    vt = lax.dot_general(wvt_ref[...], hb, _NT_DIMS, preferred_element_type=F32).astype(BF16)
    ones = jnp.ones((BF16_SUBLANES, ATT_TK), BF16)
    for cc in range(tm // ATT_TK):
        for hh in range(ATTN_HEADS):
            vt_ref[cc, hh * V_SLAB:hh * V_SLAB + HEAD_DIM, :] = (
                vt[hh * HEAD_DIM:(hh + 1) * HEAD_DIM, cc * ATT_TK:(cc + 1) * ATT_TK])
            vt_ref[cc, hh * V_SLAB + HEAD_DIM:(hh + 1) * V_SLAB, :] = ones

    qi_ref[...] = proj(2 * aw).astype(BF16)
    ki_ref[...] = jnp.dot(hb, wki_ref[...], preferred_element_type=F32).astype(BF16)
    wit_ref[...] = lax.dot_general(wwit_ref[...], hb, _NT_DIMS, preferred_element_type=F32) * IDX_SCALE

    gate_b = proj(3 * aw)
    z = proj(4 * aw) * proj(5 * aw)

    @pl.when(j == 0)
    def _():
        carry_ref[...] = jnp.zeros_like(carry_ref)

    prev = carry_ref[...]
    row = lax.broadcasted_iota(jnp.int32, z.shape, 0)
    z1 = jnp.where(row == 0, prev[SUBLANES - 1:SUBLANES, :], pltpu.roll(z, 1, 0))
    z2 = pltpu.roll(z, 2, 0)
    z2 = jnp.where(row == 0, prev[SUBLANES - 2:SUBLANES - 1, :], z2)
    z2 = jnp.where(row == 1, prev[SUBLANES - 1:SUBLANES, :], z2)
    carry_ref[...] = z[tm - SUBLANES:, :]
    conv = cw_ref[2:3, :] * z + cw_ref[1:2, :] * z1 + cw_ref[0:1, :] * z2
    yc = gate_b * conv
    cn_ref[...] = (_group_rms(yc, g_ref) * cg_ref[...]).astype(BF16)


def _pre_call(x, mod, n1, wm, wvt, wki, wwit, qg, kg, cw, cg, gmat):
    bsz, seq, d = x.shape
    tm = PRE_TM
    nck = tm // ATT_TK
    aw = ATTN_WIDTH
    const = lambda b, j: (0, 0)
    tok = lambda b, j: (b, j, 0)
    out_shape = (
        jax.ShapeDtypeStruct((bsz, seq, aw), BF16),
        jax.ShapeDtypeStruct((bsz, seq, aw), BF16),
        jax.ShapeDtypeStruct((bsz, seq // ATT_TK, ATTN_HEADS * V_SLAB, ATT_TK), BF16),
        jax.ShapeDtypeStruct((bsz, seq, aw), BF16),
        jax.ShapeDtypeStruct((bsz, seq, LANES), BF16),
        jax.ShapeDtypeStruct((bsz, IDX_HEADS, seq), F32),
        jax.ShapeDtypeStruct((bsz, seq, CONV_WIDTH), BF16),
    )
    out_specs = (
        pl.BlockSpec((None, tm, aw), tok),
        pl.BlockSpec((None, tm, aw), tok),
        pl.BlockSpec((None, nck, ATTN_HEADS * V_SLAB, ATT_TK), lambda b, j: (b, j, 0, 0)),
        pl.BlockSpec((None, tm, aw), tok),
        pl.BlockSpec((None, tm, LANES), tok),
        pl.BlockSpec((None, IDX_HEADS, tm), lambda b, j: (b, 0, j)),
        pl.BlockSpec((None, tm, CONV_WIDTH), tok),
    )
    in_specs = [
        pl.BlockSpec((None, tm, d), tok),
        pl.BlockSpec((None, 6, d), lambda b, j: (b, 0, 0)),
        pl.BlockSpec(n1.shape, const),
        pl.BlockSpec(wm.shape, const),
        pl.BlockSpec(wvt.shape, const),
        pl.BlockSpec(wki.shape, const),
        pl.BlockSpec(wwit.shape, const),
        pl.BlockSpec(qg.shape, const),
        pl.BlockSpec(kg.shape, const),
        pl.BlockSpec(cw.shape, const),
        pl.BlockSpec(cg.shape, const),
        pl.BlockSpec(gmat.shape, const),
    ]
    return pl.pallas_call(
        _pre_kernel,
        out_shape=out_shape,
        grid=(bsz, seq // tm),
        in_specs=in_specs,
        out_specs=out_specs,
        scratch_shapes=[pltpu.VMEM((SUBLANES, CONV_WIDTH), F32)],
        compiler_params=pltpu.CompilerParams(dimension_semantics=("arbitrary", "arbitrary"),
                                             vmem_limit_bytes=VMEM_LIMIT_BYTES),
        name="pre_proj",
    )(x, mod, n1, wm, wvt, wki, wwit, qg, kg, cw, cg, gmat)


def _attn_kernel(rb_ref, bnd_ref, q_ref, qi_ref, wit_ref, k_ref, ki_ref, vt_ref, og_ref,
                 o_ref,
                 s_ref, bias_ref, qpad_ref, qipad_ref, lg_ref, acc_ref, out_ref, *, topk):
    b = pl.program_id(0)
    i = pl.program_id(1)
    tq, tk = ATT_TQ, ATT_TK
    nh, hd = ATTN_HEADS, HEAD_DIM

    t_loc = lax.broadcasted_iota(jnp.int32, (tk, tq), 1)
    s_loc = lax.broadcasted_iota(jnp.int32, (tk, tq), 0)

    @pl.when((b == 0) & (i == 0))
    def _():
        for idx in range(2):
            dist = t_loc - s_loc + idx * tq
            for h in range(nh):
                bias_ref[idx, h] = jnp.full((tk, tq), (rb_ref[0, h] - rb_ref[N_BUCKETS - 1, h]) * LOG2E, F32)

            def fill(jb, carry):
                reached = dist >= bnd_ref[jb]
                for h in range(nh):
                    val = (rb_ref[jb, h] - rb_ref[N_BUCKETS - 1, h]) * LOG2E
                    bias_ref[idx, h] = jnp.where(reached, val, bias_ref[idx, h])
                return carry

            lax.fori_loop(1, N_BUCKETS, fill, 0)

    lane = lax.broadcasted_iota(jnp.int32, (tq, LANES), 1)
    for h in range(nh):
        pair = slice((h // 2) * LANES, (h // 2 + 1) * LANES)
        keep = (lane // hd) == (h % 2)
        qpad_ref[h] = jnp.where(keep, q_ref[:, pair], jnp.zeros((), BF16))
        qipad_ref[h] = jnp.where(keep, qi_ref[:, pair], jnp.zeros((), BF16))

    def score_chunk(c):
        kic = ki_ref[pl.ds(pl.multiple_of(c * tk, tk), tk), :]
        acc = jnp.zeros((tk, tq), F32)
        for h in range(nh):
            e = lax.dot_general(kic, qipad_ref[h], _NT_DIMS, preferred_element_type=F32)
            acc = acc + wit_ref[h:h + 1, :] * jnp.maximum(e, 0.0)
        return acc

    def a_body(c, carry):
        rmin, rmax = carry
        sc = score_chunk(c)
        s_ref[c] = sc
        return (jnp.minimum(rmin, jnp.min(sc, axis=0, keepdims=True)),
                jnp.maximum(rmax, jnp.max(sc, axis=0, keepdims=True)))

    rmin0 = jnp.full((1, tq), jnp.inf, F32)
    rmax0 = jnp.full((1, tq), -jnp.inf, F32)
    rmin, rmax = lax.fori_loop(0, i, a_body, (rmin0, rmax0))
    sc = score_chunk(i)
    causal = s_loc <= t_loc
    s_ref[i] = jnp.where(causal, sc, -jnp.inf)
    rmin = jnp.minimum(rmin, jnp.min(jnp.where(causal, sc, jnp.inf), axis=0, keepdims=True))
    rmax = jnp.maximum(rmax, jnp.max(jnp.where(causal, sc, -jnp.inf), axis=0, keepdims=True))

    def count_ge(thr):
        def body(c, acc):
            hit = jnp.where(s_ref[c] >= thr, 1.0, 0.0)
            return acc + _tree_sum([hit[r * SUBLANES:(r + 1) * SUBLANES] for r in range(tk // SUBLANES)])
        acc = lax.fori_loop(0, i + 1, body, jnp.zeros((SUBLANES, tq), F32))
        return jnp.sum(acc, axis=0, keepdims=True)

    def order_key(v):
        bits = pltpu.bitcast(v, jnp.int32)
        return jnp.where(bits < 0, bits ^ jnp.int32(0x7FFFFFFF), bits)

    def from_order_key(key):
        return pltpu.bitcast(jnp.where(key < 0, key ^ jnp.int32(0x7FFFFFFF), key), F32)

    t_glob = (i * tq + lax.broadcasted_iota(jnp.int32, (1, tq), 1)).astype(F32)
    n_causal = t_glob + 1.0
    kf = jnp.minimum(float(topk), n_causal)
    c_max = count_ge(rmax)
    all_sel = n_causal <= kf
    max_ge = c_max >= kf
    active0 = jnp.where(all_sel | max_ge, 0.0, 1.0)
    thr0 = jnp.where(all_sel, rmin, rmax)
    tie0 = jnp.where(jnp.logical_not(all_sel) & (c_max > kf), 1.0, 0.0)
    hif0 = jnp.full((1, tq), jnp.inf, F32)
    need0 = kf

    def b_cond(st):
        return (jnp.max(st[0]) > 0.0) & (st[8] <= BISECT_MAX_STEPS)

    def b_body(st):
        active, lo, hi, fhi, thr, tie, hif, need, step = st
        lo_key = order_key(lo)
        hi_key = order_key(hi)
        mid_key = (lo_key >> 1) + (hi_key >> 1) + (lo_key & hi_key & 1)
        mid_val = lo + (hi - lo) * 0.5
        use_val = (step < BISECT_VALUE_STEPS) & (mid_val > lo) & (mid_val < hi)
        mid = jnp.where(use_val, mid_val, from_order_key(mid_key))
        collapsed = (mid_key == lo_key) | (step >= BISECT_MAX_STEPS)
        cm = count_ge(mid)
        act = active > 0.0
        live = act & jnp.logical_not(collapsed)
        found = live & (cm == kf)
        go_up = live & (cm > kf)
        go_dn = live & (cm < kf)
        ends_tie = act & collapsed
        thr = jnp.where(found, mid, jnp.where(ends_tie, lo, thr))
        tie = jnp.where(ends_tie, 1.0, tie)
        hif = jnp.where(ends_tie, hi, hif)
        need = jnp.where(ends_tie, kf - fhi, need)
        lo = jnp.where(go_up, mid, lo)
        fhi = jnp.where(go_dn, cm, fhi)
        hi = jnp.where(go_dn, mid, hi)
        active = jnp.where(found | ends_tie, 0.0, active)
        return active, lo, hi, fhi, thr, tie, hif, need, step + 1

    _, _, _, _, thr, tie, hif, need, _ = lax.while_loop(
        b_cond, b_body, (active0, rmin, rmax, c_max, thr0, tie0, hif0, need0, jnp.int32(0)))

    @pl.when(jnp.max(tie) > 0.0)
    def _():
        tri = jnp.where(lax.broadcasted_iota(jnp.int32, (tk, tk), 1)
                        <= lax.broadcasted_iota(jnp.int32, (tk, tk), 0), 1.0, 0.0).astype(BF16)

        def body(c, seen):
            sc_c = s_ref[c]
            tied = (sc_c >= thr) & (sc_c < hif) & (tie > 0.0)
            rank = jnp.dot(tri, jnp.where(tied, 1.0, 0.0).astype(BF16), preferred_element_type=F32) + seen
            s_ref[c] = jnp.where(tied & (rank > need), -jnp.inf, sc_c)
            return rank[tk - 1:tk, :]

        lax.fori_loop(0, i + 1, body, jnp.zeros((1, tq), F32))

    acc_ref[...] = jnp.zeros(acc_ref.shape, F32)

    def store_logits(c, slot, bias_idx):
        masked = jnp.where(s_ref[c] >= thr, 0.0, NEG_BIG)
        row0 = pl.multiple_of(c * tk, tk)
        for h in range(nh):
            kc = k_ref[pl.ds(row0, tk), (h // 2) * LANES:(h // 2 + 1) * LANES]
            lt = lax.dot_general(kc, qpad_ref[h], _NT_DIMS, preferred_element_type=F32) + masked
            if bias_idx is not None:
                lt = lt + bias_ref[bias_idx, h]
            lg_ref[slot, h] = lt

    def softmax_pv(c, slot, m_all):
        m_out = []
        for h in range(nh):
            m_old = m_all[h]
            m_new = jnp.maximum(m_old, jnp.max(lg_ref[slot, h], axis=0, keepdims=True))
            p = jnp.exp2(lg_ref[slot, h] - m_new).astype(BF16)
            alpha = jnp.exp2(m_old - m_new)
            pv = jnp.dot(vt_ref[c, h * V_SLAB:(h + 1) * V_SLAB, :], p, preferred_element_type=F32)
            acc_ref[h] = alpha * acc_ref[h] + pv
            m_out.append(m_new)
        return tuple(m_out)

    def near_step(m_all):
        store_logits(i - 1, 1, 1)
        return softmax_pv(i, 0, m_all)

    def far_step(j, parity, m_all):
        c = i - 2 - j
        store_logits(c, parity, None)
        return softmax_pv(c + 1, 1 - parity, m_all)

    def far_pair(jj, m_all):
        return far_step(2 * jj + 1, 1, far_step(2 * jj, 0, m_all))

    n_far = jnp.maximum(i - 1, 0)
    m_all = tuple(jnp.full((1, tq), NEG_BIG, F32) for _ in range(nh))
    store_logits(i, 0, 0)
    m_all = lax.cond(i >= 1, near_step, lambda m: m, m_all)
    m_all = lax.fori_loop(0, n_far // 2, far_pair, m_all)
    m_all = lax.cond((n_far & 1) == 1, lambda m: far_step(n_far - 1, 0, m), lambda m: m, m_all)
    lax.cond((i & 1) == 0, lambda m: softmax_pv(0, 0, m), lambda m: softmax_pv(0, 1, m), m_all)

    for h in range(nh):
        o = acc_ref[h, :hd, :] / acc_ref[h, hd:hd + 1, :]
        ms = jnp.mean(o * o, axis=0, keepdims=True)
        out_ref[h * hd:(h + 1) * hd, :] = o * lax.rsqrt(ms + EPS)
    o_ref[...] = (out_ref[...].T * og_ref[...]).astype(BF16)


def _attn_call(rel_bias, bounds, q, qi, wit, k, ki, vt, og, topk):
    bsz, seq, aw = q.shape
    tq, tk = ATT_TQ, ATT_TK
    nck = seq // tk
    blk_q = lambda b, i: (b, i, 0)
    whole = lambda b, i: (b, 0, 0)
    smem = pl.BlockSpec(memory_space=pltpu.SMEM)
    return pl.pallas_call(
        functools.partial(_attn_kernel, topk=topk),
        out_shape=jax.ShapeDtypeStruct((bsz, seq, aw), BF16),
        grid=(bsz, seq // tq),
        in_specs=[
            smem, smem,
            pl.BlockSpec((None, tq, aw), blk_q),
            pl.BlockSpec((None, tq, aw), blk_q),
            pl.BlockSpec((None, IDX_HEADS, tq), lambda b, i: (b, 0, i)),
            pl.BlockSpec((None, seq, aw), whole),
            pl.BlockSpec((None, seq, LANES), whole),
            pl.BlockSpec((None, nck, ATTN_HEADS * V_SLAB, tk), lambda b, i: (b, 0, 0, 0)),
            pl.BlockSpec(og.shape, lambda b, i: (0, 0)),
        ],
        out_specs=pl.BlockSpec((None, tq, aw), blk_q),
        scratch_shapes=[
            pltpu.VMEM((nck, tk, tq), F32),
            pltpu.VMEM((2, ATTN_HEADS, tk, tq), F32),
            pltpu.VMEM((ATTN_HEADS, tq, LANES), BF16),
            pltpu.VMEM((IDX_HEADS, tq, LANES), BF16),
            pltpu.VMEM((2, ATTN_HEADS, tk, tq), F32),
            pltpu.VMEM((ATTN_HEADS, V_SLAB, tq), F32),
            pltpu.VMEM((aw, tq), F32),
        ],
        compiler_params=pltpu.CompilerParams(dimension_semantics=("arbitrary", "arbitrary"),
                                             vmem_limit_bytes=VMEM_LIMIT_BYTES),
        name="dsa_attention",
    )(rel_bias, bounds, q, qi, wit, k, ki, vt, og)


def _post_kernel(an_ref, cn_ref, x_ref, mod_ref, n2_ref, woa_ref, woc_ref, wr_ref, br_ref,
                 x1_ref, h2_ref, comb_ref):
    mix = (jnp.dot(an_ref[...], woa_ref[...], preferred_element_type=F32)
           + jnp.dot(cn_ref[...], woc_ref[...], preferred_element_type=F32))
    x1 = x_ref[...] + mod_ref[2:3, :] * mix
    x1_ref[...] = x1
    ms = jnp.mean(x1 * x1, axis=-1, keepdims=True)
    h2 = x1 * lax.rsqrt(ms + EPS) * n2_ref[...] * (1.0 + mod_ref[4:5, :]) + mod_ref[3:4, :]
    h2b = h2.astype(BF16)
    h2_ref[...] = h2b

    logits = jnp.dot(h2b, wr_ref[...], preferred_element_type=F32) + br_ref[...]
    lane = lax.broadcasted_iota(jnp.int32, logits.shape, 1)
    lane_f = lane.astype(F32)
    far = float(LANES)
    is_g = (lane >= N_EXPERTS) & (lane < N_EXPERTS + N_GROUPS)
    gl = jnp.where(is_g, logits, -jnp.inf)
    gmax = jnp.max(gl, axis=-1, keepdims=True)
    g_sel = jnp.min(jnp.where(is_g & (gl == gmax), lane_f, far), axis=-1, keepdims=True) - float(N_EXPERTS)
    p_g = 1.0 / jnp.sum(jnp.exp(gl - gmax), axis=-1, keepdims=True)

    in_grp = (lane < N_EXPERTS) & ((lane // EXPERTS_PER_GROUP).astype(F32) == g_sel)
    e1 = jnp.where(in_grp, logits, -jnp.inf)
    l1 = jnp.max(e1, axis=-1, keepdims=True)
    i1 = jnp.min(jnp.where(in_grp & (e1 == l1), lane_f, far), axis=-1, keepdims=True)
    rest = in_grp & (lane_f != i1)
    e2 = jnp.where(rest, logits, -jnp.inf)
    l2 = jnp.max(e2, axis=-1, keepdims=True)
    i2 = jnp.min(jnp.where(rest & (e2 == l2), lane_f, far), axis=-1, keepdims=True)
    r = jnp.exp(l2 - l1)
    w1 = 1.0 / (1.0 + r)
    w2 = r / (1.0 + r)
    comb_ref[...] = jnp.where(lane_f == i1, p_g * w1, 0.0) + jnp.where(lane_f == i2, p_g * w2, 0.0)


def _post_call(an, cn, x, mod, n2, woa, woc, wr, br):
    bsz, seq, d = x.shape
    tm = POST_TM
    tok = lambda b, j: (b, j, 0)
    const = lambda b, j: (0, 0)
    return pl.pallas_call(
        _post_kernel,
        out_shape=(jax.ShapeDtypeStruct((bsz, seq, d), F32),
                   jax.ShapeDtypeStruct((bsz, seq, d), BF16),
                   jax.ShapeDtypeStruct((bsz, seq, LANES), F32)),
        grid=(bsz, seq // tm),
        in_specs=[
            pl.BlockSpec((None, tm, ATTN_WIDTH), tok),
            pl.BlockSpec((None, tm, CONV_WIDTH), tok),
            pl.BlockSpec((None, tm, d), tok),
            pl.BlockSpec((None, 6, d), lambda b, j: (b, 0, 0)),
            pl.BlockSpec(n2.shape, const),
            pl.BlockSpec(woa.shape, const),
            pl.BlockSpec(woc.shape, const),
            pl.BlockSpec(wr.shape, const),
            pl.BlockSpec(br.shape, const),
        ],
        out_specs=(pl.BlockSpec((None, tm, d), tok),
                   pl.BlockSpec((None, tm, d), tok),
                   pl.BlockSpec((None, tm, LANES), tok)),
        compiler_params=pltpu.CompilerParams(dimension_semantics=("arbitrary", "arbitrary"),
                                             vmem_limit_bytes=VMEM_LIMIT_BYTES),
        name="post_router",
    )(an, cn, x, mod, n2, woa, woc, wr, br)


def _moe_kernel(h2_ref, comb_ref, x1_ref, mod_ref, wgu_ref, wd_ref, o_ref, acc_ref):
    e = pl.program_id(2)
    ab = jnp.dot(h2_ref[...], wgu_ref[...], preferred_element_type=F32)
    a = ab[:, :EXPERT_FF]
    up = ab[:, EXPERT_FF:]
    comb = comb_ref[...]
    lane = lax.broadcasted_iota(jnp.int32, comb.shape, 1)
    cw = jnp.sum(jnp.where(lane == e, comb, 0.0), axis=-1, keepdims=True)
    hid = (a * jax.nn.sigmoid(a)) * up * cw
    y = jnp.dot(hid.astype(BF16), wd_ref[...], preferred_element_type=F32)

    @pl.when(e == 0)
    def _():
        acc_ref[...] = y

    @pl.when(e > 0)
    def _():
        acc_ref[...] += y

    @pl.when(e == N_EXPERTS - 1)
    def _():
        o_ref[...] = x1_ref[...] + mod_ref[5:6, :] * acc_ref[...]


def _moe_call(h2, comb, x1, mod, wgu, wd):
    bsz, seq, d = x1.shape
    tm = MOE_TM
    tok = lambda b, j, e: (b, j, 0)
    return pl.pallas_call(
        _moe_kernel,
        out_shape=jax.ShapeDtypeStruct((bsz, seq, d), F32),
        grid=(bsz, seq // tm, N_EXPERTS),
        in_specs=[
            pl.BlockSpec((None, tm, d), tok),
            pl.BlockSpec((None, tm, LANES), tok),
            pl.BlockSpec((None, tm, d), tok),
            pl.BlockSpec((None, 6, d), lambda b, j, e: (b, 0, 0)),
            pl.BlockSpec((None, d, 2 * EXPERT_FF), lambda b, j, e: (e, 0, 0)),
            pl.BlockSpec((None, EXPERT_FF, d), lambda b, j, e: (e, 0, 0)),
        ],
        out_specs=pl.BlockSpec((None, tm, d), tok),
        scratch_shapes=[pltpu.VMEM((tm, d), F32)],
        compiler_params=pltpu.CompilerParams(dimension_semantics=("arbitrary", "arbitrary", "arbitrary"),
                                             vmem_limit_bytes=VMEM_LIMIT_BYTES),
        name="moe_experts",
    )(h2, comb, x1, mod, wgu, wd)


def _layer(x, mod, rel_bias, norm1, w_in, q_norm, k_norm, conv_w, attn_out_norm, conv_out_norm, w_out,
           norm2, w_group_router, b_group_router, w_expert_router, b_expert_router, w_gate, w_up, w_down):
    bsz, seq, d = x.shape
    aw = ATTN_WIDTH
    topk = min(TOPK_MAX, seq // 4)

    offs = np.cumsum([0, aw, aw, aw, IDX_HEADS * IDX_DIM, IDX_DIM, IDX_HEADS, CONV_WIDTH, CONV_WIDTH, CONV_WIDTH])
    col = lambda n: w_in[:, int(offs[n]):int(offs[n + 1])]
    wm = jnp.concatenate([col(0), col(1), col(3), col(6), col(7), col(8)], axis=1).astype(BF16)
    wvt = col(2).T.astype(BF16)
    wki = jnp.concatenate([col(4), col(4)], axis=1).astype(BF16)
    wwit = col(5).T.astype(BF16)
    qg = (jnp.tile(q_norm, ATTN_HEADS) * ((HEAD_DIM ** -0.5) * LOG2E))[None, :]
    kg = jnp.tile(k_norm, ATTN_HEADS)[None, :]
    grp = np.arange(aw) // CONV_GROUP_DIM
    gmat = jnp.asarray((grp[:, None] == grp[None, :]).astype(np.float32) / CONV_GROUP_DIM, dtype=BF16)

    q, k, vt, qi, ki, wit, cn = _pre_call(
        x, mod, norm1[None, :], wm, wvt, wki, wwit, qg, kg, conv_w, conv_out_norm.reshape(1, -1), gmat)

    bounds = jnp.asarray(_bucket_boundaries())
    an = _attn_call(rel_bias, bounds, q, qi, wit, k, ki, vt, attn_out_norm.reshape(1, -1), topk)

    wr = jnp.concatenate([w_expert_router, w_group_router,
                          jnp.zeros((d, LANES - N_EXPERTS - N_GROUPS), F32)], axis=1).astype(BF16)
    br = jnp.concatenate([b_expert_router, b_group_router,
                          jnp.zeros((LANES - N_EXPERTS - N_GROUPS,), F32)])[None, :]
    x1, h2, comb = _post_call(an, cn, x, mod, norm2[None, :], w_out[:aw].astype(BF16), w_out[aw:].astype(BF16),
                              wr, br)

    wgu = jnp.concatenate([w_gate, w_up], axis=-1).astype(BF16)
    return _moe_call(h2, comb, x1, mod, wgu, w_down.astype(BF16))


def kernel(x, c, rel_bias, w_ada, b_ada, norm1, w_in, q_norm, k_norm, conv_w, attn_out_norm, conv_out_norm,
           w_out, norm2, w_group_router, b_group_router, w_expert_router, b_expert_router, w_gate, w_up,
           w_down):
    bsz, seq, d = x.shape
    assert d == D_MODEL and seq % max(PRE_TM, POST_TM, MOE_TM) == 0 and ATT_TQ == ATT_TK
    depth = w_ada.shape[0]
    for l in range(depth):
        mod = _mod_call(c, w_ada[l], b_ada[l][None, :]).reshape(bsz, 6, d)
        x = _layer(x, mod, rel_bias, norm1[l], w_in[l], q_norm[l], k_norm[l], conv_w[l], attn_out_norm[l],
                   conv_out_norm[l], w_out[l], norm2[l], w_group_router[l], b_group_router[l],
                   w_expert_router[l], b_expert_router[l], w_gate[l], w_up[l], w_down[l])
    return x
```

```python
import functools
import math

import jax
import jax.numpy as jnp
import numpy as np
from jax import lax
from jax.experimental import pallas as pl
from jax.experimental.pallas import tpu as pltpu

F32 = jnp.float32
BF16 = jnp.bfloat16

D_MODEL = 1024
HEAD_DIM = 64
ATTN_HEADS = 8
ATTN_WIDTH = ATTN_HEADS * HEAD_DIM
CONV_WIDTH = D_MODEL - ATTN_WIDTH
CONV_GROUP_DIM = 64
CONV_K = 3
IDX_HEADS = 8
IDX_DIM = 64
TOPK_MAX = 256
IDX_SCALE = (IDX_DIM ** -0.5) * (IDX_HEADS ** -0.5)
N_BUCKETS = 32
MAX_DISTANCE = 128
N_GROUPS = 4
EXPERTS_PER_GROUP = 8
N_EXPERTS = N_GROUPS * EXPERTS_PER_GROUP
EXPERT_FF = 256
EPS = 1e-6
LOG2E = 1.4426950408889634
NEG_BIG = -1e30
BISECT_VALUE_STEPS = 8
BISECT_MAX_STEPS = 64

LANES = 128
SUBLANES = 8
BF16_SUBLANES = 16
V_SLAB = HEAD_DIM + BF16_SUBLANES
VMEM_LIMIT_BYTES = 56 * 1024 * 1024

PRE_TM = 512
ATT_TQ = 256
ATT_TK = 256
POST_TM = 512
MOE_TM = 1024
MOE_TILE = 64
MOE_CHUNK = 512
MOD_TN = 1536

_NT_DIMS = (((1,), (1,)), ((), ()))


def _tree_sum(parts):
    while len(parts) > 1:
        nxt = [parts[j] + parts[j + 1] for j in range(0, len(parts) - 1, 2)]
        if len(parts) % 2:
            nxt.append(parts[-1])
        parts = nxt
    return parts[0]


def _bucket_boundaries():
    max_exact = N_BUCKETS // 2
    d = np.arange(0, 4 * MAX_DISTANCE, dtype=np.int64)
    nf = np.maximum(d, 1).astype(np.float32)
    large = max_exact + (np.log(nf / np.float32(max_exact)) / np.float32(math.log(MAX_DISTANCE / max_exact))
                         * np.float32(N_BUCKETS - max_exact)).astype(np.int32)
    large = np.minimum(large, N_BUCKETS - 1)
    bucket = np.where(d < max_exact, d, large)
    assert np.all(np.diff(bucket) >= 0) and bucket[-1] == N_BUCKETS - 1
    bounds = [int(np.argmax(bucket >= j)) for j in range(1, N_BUCKETS)]
    return np.asarray([0] + bounds, dtype=np.int32)


def _mod_kernel(c_ref, w_ref, b_ref, o_ref):
    c = c_ref[...]
    act = c * jax.nn.sigmoid(c)
    o_ref[...] = jnp.dot(act, w_ref[...], preferred_element_type=F32,
                         precision=lax.Precision.HIGHEST) + b_ref[...]


def _mod_call(c, w_ada, b_ada):
    bsz, d = c.shape
    n = w_ada.shape[1]
    return pl.pallas_call(
        _mod_kernel,
        out_shape=jax.ShapeDtypeStruct((bsz, n), F32),
        grid=(n // MOD_TN,),
        in_specs=[pl.BlockSpec((bsz, d), lambda j: (0, 0)),
                  pl.BlockSpec((d, MOD_TN), lambda j: (0, j)),
                  pl.BlockSpec((1, MOD_TN), lambda j: (0, j))],
        out_specs=pl.BlockSpec((bsz, MOD_TN), lambda j: (0, j)),
        compiler_params=pltpu.CompilerParams(dimension_semantics=("arbitrary",),
                                             vmem_limit_bytes=VMEM_LIMIT_BYTES),
        name="adaln_mod",
    )(c, w_ada, b_ada)


def _group_rms(y, g_ref):
    ms = jnp.dot((y * y).astype(BF16), g_ref[...], preferred_element_type=F32)
    return y * lax.rsqrt(ms + EPS)


def _pre_kernel(x_ref, mod_ref, n1_ref, wm_ref, wvt_ref, wki_ref, wwit_ref, qg_ref, kg_ref,
                cw_ref, cg_ref, g_ref,
                q_ref, k_ref, vt_ref, qi_ref, ki_ref, wit_ref, cn_ref, carry_ref):
    j = pl.program_id(1)
    tm = x_ref.shape[0]
    aw = ATTN_WIDTH

    x = x_ref[...]
    ms = jnp.mean(x * x, axis=-1, keepdims=True)
    y = x * lax.rsqrt(ms + EPS) * n1_ref[...]
    h = y * (1.0 + mod_ref[1:2, :]) + mod_ref[0:1, :]
    hb = h.astype(BF16)

    def proj(lo):
        return jnp.dot(hb, wm_ref[:, lo:lo + aw], preferred_element_type=F32)

    q = _group_rms(proj(0), g_ref) * qg_ref[...]
    q_ref[...] = q.astype(BF16)
    k = _group_rms(proj(aw), g_ref) * kg_ref[...]
    k_ref[...] = k.astype(BF16)

    vt = lax.dot_general(wvt_ref[...], hb, _NT_DIMS, preferred_element_type=F32).astype(BF16)
    ones = jnp.ones((BF16_SUBLANES, ATT_TK), BF16)
    for cc in range(tm // ATT_TK):
        for hh in range(ATTN_HEADS):
            vt_ref[cc, hh * V_SLAB:hh * V_SLAB + HEAD_DIM, :] = (
                vt[hh * HEAD_DIM:(hh + 1) * HEAD_DIM, cc * ATT_TK:(cc + 1) * ATT_TK])
            vt_ref[cc, hh * V_SLAB + HEAD_DIM:(hh + 1) * V_SLAB, :] = ones

    qi_ref[...] = proj(2 * aw).astype(BF16)
    ki_ref[...] = jnp.dot(hb, wki_ref[...], preferred_element_type=F32).astype(BF16)
    wit_ref[...] = lax.dot_general(wwit_ref[...], hb, _NT_DIMS, preferred_element_type=F32) * IDX_SCALE

    gate_b = proj(3 * aw)
    z = proj(4 * aw) * proj(5 * aw)

    @pl.when(j == 0)
    def _():
        carry_ref[...] = jnp.zeros_like(carry_ref)

    prev = carry_ref[...]
    row = lax.broadcasted_iota(jnp.int32, z.shape, 0)
    z1 = jnp.where(row == 0, prev[SUBLANES - 1:SUBLANES, :], pltpu.roll(z, 1, 0))
    z2 = pltpu.roll(z, 2, 0)
    z2 = jnp.where(row == 0, prev[SUBLANES - 2:SUBLANES - 1, :], z2)
    z2 = jnp.where(row == 1, prev[SUBLANES - 1:SUBLANES, :], z2)
    carry_ref[...] = z[tm - SUBLANES:, :]
    conv = cw_ref[2:3, :] * z + cw_ref[1:2, :] * z1 + cw_ref[0:1, :] * z2
    yc = gate_b * conv
    cn_ref[...] = (_group_rms(yc, g_ref) * cg_ref[...]).astype(BF16)


def _pre_call(x, mod, n1, wm, wvt, wki, wwit, qg, kg, cw, cg, gmat):
    bsz, seq, d = x.shape
    tm = PRE_TM
    nck = tm // ATT_TK
    aw = ATTN_WIDTH
    const = lambda b, j: (0, 0)
    tok = lambda b, j: (b, j, 0)
    out_shape = (
        jax.ShapeDtypeStruct((bsz, seq, aw), BF16),
        jax.ShapeDtypeStruct((bsz, seq, aw), BF16),
        jax.ShapeDtypeStruct((bsz, seq // ATT_TK, ATTN_HEADS * V_SLAB, ATT_TK), BF16),
        jax.ShapeDtypeStruct((bsz, seq, aw), BF16),
        jax.ShapeDtypeStruct((bsz, seq, LANES), BF16),
        jax.ShapeDtypeStruct((bsz, IDX_HEADS, seq), F32),
        jax.ShapeDtypeStruct((bsz, seq, CONV_WIDTH), BF16),
    )
    out_specs = (
        pl.BlockSpec((None, tm, aw), tok),
        pl.BlockSpec((None, tm, aw), tok),
        pl.BlockSpec((None, nck, ATTN_HEADS * V_SLAB, ATT_TK), lambda b, j: (b, j, 0, 0)),
        pl.BlockSpec((None, tm, aw), tok),
        pl.BlockSpec((None, tm, LANES), tok),
        pl.BlockSpec((None, IDX_HEADS, tm), lambda b, j: (b, 0, j)),
        pl.BlockSpec((None, tm, CONV_WIDTH), tok),
    )
    in_specs = [
        pl.BlockSpec((None, tm, d), tok),
        pl.BlockSpec((None, 6, d), lambda b, j: (b, 0, 0)),
        pl.BlockSpec(n1.shape, const),
        pl.BlockSpec(wm.shape, const),
        pl.BlockSpec(wvt.shape, const),
        pl.BlockSpec(wki.shape, const),
        pl.BlockSpec(wwit.shape, const),
        pl.BlockSpec(qg.shape, const),
        pl.BlockSpec(kg.shape, const),
        pl.BlockSpec(cw.shape, const),
        pl.BlockSpec(cg.shape, const),
        pl.BlockSpec(gmat.shape, const),
    ]
    return pl.pallas_call(
        _pre_kernel,
        out_shape=out_shape,
        grid=(bsz, seq // tm),
        in_specs=in_specs,
        out_specs=out_specs,
        scratch_shapes=[pltpu.VMEM((SUBLANES, CONV_WIDTH), F32)],
        compiler_params=pltpu.CompilerParams(dimension_semantics=("arbitrary", "arbitrary"),
                                             vmem_limit_bytes=VMEM_LIMIT_BYTES),
        name="pre_proj",
    )(x, mod, n1, wm, wvt, wki, wwit, qg, kg, cw, cg, gmat)


def _attn_kernel(rb_ref, bnd_ref, q_ref, qi_ref, wit_ref, k_ref, ki_ref, vt_ref, og_ref,
                 o_ref,
                 s_ref, bias_ref, qpad_ref, qipad_ref, lg_ref, acc_ref, out_ref, *, topk):
    b = pl.program_id(0)
    i = pl.program_id(1)
    tq, tk = ATT_TQ, ATT_TK
    nh, hd = ATTN_HEADS, HEAD_DIM

    t_loc = lax.broadcasted_iota(jnp.int32, (tk, tq), 1)
    s_loc = lax.broadcasted_iota(jnp.int32, (tk, tq), 0)

    @pl.when((b == 0) & (i == 0))
    def _():
        for idx in range(2):
            dist = t_loc - s_loc + idx * tq
            for h in range(nh):
                bias_ref[idx, h] = jnp.full((tk, tq), (rb_ref[0, h] - rb_ref[N_BUCKETS - 1, h]) * LOG2E, F32)

            def fill(jb, carry):
                reached = dist >= bnd_ref[jb]
                for h in range(nh):
                    val = (rb_ref[jb, h] - rb_ref[N_BUCKETS - 1, h]) * LOG2E
                    bias_ref[idx, h] = jnp.where(reached, val, bias_ref[idx, h])
                return carry

            lax.fori_loop(1, N_BUCKETS, fill, 0)

    lane = lax.broadcasted_iota(jnp.int32, (tq, LANES), 1)
    for h in range(nh):
        pair = slice((h // 2) * LANES, (h // 2 + 1) * LANES)
        keep = (lane // hd) == (h % 2)
        qpad_ref[h] = jnp.where(keep, q_ref[:, pair], jnp.zeros((), BF16))
        qipad_ref[h] = jnp.where(keep, qi_ref[:, pair], jnp.zeros((), BF16))

    def score_chunk(c):
        kic = ki_ref[pl.ds(pl.multiple_of(c * tk, tk), tk), :]
        acc = jnp.zeros((tk, tq), F32)
        for h in range(nh):
            e = lax.dot_general(kic, qipad_ref[h], _NT_DIMS, preferred_element_type=F32)
            acc = acc + wit_ref[h:h + 1, :] * jnp.maximum(e, 0.0)
        return acc

    def a_body(c, carry):
        rmin, rmax = carry
        sc = score_chunk(c)
        s_ref[c] = sc
        return (jnp.minimum(rmin, jnp.min(sc, axis=0, keepdims=True)),
                jnp.maximum(rmax, jnp.max(sc, axis=0, keepdims=True)))

    rmin0 = jnp.full((1, tq), jnp.inf, F32)
    rmax0 = jnp.full((1, tq), -jnp.inf, F32)
    rmin, rmax = lax.fori_loop(0, i, a_body, (rmin0, rmax0))
    sc = score_chunk(i)
    causal = s_loc <= t_loc
    s_ref[i] = jnp.where(causal, sc, -jnp.inf)
    rmin = jnp.minimum(rmin, jnp.min(jnp.where(causal, sc, jnp.inf), axis=0, keepdims=True))
    rmax = jnp.maximum(rmax, jnp.max(jnp.where(causal, sc, -jnp.inf), axis=0, keepdims=True))

    def count_ge(thr):
        def body(c, acc):
            hit = jnp.where(s_ref[c] >= thr, 1.0, 0.0)
            return acc + _tree_sum([hit[r * SUBLANES:(r + 1) * SUBLANES] for r in range(tk // SUBLANES)])
        acc = lax.fori_loop(0, i + 1, body, jnp.zeros((SUBLANES, tq), F32))
        return jnp.sum(acc, axis=0, keepdims=True)

    def order_key(v):
        bits = pltpu.bitcast(v, jnp.int32)
        return jnp.where(bits < 0, bits ^ jnp.int32(0x7FFFFFFF), bits)

    def from_order_key(key):
        return pltpu.bitcast(jnp.where(key < 0, key ^ jnp.int32(0x7FFFFFFF), key), F32)

    t_glob = (i * tq + lax.broadcasted_iota(jnp.int32, (1, tq), 1)).astype(F32)
    n_causal = t_glob + 1.0
    kf = jnp.minimum(float(topk), n_causal)
    c_max = count_ge(rmax)
    all_sel = n_causal <= kf
    max_ge = c_max >= kf
    active0 = jnp.where(all_sel | max_ge, 0.0, 1.0)
    thr0 = jnp.where(all_sel, rmin, rmax)
    tie0 = jnp.where(jnp.logical_not(all_sel) & (c_max > kf), 1.0, 0.0)
    hif0 = jnp.full((1, tq), jnp.inf, F32)
    need0 = kf

    def b_cond(st):
        return (jnp.max(st[0]) > 0.0) & (st[8] <= BISECT_MAX_STEPS)

    def b_body(st):
        active, lo, hi, fhi, thr, tie, hif, need, step = st
        lo_key = order_key(lo)
        hi_key = order_key(hi)
        mid_key = (lo_key >> 1) + (hi_key >> 1) + (lo_key & hi_key & 1)
        mid_val = lo + (hi - lo) * 0.5
        use_val = (step < BISECT_VALUE_STEPS) & (mid_val > lo) & (mid_val < hi)
        mid = jnp.where(use_val, mid_val, from_order_key(mid_key))
        collapsed = (mid_key == lo_key) | (step >= BISECT_MAX_STEPS)
        cm = count_ge(mid)
        act = active > 0.0
        live = act & jnp.logical_not(collapsed)
        found = live & (cm == kf)
        go_up = live & (cm > kf)
        go_dn = live & (cm < kf)
        ends_tie = act & collapsed
        thr = jnp.where(found, mid, jnp.where(ends_tie, lo, thr))
        tie = jnp.where(ends_tie, 1.0, tie)
        hif = jnp.where(ends_tie, hi, hif)
        need = jnp.where(ends_tie, kf - fhi, need)
        lo = jnp.where(go_up, mid, lo)
        fhi = jnp.where(go_dn, cm, fhi)
        hi = jnp.where(go_dn, mid, hi)
        active = jnp.where(found | ends_tie, 0.0, active)
        return active, lo, hi, fhi, thr, tie, hif, need, step + 1

    _, _, _, _, thr, tie, hif, need, _ = lax.while_loop(
        b_cond, b_body, (active0, rmin, rmax, c_max, thr0, tie0, hif0, need0, jnp.int32(0)))

    @pl.when(jnp.max(tie) > 0.0)
    def _():
        tri = jnp.where(lax.broadcasted_iota(jnp.int32, (tk, tk), 1)
                        <= lax.broadcasted_iota(jnp.int32, (tk, tk), 0), 1.0, 0.0).astype(BF16)

        def body(c, seen):
            sc_c = s_ref[c]
            tied = (sc_c >= thr) & (sc_c < hif) & (tie > 0.0)
            rank = jnp.dot(tri, jnp.where(tied, 1.0, 0.0).astype(BF16), preferred_element_type=F32) + seen
            s_ref[c] = jnp.where(tied & (rank > need), -jnp.inf, sc_c)
            return rank[tk - 1:tk, :]

        lax.fori_loop(0, i + 1, body, jnp.zeros((1, tq), F32))

    acc_ref[...] = jnp.zeros(acc_ref.shape, F32)

    def store_logits(c, slot, bias_idx):
        masked = jnp.where(s_ref[c] >= thr, 0.0, NEG_BIG)
        row0 = pl.multiple_of(c * tk, tk)
        for h in range(nh):
            kc = k_ref[pl.ds(row0, tk), (h // 2) * LANES:(h // 2 + 1) * LANES]
            lt = lax.dot_general(kc, qpad_ref[h], _NT_DIMS, preferred_element_type=F32) + masked
            if bias_idx is not None:
                lt = lt + bias_ref[bias_idx, h]
            lg_ref[slot, h] = lt

    def softmax_pv(c, slot, m_all):
        m_out = []
        for h in range(nh):
            m_old = m_all[h]
            m_new = jnp.maximum(m_old, jnp.max(lg_ref[slot, h], axis=0, keepdims=True))
            p = jnp.exp2(lg_ref[slot, h] - m_new).astype(BF16)
            alpha = jnp.exp2(m_old - m_new)
            pv = jnp.dot(vt_ref[c, h * V_SLAB:(h + 1) * V_SLAB, :], p, preferred_element_type=F32)
            acc_ref[h] = alpha * acc_ref[h] + pv
            m_out.append(m_new)
        return tuple(m_out)

    def near_step(m_all):
        store_logits(i - 1, 1, 1)
        return softmax_pv(i, 0, m_all)

    def far_step(j, parity, m_all):
        c = i - 2 - j
        store_logits(c, parity, None)
        return softmax_pv(c + 1, 1 - parity, m_all)

    def far_pair(jj, m_all):
        return far_step(2 * jj + 1, 1, far_step(2 * jj, 0, m_all))

    n_far = jnp.maximum(i - 1, 0)
    m_all = tuple(jnp.full((1, tq), NEG_BIG, F32) for _ in range(nh))
    store_logits(i, 0, 0)
    m_all = lax.cond(i >= 1, near_step, lambda m: m, m_all)
    m_all = lax.fori_loop(0, n_far // 2, far_pair, m_all)
    m_all = lax.cond((n_far & 1) == 1, lambda m: far_step(n_far - 1, 0, m), lambda m: m, m_all)
    lax.cond((i & 1) == 0, lambda m: softmax_pv(0, 0, m), lambda m: softmax_pv(0, 1, m), m_all)

    for h in range(nh):
        o = acc_ref[h, :hd, :] / acc_ref[h, hd:hd + 1, :]
        ms = jnp.mean(o * o, axis=0, keepdims=True)
        out_ref[h * hd:(h + 1) * hd, :] = o * lax.rsqrt(ms + EPS)
    o_ref[...] = (out_ref[...].T * og_ref[...]).astype(BF16)


def _attn_call(rel_bias, bounds, q, qi, wit, k, ki, vt, og, topk):
    bsz, seq, aw = q.shape
    tq, tk = ATT_TQ, ATT_TK
    nck = seq // tk
    blk_q = lambda b, i: (b, i, 0)
    whole = lambda b, i: (b, 0, 0)
    smem = pl.BlockSpec(memory_space=pltpu.SMEM)
    return pl.pallas_call(
        functools.partial(_attn_kernel, topk=topk),
        out_shape=jax.ShapeDtypeStruct((bsz, seq, aw), BF16),
        grid=(bsz, seq // tq),
        in_specs=[
            smem, smem,
            pl.BlockSpec((None, tq, aw), blk_q),
            pl.BlockSpec((None, tq, aw), blk_q),
            pl.BlockSpec((None, IDX_HEADS, tq), lambda b, i: (b, 0, i)),
            pl.BlockSpec((None, seq, aw), whole),
            pl.BlockSpec((None, seq, LANES), whole),
            pl.BlockSpec((None, nck, ATTN_HEADS * V_SLAB, tk), lambda b, i: (b, 0, 0, 0)),
            pl.BlockSpec(og.shape, lambda b, i: (0, 0)),
        ],
        out_specs=pl.BlockSpec((None, tq, aw), blk_q),
        scratch_shapes=[
            pltpu.VMEM((nck, tk, tq), F32),
            pltpu.VMEM((2, ATTN_HEADS, tk, tq), F32),
            pltpu.VMEM((ATTN_HEADS, tq, LANES), BF16),
            pltpu.VMEM((IDX_HEADS, tq, LANES), BF16),
            pltpu.VMEM((2, ATTN_HEADS, tk, tq), F32),
            pltpu.VMEM((ATTN_HEADS, V_SLAB, tq), F32),
            pltpu.VMEM((aw, tq), F32),
        ],
        compiler_params=pltpu.CompilerParams(dimension_semantics=("arbitrary", "arbitrary"),
                                             vmem_limit_bytes=VMEM_LIMIT_BYTES),
        name="dsa_attention",
    )(rel_bias, bounds, q, qi, wit, k, ki, vt, og)


def _post_kernel(an_ref, cn_ref, x_ref, mod_ref, n2_ref, woa_ref, woc_ref, wr_ref, br_ref,
                 x1_ref, h2_ref, comb_ref):
    mix = (jnp.dot(an_ref[...], woa_ref[...], preferred_element_type=F32)
           + jnp.dot(cn_ref[...], woc_ref[...], preferred_element_type=F32))
    x1 = x_ref[...] + mod_ref[2:3, :] * mix
    x1_ref[...] = x1
    ms = jnp.mean(x1 * x1, axis=-1, keepdims=True)
    h2 = x1 * lax.rsqrt(ms + EPS) * n2_ref[...] * (1.0 + mod_ref[4:5, :]) + mod_ref[3:4, :]
    h2b = h2.astype(BF16)
    h2_ref[...] = h2b

    logits = jnp.dot(h2b, wr_ref[...], preferred_element_type=F32) + br_ref[...]
    lane = lax.broadcasted_iota(jnp.int32, logits.shape, 1)
    lane_f = lane.astype(F32)
    far = float(LANES)
    is_g = (lane >= N_EXPERTS) & (lane < N_EXPERTS + N_GROUPS)
    gl = jnp.where(is_g, logits, -jnp.inf)
    gmax = jnp.max(gl, axis=-1, keepdims=True)
    g_sel = jnp.min(jnp.where(is_g & (gl == gmax), lane_f, far), axis=-1, keepdims=True) - float(N_EXPERTS)
    p_g = 1.0 / jnp.sum(jnp.exp(gl - gmax), axis=-1, keepdims=True)

    in_grp = (lane < N_EXPERTS) & ((lane // EXPERTS_PER_GROUP).astype(F32) == g_sel)
    e1 = jnp.where(in_grp, logits, -jnp.inf)
    l1 = jnp.max(e1, axis=-1, keepdims=True)
    i1 = jnp.min(jnp.where(in_grp & (e1 == l1), lane_f, far), axis=-1, keepdims=True)
    rest = in_grp & (lane_f != i1)
    e2 = jnp.where(rest, logits, -jnp.inf)
    l2 = jnp.max(e2, axis=-1, keepdims=True)
    i2 = jnp.min(jnp.where(rest & (e2 == l2), lane_f, far), axis=-1, keepdims=True)
    r = jnp.exp(l2 - l1)
    w1 = 1.0 / (1.0 + r)
    w2 = r / (1.0 + r)
    comb_ref[...] = jnp.where(lane_f == i1, p_g * w1, 0.0) + jnp.where(lane_f == i2, p_g * w2, 0.0)


def _post_call(an, cn, x, mod, n2, woa, woc, wr, br):
    bsz, seq, d = x.shape
    tm = POST_TM
    tok = lambda b, j: (b, j, 0)
    const = lambda b, j: (0, 0)
    return pl.pallas_call(
        _post_kernel,
        out_shape=(jax.ShapeDtypeStruct((bsz, seq, d), F32),
                   jax.ShapeDtypeStruct((bsz, seq, d), BF16),
                   jax.ShapeDtypeStruct((bsz, seq, LANES), F32)),
        grid=(bsz, seq // tm),
        in_specs=[
            pl.BlockSpec((None, tm, ATTN_WIDTH), tok),
            pl.BlockSpec((None, tm, CONV_WIDTH), tok),
            pl.BlockSpec((None, tm, d), tok),
            pl.BlockSpec((None, 6, d), lambda b, j: (b, 0, 0)),
            pl.BlockSpec(n2.shape, const),
            pl.BlockSpec(woa.shape, const),
            pl.BlockSpec(woc.shape, const),
            pl.BlockSpec(wr.shape, const),
            pl.BlockSpec(br.shape, const),
        ],
        out_specs=(pl.BlockSpec((None, tm, d), tok),
                   pl.BlockSpec((None, tm, d), tok),
                   pl.BlockSpec((None, tm, LANES), tok)),
        compiler_params=pltpu.CompilerParams(dimension_semantics=("arbitrary", "arbitrary"),
                                             vmem_limit_bytes=VMEM_LIMIT_BYTES),
        name="post_router",
    )(an, cn, x, mod, n2, woa, woc, wr, br)


def _lane_scalar_i32(row, idx):
    lane = lax.broadcasted_iota(jnp.int32, row.shape, 1)
    picked = jnp.sum(jnp.where(lane == idx, row, 0.0), axis=1, keepdims=True)
    return picked.astype(jnp.int32)[0, 0]


def _moe_kernel(h2_ref, comb_ref, x1_ref, mod_ref, ltri_ref, utri_ref, wgu_ref, wd_ref,
                o_ref,
                xg_ref, y_ref, col_ref, row_ref, off_ref):
    e = pl.program_id(1)
    nb = h2_ref.shape[0]
    tile, chunk = MOE_TILE, MOE_CHUNK
    lane = lax.broadcasted_iota(jnp.int32, (nb, LANES), 1)

    @pl.when(e == 0)
    def _():
        comb = comb_ref[...]
        assigned = comb != 0.0
        a_f = jnp.where(assigned, 1.0, 0.0)
        rank = jnp.dot(ltri_ref[...], a_f.astype(BF16), preferred_element_type=F32)
        cnt = rank[nb - 1:nb, :] + a_f[nb - 1:nb, :]
        ntile = jnp.floor((cnt + float(tile - 1)) * (1.0 / tile))
        first = jnp.dot(jnp.broadcast_to(ntile, (SUBLANES, LANES)).astype(BF16), utri_ref[...],
                        preferred_element_type=F32)[0:1, :]
        off_ref[0:1, :] = first
        off_ref[1:2, :] = ntile
        pos = first * float(tile) + rank
        pos1 = jnp.min(jnp.where(assigned, pos, 1e9), axis=1, keepdims=True)
        pos2 = jnp.max(jnp.where(assigned, pos, -1.0), axis=1, keepdims=True)
        pos2 = jnp.where(pos2 == pos1, -1.0, pos2)
        cw1 = jnp.sum(jnp.where(assigned & (pos == pos1), comb, 0.0), axis=1, keepdims=True)
        cw2 = jnp.sum(jnp.where(assigned & (pos == pos2), comb, 0.0), axis=1, keepdims=True)
        info = jnp.where(lane == 0, pos1, jnp.where(lane == 1, pos2, jnp.where(lane == 2, cw1,
                         jnp.where(lane == 3, cw2, 0.0))))
        col_ref[...] = info
        row_ref[...] = info.T

        n_chunks = (_lane_scalar_i32(first, N_EXPERTS) * tile + (chunk - 1)) // chunk
        p1 = row_ref[0:1, :].astype(jnp.int32)
        p2 = row_ref[1:2, :].astype(jnp.int32)
        sub = lax.broadcasted_iota(jnp.int32, (chunk, nb), 0)

        def gather(c, carry):
            p = sub + c * chunk
            sel = jnp.where((p == p1) | (p == p2), 1.0, 0.0).astype(BF16)
            r0 = pl.multiple_of(c * chunk, chunk)
            xg_ref[pl.ds(r0, chunk), :] = jnp.dot(sel, h2_ref[...], preferred_element_type=F32).astype(BF16)
            return carry

        lax.fori_loop(0, n_chunks, gather, 0)

    first_e = _lane_scalar_i32(off_ref[0:1, :], e)
    ntile_e = _lane_scalar_i32(off_ref[1:2, :], e)
    total = _lane_scalar_i32(off_ref[0:1, :], N_EXPERTS)
    padded_total = ((total * tile + (chunk - 1)) // chunk) * (chunk // tile)
    ntile_e = jnp.where(e == N_EXPERTS - 1, padded_total - first_e, ntile_e)

    def expert_tile(j, carry):
        r0 = pl.multiple_of((first_e + j) * tile, tile)
        ab = jnp.dot(xg_ref[pl.ds(r0, tile), :], wgu_ref[...], preferred_element_type=F32)
        a = ab[:, :EXPERT_FF]
        hid = (a * jax.nn.sigmoid(a)) * ab[:, EXPERT_FF:]
        y_ref[pl.ds(r0, tile), :] = jnp.dot(hid.astype(BF16), wd_ref[...],
                                            preferred_element_type=F32).astype(BF16)
        return carry

    lax.fori_loop(0, ntile_e, expert_tile, 0)

    @pl.when(e == N_EXPERTS - 1)
    def _():
        p1 = col_ref[:, 0:1].astype(jnp.int32)
        p2 = col_ref[:, 1:2].astype(jnp.int32)
        cw1 = col_ref[:, 2:3]
        cw2 = col_ref[:, 3:4]
        gate = mod_ref[5:6, :]
        lane_c = lax.broadcasted_iota(jnp.int32, (nb, chunk), 1)
        o_ref[...] = x1_ref[...]

        def scatter(c, carry):
            p = lane_c + c * chunk
            w = (jnp.where(p == p1, cw1, 0.0) + jnp.where(p == p2, cw2, 0.0)).astype(BF16)
            r0 = pl.multiple_of(c * chunk, chunk)
            o_ref[...] += gate * jnp.dot(w, y_ref[pl.ds(r0, chunk), :], preferred_element_type=F32)
            return carry

        lax.fori_loop(0, padded_total // (chunk // tile), scatter, 0)


def _moe_call(h2, comb, x1, mod, wgu, wd):
    bsz, seq, d = x1.shape
    nb = MOE_TM
    n_blk = bsz * seq // nb
    rows_max = 2 * nb + N_EXPERTS * MOE_TILE
    rows_max = -(-rows_max // MOE_CHUNK) * MOE_CHUNK
    ltri = jnp.asarray(np.tril(np.ones((nb, nb), np.float32), -1), dtype=BF16)
    utri = jnp.asarray(np.triu(np.ones((LANES, LANES), np.float32), 1), dtype=BF16)
    tok = lambda j, e: (j, 0)
    const = lambda j, e: (0, 0)
    out = pl.pallas_call(
        _moe_kernel,
        out_shape=jax.ShapeDtypeStruct((bsz * seq, d), F32),
        grid=(n_blk, N_EXPERTS),
        in_specs=[
            pl.BlockSpec((nb, d), tok),
            pl.BlockSpec((nb, LANES), tok),
            pl.BlockSpec((nb, d), tok),
            pl.BlockSpec((None, 6, d), lambda j, e: ((j * nb) // seq, 0, 0)),
            pl.BlockSpec((nb, nb), const),
            pl.BlockSpec((LANES, LANES), const),
            pl.BlockSpec((None, d, 2 * EXPERT_FF), lambda j, e: (e, 0, 0)),
            pl.BlockSpec((None, EXPERT_FF, d), lambda j, e: (e, 0, 0)),
        ],
        out_specs=pl.BlockSpec((nb, d), tok),
        scratch_shapes=[
            pltpu.VMEM((rows_max, d), BF16),
            pltpu.VMEM((rows_max, d), BF16),
            pltpu.VMEM((nb, LANES), F32),
            pltpu.VMEM((LANES, nb), F32),
            pltpu.VMEM((SUBLANES, LANES), F32),
        ],
        compiler_params=pltpu.CompilerParams(dimension_semantics=("arbitrary", "arbitrary"),
                                             vmem_limit_bytes=VMEM_LIMIT_BYTES),
        name="moe_experts",
    )(h2.reshape(bsz * seq, d), comb.reshape(bsz * seq, LANES), x1.reshape(bsz * seq, d), mod, ltri, utri,
      wgu, wd)
    return out.reshape(bsz, seq, d)


def _layer(x, mod, rel_bias, norm1, w_in, q_norm, k_norm, conv_w, attn_out_norm, conv_out_norm, w_out,
           norm2, w_group_router, b_group_router, w_expert_router, b_expert_router, w_gate, w_up, w_down):
    bsz, seq, d = x.shape
    aw = ATTN_WIDTH
    topk = min(TOPK_MAX, seq // 4)

    offs = np.cumsum([0, aw, aw, aw, IDX_HEADS * IDX_DIM, IDX_DIM, IDX_HEADS, CONV_WIDTH, CONV_WIDTH, CONV_WIDTH])
    col = lambda n: w_in[:, int(offs[n]):int(offs[n + 1])]
    wm = jnp.concatenate([col(0), col(1), col(3), col(6), col(7), col(8)], axis=1).astype(BF16)
    wvt = col(2).T.astype(BF16)
    wki = jnp.concatenate([col(4), col(4)], axis=1).astype(BF16)
    wwit = col(5).T.astype(BF16)
    qg = (jnp.tile(q_norm, ATTN_HEADS) * ((HEAD_DIM ** -0.5) * LOG2E))[None, :]
    kg = jnp.tile(k_norm, ATTN_HEADS)[None, :]
    grp = np.arange(aw) // CONV_GROUP_DIM
    gmat = jnp.asarray((grp[:, None] == grp[None, :]).astype(np.float32) / CONV_GROUP_DIM, dtype=BF16)

    q, k, vt, qi, ki, wit, cn = _pre_call(
        x, mod, norm1[None, :], wm, wvt, wki, wwit, qg, kg, conv_w, conv_out_norm.reshape(1, -1), gmat)

    bounds = jnp.asarray(_bucket_boundaries())
    an = _attn_call(rel_bias, bounds, q, qi, wit, k, ki, vt, attn_out_norm.reshape(1, -1), topk)

    wr = jnp.concatenate([w_expert_router, w_group_router,
                          jnp.zeros((d, LANES - N_EXPERTS - N_GROUPS), F32)], axis=1).astype(BF16)
    br = jnp.concatenate([b_expert_router, b_group_router,
                          jnp.zeros((LANES - N_EXPERTS - N_GROUPS,), F32)])[None, :]
    x1, h2, comb = _post_call(an, cn, x, mod, norm2[None, :], w_out[:aw].astype(BF16), w_out[aw:].astype(BF16),
                              wr, br)

    wgu = jnp.concatenate([w_gate, w_up], axis=-1).astype(BF16)
    return _moe_call(h2, comb, x1, mod, wgu, w_down.astype(BF16))


def kernel(x, c, rel_bias, w_ada, b_ada, norm1, w_in, q_norm, k_norm, conv_w, attn_out_norm, conv_out_norm,
           w_out, norm2, w_group_router, b_group_router, w_expert_router, b_expert_router, w_gate, w_up,
           w_down):
    bsz, seq, d = x.shape
    assert d == D_MODEL and seq % max(PRE_TM, POST_TM, MOE_TM) == 0 and ATT_TQ == ATT_TK
    depth = w_ada.shape[0]
    for l in range(depth):
        mod = _mod_call(c, w_ada[l], b_ada[l][None, :]).reshape(bsz, 6, d)
        x = _layer(x, mod, rel_bias, norm1[l], w_in[l], q_norm[l], k_norm[l], conv_w[l], attn_out_norm[l],
                   conv_out_norm[l], w_out[l], norm2[l], w_group_router[l], b_group_router[l],
                   w_expert_router[l], b_expert_router[l], w_gate[l], w_up[l], w_down[l])
    return x
```

```python
import functools
import math

import jax
import jax.numpy as jnp
import numpy as np
from jax import lax
from jax.experimental import pallas as pl
from jax.experimental.pallas import tpu as pltpu

F32 = jnp.float32
BF16 = jnp.bfloat16

D_MODEL = 1024
HEAD_DIM = 64
ATTN_HEADS = 8
ATTN_WIDTH = ATTN_HEADS * HEAD_DIM
CONV_WIDTH = D_MODEL - ATTN_WIDTH
CONV_GROUP_DIM = 64
CONV_K = 3
IDX_HEADS = 8
IDX_DIM = 64
TOPK_MAX = 256
IDX_SCALE = (IDX_DIM ** -0.5) * (IDX_HEADS ** -0.5)
N_BUCKETS = 32
MAX_DISTANCE = 128
N_GROUPS = 4
EXPERTS_PER_GROUP = 8
N_EXPERTS = N_GROUPS * EXPERTS_PER_GROUP
EXPERT_FF = 256
EPS = 1e-6
LOG2E = 1.4426950408889634
NEG_BIG = -1e30
BISECT_VALUE_STEPS = 8
BISECT_MAX_STEPS = 64

LANES = 128
SUBLANES = 8
BF16_SUBLANES = 16
V_SLAB = HEAD_DIM + BF16_SUBLANES
VMEM_LIMIT_BYTES = 56 * 1024 * 1024

PRE_TM = 512
ATT_TQ = 256
ATT_TK = 256
POST_TM = 512
MOE_TM = 1024
MOE_TILE = 64
MOE_CHUNK = 512
MOE_EPS = 4
MOD_TN = 1536

_NT_DIMS = (((1,), (1,)), ((), ()))


def _tree_sum(parts):
    while len(parts) > 1:
        nxt = [parts[j] + parts[j + 1] for j in range(0, len(parts) - 1, 2)]
        if len(parts) % 2:
            nxt.append(parts[-1])
        parts = nxt
    return parts[0]


def _bucket_boundaries():
    max_exact = N_BUCKETS // 2
    d = np.arange(0, 4 * MAX_DISTANCE, dtype=np.int64)
    nf = np.maximum(d, 1).astype(np.float32)
    large = max_exact + (np.log(nf / np.float32(max_exact)) / np.float32(math.log(MAX_DISTANCE / max_exact))
                         * np.float32(N_BUCKETS - max_exact)).astype(np.int32)
    large = np.minimum(large, N_BUCKETS - 1)
    bucket = np.where(d < max_exact, d, large)
    assert np.all(np.diff(bucket) >= 0) and bucket[-1] == N_BUCKETS - 1
    bounds = [int(np.argmax(bucket >= j)) for j in range(1, N_BUCKETS)]
    return np.asarray([0] + bounds, dtype=np.int32)


def _mod_kernel(c_ref, w_ref, b_ref, o_ref):
    c = c_ref[...]
    act = c * jax.nn.sigmoid(c)
    o_ref[...] = jnp.dot(act, w_ref[...], preferred_element_type=F32,
                         precision=lax.Precision.HIGHEST) + b_ref[...]


def _mod_call(c, w_ada, b_ada):
    bsz, d = c.shape
    n = w_ada.shape[1]
    return pl.pallas_call(
        _mod_kernel,
        out_shape=jax.ShapeDtypeStruct((bsz, n), F32),
        grid=(n // MOD_TN,),
        in_specs=[pl.BlockSpec((bsz, d), lambda j: (0, 0)),
                  pl.BlockSpec((d, MOD_TN), lambda j: (0, j)),
                  pl.BlockSpec((1, MOD_TN), lambda j: (0, j))],
        out_specs=pl.BlockSpec((bsz, MOD_TN), lambda j: (0, j)),
        compiler_params=pltpu.CompilerParams(dimension_semantics=("arbitrary",),
                                             vmem_limit_bytes=VMEM_LIMIT_BYTES),
        name="adaln_mod",
    )(c, w_ada, b_ada)


def _group_rms(y, g_ref):
    ms = jnp.dot((y * y).astype(BF16), g_ref[...], preferred_element_type=F32)
    return y * lax.rsqrt(ms + EPS)


def _pre_kernel(x_ref, mod_ref, n1_ref, wm_ref, wvt_ref, wki_ref, wwit_ref, qg_ref, kg_ref,
                cw_ref, cg_ref, g_ref,
                q_ref, k_ref, vt_ref, qi_ref, ki_ref, wit_ref, cn_ref, carry_ref):
    j = pl.program_id(1)
    tm = x_ref.shape[0]
    aw = ATTN_WIDTH

    x = x_ref[...]
    ms = jnp.mean(x * x, axis=-1, keepdims=True)
    y = x * lax.rsqrt(ms + EPS) * n1_ref[...]
    h = y * (1.0 + mod_ref[1:2, :]) + mod_ref[0:1, :]
    hb = h.astype(BF16)

    def proj(lo):
        return jnp.dot(hb, wm_ref[:, lo:lo + aw], preferred_element_type=F32)

    q = _group_rms(proj(0), g_ref) * qg_ref[...]
    q_ref[...] = q.astype(BF16)
    k = _group_rms(proj(aw), g_ref) * kg_ref[...]
    k_ref[...] = k.astype(BF16)

    vt = lax.dot_general(wvt_ref[...], hb, _NT_DIMS, preferred_element_type=F32).astype(BF16)
    ones = jnp.ones((BF16_SUBLANES, ATT_TK), BF16)
    for cc in range(tm // ATT_TK):
        for hh in range(ATTN_HEADS):
            vt_ref[cc, hh * V_SLAB:hh * V_SLAB + HEAD_DIM, :] = (
                vt[hh * HEAD_DIM:(hh + 1) * HEAD_DIM, cc * ATT_TK:(cc + 1) * ATT_TK])
            vt_ref[cc, hh * V_SLAB + HEAD_DIM:(hh + 1) * V_SLAB, :] = ones

    qi_ref[...] = proj(2 * aw).astype(BF16)
    ki_ref[...] = jnp.dot(hb, wki_ref[...], preferred_element_type=F32).astype(BF16)
    wit_ref[...] = lax.dot_general(wwit_ref[...], hb, _NT_DIMS, preferred_element_type=F32) * IDX_SCALE

    gate_b = proj(3 * aw)
    z = proj(4 * aw) * proj(5 * aw)

    @pl.when(j == 0)
    def _():
        carry_ref[...] = jnp.zeros_like(carry_ref)

    prev = carry_ref[...]
    row = lax.broadcasted_iota(jnp.int32, z.shape, 0)
    z1 = jnp.where(row == 0, prev[SUBLANES - 1:SUBLANES, :], pltpu.roll(z, 1, 0))
    z2 = pltpu.roll(z, 2, 0)
    z2 = jnp.where(row == 0, prev[SUBLANES - 2:SUBLANES - 1, :], z2)
    z2 = jnp.where(row == 1, prev[SUBLANES - 1:SUBLANES, :], z2)
    carry_ref[...] = z[tm - SUBLANES:, :]
    conv = cw_ref[2:3, :] * z + cw_ref[1:2, :] * z1 + cw_ref[0:1, :] * z2
    yc = gate_b * conv
    cn_ref[...] = (_group_rms(yc, g_ref) * cg_ref[...]).astype(BF16)


def _pre_call(x, mod, n1, wm, wvt, wki, wwit, qg, kg, cw, cg, gmat):
    bsz, seq, d = x.shape
    tm = PRE_TM
    nck = tm // ATT_TK
    aw = ATTN_WIDTH
    const = lambda b, j: (0, 0)
    tok = lambda b, j: (b, j, 0)
    out_shape = (
        jax.ShapeDtypeStruct((bsz, seq, aw), BF16),
        jax.ShapeDtypeStruct((bsz, seq, aw), BF16),
        jax.ShapeDtypeStruct((bsz, seq // ATT_TK, ATTN_HEADS * V_SLAB, ATT_TK), BF16),
        jax.ShapeDtypeStruct((bsz, seq, aw), BF16),
        jax.ShapeDtypeStruct((bsz, seq, LANES), BF16),
        jax.ShapeDtypeStruct((bsz, IDX_HEADS, seq), F32),
        jax.ShapeDtypeStruct((bsz, seq, CONV_WIDTH), BF16),
    )
    out_specs = (
        pl.BlockSpec((None, tm, aw), tok),
        pl.BlockSpec((None, tm, aw), tok),
        pl.BlockSpec((None, nck, ATTN_HEADS * V_SLAB, ATT_TK), lambda b, j: (b, j, 0, 0)),
        pl.BlockSpec((None, tm, aw), tok),
        pl.BlockSpec((None, tm, LANES), tok),
        pl.BlockSpec((None, IDX_HEADS, tm), lambda b, j: (b, 0, j)),
        pl.BlockSpec((None, tm, CONV_WIDTH), tok),
    )
    in_specs = [
        pl.BlockSpec((None, tm, d), tok),
        pl.BlockSpec((None, 6, d), lambda b, j: (b, 0, 0)),
        pl.BlockSpec(n1.shape, const),
        pl.BlockSpec(wm.shape, const),
        pl.BlockSpec(wvt.shape, const),
        pl.BlockSpec(wki.shape, const),
        pl.BlockSpec(wwit.shape, const),
        pl.BlockSpec(qg.shape, const),
        pl.BlockSpec(kg.shape, const),
        pl.BlockSpec(cw.shape, const),
        pl.BlockSpec(cg.shape, const),
        pl.BlockSpec(gmat.shape, const),
    ]
    return pl.pallas_call(
        _pre_kernel,
        out_shape=out_shape,
        grid=(bsz, seq // tm),
        in_specs=in_specs,
        out_specs=out_specs,
        scratch_shapes=[pltpu.VMEM((SUBLANES, CONV_WIDTH), F32)],
        compiler_params=pltpu.CompilerParams(dimension_semantics=("arbitrary", "arbitrary"),
                                             vmem_limit_bytes=VMEM_LIMIT_BYTES),
        name="pre_proj",
    )(x, mod, n1, wm, wvt, wki, wwit, qg, kg, cw, cg, gmat)


def _attn_kernel(rb_ref, bnd_ref, q_ref, qi_ref, wit_ref, k_ref, ki_ref, vt_ref, og_ref,
                 o_ref,
                 s_ref, bias_ref, qpad_ref, qipad_ref, lg_ref, acc_ref, out_ref, *, topk):
    b = pl.program_id(0)
    i = pl.program_id(1)
    tq, tk = ATT_TQ, ATT_TK
    nh, hd = ATTN_HEADS, HEAD_DIM

    t_loc = lax.broadcasted_iota(jnp.int32, (tk, tq), 1)
    s_loc = lax.broadcasted_iota(jnp.int32, (tk, tq), 0)

    @pl.when((b == 0) & (i == 0))
    def _():
        for idx in range(2):
            dist = t_loc - s_loc + idx * tq
            for h in range(nh):
                bias_ref[idx, h] = jnp.full((tk, tq), (rb_ref[0, h] - rb_ref[N_BUCKETS - 1, h]) * LOG2E, F32)

            def fill(jb, carry):
                reached = dist >= bnd_ref[jb]
                for h in range(nh):
                    val = (rb_ref[jb, h] - rb_ref[N_BUCKETS - 1, h]) * LOG2E
                    bias_ref[idx, h] = jnp.where(reached, val, bias_ref[idx, h])
                return carry

            lax.fori_loop(1, N_BUCKETS, fill, 0)

    lane = lax.broadcasted_iota(jnp.int32, (tq, LANES), 1)
    for h in range(nh):
        pair = slice((h // 2) * LANES, (h // 2 + 1) * LANES)
        keep = (lane // hd) == (h % 2)
        qpad_ref[h] = jnp.where(keep, q_ref[:, pair], jnp.zeros((), BF16))
        qipad_ref[h] = jnp.where(keep, qi_ref[:, pair], jnp.zeros((), BF16))

    def score_chunk(c):
        kic = ki_ref[pl.ds(pl.multiple_of(c * tk, tk), tk), :]
        acc = jnp.zeros((tk, tq), F32)
        for h in range(nh):
            e = lax.dot_general(kic, qipad_ref[h], _NT_DIMS, preferred_element_type=F32)
            acc = acc + wit_ref[h:h + 1, :] * jnp.maximum(e, 0.0)
        return acc

    def a_body(c, carry):
        rmin, rmax = carry
        sc = score_chunk(c)
        s_ref[c] = sc
        return (jnp.minimum(rmin, jnp.min(sc, axis=0, keepdims=True)),
                jnp.maximum(rmax, jnp.max(sc, axis=0, keepdims=True)))

    rmin0 = jnp.full((1, tq), jnp.inf, F32)
    rmax0 = jnp.full((1, tq), -jnp.inf, F32)
    rmin, rmax = lax.fori_loop(0, i, a_body, (rmin0, rmax0))
    sc = score_chunk(i)
    causal = s_loc <= t_loc
    s_ref[i] = jnp.where(causal, sc, -jnp.inf)
    rmin = jnp.minimum(rmin, jnp.min(jnp.where(causal, sc, jnp.inf), axis=0, keepdims=True))
    rmax = jnp.maximum(rmax, jnp.max(jnp.where(causal, sc, -jnp.inf), axis=0, keepdims=True))

    def count_ge(thr):
        def body(c, acc):
            hit = jnp.where(s_ref[c] >= thr, 1.0, 0.0)
            return acc + _tree_sum([hit[r * SUBLANES:(r + 1) * SUBLANES] for r in range(tk // SUBLANES)])
        acc = lax.fori_loop(0, i + 1, body, jnp.zeros((SUBLANES, tq), F32))
        return jnp.sum(acc, axis=0, keepdims=True)

    def order_key(v):
        bits = pltpu.bitcast(v, jnp.int32)
        return jnp.where(bits < 0, bits ^ jnp.int32(0x7FFFFFFF), bits)

    def from_order_key(key):
        return pltpu.bitcast(jnp.where(key < 0, key ^ jnp.int32(0x7FFFFFFF), key), F32)

    t_glob = (i * tq + lax.broadcasted_iota(jnp.int32, (1, tq), 1)).astype(F32)
    n_causal = t_glob + 1.0
    kf = jnp.minimum(float(topk), n_causal)
    c_max = count_ge(rmax)
    all_sel = n_causal <= kf
    max_ge = c_max >= kf
    active0 = jnp.where(all_sel | max_ge, 0.0, 1.0)
    thr0 = jnp.where(all_sel, rmin, rmax)
    tie0 = jnp.where(jnp.logical_not(all_sel) & (c_max > kf), 1.0, 0.0)
    hif0 = jnp.full((1, tq), jnp.inf, F32)
    need0 = kf

    def b_cond(st):
        return (jnp.max(st[0]) > 0.0) & (st[8] <= BISECT_MAX_STEPS)

    def b_body(st):
        active, lo, hi, fhi, thr, tie, hif, need, step = st
        lo_key = order_key(lo)
        hi_key = order_key(hi)
        mid_key = (lo_key >> 1) + (hi_key >> 1) + (lo_key & hi_key & 1)
        mid_val = lo + (hi - lo) * 0.5
        use_val = (step < BISECT_VALUE_STEPS) & (mid_val > lo) & (mid_val < hi)
        mid = jnp.where(use_val, mid_val, from_order_key(mid_key))
        collapsed = (mid_key == lo_key) | (step >= BISECT_MAX_STEPS)
        cm = count_ge(mid)
        act = active > 0.0
        live = act & jnp.logical_not(collapsed)
        found = live & (cm == kf)
        go_up = live & (cm > kf)
        go_dn = live & (cm < kf)
        ends_tie = act & collapsed
        thr = jnp.where(found, mid, jnp.where(ends_tie, lo, thr))
        tie = jnp.where(ends_tie, 1.0, tie)
        hif = jnp.where(ends_tie, hi, hif)
        need = jnp.where(ends_tie, kf - fhi, need)
        lo = jnp.where(go_up, mid, lo)
        fhi = jnp.where(go_dn, cm, fhi)
        hi = jnp.where(go_dn, mid, hi)
        active = jnp.where(found | ends_tie, 0.0, active)
        return active, lo, hi, fhi, thr, tie, hif, need, step + 1

    _, _, _, _, thr, tie, hif, need, _ = lax.while_loop(
        b_cond, b_body, (active0, rmin, rmax, c_max, thr0, tie0, hif0, need0, jnp.int32(0)))

    @pl.when(jnp.max(tie) > 0.0)
    def _():
        tri = jnp.where(lax.broadcasted_iota(jnp.int32, (tk, tk), 1)
                        <= lax.broadcasted_iota(jnp.int32, (tk, tk), 0), 1.0, 0.0).astype(BF16)

        def body(c, seen):
            sc_c = s_ref[c]
            tied = (sc_c >= thr) & (sc_c < hif) & (tie > 0.0)
            rank = jnp.dot(tri, jnp.where(tied, 1.0, 0.0).astype(BF16), preferred_element_type=F32) + seen
            s_ref[c] = jnp.where(tied & (rank > need), -jnp.inf, sc_c)
            return rank[tk - 1:tk, :]

        lax.fori_loop(0, i + 1, body, jnp.zeros((1, tq), F32))

    acc_ref[...] = jnp.zeros(acc_ref.shape, F32)

    def store_logits(c, slot, bias_idx):
        masked = jnp.where(s_ref[c] >= thr, 0.0, NEG_BIG)
        row0 = pl.multiple_of(c * tk, tk)
        for h in range(nh):
            kc = k_ref[pl.ds(row0, tk), (h // 2) * LANES:(h // 2 + 1) * LANES]
            lt = lax.dot_general(kc, qpad_ref[h], _NT_DIMS, preferred_element_type=F32) + masked
            if bias_idx is not None:
                lt = lt + bias_ref[bias_idx, h]
            lg_ref[slot, h] = lt

    def softmax_pv(c, slot, m_all):
        m_out = []
        for h in range(nh):
            m_old = m_all[h]
            m_new = jnp.maximum(m_old, jnp.max(lg_ref[slot, h], axis=0, keepdims=True))
            p = jnp.exp2(lg_ref[slot, h] - m_new).astype(BF16)
            alpha = jnp.exp2(m_old - m_new)
            pv = jnp.dot(vt_ref[c, h * V_SLAB:(h + 1) * V_SLAB, :], p, preferred_element_type=F32)
            acc_ref[h] = alpha * acc_ref[h] + pv
            m_out.append(m_new)
        return tuple(m_out)

    def near_step(m_all):
        store_logits(i - 1, 1, 1)
        return softmax_pv(i, 0, m_all)

    def far_step(j, parity, m_all):
        c = i - 2 - j
        store_logits(c, parity, None)
        return softmax_pv(c + 1, 1 - parity, m_all)

    def far_pair(jj, m_all):
        return far_step(2 * jj + 1, 1, far_step(2 * jj, 0, m_all))

    n_far = jnp.maximum(i - 1, 0)
    m_all = tuple(jnp.full((1, tq), NEG_BIG, F32) for _ in range(nh))
    store_logits(i, 0, 0)
    m_all = lax.cond(i >= 1, near_step, lambda m: m, m_all)
    m_all = lax.fori_loop(0, n_far // 2, far_pair, m_all)
    m_all = lax.cond((n_far & 1) == 1, lambda m: far_step(n_far - 1, 0, m), lambda m: m, m_all)
    lax.cond((i & 1) == 0, lambda m: softmax_pv(0, 0, m), lambda m: softmax_pv(0, 1, m), m_all)

    for h in range(nh):
        o = acc_ref[h, :hd, :] / acc_ref[h, hd:hd + 1, :]
        ms = jnp.mean(o * o, axis=0, keepdims=True)
        out_ref[h * hd:(h + 1) * hd, :] = o * lax.rsqrt(ms + EPS)
    o_ref[...] = (out_ref[...].T * og_ref[...]).astype(BF16)


def _attn_call(rel_bias, bounds, q, qi, wit, k, ki, vt, og, topk):
    bsz, seq, aw = q.shape
    tq, tk = ATT_TQ, ATT_TK
    nck = seq // tk
    blk_q = lambda b, i: (b, i, 0)
    whole = lambda b, i: (b, 0, 0)
    smem = pl.BlockSpec(memory_space=pltpu.SMEM)
    return pl.pallas_call(
        functools.partial(_attn_kernel, topk=topk),
        out_shape=jax.ShapeDtypeStruct((bsz, seq, aw), BF16),
        grid=(bsz, seq // tq),
        in_specs=[
            smem, smem,
            pl.BlockSpec((None, tq, aw), blk_q),
            pl.BlockSpec((None, tq, aw), blk_q),
            pl.BlockSpec((None, IDX_HEADS, tq), lambda b, i: (b, 0, i)),
            pl.BlockSpec((None, seq, aw), whole),
            pl.BlockSpec((None, seq, LANES), whole),
            pl.BlockSpec((None, nck, ATTN_HEADS * V_SLAB, tk), lambda b, i: (b, 0, 0, 0)),
            pl.BlockSpec(og.shape, lambda b, i: (0, 0)),
        ],
        out_specs=pl.BlockSpec((None, tq, aw), blk_q),
        scratch_shapes=[
            pltpu.VMEM((nck, tk, tq), F32),
            pltpu.VMEM((2, ATTN_HEADS, tk, tq), F32),
            pltpu.VMEM((ATTN_HEADS, tq, LANES), BF16),
            pltpu.VMEM((IDX_HEADS, tq, LANES), BF16),
            pltpu.VMEM((2, ATTN_HEADS, tk, tq), F32),
            pltpu.VMEM((ATTN_HEADS, V_SLAB, tq), F32),
            pltpu.VMEM((aw, tq), F32),
        ],
        compiler_params=pltpu.CompilerParams(dimension_semantics=("arbitrary", "arbitrary"),
                                             vmem_limit_bytes=VMEM_LIMIT_BYTES),
        name="dsa_attention",
    )(rel_bias, bounds, q, qi, wit, k, ki, vt, og)


def _post_kernel(an_ref, cn_ref, x_ref, mod_ref, n2_ref, woa_ref, woc_ref, wr_ref, br_ref,
                 x1_ref, h2_ref, comb_ref):
    mix = (jnp.dot(an_ref[...], woa_ref[...], preferred_element_type=F32)
           + jnp.dot(cn_ref[...], woc_ref[...], preferred_element_type=F32))
    x1 = x_ref[...] + mod_ref[2:3, :] * mix
    x1_ref[...] = x1
    ms = jnp.mean(x1 * x1, axis=-1, keepdims=True)
    h2 = x1 * lax.rsqrt(ms + EPS) * n2_ref[...] * (1.0 + mod_ref[4:5, :]) + mod_ref[3:4, :]
    h2b = h2.astype(BF16)
    h2_ref[...] = h2b

    logits = jnp.dot(h2b, wr_ref[...], preferred_element_type=F32) + br_ref[...]
    lane = lax.broadcasted_iota(jnp.int32, logits.shape, 1)
    lane_f = lane.astype(F32)
    far = float(LANES)
    is_g = (lane >= N_EXPERTS) & (lane < N_EXPERTS + N_GROUPS)
    gl = jnp.where(is_g, logits, -jnp.inf)
    gmax = jnp.max(gl, axis=-1, keepdims=True)
    g_sel = jnp.min(jnp.where(is_g & (gl == gmax), lane_f, far), axis=-1, keepdims=True) - float(N_EXPERTS)
    p_g = 1.0 / jnp.sum(jnp.exp(gl - gmax), axis=-1, keepdims=True)

    in_grp = (lane < N_EXPERTS) & ((lane // EXPERTS_PER_GROUP).astype(F32) == g_sel)
    e1 = jnp.where(in_grp, logits, -jnp.inf)
    l1 = jnp.max(e1, axis=-1, keepdims=True)
    i1 = jnp.min(jnp.where(in_grp & (e1 == l1), lane_f, far), axis=-1, keepdims=True)
    rest = in_grp & (lane_f != i1)
    e2 = jnp.where(rest, logits, -jnp.inf)
    l2 = jnp.max(e2, axis=-1, keepdims=True)
    i2 = jnp.min(jnp.where(rest & (e2 == l2), lane_f, far), axis=-1, keepdims=True)
    r = jnp.exp(l2 - l1)
    w1 = 1.0 / (1.0 + r)
    w2 = r / (1.0 + r)
    comb_ref[...] = jnp.where(lane_f == i1, p_g * w1, 0.0) + jnp.where(lane_f == i2, p_g * w2, 0.0)


def _post_call(an, cn, x, mod, n2, woa, woc, wr, br):
    bsz, seq, d = x.shape
    tm = POST_TM
    tok = lambda b, j: (b, j, 0)
    const = lambda b, j: (0, 0)
    return pl.pallas_call(
        _post_kernel,
        out_shape=(jax.ShapeDtypeStruct((bsz, seq, d), F32),
                   jax.ShapeDtypeStruct((bsz, seq, d), BF16),
                   jax.ShapeDtypeStruct((bsz, seq, LANES), F32)),
        grid=(bsz, seq // tm),
        in_specs=[
            pl.BlockSpec((None, tm, ATTN_WIDTH), tok),
            pl.BlockSpec((None, tm, CONV_WIDTH), tok),
            pl.BlockSpec((None, tm, d), tok),
            pl.BlockSpec((None, 6, d), lambda b, j: (b, 0, 0)),
            pl.BlockSpec(n2.shape, const),
            pl.BlockSpec(woa.shape, const),
            pl.BlockSpec(woc.shape, const),
            pl.BlockSpec(wr.shape, const),
            pl.BlockSpec(br.shape, const),
        ],
        out_specs=(pl.BlockSpec((None, tm, d), tok),
                   pl.BlockSpec((None, tm, d), tok),
                   pl.BlockSpec((None, tm, LANES), tok)),
        compiler_params=pltpu.CompilerParams(dimension_semantics=("arbitrary", "arbitrary"),
                                             vmem_limit_bytes=VMEM_LIMIT_BYTES),
        name="post_router",
    )(an, cn, x, mod, n2, woa, woc, wr, br)


def _lane_scalar_i32(row, idx):
    lane = lax.broadcasted_iota(jnp.int32, row.shape, 1)
    picked = jnp.sum(jnp.where(lane == idx, row, 0.0), axis=1, keepdims=True)
    return picked.astype(jnp.int32)[0, 0]


def _strict_tri(n, lower):
    r = lax.broadcasted_iota(jnp.int32, (n, n), 0)
    c = lax.broadcasted_iota(jnp.int32, (n, n), 1)
    return jnp.where((c < r) if lower else (r < c), 1.0, 0.0).astype(BF16)


def _moe_kernel(h2_ref, comb_ref, x1_ref, mod_ref, wgu_ref, wd_ref,
                o_ref,
                xg_ref, y_ref, col_ref, row_ref, off_ref):
    e = pl.program_id(1)
    n_steps = N_EXPERTS // MOE_EPS
    nb = h2_ref.shape[0]
    tile, chunk = MOE_TILE, MOE_CHUNK
    lane = lax.broadcasted_iota(jnp.int32, (nb, LANES), 1)

    @pl.when(e == 0)
    def _():
        xg_ref[xg_ref.shape[0] - tile:, :] = jnp.zeros((tile, xg_ref.shape[1]), BF16)
        comb = comb_ref[...]
        assigned = comb != 0.0
        a_f = jnp.where(assigned, 1.0, 0.0)
        rank = jnp.dot(_strict_tri(nb, True), a_f.astype(BF16), preferred_element_type=F32)
        cnt = rank[nb - 1:nb, :] + a_f[nb - 1:nb, :]
        ntile = jnp.floor((cnt + float(tile - 1)) * (1.0 / tile))
        first = jnp.dot(jnp.broadcast_to(ntile, (SUBLANES, LANES)).astype(BF16), _strict_tri(LANES, False),
                        preferred_element_type=F32)[0:1, :]
        off_ref[0:1, :] = first
        off_ref[1:2, :] = ntile
        pos = first * float(tile) + rank
        pos1 = jnp.min(jnp.where(assigned, pos, 1e9), axis=1, keepdims=True)
        pos2 = jnp.max(jnp.where(assigned, pos, -1.0), axis=1, keepdims=True)
        pos2 = jnp.where(pos2 == pos1, -1.0, pos2)
        cw1 = jnp.sum(jnp.where(assigned & (pos == pos1), comb, 0.0), axis=1, keepdims=True)
        cw2 = jnp.sum(jnp.where(assigned & (pos == pos2), comb, 0.0), axis=1, keepdims=True)
        info = jnp.where(lane == 0, pos1, jnp.where(lane == 1, pos2, jnp.where(lane == 2, cw1,
                         jnp.where(lane == 3, cw2, 0.0))))
        col_ref[...] = info
        row_ref[...] = info.T

        n_chunks = (_lane_scalar_i32(first, N_EXPERTS) * tile + (chunk - 1)) // chunk
        p1 = row_ref[0:1, :].astype(jnp.int32)
        p2 = row_ref[1:2, :].astype(jnp.int32)
        sub = lax.broadcasted_iota(jnp.int32, (chunk, nb), 0)

        def gather(c, carry):
            p = sub + c * chunk
            sel = jnp.where((p == p1) | (p == p2), 1.0, 0.0).astype(BF16)
            r0 = pl.multiple_of(c * chunk, chunk)
            xg_ref[pl.ds(r0, chunk), :] = jnp.dot(sel, h2_ref[...], preferred_element_type=F32).astype(BF16)
            return carry

        lax.fori_loop(0, n_chunks, gather, 0)

    total = _lane_scalar_i32(off_ref[0:1, :], N_EXPERTS)
    padded_total = ((total * tile + (chunk - 1)) // chunk) * (chunk // tile)
    spare_tile = xg_ref.shape[0] // tile - 1
    firsts, counts = [], []
    for x in range(MOE_EPS):
        ex = e * MOE_EPS + x
        first_x = _lane_scalar_i32(off_ref[0:1, :], ex)
        count_x = _lane_scalar_i32(off_ref[1:2, :], ex)
        if x == MOE_EPS - 1:
            count_x = jnp.where(e == n_steps - 1, padded_total - first_x, count_x)
        firsts.append(first_x)
        counts.append(count_x)
    most = counts[0]
    for x in range(1, MOE_EPS):
        most = jnp.maximum(most, counts[x])

    def expert_tiles(j, carry):
        for x in range(MOE_EPS):
            t_idx = jnp.where(counts[x] > 0, firsts[x] + jnp.minimum(j, counts[x] - 1), spare_tile)
            r0 = pl.multiple_of(t_idx * tile, tile)
            ab = jnp.dot(xg_ref[pl.ds(r0, tile), :], wgu_ref[x], preferred_element_type=F32)
            a = ab[:, :EXPERT_FF]
            hid = (a * jax.nn.sigmoid(a)) * ab[:, EXPERT_FF:]
            y_ref[pl.ds(r0, tile), :] = jnp.dot(hid.astype(BF16), wd_ref[x],
                                                preferred_element_type=F32).astype(BF16)
        return carry

    lax.fori_loop(0, most, expert_tiles, 0)

    @pl.when(e == n_steps - 1)
    def _():
        p1 = col_ref[:, 0:1].astype(jnp.int32)
        p2 = col_ref[:, 1:2].astype(jnp.int32)
        cw1 = col_ref[:, 2:3]
        cw2 = col_ref[:, 3:4]
        gate = mod_ref[5:6, :]
        lane_c = lax.broadcasted_iota(jnp.int32, (nb, chunk), 1)
        o_ref[...] = x1_ref[...]

        def scatter(c, carry):
            p = lane_c + c * chunk
            w = (jnp.where(p == p1, cw1, 0.0) + jnp.where(p == p2, cw2, 0.0)).astype(BF16)
            r0 = pl.multiple_of(c * chunk, chunk)
            o_ref[...] += gate * jnp.dot(w, y_ref[pl.ds(r0, chunk), :], preferred_element_type=F32)
            return carry

        lax.fori_loop(0, padded_total // (chunk // tile), scatter, 0)


def _moe_call(h2, comb, x1, mod, wgu, wd):
    bsz, seq, d = x1.shape
    nb = MOE_TM
    n_blk = bsz * seq // nb
    rows_max = 2 * nb + N_EXPERTS * MOE_TILE
    rows_max = -(-rows_max // MOE_CHUNK) * MOE_CHUNK + MOE_TILE
    tok = lambda j, e: (j, 0)
    out = pl.pallas_call(
        _moe_kernel,
        out_shape=jax.ShapeDtypeStruct((bsz * seq, d), F32),
        grid=(n_blk, N_EXPERTS // MOE_EPS),
        in_specs=[
            pl.BlockSpec((nb, d), tok),
            pl.BlockSpec((nb, LANES), tok),
            pl.BlockSpec((nb, d), tok),
            pl.BlockSpec((None, 6, d), lambda j, e: ((j * nb) // seq, 0, 0)),
            pl.BlockSpec((MOE_EPS, d, 2 * EXPERT_FF), lambda j, e: (e, 0, 0)),
            pl.BlockSpec((MOE_EPS, EXPERT_FF, d), lambda j, e: (e, 0, 0)),
        ],
        out_specs=pl.BlockSpec((nb, d), tok),
        scratch_shapes=[
            pltpu.VMEM((rows_max, d), BF16),
            pltpu.VMEM((rows_max, d), BF16),
            pltpu.VMEM((nb, LANES), F32),
            pltpu.VMEM((LANES, nb), F32),
            pltpu.VMEM((SUBLANES, LANES), F32),
        ],
        compiler_params=pltpu.CompilerParams(dimension_semantics=("arbitrary", "arbitrary"),
                                             vmem_limit_bytes=VMEM_LIMIT_BYTES),
        name="moe_experts",
    )(h2.reshape(bsz * seq, d), comb.reshape(bsz * seq, LANES), x1.reshape(bsz * seq, d), mod, wgu, wd)
    return out.reshape(bsz, seq, d)


def _layer(x, mod, rel_bias, norm1, w_in, q_norm, k_norm, conv_w, attn_out_norm, conv_out_norm, w_out,
           norm2, w_group_router, b_group_router, w_expert_router, b_expert_router, w_gate, w_up, w_down):
    bsz, seq, d = x.shape
    aw = ATTN_WIDTH
    topk = min(TOPK_MAX, seq // 4)

    offs = np.cumsum([0, aw, aw, aw, IDX_HEADS * IDX_DIM, IDX_DIM, IDX_HEADS, CONV_WIDTH, CONV_WIDTH, CONV_WIDTH])
    col = lambda n: w_in[:, int(offs[n]):int(offs[n + 1])]
    wm = jnp.concatenate([col(0), col(1), col(3), col(6), col(7), col(8)], axis=1).astype(BF16)
    wvt = col(2).T.astype(BF16)
    wki = jnp.concatenate([col(4), col(4)], axis=1).astype(BF16)
    wwit = col(5).T.astype(BF16)
    qg = (jnp.tile(q_norm, ATTN_HEADS) * ((HEAD_DIM ** -0.5) * LOG2E))[None, :]
    kg = jnp.tile(k_norm, ATTN_HEADS)[None, :]
    grp = np.arange(aw) // CONV_GROUP_DIM
    gmat = jnp.asarray((grp[:, None] == grp[None, :]).astype(np.float32) / CONV_GROUP_DIM, dtype=BF16)

    q, k, vt, qi, ki, wit, cn = _pre_call(
        x, mod, norm1[None, :], wm, wvt, wki, wwit, qg, kg, conv_w, conv_out_norm.reshape(1, -1), gmat)

    bounds = jnp.asarray(_bucket_boundaries())
    an = _attn_call(rel_bias, bounds, q, qi, wit, k, ki, vt, attn_out_norm.reshape(1, -1), topk)

    wr = jnp.concatenate([w_expert_router, w_group_router,
                          jnp.zeros((d, LANES - N_EXPERTS - N_GROUPS), F32)], axis=1).astype(BF16)
    br = jnp.concatenate([b_expert_router, b_group_router,
                          jnp.zeros((LANES - N_EXPERTS - N_GROUPS,), F32)])[None, :]
    x1, h2, comb = _post_call(an, cn, x, mod, norm2[None, :], w_out[:aw].astype(BF16), w_out[aw:].astype(BF16),
                              wr, br)

    wgu = jnp.concatenate([w_gate, w_up], axis=-1).astype(BF16)
    return _moe_call(h2, comb, x1, mod, wgu, w_down.astype(BF16))


def kernel(x, c, rel_bias, w_ada, b_ada, norm1, w_in, q_norm, k_norm, conv_w, attn_out_norm, conv_out_norm,
           w_out, norm2, w_group_router, b_group_router, w_expert_router, b_expert_router, w_gate, w_up,
           w_down):
    bsz, seq, d = x.shape
    assert d == D_MODEL and seq % max(PRE_TM, POST_TM, MOE_TM) == 0 and ATT_TQ == ATT_TK
    depth = w_ada.shape[0]
    for l in range(depth):
        mod = _mod_call(c, w_ada[l], b_ada[l][None, :]).reshape(bsz, 6, d)
        x = _layer(x, mod, rel_bias, norm1[l], w_in[l], q_norm[l], k_norm[l], conv_w[l], attn_out_norm[l],
                   conv_out_norm[l], w_out[l], norm2[l], w_group_router[l], b_group_router[l],
                   w_expert_router[l], b_expert_router[l], w_gate[l], w_up[l], w_down[l])
    return x
```

```python
import functools
import math

import jax
import jax.numpy as jnp
import numpy as np
from jax import lax
from jax.experimental import pallas as pl
from jax.experimental.pallas import tpu as pltpu

F32 = jnp.float32
BF16 = jnp.bfloat16

D_MODEL = 1024
HEAD_DIM = 64
ATTN_HEADS = 8
ATTN_WIDTH = ATTN_HEADS * HEAD_DIM
CONV_WIDTH = D_MODEL - ATTN_WIDTH
CONV_GROUP_DIM = 64
CONV_K = 3
IDX_HEADS = 8
IDX_DIM = 64
TOPK_MAX = 256
IDX_SCALE = (IDX_DIM ** -0.5) * (IDX_HEADS ** -0.5)
N_BUCKETS = 32
MAX_DISTANCE = 128
N_GROUPS = 4
EXPERTS_PER_GROUP = 8
N_EXPERTS = N_GROUPS * EXPERTS_PER_GROUP
EXPERT_FF = 256
EPS = 1e-6
LOG2E = 1.4426950408889634
NEG_BIG = -1e30
COUNT_ACCS = 4
BISECT_GROUP = 4
BISECT_VALUE_STEPS = 8
BISECT_MAX_STEPS = 64

LANES = 128
SUBLANES = 8
BF16_SUBLANES = 16
V_SLAB = HEAD_DIM + BF16_SUBLANES
VMEM_LIMIT_BYTES = 56 * 1024 * 1024

PRE_TM = 512
ATT_TQ = 256
ATT_TK = 256
POST_TM = 512
MOE_TM = 1024
MOE_TILE = 64
MOE_CHUNK = 512
MOE_EPS = 4
MOD_TN = 1536

_NT_DIMS = (((1,), (1,)), ((), ()))


def _tree_sum(parts):
    while len(parts) > 1:
        nxt = [parts[j] + parts[j + 1] for j in range(0, len(parts) - 1, 2)]
        if len(parts) % 2:
            nxt.append(parts[-1])
        parts = nxt
    return parts[0]


def _bucket_boundaries():
    max_exact = N_BUCKETS // 2
    d = np.arange(0, 4 * MAX_DISTANCE, dtype=np.int64)
    nf = np.maximum(d, 1).astype(np.float32)
    large = max_exact + (np.log(nf / np.float32(max_exact)) / np.float32(math.log(MAX_DISTANCE / max_exact))
                         * np.float32(N_BUCKETS - max_exact)).astype(np.int32)
    large = np.minimum(large, N_BUCKETS - 1)
    bucket = np.where(d < max_exact, d, large)
    assert np.all(np.diff(bucket) >= 0) and bucket[-1] == N_BUCKETS - 1
    bounds = [int(np.argmax(bucket >= j)) for j in range(1, N_BUCKETS)]
    return np.asarray([0] + bounds, dtype=np.int32)


def _mod_kernel(c_ref, w_ref, b_ref, o_ref):
    c = c_ref[...]
    act = c * jax.nn.sigmoid(c)
    o_ref[...] = jnp.dot(act, w_ref[...], preferred_element_type=F32,
                         precision=lax.Precision.HIGHEST) + b_ref[...]


def _mod_call(c, w_ada, b_ada):
    bsz, d = c.shape
    n = w_ada.shape[1]
    return pl.pallas_call(
        _mod_kernel,
        out_shape=jax.ShapeDtypeStruct((bsz, n), F32),
        grid=(n // MOD_TN,),
        in_specs=[pl.BlockSpec((bsz, d), lambda j: (0, 0)),
                  pl.BlockSpec((d, MOD_TN), lambda j: (0, j)),
                  pl.BlockSpec((1, MOD_TN), lambda j: (0, j))],
        out_specs=pl.BlockSpec((bsz, MOD_TN), lambda j: (0, j)),
        compiler_params=pltpu.CompilerParams(dimension_semantics=("arbitrary",),
                                             vmem_limit_bytes=VMEM_LIMIT_BYTES),
        name="adaln_mod",
    )(c, w_ada, b_ada)


def _group_rms(y, g_ref):
    ms = jnp.dot((y * y).astype(BF16), g_ref[...], preferred_element_type=F32)
    return y * lax.rsqrt(ms + EPS)


def _pre_kernel(x_ref, mod_ref, n1_ref, wm_ref, wvt_ref, wki_ref, wwit_ref, qg_ref, kg_ref,
                cw_ref, cg_ref, g_ref,
                q_ref, k_ref, vt_ref, qi_ref, ki_ref, wit_ref, cn_ref, carry_ref):
    j = pl.program_id(1)
    tm = x_ref.shape[0]
    aw = ATTN_WIDTH

    x = x_ref[...]
    ms = jnp.mean(x * x, axis=-1, keepdims=True)
    y = x * lax.rsqrt(ms + EPS) * n1_ref[...]
    h = y * (1.0 + mod_ref[1:2, :]) + mod_ref[0:1, :]
    hb = h.astype(BF16)

    def proj(lo):
        return jnp.dot(hb, wm_ref[:, lo:lo + aw], preferred_element_type=F32)

    q = _group_rms(proj(0), g_ref) * qg_ref[...]
    q_ref[...] = q.astype(BF16)
    k = _group_rms(proj(aw), g_ref) * kg_ref[...]
    k_ref[...] = k.astype(BF16)

    vt = lax.dot_general(wvt_ref[...], hb, _NT_DIMS, preferred_element_type=F32).astype(BF16)
    ones = jnp.ones((BF16_SUBLANES, ATT_TK), BF16)
    for cc in range(tm // ATT_TK):
        for hh in range(ATTN_HEADS):
            vt_ref[cc, hh * V_SLAB:hh * V_SLAB + HEAD_DIM, :] = (
                vt[hh * HEAD_DIM:(hh + 1) * HEAD_DIM, cc * ATT_TK:(cc + 1) * ATT_TK])
            vt_ref[cc, hh * V_SLAB + HEAD_DIM:(hh + 1) * V_SLAB, :] = ones

    qi_ref[...] = proj(2 * aw).astype(BF16)
    ki_ref[...] = jnp.dot(hb, wki_ref[...], preferred_element_type=F32).astype(BF16)
    wit_ref[...] = lax.dot_general(wwit_ref[...], hb, _NT_DIMS, preferred_element_type=F32) * IDX_SCALE

    gate_b = proj(3 * aw)
    z = proj(4 * aw) * proj(5 * aw)

    @pl.when(j == 0)
    def _():
        carry_ref[...] = jnp.zeros_like(carry_ref)

    prev = carry_ref[...]
    row = lax.broadcasted_iota(jnp.int32, z.shape, 0)
    z1 = jnp.where(row == 0, prev[SUBLANES - 1:SUBLANES, :], pltpu.roll(z, 1, 0))
    z2 = pltpu.roll(z, 2, 0)
    z2 = jnp.where(row == 0, prev[SUBLANES - 2:SUBLANES - 1, :], z2)
    z2 = jnp.where(row == 1, prev[SUBLANES - 1:SUBLANES, :], z2)
    carry_ref[...] = z[tm - SUBLANES:, :]
    conv = cw_ref[2:3, :] * z + cw_ref[1:2, :] * z1 + cw_ref[0:1, :] * z2
    yc = gate_b * conv
    cn_ref[...] = (_group_rms(yc, g_ref) * cg_ref[...]).astype(BF16)


def _pre_call(x, mod, n1, wm, wvt, wki, wwit, qg, kg, cw, cg, gmat):
    bsz, seq, d = x.shape
    tm = PRE_TM
    nck = tm // ATT_TK
    aw = ATTN_WIDTH
    const = lambda b, j: (0, 0)
    tok = lambda b, j: (b, j, 0)
    out_shape = (
        jax.ShapeDtypeStruct((bsz, seq, aw), BF16),
        jax.ShapeDtypeStruct((bsz, seq, aw), BF16),
        jax.ShapeDtypeStruct((bsz, seq // ATT_TK, ATTN_HEADS * V_SLAB, ATT_TK), BF16),
        jax.ShapeDtypeStruct((bsz, seq, aw), BF16),
        jax.ShapeDtypeStruct((bsz, seq, LANES), BF16),
        jax.ShapeDtypeStruct((bsz, IDX_HEADS, seq), F32),
        jax.ShapeDtypeStruct((bsz, seq, CONV_WIDTH), BF16),
    )
    out_specs = (
        pl.BlockSpec((None, tm, aw), tok),
        pl.BlockSpec((None, tm, aw), tok),
        pl.BlockSpec((None, nck, ATTN_HEADS * V_SLAB, ATT_TK), lambda b, j: (b, j, 0, 0)),
        pl.BlockSpec((None, tm, aw), tok),
        pl.BlockSpec((None, tm, LANES), tok),
        pl.BlockSpec((None, IDX_HEADS, tm), lambda b, j: (b, 0, j)),
        pl.BlockSpec((None, tm, CONV_WIDTH), tok),
    )
    in_specs = [
        pl.BlockSpec((None, tm, d), tok),
        pl.BlockSpec((None, 6, d), lambda b, j: (b, 0, 0)),
        pl.BlockSpec(n1.shape, const),
        pl.BlockSpec(wm.shape, const),
        pl.BlockSpec(wvt.shape, const),
        pl.BlockSpec(wki.shape, const),
        pl.BlockSpec(wwit.shape, const),
        pl.BlockSpec(qg.shape, const),
        pl.BlockSpec(kg.shape, const),
        pl.BlockSpec(cw.shape, const),
        pl.BlockSpec(cg.shape, const),
        pl.BlockSpec(gmat.shape, const),
    ]
    return pl.pallas_call(
        _pre_kernel,
        out_shape=out_shape,
        grid=(bsz, seq // tm),
        in_specs=in_specs,
        out_specs=out_specs,
        scratch_shapes=[pltpu.VMEM((SUBLANES, CONV_WIDTH), F32)],
        compiler_params=pltpu.CompilerParams(dimension_semantics=("arbitrary", "arbitrary"),
                                             vmem_limit_bytes=VMEM_LIMIT_BYTES),
        name="pre_proj",
    )(x, mod, n1, wm, wvt, wki, wwit, qg, kg, cw, cg, gmat)


def _attn_kernel(rb_ref, bnd_ref, q_ref, qi_ref, wit_ref, k_ref, ki_ref, vt_ref, og_ref,
                 o_ref,
                 s_ref, bias_ref, qpad_ref, qipad_ref, lg_ref, acc_ref, out_ref, *, topk):
    b = pl.program_id(0)
    i = pl.program_id(1)
    tq, tk = ATT_TQ, ATT_TK
    nh, hd = ATTN_HEADS, HEAD_DIM

    t_loc = lax.broadcasted_iota(jnp.int32, (tk, tq), 1)
    s_loc = lax.broadcasted_iota(jnp.int32, (tk, tq), 0)

    @pl.when((b == 0) & (i == 0))
    def _():
        for idx in range(2):
            dist = t_loc - s_loc + idx * tq
            for h in range(nh):
                bias_ref[idx, h] = jnp.full((tk, tq), (rb_ref[0, h] - rb_ref[N_BUCKETS - 1, h]) * LOG2E, F32)

            def fill(jb, carry):
                reached = dist >= bnd_ref[jb]
                for h in range(nh):
                    val = (rb_ref[jb, h] - rb_ref[N_BUCKETS - 1, h]) * LOG2E
                    bias_ref[idx, h] = jnp.where(reached, val, bias_ref[idx, h])
                return carry

            lax.fori_loop(1, N_BUCKETS, fill, 0)

    lane = lax.broadcasted_iota(jnp.int32, (tq, LANES), 1)
    for h in range(nh):
        pair = slice((h // 2) * LANES, (h // 2 + 1) * LANES)
        keep = (lane // hd) == (h % 2)
        qpad_ref[h] = jnp.where(keep, q_ref[:, pair], jnp.zeros((), BF16))
        qipad_ref[h] = jnp.where(keep, qi_ref[:, pair], jnp.zeros((), BF16))

    def score_chunk(c):
        kic = ki_ref[pl.ds(pl.multiple_of(c * tk, tk), tk), :]
        acc = jnp.zeros((tk, tq), F32)
        for h in range(nh):
            e = lax.dot_general(kic, qipad_ref[h], _NT_DIMS, preferred_element_type=F32)
            acc = acc + wit_ref[h:h + 1, :] * jnp.maximum(e, 0.0)
        return acc

    def a_body(c, carry):
        rmin, rmax = carry
        sc = score_chunk(c)
        s_ref[c] = sc
        return (jnp.minimum(rmin, jnp.min(sc, axis=0, keepdims=True)),
                jnp.maximum(rmax, jnp.max(sc, axis=0, keepdims=True)))

    rmin0 = jnp.full((1, tq), jnp.inf, F32)
    rmax0 = jnp.full((1, tq), -jnp.inf, F32)
    rmin, rmax = lax.fori_loop(0, i, a_body, (rmin0, rmax0))
    sc = score_chunk(i)
    causal = s_loc <= t_loc
    s_ref[i] = jnp.where(causal, sc, -jnp.inf)
    rmin = jnp.minimum(rmin, jnp.min(jnp.where(causal, sc, jnp.inf), axis=0, keepdims=True))
    rmax = jnp.maximum(rmax, jnp.max(jnp.where(causal, sc, -jnp.inf), axis=0, keepdims=True))

    def count_ge(thr):
        def body(c, accs):
            hit = s_ref[c] >= thr
            accs = list(accs)
            for r in range(tk // SUBLANES):
                a = accs[r % COUNT_ACCS]
                accs[r % COUNT_ACCS] = jnp.where(hit[r * SUBLANES:(r + 1) * SUBLANES], a + 1.0, a)
            return tuple(accs)
        accs = lax.fori_loop(0, i + 1, body,
                             tuple(jnp.zeros((SUBLANES, tq), F32) for _ in range(COUNT_ACCS)))
        return jnp.sum(_tree_sum(list(accs)), axis=0, keepdims=True)

    def order_key(v):
        bits = pltpu.bitcast(v, jnp.int32)
        return jnp.where(bits < 0, bits ^ jnp.int32(0x7FFFFFFF), bits)

    def from_order_key(key):
        return pltpu.bitcast(jnp.where(key < 0, key ^ jnp.int32(0x7FFFFFFF), key), F32)

    t_glob = (i * tq + lax.broadcasted_iota(jnp.int32, (1, tq), 1)).astype(F32)
    n_causal = t_glob + 1.0
    kf = jnp.minimum(float(topk), n_causal)
    c_max = count_ge(rmax)
    all_sel = n_causal <= kf
    max_ge = c_max >= kf
    active0 = jnp.where(all_sel | max_ge, 0.0, 1.0)
    thr0 = jnp.where(all_sel, rmin, rmax)
    tie0 = jnp.where(jnp.logical_not(all_sel) & (c_max > kf), 1.0, 0.0)
    hif0 = jnp.full((1, tq), jnp.inf, F32)
    need0 = kf

    def b_cond(st):
        return (jnp.max(st[0]) > 0.0) & (st[8] <= BISECT_MAX_STEPS)

    def b_body(st):
        active, lo, hi, fhi, thr, tie, hif, need, step = st
        lo_key = order_key(lo)
        hi_key = order_key(hi)
        mid_key = (lo_key >> 1) + (hi_key >> 1) + (lo_key & hi_key & 1)
        mid_val = lo + (hi - lo) * 0.5
        use_val = (step < BISECT_VALUE_STEPS) & (mid_val > lo) & (mid_val < hi)
        mid = jnp.where(use_val, mid_val, from_order_key(mid_key))
        collapsed = (mid_key == lo_key) | (step >= BISECT_MAX_STEPS)
        cm = count_ge(mid)
        act = active > 0.0
        live = act & jnp.logical_not(collapsed)
        found = live & (cm == kf)
        go_up = live & (cm > kf)
        go_dn = live & (cm < kf)
        ends_tie = act & collapsed
        thr = jnp.where(found, mid, jnp.where(ends_tie, lo, thr))
        tie = jnp.where(ends_tie, 1.0, tie)
        hif = jnp.where(ends_tie, hi, hif)
        need = jnp.where(ends_tie, kf - fhi, need)
        lo = jnp.where(go_up, mid, lo)
        fhi = jnp.where(go_dn, cm, fhi)
        hi = jnp.where(go_dn, mid, hi)
        active = jnp.where(found | ends_tie, 0.0, active)
        return active, lo, hi, fhi, thr, tie, hif, need, step + 1

    def b_group(st):
        for _ in range(BISECT_GROUP):
            st = b_body(st)
        return st

    _, _, _, _, thr, tie, hif, need, _ = lax.while_loop(
        b_cond, b_group, (active0, rmin, rmax, c_max, thr0, tie0, hif0, need0, jnp.int32(0)))

    @pl.when(jnp.max(tie) > 0.0)
    def _():
        tri = jnp.where(lax.broadcasted_iota(jnp.int32, (tk, tk), 1)
                        <= lax.broadcasted_iota(jnp.int32, (tk, tk), 0), 1.0, 0.0).astype(BF16)

        def body(c, seen):
            sc_c = s_ref[c]
            tied = (sc_c >= thr) & (sc_c < hif) & (tie > 0.0)
            rank = jnp.dot(tri, jnp.where(tied, 1.0, 0.0).astype(BF16), preferred_element_type=F32) + seen
            s_ref[c] = jnp.where(tied & (rank > need), -jnp.inf, sc_c)
            return rank[tk - 1:tk, :]

        lax.fori_loop(0, i + 1, body, jnp.zeros((1, tq), F32))

    acc_ref[...] = jnp.zeros(acc_ref.shape, F32)

    def store_logits(c, slot, bias_idx):
        masked = jnp.where(s_ref[c] >= thr, 0.0, NEG_BIG)
        row0 = pl.multiple_of(c * tk, tk)
        for h in range(nh):
            kc = k_ref[pl.ds(row0, tk), (h // 2) * LANES:(h // 2 + 1) * LANES]
            lt = lax.dot_general(kc, qpad_ref[h], _NT_DIMS, preferred_element_type=F32) + masked
            if bias_idx is not None:
                lt = lt + bias_ref[bias_idx, h]
            lg_ref[slot, h] = lt

    def softmax_pv(c, slot, m_all):
        m_out = []
        for h in range(nh):
            m_old = m_all[h]
            m_new = jnp.maximum(m_old, jnp.max(lg_ref[slot, h], axis=0, keepdims=True))
            p = jnp.exp2(lg_ref[slot, h] - m_new).astype(BF16)
            alpha = jnp.exp2(m_old - m_new)
            pv = jnp.dot(vt_ref[c, h * V_SLAB:(h + 1) * V_SLAB, :], p, preferred_element_type=F32)
            acc_ref[h] = alpha * acc_ref[h] + pv
            m_out.append(m_new)
        return tuple(m_out)

    def near_step(m_all):
        store_logits(i - 1, 1, 1)
        return softmax_pv(i, 0, m_all)

    def far_step(j, parity, m_all):
        c = i - 2 - j
        store_logits(c, parity, None)
        return softmax_pv(c + 1, 1 - parity, m_all)

    def far_pair(jj, m_all):
        return far_step(2 * jj + 1, 1, far_step(2 * jj, 0, m_all))

    n_far = jnp.maximum(i - 1, 0)
    m_all = tuple(jnp.full((1, tq), NEG_BIG, F32) for _ in range(nh))
    store_logits(i, 0, 0)
    m_all = lax.cond(i >= 1, near_step, lambda m: m, m_all)
    m_all = lax.fori_loop(0, n_far // 2, far_pair, m_all)
    m_all = lax.cond((n_far & 1) == 1, lambda m: far_step(n_far - 1, 0, m), lambda m: m, m_all)
    lax.cond((i & 1) == 0, lambda m: softmax_pv(0, 0, m), lambda m: softmax_pv(0, 1, m), m_all)

    for h in range(nh):
        o = acc_ref[h, :hd, :] / acc_ref[h, hd:hd + 1, :]
        ms = jnp.mean(o * o, axis=0, keepdims=True)
        out_ref[h * hd:(h + 1) * hd, :] = o * lax.rsqrt(ms + EPS)
    o_ref[...] = (out_ref[...].T * og_ref[...]).astype(BF16)


def _attn_call(rel_bias, bounds, q, qi, wit, k, ki, vt, og, topk):
    bsz, seq, aw = q.shape
    tq, tk = ATT_TQ, ATT_TK
    nck = seq // tk
    blk_q = lambda b, i: (b, i, 0)
    whole = lambda b, i: (b, 0, 0)
    smem = pl.BlockSpec(memory_space=pltpu.SMEM)
    return pl.pallas_call(
        functools.partial(_attn_kernel, topk=topk),
        out_shape=jax.ShapeDtypeStruct((bsz, seq, aw), BF16),
        grid=(bsz, seq // tq),
        in_specs=[
            smem, smem,
            pl.BlockSpec((None, tq, aw), blk_q),
            pl.BlockSpec((None, tq, aw), blk_q),
            pl.BlockSpec((None, IDX_HEADS, tq), lambda b, i: (b, 0, i)),
            pl.BlockSpec((None, seq, aw), whole),
            pl.BlockSpec((None, seq, LANES), whole),
            pl.BlockSpec((None, nck, ATTN_HEADS * V_SLAB, tk), lambda b, i: (b, 0, 0, 0)),
            pl.BlockSpec(og.shape, lambda b, i: (0, 0)),
        ],
        out_specs=pl.BlockSpec((None, tq, aw), blk_q),
        scratch_shapes=[
            pltpu.VMEM((nck, tk, tq), F32),
            pltpu.VMEM((2, ATTN_HEADS, tk, tq), F32),
            pltpu.VMEM((ATTN_HEADS, tq, LANES), BF16),
            pltpu.VMEM((IDX_HEADS, tq, LANES), BF16),
            pltpu.VMEM((2, ATTN_HEADS, tk, tq), F32),
            pltpu.VMEM((ATTN_HEADS, V_SLAB, tq), F32),
            pltpu.VMEM((aw, tq), F32),
        ],
        compiler_params=pltpu.CompilerParams(dimension_semantics=("arbitrary", "arbitrary"),
                                             vmem_limit_bytes=VMEM_LIMIT_BYTES),
        name="dsa_attention",
    )(rel_bias, bounds, q, qi, wit, k, ki, vt, og)


def _post_kernel(an_ref, cn_ref, x_ref, mod_ref, n2_ref, woa_ref, woc_ref, wr_ref, br_ref,
                 x1_ref, h2_ref, comb_ref):
    mix = (jnp.dot(an_ref[...], woa_ref[...], preferred_element_type=F32)
           + jnp.dot(cn_ref[...], woc_ref[...], preferred_element_type=F32))
    x1 = x_ref[...] + mod_ref[2:3, :] * mix
    x1_ref[...] = x1
    ms = jnp.mean(x1 * x1, axis=-1, keepdims=True)
    h2 = x1 * lax.rsqrt(ms + EPS) * n2_ref[...] * (1.0 + mod_ref[4:5, :]) + mod_ref[3:4, :]
    h2b = h2.astype(BF16)
    h2_ref[...] = h2b

    logits = jnp.dot(h2b, wr_ref[...], preferred_element_type=F32) + br_ref[...]
    lane = lax.broadcasted_iota(jnp.int32, logits.shape, 1)
    lane_f = lane.astype(F32)
    far = float(LANES)
    is_g = (lane >= N_EXPERTS) & (lane < N_EXPERTS + N_GROUPS)
    gl = jnp.where(is_g, logits, -jnp.inf)
    gmax = jnp.max(gl, axis=-1, keepdims=True)
    g_sel = jnp.min(jnp.where(is_g & (gl == gmax), lane_f, far), axis=-1, keepdims=True) - float(N_EXPERTS)
    p_g = 1.0 / jnp.sum(jnp.exp(gl - gmax), axis=-1, keepdims=True)

    in_grp = (lane < N_EXPERTS) & ((lane // EXPERTS_PER_GROUP).astype(F32) == g_sel)
    e1 = jnp.where(in_grp, logits, -jnp.inf)
    l1 = jnp.max(e1, axis=-1, keepdims=True)
    i1 = jnp.min(jnp.where(in_grp & (e1 == l1), lane_f, far), axis=-1, keepdims=True)
    rest = in_grp & (lane_f != i1)
    e2 = jnp.where(rest, logits, -jnp.inf)
    l2 = jnp.max(e2, axis=-1, keepdims=True)
    i2 = jnp.min(jnp.where(rest & (e2 == l2), lane_f, far), axis=-1, keepdims=True)
    r = jnp.exp(l2 - l1)
    w1 = 1.0 / (1.0 + r)
    w2 = r / (1.0 + r)
    comb_ref[...] = jnp.where(lane_f == i1, p_g * w1, 0.0) + jnp.where(lane_f == i2, p_g * w2, 0.0)


def _post_call(an, cn, x, mod, n2, woa, woc, wr, br):
    bsz, seq, d = x.shape
    tm = POST_TM
    tok = lambda b, j: (b, j, 0)
    const = lambda b, j: (0, 0)
    return pl.pallas_call(
        _post_kernel,
        out_shape=(jax.ShapeDtypeStruct((bsz, seq, d), F32),
                   jax.ShapeDtypeStruct((bsz, seq, d), BF16),
                   jax.ShapeDtypeStruct((bsz, seq, LANES), F32)),
        grid=(bsz, seq // tm),
        in_specs=[
            pl.BlockSpec((None, tm, ATTN_WIDTH), tok),
            pl.BlockSpec((None, tm, CONV_WIDTH), tok),
            pl.BlockSpec((None, tm, d), tok),
            pl.BlockSpec((None, 6, d), lambda b, j: (b, 0, 0)),
            pl.BlockSpec(n2.shape, const),
            pl.BlockSpec(woa.shape, const),
            pl.BlockSpec(woc.shape, const),
            pl.BlockSpec(wr.shape, const),
            pl.BlockSpec(br.shape, const),
        ],
        out_specs=(pl.BlockSpec((None, tm, d), tok),
                   pl.BlockSpec((None, tm, d), tok),
                   pl.BlockSpec((None, tm, LANES), tok)),
        compiler_params=pltpu.CompilerParams(dimension_semantics=("arbitrary", "arbitrary"),
                                             vmem_limit_bytes=VMEM_LIMIT_BYTES),
        name="post_router",
    )(an, cn, x, mod, n2, woa, woc, wr, br)


def _lane_scalar_i32(row, idx):
    lane = lax.broadcasted_iota(jnp.int32, row.shape, 1)
    picked = jnp.sum(jnp.where(lane == idx, row, 0.0), axis=1, keepdims=True)
    return picked.astype(jnp.int32)[0, 0]


def _strict_tri(n, lower):
    r = lax.broadcasted_iota(jnp.int32, (n, n), 0)
    c = lax.broadcasted_iota(jnp.int32, (n, n), 1)
    return jnp.where((c < r) if lower else (r < c), 1.0, 0.0).astype(BF16)


def _moe_kernel(h2_ref, comb_ref, x1_ref, mod_ref, wgu_ref, wd_ref,
                o_ref,
                xg_ref, y_ref, col_ref, row_ref, off_ref):
    e = pl.program_id(1)
    n_steps = N_EXPERTS // MOE_EPS
    nb = h2_ref.shape[0]
    tile, chunk = MOE_TILE, MOE_CHUNK
    lane = lax.broadcasted_iota(jnp.int32, (nb, LANES), 1)

    @pl.when(e == 0)
    def _():
        xg_ref[xg_ref.shape[0] - tile:, :] = jnp.zeros((tile, xg_ref.shape[1]), BF16)
        comb = comb_ref[...]
        assigned = comb != 0.0
        a_f = jnp.where(assigned, 1.0, 0.0)
        rank = jnp.dot(_strict_tri(nb, True), a_f.astype(BF16), preferred_element_type=F32)
        cnt = rank[nb - 1:nb, :] + a_f[nb - 1:nb, :]
        ntile = jnp.floor((cnt + float(tile - 1)) * (1.0 / tile))
        first = jnp.dot(jnp.broadcast_to(ntile, (SUBLANES, LANES)).astype(BF16), _strict_tri(LANES, False),
                        preferred_element_type=F32)[0:1, :]
        off_ref[0:1, :] = first
        off_ref[1:2, :] = ntile
        pos = first * float(tile) + rank
        pos1 = jnp.min(jnp.where(assigned, pos, 1e9), axis=1, keepdims=True)
        pos2 = jnp.max(jnp.where(assigned, pos, -1.0), axis=1, keepdims=True)
        pos2 = jnp.where(pos2 == pos1, -1.0, pos2)
        cw1 = jnp.sum(jnp.where(assigned & (pos == pos1), comb, 0.0), axis=1, keepdims=True)
        cw2 = jnp.sum(jnp.where(assigned & (pos == pos2), comb, 0.0), axis=1, keepdims=True)
        info = jnp.where(lane == 0, pos1, jnp.where(lane == 1, pos2, jnp.where(lane == 2, cw1,
                         jnp.where(lane == 3, cw2, 0.0))))
        col_ref[...] = info
        row_ref[...] = info.T

        n_chunks = (_lane_scalar_i32(first, N_EXPERTS) * tile + (chunk - 1)) // chunk
        p1 = row_ref[0:1, :].astype(jnp.int32)
        p2 = row_ref[1:2, :].astype(jnp.int32)
        sub = lax.broadcasted_iota(jnp.int32, (chunk, nb), 0)

        def gather(c, carry):
            p = sub + c * chunk
            sel = jnp.where((p == p1) | (p == p2), 1.0, 0.0).astype(BF16)
            r0 = pl.multiple_of(c * chunk, chunk)
            xg_ref[pl.ds(r0, chunk), :] = jnp.dot(sel, h2_ref[...], preferred_element_type=F32).astype(BF16)
            return carry

        lax.fori_loop(0, n_chunks, gather, 0)

    total = _lane_scalar_i32(off_ref[0:1, :], N_EXPERTS)
    padded_total = ((total * tile + (chunk - 1)) // chunk) * (chunk // tile)
    spare_tile = xg_ref.shape[0] // tile - 1
    firsts, counts = [], []
    for x in range(MOE_EPS):
        ex = e * MOE_EPS + x
        first_x = _lane_scalar_i32(off_ref[0:1, :], ex)
        count_x = _lane_scalar_i32(off_ref[1:2, :], ex)
        if x == MOE_EPS - 1:
            count_x = jnp.where(e == n_steps - 1, padded_total - first_x, count_x)
        firsts.append(first_x)
        counts.append(count_x)
    most = counts[0]
    for x in range(1, MOE_EPS):
        most = jnp.maximum(most, counts[x])

    def expert_tiles(j, carry):
        for x in range(MOE_EPS):
            t_idx = jnp.where(counts[x] > 0, firsts[x] + jnp.minimum(j, counts[x] - 1), spare_tile)
            r0 = pl.multiple_of(t_idx * tile, tile)
            ab = jnp.dot(xg_ref[pl.ds(r0, tile), :], wgu_ref[x], preferred_element_type=F32)
            a = ab[:, :EXPERT_FF]
            hid = (a * jax.nn.sigmoid(a)) * ab[:, EXPERT_FF:]
            y_ref[pl.ds(r0, tile), :] = jnp.dot(hid.astype(BF16), wd_ref[x],
                                                preferred_element_type=F32).astype(BF16)
        return carry

    lax.fori_loop(0, most, expert_tiles, 0)

    @pl.when(e == n_steps - 1)
    def _():
        p1 = col_ref[:, 0:1].astype(jnp.int32)
        p2 = col_ref[:, 1:2].astype(jnp.int32)
        cw1 = col_ref[:, 2:3]
        cw2 = col_ref[:, 3:4]
        gate = mod_ref[5:6, :]
        lane_c = lax.broadcasted_iota(jnp.int32, (nb, chunk), 1)
        o_ref[...] = x1_ref[...]

        def scatter(c, carry):
            p = lane_c + c * chunk
            w = (jnp.where(p == p1, cw1, 0.0) + jnp.where(p == p2, cw2, 0.0)).astype(BF16)
            r0 = pl.multiple_of(c * chunk, chunk)
            o_ref[...] += gate * jnp.dot(w, y_ref[pl.ds(r0, chunk), :], preferred_element_type=F32)
            return carry

        lax.fori_loop(0, padded_total // (chunk // tile), scatter, 0)


def _moe_call(h2, comb, x1, mod, wgu, wd):
    bsz, seq, d = x1.shape
    nb = MOE_TM
    n_blk = bsz * seq // nb
    rows_max = 2 * nb + N_EXPERTS * MOE_TILE
    rows_max = -(-rows_max // MOE_CHUNK) * MOE_CHUNK + MOE_TILE
    tok = lambda j, e: (j, 0)
    out = pl.pallas_call(
        _moe_kernel,
        out_shape=jax.ShapeDtypeStruct((bsz * seq, d), F32),
        grid=(n_blk, N_EXPERTS // MOE_EPS),
        in_specs=[
            pl.BlockSpec((nb, d), tok),
            pl.BlockSpec((nb, LANES), tok),
            pl.BlockSpec((nb, d), tok),
            pl.BlockSpec((None, 6, d), lambda j, e: ((j * nb) // seq, 0, 0)),
            pl.BlockSpec((MOE_EPS, d, 2 * EXPERT_FF), lambda j, e: (e, 0, 0)),
            pl.BlockSpec((MOE_EPS, EXPERT_FF, d), lambda j, e: (e, 0, 0)),
        ],
        out_specs=pl.BlockSpec((nb, d), tok),
        scratch_shapes=[
            pltpu.VMEM((rows_max, d), BF16),
            pltpu.VMEM((rows_max, d), BF16),
            pltpu.VMEM((nb, LANES), F32),
            pltpu.VMEM((LANES, nb), F32),
            pltpu.VMEM((SUBLANES, LANES), F32),
        ],
        compiler_params=pltpu.CompilerParams(dimension_semantics=("arbitrary", "arbitrary"),
                                             vmem_limit_bytes=VMEM_LIMIT_BYTES),
        name="moe_experts",
    )(h2.reshape(bsz * seq, d), comb.reshape(bsz * seq, LANES), x1.reshape(bsz * seq, d), mod, wgu, wd)
    return out.reshape(bsz, seq, d)


def _layer(x, mod, rel_bias, norm1, w_in, q_norm, k_norm, conv_w, attn_out_norm, conv_out_norm, w_out,
           norm2, w_group_router, b_group_router, w_expert_router, b_expert_router, w_gate, w_up, w_down):
    bsz, seq, d = x.shape
    aw = ATTN_WIDTH
    topk = min(TOPK_MAX, seq // 4)

    offs = np.cumsum([0, aw, aw, aw, IDX_HEADS * IDX_DIM, IDX_DIM, IDX_HEADS, CONV_WIDTH, CONV_WIDTH, CONV_WIDTH])
    col = lambda n: w_in[:, int(offs[n]):int(offs[n + 1])]
    wm = jnp.concatenate([col(0), col(1), col(3), col(6), col(7), col(8)], axis=1).astype(BF16)
    wvt = col(2).T.astype(BF16)
    wki = jnp.concatenate([col(4), col(4)], axis=1).astype(BF16)
    wwit = col(5).T.astype(BF16)
    qg = (jnp.tile(q_norm, ATTN_HEADS) * ((HEAD_DIM ** -0.5) * LOG2E))[None, :]
    kg = jnp.tile(k_norm, ATTN_HEADS)[None, :]
    grp = np.arange(aw) // CONV_GROUP_DIM
    gmat = jnp.asarray((grp[:, None] == grp[None, :]).astype(np.float32) / CONV_GROUP_DIM, dtype=BF16)

    q, k, vt, qi, ki, wit, cn = _pre_call(
        x, mod, norm1[None, :], wm, wvt, wki, wwit, qg, kg, conv_w, conv_out_norm.reshape(1, -1), gmat)

    bounds = jnp.asarray(_bucket_boundaries())
    an = _attn_call(rel_bias, bounds, q, qi, wit, k, ki, vt, attn_out_norm.reshape(1, -1), topk)

    wr = jnp.concatenate([w_expert_router, w_group_router,
                          jnp.zeros((d, LANES - N_EXPERTS - N_GROUPS), F32)], axis=1).astype(BF16)
    br = jnp.concatenate([b_expert_router, b_group_router,
                          jnp.zeros((LANES - N_EXPERTS - N_GROUPS,), F32)])[None, :]
    x1, h2, comb = _post_call(an, cn, x, mod, norm2[None, :], w_out[:aw].astype(BF16), w_out[aw:].astype(BF16),
                              wr, br)

    wgu = jnp.concatenate([w_gate, w_up], axis=-1).astype(BF16)
    return _moe_call(h2, comb, x1, mod, wgu, w_down.astype(BF16))


def kernel(x, c, rel_bias, w_ada, b_ada, norm1, w_in, q_norm, k_norm, conv_w, attn_out_norm, conv_out_norm,
           w_out, norm2, w_group_router, b_group_router, w_expert_router, b_expert_router, w_gate, w_up,
           w_down):
    bsz, seq, d = x.shape
    assert d == D_MODEL and seq % max(PRE_TM, POST_TM, MOE_TM) == 0 and ATT_TQ == ATT_TK
    depth = w_ada.shape[0]
    for l in range(depth):
        mod = _mod_call(c, w_ada[l], b_ada[l][None, :]).reshape(bsz, 6, d)
        x = _layer(x, mod, rel_bias, norm1[l], w_in[l], q_norm[l], k_norm[l], conv_w[l], attn_out_norm[l],
                   conv_out_norm[l], w_out[l], norm2[l], w_group_router[l], b_group_router[l],
                   w_expert_router[l], b_expert_router[l], w_gate[l], w_up[l], w_down[l])
    return x
```

```python
import functools
import math

import jax
import jax.numpy as jnp
import numpy as np
from jax import lax
from jax.experimental import pallas as pl
from jax.experimental.pallas import tpu as pltpu

F32 = jnp.float32
BF16 = jnp.bfloat16

D_MODEL = 1024
HEAD_DIM = 64
ATTN_HEADS = 8
ATTN_WIDTH = ATTN_HEADS * HEAD_DIM
CONV_WIDTH = D_MODEL - ATTN_WIDTH
CONV_GROUP_DIM = 64
CONV_K = 3
IDX_HEADS = 8
IDX_DIM = 64
TOPK_MAX = 256
IDX_SCALE = (IDX_DIM ** -0.5) * (IDX_HEADS ** -0.5)
N_BUCKETS = 32
MAX_DISTANCE = 128
N_GROUPS = 4
EXPERTS_PER_GROUP = 8
N_EXPERTS = N_GROUPS * EXPERTS_PER_GROUP
EXPERT_FF = 256
EPS = 1e-6
LOG2E = 1.4426950408889634
NEG_BIG = -1e30
COUNT_ACCS = 4
BISECT_GROUP = 4
BISECT_VALUE_STEPS = 8
BISECT_MAX_STEPS = 64

LANES = 128
SUBLANES = 8
BF16_SUBLANES = 16
V_SLAB = HEAD_DIM + BF16_SUBLANES
VMEM_LIMIT_BYTES = 56 * 1024 * 1024

PRE_TM = 512
ATT_TQ = 256
ATT_TK = 256
POST_TM = 512
MOE_TM = 1024
MOE_TILE = 64
MOE_CHUNK = 512
MOE_EPS = 4
MOD_TN = 1536

_NT_DIMS = (((1,), (1,)), ((), ()))


def _tree_sum(parts):
    while len(parts) > 1:
        nxt = [parts[j] + parts[j + 1] for j in range(0, len(parts) - 1, 2)]
        if len(parts) % 2:
            nxt.append(parts[-1])
        parts = nxt
    return parts[0]


def _bucket_boundaries():
    max_exact = N_BUCKETS // 2
    d = np.arange(0, 4 * MAX_DISTANCE, dtype=np.int64)
    nf = np.maximum(d, 1).astype(np.float32)
    large = max_exact + (np.log(nf / np.float32(max_exact)) / np.float32(math.log(MAX_DISTANCE / max_exact))
                         * np.float32(N_BUCKETS - max_exact)).astype(np.int32)
    large = np.minimum(large, N_BUCKETS - 1)
    bucket = np.where(d < max_exact, d, large)
    assert np.all(np.diff(bucket) >= 0) and bucket[-1] == N_BUCKETS - 1
    bounds = [int(np.argmax(bucket >= j)) for j in range(1, N_BUCKETS)]
    return np.asarray([0] + bounds, dtype=np.int32)


def _mod_kernel(c_ref, w_ref, b_ref, o_ref):
    c = c_ref[...]
    act = c * jax.nn.sigmoid(c)
    o_ref[...] = jnp.dot(act, w_ref[...], preferred_element_type=F32,
                         precision=lax.Precision.HIGHEST) + b_ref[...]


def _mod_call(c, w_ada, b_ada):
    bsz, d = c.shape
    n = w_ada.shape[1]
    return pl.pallas_call(
        _mod_kernel,
        out_shape=jax.ShapeDtypeStruct((bsz, n), F32),
        grid=(n // MOD_TN,),
        in_specs=[pl.BlockSpec((bsz, d), lambda j: (0, 0)),
                  pl.BlockSpec((d, MOD_TN), lambda j: (0, j)),
                  pl.BlockSpec((1, MOD_TN), lambda j: (0, j))],
        out_specs=pl.BlockSpec((bsz, MOD_TN), lambda j: (0, j)),
        compiler_params=pltpu.CompilerParams(dimension_semantics=("arbitrary",),
                                             vmem_limit_bytes=VMEM_LIMIT_BYTES),
        name="adaln_mod",
    )(c, w_ada, b_ada)


def _group_rms(y, g_ref):
    ms = jnp.dot((y * y).astype(BF16), g_ref[...], preferred_element_type=F32)
    return y * lax.rsqrt(ms + EPS)


def _pre_kernel(x_ref, mod_ref, n1_ref, wm_ref, wvt_ref, wki_ref, wwit_ref, qg_ref, kg_ref,
                cw_ref, cg_ref, g_ref,
                q_ref, k_ref, vt_ref, qi_ref, ki_ref, wit_ref, cn_ref, carry_ref):
    j = pl.program_id(1)
    tm = x_ref.shape[0]
    aw = ATTN_WIDTH

    x = x_ref[...]
    ms = jnp.mean(x * x, axis=-1, keepdims=True)
    y = x * lax.rsqrt(ms + EPS) * n1_ref[...]
    h = y * (1.0 + mod_ref[1:2, :]) + mod_ref[0:1, :]
    hb = h.astype(BF16)

    def proj(lo):
        return jnp.dot(hb, wm_ref[:, lo:lo + aw], preferred_element_type=F32)

    q = _group_rms(proj(0), g_ref) * qg_ref[...]
    q_ref[...] = q.astype(BF16)
    k = _group_rms(proj(aw), g_ref) * kg_ref[...]
    k_ref[...] = k.astype(BF16)

    vt = lax.dot_general(wvt_ref[...], hb, _NT_DIMS, preferred_element_type=F32).astype(BF16)
    ones = jnp.ones((BF16_SUBLANES, ATT_TK), BF16)
    for cc in range(tm // ATT_TK):
        for hh in range(ATTN_HEADS):
            vt_ref[cc, hh * V_SLAB:hh * V_SLAB + HEAD_DIM, :] = (
                vt[hh * HEAD_DIM:(hh + 1) * HEAD_DIM, cc * ATT_TK:(cc + 1) * ATT_TK])
            vt_ref[cc, hh * V_SLAB + HEAD_DIM:(hh + 1) * V_SLAB, :] = ones

    qi_ref[...] = proj(2 * aw).astype(BF16)
    ki_ref[...] = jnp.dot(hb, wki_ref[...], preferred_element_type=F32).astype(BF16)
    wit_ref[...] = lax.dot_general(wwit_ref[...], hb, _NT_DIMS, preferred_element_type=F32) * IDX_SCALE

    gate_b = proj(3 * aw)
    z = proj(4 * aw) * proj(5 * aw)

    @pl.when(j == 0)
    def _():
        carry_ref[...] = jnp.zeros_like(carry_ref)

    prev = carry_ref[...]
    row = lax.broadcasted_iota(jnp.int32, z.shape, 0)
    z1 = jnp.where(row == 0, prev[SUBLANES - 1:SUBLANES, :], pltpu.roll(z, 1, 0))
    z2 = pltpu.roll(z, 2, 0)
    z2 = jnp.where(row == 0, prev[SUBLANES - 2:SUBLANES - 1, :], z2)
    z2 = jnp.where(row == 1, prev[SUBLANES - 1:SUBLANES, :], z2)
    carry_ref[...] = z[tm - SUBLANES:, :]
    conv = cw_ref[2:3, :] * z + cw_ref[1:2, :] * z1 + cw_ref[0:1, :] * z2
    yc = gate_b * conv
    cn_ref[...] = (_group_rms(yc, g_ref) * cg_ref[...]).astype(BF16)


def _pre_call(x, mod, n1, wm, wvt, wki, wwit, qg, kg, cw, cg, gmat):
    bsz, seq, d = x.shape
    tm = PRE_TM
    nck = tm // ATT_TK
    aw = ATTN_WIDTH
    const = lambda b, j: (0, 0)
    tok = lambda b, j: (b, j, 0)
    out_shape = (
        jax.ShapeDtypeStruct((bsz, seq, aw), BF16),
        jax.ShapeDtypeStruct((bsz, seq, aw), BF16),
        jax.ShapeDtypeStruct((bsz, seq // ATT_TK, ATTN_HEADS * V_SLAB, ATT_TK), BF16),
        jax.ShapeDtypeStruct((bsz, seq, aw), BF16),
        jax.ShapeDtypeStruct((bsz, seq, LANES), BF16),
        jax.ShapeDtypeStruct((bsz, IDX_HEADS, seq), F32),
        jax.ShapeDtypeStruct((bsz, seq, CONV_WIDTH), BF16),
    )
    out_specs = (
        pl.BlockSpec((None, tm, aw), tok),
        pl.BlockSpec((None, tm, aw), tok),
        pl.BlockSpec((None, nck, ATTN_HEADS * V_SLAB, ATT_TK), lambda b, j: (b, j, 0, 0)),
        pl.BlockSpec((None, tm, aw), tok),
        pl.BlockSpec((None, tm, LANES), tok),
        pl.BlockSpec((None, IDX_HEADS, tm), lambda b, j: (b, 0, j)),
        pl.BlockSpec((None, tm, CONV_WIDTH), tok),
    )
    in_specs = [
        pl.BlockSpec((None, tm, d), tok),
        pl.BlockSpec((None, 6, d), lambda b, j: (b, 0, 0)),
        pl.BlockSpec(n1.shape, const),
        pl.BlockSpec(wm.shape, const),
        pl.BlockSpec(wvt.shape, const),
        pl.BlockSpec(wki.shape, const),
        pl.BlockSpec(wwit.shape, const),
        pl.BlockSpec(qg.shape, const),
        pl.BlockSpec(kg.shape, const),
        pl.BlockSpec(cw.shape, const),
        pl.BlockSpec(cg.shape, const),
        pl.BlockSpec(gmat.shape, const),
    ]
    return pl.pallas_call(
        _pre_kernel,
        out_shape=out_shape,
        grid=(bsz, seq // tm),
        in_specs=in_specs,
        out_specs=out_specs,
        scratch_shapes=[pltpu.VMEM((SUBLANES, CONV_WIDTH), F32)],
        compiler_params=pltpu.CompilerParams(dimension_semantics=("arbitrary", "arbitrary"),
                                             vmem_limit_bytes=VMEM_LIMIT_BYTES),
        name="pre_proj",
    )(x, mod, n1, wm, wvt, wki, wwit, qg, kg, cw, cg, gmat)


def _attn_kernel(rb_ref, bnd_ref, q_ref, qi_ref, wit_ref, k_ref, ki_ref, vt_ref, og_ref,
                 o_ref,
                 s_ref, bias_ref, qpad_ref, qipad_ref, lg_ref, acc_ref, out_ref, *, topk):
    b = pl.program_id(0)
    i = pl.program_id(1)
    tq, tk = ATT_TQ, ATT_TK
    nh, hd = ATTN_HEADS, HEAD_DIM

    t_loc = lax.broadcasted_iota(jnp.int32, (tk, tq), 1)
    s_loc = lax.broadcasted_iota(jnp.int32, (tk, tq), 0)

    @pl.when((b == 0) & (i == 0))
    def _():
        for idx in range(2):
            dist = t_loc - s_loc + idx * tq
            for h in range(nh):
                bias_ref[idx, h] = jnp.full((tk, tq), (rb_ref[0, h] - rb_ref[N_BUCKETS - 1, h]) * LOG2E, F32)

            def fill(jb, carry):
                reached = dist >= bnd_ref[jb]
                for h in range(nh):
                    val = (rb_ref[jb, h] - rb_ref[N_BUCKETS - 1, h]) * LOG2E
                    bias_ref[idx, h] = jnp.where(reached, val, bias_ref[idx, h])
                return carry

            lax.fori_loop(1, N_BUCKETS, fill, 0)

    lane = lax.broadcasted_iota(jnp.int32, (tq, LANES), 1)
    for h in range(nh):
        pair = slice((h // 2) * LANES, (h // 2 + 1) * LANES)
        keep = (lane // hd) == (h % 2)
        qpad_ref[h] = jnp.where(keep, q_ref[:, pair], jnp.zeros((), BF16))
        qipad_ref[h] = jnp.where(keep, qi_ref[:, pair], jnp.zeros((), BF16))

    def score_chunk(c):
        kic = ki_ref[pl.ds(pl.multiple_of(c * tk, tk), tk), :]
        acc = jnp.zeros((tk, tq), F32)
        for h in range(nh):
            e = lax.dot_general(kic, qipad_ref[h], _NT_DIMS, preferred_element_type=F32)
            acc = acc + wit_ref[h:h + 1, :] * jnp.maximum(e, 0.0)
        return acc

    def a_body(c, carry):
        rmin, rmax = carry
        sc = score_chunk(c)
        s_ref[c] = sc
        return (jnp.minimum(rmin, jnp.min(sc, axis=0, keepdims=True)),
                jnp.maximum(rmax, jnp.max(sc, axis=0, keepdims=True)))

    rmin0 = jnp.full((1, tq), jnp.inf, F32)
    rmax0 = jnp.full((1, tq), -jnp.inf, F32)
    rmin, rmax = lax.fori_loop(0, i, a_body, (rmin0, rmax0))
    sc = score_chunk(i)
    causal = s_loc <= t_loc
    s_ref[i] = jnp.where(causal, sc, -jnp.inf)
    rmin = jnp.minimum(rmin, jnp.min(jnp.where(causal, sc, jnp.inf), axis=0, keepdims=True))
    rmax = jnp.maximum(rmax, jnp.max(jnp.where(causal, sc, -jnp.inf), axis=0, keepdims=True))

    def count_ge(thr):
        def body(c, accs):
            hit = s_ref[c] >= thr
            accs = list(accs)
            for r in range(tk // SUBLANES):
                a = accs[r % COUNT_ACCS]
                accs[r % COUNT_ACCS] = jnp.where(hit[r * SUBLANES:(r + 1) * SUBLANES], a + 1.0, a)
            return tuple(accs)
        accs = lax.fori_loop(0, i + 1, body,
                             tuple(jnp.zeros((SUBLANES, tq), F32) for _ in range(COUNT_ACCS)))
        return jnp.sum(_tree_sum(list(accs)), axis=0, keepdims=True)

    def order_key(v):
        bits = pltpu.bitcast(v, jnp.int32)
        return jnp.where(bits < 0, bits ^ jnp.int32(0x7FFFFFFF), bits)

    def from_order_key(key):
        return pltpu.bitcast(jnp.where(key < 0, key ^ jnp.int32(0x7FFFFFFF), key), F32)

    t_glob = (i * tq + lax.broadcasted_iota(jnp.int32, (1, tq), 1)).astype(F32)
    n_causal = t_glob + 1.0
    kf = jnp.minimum(float(topk), n_causal)
    c_max = count_ge(rmax)
    all_sel = n_causal <= kf
    max_ge = c_max >= kf
    active0 = jnp.where(all_sel | max_ge, 0.0, 1.0)
    thr0 = jnp.where(all_sel, rmin, rmax)
    tie0 = jnp.where(jnp.logical_not(all_sel) & (c_max > kf), 1.0, 0.0)
    hif0 = jnp.full((1, tq), jnp.inf, F32)
    need0 = kf

    def b_cond(st):
        return (jnp.max(st[0]) > 0.0) & (st[8] <= BISECT_MAX_STEPS)

    def b_body(st):
        active, lo, hi, fhi, thr, tie, hif, need, step = st
        lo_key = order_key(lo)
        hi_key = order_key(hi)
        mid_key = (lo_key >> 1) + (hi_key >> 1) + (lo_key & hi_key & 1)
        mid_val = lo + (hi - lo) * 0.5
        use_val = (step < BISECT_VALUE_STEPS) & (mid_val > lo) & (mid_val < hi)
        mid = jnp.where(use_val, mid_val, from_order_key(mid_key))
        collapsed = (mid_key == lo_key) | (step >= BISECT_MAX_STEPS)
        cm = count_ge(mid)
        act = active > 0.0
        live = act & jnp.logical_not(collapsed)
        found = live & (cm == kf)
        go_up = live & (cm > kf)
        go_dn = live & (cm < kf)
        ends_tie = act & collapsed
        thr = jnp.where(found, mid, jnp.where(ends_tie, lo, thr))
        tie = jnp.where(ends_tie, 1.0, tie)
        hif = jnp.where(ends_tie, hi, hif)
        need = jnp.where(ends_tie, kf - fhi, need)
        lo = jnp.where(go_up, mid, lo)
        fhi = jnp.where(go_dn, cm, fhi)
        hi = jnp.where(go_dn, mid, hi)
        active = jnp.where(found | ends_tie, 0.0, active)
        return active, lo, hi, fhi, thr, tie, hif, need, step + 1

    def b_group(st):
        for _ in range(BISECT_GROUP):
            st = b_body(st)
        return st

    _, _, _, _, thr, tie, hif, need, _ = lax.while_loop(
        b_cond, b_group, (active0, rmin, rmax, c_max, thr0, tie0, hif0, need0, jnp.int32(0)))

    @pl.when(jnp.max(tie) > 0.0)
    def _():
        tri = jnp.where(lax.broadcasted_iota(jnp.int32, (tk, tk), 1)
                        <= lax.broadcasted_iota(jnp.int32, (tk, tk), 0), 1.0, 0.0).astype(BF16)

        def body(c, seen):
            sc_c = s_ref[c]
            tied = (sc_c >= thr) & (sc_c < hif) & (tie > 0.0)
            rank = jnp.dot(tri, jnp.where(tied, 1.0, 0.0).astype(BF16), preferred_element_type=F32) + seen
            s_ref[c] = jnp.where(tied & (rank > need), -jnp.inf, sc_c)
            return rank[tk - 1:tk, :]

        lax.fori_loop(0, i + 1, body, jnp.zeros((1, tq), F32))

    acc_ref[...] = jnp.zeros(acc_ref.shape, F32)

    def store_logits(c, slot, bias_idx):
        masked = jnp.where(s_ref[c] >= thr, 0.0, NEG_BIG)
        row0 = pl.multiple_of(c * tk, tk)
        for h in range(nh):
            kc = k_ref[pl.ds(row0, tk), (h // 2) * LANES:(h // 2 + 1) * LANES]
            lt = lax.dot_general(kc, qpad_ref[h], _NT_DIMS, preferred_element_type=F32) + masked
            if bias_idx is not None:
                lt = lt + bias_ref[bias_idx, h]
            lg_ref[slot, h] = lt

    def softmax_pv(c, slot, m_all):
        m_out = []
        for h in range(nh):
            m_old = m_all[h]
            m_new = jnp.maximum(m_old, jnp.max(lg_ref[slot, h], axis=0, keepdims=True))
            p = jnp.exp2(lg_ref[slot, h] - m_new).astype(BF16)
            alpha = jnp.exp2(m_old - m_new)
            pv = jnp.dot(vt_ref[c, h * V_SLAB:(h + 1) * V_SLAB, :], p, preferred_element_type=F32)
            acc_ref[h] = alpha * acc_ref[h] + pv
            m_out.append(m_new)
        return tuple(m_out)

    def near_step(m_all):
        store_logits(i - 1, 1, 1)
        return softmax_pv(i, 0, m_all)

    def far_step(j, parity, m_all):
        c = i - 2 - j
        store_logits(c, parity, None)
        return softmax_pv(c + 1, 1 - parity, m_all)

    def far_pair(jj, m_all):
        return far_step(2 * jj + 1, 1, far_step(2 * jj, 0, m_all))

    n_far = jnp.maximum(i - 1, 0)
    m_all = tuple(jnp.full((1, tq), NEG_BIG, F32) for _ in range(nh))
    store_logits(i, 0, 0)
    m_all = lax.cond(i >= 1, near_step, lambda m: m, m_all)
    m_all = lax.fori_loop(0, n_far // 2, far_pair, m_all)
    m_all = lax.cond((n_far & 1) == 1, lambda m: far_step(n_far - 1, 0, m), lambda m: m, m_all)
    lax.cond((i & 1) == 0, lambda m: softmax_pv(0, 0, m), lambda m: softmax_pv(0, 1, m), m_all)

    for h in range(nh):
        o = acc_ref[h, :hd, :] / acc_ref[h, hd:hd + 1, :]
        ms = jnp.mean(o * o, axis=0, keepdims=True)
        out_ref[h * hd:(h + 1) * hd, :] = o * lax.rsqrt(ms + EPS)
    o_ref[...] = (out_ref[...].T * og_ref[...]).astype(BF16)


def _attn_call(rel_bias, bounds, q, qi, wit, k, ki, vt, og, topk):
    bsz, seq, aw = q.shape
    tq, tk = ATT_TQ, ATT_TK
    nck = seq // tk
    blk_q = lambda b, i: (b, i, 0)
    whole = lambda b, i: (b, 0, 0)
    smem = pl.BlockSpec(memory_space=pltpu.SMEM)
    return pl.pallas_call(
        functools.partial(_attn_kernel, topk=topk),
        out_shape=jax.ShapeDtypeStruct((bsz, seq, aw), BF16),
        grid=(bsz, seq // tq),
        in_specs=[
            smem, smem,
            pl.BlockSpec((None, tq, aw), blk_q),
            pl.BlockSpec((None, tq, aw), blk_q),
            pl.BlockSpec((None, IDX_HEADS, tq), lambda b, i: (b, 0, i)),
            pl.BlockSpec((None, seq, aw), whole),
            pl.BlockSpec((None, seq, LANES), whole),
            pl.BlockSpec((None, nck, ATTN_HEADS * V_SLAB, tk), lambda b, i: (b, 0, 0, 0)),
            pl.BlockSpec(og.shape, lambda b, i: (0, 0)),
        ],
        out_specs=pl.BlockSpec((None, tq, aw), blk_q),
        scratch_shapes=[
            pltpu.VMEM((nck, tk, tq), F32),
            pltpu.VMEM((2, ATTN_HEADS, tk, tq), F32),
            pltpu.VMEM((ATTN_HEADS, tq, LANES), BF16),
            pltpu.VMEM((IDX_HEADS, tq, LANES), BF16),
            pltpu.VMEM((2, ATTN_HEADS, tk, tq), F32),
            pltpu.VMEM((ATTN_HEADS, V_SLAB, tq), F32),
            pltpu.VMEM((aw, tq), F32),
        ],
        compiler_params=pltpu.CompilerParams(dimension_semantics=("arbitrary", "arbitrary"),
                                             vmem_limit_bytes=VMEM_LIMIT_BYTES),
        name="dsa_attention",
    )(rel_bias, bounds, q, qi, wit, k, ki, vt, og)


def _post_kernel(an_ref, cn_ref, x_ref, mod_ref, n2_ref, woa_ref, woc_ref, wr_ref, br_ref,
                 x1_ref, h2_ref, comb_ref):
    mix = (jnp.dot(an_ref[...], woa_ref[...], preferred_element_type=F32)
           + jnp.dot(cn_ref[...], woc_ref[...], preferred_element_type=F32))
    x1 = x_ref[...] + mod_ref[2:3, :] * mix
    x1_ref[...] = x1
    ms = jnp.mean(x1 * x1, axis=-1, keepdims=True)
    h2 = x1 * lax.rsqrt(ms + EPS) * n2_ref[...] * (1.0 + mod_ref[4:5, :]) + mod_ref[3:4, :]
    h2b = h2.astype(BF16)
    h2_ref[...] = h2b

    logits = jnp.dot(h2b, wr_ref[...], preferred_element_type=F32) + br_ref[...]
    lane = lax.broadcasted_iota(jnp.int32, logits.shape, 1)
    lane_f = lane.astype(F32)
    far = float(LANES)
    is_g = (lane >= N_EXPERTS) & (lane < N_EXPERTS + N_GROUPS)
    gl = jnp.where(is_g, logits, -jnp.inf)
    gmax = jnp.max(gl, axis=-1, keepdims=True)
    g_sel = jnp.min(jnp.where(is_g & (gl == gmax), lane_f, far), axis=-1, keepdims=True) - float(N_EXPERTS)
    p_g = 1.0 / jnp.sum(jnp.exp(gl - gmax), axis=-1, keepdims=True)

    in_grp = (lane < N_EXPERTS) & ((lane // EXPERTS_PER_GROUP).astype(F32) == g_sel)
    e1 = jnp.where(in_grp, logits, -jnp.inf)
    l1 = jnp.max(e1, axis=-1, keepdims=True)
    i1 = jnp.min(jnp.where(in_grp & (e1 == l1), lane_f, far), axis=-1, keepdims=True)
    rest = in_grp & (lane_f != i1)
    e2 = jnp.where(rest, logits, -jnp.inf)
    l2 = jnp.max(e2, axis=-1, keepdims=True)
    i2 = jnp.min(jnp.where(rest & (e2 == l2), lane_f, far), axis=-1, keepdims=True)
    r = jnp.exp(l2 - l1)
    w1 = 1.0 / (1.0 + r)
    w2 = r / (1.0 + r)
    comb_ref[...] = jnp.where(lane_f == i1, p_g * w1, 0.0) + jnp.where(lane_f == i2, p_g * w2, 0.0)


def _post_call(an, cn, x, mod, n2, woa, woc, wr, br):
    bsz, seq, d = x.shape
    tm = POST_TM
    tok = lambda b, j: (b, j, 0)
    const = lambda b, j: (0, 0)
    return pl.pallas_call(
        _post_kernel,
        out_shape=(jax.ShapeDtypeStruct((bsz, seq, d), F32),
                   jax.ShapeDtypeStruct((bsz, seq, d), BF16),
                   jax.ShapeDtypeStruct((bsz, seq, LANES), F32)),
        grid=(bsz, seq // tm),
        in_specs=[
            pl.BlockSpec((None, tm, ATTN_WIDTH), tok),
            pl.BlockSpec((None, tm, CONV_WIDTH), tok),
            pl.BlockSpec((None, tm, d), tok),
            pl.BlockSpec((None, 6, d), lambda b, j: (b, 0, 0)),
            pl.BlockSpec(n2.shape, const),
            pl.BlockSpec(woa.shape, const),
            pl.BlockSpec(woc.shape, const),
            pl.BlockSpec(wr.shape, const),
            pl.BlockSpec(br.shape, const),
        ],
        out_specs=(pl.BlockSpec((None, tm, d), tok),
                   pl.BlockSpec((None, tm, d), tok),
                   pl.BlockSpec((None, tm, LANES), tok)),
        compiler_params=pltpu.CompilerParams(dimension_semantics=("arbitrary", "arbitrary"),
                                             vmem_limit_bytes=VMEM_LIMIT_BYTES),
        name="post_router",
    )(an, cn, x, mod, n2, woa, woc, wr, br)


def _lane_scalar_i32(row, idx):
    lane = lax.broadcasted_iota(jnp.int32, row.shape, 1)
    picked = jnp.sum(jnp.where(lane == idx, row, 0.0), axis=1, keepdims=True)
    return picked.astype(jnp.int32)[0, 0]


def _strict_tri(n, lower):
    r = lax.broadcasted_iota(jnp.int32, (n, n), 0)
    c = lax.broadcasted_iota(jnp.int32, (n, n), 1)
    return jnp.where((c < r) if lower else (r < c), 1.0, 0.0).astype(BF16)


def _moe_kernel(h2_ref, comb_ref, x1_ref, mod_ref, wgu_ref, wd_ref,
                o_ref,
                xg_ref, y_ref, col_ref, row_ref, off_ref, hid_ref):
    e = pl.program_id(1)
    n_steps = N_EXPERTS // MOE_EPS
    nb = h2_ref.shape[0]
    tile, chunk = MOE_TILE, MOE_CHUNK
    lane = lax.broadcasted_iota(jnp.int32, (nb, LANES), 1)

    @pl.when(e == 0)
    def _():
        xg_ref[xg_ref.shape[0] - tile:, :] = jnp.zeros((tile, xg_ref.shape[1]), BF16)
        comb = comb_ref[...]
        assigned = comb != 0.0
        a_f = jnp.where(assigned, 1.0, 0.0)
        rank = jnp.dot(_strict_tri(nb, True), a_f.astype(BF16), preferred_element_type=F32)
        cnt = rank[nb - 1:nb, :] + a_f[nb - 1:nb, :]
        ntile = jnp.floor((cnt + float(tile - 1)) * (1.0 / tile))
        first = jnp.dot(jnp.broadcast_to(ntile, (SUBLANES, LANES)).astype(BF16), _strict_tri(LANES, False),
                        preferred_element_type=F32)[0:1, :]
        off_ref[0:1, :] = first
        off_ref[1:2, :] = ntile
        pos = first * float(tile) + rank
        pos1 = jnp.min(jnp.where(assigned, pos, 1e9), axis=1, keepdims=True)
        pos2 = jnp.max(jnp.where(assigned, pos, -1.0), axis=1, keepdims=True)
        pos2 = jnp.where(pos2 == pos1, -1.0, pos2)
        cw1 = jnp.sum(jnp.where(assigned & (pos == pos1), comb, 0.0), axis=1, keepdims=True)
        cw2 = jnp.sum(jnp.where(assigned & (pos == pos2), comb, 0.0), axis=1, keepdims=True)
        info = jnp.where(lane == 0, pos1, jnp.where(lane == 1, pos2, jnp.where(lane == 2, cw1,
                         jnp.where(lane == 3, cw2, 0.0))))
        col_ref[...] = info
        row_ref[...] = info.T

        n_chunks = (_lane_scalar_i32(first, N_EXPERTS) * tile + (chunk - 1)) // chunk
        p1 = row_ref[0:1, :].astype(jnp.int32)
        p2 = row_ref[1:2, :].astype(jnp.int32)
        sub = lax.broadcasted_iota(jnp.int32, (chunk, nb), 0)

        def gather(c, carry):
            p = sub + c * chunk
            sel = jnp.where((p == p1) | (p == p2), 1.0, 0.0).astype(BF16)
            r0 = pl.multiple_of(c * chunk, chunk)
            xg_ref[pl.ds(r0, chunk), :] = jnp.dot(sel, h2_ref[...], preferred_element_type=F32).astype(BF16)
            return carry

        lax.fori_loop(0, n_chunks, gather, 0)

    total = _lane_scalar_i32(off_ref[0:1, :], N_EXPERTS)
    padded_total = ((total * tile + (chunk - 1)) // chunk) * (chunk // tile)
    spare_tile = xg_ref.shape[0] // tile - 1
    firsts, counts = [], []
    for x in range(MOE_EPS):
        ex = e * MOE_EPS + x
        first_x = _lane_scalar_i32(off_ref[0:1, :], ex)
        count_x = _lane_scalar_i32(off_ref[1:2, :], ex)
        if x == MOE_EPS - 1:
            count_x = jnp.where(e == n_steps - 1, padded_total - first_x, count_x)
        firsts.append(first_x)
        counts.append(count_x)
    most = counts[0]
    for x in range(1, MOE_EPS):
        most = jnp.maximum(most, counts[x])

    def tile_row(x, j):
        t_idx = jnp.where(counts[x] > 0, firsts[x] + jnp.minimum(j, counts[x] - 1), spare_tile)
        return pl.multiple_of(t_idx * tile, tile)

    def gate_up(j):
        for x in range(MOE_EPS):
            ab = jnp.dot(xg_ref[pl.ds(tile_row(x, j), tile), :], wgu_ref[x], preferred_element_type=F32)
            a = ab[:, :EXPERT_FF]
            hid_ref[x] = ((a * jax.nn.sigmoid(a)) * ab[:, EXPERT_FF:]).astype(BF16)

    def down(j):
        for x in range(MOE_EPS):
            y_ref[pl.ds(tile_row(x, j), tile), :] = jnp.dot(
                hid_ref[x], wd_ref[x], preferred_element_type=F32).astype(BF16)

    def slot(j, carry):
        down(j - 1)
        gate_up(j)
        return carry

    n_slots = jnp.maximum(most, 1)
    gate_up(0)
    lax.fori_loop(1, n_slots, slot, 0)
    down(n_slots - 1)

    @pl.when(e == n_steps - 1)
    def _():
        p1 = col_ref[:, 0:1].astype(jnp.int32)
        p2 = col_ref[:, 1:2].astype(jnp.int32)
        cw1 = col_ref[:, 2:3]
        cw2 = col_ref[:, 3:4]
        gate = mod_ref[5:6, :]
        lane_c = lax.broadcasted_iota(jnp.int32, (nb, chunk), 1)
        o_ref[...] = x1_ref[...]

        def scatter(c, carry):
            p = lane_c + c * chunk
            w = (jnp.where(p == p1, cw1, 0.0) + jnp.where(p == p2, cw2, 0.0)).astype(BF16)
            r0 = pl.multiple_of(c * chunk, chunk)
            o_ref[...] += gate * jnp.dot(w, y_ref[pl.ds(r0, chunk), :], preferred_element_type=F32)
            return carry

        lax.fori_loop(0, padded_total // (chunk // tile), scatter, 0)


def _moe_call(h2, comb, x1, mod, wgu, wd):
    bsz, seq, d = x1.shape
    nb = MOE_TM
    n_blk = bsz * seq // nb
    rows_max = 2 * nb + N_EXPERTS * MOE_TILE
    rows_max = -(-rows_max // MOE_CHUNK) * MOE_CHUNK + MOE_TILE
    tok = lambda j, e: (j, 0)
    out = pl.pallas_call(
        _moe_kernel,
        out_shape=jax.ShapeDtypeStruct((bsz * seq, d), F32),
        grid=(n_blk, N_EXPERTS // MOE_EPS),
        in_specs=[
            pl.BlockSpec((nb, d), tok),
            pl.BlockSpec((nb, LANES), tok),
            pl.BlockSpec((nb, d), tok),
            pl.BlockSpec((None, 6, d), lambda j, e: ((j * nb) // seq, 0, 0)),
            pl.BlockSpec((MOE_EPS, d, 2 * EXPERT_FF), lambda j, e: (e, 0, 0)),
            pl.BlockSpec((MOE_EPS, EXPERT_FF, d), lambda j, e: (e, 0, 0)),
        ],
        out_specs=pl.BlockSpec((nb, d), tok),
        scratch_shapes=[
            pltpu.VMEM((rows_max, d), BF16),
            pltpu.VMEM((rows_max, d), BF16),
            pltpu.VMEM((nb, LANES), F32),
            pltpu.VMEM((LANES, nb), F32),
            pltpu.VMEM((SUBLANES, LANES), F32),
            pltpu.VMEM((MOE_EPS, MOE_TILE, EXPERT_FF), BF16),
        ],
        compiler_params=pltpu.CompilerParams(dimension_semantics=("arbitrary", "arbitrary"),
                                             vmem_limit_bytes=VMEM_LIMIT_BYTES),
        name="moe_experts",
    )(h2.reshape(bsz * seq, d), comb.reshape(bsz * seq, LANES), x1.reshape(bsz * seq, d), mod, wgu, wd)
    return out.reshape(bsz, seq, d)


def _layer(x, mod, rel_bias, norm1, w_in, q_norm, k_norm, conv_w, attn_out_norm, conv_out_norm, w_out,
           norm2, w_group_router, b_group_router, w_expert_router, b_expert_router, w_gate, w_up, w_down):
    bsz, seq, d = x.shape
    aw = ATTN_WIDTH
    topk = min(TOPK_MAX, seq // 4)

    offs = np.cumsum([0, aw, aw, aw, IDX_HEADS * IDX_DIM, IDX_DIM, IDX_HEADS, CONV_WIDTH, CONV_WIDTH, CONV_WIDTH])
    col = lambda n: w_in[:, int(offs[n]):int(offs[n + 1])]
    wm = jnp.concatenate([col(0), col(1), col(3), col(6), col(7), col(8)], axis=1).astype(BF16)
    wvt = col(2).T.astype(BF16)
    wki = jnp.concatenate([col(4), col(4)], axis=1).astype(BF16)
    wwit = col(5).T.astype(BF16)
    qg = (jnp.tile(q_norm, ATTN_HEADS) * ((HEAD_DIM ** -0.5) * LOG2E))[None, :]
    kg = jnp.tile(k_norm, ATTN_HEADS)[None, :]
    grp = np.arange(aw) // CONV_GROUP_DIM
    gmat = jnp.asarray((grp[:, None] == grp[None, :]).astype(np.float32) / CONV_GROUP_DIM, dtype=BF16)

    q, k, vt, qi, ki, wit, cn = _pre_call(
        x, mod, norm1[None, :], wm, wvt, wki, wwit, qg, kg, conv_w, conv_out_norm.reshape(1, -1), gmat)

    bounds = jnp.asarray(_bucket_boundaries())
    an = _attn_call(rel_bias, bounds, q, qi, wit, k, ki, vt, attn_out_norm.reshape(1, -1), topk)

    wr = jnp.concatenate([w_expert_router, w_group_router,
                          jnp.zeros((d, LANES - N_EXPERTS - N_GROUPS), F32)], axis=1).astype(BF16)
    br = jnp.concatenate([b_expert_router, b_group_router,
                          jnp.zeros((LANES - N_EXPERTS - N_GROUPS,), F32)])[None, :]
    x1, h2, comb = _post_call(an, cn, x, mod, norm2[None, :], w_out[:aw].astype(BF16), w_out[aw:].astype(BF16),
                              wr, br)

    wgu = jnp.concatenate([w_gate, w_up], axis=-1).astype(BF16)
    return _moe_call(h2, comb, x1, mod, wgu, w_down.astype(BF16))


def kernel(x, c, rel_bias, w_ada, b_ada, norm1, w_in, q_norm, k_norm, conv_w, attn_out_norm, conv_out_norm,
           w_out, norm2, w_group_router, b_group_router, w_expert_router, b_expert_router, w_gate, w_up,
           w_down):
    bsz, seq, d = x.shape
    assert d == D_MODEL and seq % max(PRE_TM, POST_TM, MOE_TM) == 0 and ATT_TQ == ATT_TK
    depth = w_ada.shape[0]
    for l in range(depth):
        mod = _mod_call(c, w_ada[l], b_ada[l][None, :]).reshape(bsz, 6, d)
        x = _layer(x, mod, rel_bias, norm1[l], w_in[l], q_norm[l], k_norm[l], conv_w[l], attn_out_norm[l],
                   conv_out_norm[l], w_out[l], norm2[l], w_group_router[l], b_group_router[l],
                   w_expert_router[l], b_expert_router[l], w_gate[l], w_up[l], w_down[l])
    return x
```

```python
import functools
import math

import jax
import jax.numpy as jnp
import numpy as np
from jax import lax
from jax.experimental import pallas as pl
from jax.experimental.pallas import tpu as pltpu

F32 = jnp.float32
BF16 = jnp.bfloat16

D_MODEL = 1024
HEAD_DIM = 64
ATTN_HEADS = 8
ATTN_WIDTH = ATTN_HEADS * HEAD_DIM
CONV_WIDTH = D_MODEL - ATTN_WIDTH
CONV_GROUP_DIM = 64
CONV_K = 3
IDX_HEADS = 8
IDX_DIM = 64
TOPK_MAX = 256
IDX_SCALE = (IDX_DIM ** -0.5) * (IDX_HEADS ** -0.5)
N_BUCKETS = 32
MAX_DISTANCE = 128
N_GROUPS = 4
EXPERTS_PER_GROUP = 8
N_EXPERTS = N_GROUPS * EXPERTS_PER_GROUP
EXPERT_FF = 256
EPS = 1e-6
LOG2E = 1.4426950408889634
NEG_BIG = -1e30
COUNT_ACCS = 4
BISECT_GROUP = 4
BISECT_VALUE_STEPS = 8
BISECT_MAX_STEPS = 64

LANES = 128
SUBLANES = 8
BF16_SUBLANES = 16
V_SLAB = HEAD_DIM + BF16_SUBLANES
VMEM_LIMIT_BYTES = 56 * 1024 * 1024

PRE_TM = 512
ATT_TQ = 256
ATT_TK = 256
POST_TM = 512
MOE_TM = 1024
MOE_TILE = 64
MOE_CHUNK = 512
MOE_FFN_TM = 512
MOD_TN = 1536

_NT_DIMS = (((1,), (1,)), ((), ()))


def _tree_sum(parts):
    while len(parts) > 1:
        nxt = [parts[j] + parts[j + 1] for j in range(0, len(parts) - 1, 2)]
        if len(parts) % 2:
            nxt.append(parts[-1])
        parts = nxt
    return parts[0]


def _bucket_boundaries():
    max_exact = N_BUCKETS // 2
    d = np.arange(0, 4 * MAX_DISTANCE, dtype=np.int64)
    nf = np.maximum(d, 1).astype(np.float32)
    large = max_exact + (np.log(nf / np.float32(max_exact)) / np.float32(math.log(MAX_DISTANCE / max_exact))
                         * np.float32(N_BUCKETS - max_exact)).astype(np.int32)
    large = np.minimum(large, N_BUCKETS - 1)
    bucket = np.where(d < max_exact, d, large)
    assert np.all(np.diff(bucket) >= 0) and bucket[-1] == N_BUCKETS - 1
    bounds = [int(np.argmax(bucket >= j)) for j in range(1, N_BUCKETS)]
    return np.asarray([0] + bounds, dtype=np.int32)


def _mod_kernel(c_ref, w_ref, b_ref, o_ref):
    c = c_ref[...]
    act = c * jax.nn.sigmoid(c)
    o_ref[...] = jnp.dot(act, w_ref[...], preferred_element_type=F32,
                         precision=lax.Precision.HIGHEST) + b_ref[...]


def _mod_call(c, w_ada, b_ada):
    bsz, d = c.shape
    n = w_ada.shape[1]
    return pl.pallas_call(
        _mod_kernel,
        out_shape=jax.ShapeDtypeStruct((bsz, n), F32),
        grid=(n // MOD_TN,),
        in_specs=[pl.BlockSpec((bsz, d), lambda j: (0, 0)),
                  pl.BlockSpec((d, MOD_TN), lambda j: (0, j)),
                  pl.BlockSpec((1, MOD_TN), lambda j: (0, j))],
        out_specs=pl.BlockSpec((bsz, MOD_TN), lambda j: (0, j)),
        compiler_params=pltpu.CompilerParams(dimension_semantics=("arbitrary",),
                                             vmem_limit_bytes=VMEM_LIMIT_BYTES),
        name="adaln_mod",
    )(c, w_ada, b_ada)


def _group_rms(y, g_ref):
    ms = jnp.dot((y * y).astype(BF16), g_ref[...], preferred_element_type=F32)
    return y * lax.rsqrt(ms + EPS)


def _pre_kernel(x_ref, mod_ref, n1_ref, wm_ref, wvt_ref, wki_ref, wwit_ref, qg_ref, kg_ref,
                cw_ref, cg_ref, g_ref,
                q_ref, k_ref, vt_ref, qi_ref, ki_ref, wit_ref, cn_ref, carry_ref):
    j = pl.program_id(1)
    tm = x_ref.shape[0]
    aw = ATTN_WIDTH

    x = x_ref[...]
    ms = jnp.mean(x * x, axis=-1, keepdims=True)
    y = x * lax.rsqrt(ms + EPS) * n1_ref[...]
    h = y * (1.0 + mod_ref[1:2, :]) + mod_ref[0:1, :]
    hb = h.astype(BF16)

    def proj(lo):
        return jnp.dot(hb, wm_ref[:, lo:lo + aw], preferred_element_type=F32)

    q = _group_rms(proj(0), g_ref) * qg_ref[...]
    q_ref[...] = q.astype(BF16)
    k = _group_rms(proj(aw), g_ref) * kg_ref[...]
    k_ref[...] = k.astype(BF16)

    vt = lax.dot_general(wvt_ref[...], hb, _NT_DIMS, preferred_element_type=F32).astype(BF16)
    ones = jnp.ones((BF16_SUBLANES, ATT_TK), BF16)
    for cc in range(tm // ATT_TK):
        for hh in range(ATTN_HEADS):
            vt_ref[cc, hh * V_SLAB:hh * V_SLAB + HEAD_DIM, :] = (
                vt[hh * HEAD_DIM:(hh + 1) * HEAD_DIM, cc * ATT_TK:(cc + 1) * ATT_TK])
            vt_ref[cc, hh * V_SLAB + HEAD_DIM:(hh + 1) * V_SLAB, :] = ones

    qi_ref[...] = proj(2 * aw).astype(BF16)
    ki_ref[...] = jnp.dot(hb, wki_ref[...], preferred_element_type=F32).astype(BF16)
    wit_ref[...] = lax.dot_general(wwit_ref[...], hb, _NT_DIMS, preferred_element_type=F32) * IDX_SCALE

    gate_b = proj(3 * aw)
    z = proj(4 * aw) * proj(5 * aw)

    @pl.when(j == 0)
    def _():
        carry_ref[...] = jnp.zeros_like(carry_ref)

    prev = carry_ref[...]
    row = lax.broadcasted_iota(jnp.int32, z.shape, 0)
    z1 = jnp.where(row == 0, prev[SUBLANES - 1:SUBLANES, :], pltpu.roll(z, 1, 0))
    z2 = pltpu.roll(z, 2, 0)
    z2 = jnp.where(row == 0, prev[SUBLANES - 2:SUBLANES - 1, :], z2)
    z2 = jnp.where(row == 1, prev[SUBLANES - 1:SUBLANES, :], z2)
    carry_ref[...] = z[tm - SUBLANES:, :]
    conv = cw_ref[2:3, :] * z + cw_ref[1:2, :] * z1 + cw_ref[0:1, :] * z2
    yc = gate_b * conv
    cn_ref[...] = (_group_rms(yc, g_ref) * cg_ref[...]).astype(BF16)


def _pre_call(x, mod, n1, wm, wvt, wki, wwit, qg, kg, cw, cg, gmat):
    bsz, seq, d = x.shape
    tm = PRE_TM
    nck = tm // ATT_TK
    aw = ATTN_WIDTH
    const = lambda b, j: (0, 0)
    tok = lambda b, j: (b, j, 0)
    out_shape = (
        jax.ShapeDtypeStruct((bsz, seq, aw), BF16),
        jax.ShapeDtypeStruct((bsz, seq, aw), BF16),
        jax.ShapeDtypeStruct((bsz, seq // ATT_TK, ATTN_HEADS * V_SLAB, ATT_TK), BF16),
        jax.ShapeDtypeStruct((bsz, seq, aw), BF16),
        jax.ShapeDtypeStruct((bsz, seq, LANES), BF16),
        jax.ShapeDtypeStruct((bsz, IDX_HEADS, seq), F32),
        jax.ShapeDtypeStruct((bsz, seq, CONV_WIDTH), BF16),
    )
    out_specs = (
        pl.BlockSpec((None, tm, aw), tok),
        pl.BlockSpec((None, tm, aw), tok),
        pl.BlockSpec((None, nck, ATTN_HEADS * V_SLAB, ATT_TK), lambda b, j: (b, j, 0, 0)),
        pl.BlockSpec((None, tm, aw), tok),
        pl.BlockSpec((None, tm, LANES), tok),
        pl.BlockSpec((None, IDX_HEADS, tm), lambda b, j: (b, 0, j)),
        pl.BlockSpec((None, tm, CONV_WIDTH), tok),
    )
    in_specs = [
        pl.BlockSpec((None, tm, d), tok),
        pl.BlockSpec((None, 6, d), lambda b, j: (b, 0, 0)),
        pl.BlockSpec(n1.shape, const),
        pl.BlockSpec(wm.shape, const),
        pl.BlockSpec(wvt.shape, const),
        pl.BlockSpec(wki.shape, const),
        pl.BlockSpec(wwit.shape, const),
        pl.BlockSpec(qg.shape, const),
        pl.BlockSpec(kg.shape, const),
        pl.BlockSpec(cw.shape, const),
        pl.BlockSpec(cg.shape, const),
        pl.BlockSpec(gmat.shape, const),
    ]
    return pl.pallas_call(
        _pre_kernel,
        out_shape=out_shape,
        grid=(bsz, seq // tm),
        in_specs=in_specs,
        out_specs=out_specs,
        scratch_shapes=[pltpu.VMEM((SUBLANES, CONV_WIDTH), F32)],
        compiler_params=pltpu.CompilerParams(dimension_semantics=("arbitrary", "arbitrary"),
                                             vmem_limit_bytes=VMEM_LIMIT_BYTES),
        name="pre_proj",
    )(x, mod, n1, wm, wvt, wki, wwit, qg, kg, cw, cg, gmat)


def _attn_kernel(rb_ref, bnd_ref, q_ref, qi_ref, wit_ref, k_ref, ki_ref, vt_ref, og_ref,
                 o_ref,
                 s_ref, bias_ref, qpad_ref, qipad_ref, lg_ref, acc_ref, out_ref, *, topk):
    b = pl.program_id(0)
    i = pl.program_id(1)
    tq, tk = ATT_TQ, ATT_TK
    nh, hd = ATTN_HEADS, HEAD_DIM

    t_loc = lax.broadcasted_iota(jnp.int32, (tk, tq), 1)
    s_loc = lax.broadcasted_iota(jnp.int32, (tk, tq), 0)

    @pl.when((b == 0) & (i == 0))
    def _():
        for idx in range(2):
            dist = t_loc - s_loc + idx * tq
            for h in range(nh):
                bias_ref[idx, h] = jnp.full((tk, tq), (rb_ref[0, h] - rb_ref[N_BUCKETS - 1, h]) * LOG2E, F32)

            def fill(jb, carry):
                reached = dist >= bnd_ref[jb]
                for h in range(nh):
                    val = (rb_ref[jb, h] - rb_ref[N_BUCKETS - 1, h]) * LOG2E
                    bias_ref[idx, h] = jnp.where(reached, val, bias_ref[idx, h])
                return carry

            lax.fori_loop(1, N_BUCKETS, fill, 0)

    lane = lax.broadcasted_iota(jnp.int32, (tq, LANES), 1)
    for h in range(nh):
        pair = slice((h // 2) * LANES, (h // 2 + 1) * LANES)
        keep = (lane // hd) == (h % 2)
        qpad_ref[h] = jnp.where(keep, q_ref[:, pair], jnp.zeros((), BF16))
        qipad_ref[h] = jnp.where(keep, qi_ref[:, pair], jnp.zeros((), BF16))

    def score_chunk(c):
        kic = ki_ref[pl.ds(pl.multiple_of(c * tk, tk), tk), :]
        acc = jnp.zeros((tk, tq), F32)
        for h in range(nh):
            e = lax.dot_general(kic, qipad_ref[h], _NT_DIMS, preferred_element_type=F32)
            acc = acc + wit_ref[h:h + 1, :] * jnp.maximum(e, 0.0)
        return acc

    def a_body(c, carry):
        rmin, rmax = carry
        sc = score_chunk(c)
        s_ref[c] = sc
        return (jnp.minimum(rmin, jnp.min(sc, axis=0, keepdims=True)),
                jnp.maximum(rmax, jnp.max(sc, axis=0, keepdims=True)))

    rmin0 = jnp.full((1, tq), jnp.inf, F32)
    rmax0 = jnp.full((1, tq), -jnp.inf, F32)
    rmin, rmax = lax.fori_loop(0, i, a_body, (rmin0, rmax0))
    sc = score_chunk(i)
    causal = s_loc <= t_loc
    s_ref[i] = jnp.where(causal, sc, -jnp.inf)
    rmin = jnp.minimum(rmin, jnp.min(jnp.where(causal, sc, jnp.inf), axis=0, keepdims=True))
    rmax = jnp.maximum(rmax, jnp.max(jnp.where(causal, sc, -jnp.inf), axis=0, keepdims=True))

    def count_ge(thr):
        def body(c, accs):
            hit = s_ref[c] >= thr
            accs = list(accs)
            for r in range(tk // SUBLANES):
                a = accs[r % COUNT_ACCS]
                accs[r % COUNT_ACCS] = jnp.where(hit[r * SUBLANES:(r + 1) * SUBLANES], a + 1.0, a)
            return tuple(accs)
        accs = lax.fori_loop(0, i + 1, body,
                             tuple(jnp.zeros((SUBLANES, tq), F32) for _ in range(COUNT_ACCS)))
        return jnp.sum(_tree_sum(list(accs)), axis=0, keepdims=True)

    def order_key(v):
        bits = pltpu.bitcast(v, jnp.int32)
        return jnp.where(bits < 0, bits ^ jnp.int32(0x7FFFFFFF), bits)

    def from_order_key(key):
        return pltpu.bitcast(jnp.where(key < 0, key ^ jnp.int32(0x7FFFFFFF), key), F32)

    t_glob = (i * tq + lax.broadcasted_iota(jnp.int32, (1, tq), 1)).astype(F32)
    n_causal = t_glob + 1.0
    kf = jnp.minimum(float(topk), n_causal)
    c_max = count_ge(rmax)
    all_sel = n_causal <= kf
    max_ge = c_max >= kf
    active0 = jnp.where(all_sel | max_ge, 0.0, 1.0)
    thr0 = jnp.where(all_sel, rmin, rmax)
    tie0 = jnp.where(jnp.logical_not(all_sel) & (c_max > kf), 1.0, 0.0)
    hif0 = jnp.full((1, tq), jnp.inf, F32)
    need0 = kf

    def b_cond(st):
        return (jnp.max(st[0]) > 0.0) & (st[8] <= BISECT_MAX_STEPS)

    def b_body(st):
        active, lo, hi, fhi, thr, tie, hif, need, step = st
        lo_key = order_key(lo)
        hi_key = order_key(hi)
        mid_key = (lo_key >> 1) + (hi_key >> 1) + (lo_key & hi_key & 1)
        mid_val = lo + (hi - lo) * 0.5
        use_val = (step < BISECT_VALUE_STEPS) & (mid_val > lo) & (mid_val < hi)
        mid = jnp.where(use_val, mid_val, from_order_key(mid_key))
        collapsed = (mid_key == lo_key) | (step >= BISECT_MAX_STEPS)
        cm = count_ge(mid)
        act = active > 0.0
        live = act & jnp.logical_not(collapsed)
        found = live & (cm == kf)
        go_up = live & (cm > kf)
        go_dn = live & (cm < kf)
        ends_tie = act & collapsed
        thr = jnp.where(found, mid, jnp.where(ends_tie, lo, thr))
        tie = jnp.where(ends_tie, 1.0, tie)
        hif = jnp.where(ends_tie, hi, hif)
        need = jnp.where(ends_tie, kf - fhi, need)
        lo = jnp.where(go_up, mid, lo)
        fhi = jnp.where(go_dn, cm, fhi)
        hi = jnp.where(go_dn, mid, hi)
        active = jnp.where(found | ends_tie, 0.0, active)
        return active, lo, hi, fhi, thr, tie, hif, need, step + 1

    def b_group(st):
        for _ in range(BISECT_GROUP):
            st = b_body(st)
        return st

    _, _, _, _, thr, tie, hif, need, _ = lax.while_loop(
        b_cond, b_group, (active0, rmin, rmax, c_max, thr0, tie0, hif0, need0, jnp.int32(0)))

    @pl.when(jnp.max(tie) > 0.0)
    def _():
        tri = jnp.where(lax.broadcasted_iota(jnp.int32, (tk, tk), 1)
                        <= lax.broadcasted_iota(jnp.int32, (tk, tk), 0), 1.0, 0.0).astype(BF16)

        def body(c, seen):
            sc_c = s_ref[c]
            tied = (sc_c >= thr) & (sc_c < hif) & (tie > 0.0)
            rank = jnp.dot(tri, jnp.where(tied, 1.0, 0.0).astype(BF16), preferred_element_type=F32) + seen
            s_ref[c] = jnp.where(tied & (rank > need), -jnp.inf, sc_c)
            return rank[tk - 1:tk, :]

        lax.fori_loop(0, i + 1, body, jnp.zeros((1, tq), F32))

    acc_ref[...] = jnp.zeros(acc_ref.shape, F32)

    def store_logits(c, slot, bias_idx):
        masked = jnp.where(s_ref[c] >= thr, 0.0, NEG_BIG)
        row0 = pl.multiple_of(c * tk, tk)
        for h in range(nh):
            kc = k_ref[pl.ds(row0, tk), (h // 2) * LANES:(h // 2 + 1) * LANES]
            lt = lax.dot_general(kc, qpad_ref[h], _NT_DIMS, preferred_element_type=F32) + masked
            if bias_idx is not None:
                lt = lt + bias_ref[bias_idx, h]
            lg_ref[slot, h] = lt

    def softmax_pv(c, slot, m_all):
        m_out = []
        for h in range(nh):
            m_old = m_all[h]
            m_new = jnp.maximum(m_old, jnp.max(lg_ref[slot, h], axis=0, keepdims=True))
            p = jnp.exp2(lg_ref[slot, h] - m_new).astype(BF16)
            alpha = jnp.exp2(m_old - m_new)
            pv = jnp.dot(vt_ref[c, h * V_SLAB:(h + 1) * V_SLAB, :], p, preferred_element_type=F32)
            acc_ref[h] = alpha * acc_ref[h] + pv
            m_out.append(m_new)
        return tuple(m_out)

    def near_step(m_all):
        store_logits(i - 1, 1, 1)
        return softmax_pv(i, 0, m_all)

    def far_step(j, parity, m_all):
        c = i - 2 - j
        store_logits(c, parity, None)
        return softmax_pv(c + 1, 1 - parity, m_all)

    def far_pair(jj, m_all):
        return far_step(2 * jj + 1, 1, far_step(2 * jj, 0, m_all))

    n_far = jnp.maximum(i - 1, 0)
    m_all = tuple(jnp.full((1, tq), NEG_BIG, F32) for _ in range(nh))
    store_logits(i, 0, 0)
    m_all = lax.cond(i >= 1, near_step, lambda m: m, m_all)
    m_all = lax.fori_loop(0, n_far // 2, far_pair, m_all)
    m_all = lax.cond((n_far & 1) == 1, lambda m: far_step(n_far - 1, 0, m), lambda m: m, m_all)
    lax.cond((i & 1) == 0, lambda m: softmax_pv(0, 0, m), lambda m: softmax_pv(0, 1, m), m_all)

    for h in range(nh):
        o = acc_ref[h, :hd, :] / acc_ref[h, hd:hd + 1, :]
        ms = jnp.mean(o * o, axis=0, keepdims=True)
        out_ref[h * hd:(h + 1) * hd, :] = o * lax.rsqrt(ms + EPS)
    o_ref[...] = (out_ref[...].T * og_ref[...]).astype(BF16)


def _attn_call(rel_bias, bounds, q, qi, wit, k, ki, vt, og, topk):
    bsz, seq, aw = q.shape
    tq, tk = ATT_TQ, ATT_TK
    nck = seq // tk
    blk_q = lambda b, i: (b, i, 0)
    whole = lambda b, i: (b, 0, 0)
    smem = pl.BlockSpec(memory_space=pltpu.SMEM)
    return pl.pallas_call(
        functools.partial(_attn_kernel, topk=topk),
        out_shape=jax.ShapeDtypeStruct((bsz, seq, aw), BF16),
        grid=(bsz, seq // tq),
        in_specs=[
            smem, smem,
            pl.BlockSpec((None, tq, aw), blk_q),
            pl.BlockSpec((None, tq, aw), blk_q),
            pl.BlockSpec((None, IDX_HEADS, tq), lambda b, i: (b, 0, i)),
            pl.BlockSpec((None, seq, aw), whole),
            pl.BlockSpec((None, seq, LANES), whole),
            pl.BlockSpec((None, nck, ATTN_HEADS * V_SLAB, tk), lambda b, i: (b, 0, 0, 0)),
            pl.BlockSpec(og.shape, lambda b, i: (0, 0)),
        ],
        out_specs=pl.BlockSpec((None, tq, aw), blk_q),
        scratch_shapes=[
            pltpu.VMEM((nck, tk, tq), F32),
            pltpu.VMEM((2, ATTN_HEADS, tk, tq), F32),
            pltpu.VMEM((ATTN_HEADS, tq, LANES), BF16),
            pltpu.VMEM((IDX_HEADS, tq, LANES), BF16),
            pltpu.VMEM((2, ATTN_HEADS, tk, tq), F32),
            pltpu.VMEM((ATTN_HEADS, V_SLAB, tq), F32),
            pltpu.VMEM((aw, tq), F32),
        ],
        compiler_params=pltpu.CompilerParams(dimension_semantics=("arbitrary", "arbitrary"),
                                             vmem_limit_bytes=VMEM_LIMIT_BYTES),
        name="dsa_attention",
    )(rel_bias, bounds, q, qi, wit, k, ki, vt, og)


def _post_kernel(an_ref, cn_ref, x_ref, mod_ref, n2_ref, woa_ref, woc_ref, wr_ref, br_ref,
                 x1_ref, h2_ref, comb_ref, cnt_ref):
    mix = (jnp.dot(an_ref[...], woa_ref[...], preferred_element_type=F32)
           + jnp.dot(cn_ref[...], woc_ref[...], preferred_element_type=F32))
    x1 = x_ref[...] + mod_ref[2:3, :] * mix
    x1_ref[...] = x1
    ms = jnp.mean(x1 * x1, axis=-1, keepdims=True)
    h2 = x1 * lax.rsqrt(ms + EPS) * n2_ref[...] * (1.0 + mod_ref[4:5, :]) + mod_ref[3:4, :]
    h2b = h2.astype(BF16)
    h2_ref[...] = h2b

    logits = jnp.dot(h2b, wr_ref[...], preferred_element_type=F32) + br_ref[...]
    lane = lax.broadcasted_iota(jnp.int32, logits.shape, 1)
    lane_f = lane.astype(F32)
    far = float(LANES)
    is_g = (lane >= N_EXPERTS) & (lane < N_EXPERTS + N_GROUPS)
    gl = jnp.where(is_g, logits, -jnp.inf)
    gmax = jnp.max(gl, axis=-1, keepdims=True)
    g_sel = jnp.min(jnp.where(is_g & (gl == gmax), lane_f, far), axis=-1, keepdims=True) - float(N_EXPERTS)
    p_g = 1.0 / jnp.sum(jnp.exp(gl - gmax), axis=-1, keepdims=True)

    in_grp = (lane < N_EXPERTS) & ((lane // EXPERTS_PER_GROUP).astype(F32) == g_sel)
    e1 = jnp.where(in_grp, logits, -jnp.inf)
    l1 = jnp.max(e1, axis=-1, keepdims=True)
    i1 = jnp.min(jnp.where(in_grp & (e1 == l1), lane_f, far), axis=-1, keepdims=True)
    rest = in_grp & (lane_f != i1)
    e2 = jnp.where(rest, logits, -jnp.inf)
    l2 = jnp.max(e2, axis=-1, keepdims=True)
    i2 = jnp.min(jnp.where(rest & (e2 == l2), lane_f, far), axis=-1, keepdims=True)
    r = jnp.exp(l2 - l1)
    w1 = 1.0 / (1.0 + r)
    w2 = r / (1.0 + r)
    comb = jnp.where(lane_f == i1, p_g * w1, 0.0) + jnp.where(lane_f == i2, p_g * w2, 0.0)
    comb_ref[...] = comb
    cnt = jnp.sum(jnp.where(comb != 0.0, 1.0, 0.0), axis=0, keepdims=True)
    cnt_ref[...] = jnp.broadcast_to(cnt, cnt_ref.shape)


def _post_call(an, cn, x, mod, n2, woa, woc, wr, br):
    bsz, seq, d = x.shape
    tm = POST_TM
    tok = lambda b, j: (b, j, 0)
    const = lambda b, j: (0, 0)
    return pl.pallas_call(
        _post_kernel,
        out_shape=(jax.ShapeDtypeStruct((bsz, seq, d), F32),
                   jax.ShapeDtypeStruct((bsz, seq, d), BF16),
                   jax.ShapeDtypeStruct((bsz, seq, LANES), F32),
                   jax.ShapeDtypeStruct((bsz, seq // tm, SUBLANES, LANES), F32)),
        grid=(bsz, seq // tm),
        in_specs=[
            pl.BlockSpec((None, tm, ATTN_WIDTH), tok),
            pl.BlockSpec((None, tm, CONV_WIDTH), tok),
            pl.BlockSpec((None, tm, d), tok),
            pl.BlockSpec((None, 6, d), lambda b, j: (b, 0, 0)),
            pl.BlockSpec(n2.shape, const),
            pl.BlockSpec(woa.shape, const),
            pl.BlockSpec(woc.shape, const),
            pl.BlockSpec(wr.shape, const),
            pl.BlockSpec(br.shape, const),
        ],
        out_specs=(pl.BlockSpec((None, tm, d), tok),
                   pl.BlockSpec((None, tm, d), tok),
                   pl.BlockSpec((None, tm, LANES), tok),
                   pl.BlockSpec((None, None, SUBLANES, LANES), lambda b, j: (b, j, 0, 0))),
        compiler_params=pltpu.CompilerParams(dimension_semantics=("arbitrary", "arbitrary"),
                                             vmem_limit_bytes=VMEM_LIMIT_BYTES),
        name="post_router",
    )(an, cn, x, mod, n2, woa, woc, wr, br)


def _strict_tri(n, lower):
    r = lax.broadcasted_iota(jnp.int32, (n, n), 0)
    c = lax.broadcasted_iota(jnp.int32, (n, n), 1)
    return jnp.where((c < r) if lower else (r < c), 1.0, 0.0).astype(BF16)


def _moe_tile_copies(ntile_ref, lfirst_ref, gfirst_ref, blk, local_ref, global_ref, sem, to_global, wait):
    tile = MOE_TILE

    def per_expert(x, carry):
        lf = lfirst_ref[blk, x]
        gf = gfirst_ref[blk, x]

        def per_tile(j, c):
            loc = local_ref.at[pl.ds(pl.multiple_of((lf + j) * tile, tile), tile), :]
            glo = global_ref.at[pl.ds(pl.multiple_of((gf + j) * tile, tile), tile), :]
            cp = pltpu.make_async_copy(loc, glo, sem) if to_global else pltpu.make_async_copy(glo, loc, sem)
            if wait:
                cp.wait()
            else:
                cp.start()
            return c

        lax.fori_loop(0, ntile_ref[blk, x], per_tile, 0)
        return carry

    lax.fori_loop(0, N_EXPERTS, per_expert, 0)


def _moe_gather_kernel(ntile_ref, lfirst_ref, gfirst_ref, pad_ref,
                       h2_ref, comb_ref,
                       col_ref, xg_hbm,
                       xg_ref, row_ref, zero_ref, sem):
    blk = pl.program_id(0)
    nb = h2_ref.shape[0]
    tile, chunk = MOE_TILE, MOE_CHUNK
    lane = lax.broadcasted_iota(jnp.int32, (nb, LANES), 1)

    comb = comb_ref[...]
    assigned = comb != 0.0
    a_f = jnp.where(assigned, 1.0, 0.0)
    rank = jnp.dot(_strict_tri(nb, True), a_f.astype(BF16), preferred_element_type=F32)
    cnt = rank[nb - 1:nb, :] + a_f[nb - 1:nb, :]
    ntile = jnp.floor((cnt + float(tile - 1)) * (1.0 / tile))
    first = jnp.dot(jnp.broadcast_to(ntile, (SUBLANES, LANES)).astype(BF16), _strict_tri(LANES, False),
                    preferred_element_type=F32)[0:1, :]
    pos = first * float(tile) + rank
    pos1 = jnp.min(jnp.where(assigned, pos, 1e9), axis=1, keepdims=True)
    pos2 = jnp.max(jnp.where(assigned, pos, -1.0), axis=1, keepdims=True)
    pos2 = jnp.where(pos2 == pos1, -1.0, pos2)
    cw1 = jnp.sum(jnp.where(assigned & (pos == pos1), comb, 0.0), axis=1, keepdims=True)
    cw2 = jnp.sum(jnp.where(assigned & (pos == pos2), comb, 0.0), axis=1, keepdims=True)
    info = jnp.where(lane == 0, pos1, jnp.where(lane == 1, pos2, jnp.where(lane == 2, cw1,
                     jnp.where(lane == 3, cw2, 0.0))))
    col_ref[...] = info
    row_ref[...] = info.T

    total = lfirst_ref[blk, N_EXPERTS - 1] + ntile_ref[blk, N_EXPERTS - 1]
    n_chunks = (total * tile + (chunk - 1)) // chunk
    p1 = row_ref[0:1, :].astype(jnp.int32)
    p2 = row_ref[1:2, :].astype(jnp.int32)
    sub = lax.broadcasted_iota(jnp.int32, (chunk, nb), 0)

    def gather(c, carry):
        p = sub + c * chunk
        sel = jnp.where((p == p1) | (p == p2), 1.0, 0.0).astype(BF16)
        r0 = pl.multiple_of(c * chunk, chunk)
        xg_ref[pl.ds(r0, chunk), :] = jnp.dot(sel, h2_ref[...], preferred_element_type=F32).astype(BF16)
        return carry

    lax.fori_loop(0, n_chunks, gather, 0)

    _moe_tile_copies(ntile_ref, lfirst_ref, gfirst_ref, blk, xg_ref, xg_hbm, sem, True, False)

    is_last = blk == pl.num_programs(0) - 1

    def pad_copies(wait):
        def per_expert(x, carry):
            g0 = gfirst_ref[blk, x] + ntile_ref[blk, x]

            def per_tile(j, c):
                dst = xg_hbm.at[pl.ds(pl.multiple_of((g0 + j) * tile, tile), tile), :]
                cp = pltpu.make_async_copy(zero_ref, dst, sem)
                if wait:
                    cp.wait()
                else:
                    cp.start()
                return c

            lax.fori_loop(0, pad_ref[x], per_tile, 0)
            return carry

        lax.fori_loop(0, N_EXPERTS, per_expert, 0)

    @pl.when(is_last)
    def _():
        zero_ref[...] = jnp.zeros(zero_ref.shape, BF16)
        pad_copies(False)

    _moe_tile_copies(ntile_ref, lfirst_ref, gfirst_ref, blk, xg_ref, xg_hbm, sem, True, True)

    @pl.when(is_last)
    def _():
        pad_copies(True)


def _moe_ffn_kernel(texp_ref, nt_ref, x_ref, wgu_ref, wd_ref, y_ref):
    @pl.when(pl.program_id(0) < nt_ref[0])
    def _():
        ab = jnp.dot(x_ref[...], wgu_ref[...], preferred_element_type=F32)
        a = ab[:, :EXPERT_FF]
        hid = ((a * jax.nn.sigmoid(a)) * ab[:, EXPERT_FF:]).astype(BF16)
        y_ref[...] = jnp.dot(hid, wd_ref[...], preferred_element_type=F32).astype(BF16)


def _moe_scatter_kernel(ntile_ref, lfirst_ref, gfirst_ref,
                        col_ref, x1_ref, mod_ref, y_hbm,
                        o_ref,
                        y_ref, sem):
    blk = pl.program_id(0)
    nb = x1_ref.shape[0]
    tile, chunk = MOE_TILE, MOE_CHUNK
    _moe_tile_copies(ntile_ref, lfirst_ref, gfirst_ref, blk, y_ref, y_hbm, sem, False, False)

    total = lfirst_ref[blk, N_EXPERTS - 1] + ntile_ref[blk, N_EXPERTS - 1]
    n_chunks = (total * tile + (chunk - 1)) // chunk

    def clear(t, carry):
        y_ref[pl.ds(pl.multiple_of(t * tile, tile), tile), :] = jnp.zeros((tile, y_ref.shape[1]), BF16)
        return carry

    lax.fori_loop(total, n_chunks * (chunk // tile), clear, 0)

    p1 = col_ref[:, 0:1].astype(jnp.int32)
    p2 = col_ref[:, 1:2].astype(jnp.int32)
    cw1 = col_ref[:, 2:3]
    cw2 = col_ref[:, 3:4]
    gate = mod_ref[5:6, :]
    lane_c = lax.broadcasted_iota(jnp.int32, (nb, chunk), 1)
    o_ref[...] = x1_ref[...]
    _moe_tile_copies(ntile_ref, lfirst_ref, gfirst_ref, blk, y_ref, y_hbm, sem, False, True)

    def scatter(c, carry):
        p = lane_c + c * chunk
        w = (jnp.where(p == p1, cw1, 0.0) + jnp.where(p == p2, cw2, 0.0)).astype(BF16)
        r0 = pl.multiple_of(c * chunk, chunk)
        o_ref[...] += gate * jnp.dot(w, y_ref[pl.ds(r0, chunk), :], preferred_element_type=F32)
        return carry

    lax.fori_loop(0, n_chunks, scatter, 0)


def _moe_call(h2, comb, cnt_tiles, x1, mod, wgu, wd):
    bsz, seq, d = x1.shape
    nb, tile, ftm = MOE_TM, MOE_TILE, MOE_FFN_TM
    n_tok = bsz * seq
    n_blk = n_tok // nb
    region = ftm // tile
    rows_local = -(-(2 * nb + N_EXPERTS * tile) // MOE_CHUNK) * MOE_CHUNK
    tiles_global = (2 * n_tok) // tile + n_blk * N_EXPERTS + N_EXPERTS * (region - 1)
    n_ffn_max = -(-tiles_global // region)
    rows_global = n_ffn_max * ftm

    cnt = cnt_tiles[:, :, 0, :N_EXPERTS].reshape(n_blk, nb // POST_TM, N_EXPERTS).sum(axis=1).astype(jnp.int32)
    ntile = (cnt + (tile - 1)) // tile
    lfirst = jnp.cumsum(ntile, axis=1) - ntile
    tot = ntile.sum(axis=0)
    ptot = (tot + (region - 1)) // region * region
    ebase = jnp.cumsum(ptot) - ptot
    gfirst = ebase[None, :] + jnp.cumsum(ntile, axis=0) - ntile
    pad = ptot - tot
    n_ffn = (ptot.sum() // region).reshape(1)
    ends = jnp.cumsum(ptot) // region
    texp = jnp.minimum(jnp.searchsorted(ends, jnp.arange(n_ffn_max, dtype=jnp.int32), side="right"),
                       N_EXPERTS - 1).astype(jnp.int32)

    h2f = h2.reshape(n_tok, d)
    combf = comb.reshape(n_tok, LANES)
    col, xg = pl.pallas_call(
        _moe_gather_kernel,
        out_shape=(jax.ShapeDtypeStruct((n_tok, LANES), F32),
                   jax.ShapeDtypeStruct((rows_global, d), BF16)),
        grid_spec=pltpu.PrefetchScalarGridSpec(
            num_scalar_prefetch=4,
            grid=(n_blk,),
            in_specs=[pl.BlockSpec((nb, d), lambda j, *_: (j, 0)),
                      pl.BlockSpec((nb, LANES), lambda j, *_: (j, 0))],
            out_specs=(pl.BlockSpec((nb, LANES), lambda j, *_: (j, 0)),
                       pl.BlockSpec(memory_space=pl.ANY)),
            scratch_shapes=[
                pltpu.VMEM((rows_local, d), BF16),
                pltpu.VMEM((LANES, nb), F32),
                pltpu.VMEM((tile, d), BF16),
                pltpu.SemaphoreType.DMA,
            ]),
        compiler_params=pltpu.CompilerParams(dimension_semantics=("arbitrary",),
                                             vmem_limit_bytes=VMEM_LIMIT_BYTES),
        name="moe_gather",
    )(ntile, lfirst, gfirst, pad, h2f, combf)

    last = lambda t, te, nt: jnp.minimum(t, nt[0] - 1)
    y = pl.pallas_call(
        _moe_ffn_kernel,
        out_shape=jax.ShapeDtypeStruct((rows_global, d), BF16),
        grid_spec=pltpu.PrefetchScalarGridSpec(
            num_scalar_prefetch=2,
            grid=(n_ffn_max,),
            in_specs=[pl.BlockSpec((ftm, d), lambda t, te, nt: (last(t, te, nt), 0)),
                      pl.BlockSpec((None, d, 2 * EXPERT_FF), lambda t, te, nt: (te[last(t, te, nt)], 0, 0)),
                      pl.BlockSpec((None, EXPERT_FF, d), lambda t, te, nt: (te[last(t, te, nt)], 0, 0))],
            out_specs=pl.BlockSpec((ftm, d), lambda t, te, nt: (last(t, te, nt), 0))),
        compiler_params=pltpu.CompilerParams(dimension_semantics=("arbitrary",),
                                             vmem_limit_bytes=VMEM_LIMIT_BYTES),
        name="moe_ffn",
    )(texp, n_ffn, xg, wgu, wd)

    out = pl.pallas_call(
        _moe_scatter_kernel,
        out_shape=jax.ShapeDtypeStruct((n_tok, d), F32),
        grid_spec=pltpu.PrefetchScalarGridSpec(
            num_scalar_prefetch=3,
            grid=(n_blk,),
            in_specs=[pl.BlockSpec((nb, LANES), lambda j, *_: (j, 0)),
                      pl.BlockSpec((nb, d), lambda j, *_: (j, 0)),
                      pl.BlockSpec((None, 6, d), lambda j, *_: ((j * nb) // seq, 0, 0)),
                      pl.BlockSpec(memory_space=pl.ANY)],
            out_specs=pl.BlockSpec((nb, d), lambda j, *_: (j, 0)),
            scratch_shapes=[pltpu.VMEM((rows_local, d), BF16),
                            pltpu.SemaphoreType.DMA]),
        compiler_params=pltpu.CompilerParams(dimension_semantics=("arbitrary",),
                                             vmem_limit_bytes=VMEM_LIMIT_BYTES),
        name="moe_scatter",
    )(ntile, lfirst, gfirst, col, x1.reshape(n_tok, d), mod, y)
    return out.reshape(bsz, seq, d)


def _layer(x, mod, rel_bias, norm1, w_in, q_norm, k_norm, conv_w, attn_out_norm, conv_out_norm, w_out,
           norm2, w_group_router, b_group_router, w_expert_router, b_expert_router, w_gate, w_up, w_down):
    bsz, seq, d = x.shape
    aw = ATTN_WIDTH
    topk = min(TOPK_MAX, seq // 4)

    offs = np.cumsum([0, aw, aw, aw, IDX_HEADS * IDX_DIM, IDX_DIM, IDX_HEADS, CONV_WIDTH, CONV_WIDTH, CONV_WIDTH])
    col = lambda n: w_in[:, int(offs[n]):int(offs[n + 1])]
    wm = jnp.concatenate([col(0), col(1), col(3), col(6), col(7), col(8)], axis=1).astype(BF16)
    wvt = col(2).T.astype(BF16)
    wki = jnp.concatenate([col(4), col(4)], axis=1).astype(BF16)
    wwit = col(5).T.astype(BF16)
    qg = (jnp.tile(q_norm, ATTN_HEADS) * ((HEAD_DIM ** -0.5) * LOG2E))[None, :]
    kg = jnp.tile(k_norm, ATTN_HEADS)[None, :]
    grp = np.arange(aw) // CONV_GROUP_DIM
    gmat = jnp.asarray((grp[:, None] == grp[None, :]).astype(np.float32) / CONV_GROUP_DIM, dtype=BF16)

    q, k, vt, qi, ki, wit, cn = _pre_call(
        x, mod, norm1[None, :], wm, wvt, wki, wwit, qg, kg, conv_w, conv_out_norm.reshape(1, -1), gmat)

    bounds = jnp.asarray(_bucket_boundaries())
    an = _attn_call(rel_bias, bounds, q, qi, wit, k, ki, vt, attn_out_norm.reshape(1, -1), topk)

    wr = jnp.concatenate([w_expert_router, w_group_router,
                          jnp.zeros((d, LANES - N_EXPERTS - N_GROUPS), F32)], axis=1).astype(BF16)
    br = jnp.concatenate([b_expert_router, b_group_router,
                          jnp.zeros((LANES - N_EXPERTS - N_GROUPS,), F32)])[None, :]
    x1, h2, comb, cnt_tiles = _post_call(an, cn, x, mod, norm2[None, :], w_out[:aw].astype(BF16),
                                         w_out[aw:].astype(BF16), wr, br)

    wgu = jnp.concatenate([w_gate, w_up], axis=-1).astype(BF16)
    return _moe_call(h2, comb, cnt_tiles, x1, mod, wgu, w_down.astype(BF16))


def kernel(x, c, rel_bias, w_ada, b_ada, norm1, w_in, q_norm, k_norm, conv_w, attn_out_norm, conv_out_norm,
           w_out, norm2, w_group_router, b_group_router, w_expert_router, b_expert_router, w_gate, w_up,
           w_down):
    bsz, seq, d = x.shape
    assert d == D_MODEL and seq % max(PRE_TM, POST_TM, MOE_TM) == 0 and ATT_TQ == ATT_TK
    depth = w_ada.shape[0]
    for l in range(depth):
        mod = _mod_call(c, w_ada[l], b_ada[l][None, :]).reshape(bsz, 6, d)
        x = _layer(x, mod, rel_bias, norm1[l], w_in[l], q_norm[l], k_norm[l], conv_w[l], attn_out_norm[l],
                   conv_out_norm[l], w_out[l], norm2[l], w_group_router[l], b_group_router[l],
                   w_expert_router[l], b_expert_router[l], w_gate[l], w_up[l], w_down[l])
    return x
```

```python
import functools
import math

import jax
import jax.numpy as jnp
import numpy as np
from jax import lax
from jax.experimental import pallas as pl
from jax.experimental.pallas import tpu as pltpu

F32 = jnp.float32
BF16 = jnp.bfloat16

D_MODEL = 1024
HEAD_DIM = 64
ATTN_HEADS = 8
ATTN_WIDTH = ATTN_HEADS * HEAD_DIM
CONV_WIDTH = D_MODEL - ATTN_WIDTH
CONV_GROUP_DIM = 64
CONV_K = 3
IDX_HEADS = 8
IDX_DIM = 64
TOPK_MAX = 256
IDX_SCALE = (IDX_DIM ** -0.5) * (IDX_HEADS ** -0.5)
N_BUCKETS = 32
MAX_DISTANCE = 128
N_GROUPS = 4
EXPERTS_PER_GROUP = 8
N_EXPERTS = N_GROUPS * EXPERTS_PER_GROUP
EXPERT_FF = 256
EPS = 1e-6
LOG2E = 1.4426950408889634
NEG_BIG = -1e30
COUNT_ACCS = 4
BISECT_GROUP = 4
BISECT_VALUE_STEPS = 8
BISECT_MAX_STEPS = 64

LANES = 128
SUBLANES = 8
BF16_SUBLANES = 16
V_SLAB = HEAD_DIM + BF16_SUBLANES
VMEM_LIMIT_BYTES = 56 * 1024 * 1024

PRE_TM = 512
ATT_TQ = 256
ATT_TK = 256
POST_TM = 512
MOE_TM = 1024
MOE_TILE = 64
MOE_CHUNK = 512
MOE_FFN_TM = 512
MOD_TN = 1536

_NT_DIMS = (((1,), (1,)), ((), ()))


def _tree_sum(parts):
    while len(parts) > 1:
        nxt = [parts[j] + parts[j + 1] for j in range(0, len(parts) - 1, 2)]
        if len(parts) % 2:
            nxt.append(parts[-1])
        parts = nxt
    return parts[0]


def _bucket_boundaries():
    max_exact = N_BUCKETS // 2
    d = np.arange(0, 4 * MAX_DISTANCE, dtype=np.int64)
    nf = np.maximum(d, 1).astype(np.float32)
    large = max_exact + (np.log(nf / np.float32(max_exact)) / np.float32(math.log(MAX_DISTANCE / max_exact))
                         * np.float32(N_BUCKETS - max_exact)).astype(np.int32)
    large = np.minimum(large, N_BUCKETS - 1)
    bucket = np.where(d < max_exact, d, large)
    assert np.all(np.diff(bucket) >= 0) and bucket[-1] == N_BUCKETS - 1
    bounds = [int(np.argmax(bucket >= j)) for j in range(1, N_BUCKETS)]
    return np.asarray([0] + bounds, dtype=np.int32)


def _mod_kernel(c_ref, w_ref, b_ref, o_ref):
    c = c_ref[...]
    act = c * jax.nn.sigmoid(c)
    o_ref[...] = jnp.dot(act, w_ref[...], preferred_element_type=F32,
                         precision=lax.Precision.HIGHEST) + b_ref[...]


def _mod_call(c, w_ada, b_ada):
    bsz, d = c.shape
    n = w_ada.shape[1]
    return pl.pallas_call(
        _mod_kernel,
        out_shape=jax.ShapeDtypeStruct((bsz, n), F32),
        grid=(n // MOD_TN,),
        in_specs=[pl.BlockSpec((bsz, d), lambda j: (0, 0)),
                  pl.BlockSpec((d, MOD_TN), lambda j: (0, j)),
                  pl.BlockSpec((1, MOD_TN), lambda j: (0, j))],
        out_specs=pl.BlockSpec((bsz, MOD_TN), lambda j: (0, j)),
        compiler_params=pltpu.CompilerParams(dimension_semantics=("arbitrary",),
                                             vmem_limit_bytes=VMEM_LIMIT_BYTES),
        name="adaln_mod",
    )(c, w_ada, b_ada)


def _group_rms(y, g_ref):
    ms = jnp.dot((y * y).astype(BF16), g_ref[...], preferred_element_type=F32)
    return y * lax.rsqrt(ms + EPS)


def _pre_kernel(x_ref, mod_ref, n1_ref, wm_ref, wvt_ref, wki_ref, wwit_ref, qg_ref, kg_ref,
                cw_ref, cg_ref, g_ref,
                q_ref, k_ref, vt_ref, qi_ref, ki_ref, wit_ref, cn_ref, carry_ref):
    j = pl.program_id(1)
    tm = x_ref.shape[0]
    aw = ATTN_WIDTH

    x = x_ref[...]
    ms = jnp.mean(x * x, axis=-1, keepdims=True)
    y = x * lax.rsqrt(ms + EPS) * n1_ref[...]
    h = y * (1.0 + mod_ref[1:2, :]) + mod_ref[0:1, :]
    hb = h.astype(BF16)

    def proj(lo):
        return jnp.dot(hb, wm_ref[:, lo:lo + aw], preferred_element_type=F32)

    q = _group_rms(proj(0), g_ref) * qg_ref[...]
    q_ref[...] = q.astype(BF16)
    k = _group_rms(proj(aw), g_ref) * kg_ref[...]
    k_ref[...] = k.astype(BF16)

    vt = lax.dot_general(wvt_ref[...], hb, _NT_DIMS, preferred_element_type=F32).astype(BF16)
    ones = jnp.ones((BF16_SUBLANES, ATT_TK), BF16)
    for cc in range(tm // ATT_TK):
        for hh in range(ATTN_HEADS):
            vt_ref[cc, hh * V_SLAB:hh * V_SLAB + HEAD_DIM, :] = (
                vt[hh * HEAD_DIM:(hh + 1) * HEAD_DIM, cc * ATT_TK:(cc + 1) * ATT_TK])
            vt_ref[cc, hh * V_SLAB + HEAD_DIM:(hh + 1) * V_SLAB, :] = ones

    qi_ref[...] = proj(2 * aw).astype(BF16)
    ki_ref[...] = jnp.dot(hb, wki_ref[...], preferred_element_type=F32).astype(BF16)
    wit_ref[...] = lax.dot_general(wwit_ref[...], hb, _NT_DIMS, preferred_element_type=F32) * IDX_SCALE

    gate_b = proj(3 * aw)
    z = proj(4 * aw) * proj(5 * aw)

    @pl.when(j == 0)
    def _():
        carry_ref[...] = jnp.zeros_like(carry_ref)

    prev = carry_ref[...]
    row = lax.broadcasted_iota(jnp.int32, z.shape, 0)
    z1 = jnp.where(row == 0, prev[SUBLANES - 1:SUBLANES, :], pltpu.roll(z, 1, 0))
    z2 = pltpu.roll(z, 2, 0)
    z2 = jnp.where(row == 0, prev[SUBLANES - 2:SUBLANES - 1, :], z2)
    z2 = jnp.where(row == 1, prev[SUBLANES - 1:SUBLANES, :], z2)
    carry_ref[...] = z[tm - SUBLANES:, :]
    conv = cw_ref[2:3, :] * z + cw_ref[1:2, :] * z1 + cw_ref[0:1, :] * z2
    yc = gate_b * conv
    cn_ref[...] = (_group_rms(yc, g_ref) * cg_ref[...]).astype(BF16)


def _pre_call(x, mod, n1, wm, wvt, wki, wwit, qg, kg, cw, cg, gmat):
    bsz, seq, d = x.shape
    tm = PRE_TM
    nck = tm // ATT_TK
    aw = ATTN_WIDTH
    const = lambda b, j: (0, 0)
    tok = lambda b, j: (b, j, 0)
    out_shape = (
        jax.ShapeDtypeStruct((bsz, seq, aw), BF16),
        jax.ShapeDtypeStruct((bsz, seq, aw), BF16),
        jax.ShapeDtypeStruct((bsz, seq // ATT_TK, ATTN_HEADS * V_SLAB, ATT_TK), BF16),
        jax.ShapeDtypeStruct((bsz, seq, aw), BF16),
        jax.ShapeDtypeStruct((bsz, seq, LANES), BF16),
        jax.ShapeDtypeStruct((bsz, IDX_HEADS, seq), F32),
        jax.ShapeDtypeStruct((bsz, seq, CONV_WIDTH), BF16),
    )
    out_specs = (
        pl.BlockSpec((None, tm, aw), tok),
        pl.BlockSpec((None, tm, aw), tok),
        pl.BlockSpec((None, nck, ATTN_HEADS * V_SLAB, ATT_TK), lambda b, j: (b, j, 0, 0)),
        pl.BlockSpec((None, tm, aw), tok),
        pl.BlockSpec((None, tm, LANES), tok),
        pl.BlockSpec((None, IDX_HEADS, tm), lambda b, j: (b, 0, j)),
        pl.BlockSpec((None, tm, CONV_WIDTH), tok),
    )
    in_specs = [
        pl.BlockSpec((None, tm, d), tok),
        pl.BlockSpec((None, 6, d), lambda b, j: (b, 0, 0)),
        pl.BlockSpec(n1.shape, const),
        pl.BlockSpec(wm.shape, const),
        pl.BlockSpec(wvt.shape, const),
        pl.BlockSpec(wki.shape, const),
        pl.BlockSpec(wwit.shape, const),
        pl.BlockSpec(qg.shape, const),
        pl.BlockSpec(kg.shape, const),
        pl.BlockSpec(cw.shape, const),
        pl.BlockSpec(cg.shape, const),
        pl.BlockSpec(gmat.shape, const),
    ]
    return pl.pallas_call(
        _pre_kernel,
        out_shape=out_shape,
        grid=(bsz, seq // tm),
        in_specs=in_specs,
        out_specs=out_specs,
        scratch_shapes=[pltpu.VMEM((SUBLANES, CONV_WIDTH), F32)],
        compiler_params=pltpu.CompilerParams(dimension_semantics=("arbitrary", "arbitrary"),
                                             vmem_limit_bytes=VMEM_LIMIT_BYTES),
        name="pre_proj",
    )(x, mod, n1, wm, wvt, wki, wwit, qg, kg, cw, cg, gmat)


def _attn_kernel(rb_ref, bnd_ref, q_ref, qi_ref, wit_ref, k_ref, ki_ref, vt_ref, og_ref,
                 o_ref,
                 s_ref, bias_ref, qpad_ref, qipad_ref, lg_ref, acc_ref, out_ref, *, topk):
    b = pl.program_id(0)
    i = pl.program_id(1)
    tq, tk = ATT_TQ, ATT_TK
    nh, hd = ATTN_HEADS, HEAD_DIM

    t_loc = lax.broadcasted_iota(jnp.int32, (tk, tq), 1)
    s_loc = lax.broadcasted_iota(jnp.int32, (tk, tq), 0)

    @pl.when((b == 0) & (i == 0))
    def _():
        for idx in range(2):
            dist = t_loc - s_loc + idx * tq
            for h in range(nh):
                bias_ref[idx, h] = jnp.full((tk, tq), (rb_ref[0, h] - rb_ref[N_BUCKETS - 1, h]) * LOG2E, F32)

            def fill(jb, carry):
                reached = dist >= bnd_ref[jb]
                for h in range(nh):
                    val = (rb_ref[jb, h] - rb_ref[N_BUCKETS - 1, h]) * LOG2E
                    bias_ref[idx, h] = jnp.where(reached, val, bias_ref[idx, h])
                return carry

            lax.fori_loop(1, N_BUCKETS, fill, 0)

    lane = lax.broadcasted_iota(jnp.int32, (tq, LANES), 1)
    for h in range(nh):
        pair = slice((h // 2) * LANES, (h // 2 + 1) * LANES)
        keep = (lane // hd) == (h % 2)
        qpad_ref[h] = jnp.where(keep, q_ref[:, pair], jnp.zeros((), BF16))
        qipad_ref[h] = jnp.where(keep, qi_ref[:, pair], jnp.zeros((), BF16))

    def score_chunk(c):
        kic = ki_ref[pl.ds(pl.multiple_of(c * tk, tk), tk), :]
        acc = jnp.zeros((tk, tq), F32)
        for h in range(nh):
            e = lax.dot_general(kic, qipad_ref[h], _NT_DIMS, preferred_element_type=F32)
            acc = acc + wit_ref[h:h + 1, :] * jnp.maximum(e, 0.0)
        return acc

    def a_body(c, carry):
        rmin, rmax = carry
        sc = score_chunk(c)
        s_ref[c] = sc
        return (jnp.minimum(rmin, jnp.min(sc, axis=0, keepdims=True)),
                jnp.maximum(rmax, jnp.max(sc, axis=0, keepdims=True)))

    rmin0 = jnp.full((1, tq), jnp.inf, F32)
    rmax0 = jnp.full((1, tq), -jnp.inf, F32)
    rmin, rmax = lax.fori_loop(0, i, a_body, (rmin0, rmax0))
    sc = score_chunk(i)
    causal = s_loc <= t_loc
    s_ref[i] = jnp.where(causal, sc, -jnp.inf)
    rmin = jnp.minimum(rmin, jnp.min(jnp.where(causal, sc, jnp.inf), axis=0, keepdims=True))
    rmax = jnp.maximum(rmax, jnp.max(jnp.where(causal, sc, -jnp.inf), axis=0, keepdims=True))

    def count_ge(thr):
        def body(c, accs):
            hit = s_ref[c] >= thr
            accs = list(accs)
            for r in range(tk // SUBLANES):
                a = accs[r % COUNT_ACCS]
                accs[r % COUNT_ACCS] = jnp.where(hit[r * SUBLANES:(r + 1) * SUBLANES], a + 1.0, a)
            return tuple(accs)
        accs = lax.fori_loop(0, i + 1, body,
                             tuple(jnp.zeros((SUBLANES, tq), F32) for _ in range(COUNT_ACCS)))
        return jnp.sum(_tree_sum(list(accs)), axis=0, keepdims=True)

    def order_key(v):
        bits = pltpu.bitcast(v, jnp.int32)
        return jnp.where(bits < 0, bits ^ jnp.int32(0x7FFFFFFF), bits)

    def from_order_key(key):
        return pltpu.bitcast(jnp.where(key < 0, key ^ jnp.int32(0x7FFFFFFF), key), F32)

    t_glob = (i * tq + lax.broadcasted_iota(jnp.int32, (1, tq), 1)).astype(F32)
    n_causal = t_glob + 1.0
    kf = jnp.minimum(float(topk), n_causal)
    c_max = count_ge(rmax)
    all_sel = n_causal <= kf
    max_ge = c_max >= kf
    active0 = jnp.where(all_sel | max_ge, 0.0, 1.0)
    thr0 = jnp.where(all_sel, rmin, rmax)
    tie0 = jnp.where(jnp.logical_not(all_sel) & (c_max > kf), 1.0, 0.0)
    hif0 = jnp.full((1, tq), jnp.inf, F32)
    need0 = kf

    def b_cond(st):
        return (jnp.max(st[0]) > 0.0) & (st[8] <= BISECT_MAX_STEPS)

    def b_body(st):
        active, lo, hi, fhi, thr, tie, hif, need, step = st
        lo_key = order_key(lo)
        hi_key = order_key(hi)
        mid_key = (lo_key >> 1) + (hi_key >> 1) + (lo_key & hi_key & 1)
        mid_val = lo + (hi - lo) * 0.5
        use_val = (step < BISECT_VALUE_STEPS) & (mid_val > lo) & (mid_val < hi)
        mid = jnp.where(use_val, mid_val, from_order_key(mid_key))
        collapsed = (mid_key == lo_key) | (step >= BISECT_MAX_STEPS)
        cm = count_ge(mid)
        act = active > 0.0
        live = act & jnp.logical_not(collapsed)
        found = live & (cm == kf)
        go_up = live & (cm > kf)
        go_dn = live & (cm < kf)
        ends_tie = act & collapsed
        thr = jnp.where(found, mid, jnp.where(ends_tie, lo, thr))
        tie = jnp.where(ends_tie, 1.0, tie)
        hif = jnp.where(ends_tie, hi, hif)
        need = jnp.where(ends_tie, kf - fhi, need)
        lo = jnp.where(go_up, mid, lo)
        fhi = jnp.where(go_dn, cm, fhi)
        hi = jnp.where(go_dn, mid, hi)
        active = jnp.where(found | ends_tie, 0.0, active)
        return active, lo, hi, fhi, thr, tie, hif, need, step + 1

    def b_group(st):
        for _ in range(BISECT_GROUP):
            st = b_body(st)
        return st

    _, _, _, _, thr, tie, hif, need, _ = lax.while_loop(
        b_cond, b_group, (active0, rmin, rmax, c_max, thr0, tie0, hif0, need0, jnp.int32(0)))

    @pl.when(jnp.max(tie) > 0.0)
    def _():
        tri = jnp.where(lax.broadcasted_iota(jnp.int32, (tk, tk), 1)
                        <= lax.broadcasted_iota(jnp.int32, (tk, tk), 0), 1.0, 0.0).astype(BF16)

        def body(c, seen):
            sc_c = s_ref[c]
            tied = (sc_c >= thr) & (sc_c < hif) & (tie > 0.0)
            rank = jnp.dot(tri, jnp.where(tied, 1.0, 0.0).astype(BF16), preferred_element_type=F32) + seen
            s_ref[c] = jnp.where(tied & (rank > need), -jnp.inf, sc_c)
            return rank[tk - 1:tk, :]

        lax.fori_loop(0, i + 1, body, jnp.zeros((1, tq), F32))

    acc_ref[...] = jnp.zeros(acc_ref.shape, F32)

    def store_logits(c, slot, bias_idx):
        masked = jnp.where(s_ref[c] >= thr, 0.0, NEG_BIG)
        row0 = pl.multiple_of(c * tk, tk)
        for h in range(nh):
            kc = k_ref[pl.ds(row0, tk), (h // 2) * LANES:(h // 2 + 1) * LANES]
            lt = lax.dot_general(kc, qpad_ref[h], _NT_DIMS, preferred_element_type=F32) + masked
            if bias_idx is not None:
                lt = lt + bias_ref[bias_idx, h]
            lg_ref[slot, h] = lt

    def softmax_pv(c, slot, m_all):
        m_out = []
        for h in range(nh):
            m_old = m_all[h]
            m_new = jnp.maximum(m_old, jnp.max(lg_ref[slot, h], axis=0, keepdims=True))
            p = jnp.exp2(lg_ref[slot, h] - m_new).astype(BF16)
            alpha = jnp.exp2(m_old - m_new)
            pv = jnp.dot(vt_ref[c, h * V_SLAB:(h + 1) * V_SLAB, :], p, preferred_element_type=F32)
            acc_ref[h] = alpha * acc_ref[h] + pv
            m_out.append(m_new)
        return tuple(m_out)

    def near_step(m_all):
        store_logits(i - 1, 1, 1)
        return softmax_pv(i, 0, m_all)

    def far_step(j, parity, m_all):
        c = i - 2 - j
        store_logits(c, parity, None)
        return softmax_pv(c + 1, 1 - parity, m_all)

    def far_pair(jj, m_all):
        return far_step(2 * jj + 1, 1, far_step(2 * jj, 0, m_all))

    n_far = jnp.maximum(i - 1, 0)
    m_all = tuple(jnp.full((1, tq), NEG_BIG, F32) for _ in range(nh))
    store_logits(i, 0, 0)
    m_all = lax.cond(i >= 1, near_step, lambda m: m, m_all)
    m_all = lax.fori_loop(0, n_far // 2, far_pair, m_all)
    m_all = lax.cond((n_far & 1) == 1, lambda m: far_step(n_far - 1, 0, m), lambda m: m, m_all)
    lax.cond((i & 1) == 0, lambda m: softmax_pv(0, 0, m), lambda m: softmax_pv(0, 1, m), m_all)

    for h in range(nh):
        o = acc_ref[h, :hd, :] / acc_ref[h, hd:hd + 1, :]
        ms = jnp.mean(o * o, axis=0, keepdims=True)
        out_ref[h * hd:(h + 1) * hd, :] = o * lax.rsqrt(ms + EPS)
    o_ref[...] = (out_ref[...].T * og_ref[...]).astype(BF16)


def _attn_call(rel_bias, bounds, q, qi, wit, k, ki, vt, og, topk):
    bsz, seq, aw = q.shape
    tq, tk = ATT_TQ, ATT_TK
    nck = seq // tk
    blk_q = lambda b, i: (b, i, 0)
    whole = lambda b, i: (b, 0, 0)
    smem = pl.BlockSpec(memory_space=pltpu.SMEM)
    return pl.pallas_call(
        functools.partial(_attn_kernel, topk=topk),
        out_shape=jax.ShapeDtypeStruct((bsz, seq, aw), BF16),
        grid=(bsz, seq // tq),
        in_specs=[
            smem, smem,
            pl.BlockSpec((None, tq, aw), blk_q),
            pl.BlockSpec((None, tq, aw), blk_q),
            pl.BlockSpec((None, IDX_HEADS, tq), lambda b, i: (b, 0, i)),
            pl.BlockSpec((None, seq, aw), whole),
            pl.BlockSpec((None, seq, LANES), whole),
            pl.BlockSpec((None, nck, ATTN_HEADS * V_SLAB, tk), lambda b, i: (b, 0, 0, 0)),
            pl.BlockSpec(og.shape, lambda b, i: (0, 0)),
        ],
        out_specs=pl.BlockSpec((None, tq, aw), blk_q),
        scratch_shapes=[
            pltpu.VMEM((nck, tk, tq), F32),
            pltpu.VMEM((2, ATTN_HEADS, tk, tq), F32),
            pltpu.VMEM((ATTN_HEADS, tq, LANES), BF16),
            pltpu.VMEM((IDX_HEADS, tq, LANES), BF16),
            pltpu.VMEM((2, ATTN_HEADS, tk, tq), F32),
            pltpu.VMEM((ATTN_HEADS, V_SLAB, tq), F32),
            pltpu.VMEM((aw, tq), F32),
        ],
        compiler_params=pltpu.CompilerParams(dimension_semantics=("arbitrary", "arbitrary"),
                                             vmem_limit_bytes=VMEM_LIMIT_BYTES),
        name="dsa_attention",
    )(rel_bias, bounds, q, qi, wit, k, ki, vt, og)


def _post_kernel(an_ref, cn_ref, x_ref, mod_ref, n2_ref, woa_ref, woc_ref, wr_ref, br_ref,
                 x1_ref, h2_ref, comb_ref, cnt_ref):
    mix = (jnp.dot(an_ref[...], woa_ref[...], preferred_element_type=F32)
           + jnp.dot(cn_ref[...], woc_ref[...], preferred_element_type=F32))
    x1 = x_ref[...] + mod_ref[2:3, :] * mix
    x1_ref[...] = x1
    ms = jnp.mean(x1 * x1, axis=-1, keepdims=True)
    h2 = x1 * lax.rsqrt(ms + EPS) * n2_ref[...] * (1.0 + mod_ref[4:5, :]) + mod_ref[3:4, :]
    h2b = h2.astype(BF16)
    h2_ref[...] = h2b

    logits = jnp.dot(h2b, wr_ref[...], preferred_element_type=F32) + br_ref[...]
    lane = lax.broadcasted_iota(jnp.int32, logits.shape, 1)
    lane_f = lane.astype(F32)
    far = float(LANES)
    is_g = (lane >= N_EXPERTS) & (lane < N_EXPERTS + N_GROUPS)
    gl = jnp.where(is_g, logits, -jnp.inf)
    gmax = jnp.max(gl, axis=-1, keepdims=True)
    g_sel = jnp.min(jnp.where(is_g & (gl == gmax), lane_f, far), axis=-1, keepdims=True) - float(N_EXPERTS)
    p_g = 1.0 / jnp.sum(jnp.exp(gl - gmax), axis=-1, keepdims=True)

    in_grp = (lane < N_EXPERTS) & ((lane // EXPERTS_PER_GROUP).astype(F32) == g_sel)
    e1 = jnp.where(in_grp, logits, -jnp.inf)
    l1 = jnp.max(e1, axis=-1, keepdims=True)
    i1 = jnp.min(jnp.where(in_grp & (e1 == l1), lane_f, far), axis=-1, keepdims=True)
    rest = in_grp & (lane_f != i1)
    e2 = jnp.where(rest, logits, -jnp.inf)
    l2 = jnp.max(e2, axis=-1, keepdims=True)
    i2 = jnp.min(jnp.where(rest & (e2 == l2), lane_f, far), axis=-1, keepdims=True)
    r = jnp.exp(l2 - l1)
    w1 = 1.0 / (1.0 + r)
    w2 = r / (1.0 + r)
    comb = jnp.where(lane_f == i1, p_g * w1, 0.0) + jnp.where(lane_f == i2, p_g * w2, 0.0)
    comb_ref[...] = comb
    cnt = jnp.sum(jnp.where(comb != 0.0, 1.0, 0.0), axis=0, keepdims=True)
    cnt_ref[...] = jnp.broadcast_to(cnt, cnt_ref.shape)


def _post_call(an, cn, x, mod, n2, woa, woc, wr, br):
    bsz, seq, d = x.shape
    tm = POST_TM
    tok = lambda b, j: (b, j, 0)
    const = lambda b, j: (0, 0)
    return pl.pallas_call(
        _post_kernel,
        out_shape=(jax.ShapeDtypeStruct((bsz, seq, d), F32),
                   jax.ShapeDtypeStruct((bsz, seq, d), BF16),
                   jax.ShapeDtypeStruct((bsz, seq, LANES), F32),
                   jax.ShapeDtypeStruct((bsz, seq // tm, SUBLANES, LANES), F32)),
        grid=(bsz, seq // tm),
        in_specs=[
            pl.BlockSpec((None, tm, ATTN_WIDTH), tok),
            pl.BlockSpec((None, tm, CONV_WIDTH), tok),
            pl.BlockSpec((None, tm, d), tok),
            pl.BlockSpec((None, 6, d), lambda b, j: (b, 0, 0)),
            pl.BlockSpec(n2.shape, const),
            pl.BlockSpec(woa.shape, const),
            pl.BlockSpec(woc.shape, const),
            pl.BlockSpec(wr.shape, const),
            pl.BlockSpec(br.shape, const),
        ],
        out_specs=(pl.BlockSpec((None, tm, d), tok),
                   pl.BlockSpec((None, tm, d), tok),
                   pl.BlockSpec((None, tm, LANES), tok),
                   pl.BlockSpec((None, None, SUBLANES, LANES), lambda b, j: (b, j, 0, 0))),
        compiler_params=pltpu.CompilerParams(dimension_semantics=("arbitrary", "arbitrary"),
                                             vmem_limit_bytes=VMEM_LIMIT_BYTES),
        name="post_router",
    )(an, cn, x, mod, n2, woa, woc, wr, br)


def _strict_tri(n, lower):
    r = lax.broadcasted_iota(jnp.int32, (n, n), 0)
    c = lax.broadcasted_iota(jnp.int32, (n, n), 1)
    return jnp.where((c < r) if lower else (r < c), 1.0, 0.0).astype(BF16)


def _moe_tile_copies(ntile_ref, lfirst_ref, gfirst_ref, blk, local_ref, global_ref, sem, to_global, wait):
    tile = MOE_TILE

    def per_expert(x, carry):
        lf = lfirst_ref[blk, x]
        gf = gfirst_ref[blk, x]

        def per_tile(j, c):
            loc = local_ref.at[pl.ds(pl.multiple_of((lf + j) * tile, tile), tile), :]
            glo = global_ref.at[pl.ds(pl.multiple_of((gf + j) * tile, tile), tile), :]
            cp = pltpu.make_async_copy(loc, glo, sem) if to_global else pltpu.make_async_copy(glo, loc, sem)
            if wait:
                cp.wait()
            else:
                cp.start()
            return c

        lax.fori_loop(0, ntile_ref[blk, x], per_tile, 0)
        return carry

    lax.fori_loop(0, N_EXPERTS, per_expert, 0)


def _moe_gather_kernel(ntile_ref, lfirst_ref, gfirst_ref, pad_ref,
                       h2_ref, comb_ref,
                       col_ref, xg_hbm,
                       xg_ref, row_ref, zero_ref, sem):
    blk = pl.program_id(0)
    nb = h2_ref.shape[0]
    tile, chunk = MOE_TILE, MOE_CHUNK
    lane = lax.broadcasted_iota(jnp.int32, (nb, LANES), 1)

    comb = comb_ref[...]
    assigned = comb != 0.0
    a_f = jnp.where(assigned, 1.0, 0.0)
    rank = jnp.dot(_strict_tri(nb, True), a_f.astype(BF16), preferred_element_type=F32)
    cnt = rank[nb - 1:nb, :] + a_f[nb - 1:nb, :]
    ntile = jnp.floor((cnt + float(tile - 1)) * (1.0 / tile))
    first = jnp.dot(jnp.broadcast_to(ntile, (SUBLANES, LANES)).astype(BF16), _strict_tri(LANES, False),
                    preferred_element_type=F32)[0:1, :]
    pos = first * float(tile) + rank
    pos1 = jnp.min(jnp.where(assigned, pos, 1e9), axis=1, keepdims=True)
    pos2 = jnp.max(jnp.where(assigned, pos, -1.0), axis=1, keepdims=True)
    pos2 = jnp.where(pos2 == pos1, -1.0, pos2)
    cw1 = jnp.sum(jnp.where(assigned & (pos == pos1), comb, 0.0), axis=1, keepdims=True)
    cw2 = jnp.sum(jnp.where(assigned & (pos == pos2), comb, 0.0), axis=1, keepdims=True)
    info = jnp.where(lane == 0, pos1, jnp.where(lane == 1, pos2, jnp.where(lane == 2, cw1,
                     jnp.where(lane == 3, cw2, 0.0))))
    col_ref[...] = info
    row_ref[...] = info.T

    total = lfirst_ref[blk, N_EXPERTS - 1] + ntile_ref[blk, N_EXPERTS - 1]
    n_chunks = (total * tile + (chunk - 1)) // chunk
    p1 = row_ref[0:1, :].astype(jnp.int32)
    p2 = row_ref[1:2, :].astype(jnp.int32)
    sub = lax.broadcasted_iota(jnp.int32, (chunk, nb), 0)

    def gather(c, carry):
        p = sub + c * chunk
        sel = jnp.where((p == p1) | (p == p2), 1.0, 0.0).astype(BF16)
        r0 = pl.multiple_of(c * chunk, chunk)
        xg_ref[pl.ds(r0, chunk), :] = jnp.dot(sel, h2_ref[...], preferred_element_type=F32).astype(BF16)
        return carry

    @pl.when(blk > 0)
    def _():
        _moe_tile_copies(ntile_ref, lfirst_ref, gfirst_ref, blk - 1, xg_ref, xg_hbm, sem, True, True)

    lax.fori_loop(0, n_chunks, gather, 0)

    _moe_tile_copies(ntile_ref, lfirst_ref, gfirst_ref, blk, xg_ref, xg_hbm, sem, True, False)

    is_last = blk == pl.num_programs(0) - 1

    def pad_copies(wait):
        def per_expert(x, carry):
            g0 = gfirst_ref[blk, x] + ntile_ref[blk, x]

            def per_tile(j, c):
                dst = xg_hbm.at[pl.ds(pl.multiple_of((g0 + j) * tile, tile), tile), :]
                cp = pltpu.make_async_copy(zero_ref, dst, sem)
                if wait:
                    cp.wait()
                else:
                    cp.start()
                return c

            lax.fori_loop(0, pad_ref[x], per_tile, 0)
            return carry

        lax.fori_loop(0, N_EXPERTS, per_expert, 0)

    @pl.when(is_last)
    def _():
        zero_ref[...] = jnp.zeros(zero_ref.shape, BF16)
        pad_copies(False)
        _moe_tile_copies(ntile_ref, lfirst_ref, gfirst_ref, blk, xg_ref, xg_hbm, sem, True, True)
        pad_copies(True)


def _moe_ffn_kernel(texp_ref, nt_ref, x_ref, wgu_ref, wd_ref, y_ref):
    @pl.when(pl.program_id(0) < nt_ref[0])
    def _():
        ab = jnp.dot(x_ref[...], wgu_ref[...], preferred_element_type=F32)
        a = ab[:, :EXPERT_FF]
        hid = ((a * jax.nn.sigmoid(a)) * ab[:, EXPERT_FF:]).astype(BF16)
        y_ref[...] = jnp.dot(hid, wd_ref[...], preferred_element_type=F32).astype(BF16)


def _moe_scatter_kernel(ntile_ref, lfirst_ref, gfirst_ref,
                        col_ref, x1_ref, mod_ref, y_hbm,
                        o_ref,
                        y_ref, sem):
    blk = pl.program_id(0)
    nb = x1_ref.shape[0]
    tile, chunk = MOE_TILE, MOE_CHUNK
    slot = blk & 1

    def copies(b, s, wait):
        _moe_tile_copies(ntile_ref, lfirst_ref, gfirst_ref, b, y_ref.at[s], y_hbm, sem.at[s], False, wait)

    @pl.when(blk == 0)
    def _():
        copies(0, 0, False)

    @pl.when(blk + 1 < pl.num_programs(0))
    def _():
        copies(blk + 1, 1 - slot, False)

    total = lfirst_ref[blk, N_EXPERTS - 1] + ntile_ref[blk, N_EXPERTS - 1]
    n_chunks = (total * tile + (chunk - 1)) // chunk

    def clear(t, carry):
        y_ref[slot, pl.ds(pl.multiple_of(t * tile, tile), tile), :] = jnp.zeros((tile, y_ref.shape[2]), BF16)
        return carry

    lax.fori_loop(total, n_chunks * (chunk // tile), clear, 0)

    p1 = col_ref[:, 0:1].astype(jnp.int32)
    p2 = col_ref[:, 1:2].astype(jnp.int32)
    cw1 = col_ref[:, 2:3]
    cw2 = col_ref[:, 3:4]
    gate = mod_ref[5:6, :]
    lane_c = lax.broadcasted_iota(jnp.int32, (nb, chunk), 1)
    o_ref[...] = x1_ref[...]
    copies(blk, slot, True)

    def scatter(c, carry):
        p = lane_c + c * chunk
        w = (jnp.where(p == p1, cw1, 0.0) + jnp.where(p == p2, cw2, 0.0)).astype(BF16)
        r0 = pl.multiple_of(c * chunk, chunk)
        o_ref[...] += gate * jnp.dot(w, y_ref[slot, pl.ds(r0, chunk), :], preferred_element_type=F32)
        return carry

    lax.fori_loop(0, n_chunks, scatter, 0)


def _moe_call(h2, comb, cnt_tiles, x1, mod, wgu, wd):
    bsz, seq, d = x1.shape
    nb, tile, ftm = MOE_TM, MOE_TILE, MOE_FFN_TM
    n_tok = bsz * seq
    n_blk = n_tok // nb
    region = ftm // tile
    rows_local = -(-(2 * nb + N_EXPERTS * tile) // MOE_CHUNK) * MOE_CHUNK
    tiles_global = (2 * n_tok) // tile + n_blk * N_EXPERTS + N_EXPERTS * (region - 1)
    n_ffn_max = -(-tiles_global // region)
    rows_global = n_ffn_max * ftm

    cnt = cnt_tiles[:, :, 0, :N_EXPERTS].reshape(n_blk, nb // POST_TM, N_EXPERTS).sum(axis=1).astype(jnp.int32)
    ntile = (cnt + (tile - 1)) // tile
    lfirst = jnp.cumsum(ntile, axis=1) - ntile
    tot = ntile.sum(axis=0)
    ptot = (tot + (region - 1)) // region * region
    ebase = jnp.cumsum(ptot) - ptot
    gfirst = ebase[None, :] + jnp.cumsum(ntile, axis=0) - ntile
    pad = ptot - tot
    n_ffn = (ptot.sum() // region).reshape(1)
    ends = jnp.cumsum(ptot) // region
    texp = jnp.minimum((jnp.arange(n_ffn_max, dtype=jnp.int32)[:, None] >= ends[None, :]).sum(axis=1),
                       N_EXPERTS - 1).astype(jnp.int32)

    h2f = h2.reshape(n_tok, d)
    combf = comb.reshape(n_tok, LANES)
    col, xg = pl.pallas_call(
        _moe_gather_kernel,
        out_shape=(jax.ShapeDtypeStruct((n_tok, LANES), F32),
                   jax.ShapeDtypeStruct((rows_global, d), BF16)),
        grid_spec=pltpu.PrefetchScalarGridSpec(
            num_scalar_prefetch=4,
            grid=(n_blk,),
            in_specs=[pl.BlockSpec((nb, d), lambda j, *_: (j, 0)),
                      pl.BlockSpec((nb, LANES), lambda j, *_: (j, 0))],
            out_specs=(pl.BlockSpec((nb, LANES), lambda j, *_: (j, 0)),
                       pl.BlockSpec(memory_space=pl.ANY)),
            scratch_shapes=[
                pltpu.VMEM((rows_local, d), BF16),
                pltpu.VMEM((LANES, nb), F32),
                pltpu.VMEM((tile, d), BF16),
                pltpu.SemaphoreType.DMA,
            ]),
        compiler_params=pltpu.CompilerParams(dimension_semantics=("arbitrary",),
                                             vmem_limit_bytes=VMEM_LIMIT_BYTES),
        name="moe_gather",
    )(ntile, lfirst, gfirst, pad, h2f, combf)

    last = lambda t, te, nt: jnp.minimum(t, nt[0] - 1)
    y = pl.pallas_call(
        _moe_ffn_kernel,
        out_shape=jax.ShapeDtypeStruct((rows_global, d), BF16),
        grid_spec=pltpu.PrefetchScalarGridSpec(
            num_scalar_prefetch=2,
            grid=(n_ffn_max,),
            in_specs=[pl.BlockSpec((ftm, d), lambda t, te, nt: (last(t, te, nt), 0)),
                      pl.BlockSpec((None, d, 2 * EXPERT_FF), lambda t, te, nt: (te[last(t, te, nt)], 0, 0)),
                      pl.BlockSpec((None, EXPERT_FF, d), lambda t, te, nt: (te[last(t, te, nt)], 0, 0))],
            out_specs=pl.BlockSpec((ftm, d), lambda t, te, nt: (last(t, te, nt), 0))),
        compiler_params=pltpu.CompilerParams(dimension_semantics=("arbitrary",),
                                             vmem_limit_bytes=VMEM_LIMIT_BYTES),
        name="moe_ffn",
    )(texp, n_ffn, xg, wgu, wd)

    out = pl.pallas_call(
        _moe_scatter_kernel,
        out_shape=jax.ShapeDtypeStruct((n_tok, d), F32),
        grid_spec=pltpu.PrefetchScalarGridSpec(
            num_scalar_prefetch=3,
            grid=(n_blk,),
            in_specs=[pl.BlockSpec((nb, LANES), lambda j, *_: (j, 0)),
                      pl.BlockSpec((nb, d), lambda j, *_: (j, 0)),
                      pl.BlockSpec((None, 6, d), lambda j, *_: ((j * nb) // seq, 0, 0)),
                      pl.BlockSpec(memory_space=pl.ANY)],
            out_specs=pl.BlockSpec((nb, d), lambda j, *_: (j, 0)),
            scratch_shapes=[pltpu.VMEM((2, rows_local, d), BF16),
                            pltpu.SemaphoreType.DMA((2,))]),
        compiler_params=pltpu.CompilerParams(dimension_semantics=("arbitrary",),
                                             vmem_limit_bytes=VMEM_LIMIT_BYTES),
        name="moe_scatter",
    )(ntile, lfirst, gfirst, col, x1.reshape(n_tok, d), mod, y)
    return out.reshape(bsz, seq, d)


def _layer(x, mod, rel_bias, norm1, w_in, q_norm, k_norm, conv_w, attn_out_norm, conv_out_norm, w_out,
           norm2, w_group_router, b_group_router, w_expert_router, b_expert_router, w_gate, w_up, w_down):
    bsz, seq, d = x.shape
    aw = ATTN_WIDTH
    topk = min(TOPK_MAX, seq // 4)

    offs = np.cumsum([0, aw, aw, aw, IDX_HEADS * IDX_DIM, IDX_DIM, IDX_HEADS, CONV_WIDTH, CONV_WIDTH, CONV_WIDTH])
    col = lambda n: w_in[:, int(offs[n]):int(offs[n + 1])]
    wm = jnp.concatenate([col(0), col(1), col(3), col(6), col(7), col(8)], axis=1).astype(BF16)
    wvt = col(2).T.astype(BF16)
    wki = jnp.concatenate([col(4), col(4)], axis=1).astype(BF16)
    wwit = col(5).T.astype(BF16)
    qg = (jnp.tile(q_norm, ATTN_HEADS) * ((HEAD_DIM ** -0.5) * LOG2E))[None, :]
    kg = jnp.tile(k_norm, ATTN_HEADS)[None, :]
    grp = np.arange(aw) // CONV_GROUP_DIM
    gmat = jnp.asarray((grp[:, None] == grp[None, :]).astype(np.float32) / CONV_GROUP_DIM, dtype=BF16)

    q, k, vt, qi, ki, wit, cn = _pre_call(
        x, mod, norm1[None, :], wm, wvt, wki, wwit, qg, kg, conv_w, conv_out_norm.reshape(1, -1), gmat)

    bounds = jnp.asarray(_bucket_boundaries())
    an = _attn_call(rel_bias, bounds, q, qi, wit, k, ki, vt, attn_out_norm.reshape(1, -1), topk)

    wr = jnp.concatenate([w_expert_router, w_group_router,
                          jnp.zeros((d, LANES - N_EXPERTS - N_GROUPS), F32)], axis=1).astype(BF16)
    br = jnp.concatenate([b_expert_router, b_group_router,
                          jnp.zeros((LANES - N_EXPERTS - N_GROUPS,), F32)])[None, :]
    x1, h2, comb, cnt_tiles = _post_call(an, cn, x, mod, norm2[None, :], w_out[:aw].astype(BF16),
                                         w_out[aw:].astype(BF16), wr, br)

    wgu = jnp.concatenate([w_gate, w_up], axis=-1).astype(BF16)
    return _moe_call(h2, comb, cnt_tiles, x1, mod, wgu, w_down.astype(BF16))


def kernel(x, c, rel_bias, w_ada, b_ada, norm1, w_in, q_norm, k_norm, conv_w, attn_out_norm, conv_out_norm,
           w_out, norm2, w_group_router, b_group_router, w_expert_router, b_expert_router, w_gate, w_up,
           w_down):
    bsz, seq, d = x.shape
    assert d == D_MODEL and seq % max(PRE_TM, POST_TM, MOE_TM) == 0 and ATT_TQ == ATT_TK
    depth = w_ada.shape[0]
    for l in range(depth):
        mod = _mod_call(c, w_ada[l], b_ada[l][None, :]).reshape(bsz, 6, d)
        x = _layer(x, mod, rel_bias, norm1[l], w_in[l], q_norm[l], k_norm[l], conv_w[l], attn_out_norm[l],
                   conv_out_norm[l], w_out[l], norm2[l], w_group_router[l], b_group_router[l],
                   w_expert_router[l], b_expert_router[l], w_gate[l], w_up[l], w_down[l])
    return x
```

```python
import functools
import math

import jax
import jax.numpy as jnp
import numpy as np
from jax import lax
from jax.experimental import pallas as pl
from jax.experimental.pallas import tpu as pltpu

F32 = jnp.float32
BF16 = jnp.bfloat16

D_MODEL = 1024
HEAD_DIM = 64
ATTN_HEADS = 8
ATTN_WIDTH = ATTN_HEADS * HEAD_DIM
CONV_WIDTH = D_MODEL - ATTN_WIDTH
CONV_GROUP_DIM = 64
CONV_K = 3
IDX_HEADS = 8
IDX_DIM = 64
TOPK_MAX = 256
IDX_SCALE = (IDX_DIM ** -0.5) * (IDX_HEADS ** -0.5)
N_BUCKETS = 32
MAX_DISTANCE = 128
N_GROUPS = 4
EXPERTS_PER_GROUP = 8
N_EXPERTS = N_GROUPS * EXPERTS_PER_GROUP
EXPERT_FF = 256
EPS = 1e-6
LOG2E = 1.4426950408889634
NEG_BIG = -1e30
COUNT_ACCS = 4
BISECT_GROUP = 4
BISECT_VALUE_STEPS = 8
BISECT_MAX_STEPS = 64

LANES = 128
SUBLANES = 8
BF16_SUBLANES = 16
V_SLAB = HEAD_DIM + BF16_SUBLANES
VMEM_LIMIT_BYTES = 56 * 1024 * 1024

PRE_TM = 512
ATT_TQ = 256
ATT_TK = 256
POST_TM = 512
MOE_TM = 1024
MOE_TILE = 64
MOE_CHUNK = 512
MOE_USUAL_CHUNKS = 6
MOE_FFN_TM = 1024
MOD_TN = 1536

_NT_DIMS = (((1,), (1,)), ((), ()))


def _tree_sum(parts):
    while len(parts) > 1:
        nxt = [parts[j] + parts[j + 1] for j in range(0, len(parts) - 1, 2)]
        if len(parts) % 2:
            nxt.append(parts[-1])
        parts = nxt
    return parts[0]


def _bucket_boundaries():
    max_exact = N_BUCKETS // 2
    d = np.arange(0, 4 * MAX_DISTANCE, dtype=np.int64)
    nf = np.maximum(d, 1).astype(np.float32)
    large = max_exact + (np.log(nf / np.float32(max_exact)) / np.float32(math.log(MAX_DISTANCE / max_exact))
                         * np.float32(N_BUCKETS - max_exact)).astype(np.int32)
    large = np.minimum(large, N_BUCKETS - 1)
    bucket = np.where(d < max_exact, d, large)
    assert np.all(np.diff(bucket) >= 0) and bucket[-1] == N_BUCKETS - 1
    bounds = [int(np.argmax(bucket >= j)) for j in range(1, N_BUCKETS)]
    return np.asarray([0] + bounds, dtype=np.int32)


def _mod_kernel(c_ref, w_ref, b_ref, o_ref):
    c = c_ref[...]
    act = c * jax.nn.sigmoid(c)
    o_ref[...] = jnp.dot(act, w_ref[...], preferred_element_type=F32,
                         precision=lax.Precision.HIGHEST) + b_ref[...]


def _mod_call(c, w_ada, b_ada):
    bsz, d = c.shape
    n = w_ada.shape[1]
    return pl.pallas_call(
        _mod_kernel,
        out_shape=jax.ShapeDtypeStruct((bsz, n), F32),
        grid=(n // MOD_TN,),
        in_specs=[pl.BlockSpec((bsz, d), lambda j: (0, 0)),
                  pl.BlockSpec((d, MOD_TN), lambda j: (0, j)),
                  pl.BlockSpec((1, MOD_TN), lambda j: (0, j))],
        out_specs=pl.BlockSpec((bsz, MOD_TN), lambda j: (0, j)),
        compiler_params=pltpu.CompilerParams(dimension_semantics=("arbitrary",),
                                             vmem_limit_bytes=VMEM_LIMIT_BYTES),
        name="adaln_mod",
    )(c, w_ada, b_ada)


def _group_rms(y, g_ref):
    ms = jnp.dot((y * y).astype(BF16), g_ref[...], preferred_element_type=F32)
    return y * lax.rsqrt(ms + EPS)


def _pre_kernel(x_ref, mod_ref, n1_ref, wm_ref, wvt_ref, wki_ref, wwit_ref, qg_ref, kg_ref,
                cw_ref, cg_ref, g_ref,
                q_ref, k_ref, vt_ref, qi_ref, ki_ref, wit_ref, cn_ref, carry_ref):
    j = pl.program_id(1)
    tm = x_ref.shape[0]
    aw = ATTN_WIDTH

    x = x_ref[...]
    ms = jnp.mean(x * x, axis=-1, keepdims=True)
    y = x * lax.rsqrt(ms + EPS) * n1_ref[...]
    h = y * (1.0 + mod_ref[1:2, :]) + mod_ref[0:1, :]
    hb = h.astype(BF16)

    def proj(lo):
        return jnp.dot(hb, wm_ref[:, lo:lo + aw], preferred_element_type=F32)

    q = _group_rms(proj(0), g_ref) * qg_ref[...]
    q_ref[...] = q.astype(BF16)
    k = _group_rms(proj(aw), g_ref) * kg_ref[...]
    k_ref[...] = k.astype(BF16)

    vt = lax.dot_general(wvt_ref[...], hb, _NT_DIMS, preferred_element_type=F32).astype(BF16)
    ones = jnp.ones((BF16_SUBLANES, ATT_TK), BF16)
    for cc in range(tm // ATT_TK):
        for hh in range(ATTN_HEADS):
            vt_ref[cc, hh * V_SLAB:hh * V_SLAB + HEAD_DIM, :] = (
                vt[hh * HEAD_DIM:(hh + 1) * HEAD_DIM, cc * ATT_TK:(cc + 1) * ATT_TK])
            vt_ref[cc, hh * V_SLAB + HEAD_DIM:(hh + 1) * V_SLAB, :] = ones

    qi_ref[...] = proj(2 * aw).astype(BF16)
    ki_ref[...] = jnp.dot(hb, wki_ref[...], preferred_element_type=F32).astype(BF16)
    wit_ref[...] = lax.dot_general(wwit_ref[...], hb, _NT_DIMS, preferred_element_type=F32) * IDX_SCALE

    gate_b = proj(3 * aw)
    z = proj(4 * aw) * proj(5 * aw)

    @pl.when(j == 0)
    def _():
        carry_ref[...] = jnp.zeros_like(carry_ref)

    prev = carry_ref[...]
    row = lax.broadcasted_iota(jnp.int32, z.shape, 0)
    z1 = jnp.where(row == 0, prev[SUBLANES - 1:SUBLANES, :], pltpu.roll(z, 1, 0))
    z2 = pltpu.roll(z, 2, 0)
    z2 = jnp.where(row == 0, prev[SUBLANES - 2:SUBLANES - 1, :], z2)
    z2 = jnp.where(row == 1, prev[SUBLANES - 1:SUBLANES, :], z2)
    carry_ref[...] = z[tm - SUBLANES:, :]
    conv = cw_ref[2:3, :] * z + cw_ref[1:2, :] * z1 + cw_ref[0:1, :] * z2
    yc = gate_b * conv
    cn_ref[...] = (_group_rms(yc, g_ref) * cg_ref[...]).astype(BF16)


def _pre_call(x, mod, n1, wm, wvt, wki, wwit, qg, kg, cw, cg, gmat):
    bsz, seq, d = x.shape
    tm = PRE_TM
    nck = tm // ATT_TK
    aw = ATTN_WIDTH
    const = lambda b, j: (0, 0)
    tok = lambda b, j: (b, j, 0)
    out_shape = (
        jax.ShapeDtypeStruct((bsz, seq, aw), BF16),
        jax.ShapeDtypeStruct((bsz, seq, aw), BF16),
        jax.ShapeDtypeStruct((bsz, seq // ATT_TK, ATTN_HEADS * V_SLAB, ATT_TK), BF16),
        jax.ShapeDtypeStruct((bsz, seq, aw), BF16),
        jax.ShapeDtypeStruct((bsz, seq, LANES), BF16),
        jax.ShapeDtypeStruct((bsz, IDX_HEADS, seq), F32),
        jax.ShapeDtypeStruct((bsz, seq, CONV_WIDTH), BF16),
    )
    out_specs = (
        pl.BlockSpec((None, tm, aw), tok),
        pl.BlockSpec((None, tm, aw), tok),
        pl.BlockSpec((None, nck, ATTN_HEADS * V_SLAB, ATT_TK), lambda b, j: (b, j, 0, 0)),
        pl.BlockSpec((None, tm, aw), tok),
        pl.BlockSpec((None, tm, LANES), tok),
        pl.BlockSpec((None, IDX_HEADS, tm), lambda b, j: (b, 0, j)),
        pl.BlockSpec((None, tm, CONV_WIDTH), tok),
    )
    in_specs = [
        pl.BlockSpec((None, tm, d), tok),
        pl.BlockSpec((None, 6, d), lambda b, j: (b, 0, 0)),
        pl.BlockSpec(n1.shape, const),
        pl.BlockSpec(wm.shape, const),
        pl.BlockSpec(wvt.shape, const),
        pl.BlockSpec(wki.shape, const),
        pl.BlockSpec(wwit.shape, const),
        pl.BlockSpec(qg.shape, const),
        pl.BlockSpec(kg.shape, const),
        pl.BlockSpec(cw.shape, const),
        pl.BlockSpec(cg.shape, const),
        pl.BlockSpec(gmat.shape, const),
    ]
    return pl.pallas_call(
        _pre_kernel,
        out_shape=out_shape,
        grid=(bsz, seq // tm),
        in_specs=in_specs,
        out_specs=out_specs,
        scratch_shapes=[pltpu.VMEM((SUBLANES, CONV_WIDTH), F32)],
        compiler_params=pltpu.CompilerParams(dimension_semantics=("arbitrary", "arbitrary"),
                                             vmem_limit_bytes=VMEM_LIMIT_BYTES),
        name="pre_proj",
    )(x, mod, n1, wm, wvt, wki, wwit, qg, kg, cw, cg, gmat)


def _attn_kernel(rb_ref, bnd_ref, q_ref, qi_ref, wit_ref, k_ref, ki_ref, vt_ref, og_ref,
                 o_ref,
                 s_ref, bias_ref, qpad_ref, qipad_ref, lg_ref, acc_ref, out_ref, *, topk):
    b = pl.program_id(0)
    i = pl.program_id(1)
    tq, tk = ATT_TQ, ATT_TK
    nh, hd = ATTN_HEADS, HEAD_DIM

    t_loc = lax.broadcasted_iota(jnp.int32, (tk, tq), 1)
    s_loc = lax.broadcasted_iota(jnp.int32, (tk, tq), 0)

    @pl.when((b == 0) & (i == 0))
    def _():
        for idx in range(2):
            dist = t_loc - s_loc + idx * tq
            for h in range(nh):
                bias_ref[idx, h] = jnp.full((tk, tq), (rb_ref[0, h] - rb_ref[N_BUCKETS - 1, h]) * LOG2E, F32)

            def fill(jb, carry):
                reached = dist >= bnd_ref[jb]
                for h in range(nh):
                    val = (rb_ref[jb, h] - rb_ref[N_BUCKETS - 1, h]) * LOG2E
                    bias_ref[idx, h] = jnp.where(reached, val, bias_ref[idx, h])
                return carry

            lax.fori_loop(1, N_BUCKETS, fill, 0)

    lane = lax.broadcasted_iota(jnp.int32, (tq, LANES), 1)
    for h in range(nh):
        pair = slice((h // 2) * LANES, (h // 2 + 1) * LANES)
        keep = (lane // hd) == (h % 2)
        qpad_ref[h] = jnp.where(keep, q_ref[:, pair], jnp.zeros((), BF16))
        qipad_ref[h] = jnp.where(keep, qi_ref[:, pair], jnp.zeros((), BF16))

    def score_chunk(c):
        kic = ki_ref[pl.ds(pl.multiple_of(c * tk, tk), tk), :]
        acc = jnp.zeros((tk, tq), F32)
        for h in range(nh):
            e = lax.dot_general(kic, qipad_ref[h], _NT_DIMS, preferred_element_type=F32)
            acc = acc + wit_ref[h:h + 1, :] * jnp.maximum(e, 0.0)
        return acc

    def a_body(c, carry):
        rmin, rmax = carry
        sc = score_chunk(c)
        s_ref[c] = sc
        return (jnp.minimum(rmin, jnp.min(sc, axis=0, keepdims=True)),
                jnp.maximum(rmax, jnp.max(sc, axis=0, keepdims=True)))

    rmin0 = jnp.full((1, tq), jnp.inf, F32)
    rmax0 = jnp.full((1, tq), -jnp.inf, F32)
    rmin, rmax = lax.fori_loop(0, i, a_body, (rmin0, rmax0))
    sc = score_chunk(i)
    causal = s_loc <= t_loc
    s_ref[i] = jnp.where(causal, sc, -jnp.inf)
    rmin = jnp.minimum(rmin, jnp.min(jnp.where(causal, sc, jnp.inf), axis=0, keepdims=True))
    rmax = jnp.maximum(rmax, jnp.max(jnp.where(causal, sc, -jnp.inf), axis=0, keepdims=True))

    def count_ge(thr):
        def body(c, accs):
            hit = s_ref[c] >= thr
            accs = list(accs)
            for r in range(tk // SUBLANES):
                a = accs[r % COUNT_ACCS]
                accs[r % COUNT_ACCS] = jnp.where(hit[r * SUBLANES:(r + 1) * SUBLANES], a + 1.0, a)
            return tuple(accs)
        accs = lax.fori_loop(0, i + 1, body,
                             tuple(jnp.zeros((SUBLANES, tq), F32) for _ in range(COUNT_ACCS)))
        return jnp.sum(_tree_sum(list(accs)), axis=0, keepdims=True)

    def order_key(v):
        bits = pltpu.bitcast(v, jnp.int32)
        return jnp.where(bits < 0, bits ^ jnp.int32(0x7FFFFFFF), bits)

    def from_order_key(key):
        return pltpu.bitcast(jnp.where(key < 0, key ^ jnp.int32(0x7FFFFFFF), key), F32)

    t_glob = (i * tq + lax.broadcasted_iota(jnp.int32, (1, tq), 1)).astype(F32)
    n_causal = t_glob + 1.0
    kf = jnp.minimum(float(topk), n_causal)
    c_max = count_ge(rmax)
    all_sel = n_causal <= kf
    max_ge = c_max >= kf
    active0 = jnp.where(all_sel | max_ge, 0.0, 1.0)
    thr0 = jnp.where(all_sel, rmin, rmax)
    tie0 = jnp.where(jnp.logical_not(all_sel) & (c_max > kf), 1.0, 0.0)
    hif0 = jnp.full((1, tq), jnp.inf, F32)
    need0 = kf

    def b_cond(st):
        return (jnp.max(st[0]) > 0.0) & (st[8] <= BISECT_MAX_STEPS)

    def b_body(st):
        active, lo, hi, fhi, thr, tie, hif, need, step = st
        lo_key = order_key(lo)
        hi_key = order_key(hi)
        mid_key = (lo_key >> 1) + (hi_key >> 1) + (lo_key & hi_key & 1)
        mid_val = lo + (hi - lo) * 0.5
        use_val = (step < BISECT_VALUE_STEPS) & (mid_val > lo) & (mid_val < hi)
        mid = jnp.where(use_val, mid_val, from_order_key(mid_key))
        collapsed = (mid_key == lo_key) | (step >= BISECT_MAX_STEPS)
        cm = count_ge(mid)
        act = active > 0.0
        live = act & jnp.logical_not(collapsed)
        found = live & (cm == kf)
        go_up = live & (cm > kf)
        go_dn = live & (cm < kf)
        ends_tie = act & collapsed
        thr = jnp.where(found, mid, jnp.where(ends_tie, lo, thr))
        tie = jnp.where(ends_tie, 1.0, tie)
        hif = jnp.where(ends_tie, hi, hif)
        need = jnp.where(ends_tie, kf - fhi, need)
        lo = jnp.where(go_up, mid, lo)
        fhi = jnp.where(go_dn, cm, fhi)
        hi = jnp.where(go_dn, mid, hi)
        active = jnp.where(found | ends_tie, 0.0, active)
        return active, lo, hi, fhi, thr, tie, hif, need, step + 1

    def b_group(st):
        for _ in range(BISECT_GROUP):
            st = b_body(st)
        return st

    _, _, _, _, thr, tie, hif, need, _ = lax.while_loop(
        b_cond, b_group, (active0, rmin, rmax, c_max, thr0, tie0, hif0, need0, jnp.int32(0)))

    @pl.when(jnp.max(tie) > 0.0)
    def _():
        tri = jnp.where(lax.broadcasted_iota(jnp.int32, (tk, tk), 1)
                        <= lax.broadcasted_iota(jnp.int32, (tk, tk), 0), 1.0, 0.0).astype(BF16)

        def body(c, seen):
            sc_c = s_ref[c]
            tied = (sc_c >= thr) & (sc_c < hif) & (tie > 0.0)
            rank = jnp.dot(tri, jnp.where(tied, 1.0, 0.0).astype(BF16), preferred_element_type=F32) + seen
            s_ref[c] = jnp.where(tied & (rank > need), -jnp.inf, sc_c)
            return rank[tk - 1:tk, :]

        lax.fori_loop(0, i + 1, body, jnp.zeros((1, tq), F32))

    acc_ref[...] = jnp.zeros(acc_ref.shape, F32)

    def store_logits(c, slot, bias_idx):
        masked = jnp.where(s_ref[c] >= thr, 0.0, NEG_BIG)
        row0 = pl.multiple_of(c * tk, tk)
        for h in range(nh):
            kc = k_ref[pl.ds(row0, tk), (h // 2) * LANES:(h // 2 + 1) * LANES]
            lt = lax.dot_general(kc, qpad_ref[h], _NT_DIMS, preferred_element_type=F32) + masked
            if bias_idx is not None:
                lt = lt + bias_ref[bias_idx, h]
            lg_ref[slot, h] = lt

    def softmax_pv(c, slot, m_all):
        m_out = []
        for h in range(nh):
            m_old = m_all[h]
            m_new = jnp.maximum(m_old, jnp.max(lg_ref[slot, h], axis=0, keepdims=True))
            p = jnp.exp2(lg_ref[slot, h] - m_new).astype(BF16)
            alpha = jnp.exp2(m_old - m_new)
            pv = jnp.dot(vt_ref[c, h * V_SLAB:(h + 1) * V_SLAB, :], p, preferred_element_type=F32)
            acc_ref[h] = alpha * acc_ref[h] + pv
            m_out.append(m_new)
        return tuple(m_out)

    def near_step(m_all):
        store_logits(i - 1, 1, 1)
        return softmax_pv(i, 0, m_all)

    def far_step(j, parity, m_all):
        c = i - 2 - j
        store_logits(c, parity, None)
        return softmax_pv(c + 1, 1 - parity, m_all)

    def far_pair(jj, m_all):
        return far_step(2 * jj + 1, 1, far_step(2 * jj, 0, m_all))

    n_far = jnp.maximum(i - 1, 0)
    m_all = tuple(jnp.full((1, tq), NEG_BIG, F32) for _ in range(nh))
    store_logits(i, 0, 0)
    m_all = lax.cond(i >= 1, near_step, lambda m: m, m_all)
    m_all = lax.fori_loop(0, n_far // 2, far_pair, m_all)
    m_all = lax.cond((n_far & 1) == 1, lambda m: far_step(n_far - 1, 0, m), lambda m: m, m_all)
    lax.cond((i & 1) == 0, lambda m: softmax_pv(0, 0, m), lambda m: softmax_pv(0, 1, m), m_all)

    for h in range(nh):
        o = acc_ref[h, :hd, :] / acc_ref[h, hd:hd + 1, :]
        ms = jnp.mean(o * o, axis=0, keepdims=True)
        out_ref[h * hd:(h + 1) * hd, :] = o * lax.rsqrt(ms + EPS)
    o_ref[...] = (out_ref[...].T * og_ref[...]).astype(BF16)


def _attn_call(rel_bias, bounds, q, qi, wit, k, ki, vt, og, topk):
    bsz, seq, aw = q.shape
    tq, tk = ATT_TQ, ATT_TK
    nck = seq // tk
    blk_q = lambda b, i: (b, i, 0)
    whole = lambda b, i: (b, 0, 0)
    smem = pl.BlockSpec(memory_space=pltpu.SMEM)
    return pl.pallas_call(
        functools.partial(_attn_kernel, topk=topk),
        out_shape=jax.ShapeDtypeStruct((bsz, seq, aw), BF16),
        grid=(bsz, seq // tq),
        in_specs=[
            smem, smem,
            pl.BlockSpec((None, tq, aw), blk_q),
            pl.BlockSpec((None, tq, aw), blk_q),
            pl.BlockSpec((None, IDX_HEADS, tq), lambda b, i: (b, 0, i)),
            pl.BlockSpec((None, seq, aw), whole),
            pl.BlockSpec((None, seq, LANES), whole),
            pl.BlockSpec((None, nck, ATTN_HEADS * V_SLAB, tk), lambda b, i: (b, 0, 0, 0)),
            pl.BlockSpec(og.shape, lambda b, i: (0, 0)),
        ],
        out_specs=pl.BlockSpec((None, tq, aw), blk_q),
        scratch_shapes=[
            pltpu.VMEM((nck, tk, tq), F32),
            pltpu.VMEM((2, ATTN_HEADS, tk, tq), F32),
            pltpu.VMEM((ATTN_HEADS, tq, LANES), BF16),
            pltpu.VMEM((IDX_HEADS, tq, LANES), BF16),
            pltpu.VMEM((2, ATTN_HEADS, tk, tq), F32),
            pltpu.VMEM((ATTN_HEADS, V_SLAB, tq), F32),
            pltpu.VMEM((aw, tq), F32),
        ],
        compiler_params=pltpu.CompilerParams(dimension_semantics=("arbitrary", "arbitrary"),
                                             vmem_limit_bytes=VMEM_LIMIT_BYTES),
        name="dsa_attention",
    )(rel_bias, bounds, q, qi, wit, k, ki, vt, og)


def _post_kernel(an_ref, cn_ref, x_ref, mod_ref, n2_ref, woa_ref, woc_ref, wr_ref, br_ref,
                 x1_ref, h2_ref, comb_ref, cnt_ref):
    mix = (jnp.dot(an_ref[...], woa_ref[...], preferred_element_type=F32)
           + jnp.dot(cn_ref[...], woc_ref[...], preferred_element_type=F32))
    x1 = x_ref[...] + mod_ref[2:3, :] * mix
    x1_ref[...] = x1
    ms = jnp.mean(x1 * x1, axis=-1, keepdims=True)
    h2 = x1 * lax.rsqrt(ms + EPS) * n2_ref[...] * (1.0 + mod_ref[4:5, :]) + mod_ref[3:4, :]
    h2b = h2.astype(BF16)
    h2_ref[...] = h2b

    logits = jnp.dot(h2b, wr_ref[...], preferred_element_type=F32) + br_ref[...]
    lane = lax.broadcasted_iota(jnp.int32, logits.shape, 1)
    lane_f = lane.astype(F32)
    far = float(LANES)
    is_g = (lane >= N_EXPERTS) & (lane < N_EXPERTS + N_GROUPS)
    gl = jnp.where(is_g, logits, -jnp.inf)
    gmax = jnp.max(gl, axis=-1, keepdims=True)
    g_sel = jnp.min(jnp.where(is_g & (gl == gmax), lane_f, far), axis=-1, keepdims=True) - float(N_EXPERTS)
    p_g = 1.0 / jnp.sum(jnp.exp(gl - gmax), axis=-1, keepdims=True)

    in_grp = (lane < N_EXPERTS) & ((lane // EXPERTS_PER_GROUP).astype(F32) == g_sel)
    e1 = jnp.where(in_grp, logits, -jnp.inf)
    l1 = jnp.max(e1, axis=-1, keepdims=True)
    i1 = jnp.min(jnp.where(in_grp & (e1 == l1), lane_f, far), axis=-1, keepdims=True)
    rest = in_grp & (lane_f != i1)
    e2 = jnp.where(rest, logits, -jnp.inf)
    l2 = jnp.max(e2, axis=-1, keepdims=True)
    i2 = jnp.min(jnp.where(rest & (e2 == l2), lane_f, far), axis=-1, keepdims=True)
    r = jnp.exp(l2 - l1)
    w1 = 1.0 / (1.0 + r)
    w2 = r / (1.0 + r)
    comb = jnp.where(lane_f == i1, p_g * w1, 0.0) + jnp.where(lane_f == i2, p_g * w2, 0.0)
    comb_ref[...] = comb
    cnt = jnp.sum(jnp.where(comb != 0.0, 1.0, 0.0), axis=0, keepdims=True)
    cnt_ref[...] = jnp.broadcast_to(cnt, cnt_ref.shape)


def _post_call(an, cn, x, mod, n2, woa, woc, wr, br):
    bsz, seq, d = x.shape
    tm = POST_TM
    tok = lambda b, j: (b, j, 0)
    const = lambda b, j: (0, 0)
    return pl.pallas_call(
        _post_kernel,
        out_shape=(jax.ShapeDtypeStruct((bsz, seq, d), F32),
                   jax.ShapeDtypeStruct((bsz, seq, d), BF16),
                   jax.ShapeDtypeStruct((bsz, seq, LANES), F32),
                   jax.ShapeDtypeStruct((bsz, seq // tm, SUBLANES, LANES), F32)),
        grid=(bsz, seq // tm),
        in_specs=[
            pl.BlockSpec((None, tm, ATTN_WIDTH), tok),
            pl.BlockSpec((None, tm, CONV_WIDTH), tok),
            pl.BlockSpec((None, tm, d), tok),
            pl.BlockSpec((None, 6, d), lambda b, j: (b, 0, 0)),
            pl.BlockSpec(n2.shape, const),
            pl.BlockSpec(woa.shape, const),
            pl.BlockSpec(woc.shape, const),
            pl.BlockSpec(wr.shape, const),
            pl.BlockSpec(br.shape, const),
        ],
        out_specs=(pl.BlockSpec((None, tm, d), tok),
                   pl.BlockSpec((None, tm, d), tok),
                   pl.BlockSpec((None, tm, LANES), tok),
                   pl.BlockSpec((None, None, SUBLANES, LANES), lambda b, j: (b, j, 0, 0))),
        compiler_params=pltpu.CompilerParams(dimension_semantics=("arbitrary", "arbitrary"),
                                             vmem_limit_bytes=VMEM_LIMIT_BYTES),
        name="post_router",
    )(an, cn, x, mod, n2, woa, woc, wr, br)


def _strict_tri(n, lower):
    r = lax.broadcasted_iota(jnp.int32, (n, n), 0)
    c = lax.broadcasted_iota(jnp.int32, (n, n), 1)
    return jnp.where((c < r) if lower else (r < c), 1.0, 0.0).astype(BF16)


def _moe_tile_copies(ntile_ref, lfirst_ref, gfirst_ref, blk, local_ref, global_ref, sem, to_global, wait):
    tile = MOE_TILE

    def per_expert(x, carry):
        lf = lfirst_ref[blk, x]
        gf = gfirst_ref[blk, x]

        def per_tile(j, c):
            loc = local_ref.at[pl.ds(pl.multiple_of((lf + j) * tile, tile), tile), :]
            glo = global_ref.at[pl.ds(pl.multiple_of((gf + j) * tile, tile), tile), :]
            cp = pltpu.make_async_copy(loc, glo, sem) if to_global else pltpu.make_async_copy(glo, loc, sem)
            if wait:
                cp.wait()
            else:
                cp.start()
            return c

        lax.fori_loop(0, ntile_ref[blk, x], per_tile, 0)
        return carry

    lax.fori_loop(0, N_EXPERTS, per_expert, 0)


def _moe_gather_kernel(ntile_ref, lfirst_ref, gfirst_ref, pad_ref,
                       h2_ref, comb_ref,
                       col_ref, xg_hbm,
                       xg_ref, row_ref, zero_ref, sem):
    blk = pl.program_id(0)
    nb = h2_ref.shape[0]
    tile, chunk = MOE_TILE, MOE_CHUNK
    lane = lax.broadcasted_iota(jnp.int32, (nb, LANES), 1)

    comb = comb_ref[...]
    assigned = comb != 0.0
    a_f = jnp.where(assigned, 1.0, 0.0)
    rank = jnp.dot(_strict_tri(nb, True), a_f.astype(BF16), preferred_element_type=F32)
    cnt = rank[nb - 1:nb, :] + a_f[nb - 1:nb, :]
    ntile = jnp.floor((cnt + float(tile - 1)) * (1.0 / tile))
    first = jnp.dot(jnp.broadcast_to(ntile, (SUBLANES, LANES)).astype(BF16), _strict_tri(LANES, False),
                    preferred_element_type=F32)[0:1, :]
    pos = first * float(tile) + rank
    pos1 = jnp.min(jnp.where(assigned, pos, 1e9), axis=1, keepdims=True)
    pos2 = jnp.max(jnp.where(assigned, pos, -1.0), axis=1, keepdims=True)
    pos2 = jnp.where(pos2 == pos1, -1.0, pos2)
    cw1 = jnp.sum(jnp.where(assigned & (pos == pos1), comb, 0.0), axis=1, keepdims=True)
    cw2 = jnp.sum(jnp.where(assigned & (pos == pos2), comb, 0.0), axis=1, keepdims=True)
    info = jnp.where(lane == 0, pos1, jnp.where(lane == 1, pos2, jnp.where(lane == 2, cw1,
                     jnp.where(lane == 3, cw2, 0.0))))
    col_ref[...] = info
    row_ref[...] = info.T

    total = lfirst_ref[blk, N_EXPERTS - 1] + ntile_ref[blk, N_EXPERTS - 1]
    n_chunks = (total * tile + (chunk - 1)) // chunk
    p1 = row_ref[0:1, :].astype(jnp.int32)
    p2 = row_ref[1:2, :].astype(jnp.int32)
    sub = lax.broadcasted_iota(jnp.int32, (chunk, nb), 0)

    def gather(c, carry):
        p = sub + c * chunk
        sel = jnp.where((p == p1) | (p == p2), 1.0, 0.0).astype(BF16)
        r0 = pl.multiple_of(c * chunk, chunk)
        xg_ref[pl.ds(r0, chunk), :] = jnp.dot(sel, h2_ref[...], preferred_element_type=F32).astype(BF16)
        return carry

    @pl.when(blk > 0)
    def _():
        _moe_tile_copies(ntile_ref, lfirst_ref, gfirst_ref, blk - 1, xg_ref, xg_hbm, sem, True, True)

    lax.fori_loop(0, n_chunks, gather, 0)

    _moe_tile_copies(ntile_ref, lfirst_ref, gfirst_ref, blk, xg_ref, xg_hbm, sem, True, False)

    is_last = blk == pl.num_programs(0) - 1

    def pad_copies(wait):
        def per_expert(x, carry):
            g0 = gfirst_ref[blk, x] + ntile_ref[blk, x]

            def per_tile(j, c):
                dst = xg_hbm.at[pl.ds(pl.multiple_of((g0 + j) * tile, tile), tile), :]
                cp = pltpu.make_async_copy(zero_ref, dst, sem)
                if wait:
                    cp.wait()
                else:
                    cp.start()
                return c

            lax.fori_loop(0, pad_ref[x], per_tile, 0)
            return carry

        lax.fori_loop(0, N_EXPERTS, per_expert, 0)

    @pl.when(is_last)
    def _():
        zero_ref[...] = jnp.zeros(zero_ref.shape, BF16)
        pad_copies(False)
        _moe_tile_copies(ntile_ref, lfirst_ref, gfirst_ref, blk, xg_ref, xg_hbm, sem, True, True)
        pad_copies(True)


def _moe_ffn_kernel(texp_ref, nt_ref, x_ref, wgu_ref, wd_ref, y_ref):
    @pl.when(pl.program_id(0) < nt_ref[0])
    def _():
        ab = jnp.dot(x_ref[...], wgu_ref[...], preferred_element_type=F32)
        a = ab[:, :EXPERT_FF]
        hid = ((a * jax.nn.sigmoid(a)) * ab[:, EXPERT_FF:]).astype(BF16)
        y_ref[...] = jnp.dot(hid, wd_ref[...], preferred_element_type=F32).astype(BF16)


def _moe_scatter_kernel(ntile_ref, lfirst_ref, gfirst_ref,
                        col_ref, x1_ref, mod_ref, y_hbm,
                        o_ref,
                        y_ref, sem):
    blk = pl.program_id(0)
    nb = x1_ref.shape[0]
    tile, chunk = MOE_TILE, MOE_CHUNK
    slot = blk & 1

    def copies(b, s, wait):
        _moe_tile_copies(ntile_ref, lfirst_ref, gfirst_ref, b, y_ref.at[s], y_hbm, sem.at[s], False, wait)

    @pl.when(blk == 0)
    def _():
        copies(0, 0, False)

    @pl.when(blk + 1 < pl.num_programs(0))
    def _():
        copies(blk + 1, 1 - slot, False)

    total = lfirst_ref[blk, N_EXPERTS - 1] + ntile_ref[blk, N_EXPERTS - 1]
    n_chunks = (total * tile + (chunk - 1)) // chunk
    max_chunks = y_ref.shape[1] // chunk
    usual = n_chunks <= MOE_USUAL_CHUNKS
    n_static = jnp.where(usual, MOE_USUAL_CHUNKS, max_chunks)

    def clear(t, carry):
        y_ref[slot, pl.ds(pl.multiple_of(t * tile, tile), tile), :] = jnp.zeros((tile, y_ref.shape[2]), BF16)
        return carry

    lax.fori_loop(total, n_static * (chunk // tile), clear, 0)

    p1 = col_ref[:, 0:1].astype(jnp.int32)
    p2 = col_ref[:, 1:2].astype(jnp.int32)
    cw1 = col_ref[:, 2:3]
    cw2 = col_ref[:, 3:4]
    gate = mod_ref[5:6, :]
    lane_c = lax.broadcasted_iota(jnp.int32, (nb, chunk), 1)
    copies(blk, slot, True)

    def scatter(n_unrolled):
        acc = None
        for c in range(n_unrolled):
            p = lane_c + c * chunk
            w = (jnp.where(p == p1, cw1, 0.0) + jnp.where(p == p2, cw2, 0.0)).astype(BF16)
            part = jnp.dot(w, y_ref[slot, c * chunk:(c + 1) * chunk, :], preferred_element_type=F32)
            acc = part if acc is None else acc + part
        o_ref[...] = x1_ref[...] + gate * acc

    lax.cond(usual, lambda: scatter(MOE_USUAL_CHUNKS), lambda: scatter(max_chunks))


def _moe_call(h2, comb, cnt_tiles, x1, mod, wgu, wd):
    bsz, seq, d = x1.shape
    nb, tile, ftm = MOE_TM, MOE_TILE, MOE_FFN_TM
    n_tok = bsz * seq
    n_blk = n_tok // nb
    region = ftm // tile
    rows_local = -(-(2 * nb + N_EXPERTS * tile) // MOE_CHUNK) * MOE_CHUNK
    tiles_global = (2 * n_tok) // tile + n_blk * N_EXPERTS + N_EXPERTS * (region - 1)
    n_ffn_max = -(-tiles_global // region)
    rows_global = n_ffn_max * ftm

    cnt = cnt_tiles[:, :, 0, :N_EXPERTS].reshape(n_blk, nb // POST_TM, N_EXPERTS).sum(axis=1).astype(jnp.int32)
    ntile = (cnt + (tile - 1)) // tile
    lfirst = jnp.cumsum(ntile, axis=1) - ntile
    tot = ntile.sum(axis=0)
    ptot = (tot + (region - 1)) // region * region
    ebase = jnp.cumsum(ptot) - ptot
    gfirst = ebase[None, :] + jnp.cumsum(ntile, axis=0) - ntile
    pad = ptot - tot
    n_ffn = (ptot.sum() // region).reshape(1)
    ends = jnp.cumsum(ptot) // region
    texp = jnp.minimum((jnp.arange(n_ffn_max, dtype=jnp.int32)[:, None] >= ends[None, :]).sum(axis=1),
                       N_EXPERTS - 1).astype(jnp.int32)

    h2f = h2.reshape(n_tok, d)
    combf = comb.reshape(n_tok, LANES)
    col, xg = pl.pallas_call(
        _moe_gather_kernel,
        out_shape=(jax.ShapeDtypeStruct((n_tok, LANES), F32),
                   jax.ShapeDtypeStruct((rows_global, d), BF16)),
        grid_spec=pltpu.PrefetchScalarGridSpec(
            num_scalar_prefetch=4,
            grid=(n_blk,),
            in_specs=[pl.BlockSpec((nb, d), lambda j, *_: (j, 0)),
                      pl.BlockSpec((nb, LANES), lambda j, *_: (j, 0))],
            out_specs=(pl.BlockSpec((nb, LANES), lambda j, *_: (j, 0)),
                       pl.BlockSpec(memory_space=pl.ANY)),
            scratch_shapes=[
                pltpu.VMEM((rows_local, d), BF16),
                pltpu.VMEM((LANES, nb), F32),
                pltpu.VMEM((tile, d), BF16),
                pltpu.SemaphoreType.DMA,
            ]),
        compiler_params=pltpu.CompilerParams(dimension_semantics=("arbitrary",),
                                             vmem_limit_bytes=VMEM_LIMIT_BYTES),
        name="moe_gather",
    )(ntile, lfirst, gfirst, pad, h2f, combf)

    last = lambda t, te, nt: jnp.minimum(t, nt[0] - 1)
    y = pl.pallas_call(
        _moe_ffn_kernel,
        out_shape=jax.ShapeDtypeStruct((rows_global, d), BF16),
        grid_spec=pltpu.PrefetchScalarGridSpec(
            num_scalar_prefetch=2,
            grid=(n_ffn_max,),
            in_specs=[pl.BlockSpec((ftm, d), lambda t, te, nt: (last(t, te, nt), 0)),
                      pl.BlockSpec((None, d, 2 * EXPERT_FF), lambda t, te, nt: (te[last(t, te, nt)], 0, 0)),
                      pl.BlockSpec((None, EXPERT_FF, d), lambda t, te, nt: (te[last(t, te, nt)], 0, 0))],
            out_specs=pl.BlockSpec((ftm, d), lambda t, te, nt: (last(t, te, nt), 0))),
        compiler_params=pltpu.CompilerParams(dimension_semantics=("arbitrary",),
                                             vmem_limit_bytes=VMEM_LIMIT_BYTES),
        name="moe_ffn",
    )(texp, n_ffn, xg, wgu, wd)

    out = pl.pallas_call(
        _moe_scatter_kernel,
        out_shape=jax.ShapeDtypeStruct((n_tok, d), F32),
        grid_spec=pltpu.PrefetchScalarGridSpec(
            num_scalar_prefetch=3,
            grid=(n_blk,),
            in_specs=[pl.BlockSpec((nb, LANES), lambda j, *_: (j, 0)),
                      pl.BlockSpec((nb, d), lambda j, *_: (j, 0)),
                      pl.BlockSpec((None, 6, d), lambda j, *_: ((j * nb) // seq, 0, 0)),
                      pl.BlockSpec(memory_space=pl.ANY)],
            out_specs=pl.BlockSpec((nb, d), lambda j, *_: (j, 0)),
            scratch_shapes=[pltpu.VMEM((2, rows_local, d), BF16),
                            pltpu.SemaphoreType.DMA((2,))]),
        compiler_params=pltpu.CompilerParams(dimension_semantics=("arbitrary",),
                                             vmem_limit_bytes=VMEM_LIMIT_BYTES),
        name="moe_scatter",
    )(ntile, lfirst, gfirst, col, x1.reshape(n_tok, d), mod, y)
    return out.reshape(bsz, seq, d)


def _layer(x, mod, rel_bias, norm1, w_in, q_norm, k_norm, conv_w, attn_out_norm, conv_out_norm, w_out,
           norm2, w_group_router, b_group_router, w_expert_router, b_expert_router, w_gate, w_up, w_down):
    bsz, seq, d = x.shape
    aw = ATTN_WIDTH
    topk = min(TOPK_MAX, seq // 4)

    offs = np.cumsum([0, aw, aw, aw, IDX_HEADS * IDX_DIM, IDX_DIM, IDX_HEADS, CONV_WIDTH, CONV_WIDTH, CONV_WIDTH])
    col = lambda n: w_in[:, int(offs[n]):int(offs[n + 1])]
    wm = jnp.concatenate([col(0), col(1), col(3), col(6), col(7), col(8)], axis=1).astype(BF16)
    wvt = col(2).T.astype(BF16)
    wki = jnp.concatenate([col(4), col(4)], axis=1).astype(BF16)
    wwit = col(5).T.astype(BF16)
    qg = (jnp.tile(q_norm, ATTN_HEADS) * ((HEAD_DIM ** -0.5) * LOG2E))[None, :]
    kg = jnp.tile(k_norm, ATTN_HEADS)[None, :]
    grp = np.arange(aw) // CONV_GROUP_DIM
    gmat = jnp.asarray((grp[:, None] == grp[None, :]).astype(np.float32) / CONV_GROUP_DIM, dtype=BF16)

    q, k, vt, qi, ki, wit, cn = _pre_call(
        x, mod, norm1[None, :], wm, wvt, wki, wwit, qg, kg, conv_w, conv_out_norm.reshape(1, -1), gmat)

    bounds = jnp.asarray(_bucket_boundaries())
    an = _attn_call(rel_bias, bounds, q, qi, wit, k, ki, vt, attn_out_norm.reshape(1, -1), topk)

    wr = jnp.concatenate([w_expert_router, w_group_router,
                          jnp.zeros((d, LANES - N_EXPERTS - N_GROUPS), F32)], axis=1).astype(BF16)
    br = jnp.concatenate([b_expert_router, b_group_router,
                          jnp.zeros((LANES - N_EXPERTS - N_GROUPS,), F32)])[None, :]
    x1, h2, comb, cnt_tiles = _post_call(an, cn, x, mod, norm2[None, :], w_out[:aw].astype(BF16),
                                         w_out[aw:].astype(BF16), wr, br)

    wgu = jnp.concatenate([w_gate, w_up], axis=-1).astype(BF16)
    return _moe_call(h2, comb, cnt_tiles, x1, mod, wgu, w_down.astype(BF16))


def kernel(x, c, rel_bias, w_ada, b_ada, norm1, w_in, q_norm, k_norm, conv_w, attn_out_norm, conv_out_norm,
           w_out, norm2, w_group_router, b_group_router, w_expert_router, b_expert_router, w_gate, w_up,
           w_down):
    bsz, seq, d = x.shape
    assert d == D_MODEL and seq % max(PRE_TM, POST_TM, MOE_TM) == 0 and ATT_TQ == ATT_TK
    depth = w_ada.shape[0]
    for l in range(depth):
        mod = _mod_call(c, w_ada[l], b_ada[l][None, :]).reshape(bsz, 6, d)
        x = _layer(x, mod, rel_bias, norm1[l], w_in[l], q_norm[l], k_norm[l], conv_w[l], attn_out_norm[l],
                   conv_out_norm[l], w_out[l], norm2[l], w_group_router[l], b_group_router[l],
                   w_expert_router[l], b_expert_router[l], w_gate[l], w_up[l], w_down[l])
    return x
```

```python
import functools
import math

import jax
import jax.numpy as jnp
import numpy as np
from jax import lax
from jax.experimental import pallas as pl
from jax.experimental.pallas import tpu as pltpu

F32 = jnp.float32
BF16 = jnp.bfloat16

D_MODEL = 1024
HEAD_DIM = 64
ATTN_HEADS = 8
ATTN_WIDTH = ATTN_HEADS * HEAD_DIM
CONV_WIDTH = D_MODEL - ATTN_WIDTH
CONV_GROUP_DIM = 64
CONV_K = 3
IDX_HEADS = 8
IDX_DIM = 64
TOPK_MAX = 256
IDX_SCALE = (IDX_DIM ** -0.5) * (IDX_HEADS ** -0.5)
N_BUCKETS = 32
MAX_DISTANCE = 128
N_GROUPS = 4
EXPERTS_PER_GROUP = 8
N_EXPERTS = N_GROUPS * EXPERTS_PER_GROUP
EXPERT_FF = 256
EPS = 1e-6
LOG2E = 1.4426950408889634
NEG_BIG = -1e30
COUNT_ACCS = 4
BISECT_GROUP = 4
BISECT_VALUE_STEPS = 8
BISECT_MAX_STEPS = 64

LANES = 128
SUBLANES = 8
BF16_SUBLANES = 16
V_SLAB = HEAD_DIM + BF16_SUBLANES
VMEM_LIMIT_BYTES = 56 * 1024 * 1024

PRE_TM = 512
ATT_TQ = 256
ATT_TK = 256
POST_TM = 512
MOE_TM = 1024
MOE_TILE = 64
MOE_CHUNK = 512
MOE_USUAL_CHUNKS = 6
MOE_FFN_TM = 1024
MOD_TN = 1536

_NT_DIMS = (((1,), (1,)), ((), ()))


def _tree_sum(parts):
    while len(parts) > 1:
        nxt = [parts[j] + parts[j + 1] for j in range(0, len(parts) - 1, 2)]
        if len(parts) % 2:
            nxt.append(parts[-1])
        parts = nxt
    return parts[0]


def _bucket_boundaries():
    max_exact = N_BUCKETS // 2
    d = np.arange(0, 4 * MAX_DISTANCE, dtype=np.int64)
    nf = np.maximum(d, 1).astype(np.float32)
    large = max_exact + (np.log(nf / np.float32(max_exact)) / np.float32(math.log(MAX_DISTANCE / max_exact))
                         * np.float32(N_BUCKETS - max_exact)).astype(np.int32)
    large = np.minimum(large, N_BUCKETS - 1)
    bucket = np.where(d < max_exact, d, large)
    assert np.all(np.diff(bucket) >= 0) and bucket[-1] == N_BUCKETS - 1
    bounds = [int(np.argmax(bucket >= j)) for j in range(1, N_BUCKETS)]
    return np.asarray([0] + bounds, dtype=np.int32)


def _mod_kernel(c_ref, w_ref, b_ref, o_ref):
    c = c_ref[...]
    act = c * jax.nn.sigmoid(c)
    o_ref[...] = jnp.dot(act, w_ref[...], preferred_element_type=F32,
                         precision=lax.Precision.HIGHEST) + b_ref[...]


def _mod_call(c, w_ada, b_ada):
    bsz, d = c.shape
    n = w_ada.shape[1]
    return pl.pallas_call(
        _mod_kernel,
        out_shape=jax.ShapeDtypeStruct((bsz, n), F32),
        grid=(n // MOD_TN,),
        in_specs=[pl.BlockSpec((bsz, d), lambda j: (0, 0)),
                  pl.BlockSpec((d, MOD_TN), lambda j: (0, j)),
                  pl.BlockSpec((1, MOD_TN), lambda j: (0, j))],
        out_specs=pl.BlockSpec((bsz, MOD_TN), lambda j: (0, j)),
        compiler_params=pltpu.CompilerParams(dimension_semantics=("arbitrary",),
                                             vmem_limit_bytes=VMEM_LIMIT_BYTES),
        name="adaln_mod",
    )(c, w_ada, b_ada)


def _group_rms(y, g_ref):
    ms = jnp.dot((y * y).astype(BF16), g_ref[...], preferred_element_type=F32)
    return y * lax.rsqrt(ms + EPS)


def _pre_kernel(x_ref, mod_ref, n1_ref, wm_ref, wvt_ref, wki_ref, wwit_ref, qg_ref, kg_ref,
                cw_ref, cg_ref, g_ref,
                q_ref, k_ref, vt_ref, qi_ref, ki_ref, wit_ref, cn_ref, carry_ref):
    j = pl.program_id(1)
    tm = x_ref.shape[0]
    aw = ATTN_WIDTH

    x = x_ref[...]
    ms = jnp.mean(x * x, axis=-1, keepdims=True)
    y = x * lax.rsqrt(ms + EPS) * n1_ref[...]
    h = y * (1.0 + mod_ref[1:2, :]) + mod_ref[0:1, :]
    hb = h.astype(BF16)

    def proj(lo):
        return jnp.dot(hb, wm_ref[:, lo:lo + aw], preferred_element_type=F32)

    q = _group_rms(proj(0), g_ref) * qg_ref[...]
    q_ref[...] = q.astype(BF16)
    k = _group_rms(proj(aw), g_ref) * kg_ref[...]
    k_ref[...] = k.astype(BF16)

    vt = lax.dot_general(wvt_ref[...], hb, _NT_DIMS, preferred_element_type=F32).astype(BF16)
    ones = jnp.ones((BF16_SUBLANES, ATT_TK), BF16)
    for cc in range(tm // ATT_TK):
        for hh in range(ATTN_HEADS):
            vt_ref[cc, hh * V_SLAB:hh * V_SLAB + HEAD_DIM, :] = (
                vt[hh * HEAD_DIM:(hh + 1) * HEAD_DIM, cc * ATT_TK:(cc + 1) * ATT_TK])
            vt_ref[cc, hh * V_SLAB + HEAD_DIM:(hh + 1) * V_SLAB, :] = ones

    qi_ref[...] = proj(2 * aw).astype(BF16)
    ki_ref[...] = jnp.dot(hb, wki_ref[...], preferred_element_type=F32).astype(BF16)
    wit_ref[...] = lax.dot_general(wwit_ref[...], hb, _NT_DIMS, preferred_element_type=F32) * IDX_SCALE

    gate_b = proj(3 * aw)
    z = proj(4 * aw) * proj(5 * aw)

    @pl.when(j == 0)
    def _():
        carry_ref[...] = jnp.zeros_like(carry_ref)

    prev = carry_ref[...]
    row = lax.broadcasted_iota(jnp.int32, z.shape, 0)
    z1 = jnp.where(row == 0, prev[SUBLANES - 1:SUBLANES, :], pltpu.roll(z, 1, 0))
    z2 = pltpu.roll(z, 2, 0)
    z2 = jnp.where(row == 0, prev[SUBLANES - 2:SUBLANES - 1, :], z2)
    z2 = jnp.where(row == 1, prev[SUBLANES - 1:SUBLANES, :], z2)
    carry_ref[...] = z[tm - SUBLANES:, :]
    conv = cw_ref[2:3, :] * z + cw_ref[1:2, :] * z1 + cw_ref[0:1, :] * z2
    yc = gate_b * conv
    cn_ref[...] = (_group_rms(yc, g_ref) * cg_ref[...]).astype(BF16)


def _pre_call(x, mod, n1, wm, wvt, wki, wwit, qg, kg, cw, cg, gmat):
    bsz, seq, d = x.shape
    tm = PRE_TM
    nck = tm // ATT_TK
    aw = ATTN_WIDTH
    const = lambda b, j: (0, 0)
    tok = lambda b, j: (b, j, 0)
    out_shape = (
        jax.ShapeDtypeStruct((bsz, seq, aw), BF16),
        jax.ShapeDtypeStruct((bsz, seq, aw), BF16),
        jax.ShapeDtypeStruct((bsz, seq // ATT_TK, ATTN_HEADS * V_SLAB, ATT_TK), BF16),
        jax.ShapeDtypeStruct((bsz, seq, aw), BF16),
        jax.ShapeDtypeStruct((bsz, seq, LANES), BF16),
        jax.ShapeDtypeStruct((bsz, IDX_HEADS, seq), F32),
        jax.ShapeDtypeStruct((bsz, seq, CONV_WIDTH), BF16),
    )
    out_specs = (
        pl.BlockSpec((None, tm, aw), tok),
        pl.BlockSpec((None, tm, aw), tok),
        pl.BlockSpec((None, nck, ATTN_HEADS * V_SLAB, ATT_TK), lambda b, j: (b, j, 0, 0)),
        pl.BlockSpec((None, tm, aw), tok),
        pl.BlockSpec((None, tm, LANES), tok),
        pl.BlockSpec((None, IDX_HEADS, tm), lambda b, j: (b, 0, j)),
        pl.BlockSpec((None, tm, CONV_WIDTH), tok),
    )
    in_specs = [
        pl.BlockSpec((None, tm, d), tok),
        pl.BlockSpec((None, 6, d), lambda b, j: (b, 0, 0)),
        pl.BlockSpec(n1.shape, const),
        pl.BlockSpec(wm.shape, const),
        pl.BlockSpec(wvt.shape, const),
        pl.BlockSpec(wki.shape, const),
        pl.BlockSpec(wwit.shape, const),
        pl.BlockSpec(qg.shape, const),
        pl.BlockSpec(kg.shape, const),
        pl.BlockSpec(cw.shape, const),
        pl.BlockSpec(cg.shape, const),
        pl.BlockSpec(gmat.shape, const),
    ]
    return pl.pallas_call(
        _pre_kernel,
        out_shape=out_shape,
        grid=(bsz, seq // tm),
        in_specs=in_specs,
        out_specs=out_specs,
        scratch_shapes=[pltpu.VMEM((SUBLANES, CONV_WIDTH), F32)],
        compiler_params=pltpu.CompilerParams(dimension_semantics=("arbitrary", "arbitrary"),
                                             vmem_limit_bytes=VMEM_LIMIT_BYTES),
        name="pre_proj",
    )(x, mod, n1, wm, wvt, wki, wwit, qg, kg, cw, cg, gmat)


def _attn_kernel(rb_ref, bnd_ref, q_ref, qi_ref, wit_ref, k_ref, ki_ref, vt_ref, og_ref,
                 o_ref,
                 s_ref, bias_ref, qpad_ref, qipad_ref, lg_ref, acc_ref, out_ref, *, topk):
    b = pl.program_id(0)
    i = pl.program_id(1)
    tq, tk = ATT_TQ, ATT_TK
    nh, hd = ATTN_HEADS, HEAD_DIM

    t_loc = lax.broadcasted_iota(jnp.int32, (tk, tq), 1)
    s_loc = lax.broadcasted_iota(jnp.int32, (tk, tq), 0)

    @pl.when((b == 0) & (i == 0))
    def _():
        for idx in range(2):
            dist = t_loc - s_loc + idx * tq
            for h in range(nh):
                bias_ref[idx, h] = jnp.full((tk, tq), (rb_ref[0, h] - rb_ref[N_BUCKETS - 1, h]) * LOG2E, F32)

            def fill(jb, carry):
                reached = dist >= bnd_ref[jb]
                for h in range(nh):
                    val = (rb_ref[jb, h] - rb_ref[N_BUCKETS - 1, h]) * LOG2E
                    bias_ref[idx, h] = jnp.where(reached, val, bias_ref[idx, h])
                return carry

            lax.fori_loop(1, N_BUCKETS, fill, 0)

    lane = lax.broadcasted_iota(jnp.int32, (tq, LANES), 1)
    for h in range(nh):
        pair = slice((h // 2) * LANES, (h // 2 + 1) * LANES)
        keep = (lane // hd) == (h % 2)
        qpad_ref[h] = jnp.where(keep, q_ref[:, pair], jnp.zeros((), BF16))
        qipad_ref[h] = jnp.where(keep, qi_ref[:, pair], jnp.zeros((), BF16))

    def idx_dots(c, slot):
        kic = ki_ref[pl.ds(pl.multiple_of(c * tk, tk), tk), :]
        for h in range(nh):
            lg_ref[slot, h] = lax.dot_general(kic, qipad_ref[h], _NT_DIMS, preferred_element_type=F32)

    def idx_reduce(c, slot, carry, diagonal):
        rmin, rmax = carry
        sc = _tree_sum([wit_ref[h:h + 1, :] * jnp.maximum(lg_ref[slot, h], 0.0) for h in range(nh)])
        if diagonal:
            causal = s_loc <= t_loc
            s_ref[c] = jnp.where(causal, sc, -jnp.inf)
            lo_c, hi_c = jnp.where(causal, sc, jnp.inf), jnp.where(causal, sc, -jnp.inf)
        else:
            s_ref[c] = sc
            lo_c, hi_c = sc, sc
        return (jnp.minimum(rmin, jnp.min(lo_c, axis=0, keepdims=True)),
                jnp.maximum(rmax, jnp.max(hi_c, axis=0, keepdims=True)))

    def idx_pair(jj, carry):
        idx_dots(2 * jj + 1, 1)
        carry = idx_reduce(2 * jj, 0, carry, False)
        idx_dots(2 * jj + 2, 0)
        return idx_reduce(2 * jj + 1, 1, carry, False)

    def idx_tail_odd(carry):
        idx_dots(i, 1)
        return idx_reduce(i, 1, idx_reduce(i - 1, 0, carry, False), True)

    idx_dots(0, 0)
    carry = (jnp.full((1, tq), jnp.inf, F32), jnp.full((1, tq), -jnp.inf, F32))
    carry = lax.fori_loop(0, i // 2, idx_pair, carry)
    rmin, rmax = lax.cond((i & 1) == 1, idx_tail_odd, lambda cr: idx_reduce(i, 0, cr, True), carry)

    def count_ge(thr):
        def body(c, accs):
            hit = s_ref[c] >= thr
            accs = list(accs)
            for r in range(tk // SUBLANES):
                a = accs[r % COUNT_ACCS]
                accs[r % COUNT_ACCS] = jnp.where(hit[r * SUBLANES:(r + 1) * SUBLANES], a + 1.0, a)
            return tuple(accs)
        accs = lax.fori_loop(0, i + 1, body,
                             tuple(jnp.zeros((SUBLANES, tq), F32) for _ in range(COUNT_ACCS)))
        return jnp.sum(_tree_sum(list(accs)), axis=0, keepdims=True)

    def order_key(v):
        bits = pltpu.bitcast(v, jnp.int32)
        return jnp.where(bits < 0, bits ^ jnp.int32(0x7FFFFFFF), bits)

    def from_order_key(key):
        return pltpu.bitcast(jnp.where(key < 0, key ^ jnp.int32(0x7FFFFFFF), key), F32)

    t_glob = (i * tq + lax.broadcasted_iota(jnp.int32, (1, tq), 1)).astype(F32)
    n_causal = t_glob + 1.0
    kf = jnp.minimum(float(topk), n_causal)
    all_sel = n_causal <= kf
    active0 = jnp.where(all_sel, 0.0, 1.0)
    thr0 = rmin
    tie0 = jnp.zeros((1, tq), F32)
    hi0 = from_order_key(order_key(rmax + 0.0) + 1)
    fhi0 = jnp.zeros((1, tq), F32)
    hif0 = jnp.full((1, tq), jnp.inf, F32)
    need0 = kf

    def b_cond(st):
        return (jnp.max(st[0]) > 0.0) & (st[8] <= BISECT_MAX_STEPS)

    def b_body(st):
        active, lo, hi, fhi, thr, tie, hif, need, step = st
        lo_key = order_key(lo)
        hi_key = order_key(hi)
        mid_key = (lo_key >> 1) + (hi_key >> 1) + (lo_key & hi_key & 1)
        mid_val = lo + (hi - lo) * 0.5
        use_val = (step < BISECT_VALUE_STEPS) & (mid_val > lo) & (mid_val < hi)
        mid = jnp.where(use_val, mid_val, from_order_key(mid_key))
        collapsed = (mid_key == lo_key) | (step >= BISECT_MAX_STEPS)
        cm = count_ge(mid)
        act = active > 0.0
        live = act & jnp.logical_not(collapsed)
        found = live & (cm == kf)
        go_up = live & (cm > kf)
        go_dn = live & (cm < kf)
        ends_tie = act & collapsed
        thr = jnp.where(found, mid, jnp.where(ends_tie, lo, thr))
        tie = jnp.where(ends_tie, 1.0, tie)
        hif = jnp.where(ends_tie, hi, hif)
        need = jnp.where(ends_tie, kf - fhi, need)
        lo = jnp.where(go_up, mid, lo)
        fhi = jnp.where(go_dn, cm, fhi)
        hi = jnp.where(go_dn, mid, hi)
        active = jnp.where(found | ends_tie, 0.0, active)
        return active, lo, hi, fhi, thr, tie, hif, need, step + 1

    def b_group(st):
        for _ in range(BISECT_GROUP):
            st = b_body(st)
        return st

    _, _, _, _, thr, tie, hif, need, _ = lax.while_loop(
        b_cond, b_group, (active0, rmin, hi0, fhi0, thr0, tie0, hif0, need0, jnp.int32(0)))

    @pl.when(jnp.max(tie) > 0.0)
    def _():
        tri = jnp.where(lax.broadcasted_iota(jnp.int32, (tk, tk), 1)
                        <= lax.broadcasted_iota(jnp.int32, (tk, tk), 0), 1.0, 0.0).astype(BF16)

        def body(c, seen):
            sc_c = s_ref[c]
            tied = (sc_c >= thr) & (sc_c < hif) & (tie > 0.0)
            rank = jnp.dot(tri, jnp.where(tied, 1.0, 0.0).astype(BF16), preferred_element_type=F32) + seen
            s_ref[c] = jnp.where(tied & (rank > need), -jnp.inf, sc_c)
            return rank[tk - 1:tk, :]

        lax.fori_loop(0, i + 1, body, jnp.zeros((1, tq), F32))

    acc_ref[...] = jnp.zeros(acc_ref.shape, F32)

    def store_logits(c, slot, bias_idx):
        masked = jnp.where(s_ref[c] >= thr, 0.0, NEG_BIG)
        row0 = pl.multiple_of(c * tk, tk)
        for h in range(nh):
            kc = k_ref[pl.ds(row0, tk), (h // 2) * LANES:(h // 2 + 1) * LANES]
            lt = lax.dot_general(kc, qpad_ref[h], _NT_DIMS, preferred_element_type=F32) + masked
            if bias_idx is not None:
                lt = lt + bias_ref[bias_idx, h]
            lg_ref[slot, h] = lt

    def softmax_pv(c, slot, m_all):
        m_out = []
        for h in range(nh):
            m_old = m_all[h]
            m_new = jnp.maximum(m_old, jnp.max(lg_ref[slot, h], axis=0, keepdims=True))
            p = jnp.exp2(lg_ref[slot, h] - m_new).astype(BF16)
            alpha = jnp.exp2(m_old - m_new)
            pv = jnp.dot(vt_ref[c, h * V_SLAB:(h + 1) * V_SLAB, :], p, preferred_element_type=F32)
            acc_ref[h] = alpha * acc_ref[h] + pv
            m_out.append(m_new)
        return tuple(m_out)

    def near_step(m_all):
        store_logits(i - 1, 1, 1)
        return softmax_pv(i, 0, m_all)

    def far_step(j, parity, m_all):
        c = i - 2 - j
        store_logits(c, parity, None)
        return softmax_pv(c + 1, 1 - parity, m_all)

    def far_pair(jj, m_all):
        return far_step(2 * jj + 1, 1, far_step(2 * jj, 0, m_all))

    n_far = jnp.maximum(i - 1, 0)
    m_all = tuple(jnp.full((1, tq), NEG_BIG, F32) for _ in range(nh))
    store_logits(i, 0, 0)
    m_all = lax.cond(i >= 1, near_step, lambda m: m, m_all)
    m_all = lax.fori_loop(0, n_far // 2, far_pair, m_all)
    m_all = lax.cond((n_far & 1) == 1, lambda m: far_step(n_far - 1, 0, m), lambda m: m, m_all)
    lax.cond((i & 1) == 0, lambda m: softmax_pv(0, 0, m), lambda m: softmax_pv(0, 1, m), m_all)

    for h in range(nh):
        o = acc_ref[h, :hd, :] / acc_ref[h, hd:hd + 1, :]
        ms = jnp.mean(o * o, axis=0, keepdims=True)
        out_ref[h * hd:(h + 1) * hd, :] = o * lax.rsqrt(ms + EPS)
    o_ref[...] = (out_ref[...].T * og_ref[...]).astype(BF16)


def _attn_call(rel_bias, bounds, q, qi, wit, k, ki, vt, og, topk):
    bsz, seq, aw = q.shape
    tq, tk = ATT_TQ, ATT_TK
    nck = seq // tk
    blk_q = lambda b, i: (b, i, 0)
    whole = lambda b, i: (b, 0, 0)
    smem = pl.BlockSpec(memory_space=pltpu.SMEM)
    return pl.pallas_call(
        functools.partial(_attn_kernel, topk=topk),
        out_shape=jax.ShapeDtypeStruct((bsz, seq, aw), BF16),
        grid=(bsz, seq // tq),
        in_specs=[
            smem, smem,
            pl.BlockSpec((None, tq, aw), blk_q),
            pl.BlockSpec((None, tq, aw), blk_q),
            pl.BlockSpec((None, IDX_HEADS, tq), lambda b, i: (b, 0, i)),
            pl.BlockSpec((None, seq, aw), whole),
            pl.BlockSpec((None, seq, LANES), whole),
            pl.BlockSpec((None, nck, ATTN_HEADS * V_SLAB, tk), lambda b, i: (b, 0, 0, 0)),
            pl.BlockSpec(og.shape, lambda b, i: (0, 0)),
        ],
        out_specs=pl.BlockSpec((None, tq, aw), blk_q),
        scratch_shapes=[
            pltpu.VMEM((nck, tk, tq), F32),
            pltpu.VMEM((2, ATTN_HEADS, tk, tq), F32),
            pltpu.VMEM((ATTN_HEADS, tq, LANES), BF16),
            pltpu.VMEM((IDX_HEADS, tq, LANES), BF16),
            pltpu.VMEM((2, ATTN_HEADS, tk, tq), F32),
            pltpu.VMEM((ATTN_HEADS, V_SLAB, tq), F32),
            pltpu.VMEM((aw, tq), F32),
        ],
        compiler_params=pltpu.CompilerParams(dimension_semantics=("arbitrary", "arbitrary"),
                                             vmem_limit_bytes=VMEM_LIMIT_BYTES),
        name="dsa_attention",
    )(rel_bias, bounds, q, qi, wit, k, ki, vt, og)


def _post_kernel(an_ref, cn_ref, x_ref, mod_ref, n2_ref, woa_ref, woc_ref, wr_ref, br_ref,
                 x1_ref, h2_ref, comb_ref, cnt_ref):
    mix = (jnp.dot(an_ref[...], woa_ref[...], preferred_element_type=F32)
           + jnp.dot(cn_ref[...], woc_ref[...], preferred_element_type=F32))
    x1 = x_ref[...] + mod_ref[2:3, :] * mix
    x1_ref[...] = x1
    ms = jnp.mean(x1 * x1, axis=-1, keepdims=True)
    h2 = x1 * lax.rsqrt(ms + EPS) * n2_ref[...] * (1.0 + mod_ref[4:5, :]) + mod_ref[3:4, :]
    h2b = h2.astype(BF16)
    h2_ref[...] = h2b

    logits = jnp.dot(h2b, wr_ref[...], preferred_element_type=F32) + br_ref[...]
    lane = lax.broadcasted_iota(jnp.int32, logits.shape, 1)
    lane_f = lane.astype(F32)
    far = float(LANES)
    is_g = (lane >= N_EXPERTS) & (lane < N_EXPERTS + N_GROUPS)
    gl = jnp.where(is_g, logits, -jnp.inf)
    gmax = jnp.max(gl, axis=-1, keepdims=True)
    g_sel = jnp.min(jnp.where(is_g & (gl == gmax), lane_f, far), axis=-1, keepdims=True) - float(N_EXPERTS)
    p_g = 1.0 / jnp.sum(jnp.exp(gl - gmax), axis=-1, keepdims=True)

    in_grp = (lane < N_EXPERTS) & ((lane // EXPERTS_PER_GROUP).astype(F32) == g_sel)
    e1 = jnp.where(in_grp, logits, -jnp.inf)
    l1 = jnp.max(e1, axis=-1, keepdims=True)
    i1 = jnp.min(jnp.where(in_grp & (e1 == l1), lane_f, far), axis=-1, keepdims=True)
    rest = in_grp & (lane_f != i1)
    e2 = jnp.where(rest, logits, -jnp.inf)
    l2 = jnp.max(e2, axis=-1, keepdims=True)
    i2 = jnp.min(jnp.where(rest & (e2 == l2), lane_f, far), axis=-1, keepdims=True)
    r = jnp.exp(l2 - l1)
    w1 = 1.0 / (1.0 + r)
    w2 = r / (1.0 + r)
    comb = jnp.where(lane_f == i1, p_g * w1, 0.0) + jnp.where(lane_f == i2, p_g * w2, 0.0)
    comb_ref[...] = comb
    cnt = jnp.sum(jnp.where(comb != 0.0, 1.0, 0.0), axis=0, keepdims=True)
    cnt_ref[...] = jnp.broadcast_to(cnt, cnt_ref.shape)


def _post_call(an, cn, x, mod, n2, woa, woc, wr, br):
    bsz, seq, d = x.shape
    tm = POST_TM
    tok = lambda b, j: (b, j, 0)
    const = lambda b, j: (0, 0)
    return pl.pallas_call(
        _post_kernel,
        out_shape=(jax.ShapeDtypeStruct((bsz, seq, d), F32),
                   jax.ShapeDtypeStruct((bsz, seq, d), BF16),
                   jax.ShapeDtypeStruct((bsz, seq, LANES), F32),
                   jax.ShapeDtypeStruct((bsz, seq // tm, SUBLANES, LANES), F32)),
        grid=(bsz, seq // tm),
        in_specs=[
            pl.BlockSpec((None, tm, ATTN_WIDTH), tok),
            pl.BlockSpec((None, tm, CONV_WIDTH), tok),
            pl.BlockSpec((None, tm, d), tok),
            pl.BlockSpec((None, 6, d), lambda b, j: (b, 0, 0)),
            pl.BlockSpec(n2.shape, const),
            pl.BlockSpec(woa.shape, const),
            pl.BlockSpec(woc.shape, const),
            pl.BlockSpec(wr.shape, const),
            pl.BlockSpec(br.shape, const),
        ],
        out_specs=(pl.BlockSpec((None, tm, d), tok),
                   pl.BlockSpec((None, tm, d), tok),
                   pl.BlockSpec((None, tm, LANES), tok),
                   pl.BlockSpec((None, None, SUBLANES, LANES), lambda b, j: (b, j, 0, 0))),
        compiler_params=pltpu.CompilerParams(dimension_semantics=("arbitrary", "arbitrary"),
                                             vmem_limit_bytes=VMEM_LIMIT_BYTES),
        name="post_router",
    )(an, cn, x, mod, n2, woa, woc, wr, br)


def _strict_tri(n, lower):
    r = lax.broadcasted_iota(jnp.int32, (n, n), 0)
    c = lax.broadcasted_iota(jnp.int32, (n, n), 1)
    return jnp.where((c < r) if lower else (r < c), 1.0, 0.0).astype(BF16)


def _moe_tile_copies(ntile_ref, lfirst_ref, gfirst_ref, blk, local_ref, global_ref, sem, to_global, wait):
    tile = MOE_TILE

    def per_expert(x, carry):
        lf = lfirst_ref[blk, x]
        gf = gfirst_ref[blk, x]

        def per_tile(j, c):
            loc = local_ref.at[pl.ds(pl.multiple_of((lf + j) * tile, tile), tile), :]
            glo = global_ref.at[pl.ds(pl.multiple_of((gf + j) * tile, tile), tile), :]
            cp = pltpu.make_async_copy(loc, glo, sem) if to_global else pltpu.make_async_copy(glo, loc, sem)
            if wait:
                cp.wait()
            else:
                cp.start()
            return c

        lax.fori_loop(0, ntile_ref[blk, x], per_tile, 0)
        return carry

    lax.fori_loop(0, N_EXPERTS, per_expert, 0)


def _moe_gather_kernel(ntile_ref, lfirst_ref, gfirst_ref, pad_ref,
                       h2_ref, comb_ref,
                       col_ref, xg_hbm,
                       xg_ref, row_ref, zero_ref, sem):
    blk = pl.program_id(0)
    nb = h2_ref.shape[0]
    tile, chunk = MOE_TILE, MOE_CHUNK
    lane = lax.broadcasted_iota(jnp.int32, (nb, LANES), 1)

    comb = comb_ref[...]
    assigned = comb != 0.0
    a_f = jnp.where(assigned, 1.0, 0.0)
    rank = jnp.dot(_strict_tri(nb, True), a_f.astype(BF16), preferred_element_type=F32)
    cnt = rank[nb - 1:nb, :] + a_f[nb - 1:nb, :]
    ntile = jnp.floor((cnt + float(tile - 1)) * (1.0 / tile))
    first = jnp.dot(jnp.broadcast_to(ntile, (SUBLANES, LANES)).astype(BF16), _strict_tri(LANES, False),
                    preferred_element_type=F32)[0:1, :]
    pos = first * float(tile) + rank
    pos1 = jnp.min(jnp.where(assigned, pos, 1e9), axis=1, keepdims=True)
    pos2 = jnp.max(jnp.where(assigned, pos, -1.0), axis=1, keepdims=True)
    pos2 = jnp.where(pos2 == pos1, -1.0, pos2)
    cw1 = jnp.sum(jnp.where(assigned & (pos == pos1), comb, 0.0), axis=1, keepdims=True)
    cw2 = jnp.sum(jnp.where(assigned & (pos == pos2), comb, 0.0), axis=1, keepdims=True)
    info = jnp.where(lane == 0, pos1, jnp.where(lane == 1, pos2, jnp.where(lane == 2, cw1,
                     jnp.where(lane == 3, cw2, 0.0))))
    col_ref[...] = info
    row_ref[...] = info.T

    total = lfirst_ref[blk, N_EXPERTS - 1] + ntile_ref[blk, N_EXPERTS - 1]
    n_chunks = (total * tile + (chunk - 1)) // chunk
    p1 = row_ref[0:1, :].astype(jnp.int32)
    p2 = row_ref[1:2, :].astype(jnp.int32)
    sub = lax.broadcasted_iota(jnp.int32, (chunk, nb), 0)

    def gather(c, carry):
        p = sub + c * chunk
        sel = jnp.where((p == p1) | (p == p2), 1.0, 0.0).astype(BF16)
        r0 = pl.multiple_of(c * chunk, chunk)
        xg_ref[pl.ds(r0, chunk), :] = jnp.dot(sel, h2_ref[...], preferred_element_type=F32).astype(BF16)
        return carry

    @pl.when(blk > 0)
    def _():
        _moe_tile_copies(ntile_ref, lfirst_ref, gfirst_ref, blk - 1, xg_ref, xg_hbm, sem, True, True)

    lax.fori_loop(0, n_chunks, gather, 0)

    _moe_tile_copies(ntile_ref, lfirst_ref, gfirst_ref, blk, xg_ref, xg_hbm, sem, True, False)

    is_last = blk == pl.num_programs(0) - 1

    def pad_copies(wait):
        def per_expert(x, carry):
            g0 = gfirst_ref[blk, x] + ntile_ref[blk, x]

            def per_tile(j, c):
                dst = xg_hbm.at[pl.ds(pl.multiple_of((g0 + j) * tile, tile), tile), :]
                cp = pltpu.make_async_copy(zero_ref, dst, sem)
                if wait:
                    cp.wait()
                else:
                    cp.start()
                return c

            lax.fori_loop(0, pad_ref[x], per_tile, 0)
            return carry

        lax.fori_loop(0, N_EXPERTS, per_expert, 0)

    @pl.when(is_last)
    def _():
        zero_ref[...] = jnp.zeros(zero_ref.shape, BF16)
        pad_copies(False)
        _moe_tile_copies(ntile_ref, lfirst_ref, gfirst_ref, blk, xg_ref, xg_hbm, sem, True, True)
        pad_copies(True)


def _moe_ffn_kernel(texp_ref, nt_ref, x_ref, wgu_ref, wd_ref, y_ref):
    @pl.when(pl.program_id(0) < nt_ref[0])
    def _():
        ab = jnp.dot(x_ref[...], wgu_ref[...], preferred_element_type=F32)
        a = ab[:, :EXPERT_FF]
        hid = ((a * jax.nn.sigmoid(a)) * ab[:, EXPERT_FF:]).astype(BF16)
        y_ref[...] = jnp.dot(hid, wd_ref[...], preferred_element_type=F32).astype(BF16)


def _moe_scatter_kernel(ntile_ref, lfirst_ref, gfirst_ref,
                        col_ref, x1_ref, mod_ref, y_hbm,
                        o_ref,
                        y_ref, sem):
    blk = pl.program_id(0)
    nb = x1_ref.shape[0]
    tile, chunk = MOE_TILE, MOE_CHUNK
    slot = blk & 1

    def copies(b, s, wait):
        _moe_tile_copies(ntile_ref, lfirst_ref, gfirst_ref, b, y_ref.at[s], y_hbm, sem.at[s], False, wait)

    @pl.when(blk == 0)
    def _():
        copies(0, 0, False)

    @pl.when(blk + 1 < pl.num_programs(0))
    def _():
        copies(blk + 1, 1 - slot, False)

    total = lfirst_ref[blk, N_EXPERTS - 1] + ntile_ref[blk, N_EXPERTS - 1]
    n_chunks = (total * tile + (chunk - 1)) // chunk
    max_chunks = y_ref.shape[1] // chunk
    usual = n_chunks <= MOE_USUAL_CHUNKS
    n_static = jnp.where(usual, MOE_USUAL_CHUNKS, max_chunks)

    def clear(t, carry):
        y_ref[slot, pl.ds(pl.multiple_of(t * tile, tile), tile), :] = jnp.zeros((tile, y_ref.shape[2]), BF16)
        return carry

    lax.fori_loop(total, n_static * (chunk // tile), clear, 0)

    p1 = col_ref[:, 0:1].astype(jnp.int32)
    p2 = col_ref[:, 1:2].astype(jnp.int32)
    cw1 = col_ref[:, 2:3]
    cw2 = col_ref[:, 3:4]
    gate = mod_ref[5:6, :]
    lane_c = lax.broadcasted_iota(jnp.int32, (nb, chunk), 1)
    copies(blk, slot, True)

    def scatter(n_unrolled):
        acc = None
        for c in range(n_unrolled):
            p = lane_c + c * chunk
            w = (jnp.where(p == p1, cw1, 0.0) + jnp.where(p == p2, cw2, 0.0)).astype(BF16)
            part = jnp.dot(w, y_ref[slot, c * chunk:(c + 1) * chunk, :], preferred_element_type=F32)
            acc = part if acc is None else acc + part
        o_ref[...] = x1_ref[...] + gate * acc

    lax.cond(usual, lambda: scatter(MOE_USUAL_CHUNKS), lambda: scatter(max_chunks))


def _moe_call(h2, comb, cnt_tiles, x1, mod, wgu, wd):
    bsz, seq, d = x1.shape
    nb, tile, ftm = MOE_TM, MOE_TILE, MOE_FFN_TM
    n_tok = bsz * seq
    n_blk = n_tok // nb
    region = ftm // tile
    rows_local = -(-(2 * nb + N_EXPERTS * tile) // MOE_CHUNK) * MOE_CHUNK
    tiles_global = (2 * n_tok) // tile + n_blk * N_EXPERTS + N_EXPERTS * (region - 1)
    n_ffn_max = -(-tiles_global // region)
    rows_global = n_ffn_max * ftm

    cnt = cnt_tiles[:, :, 0, :N_EXPERTS].reshape(n_blk, nb // POST_TM, N_EXPERTS).sum(axis=1).astype(jnp.int32)
    ntile = (cnt + (tile - 1)) // tile
    lfirst = jnp.cumsum(ntile, axis=1) - ntile
    tot = ntile.sum(axis=0)
    ptot = (tot + (region - 1)) // region * region
    ebase = jnp.cumsum(ptot) - ptot
    gfirst = ebase[None, :] + jnp.cumsum(ntile, axis=0) - ntile
    pad = ptot - tot
    n_ffn = (ptot.sum() // region).reshape(1)
    ends = jnp.cumsum(ptot) // region
    texp = jnp.minimum((jnp.arange(n_ffn_max, dtype=jnp.int32)[:, None] >= ends[None, :]).sum(axis=1),
                       N_EXPERTS - 1).astype(jnp.int32)

    h2f = h2.reshape(n_tok, d)
    combf = comb.reshape(n_tok, LANES)
    col, xg = pl.pallas_call(
        _moe_gather_kernel,
        out_shape=(jax.ShapeDtypeStruct((n_tok, LANES), F32),
                   jax.ShapeDtypeStruct((rows_global, d), BF16)),
        grid_spec=pltpu.PrefetchScalarGridSpec(
            num_scalar_prefetch=4,
            grid=(n_blk,),
            in_specs=[pl.BlockSpec((nb, d), lambda j, *_: (j, 0)),
                      pl.BlockSpec((nb, LANES), lambda j, *_: (j, 0))],
            out_specs=(pl.BlockSpec((nb, LANES), lambda j, *_: (j, 0)),
                       pl.BlockSpec(memory_space=pl.ANY)),
            scratch_shapes=[
                pltpu.VMEM((rows_local, d), BF16),
                pltpu.VMEM((LANES, nb), F32),
                pltpu.VMEM((tile, d), BF16),
                pltpu.SemaphoreType.DMA,
            ]),
        compiler_params=pltpu.CompilerParams(dimension_semantics=("arbitrary",),
                                             vmem_limit_bytes=VMEM_LIMIT_BYTES),
        name="moe_gather",
    )(ntile, lfirst, gfirst, pad, h2f, combf)

    last = lambda t, te, nt: jnp.minimum(t, nt[0] - 1)
    y = pl.pallas_call(
        _moe_ffn_kernel,
        out_shape=jax.ShapeDtypeStruct((rows_global, d), BF16),
        grid_spec=pltpu.PrefetchScalarGridSpec(
            num_scalar_prefetch=2,
            grid=(n_ffn_max,),
            in_specs=[pl.BlockSpec((ftm, d), lambda t, te, nt: (last(t, te, nt), 0)),
                      pl.BlockSpec((None, d, 2 * EXPERT_FF), lambda t, te, nt: (te[last(t, te, nt)], 0, 0)),
                      pl.BlockSpec((None, EXPERT_FF, d), lambda t, te, nt: (te[last(t, te, nt)], 0, 0))],
            out_specs=pl.BlockSpec((ftm, d), lambda t, te, nt: (last(t, te, nt), 0))),
        compiler_params=pltpu.CompilerParams(dimension_semantics=("arbitrary",),
                                             vmem_limit_bytes=VMEM_LIMIT_BYTES),
        name="moe_ffn",
    )(texp, n_ffn, xg, wgu, wd)

    out = pl.pallas_call(
        _moe_scatter_kernel,
        out_shape=jax.ShapeDtypeStruct((n_tok, d), F32),
        grid_spec=pltpu.PrefetchScalarGridSpec(
            num_scalar_prefetch=3,
            grid=(n_blk,),
            in_specs=[pl.BlockSpec((nb, LANES), lambda j, *_: (j, 0)),
                      pl.BlockSpec((nb, d), lambda j, *_: (j, 0)),
                      pl.BlockSpec((None, 6, d), lambda j, *_: ((j * nb) // seq, 0, 0)),
                      pl.BlockSpec(memory_space=pl.ANY)],
            out_specs=pl.BlockSpec((nb, d), lambda j, *_: (j, 0)),
            scratch_shapes=[pltpu.VMEM((2, rows_local, d), BF16),
                            pltpu.SemaphoreType.DMA((2,))]),
        compiler_params=pltpu.CompilerParams(dimension_semantics=("arbitrary",),
                                             vmem_limit_bytes=VMEM_LIMIT_BYTES),
        name="moe_scatter",
    )(ntile, lfirst, gfirst, col, x1.reshape(n_tok, d), mod, y)
    return out.reshape(bsz, seq, d)


def _layer(x, mod, rel_bias, norm1, w_in, q_norm, k_norm, conv_w, attn_out_norm, conv_out_norm, w_out,
           norm2, w_group_router, b_group_router, w_expert_router, b_expert_router, w_gate, w_up, w_down):
    bsz, seq, d = x.shape
    aw = ATTN_WIDTH
    topk = min(TOPK_MAX, seq // 4)

    offs = np.cumsum([0, aw, aw, aw, IDX_HEADS * IDX_DIM, IDX_DIM, IDX_HEADS, CONV_WIDTH, CONV_WIDTH, CONV_WIDTH])
    col = lambda n: w_in[:, int(offs[n]):int(offs[n + 1])]
    wm = jnp.concatenate([col(0), col(1), col(3), col(6), col(7), col(8)], axis=1).astype(BF16)
    wvt = col(2).T.astype(BF16)
    wki = jnp.concatenate([col(4), col(4)], axis=1).astype(BF16)
    wwit = col(5).T.astype(BF16)
    qg = (jnp.tile(q_norm, ATTN_HEADS) * ((HEAD_DIM ** -0.5) * LOG2E))[None, :]
    kg = jnp.tile(k_norm, ATTN_HEADS)[None, :]
    grp = np.arange(aw) // CONV_GROUP_DIM
    gmat = jnp.asarray((grp[:, None] == grp[None, :]).astype(np.float32) / CONV_GROUP_DIM, dtype=BF16)

    q, k, vt, qi, ki, wit, cn = _pre_call(
        x, mod, norm1[None, :], wm, wvt, wki, wwit, qg, kg, conv_w, conv_out_norm.reshape(1, -1), gmat)

    bounds = jnp.asarray(_bucket_boundaries())
    an = _attn_call(rel_bias, bounds, q, qi, wit, k, ki, vt, attn_out_norm.reshape(1, -1), topk)

    wr = jnp.concatenate([w_expert_router, w_group_router,
                          jnp.zeros((d, LANES - N_EXPERTS - N_GROUPS), F32)], axis=1).astype(BF16)
    br = jnp.concatenate([b_expert_router, b_group_router,
                          jnp.zeros((LANES - N_EXPERTS - N_GROUPS,), F32)])[None, :]
    x1, h2, comb, cnt_tiles = _post_call(an, cn, x, mod, norm2[None, :], w_out[:aw].astype(BF16),
                                         w_out[aw:].astype(BF16), wr, br)

    wgu = jnp.concatenate([w_gate, w_up], axis=-1).astype(BF16)
    return _moe_call(h2, comb, cnt_tiles, x1, mod, wgu, w_down.astype(BF16))


def kernel(x, c, rel_bias, w_ada, b_ada, norm1, w_in, q_norm, k_norm, conv_w, attn_out_norm, conv_out_norm,
           w_out, norm2, w_group_router, b_group_router, w_expert_router, b_expert_router, w_gate, w_up,
           w_down):
    bsz, seq, d = x.shape
    assert d == D_MODEL and seq % max(PRE_TM, POST_TM, MOE_TM) == 0 and ATT_TQ == ATT_TK
    depth = w_ada.shape[0]
    for l in range(depth):
        mod = _mod_call(c, w_ada[l], b_ada[l][None, :]).reshape(bsz, 6, d)
        x = _layer(x, mod, rel_bias, norm1[l], w_in[l], q_norm[l], k_norm[l], conv_w[l], attn_out_norm[l],
                   conv_out_norm[l], w_out[l], norm2[l], w_group_router[l], b_group_router[l],
                   w_expert_router[l], b_expert_router[l], w_gate[l], w_up[l], w_down[l])
    return x
```

```python
import functools
import math

import jax
import jax.numpy as jnp
import numpy as np
from jax import lax
from jax.experimental import pallas as pl
from jax.experimental.pallas import tpu as pltpu

F32 = jnp.float32
BF16 = jnp.bfloat16

D_MODEL = 1024
HEAD_DIM = 64
ATTN_HEADS = 8
ATTN_WIDTH = ATTN_HEADS * HEAD_DIM
CONV_WIDTH = D_MODEL - ATTN_WIDTH
CONV_GROUP_DIM = 64
CONV_K = 3
IDX_HEADS = 8
IDX_DIM = 64
TOPK_MAX = 256
IDX_SCALE = (IDX_DIM ** -0.5) * (IDX_HEADS ** -0.5)
N_BUCKETS = 32
MAX_DISTANCE = 128
N_GROUPS = 4
EXPERTS_PER_GROUP = 8
N_EXPERTS = N_GROUPS * EXPERTS_PER_GROUP
EXPERT_FF = 256
EPS = 1e-6
LOG2E = 1.4426950408889634
NEG_BIG = -1e30
COUNT_ACCS = 4
BISECT_GROUP = 4
BISECT_VALUE_STEPS = 8
BISECT_MAX_STEPS = 64

LANES = 128
SUBLANES = 8
BF16_SUBLANES = 16
V_SLAB = HEAD_DIM + BF16_SUBLANES
VMEM_LIMIT_BYTES = 56 * 1024 * 1024

PRE_TM = 512
ATT_TQ = 256
ATT_TK = 256
POST_TM = 512
MOE_TM = 1024
MOE_TILE = 64
MOE_CHUNK = 512
MOE_USUAL_CHUNKS = 6
MOE_FFN_TM = 1024
MOD_TN = 1536

_NT_DIMS = (((1,), (1,)), ((), ()))


def _tree_sum(parts):
    while len(parts) > 1:
        nxt = [parts[j] + parts[j + 1] for j in range(0, len(parts) - 1, 2)]
        if len(parts) % 2:
            nxt.append(parts[-1])
        parts = nxt
    return parts[0]


def _bucket_boundaries():
    max_exact = N_BUCKETS // 2
    d = np.arange(0, 4 * MAX_DISTANCE, dtype=np.int64)
    nf = np.maximum(d, 1).astype(np.float32)
    large = max_exact + (np.log(nf / np.float32(max_exact)) / np.float32(math.log(MAX_DISTANCE / max_exact))
                         * np.float32(N_BUCKETS - max_exact)).astype(np.int32)
    large = np.minimum(large, N_BUCKETS - 1)
    bucket = np.where(d < max_exact, d, large)
    assert np.all(np.diff(bucket) >= 0) and bucket[-1] == N_BUCKETS - 1
    bounds = [int(np.argmax(bucket >= j)) for j in range(1, N_BUCKETS)]
    return np.asarray([0] + bounds, dtype=np.int32)


def _mod_kernel(c_ref, w_ref, b_ref, o_ref):
    c = c_ref[...]
    act = c * jax.nn.sigmoid(c)
    o_ref[...] = jnp.dot(act, w_ref[...], preferred_element_type=F32,
                         precision=lax.Precision.HIGHEST) + b_ref[...]


def _mod_call(c, w_ada, b_ada):
    bsz, d = c.shape
    n = w_ada.shape[1]
    return pl.pallas_call(
        _mod_kernel,
        out_shape=jax.ShapeDtypeStruct((bsz, n), F32),
        grid=(n // MOD_TN,),
        in_specs=[pl.BlockSpec((bsz, d), lambda j: (0, 0)),
                  pl.BlockSpec((d, MOD_TN), lambda j: (0, j)),
                  pl.BlockSpec((1, MOD_TN), lambda j: (0, j))],
        out_specs=pl.BlockSpec((bsz, MOD_TN), lambda j: (0, j)),
        compiler_params=pltpu.CompilerParams(dimension_semantics=("arbitrary",),
                                             vmem_limit_bytes=VMEM_LIMIT_BYTES),
        name="adaln_mod",
    )(c, w_ada, b_ada)


def _group_rms(y, g_ref):
    ms = jnp.dot((y * y).astype(BF16), g_ref[...], preferred_element_type=F32)
    return y * lax.rsqrt(ms + EPS)


def _pre_kernel(x_ref, mod_ref, n1_ref, wm_ref, wvt_ref, wki_ref, wwit_ref, qg_ref, kg_ref,
                cw_ref, cg_ref, g_ref,
                q_ref, k_ref, vt_ref, qi_ref, ki_ref, wit_ref, cn_ref, carry_ref):
    j = pl.program_id(1)
    tm = x_ref.shape[0]
    aw = ATTN_WIDTH

    x = x_ref[...]
    ms = jnp.mean(x * x, axis=-1, keepdims=True)
    y = x * lax.rsqrt(ms + EPS) * n1_ref[...]
    h = y * (1.0 + mod_ref[1:2, :]) + mod_ref[0:1, :]
    hb = h.astype(BF16)

    def proj(lo):
        return jnp.dot(hb, wm_ref[:, lo:lo + aw], preferred_element_type=F32)

    q = _group_rms(proj(0), g_ref) * qg_ref[...]
    q_ref[...] = q.astype(BF16)
    k = _group_rms(proj(aw), g_ref) * kg_ref[...]
    k_ref[...] = k.astype(BF16)

    vt = lax.dot_general(wvt_ref[...], hb, _NT_DIMS, preferred_element_type=F32).astype(BF16)
    ones = jnp.ones((BF16_SUBLANES, ATT_TK), BF16)
    for cc in range(tm // ATT_TK):
        for hh in range(ATTN_HEADS):
            vt_ref[cc, hh * V_SLAB:hh * V_SLAB + HEAD_DIM, :] = (
                vt[hh * HEAD_DIM:(hh + 1) * HEAD_DIM, cc * ATT_TK:(cc + 1) * ATT_TK])
            vt_ref[cc, hh * V_SLAB + HEAD_DIM:(hh + 1) * V_SLAB, :] = ones

    qi_ref[...] = proj(2 * aw).astype(BF16)
    ki_ref[...] = jnp.dot(hb, wki_ref[...], preferred_element_type=F32).astype(BF16)
    wit_ref[...] = lax.dot_general(wwit_ref[...], hb, _NT_DIMS, preferred_element_type=F32) * IDX_SCALE

    gate_b = proj(3 * aw)
    z = proj(4 * aw) * proj(5 * aw)

    @pl.when(j == 0)
    def _():
        carry_ref[...] = jnp.zeros_like(carry_ref)

    prev = carry_ref[...]
    row = lax.broadcasted_iota(jnp.int32, z.shape, 0)
    z1 = jnp.where(row == 0, prev[SUBLANES - 1:SUBLANES, :], pltpu.roll(z, 1, 0))
    z2 = pltpu.roll(z, 2, 0)
    z2 = jnp.where(row == 0, prev[SUBLANES - 2:SUBLANES - 1, :], z2)
    z2 = jnp.where(row == 1, prev[SUBLANES - 1:SUBLANES, :], z2)
    carry_ref[...] = z[tm - SUBLANES:, :]
    conv = cw_ref[2:3, :] * z + cw_ref[1:2, :] * z1 + cw_ref[0:1, :] * z2
    yc = gate_b * conv
    cn_ref[...] = (_group_rms(yc, g_ref) * cg_ref[...]).astype(BF16)


def _pre_call(x, mod, n1, wm, wvt, wki, wwit, qg, kg, cw, cg, gmat):
    bsz, seq, d = x.shape
    tm = PRE_TM
    nck = tm // ATT_TK
    aw = ATTN_WIDTH
    const = lambda b, j: (0, 0)
    tok = lambda b, j: (b, j, 0)
    out_shape = (
        jax.ShapeDtypeStruct((bsz, seq, aw), BF16),
        jax.ShapeDtypeStruct((bsz, seq, aw), BF16),
        jax.ShapeDtypeStruct((bsz, seq // ATT_TK, ATTN_HEADS * V_SLAB, ATT_TK), BF16),
        jax.ShapeDtypeStruct((bsz, seq, aw), BF16),
        jax.ShapeDtypeStruct((bsz, seq, LANES), BF16),
        jax.ShapeDtypeStruct((bsz, IDX_HEADS, seq), F32),
        jax.ShapeDtypeStruct((bsz, seq, CONV_WIDTH), BF16),
    )
    out_specs = (
        pl.BlockSpec((None, tm, aw), tok),
        pl.BlockSpec((None, tm, aw), tok),
        pl.BlockSpec((None, nck, ATTN_HEADS * V_SLAB, ATT_TK), lambda b, j: (b, j, 0, 0)),
        pl.BlockSpec((None, tm, aw), tok),
        pl.BlockSpec((None, tm, LANES), tok),
        pl.BlockSpec((None, IDX_HEADS, tm), lambda b, j: (b, 0, j)),
        pl.BlockSpec((None, tm, CONV_WIDTH), tok),
    )
    in_specs = [
        pl.BlockSpec((None, tm, d), tok),
        pl.BlockSpec((None, 6, d), lambda b, j: (b, 0, 0)),
        pl.BlockSpec(n1.shape, const),
        pl.BlockSpec(wm.shape, const),
        pl.BlockSpec(wvt.shape, const),
        pl.BlockSpec(wki.shape, const),
        pl.BlockSpec(wwit.shape, const),
        pl.BlockSpec(qg.shape, const),
        pl.BlockSpec(kg.shape, const),
        pl.BlockSpec(cw.shape, const),
        pl.BlockSpec(cg.shape, const),
        pl.BlockSpec(gmat.shape, const),
    ]
    return pl.pallas_call(
        _pre_kernel,
        out_shape=out_shape,
        grid=(bsz, seq // tm),
        in_specs=in_specs,
        out_specs=out_specs,
        scratch_shapes=[pltpu.VMEM((SUBLANES, CONV_WIDTH), F32)],
        compiler_params=pltpu.CompilerParams(dimension_semantics=("arbitrary", "arbitrary"),
                                             vmem_limit_bytes=VMEM_LIMIT_BYTES),
        name="pre_proj",
    )(x, mod, n1, wm, wvt, wki, wwit, qg, kg, cw, cg, gmat)


def _attn_kernel(rb_ref, bnd_ref, q_ref, qi_ref, wit_ref, k_ref, ki_ref, vt_ref, og_ref,
                 o_ref,
                 s_ref, s16_ref, bias_ref, qpad_ref, qipad_ref, lg_ref, acc_ref, out_ref, *, topk):
    b = pl.program_id(0)
    i = pl.program_id(1)
    tq, tk = ATT_TQ, ATT_TK
    nh, hd = ATTN_HEADS, HEAD_DIM

    t_loc = lax.broadcasted_iota(jnp.int32, (tk, tq), 1)
    s_loc = lax.broadcasted_iota(jnp.int32, (tk, tq), 0)

    @pl.when((b == 0) & (i == 0))
    def _():
        for idx in range(2):
            dist = t_loc - s_loc + idx * tq
            for h in range(nh):
                bias_ref[idx, h] = jnp.full((tk, tq), (rb_ref[0, h] - rb_ref[N_BUCKETS - 1, h]) * LOG2E, F32)

            def fill(jb, carry):
                reached = dist >= bnd_ref[jb]
                for h in range(nh):
                    val = (rb_ref[jb, h] - rb_ref[N_BUCKETS - 1, h]) * LOG2E
                    bias_ref[idx, h] = jnp.where(reached, val, bias_ref[idx, h])
                return carry

            lax.fori_loop(1, N_BUCKETS, fill, 0)

    lane = lax.broadcasted_iota(jnp.int32, (tq, LANES), 1)
    for h in range(nh):
        pair = slice((h // 2) * LANES, (h // 2 + 1) * LANES)
        keep = (lane // hd) == (h % 2)
        qpad_ref[h] = jnp.where(keep, q_ref[:, pair], jnp.zeros((), BF16))
        qipad_ref[h] = jnp.where(keep, qi_ref[:, pair], jnp.zeros((), BF16))

    def idx_dots(c, slot):
        kic = ki_ref[pl.ds(pl.multiple_of(c * tk, tk), tk), :]
        for h in range(nh):
            lg_ref[slot, h] = lax.dot_general(kic, qipad_ref[h], _NT_DIMS, preferred_element_type=F32)

    def idx_reduce(c, slot, carry, diagonal):
        rmin, rmax = carry
        sc = _tree_sum([wit_ref[h:h + 1, :] * jnp.maximum(lg_ref[slot, h], 0.0) for h in range(nh)])
        if diagonal:
            causal = s_loc <= t_loc
            lo_c, hi_c = jnp.where(causal, sc, jnp.inf), jnp.where(causal, sc, -jnp.inf)
            sc = hi_c
        else:
            lo_c, hi_c = sc, sc
        s_ref[c] = sc
        s16_ref[c] = sc.astype(BF16)
        return (jnp.minimum(rmin, jnp.min(lo_c, axis=0, keepdims=True)),
                jnp.maximum(rmax, jnp.max(hi_c, axis=0, keepdims=True)))

    def idx_pair(jj, carry):
        idx_dots(2 * jj + 1, 1)
        carry = idx_reduce(2 * jj, 0, carry, False)
        idx_dots(2 * jj + 2, 0)
        return idx_reduce(2 * jj + 1, 1, carry, False)

    def idx_tail_odd(carry):
        idx_dots(i, 1)
        return idx_reduce(i, 1, idx_reduce(i - 1, 0, carry, False), True)

    idx_dots(0, 0)
    carry = (jnp.full((1, tq), jnp.inf, F32), jnp.full((1, tq), -jnp.inf, F32))
    carry = lax.fori_loop(0, i // 2, idx_pair, carry)
    rmin, rmax = lax.cond((i & 1) == 1, idx_tail_odd, lambda cr: idx_reduce(i, 0, cr, True), carry)

    def count_ge(thr):
        def body(c, accs):
            hit = s_ref[c] >= thr
            accs = list(accs)
            for r in range(tk // SUBLANES):
                a = accs[r % COUNT_ACCS]
                accs[r % COUNT_ACCS] = jnp.where(hit[r * SUBLANES:(r + 1) * SUBLANES], a + 1.0, a)
            return tuple(accs)
        accs = lax.fori_loop(0, i + 1, body,
                             tuple(jnp.zeros((SUBLANES, tq), F32) for _ in range(COUNT_ACCS)))
        return jnp.sum(_tree_sum(list(accs)), axis=0, keepdims=True)

    def count16_ge(thr16):
        def body(c, accs):
            hit = s16_ref[c] >= thr16
            accs = list(accs)
            for r in range(tk // BF16_SUBLANES):
                a = accs[r % COUNT_ACCS]
                accs[r % COUNT_ACCS] = jnp.where(hit[r * BF16_SUBLANES:(r + 1) * BF16_SUBLANES], a + 1, a)
            return tuple(accs)
        accs = lax.fori_loop(0, i + 1, body,
                             tuple(jnp.zeros((BF16_SUBLANES, tq), BF16) for _ in range(COUNT_ACCS)))
        return jnp.sum(_tree_sum(list(accs)).astype(F32), axis=0, keepdims=True)

    int_min = jnp.int32(-2 ** 31)

    def order_key(v):
        bits = pltpu.bitcast(v, jnp.int32)
        return jnp.where(bits < 0, -(bits & jnp.int32(0x7FFFFFFF)), bits)

    def from_order_key(key):
        return pltpu.bitcast(jnp.where(key < 0, (-key) | int_min, key), F32)

    t_glob = (i * tq + lax.broadcasted_iota(jnp.int32, (1, tq), 1)).astype(F32)
    n_causal = t_glob + 1.0
    kf = jnp.minimum(float(topk), n_causal)
    all_sel = n_causal <= kf

    lo16_0 = order_key(rmin) >> 16
    hi16_0 = (order_key(rmax.astype(BF16).astype(F32) + 0.0) >> 16) + 1

    def c_cond(st):
        lo16, hi16, step = st
        return (jnp.max((hi16 - lo16).astype(F32)) > 1.0) & (step <= BISECT_MAX_STEPS)

    def c_body(st):
        lo16, hi16, step = st
        lo_v = from_order_key(lo16 << 16)
        hi_v = from_order_key(hi16 << 16)
        mid_val16 = order_key(lo_v + (hi_v - lo_v) * 0.5) >> 16
        mid_ord16 = (lo16 + hi16) >> 1
        use_val = (step < BISECT_VALUE_STEPS) & (mid_val16 > lo16) & (mid_val16 < hi16)
        mid16 = jnp.where(use_val, mid_val16, mid_ord16)
        open_ = (hi16 - lo16) > 1
        cm = count16_ge(from_order_key(mid16 << 16).astype(BF16))
        up = open_ & (cm >= kf)
        dn = open_ & (cm < kf)
        return jnp.where(up, mid16, lo16), jnp.where(dn, mid16, hi16), step + 1

    def c_group(st):
        for _ in range(BISECT_GROUP):
            st = c_body(st)
        return st

    lo16, hi16, _ = lax.while_loop(c_cond, c_group, (lo16_0, hi16_0, jnp.int32(0)))

    active0 = jnp.where(all_sel, 0.0, 1.0)
    thr0 = rmin
    tie0 = jnp.zeros((1, tq), F32)
    lo0 = jnp.maximum(from_order_key((lo16 - 1) << 16), rmin)
    hi0 = from_order_key(hi16 << 16)
    fhi0 = count_ge(hi0)
    hif0 = jnp.full((1, tq), jnp.inf, F32)
    need0 = kf

    def b_cond(st):
        return (jnp.max(st[0]) > 0.0) & (st[8] <= BISECT_MAX_STEPS)

    def b_body(st):
        active, lo, hi, fhi, thr, tie, hif, need, step = st
        lo_key = order_key(lo)
        hi_key = order_key(hi)
        mid_key = (lo_key >> 1) + (hi_key >> 1) + (lo_key & hi_key & 1)
        mid_val = lo + (hi - lo) * 0.5
        use_val = (step < BISECT_VALUE_STEPS) & (mid_val > lo) & (mid_val < hi)
        mid = jnp.where(use_val, mid_val, from_order_key(mid_key))
        collapsed = (mid_key == lo_key) | (step >= BISECT_MAX_STEPS)
        cm = count_ge(mid)
        act = active > 0.0
        live = act & jnp.logical_not(collapsed)
        found = live & (cm == kf)
        go_up = live & (cm > kf)
        go_dn = live & (cm < kf)
        ends_tie = act & collapsed
        thr = jnp.where(found, mid, jnp.where(ends_tie, lo, thr))
        tie = jnp.where(ends_tie, 1.0, tie)
        hif = jnp.where(ends_tie, hi, hif)
        need = jnp.where(ends_tie, kf - fhi, need)
        lo = jnp.where(go_up, mid, lo)
        fhi = jnp.where(go_dn, cm, fhi)
        hi = jnp.where(go_dn, mid, hi)
        active = jnp.where(found | ends_tie, 0.0, active)
        return active, lo, hi, fhi, thr, tie, hif, need, step + 1

    def b_group(st):
        for _ in range(BISECT_GROUP):
            st = b_body(st)
        return st

    _, _, _, _, thr, tie, hif, need, _ = lax.while_loop(
        b_cond, b_group, (active0, rmin, hi0, fhi0, thr0, tie0, hif0, need0, jnp.int32(0)))

    @pl.when(jnp.max(tie) > 0.0)
    def _():
        tri = jnp.where(lax.broadcasted_iota(jnp.int32, (tk, tk), 1)
                        <= lax.broadcasted_iota(jnp.int32, (tk, tk), 0), 1.0, 0.0).astype(BF16)

        def body(c, seen):
            sc_c = s_ref[c]
            tied = (sc_c >= thr) & (sc_c < hif) & (tie > 0.0)
            rank = jnp.dot(tri, jnp.where(tied, 1.0, 0.0).astype(BF16), preferred_element_type=F32) + seen
            s_ref[c] = jnp.where(tied & (rank > need), -jnp.inf, sc_c)
            return rank[tk - 1:tk, :]

        lax.fori_loop(0, i + 1, body, jnp.zeros((1, tq), F32))

    acc_ref[...] = jnp.zeros(acc_ref.shape, F32)

    def store_logits(c, slot, bias_idx):
        masked = jnp.where(s_ref[c] >= thr, 0.0, NEG_BIG)
        row0 = pl.multiple_of(c * tk, tk)
        for h in range(nh):
            kc = k_ref[pl.ds(row0, tk), (h // 2) * LANES:(h // 2 + 1) * LANES]
            lt = lax.dot_general(kc, qpad_ref[h], _NT_DIMS, preferred_element_type=F32) + masked
            if bias_idx is not None:
                lt = lt + bias_ref[bias_idx, h]
            lg_ref[slot, h] = lt

    def softmax_pv(c, slot, m_all):
        m_out = []
        for h in range(nh):
            m_old = m_all[h]
            m_new = jnp.maximum(m_old, jnp.max(lg_ref[slot, h], axis=0, keepdims=True))
            p = jnp.exp2(lg_ref[slot, h] - m_new).astype(BF16)
            alpha = jnp.exp2(m_old - m_new)
            pv = jnp.dot(vt_ref[c, h * V_SLAB:(h + 1) * V_SLAB, :], p, preferred_element_type=F32)
            acc_ref[h] = alpha * acc_ref[h] + pv
            m_out.append(m_new)
        return tuple(m_out)

    def near_step(m_all):
        store_logits(i - 1, 1, 1)
        return softmax_pv(i, 0, m_all)

    def far_step(j, parity, m_all):
        c = i - 2 - j
        store_logits(c, parity, None)
        return softmax_pv(c + 1, 1 - parity, m_all)

    def far_pair(jj, m_all):
        return far_step(2 * jj + 1, 1, far_step(2 * jj, 0, m_all))

    n_far = jnp.maximum(i - 1, 0)
    m_all = tuple(jnp.full((1, tq), NEG_BIG, F32) for _ in range(nh))
    store_logits(i, 0, 0)
    m_all = lax.cond(i >= 1, near_step, lambda m: m, m_all)
    m_all = lax.fori_loop(0, n_far // 2, far_pair, m_all)
    m_all = lax.cond((n_far & 1) == 1, lambda m: far_step(n_far - 1, 0, m), lambda m: m, m_all)
    lax.cond((i & 1) == 0, lambda m: softmax_pv(0, 0, m), lambda m: softmax_pv(0, 1, m), m_all)

    for h in range(nh):
        o = acc_ref[h, :hd, :] / acc_ref[h, hd:hd + 1, :]
        ms = jnp.mean(o * o, axis=0, keepdims=True)
        out_ref[h * hd:(h + 1) * hd, :] = o * lax.rsqrt(ms + EPS)
    o_ref[...] = (out_ref[...].T * og_ref[...]).astype(BF16)


def _attn_call(rel_bias, bounds, q, qi, wit, k, ki, vt, og, topk):
    bsz, seq, aw = q.shape
    tq, tk = ATT_TQ, ATT_TK
    nck = seq // tk
    blk_q = lambda b, i: (b, i, 0)
    whole = lambda b, i: (b, 0, 0)
    smem = pl.BlockSpec(memory_space=pltpu.SMEM)
    return pl.pallas_call(
        functools.partial(_attn_kernel, topk=topk),
        out_shape=jax.ShapeDtypeStruct((bsz, seq, aw), BF16),
        grid=(bsz, seq // tq),
        in_specs=[
            smem, smem,
            pl.BlockSpec((None, tq, aw), blk_q),
            pl.BlockSpec((None, tq, aw), blk_q),
            pl.BlockSpec((None, IDX_HEADS, tq), lambda b, i: (b, 0, i)),
            pl.BlockSpec((None, seq, aw), whole),
            pl.BlockSpec((None, seq, LANES), whole),
            pl.BlockSpec((None, nck, ATTN_HEADS * V_SLAB, tk), lambda b, i: (b, 0, 0, 0)),
            pl.BlockSpec(og.shape, lambda b, i: (0, 0)),
        ],
        out_specs=pl.BlockSpec((None, tq, aw), blk_q),
        scratch_shapes=[
            pltpu.VMEM((nck, tk, tq), F32),
            pltpu.VMEM((nck, tk, tq), BF16),
            pltpu.VMEM((2, ATTN_HEADS, tk, tq), F32),
            pltpu.VMEM((ATTN_HEADS, tq, LANES), BF16),
            pltpu.VMEM((IDX_HEADS, tq, LANES), BF16),
            pltpu.VMEM((2, ATTN_HEADS, tk, tq), F32),
            pltpu.VMEM((ATTN_HEADS, V_SLAB, tq), F32),
            pltpu.VMEM((aw, tq), F32),
        ],
        compiler_params=pltpu.CompilerParams(dimension_semantics=("arbitrary", "arbitrary"),
                                             vmem_limit_bytes=VMEM_LIMIT_BYTES),
        name="dsa_attention",
    )(rel_bias, bounds, q, qi, wit, k, ki, vt, og)


def _post_kernel(an_ref, cn_ref, x_ref, mod_ref, n2_ref, woa_ref, woc_ref, wr_ref, br_ref,
                 x1_ref, h2_ref, comb_ref, cnt_ref):
    mix = (jnp.dot(an_ref[...], woa_ref[...], preferred_element_type=F32)
           + jnp.dot(cn_ref[...], woc_ref[...], preferred_element_type=F32))
    x1 = x_ref[...] + mod_ref[2:3, :] * mix
    x1_ref[...] = x1
    ms = jnp.mean(x1 * x1, axis=-1, keepdims=True)
    h2 = x1 * lax.rsqrt(ms + EPS) * n2_ref[...] * (1.0 + mod_ref[4:5, :]) + mod_ref[3:4, :]
    h2b = h2.astype(BF16)
    h2_ref[...] = h2b

    logits = jnp.dot(h2b, wr_ref[...], preferred_element_type=F32) + br_ref[...]
    lane = lax.broadcasted_iota(jnp.int32, logits.shape, 1)
    lane_f = lane.astype(F32)
    far = float(LANES)
    is_g = (lane >= N_EXPERTS) & (lane < N_EXPERTS + N_GROUPS)
    gl = jnp.where(is_g, logits, -jnp.inf)
    gmax = jnp.max(gl, axis=-1, keepdims=True)
    g_sel = jnp.min(jnp.where(is_g & (gl == gmax), lane_f, far), axis=-1, keepdims=True) - float(N_EXPERTS)
    p_g = 1.0 / jnp.sum(jnp.exp(gl - gmax), axis=-1, keepdims=True)

    in_grp = (lane < N_EXPERTS) & ((lane // EXPERTS_PER_GROUP).astype(F32) == g_sel)
    e1 = jnp.where(in_grp, logits, -jnp.inf)
    l1 = jnp.max(e1, axis=-1, keepdims=True)
    i1 = jnp.min(jnp.where(in_grp & (e1 == l1), lane_f, far), axis=-1, keepdims=True)
    rest = in_grp & (lane_f != i1)
    e2 = jnp.where(rest, logits, -jnp.inf)
    l2 = jnp.max(e2, axis=-1, keepdims=True)
    i2 = jnp.min(jnp.where(rest & (e2 == l2), lane_f, far), axis=-1, keepdims=True)
    r = jnp.exp(l2 - l1)
    w1 = 1.0 / (1.0 + r)
    w2 = r / (1.0 + r)
    comb = jnp.where(lane_f == i1, p_g * w1, 0.0) + jnp.where(lane_f == i2, p_g * w2, 0.0)
    comb_ref[...] = comb
    cnt = jnp.sum(jnp.where(comb != 0.0, 1.0, 0.0), axis=0, keepdims=True)
    cnt_ref[...] = jnp.broadcast_to(cnt, cnt_ref.shape)


def _post_call(an, cn, x, mod, n2, woa, woc, wr, br):
    bsz, seq, d = x.shape
    tm = POST_TM
    tok = lambda b, j: (b, j, 0)
    const = lambda b, j: (0, 0)
    return pl.pallas_call(
        _post_kernel,
        out_shape=(jax.ShapeDtypeStruct((bsz, seq, d), F32),
                   jax.ShapeDtypeStruct((bsz, seq, d), BF16),
                   jax.ShapeDtypeStruct((bsz, seq, LANES), F32),
                   jax.ShapeDtypeStruct((bsz, seq // tm, SUBLANES, LANES), F32)),
        grid=(bsz, seq // tm),
        in_specs=[
            pl.BlockSpec((None, tm, ATTN_WIDTH), tok),
            pl.BlockSpec((None, tm, CONV_WIDTH), tok),
            pl.BlockSpec((None, tm, d), tok),
            pl.BlockSpec((None, 6, d), lambda b, j: (b, 0, 0)),
            pl.BlockSpec(n2.shape, const),
            pl.BlockSpec(woa.shape, const),
            pl.BlockSpec(woc.shape, const),
            pl.BlockSpec(wr.shape, const),
            pl.BlockSpec(br.shape, const),
        ],
        out_specs=(pl.BlockSpec((None, tm, d), tok),
                   pl.BlockSpec((None, tm, d), tok),
                   pl.BlockSpec((None, tm, LANES), tok),
                   pl.BlockSpec((None, None, SUBLANES, LANES), lambda b, j: (b, j, 0, 0))),
        compiler_params=pltpu.CompilerParams(dimension_semantics=("arbitrary", "arbitrary"),
                                             vmem_limit_bytes=VMEM_LIMIT_BYTES),
        name="post_router",
    )(an, cn, x, mod, n2, woa, woc, wr, br)


def _strict_tri(n, lower):
    r = lax.broadcasted_iota(jnp.int32, (n, n), 0)
    c = lax.broadcasted_iota(jnp.int32, (n, n), 1)
    return jnp.where((c < r) if lower else (r < c), 1.0, 0.0).astype(BF16)


def _moe_tile_copies(ntile_ref, lfirst_ref, gfirst_ref, blk, local_ref, global_ref, sem, to_global, wait):
    tile = MOE_TILE

    def per_expert(x, carry):
        lf = lfirst_ref[blk, x]
        gf = gfirst_ref[blk, x]

        def per_tile(j, c):
            loc = local_ref.at[pl.ds(pl.multiple_of((lf + j) * tile, tile), tile), :]
            glo = global_ref.at[pl.ds(pl.multiple_of((gf + j) * tile, tile), tile), :]
            cp = pltpu.make_async_copy(loc, glo, sem) if to_global else pltpu.make_async_copy(glo, loc, sem)
            if wait:
                cp.wait()
            else:
                cp.start()
            return c

        lax.fori_loop(0, ntile_ref[blk, x], per_tile, 0)
        return carry

    lax.fori_loop(0, N_EXPERTS, per_expert, 0)


def _moe_gather_kernel(ntile_ref, lfirst_ref, gfirst_ref, pad_ref,
                       h2_ref, comb_ref,
                       col_ref, xg_hbm,
                       xg_ref, row_ref, zero_ref, sem):
    blk = pl.program_id(0)
    nb = h2_ref.shape[0]
    tile, chunk = MOE_TILE, MOE_CHUNK
    lane = lax.broadcasted_iota(jnp.int32, (nb, LANES), 1)

    comb = comb_ref[...]
    assigned = comb != 0.0
    a_f = jnp.where(assigned, 1.0, 0.0)
    rank = jnp.dot(_strict_tri(nb, True), a_f.astype(BF16), preferred_element_type=F32)
    cnt = rank[nb - 1:nb, :] + a_f[nb - 1:nb, :]
    ntile = jnp.floor((cnt + float(tile - 1)) * (1.0 / tile))
    first = jnp.dot(jnp.broadcast_to(ntile, (SUBLANES, LANES)).astype(BF16), _strict_tri(LANES, False),
                    preferred_element_type=F32)[0:1, :]
    pos = first * float(tile) + rank
    pos1 = jnp.min(jnp.where(assigned, pos, 1e9), axis=1, keepdims=True)
    pos2 = jnp.max(jnp.where(assigned, pos, -1.0), axis=1, keepdims=True)
    pos2 = jnp.where(pos2 == pos1, -1.0, pos2)
    cw1 = jnp.sum(jnp.where(assigned & (pos == pos1), comb, 0.0), axis=1, keepdims=True)
    cw2 = jnp.sum(jnp.where(assigned & (pos == pos2), comb, 0.0), axis=1, keepdims=True)
    info = jnp.where(lane == 0, pos1, jnp.where(lane == 1, pos2, jnp.where(lane == 2, cw1,
                     jnp.where(lane == 3, cw2, 0.0))))
    col_ref[...] = info
    row_ref[...] = info.T

    total = lfirst_ref[blk, N_EXPERTS - 1] + ntile_ref[blk, N_EXPERTS - 1]
    n_chunks = (total * tile + (chunk - 1)) // chunk
    p1 = row_ref[0:1, :].astype(jnp.int32)
    p2 = row_ref[1:2, :].astype(jnp.int32)
    sub = lax.broadcasted_iota(jnp.int32, (chunk, nb), 0)

    def gather(c, carry):
        p = sub + c * chunk
        sel = jnp.where((p == p1) | (p == p2), 1.0, 0.0).astype(BF16)
        r0 = pl.multiple_of(c * chunk, chunk)
        xg_ref[pl.ds(r0, chunk), :] = jnp.dot(sel, h2_ref[...], preferred_element_type=F32).astype(BF16)
        return carry

    @pl.when(blk > 0)
    def _():
        _moe_tile_copies(ntile_ref, lfirst_ref, gfirst_ref, blk - 1, xg_ref, xg_hbm, sem, True, True)

    lax.fori_loop(0, n_chunks, gather, 0)

    _moe_tile_copies(ntile_ref, lfirst_ref, gfirst_ref, blk, xg_ref, xg_hbm, sem, True, False)

    is_last = blk == pl.num_programs(0) - 1

    def pad_copies(wait):
        def per_expert(x, carry):
            g0 = gfirst_ref[blk, x] + ntile_ref[blk, x]

            def per_tile(j, c):
                dst = xg_hbm.at[pl.ds(pl.multiple_of((g0 + j) * tile, tile), tile), :]
                cp = pltpu.make_async_copy(zero_ref, dst, sem)
                if wait:
                    cp.wait()
                else:
                    cp.start()
                return c

            lax.fori_loop(0, pad_ref[x], per_tile, 0)
            return carry

        lax.fori_loop(0, N_EXPERTS, per_expert, 0)

    @pl.when(is_last)
    def _():
        zero_ref[...] = jnp.zeros(zero_ref.shape, BF16)
        pad_copies(False)
        _moe_tile_copies(ntile_ref, lfirst_ref, gfirst_ref, blk, xg_ref, xg_hbm, sem, True, True)
        pad_copies(True)


def _moe_ffn_kernel(texp_ref, nt_ref, x_ref, wgu_ref, wd_ref, y_ref):
    @pl.when(pl.program_id(0) < nt_ref[0])
    def _():
        ab = jnp.dot(x_ref[...], wgu_ref[...], preferred_element_type=F32)
        a = ab[:, :EXPERT_FF]
        hid = ((a * jax.nn.sigmoid(a)) * ab[:, EXPERT_FF:]).astype(BF16)
        y_ref[...] = jnp.dot(hid, wd_ref[...], preferred_element_type=F32).astype(BF16)


def _moe_scatter_kernel(ntile_ref, lfirst_ref, gfirst_ref,
                        col_ref, x1_ref, mod_ref, y_hbm,
                        o_ref,
                        y_ref, sem):
    blk = pl.program_id(0)
    nb = x1_ref.shape[0]
    tile, chunk = MOE_TILE, MOE_CHUNK
    slot = blk & 1

    def copies(b, s, wait):
        _moe_tile_copies(ntile_ref, lfirst_ref, gfirst_ref, b, y_ref.at[s], y_hbm, sem.at[s], False, wait)

    @pl.when(blk == 0)
    def _():
        copies(0, 0, False)

    @pl.when(blk + 1 < pl.num_programs(0))
    def _():
        copies(blk + 1, 1 - slot, False)

    total = lfirst_ref[blk, N_EXPERTS - 1] + ntile_ref[blk, N_EXPERTS - 1]
    n_chunks = (total * tile + (chunk - 1)) // chunk
    max_chunks = y_ref.shape[1] // chunk
    usual = n_chunks <= MOE_USUAL_CHUNKS
    n_static = jnp.where(usual, MOE_USUAL_CHUNKS, max_chunks)

    def clear(t, carry):
        y_ref[slot, pl.ds(pl.multiple_of(t * tile, tile), tile), :] = jnp.zeros((tile, y_ref.shape[2]), BF16)
        return carry

    lax.fori_loop(total, n_static * (chunk // tile), clear, 0)

    p1 = col_ref[:, 0:1].astype(jnp.int32)
    p2 = col_ref[:, 1:2].astype(jnp.int32)
    cw1 = col_ref[:, 2:3]
    cw2 = col_ref[:, 3:4]
    gate = mod_ref[5:6, :]
    lane_c = lax.broadcasted_iota(jnp.int32, (nb, chunk), 1)
    copies(blk, slot, True)

    def scatter(n_unrolled):
        acc = None
        for c in range(n_unrolled):
            p = lane_c + c * chunk
            w = (jnp.where(p == p1, cw1, 0.0) + jnp.where(p == p2, cw2, 0.0)).astype(BF16)
            part = jnp.dot(w, y_ref[slot, c * chunk:(c + 1) * chunk, :], preferred_element_type=F32)
            acc = part if acc is None else acc + part
        o_ref[...] = x1_ref[...] + gate * acc

    lax.cond(usual, lambda: scatter(MOE_USUAL_CHUNKS), lambda: scatter(max_chunks))


def _moe_call(h2, comb, cnt_tiles, x1, mod, wgu, wd):
    bsz, seq, d = x1.shape
    nb, tile, ftm = MOE_TM, MOE_TILE, MOE_FFN_TM
    n_tok = bsz * seq
    n_blk = n_tok // nb
    region = ftm // tile
    rows_local = -(-(2 * nb + N_EXPERTS * tile) // MOE_CHUNK) * MOE_CHUNK
    tiles_global = (2 * n_tok) // tile + n_blk * N_EXPERTS + N_EXPERTS * (region - 1)
    n_ffn_max = -(-tiles_global // region)
    rows_global = n_ffn_max * ftm

    cnt = cnt_tiles[:, :, 0, :N_EXPERTS].reshape(n_blk, nb // POST_TM, N_EXPERTS).sum(axis=1).astype(jnp.int32)
    ntile = (cnt + (tile - 1)) // tile
    lfirst = jnp.cumsum(ntile, axis=1) - ntile
    tot = ntile.sum(axis=0)
    ptot = (tot + (region - 1)) // region * region
    ebase = jnp.cumsum(ptot) - ptot
    gfirst = ebase[None, :] + jnp.cumsum(ntile, axis=0) - ntile
    pad = ptot - tot
    n_ffn = (ptot.sum() // region).reshape(1)
    ends = jnp.cumsum(ptot) // region
    texp = jnp.minimum((jnp.arange(n_ffn_max, dtype=jnp.int32)[:, None] >= ends[None, :]).sum(axis=1),
                       N_EXPERTS - 1).astype(jnp.int32)

    h2f = h2.reshape(n_tok, d)
    combf = comb.reshape(n_tok, LANES)
    col, xg = pl.pallas_call(
        _moe_gather_kernel,
        out_shape=(jax.ShapeDtypeStruct((n_tok, LANES), F32),
                   jax.ShapeDtypeStruct((rows_global, d), BF16)),
        grid_spec=pltpu.PrefetchScalarGridSpec(
            num_scalar_prefetch=4,
            grid=(n_blk,),
            in_specs=[pl.BlockSpec((nb, d), lambda j, *_: (j, 0)),
                      pl.BlockSpec((nb, LANES), lambda j, *_: (j, 0))],
            out_specs=(pl.BlockSpec((nb, LANES), lambda j, *_: (j, 0)),
                       pl.BlockSpec(memory_space=pl.ANY)),
            scratch_shapes=[
                pltpu.VMEM((rows_local, d), BF16),
                pltpu.VMEM((LANES, nb), F32),
                pltpu.VMEM((tile, d), BF16),
                pltpu.SemaphoreType.DMA,
            ]),
        compiler_params=pltpu.CompilerParams(dimension_semantics=("arbitrary",),
                                             vmem_limit_bytes=VMEM_LIMIT_BYTES),
        name="moe_gather",
    )(ntile, lfirst, gfirst, pad, h2f, combf)

    last = lambda t, te, nt: jnp.minimum(t, nt[0] - 1)
    y = pl.pallas_call(
        _moe_ffn_kernel,
        out_shape=jax.ShapeDtypeStruct((rows_global, d), BF16),
        grid_spec=pltpu.PrefetchScalarGridSpec(
            num_scalar_prefetch=2,
            grid=(n_ffn_max,),
            in_specs=[pl.BlockSpec((ftm, d), lambda t, te, nt: (last(t, te, nt), 0)),
                      pl.BlockSpec((None, d, 2 * EXPERT_FF), lambda t, te, nt: (te[last(t, te, nt)], 0, 0)),
                      pl.BlockSpec((None, EXPERT_FF, d), lambda t, te, nt: (te[last(t, te, nt)], 0, 0))],
            out_specs=pl.BlockSpec((ftm, d), lambda t, te, nt: (last(t, te, nt), 0))),
        compiler_params=pltpu.CompilerParams(dimension_semantics=("arbitrary",),
                                             vmem_limit_bytes=VMEM_LIMIT_BYTES),
        name="moe_ffn",
    )(texp, n_ffn, xg, wgu, wd)

    out = pl.pallas_call(
        _moe_scatter_kernel,
        out_shape=jax.ShapeDtypeStruct((n_tok, d), F32),
        grid_spec=pltpu.PrefetchScalarGridSpec(
            num_scalar_prefetch=3,
            grid=(n_blk,),
            in_specs=[pl.BlockSpec((nb, LANES), lambda j, *_: (j, 0)),
                      pl.BlockSpec((nb, d), lambda j, *_: (j, 0)),
                      pl.BlockSpec((None, 6, d), lambda j, *_: ((j * nb) // seq, 0, 0)),
                      pl.BlockSpec(memory_space=pl.ANY)],
            out_specs=pl.BlockSpec((nb, d), lambda j, *_: (j, 0)),
            scratch_shapes=[pltpu.VMEM((2, rows_local, d), BF16),
                            pltpu.SemaphoreType.DMA((2,))]),
        compiler_params=pltpu.CompilerParams(dimension_semantics=("arbitrary",),
                                             vmem_limit_bytes=VMEM_LIMIT_BYTES),
        name="moe_scatter",
    )(ntile, lfirst, gfirst, col, x1.reshape(n_tok, d), mod, y)
    return out.reshape(bsz, seq, d)


def _layer(x, mod, rel_bias, norm1, w_in, q_norm, k_norm, conv_w, attn_out_norm, conv_out_norm, w_out,
           norm2, w_group_router, b_group_router, w_expert_router, b_expert_router, w_gate, w_up, w_down):
    bsz, seq, d = x.shape
    aw = ATTN_WIDTH
    topk = min(TOPK_MAX, seq // 4)

    offs = np.cumsum([0, aw, aw, aw, IDX_HEADS * IDX_DIM, IDX_DIM, IDX_HEADS, CONV_WIDTH, CONV_WIDTH, CONV_WIDTH])
    col = lambda n: w_in[:, int(offs[n]):int(offs[n + 1])]
    wm = jnp.concatenate([col(0), col(1), col(3), col(6), col(7), col(8)], axis=1).astype(BF16)
    wvt = col(2).T.astype(BF16)
    wki = jnp.concatenate([col(4), col(4)], axis=1).astype(BF16)
    wwit = col(5).T.astype(BF16)
    qg = (jnp.tile(q_norm, ATTN_HEADS) * ((HEAD_DIM ** -0.5) * LOG2E))[None, :]
    kg = jnp.tile(k_norm, ATTN_HEADS)[None, :]
    grp = np.arange(aw) // CONV_GROUP_DIM
    gmat = jnp.asarray((grp[:, None] == grp[None, :]).astype(np.float32) / CONV_GROUP_DIM, dtype=BF16)

    q, k, vt, qi, ki, wit, cn = _pre_call(
        x, mod, norm1[None, :], wm, wvt, wki, wwit, qg, kg, conv_w, conv_out_norm.reshape(1, -1), gmat)

    bounds = jnp.asarray(_bucket_boundaries())
    an = _attn_call(rel_bias, bounds, q, qi, wit, k, ki, vt, attn_out_norm.reshape(1, -1), topk)

    wr = jnp.concatenate([w_expert_router, w_group_router,
                          jnp.zeros((d, LANES - N_EXPERTS - N_GROUPS), F32)], axis=1).astype(BF16)
    br = jnp.concatenate([b_expert_router, b_group_router,
                          jnp.zeros((LANES - N_EXPERTS - N_GROUPS,), F32)])[None, :]
    x1, h2, comb, cnt_tiles = _post_call(an, cn, x, mod, norm2[None, :], w_out[:aw].astype(BF16),
                                         w_out[aw:].astype(BF16), wr, br)

    wgu = jnp.concatenate([w_gate, w_up], axis=-1).astype(BF16)
    return _moe_call(h2, comb, cnt_tiles, x1, mod, wgu, w_down.astype(BF16))


def kernel(x, c, rel_bias, w_ada, b_ada, norm1, w_in, q_norm, k_norm, conv_w, attn_out_norm, conv_out_norm,
           w_out, norm2, w_group_router, b_group_router, w_expert_router, b_expert_router, w_gate, w_up,
           w_down):
    bsz, seq, d = x.shape
    assert d == D_MODEL and seq % max(PRE_TM, POST_TM, MOE_TM) == 0 and ATT_TQ == ATT_TK
    depth = w_ada.shape[0]
    for l in range(depth):
        mod = _mod_call(c, w_ada[l], b_ada[l][None, :]).reshape(bsz, 6, d)
        x = _layer(x, mod, rel_bias, norm1[l], w_in[l], q_norm[l], k_norm[l], conv_w[l], attn_out_norm[l],
                   conv_out_norm[l], w_out[l], norm2[l], w_group_router[l], b_group_router[l],
                   w_expert_router[l], b_expert_router[l], w_gate[l], w_up[l], w_down[l])
    return x
```

```python
import functools
import math

import jax
import jax.numpy as jnp
import numpy as np
from jax import lax
from jax.experimental import pallas as pl
from jax.experimental.pallas import tpu as pltpu

F32 = jnp.float32
BF16 = jnp.bfloat16

D_MODEL = 1024
HEAD_DIM = 64
ATTN_HEADS = 8
ATTN_WIDTH = ATTN_HEADS * HEAD_DIM
CONV_WIDTH = D_MODEL - ATTN_WIDTH
CONV_GROUP_DIM = 64
CONV_K = 3
IDX_HEADS = 8
IDX_DIM = 64
TOPK_MAX = 256
IDX_SCALE = (IDX_DIM ** -0.5) * (IDX_HEADS ** -0.5)
N_BUCKETS = 32
MAX_DISTANCE = 128
N_GROUPS = 4
EXPERTS_PER_GROUP = 8
N_EXPERTS = N_GROUPS * EXPERTS_PER_GROUP
EXPERT_FF = 256
EPS = 1e-6
LOG2E = 1.4426950408889634
NEG_BIG = -1e30
COUNT_ACCS = 4
BISECT_GROUP = 4
BISECT_VALUE_STEPS = 8
BISECT_MAX_STEPS = 64

LANES = 128
SUBLANES = 8
BF16_SUBLANES = 16
V_SLAB = HEAD_DIM + BF16_SUBLANES
VMEM_LIMIT_BYTES = 56 * 1024 * 1024

PRE_TM = 512
ATT_TQ = 256
ATT_TK = 256
POST_TM = 512
MOE_TM = 512
MOE_TILE = 32
MOE_CHUNK = 512
MOE_USUAL_CHUNKS = 3
MOE_FFN_TM = 1024
MOD_TN = 1536

_NT_DIMS = (((1,), (1,)), ((), ()))


def _tree_sum(parts):
    while len(parts) > 1:
        nxt = [parts[j] + parts[j + 1] for j in range(0, len(parts) - 1, 2)]
        if len(parts) % 2:
            nxt.append(parts[-1])
        parts = nxt
    return parts[0]


def _bucket_boundaries():
    max_exact = N_BUCKETS // 2
    d = np.arange(0, 4 * MAX_DISTANCE, dtype=np.int64)
    nf = np.maximum(d, 1).astype(np.float32)
    large = max_exact + (np.log(nf / np.float32(max_exact)) / np.float32(math.log(MAX_DISTANCE / max_exact))
                         * np.float32(N_BUCKETS - max_exact)).astype(np.int32)
    large = np.minimum(large, N_BUCKETS - 1)
    bucket = np.where(d < max_exact, d, large)
    assert np.all(np.diff(bucket) >= 0) and bucket[-1] == N_BUCKETS - 1
    bounds = [int(np.argmax(bucket >= j)) for j in range(1, N_BUCKETS)]
    return np.asarray([0] + bounds, dtype=np.int32)


def _mod_kernel(c_ref, w_ref, b_ref, o_ref):
    c = c_ref[...]
    act = c * jax.nn.sigmoid(c)
    o_ref[...] = jnp.dot(act, w_ref[...], preferred_element_type=F32,
                         precision=lax.Precision.HIGHEST) + b_ref[...]


def _mod_call(c, w_ada, b_ada):
    bsz, d = c.shape
    n = w_ada.shape[1]
    return pl.pallas_call(
        _mod_kernel,
        out_shape=jax.ShapeDtypeStruct((bsz, n), F32),
        grid=(n // MOD_TN,),
        in_specs=[pl.BlockSpec((bsz, d), lambda j: (0, 0)),
                  pl.BlockSpec((d, MOD_TN), lambda j: (0, j)),
                  pl.BlockSpec((1, MOD_TN), lambda j: (0, j))],
        out_specs=pl.BlockSpec((bsz, MOD_TN), lambda j: (0, j)),
        compiler_params=pltpu.CompilerParams(dimension_semantics=("arbitrary",),
                                             vmem_limit_bytes=VMEM_LIMIT_BYTES),
        name="adaln_mod",
    )(c, w_ada, b_ada)


def _group_rms(y, g_ref):
    ms = jnp.dot((y * y).astype(BF16), g_ref[...], preferred_element_type=F32)
    return y * lax.rsqrt(ms + EPS)


def _pre_kernel(x_ref, mod_ref, n1_ref, wm_ref, wvt_ref, wki_ref, wwit_ref, qg_ref, kg_ref,
                cw_ref, cg_ref, g_ref,
                q_ref, k_ref, vt_ref, qi_ref, ki_ref, wit_ref, cn_ref, carry_ref):
    j = pl.program_id(1)
    tm = x_ref.shape[0]
    aw = ATTN_WIDTH

    x = x_ref[...]
    ms = jnp.mean(x * x, axis=-1, keepdims=True)
    y = x * lax.rsqrt(ms + EPS) * n1_ref[...]
    h = y * (1.0 + mod_ref[1:2, :]) + mod_ref[0:1, :]
    hb = h.astype(BF16)

    def proj(lo):
        return jnp.dot(hb, wm_ref[:, lo:lo + aw], preferred_element_type=F32)

    q = _group_rms(proj(0), g_ref) * qg_ref[...]
    q_ref[...] = q.astype(BF16)
    k = _group_rms(proj(aw), g_ref) * kg_ref[...]
    k_ref[...] = k.astype(BF16)

    vt = lax.dot_general(wvt_ref[...], hb, _NT_DIMS, preferred_element_type=F32).astype(BF16)
    ones = jnp.ones((BF16_SUBLANES, ATT_TK), BF16)
    for cc in range(tm // ATT_TK):
        for hh in range(ATTN_HEADS):
            vt_ref[cc, hh * V_SLAB:hh * V_SLAB + HEAD_DIM, :] = (
                vt[hh * HEAD_DIM:(hh + 1) * HEAD_DIM, cc * ATT_TK:(cc + 1) * ATT_TK])
            vt_ref[cc, hh * V_SLAB + HEAD_DIM:(hh + 1) * V_SLAB, :] = ones

    qi_ref[...] = proj(2 * aw).astype(BF16)
    ki_ref[...] = jnp.dot(hb, wki_ref[...], preferred_element_type=F32).astype(BF16)
    wit_ref[...] = lax.dot_general(wwit_ref[...], hb, _NT_DIMS, preferred_element_type=F32) * IDX_SCALE

    gate_b = proj(3 * aw)
    z = proj(4 * aw) * proj(5 * aw)

    @pl.when(j == 0)
    def _():
        carry_ref[...] = jnp.zeros_like(carry_ref)

    prev = carry_ref[...]
    row = lax.broadcasted_iota(jnp.int32, z.shape, 0)
    z1 = jnp.where(row == 0, prev[SUBLANES - 1:SUBLANES, :], pltpu.roll(z, 1, 0))
    z2 = pltpu.roll(z, 2, 0)
    z2 = jnp.where(row == 0, prev[SUBLANES - 2:SUBLANES - 1, :], z2)
    z2 = jnp.where(row == 1, prev[SUBLANES - 1:SUBLANES, :], z2)
    carry_ref[...] = z[tm - SUBLANES:, :]
    conv = cw_ref[2:3, :] * z + cw_ref[1:2, :] * z1 + cw_ref[0:1, :] * z2
    yc = gate_b * conv
    cn_ref[...] = (_group_rms(yc, g_ref) * cg_ref[...]).astype(BF16)


def _pre_call(x, mod, n1, wm, wvt, wki, wwit, qg, kg, cw, cg, gmat):
    bsz, seq, d = x.shape
    tm = PRE_TM
    nck = tm // ATT_TK
    aw = ATTN_WIDTH
    const = lambda b, j: (0, 0)
    tok = lambda b, j: (b, j, 0)
    out_shape = (
        jax.ShapeDtypeStruct((bsz, seq, aw), BF16),
        jax.ShapeDtypeStruct((bsz, seq, aw), BF16),
        jax.ShapeDtypeStruct((bsz, seq // ATT_TK, ATTN_HEADS * V_SLAB, ATT_TK), BF16),
        jax.ShapeDtypeStruct((bsz, seq, aw), BF16),
        jax.ShapeDtypeStruct((bsz, seq, LANES), BF16),
        jax.ShapeDtypeStruct((bsz, IDX_HEADS, seq), F32),
        jax.ShapeDtypeStruct((bsz, seq, CONV_WIDTH), BF16),
    )
    out_specs = (
        pl.BlockSpec((None, tm, aw), tok),
        pl.BlockSpec((None, tm, aw), tok),
        pl.BlockSpec((None, nck, ATTN_HEADS * V_SLAB, ATT_TK), lambda b, j: (b, j, 0, 0)),
        pl.BlockSpec((None, tm, aw), tok),
        pl.BlockSpec((None, tm, LANES), tok),
        pl.BlockSpec((None, IDX_HEADS, tm), lambda b, j: (b, 0, j)),
        pl.BlockSpec((None, tm, CONV_WIDTH), tok),
    )
    in_specs = [
        pl.BlockSpec((None, tm, d), tok),
        pl.BlockSpec((None, 6, d), lambda b, j: (b, 0, 0)),
        pl.BlockSpec(n1.shape, const),
        pl.BlockSpec(wm.shape, const),
        pl.BlockSpec(wvt.shape, const),
        pl.BlockSpec(wki.shape, const),
        pl.BlockSpec(wwit.shape, const),
        pl.BlockSpec(qg.shape, const),
        pl.BlockSpec(kg.shape, const),
        pl.BlockSpec(cw.shape, const),
        pl.BlockSpec(cg.shape, const),
        pl.BlockSpec(gmat.shape, const),
    ]
    return pl.pallas_call(
        _pre_kernel,
        out_shape=out_shape,
        grid=(bsz, seq // tm),
        in_specs=in_specs,
        out_specs=out_specs,
        scratch_shapes=[pltpu.VMEM((SUBLANES, CONV_WIDTH), F32)],
        compiler_params=pltpu.CompilerParams(dimension_semantics=("arbitrary", "arbitrary"),
                                             vmem_limit_bytes=VMEM_LIMIT_BYTES),
        name="pre_proj",
    )(x, mod, n1, wm, wvt, wki, wwit, qg, kg, cw, cg, gmat)


def _attn_kernel(rb_ref, bnd_ref, q_ref, qi_ref, wit_ref, k_ref, ki_ref, vt_ref, og_ref,
                 o_ref,
                 s_ref, bias_ref, qpad_ref, qipad_ref, lg_ref, acc_ref, out_ref, *, topk):
    b = pl.program_id(0)
    i = pl.program_id(1)
    tq, tk = ATT_TQ, ATT_TK
    nh, hd = ATTN_HEADS, HEAD_DIM

    t_loc = lax.broadcasted_iota(jnp.int32, (tk, tq), 1)
    s_loc = lax.broadcasted_iota(jnp.int32, (tk, tq), 0)

    @pl.when((b == 0) & (i == 0))
    def _():
        for idx in range(2):
            dist = t_loc - s_loc + idx * tq
            for h in range(nh):
                bias_ref[idx, h] = jnp.full((tk, tq), (rb_ref[0, h] - rb_ref[N_BUCKETS - 1, h]) * LOG2E, F32)

            def fill(jb, carry):
                reached = dist >= bnd_ref[jb]
                for h in range(nh):
                    val = (rb_ref[jb, h] - rb_ref[N_BUCKETS - 1, h]) * LOG2E
                    bias_ref[idx, h] = jnp.where(reached, val, bias_ref[idx, h])
                return carry

            lax.fori_loop(1, N_BUCKETS, fill, 0)

    lane = lax.broadcasted_iota(jnp.int32, (tq, LANES), 1)
    for h in range(nh):
        pair = slice((h // 2) * LANES, (h // 2 + 1) * LANES)
        keep = (lane // hd) == (h % 2)
        qpad_ref[h] = jnp.where(keep, q_ref[:, pair], jnp.zeros((), BF16))
        qipad_ref[h] = jnp.where(keep, qi_ref[:, pair], jnp.zeros((), BF16))

    def idx_dots(c, slot):
        kic = ki_ref[pl.ds(pl.multiple_of(c * tk, tk), tk), :]
        for h in range(nh):
            lg_ref[slot, h] = lax.dot_general(kic, qipad_ref[h], _NT_DIMS, preferred_element_type=F32)

    def idx_reduce(c, slot, carry, diagonal):
        rmin, rmax = carry
        sc = _tree_sum([wit_ref[h:h + 1, :] * jnp.maximum(lg_ref[slot, h], 0.0) for h in range(nh)])
        if diagonal:
            causal = s_loc <= t_loc
            s_ref[c] = jnp.where(causal, sc, -jnp.inf)
            lo_c, hi_c = jnp.where(causal, sc, jnp.inf), jnp.where(causal, sc, -jnp.inf)
        else:
            s_ref[c] = sc
            lo_c, hi_c = sc, sc
        return (jnp.minimum(rmin, jnp.min(lo_c, axis=0, keepdims=True)),
                jnp.maximum(rmax, jnp.max(hi_c, axis=0, keepdims=True)))

    def idx_pair(jj, carry):
        idx_dots(2 * jj + 1, 1)
        carry = idx_reduce(2 * jj, 0, carry, False)
        idx_dots(2 * jj + 2, 0)
        return idx_reduce(2 * jj + 1, 1, carry, False)

    def idx_tail_odd(carry):
        idx_dots(i, 1)
        return idx_reduce(i, 1, idx_reduce(i - 1, 0, carry, False), True)

    idx_dots(0, 0)
    carry = (jnp.full((1, tq), jnp.inf, F32), jnp.full((1, tq), -jnp.inf, F32))
    carry = lax.fori_loop(0, i // 2, idx_pair, carry)
    rmin, rmax = lax.cond((i & 1) == 1, idx_tail_odd, lambda cr: idx_reduce(i, 0, cr, True), carry)

    def count_ge(thr):
        def body(c, accs):
            hit = s_ref[c] >= thr
            accs = list(accs)
            for r in range(tk // SUBLANES):
                a = accs[r % COUNT_ACCS]
                accs[r % COUNT_ACCS] = jnp.where(hit[r * SUBLANES:(r + 1) * SUBLANES], a + 1.0, a)
            return tuple(accs)
        accs = lax.fori_loop(0, i + 1, body,
                             tuple(jnp.zeros((SUBLANES, tq), F32) for _ in range(COUNT_ACCS)))
        return jnp.sum(_tree_sum(list(accs)), axis=0, keepdims=True)

    def order_key(v):
        bits = pltpu.bitcast(v, jnp.int32)
        return jnp.where(bits < 0, bits ^ jnp.int32(0x7FFFFFFF), bits)

    def from_order_key(key):
        return pltpu.bitcast(jnp.where(key < 0, key ^ jnp.int32(0x7FFFFFFF), key), F32)

    t_glob = (i * tq + lax.broadcasted_iota(jnp.int32, (1, tq), 1)).astype(F32)
    n_causal = t_glob + 1.0
    kf = jnp.minimum(float(topk), n_causal)
    all_sel = n_causal <= kf
    active0 = jnp.where(all_sel, 0.0, 1.0)
    thr0 = rmin
    tie0 = jnp.zeros((1, tq), F32)
    hi0 = from_order_key(order_key(rmax + 0.0) + 1)
    fhi0 = jnp.zeros((1, tq), F32)
    hif0 = jnp.full((1, tq), jnp.inf, F32)
    need0 = kf

    def b_cond(st):
        return (jnp.max(st[0]) > 0.0) & (st[8] <= BISECT_MAX_STEPS)

    def b_body(st):
        active, lo, hi, fhi, thr, tie, hif, need, step = st
        lo_key = order_key(lo)
        hi_key = order_key(hi)
        mid_key = (lo_key >> 1) + (hi_key >> 1) + (lo_key & hi_key & 1)
        mid_val = lo + (hi - lo) * 0.5
        use_val = (step < BISECT_VALUE_STEPS) & (mid_val > lo) & (mid_val < hi)
        mid = jnp.where(use_val, mid_val, from_order_key(mid_key))
        collapsed = (mid_key == lo_key) | (step >= BISECT_MAX_STEPS)
        cm = count_ge(mid)
        act = active > 0.0
        live = act & jnp.logical_not(collapsed)
        found = live & (cm == kf)
        go_up = live & (cm > kf)
        go_dn = live & (cm < kf)
        ends_tie = act & collapsed
        thr = jnp.where(found, mid, jnp.where(ends_tie, lo, thr))
        tie = jnp.where(ends_tie, 1.0, tie)
        hif = jnp.where(ends_tie, hi, hif)
        need = jnp.where(ends_tie, kf - fhi, need)
        lo = jnp.where(go_up, mid, lo)
        fhi = jnp.where(go_dn, cm, fhi)
        hi = jnp.where(go_dn, mid, hi)
        active = jnp.where(found | ends_tie, 0.0, active)
        return active, lo, hi, fhi, thr, tie, hif, need, step + 1

    def b_group(st):
        for _ in range(BISECT_GROUP):
            st = b_body(st)
        return st

    _, _, _, _, thr, tie, hif, need, _ = lax.while_loop(
        b_cond, b_group, (active0, rmin, hi0, fhi0, thr0, tie0, hif0, need0, jnp.int32(0)))

    @pl.when(jnp.max(tie) > 0.0)
    def _():
        tri = jnp.where(lax.broadcasted_iota(jnp.int32, (tk, tk), 1)
                        <= lax.broadcasted_iota(jnp.int32, (tk, tk), 0), 1.0, 0.0).astype(BF16)

        def body(c, seen):
            sc_c = s_ref[c]
            tied = (sc_c >= thr) & (sc_c < hif) & (tie > 0.0)
            rank = jnp.dot(tri, jnp.where(tied, 1.0, 0.0).astype(BF16), preferred_element_type=F32) + seen
            s_ref[c] = jnp.where(tied & (rank > need), -jnp.inf, sc_c)
            return rank[tk - 1:tk, :]

        lax.fori_loop(0, i + 1, body, jnp.zeros((1, tq), F32))

    acc_ref[...] = jnp.zeros(acc_ref.shape, F32)

    def store_logits(c, slot, bias_idx):
        masked = jnp.where(s_ref[c] >= thr, 0.0, NEG_BIG)
        row0 = pl.multiple_of(c * tk, tk)
        for h in range(nh):
            kc = k_ref[pl.ds(row0, tk), (h // 2) * LANES:(h // 2 + 1) * LANES]
            lt = lax.dot_general(kc, qpad_ref[h], _NT_DIMS, preferred_element_type=F32) + masked
            if bias_idx is not None:
                lt = lt + bias_ref[bias_idx, h]
            lg_ref[slot, h] = lt

    def softmax_pv(c, slot, m_all):
        m_out = []
        for h in range(nh):
            m_old = m_all[h]
            m_new = jnp.maximum(m_old, jnp.max(lg_ref[slot, h], axis=0, keepdims=True))
            p = jnp.exp2(lg_ref[slot, h] - m_new).astype(BF16)
            alpha = jnp.exp2(m_old - m_new)
            pv = jnp.dot(vt_ref[c, h * V_SLAB:(h + 1) * V_SLAB, :], p, preferred_element_type=F32)
            acc_ref[h] = alpha * acc_ref[h] + pv
            m_out.append(m_new)
        return tuple(m_out)

    def near_step(m_all):
        store_logits(i - 1, 1, 1)
        return softmax_pv(i, 0, m_all)

    def far_step(j, parity, m_all):
        c = i - 2 - j
        store_logits(c, parity, None)
        return softmax_pv(c + 1, 1 - parity, m_all)

    def far_pair(jj, m_all):
        return far_step(2 * jj + 1, 1, far_step(2 * jj, 0, m_all))

    n_far = jnp.maximum(i - 1, 0)
    m_all = tuple(jnp.full((1, tq), NEG_BIG, F32) for _ in range(nh))
    store_logits(i, 0, 0)
    m_all = lax.cond(i >= 1, near_step, lambda m: m, m_all)
    m_all = lax.fori_loop(0, n_far // 2, far_pair, m_all)
    m_all = lax.cond((n_far & 1) == 1, lambda m: far_step(n_far - 1, 0, m), lambda m: m, m_all)
    lax.cond((i & 1) == 0, lambda m: softmax_pv(0, 0, m), lambda m: softmax_pv(0, 1, m), m_all)

    for h in range(nh):
        o = acc_ref[h, :hd, :] / acc_ref[h, hd:hd + 1, :]
        ms = jnp.mean(o * o, axis=0, keepdims=True)
        out_ref[h * hd:(h + 1) * hd, :] = o * lax.rsqrt(ms + EPS)
    o_ref[...] = (out_ref[...].T * og_ref[...]).astype(BF16)


def _attn_call(rel_bias, bounds, q, qi, wit, k, ki, vt, og, topk):
    bsz, seq, aw = q.shape
    tq, tk = ATT_TQ, ATT_TK
    nck = seq // tk
    blk_q = lambda b, i: (b, i, 0)
    whole = lambda b, i: (b, 0, 0)
    smem = pl.BlockSpec(memory_space=pltpu.SMEM)
    return pl.pallas_call(
        functools.partial(_attn_kernel, topk=topk),
        out_shape=jax.ShapeDtypeStruct((bsz, seq, aw), BF16),
        grid=(bsz, seq // tq),
        in_specs=[
            smem, smem,
            pl.BlockSpec((None, tq, aw), blk_q),
            pl.BlockSpec((None, tq, aw), blk_q),
            pl.BlockSpec((None, IDX_HEADS, tq), lambda b, i: (b, 0, i)),
            pl.BlockSpec((None, seq, aw), whole),
            pl.BlockSpec((None, seq, LANES), whole),
            pl.BlockSpec((None, nck, ATTN_HEADS * V_SLAB, tk), lambda b, i: (b, 0, 0, 0)),
            pl.BlockSpec(og.shape, lambda b, i: (0, 0)),
        ],
        out_specs=pl.BlockSpec((None, tq, aw), blk_q),
        scratch_shapes=[
            pltpu.VMEM((nck, tk, tq), F32),
            pltpu.VMEM((2, ATTN_HEADS, tk, tq), F32),
            pltpu.VMEM((ATTN_HEADS, tq, LANES), BF16),
            pltpu.VMEM((IDX_HEADS, tq, LANES), BF16),
            pltpu.VMEM((2, ATTN_HEADS, tk, tq), F32),
            pltpu.VMEM((ATTN_HEADS, V_SLAB, tq), F32),
            pltpu.VMEM((aw, tq), F32),
        ],
        compiler_params=pltpu.CompilerParams(dimension_semantics=("arbitrary", "arbitrary"),
                                             vmem_limit_bytes=VMEM_LIMIT_BYTES),
        name="dsa_attention",
    )(rel_bias, bounds, q, qi, wit, k, ki, vt, og)


def _post_kernel(an_ref, cn_ref, x_ref, mod_ref, n2_ref, woa_ref, woc_ref, wr_ref, br_ref,
                 x1_ref, h2_ref, comb_ref, cnt_ref):
    mix = (jnp.dot(an_ref[...], woa_ref[...], preferred_element_type=F32)
           + jnp.dot(cn_ref[...], woc_ref[...], preferred_element_type=F32))
    x1 = x_ref[...] + mod_ref[2:3, :] * mix
    x1_ref[...] = x1
    ms = jnp.mean(x1 * x1, axis=-1, keepdims=True)
    h2 = x1 * lax.rsqrt(ms + EPS) * n2_ref[...] * (1.0 + mod_ref[4:5, :]) + mod_ref[3:4, :]
    h2b = h2.astype(BF16)
    h2_ref[...] = h2b

    logits = jnp.dot(h2b, wr_ref[...], preferred_element_type=F32) + br_ref[...]
    lane = lax.broadcasted_iota(jnp.int32, logits.shape, 1)
    lane_f = lane.astype(F32)
    far = float(LANES)
    is_g = (lane >= N_EXPERTS) & (lane < N_EXPERTS + N_GROUPS)
    gl = jnp.where(is_g, logits, -jnp.inf)
    gmax = jnp.max(gl, axis=-1, keepdims=True)
    g_sel = jnp.min(jnp.where(is_g & (gl == gmax), lane_f, far), axis=-1, keepdims=True) - float(N_EXPERTS)
    p_g = 1.0 / jnp.sum(jnp.exp(gl - gmax), axis=-1, keepdims=True)

    in_grp = (lane < N_EXPERTS) & ((lane // EXPERTS_PER_GROUP).astype(F32) == g_sel)
    e1 = jnp.where(in_grp, logits, -jnp.inf)
    l1 = jnp.max(e1, axis=-1, keepdims=True)
    i1 = jnp.min(jnp.where(in_grp & (e1 == l1), lane_f, far), axis=-1, keepdims=True)
    rest = in_grp & (lane_f != i1)
    e2 = jnp.where(rest, logits, -jnp.inf)
    l2 = jnp.max(e2, axis=-1, keepdims=True)
    i2 = jnp.min(jnp.where(rest & (e2 == l2), lane_f, far), axis=-1, keepdims=True)
    r = jnp.exp(l2 - l1)
    w1 = 1.0 / (1.0 + r)
    w2 = r / (1.0 + r)
    comb = jnp.where(lane_f == i1, p_g * w1, 0.0) + jnp.where(lane_f == i2, p_g * w2, 0.0)
    comb_ref[...] = comb
    cnt = jnp.sum(jnp.where(comb != 0.0, 1.0, 0.0), axis=0, keepdims=True)
    cnt_ref[...] = jnp.broadcast_to(cnt, cnt_ref.shape)


def _post_call(an, cn, x, mod, n2, woa, woc, wr, br):
    bsz, seq, d = x.shape
    tm = POST_TM
    tok = lambda b, j: (b, j, 0)
    const = lambda b, j: (0, 0)
    return pl.pallas_call(
        _post_kernel,
        out_shape=(jax.ShapeDtypeStruct((bsz, seq, d), F32),
                   jax.ShapeDtypeStruct((bsz, seq, d), BF16),
                   jax.ShapeDtypeStruct((bsz, seq, LANES), F32),
                   jax.ShapeDtypeStruct((bsz, seq // tm, SUBLANES, LANES), F32)),
        grid=(bsz, seq // tm),
        in_specs=[
            pl.BlockSpec((None, tm, ATTN_WIDTH), tok),
            pl.BlockSpec((None, tm, CONV_WIDTH), tok),
            pl.BlockSpec((None, tm, d), tok),
            pl.BlockSpec((None, 6, d), lambda b, j: (b, 0, 0)),
            pl.BlockSpec(n2.shape, const),
            pl.BlockSpec(woa.shape, const),
            pl.BlockSpec(woc.shape, const),
            pl.BlockSpec(wr.shape, const),
            pl.BlockSpec(br.shape, const),
        ],
        out_specs=(pl.BlockSpec((None, tm, d), tok),
                   pl.BlockSpec((None, tm, d), tok),
                   pl.BlockSpec((None, tm, LANES), tok),
                   pl.BlockSpec((None, None, SUBLANES, LANES), lambda b, j: (b, j, 0, 0))),
        compiler_params=pltpu.CompilerParams(dimension_semantics=("arbitrary", "arbitrary"),
                                             vmem_limit_bytes=VMEM_LIMIT_BYTES),
        name="post_router",
    )(an, cn, x, mod, n2, woa, woc, wr, br)


def _strict_tri(n, lower):
    r = lax.broadcasted_iota(jnp.int32, (n, n), 0)
    c = lax.broadcasted_iota(jnp.int32, (n, n), 1)
    return jnp.where((c < r) if lower else (r < c), 1.0, 0.0).astype(BF16)


def _moe_tile_copies(ntile_ref, lfirst_ref, gfirst_ref, blk, local_ref, global_ref, sem, to_global, wait):
    tile = MOE_TILE

    def per_expert(x, carry):
        lf = lfirst_ref[blk, x]
        gf = gfirst_ref[blk, x]

        def per_tile(j, c):
            loc = local_ref.at[pl.ds(pl.multiple_of((lf + j) * tile, tile), tile), :]
            glo = global_ref.at[pl.ds(pl.multiple_of((gf + j) * tile, tile), tile), :]
            cp = pltpu.make_async_copy(loc, glo, sem) if to_global else pltpu.make_async_copy(glo, loc, sem)
            if wait:
                cp.wait()
            else:
                cp.start()
            return c

        lax.fori_loop(0, ntile_ref[blk, x], per_tile, 0)
        return carry

    lax.fori_loop(0, N_EXPERTS, per_expert, 0)


def _moe_gather_kernel(ntile_ref, lfirst_ref, gfirst_ref, pad_ref,
                       h2_ref, comb_ref,
                       col_ref, xg_hbm,
                       xg_ref, row_ref, zero_ref, sem):
    blk = pl.program_id(0)
    nb = h2_ref.shape[0]
    tile, chunk = MOE_TILE, MOE_CHUNK
    lane = lax.broadcasted_iota(jnp.int32, (nb, LANES), 1)

    comb = comb_ref[...]
    assigned = comb != 0.0
    a_f = jnp.where(assigned, 1.0, 0.0)
    rank = jnp.dot(_strict_tri(nb, True), a_f.astype(BF16), preferred_element_type=F32)
    cnt = rank[nb - 1:nb, :] + a_f[nb - 1:nb, :]
    ntile = jnp.floor((cnt + float(tile - 1)) * (1.0 / tile))
    first = jnp.dot(jnp.broadcast_to(ntile, (SUBLANES, LANES)).astype(BF16), _strict_tri(LANES, False),
                    preferred_element_type=F32)[0:1, :]
    pos = first * float(tile) + rank
    pos1 = jnp.min(jnp.where(assigned, pos, 1e9), axis=1, keepdims=True)
    pos2 = jnp.max(jnp.where(assigned, pos, -1.0), axis=1, keepdims=True)
    pos2 = jnp.where(pos2 == pos1, -1.0, pos2)
    cw1 = jnp.sum(jnp.where(assigned & (pos == pos1), comb, 0.0), axis=1, keepdims=True)
    cw2 = jnp.sum(jnp.where(assigned & (pos == pos2), comb, 0.0), axis=1, keepdims=True)
    info = jnp.where(lane == 0, pos1, jnp.where(lane == 1, pos2, jnp.where(lane == 2, cw1,
                     jnp.where(lane == 3, cw2, 0.0))))
    col_ref[...] = info
    row_ref[...] = info.T

    total = lfirst_ref[blk, N_EXPERTS - 1] + ntile_ref[blk, N_EXPERTS - 1]
    n_chunks = (total * tile + (chunk - 1)) // chunk
    p1 = row_ref[0:1, :].astype(jnp.int32)
    p2 = row_ref[1:2, :].astype(jnp.int32)
    sub = lax.broadcasted_iota(jnp.int32, (chunk, nb), 0)

    def gather(c, carry):
        p = sub + c * chunk
        sel = jnp.where((p == p1) | (p == p2), 1.0, 0.0).astype(BF16)
        r0 = pl.multiple_of(c * chunk, chunk)
        xg_ref[pl.ds(r0, chunk), :] = jnp.dot(sel, h2_ref[...], preferred_element_type=F32).astype(BF16)
        return carry

    @pl.when(blk > 0)
    def _():
        _moe_tile_copies(ntile_ref, lfirst_ref, gfirst_ref, blk - 1, xg_ref, xg_hbm, sem, True, True)

    lax.fori_loop(0, n_chunks, gather, 0)

    _moe_tile_copies(ntile_ref, lfirst_ref, gfirst_ref, blk, xg_ref, xg_hbm, sem, True, False)

    is_last = blk == pl.num_programs(0) - 1

    def pad_copies(wait):
        def per_expert(x, carry):
            g0 = gfirst_ref[blk, x] + ntile_ref[blk, x]

            def per_tile(j, c):
                dst = xg_hbm.at[pl.ds(pl.multiple_of((g0 + j) * tile, tile), tile), :]
                cp = pltpu.make_async_copy(zero_ref, dst, sem)
                if wait:
                    cp.wait()
                else:
                    cp.start()
                return c

            lax.fori_loop(0, pad_ref[x], per_tile, 0)
            return carry

        lax.fori_loop(0, N_EXPERTS, per_expert, 0)

    @pl.when(is_last)
    def _():
        zero_ref[...] = jnp.zeros(zero_ref.shape, BF16)
        pad_copies(False)
        _moe_tile_copies(ntile_ref, lfirst_ref, gfirst_ref, blk, xg_ref, xg_hbm, sem, True, True)
        pad_copies(True)


def _moe_ffn_kernel(texp_ref, nt_ref, x_ref, wgu_ref, wd_ref, y_ref):
    @pl.when(pl.program_id(0) < nt_ref[0])
    def _():
        ab = jnp.dot(x_ref[...], wgu_ref[...], preferred_element_type=F32)
        a = ab[:, :EXPERT_FF]
        hid = ((a * jax.nn.sigmoid(a)) * ab[:, EXPERT_FF:]).astype(BF16)
        y_ref[...] = jnp.dot(hid, wd_ref[...], preferred_element_type=F32).astype(BF16)


def _moe_scatter_kernel(ntile_ref, lfirst_ref, gfirst_ref,
                        col_ref, x1_ref, mod_ref, y_hbm,
                        o_ref,
                        y_ref, sem):
    blk = pl.program_id(0)
    nb = x1_ref.shape[0]
    tile, chunk = MOE_TILE, MOE_CHUNK
    slot = blk & 1

    def copies(b, s, wait):
        _moe_tile_copies(ntile_ref, lfirst_ref, gfirst_ref, b, y_ref.at[s], y_hbm, sem.at[s], False, wait)

    @pl.when(blk == 0)
    def _():
        copies(0, 0, False)

    @pl.when(blk + 1 < pl.num_programs(0))
    def _():
        copies(blk + 1, 1 - slot, False)

    total = lfirst_ref[blk, N_EXPERTS - 1] + ntile_ref[blk, N_EXPERTS - 1]
    n_chunks = (total * tile + (chunk - 1)) // chunk
    max_chunks = y_ref.shape[1] // chunk
    usual = n_chunks <= MOE_USUAL_CHUNKS
    n_static = jnp.where(usual, MOE_USUAL_CHUNKS, max_chunks)

    def clear(t, carry):
        y_ref[slot, pl.ds(pl.multiple_of(t * tile, tile), tile), :] = jnp.zeros((tile, y_ref.shape[2]), BF16)
        return carry

    lax.fori_loop(total, n_static * (chunk // tile), clear, 0)

    p1 = col_ref[:, 0:1].astype(jnp.int32)
    p2 = col_ref[:, 1:2].astype(jnp.int32)
    cw1 = col_ref[:, 2:3]
    cw2 = col_ref[:, 3:4]
    gate = mod_ref[5:6, :]
    lane_c = lax.broadcasted_iota(jnp.int32, (nb, chunk), 1)
    copies(blk, slot, True)

    def scatter(n_unrolled):
        acc = None
        for c in range(n_unrolled):
            p = lane_c + c * chunk
            w = (jnp.where(p == p1, cw1, 0.0) + jnp.where(p == p2, cw2, 0.0)).astype(BF16)
            part = jnp.dot(w, y_ref[slot, c * chunk:(c + 1) * chunk, :], preferred_element_type=F32)
            acc = part if acc is None else acc + part
        o_ref[...] = x1_ref[...] + gate * acc

    lax.cond(usual, lambda: scatter(MOE_USUAL_CHUNKS), lambda: scatter(max_chunks))


def _moe_call(h2, comb, cnt_tiles, x1, mod, wgu, wd):
    bsz, seq, d = x1.shape
    nb, tile, ftm = MOE_TM, MOE_TILE, MOE_FFN_TM
    n_tok = bsz * seq
    n_blk = n_tok // nb
    region = ftm // tile
    rows_local = -(-(2 * nb + N_EXPERTS * tile) // MOE_CHUNK) * MOE_CHUNK
    tiles_global = (2 * n_tok) // tile + n_blk * N_EXPERTS + N_EXPERTS * (region - 1)
    n_ffn_max = -(-tiles_global // region)
    rows_global = n_ffn_max * ftm

    cnt = cnt_tiles[:, :, 0, :N_EXPERTS].reshape(n_blk, nb // POST_TM, N_EXPERTS).sum(axis=1).astype(jnp.int32)
    ntile = (cnt + (tile - 1)) // tile
    lfirst = jnp.cumsum(ntile, axis=1) - ntile
    tot = ntile.sum(axis=0)
    ptot = (tot + (region - 1)) // region * region
    ebase = jnp.cumsum(ptot) - ptot
    gfirst = ebase[None, :] + jnp.cumsum(ntile, axis=0) - ntile
    pad = ptot - tot
    n_ffn = (ptot.sum() // region).reshape(1)
    ends = jnp.cumsum(ptot) // region
    texp = jnp.minimum((jnp.arange(n_ffn_max, dtype=jnp.int32)[:, None] >= ends[None, :]).sum(axis=1),
                       N_EXPERTS - 1).astype(jnp.int32)

    h2f = h2.reshape(n_tok, d)
    combf = comb.reshape(n_tok, LANES)
    col, xg = pl.pallas_call(
        _moe_gather_kernel,
        out_shape=(jax.ShapeDtypeStruct((n_tok, LANES), F32),
                   jax.ShapeDtypeStruct((rows_global, d), BF16)),
        grid_spec=pltpu.PrefetchScalarGridSpec(
            num_scalar_prefetch=4,
            grid=(n_blk,),
            in_specs=[pl.BlockSpec((nb, d), lambda j, *_: (j, 0)),
                      pl.BlockSpec((nb, LANES), lambda j, *_: (j, 0))],
            out_specs=(pl.BlockSpec((nb, LANES), lambda j, *_: (j, 0)),
                       pl.BlockSpec(memory_space=pl.ANY)),
            scratch_shapes=[
                pltpu.VMEM((rows_local, d), BF16),
                pltpu.VMEM((LANES, nb), F32),
                pltpu.VMEM((tile, d), BF16),
                pltpu.SemaphoreType.DMA,
            ]),
        compiler_params=pltpu.CompilerParams(dimension_semantics=("arbitrary",),
                                             vmem_limit_bytes=VMEM_LIMIT_BYTES),
        name="moe_gather",
    )(ntile, lfirst, gfirst, pad, h2f, combf)

    last = lambda t, te, nt: jnp.minimum(t, nt[0] - 1)
    y = pl.pallas_call(
        _moe_ffn_kernel,
        out_shape=jax.ShapeDtypeStruct((rows_global, d), BF16),
        grid_spec=pltpu.PrefetchScalarGridSpec(
            num_scalar_prefetch=2,
            grid=(n_ffn_max,),
            in_specs=[pl.BlockSpec((ftm, d), lambda t, te, nt: (last(t, te, nt), 0)),
                      pl.BlockSpec((None, d, 2 * EXPERT_FF), lambda t, te, nt: (te[last(t, te, nt)], 0, 0)),
                      pl.BlockSpec((None, EXPERT_FF, d), lambda t, te, nt: (te[last(t, te, nt)], 0, 0))],
            out_specs=pl.BlockSpec((ftm, d), lambda t, te, nt: (last(t, te, nt), 0))),
        compiler_params=pltpu.CompilerParams(dimension_semantics=("arbitrary",),
                                             vmem_limit_bytes=VMEM_LIMIT_BYTES),
        name="moe_ffn",
    )(texp, n_ffn, xg, wgu, wd)

    out = pl.pallas_call(
        _moe_scatter_kernel,
        out_shape=jax.ShapeDtypeStruct((n_tok, d), F32),
        grid_spec=pltpu.PrefetchScalarGridSpec(
            num_scalar_prefetch=3,
            grid=(n_blk,),
            in_specs=[pl.BlockSpec((nb, LANES), lambda j, *_: (j, 0)),
                      pl.BlockSpec((nb, d), lambda j, *_: (j, 0)),
                      pl.BlockSpec((None, 6, d), lambda j, *_: ((j * nb) // seq, 0, 0)),
                      pl.BlockSpec(memory_space=pl.ANY)],
            out_specs=pl.BlockSpec((nb, d), lambda j, *_: (j, 0)),
            scratch_shapes=[pltpu.VMEM((2, rows_local, d), BF16),
                            pltpu.SemaphoreType.DMA((2,))]),
        compiler_params=pltpu.CompilerParams(dimension_semantics=("arbitrary",),
                                             vmem_limit_bytes=VMEM_LIMIT_BYTES),
        name="moe_scatter",
    )(ntile, lfirst, gfirst, col, x1.reshape(n_tok, d), mod, y)
    return out.reshape(bsz, seq, d)


def _layer(x, mod, rel_bias, norm1, w_in, q_norm, k_norm, conv_w, attn_out_norm, conv_out_norm, w_out,
           norm2, w_group_router, b_group_router, w_expert_router, b_expert_router, w_gate, w_up, w_down):
    bsz, seq, d = x.shape
    aw = ATTN_WIDTH
    topk = min(TOPK_MAX, seq // 4)

    offs = np.cumsum([0, aw, aw, aw, IDX_HEADS * IDX_DIM, IDX_DIM, IDX_HEADS, CONV_WIDTH, CONV_WIDTH, CONV_WIDTH])
    col = lambda n: w_in[:, int(offs[n]):int(offs[n + 1])]
    wm = jnp.concatenate([col(0), col(1), col(3), col(6), col(7), col(8)], axis=1).astype(BF16)
    wvt = col(2).T.astype(BF16)
    wki = jnp.concatenate([col(4), col(4)], axis=1).astype(BF16)
    wwit = col(5).T.astype(BF16)
    qg = (jnp.tile(q_norm, ATTN_HEADS) * ((HEAD_DIM ** -0.5) * LOG2E))[None, :]
    kg = jnp.tile(k_norm, ATTN_HEADS)[None, :]
    grp = np.arange(aw) // CONV_GROUP_DIM
    gmat = jnp.asarray((grp[:, None] == grp[None, :]).astype(np.float32) / CONV_GROUP_DIM, dtype=BF16)

    q, k, vt, qi, ki, wit, cn = _pre_call(
        x, mod, norm1[None, :], wm, wvt, wki, wwit, qg, kg, conv_w, conv_out_norm.reshape(1, -1), gmat)

    bounds = jnp.asarray(_bucket_boundaries())
    an = _attn_call(rel_bias, bounds, q, qi, wit, k, ki, vt, attn_out_norm.reshape(1, -1), topk)

    wr = jnp.concatenate([w_expert_router, w_group_router,
                          jnp.zeros((d, LANES - N_EXPERTS - N_GROUPS), F32)], axis=1).astype(BF16)
    br = jnp.concatenate([b_expert_router, b_group_router,
                          jnp.zeros((LANES - N_EXPERTS - N_GROUPS,), F32)])[None, :]
    x1, h2, comb, cnt_tiles = _post_call(an, cn, x, mod, norm2[None, :], w_out[:aw].astype(BF16),
                                         w_out[aw:].astype(BF16), wr, br)

    wgu = jnp.concatenate([w_gate, w_up], axis=-1).astype(BF16)
    return _moe_call(h2, comb, cnt_tiles, x1, mod, wgu, w_down.astype(BF16))


def kernel(x, c, rel_bias, w_ada, b_ada, norm1, w_in, q_norm, k_norm, conv_w, attn_out_norm, conv_out_norm,
           w_out, norm2, w_group_router, b_group_router, w_expert_router, b_expert_router, w_gate, w_up,
           w_down):
    bsz, seq, d = x.shape
    assert d == D_MODEL and seq % max(PRE_TM, POST_TM, MOE_TM) == 0 and ATT_TQ == ATT_TK
    depth = w_ada.shape[0]
    for l in range(depth):
        mod = _mod_call(c, w_ada[l], b_ada[l][None, :]).reshape(bsz, 6, d)
        x = _layer(x, mod, rel_bias, norm1[l], w_in[l], q_norm[l], k_norm[l], conv_w[l], attn_out_norm[l],
                   conv_out_norm[l], w_out[l], norm2[l], w_group_router[l], b_group_router[l],
                   w_expert_router[l], b_expert_router[l], w_gate[l], w_up[l], w_down[l])
    return x
```

```python
import functools
import math

import jax
import jax.numpy as jnp
import numpy as np
from jax import lax
from jax.experimental import pallas as pl
from jax.experimental.pallas import tpu as pltpu

F32 = jnp.float32
BF16 = jnp.bfloat16

D_MODEL = 1024
HEAD_DIM = 64
ATTN_HEADS = 8
ATTN_WIDTH = ATTN_HEADS * HEAD_DIM
CONV_WIDTH = D_MODEL - ATTN_WIDTH
CONV_GROUP_DIM = 64
CONV_K = 3
IDX_HEADS = 8
IDX_DIM = 64
TOPK_MAX = 256
IDX_SCALE = (IDX_DIM ** -0.5) * (IDX_HEADS ** -0.5)
N_BUCKETS = 32
MAX_DISTANCE = 128
N_GROUPS = 4
EXPERTS_PER_GROUP = 8
N_EXPERTS = N_GROUPS * EXPERTS_PER_GROUP
EXPERT_FF = 256
EPS = 1e-6
LOG2E = 1.4426950408889634
NEG_BIG = -1e30
COUNT_ACCS = 4
BISECT_GROUP = 4
BISECT_VALUE_STEPS = 8
BISECT_MAX_STEPS = 64

LANES = 128
SUBLANES = 8
BF16_SUBLANES = 16
V_SLAB = HEAD_DIM + BF16_SUBLANES
VMEM_LIMIT_BYTES = 56 * 1024 * 1024

PRE_TM = 512
ATT_TQ = 256
ATT_TK = 256
POST_TM = 512
MOE_TM = 512
MOE_TILE = 32
MOE_CHUNK = 512
MOE_USUAL_CHUNKS = 3
MOE_FFN_TM = 1024
MOD_TN = 1536

_NT_DIMS = (((1,), (1,)), ((), ()))


def _tree_sum(parts):
    while len(parts) > 1:
        nxt = [parts[j] + parts[j + 1] for j in range(0, len(parts) - 1, 2)]
        if len(parts) % 2:
            nxt.append(parts[-1])
        parts = nxt
    return parts[0]


def _bucket_boundaries():
    max_exact = N_BUCKETS // 2
    d = np.arange(0, 4 * MAX_DISTANCE, dtype=np.int64)
    nf = np.maximum(d, 1).astype(np.float32)
    large = max_exact + (np.log(nf / np.float32(max_exact)) / np.float32(math.log(MAX_DISTANCE / max_exact))
                         * np.float32(N_BUCKETS - max_exact)).astype(np.int32)
    large = np.minimum(large, N_BUCKETS - 1)
    bucket = np.where(d < max_exact, d, large)
    assert np.all(np.diff(bucket) >= 0) and bucket[-1] == N_BUCKETS - 1
    bounds = [int(np.argmax(bucket >= j)) for j in range(1, N_BUCKETS)]
    return np.asarray([0] + bounds, dtype=np.int32)


def _mod_kernel(c_ref, w_ref, b_ref, o_ref):
    c = c_ref[...]
    act = c * jax.nn.sigmoid(c)
    o_ref[...] = jnp.dot(act, w_ref[...], preferred_element_type=F32,
                         precision=lax.Precision.HIGHEST) + b_ref[...]


def _mod_call(c, w_ada, b_ada):
    bsz, d = c.shape
    n = w_ada.shape[1]
    return pl.pallas_call(
        _mod_kernel,
        out_shape=jax.ShapeDtypeStruct((bsz, n), F32),
        grid=(n // MOD_TN,),
        in_specs=[pl.BlockSpec((bsz, d), lambda j: (0, 0)),
                  pl.BlockSpec((d, MOD_TN), lambda j: (0, j)),
                  pl.BlockSpec((1, MOD_TN), lambda j: (0, j))],
        out_specs=pl.BlockSpec((bsz, MOD_TN), lambda j: (0, j)),
        compiler_params=pltpu.CompilerParams(dimension_semantics=("arbitrary",),
                                             vmem_limit_bytes=VMEM_LIMIT_BYTES),
        name="adaln_mod",
    )(c, w_ada, b_ada)


def _group_rms(y, g_ref):
    ms = jnp.dot((y * y).astype(BF16), g_ref[...], preferred_element_type=F32)
    return y * lax.rsqrt(ms + EPS)


def _pre_kernel(x_ref, mod_ref, n1_ref, wm_ref, wvt_ref, wki_ref, wwit_ref, qg_ref, kg_ref,
                cw_ref, cg_ref, g_ref,
                q_ref, k_ref, vt_ref, qi_ref, ki_ref, wit_ref, cn_ref, carry_ref):
    j = pl.program_id(1)
    tm = x_ref.shape[0]
    aw = ATTN_WIDTH

    x = x_ref[...]
    ms = jnp.mean(x * x, axis=-1, keepdims=True)
    y = x * lax.rsqrt(ms + EPS) * n1_ref[...]
    h = y * (1.0 + mod_ref[1:2, :]) + mod_ref[0:1, :]
    hb = h.astype(BF16)

    def proj(lo):
        return jnp.dot(hb, wm_ref[:, lo:lo + aw], preferred_element_type=F32)

    q = _group_rms(proj(0), g_ref) * qg_ref[...]
    q_ref[...] = q.astype(BF16)
    k = _group_rms(proj(aw), g_ref) * kg_ref[...]
    k_ref[...] = k.astype(BF16)

    vt = lax.dot_general(wvt_ref[...], hb, _NT_DIMS, preferred_element_type=F32).astype(BF16)
    ones = jnp.ones((BF16_SUBLANES, ATT_TK), BF16)
    for cc in range(tm // ATT_TK):
        for hh in range(ATTN_HEADS):
            vt_ref[cc, hh * V_SLAB:hh * V_SLAB + HEAD_DIM, :] = (
                vt[hh * HEAD_DIM:(hh + 1) * HEAD_DIM, cc * ATT_TK:(cc + 1) * ATT_TK])
            vt_ref[cc, hh * V_SLAB + HEAD_DIM:(hh + 1) * V_SLAB, :] = ones

    qi_ref[...] = proj(2 * aw).astype(BF16)
    ki_ref[...] = jnp.dot(hb, wki_ref[...], preferred_element_type=F32).astype(BF16)
    wit_ref[...] = lax.dot_general(wwit_ref[...], hb, _NT_DIMS, preferred_element_type=F32) * IDX_SCALE

    gate_b = proj(3 * aw)
    z = proj(4 * aw) * proj(5 * aw)

    @pl.when(j == 0)
    def _():
        carry_ref[...] = jnp.zeros_like(carry_ref)

    prev = carry_ref[...]
    row = lax.broadcasted_iota(jnp.int32, z.shape, 0)
    z1 = jnp.where(row == 0, prev[SUBLANES - 1:SUBLANES, :], pltpu.roll(z, 1, 0))
    z2 = pltpu.roll(z, 2, 0)
    z2 = jnp.where(row == 0, prev[SUBLANES - 2:SUBLANES - 1, :], z2)
    z2 = jnp.where(row == 1, prev[SUBLANES - 1:SUBLANES, :], z2)
    carry_ref[...] = z[tm - SUBLANES:, :]
    conv = cw_ref[2:3, :] * z + cw_ref[1:2, :] * z1 + cw_ref[0:1, :] * z2
    yc = gate_b * conv
    cn_ref[...] = (_group_rms(yc, g_ref) * cg_ref[...]).astype(BF16)


def _pre_call(x, mod, n1, wm, wvt, wki, wwit, qg, kg, cw, cg, gmat):
    bsz, seq, d = x.shape
    tm = PRE_TM
    nck = tm // ATT_TK
    aw = ATTN_WIDTH
    const = lambda b, j: (0, 0)
    tok = lambda b, j: (b, j, 0)
    out_shape = (
        jax.ShapeDtypeStruct((bsz, seq, aw), BF16),
        jax.ShapeDtypeStruct((bsz, seq, aw), BF16),
        jax.ShapeDtypeStruct((bsz, seq // ATT_TK, ATTN_HEADS * V_SLAB, ATT_TK), BF16),
        jax.ShapeDtypeStruct((bsz, seq, aw), BF16),
        jax.ShapeDtypeStruct((bsz, seq, LANES), BF16),
        jax.ShapeDtypeStruct((bsz, IDX_HEADS, seq), F32),
        jax.ShapeDtypeStruct((bsz, seq, CONV_WIDTH), BF16),
    )
    out_specs = (
        pl.BlockSpec((None, tm, aw), tok),
        pl.BlockSpec((None, tm, aw), tok),
        pl.BlockSpec((None, nck, ATTN_HEADS * V_SLAB, ATT_TK), lambda b, j: (b, j, 0, 0)),
        pl.BlockSpec((None, tm, aw), tok),
        pl.BlockSpec((None, tm, LANES), tok),
        pl.BlockSpec((None, IDX_HEADS, tm), lambda b, j: (b, 0, j)),
        pl.BlockSpec((None, tm, CONV_WIDTH), tok),
    )
    in_specs = [
        pl.BlockSpec((None, tm, d), tok),
        pl.BlockSpec((None, 6, d), lambda b, j: (b, 0, 0)),
        pl.BlockSpec(n1.shape, const),
        pl.BlockSpec(wm.shape, const),
        pl.BlockSpec(wvt.shape, const),
        pl.BlockSpec(wki.shape, const),
        pl.BlockSpec(wwit.shape, const),
        pl.BlockSpec(qg.shape, const),
        pl.BlockSpec(kg.shape, const),
        pl.BlockSpec(cw.shape, const),
        pl.BlockSpec(cg.shape, const),
        pl.BlockSpec(gmat.shape, const),
    ]
    return pl.pallas_call(
        _pre_kernel,
        out_shape=out_shape,
        grid=(bsz, seq // tm),
        in_specs=in_specs,
        out_specs=out_specs,
        scratch_shapes=[pltpu.VMEM((SUBLANES, CONV_WIDTH), F32)],
        compiler_params=pltpu.CompilerParams(dimension_semantics=("arbitrary", "arbitrary"),
                                             vmem_limit_bytes=VMEM_LIMIT_BYTES),
        name="pre_proj",
    )(x, mod, n1, wm, wvt, wki, wwit, qg, kg, cw, cg, gmat)


def _attn_kernel(rb_ref, bnd_ref, q_ref, qi_ref, wit_ref, k_ref, ki_ref, vt_ref, og_ref,
                 o_ref,
                 s_ref, bias_ref, qpad_ref, qipad_ref, lg_ref, acc_ref, out_ref, *, topk):
    b = pl.program_id(0)
    i = pl.program_id(1)
    tq, tk = ATT_TQ, ATT_TK
    nh, hd = ATTN_HEADS, HEAD_DIM

    t_loc = lax.broadcasted_iota(jnp.int32, (tk, tq), 1)
    s_loc = lax.broadcasted_iota(jnp.int32, (tk, tq), 0)

    @pl.when((b == 0) & (i == 0))
    def _():
        for idx in range(2):
            dist = t_loc - s_loc + idx * tq
            for h in range(nh):
                bias_ref[idx, h] = jnp.full((tk, tq), (rb_ref[0, h] - rb_ref[N_BUCKETS - 1, h]) * LOG2E, F32)

            def fill(jb, carry):
                reached = dist >= bnd_ref[jb]
                for h in range(nh):
                    val = (rb_ref[jb, h] - rb_ref[N_BUCKETS - 1, h]) * LOG2E
                    bias_ref[idx, h] = jnp.where(reached, val, bias_ref[idx, h])
                return carry

            lax.fori_loop(1, N_BUCKETS, fill, 0)

    lane = lax.broadcasted_iota(jnp.int32, (tq, LANES), 1)
    for h in range(nh):
        pair = slice((h // 2) * LANES, (h // 2 + 1) * LANES)
        keep = (lane // hd) == (h % 2)
        qpad_ref[h] = jnp.where(keep, q_ref[:, pair], jnp.zeros((), BF16))
        qipad_ref[h] = jnp.where(keep, qi_ref[:, pair], jnp.zeros((), BF16))

    def idx_dots(c, slot):
        kic = ki_ref[pl.ds(pl.multiple_of(c * tk, tk), tk), :]
        for h in range(nh):
            lg_ref[slot, h] = lax.dot_general(kic, qipad_ref[h], _NT_DIMS, preferred_element_type=F32)

    def idx_reduce(c, slot, carry, diagonal):
        rmin, rmax = carry
        sc = _tree_sum([wit_ref[h:h + 1, :] * jnp.maximum(lg_ref[slot, h], 0.0) for h in range(nh)])
        if diagonal:
            causal = s_loc <= t_loc
            s_ref[c] = jnp.where(causal, sc, -jnp.inf)
            lo_c, hi_c = jnp.where(causal, sc, jnp.inf), jnp.where(causal, sc, -jnp.inf)
        else:
            s_ref[c] = sc
            lo_c, hi_c = sc, sc
        return (jnp.minimum(rmin, jnp.min(lo_c, axis=0, keepdims=True)),
                jnp.maximum(rmax, jnp.max(hi_c, axis=0, keepdims=True)))

    def idx_pair(jj, carry):
        idx_dots(2 * jj + 1, 1)
        carry = idx_reduce(2 * jj, 0, carry, False)
        idx_dots(2 * jj + 2, 0)
        return idx_reduce(2 * jj + 1, 1, carry, False)

    def idx_tail_odd(carry):
        idx_dots(i, 1)
        return idx_reduce(i, 1, idx_reduce(i - 1, 0, carry, False), True)

    idx_dots(0, 0)
    carry = (jnp.full((1, tq), jnp.inf, F32), jnp.full((1, tq), -jnp.inf, F32))
    carry = lax.fori_loop(0, i // 2, idx_pair, carry)
    rmin, rmax = lax.cond((i & 1) == 1, idx_tail_odd, lambda cr: idx_reduce(i, 0, cr, True), carry)

    def count_ge(thr):
        def body(c, accs):
            hit = s_ref[c] >= thr
            accs = list(accs)
            for r in range(tk // SUBLANES):
                a = accs[r % COUNT_ACCS]
                accs[r % COUNT_ACCS] = jnp.where(hit[r * SUBLANES:(r + 1) * SUBLANES], a + 1.0, a)
            return tuple(accs)
        accs = lax.fori_loop(0, i + 1, body,
                             tuple(jnp.zeros((SUBLANES, tq), F32) for _ in range(COUNT_ACCS)))
        return jnp.sum(_tree_sum(list(accs)), axis=0, keepdims=True)

    def order_key(v):
        bits = pltpu.bitcast(v, jnp.int32)
        return jnp.where(bits < 0, bits ^ jnp.int32(0x7FFFFFFF), bits)

    def from_order_key(key):
        return pltpu.bitcast(jnp.where(key < 0, key ^ jnp.int32(0x7FFFFFFF), key), F32)

    t_glob = (i * tq + lax.broadcasted_iota(jnp.int32, (1, tq), 1)).astype(F32)
    n_causal = t_glob + 1.0
    kf = jnp.minimum(float(topk), n_causal)
    all_sel = n_causal <= kf
    active0 = jnp.where(all_sel, 0.0, 1.0)
    thr0 = rmin
    tie0 = jnp.zeros((1, tq), F32)
    hi0 = from_order_key(order_key(rmax + 0.0) + 1)
    fhi0 = jnp.zeros((1, tq), F32)
    hif0 = jnp.full((1, tq), jnp.inf, F32)
    need0 = kf

    def b_cond(st):
        return (jnp.max(st[0]) > 0.0) & (st[8] <= BISECT_MAX_STEPS)

    def b_body(st):
        active, lo, hi, fhi, thr, tie, hif, need, step = st
        lo_key = order_key(lo)
        hi_key = order_key(hi)
        mid_key = (lo_key >> 1) + (hi_key >> 1) + (lo_key & hi_key & 1)
        mid_val = lo + (hi - lo) * 0.5
        use_val = (step < BISECT_VALUE_STEPS) & (mid_val > lo) & (mid_val < hi)
        mid = jnp.where(use_val, mid_val, from_order_key(mid_key))
        collapsed = (mid_key == lo_key) | (step >= BISECT_MAX_STEPS)
        cm = count_ge(mid)
        act = active > 0.0
        live = act & jnp.logical_not(collapsed)
        found = live & (cm == kf)
        go_up = live & (cm > kf)
        go_dn = live & (cm < kf)
        ends_tie = act & collapsed
        thr = jnp.where(found, mid, jnp.where(ends_tie, lo, thr))
        tie = jnp.where(ends_tie, 1.0, tie)
        hif = jnp.where(ends_tie, hi, hif)
        need = jnp.where(ends_tie, kf - fhi, need)
        lo = jnp.where(go_up, mid, lo)
        fhi = jnp.where(go_dn, cm, fhi)
        hi = jnp.where(go_dn, mid, hi)
        active = jnp.where(found | ends_tie, 0.0, active)
        return active, lo, hi, fhi, thr, tie, hif, need, step + 1

    def b_group(st):
        for _ in range(BISECT_GROUP):
            st = b_body(st)
        return st

    _, _, _, _, thr, tie, hif, need, _ = lax.while_loop(
        b_cond, b_group, (active0, rmin, hi0, fhi0, thr0, tie0, hif0, need0, jnp.int32(0)))

    @pl.when(jnp.max(tie) > 0.0)
    def _():
        tri = jnp.where(lax.broadcasted_iota(jnp.int32, (tk, tk), 1)
                        <= lax.broadcasted_iota(jnp.int32, (tk, tk), 0), 1.0, 0.0).astype(BF16)

        def body(c, seen):
            sc_c = s_ref[c]
            tied = (sc_c >= thr) & (sc_c < hif) & (tie > 0.0)
            rank = jnp.dot(tri, jnp.where(tied, 1.0, 0.0).astype(BF16), preferred_element_type=F32) + seen
            s_ref[c] = jnp.where(tied & (rank > need), -jnp.inf, sc_c)
            return rank[tk - 1:tk, :]

        lax.fori_loop(0, i + 1, body, jnp.zeros((1, tq), F32))

    acc_ref[...] = jnp.zeros(acc_ref.shape, F32)

    def store_logits(c, slot, bias_idx):
        masked = jnp.where(s_ref[c] >= thr, 0.0, NEG_BIG)
        row0 = pl.multiple_of(c * tk, tk)
        for h in range(nh):
            kc = k_ref[pl.ds(row0, tk), (h // 2) * LANES:(h // 2 + 1) * LANES]
            lt = lax.dot_general(kc, qpad_ref[h], _NT_DIMS, preferred_element_type=F32) + masked
            if bias_idx is not None:
                lt = lt + bias_ref[bias_idx, h]
            lg_ref[slot, h] = lt

    def softmax_pv(c, slot, m_all):
        m_out = []
        for h in range(nh):
            m_old = m_all[h]
            m_new = jnp.maximum(m_old, jnp.max(lg_ref[slot, h], axis=0, keepdims=True))
            p = jnp.exp2(lg_ref[slot, h] - m_new).astype(BF16)
            alpha = jnp.exp2(m_old - m_new)
            pv = jnp.dot(vt_ref[c, h * V_SLAB:(h + 1) * V_SLAB, :], p, preferred_element_type=F32)
            acc_ref[h] = alpha * acc_ref[h] + pv
            m_out.append(m_new)
        return tuple(m_out)

    def near_step(m_all):
        store_logits(i - 1, 1, 1)
        return softmax_pv(i, 0, m_all)

    def far_step(j, parity, m_all):
        c = i - 2 - j
        store_logits(c, parity, None)
        return softmax_pv(c + 1, 1 - parity, m_all)

    def far_pair(jj, m_all):
        return far_step(2 * jj + 1, 1, far_step(2 * jj, 0, m_all))

    n_far = jnp.maximum(i - 1, 0)
    m_all = tuple(jnp.full((1, tq), NEG_BIG, F32) for _ in range(nh))
    store_logits(i, 0, 0)
    m_all = lax.cond(i >= 1, near_step, lambda m: m, m_all)
    m_all = lax.fori_loop(0, n_far // 2, far_pair, m_all)
    m_all = lax.cond((n_far & 1) == 1, lambda m: far_step(n_far - 1, 0, m), lambda m: m, m_all)
    lax.cond((i & 1) == 0, lambda m: softmax_pv(0, 0, m), lambda m: softmax_pv(0, 1, m), m_all)

    for h in range(nh):
        o = acc_ref[h, :hd, :] / acc_ref[h, hd:hd + 1, :]
        ms = jnp.mean(o * o, axis=0, keepdims=True)
        out_ref[h * hd:(h + 1) * hd, :] = o * lax.rsqrt(ms + EPS)
    o_ref[...] = (out_ref[...].T * og_ref[...]).astype(BF16)


def _attn_call(rel_bias, bounds, q, qi, wit, k, ki, vt, og, topk):
    bsz, seq, aw = q.shape
    tq, tk = ATT_TQ, ATT_TK
    nck = seq // tk
    blk_q = lambda b, i: (b, i, 0)
    whole = lambda b, i: (b, 0, 0)
    smem = pl.BlockSpec(memory_space=pltpu.SMEM)
    return pl.pallas_call(
        functools.partial(_attn_kernel, topk=topk),
        out_shape=jax.ShapeDtypeStruct((bsz, seq, aw), BF16),
        grid=(bsz, seq // tq),
        in_specs=[
            smem, smem,
            pl.BlockSpec((None, tq, aw), blk_q),
            pl.BlockSpec((None, tq, aw), blk_q),
            pl.BlockSpec((None, IDX_HEADS, tq), lambda b, i: (b, 0, i)),
            pl.BlockSpec((None, seq, aw), whole),
            pl.BlockSpec((None, seq, LANES), whole),
            pl.BlockSpec((None, nck, ATTN_HEADS * V_SLAB, tk), lambda b, i: (b, 0, 0, 0)),
            pl.BlockSpec(og.shape, lambda b, i: (0, 0)),
        ],
        out_specs=pl.BlockSpec((None, tq, aw), blk_q),
        scratch_shapes=[
            pltpu.VMEM((nck, tk, tq), F32),
            pltpu.VMEM((2, ATTN_HEADS, tk, tq), F32),
            pltpu.VMEM((ATTN_HEADS, tq, LANES), BF16),
            pltpu.VMEM((IDX_HEADS, tq, LANES), BF16),
            pltpu.VMEM((2, ATTN_HEADS, tk, tq), F32),
            pltpu.VMEM((ATTN_HEADS, V_SLAB, tq), F32),
            pltpu.VMEM((aw, tq), F32),
        ],
        compiler_params=pltpu.CompilerParams(dimension_semantics=("arbitrary", "arbitrary"),
                                             vmem_limit_bytes=VMEM_LIMIT_BYTES),
        name="dsa_attention",
    )(rel_bias, bounds, q, qi, wit, k, ki, vt, og)


def _post_kernel(an_ref, cn_ref, x_ref, mod_ref, n2_ref, woa_ref, woc_ref, wr_ref, br_ref,
                 x1_ref, h2_ref, comb_ref, cnt_ref):
    mix = (jnp.dot(an_ref[...], woa_ref[...], preferred_element_type=F32)
           + jnp.dot(cn_ref[...], woc_ref[...], preferred_element_type=F32))
    x1 = x_ref[...] + mod_ref[2:3, :] * mix
    x1_ref[...] = x1
    ms = jnp.mean(x1 * x1, axis=-1, keepdims=True)
    h2 = x1 * lax.rsqrt(ms + EPS) * n2_ref[...] * (1.0 + mod_ref[4:5, :]) + mod_ref[3:4, :]
    h2b = h2.astype(BF16)
    h2_ref[...] = h2b

    logits = jnp.dot(h2b, wr_ref[...], preferred_element_type=F32) + br_ref[...]
    lane = lax.broadcasted_iota(jnp.int32, logits.shape, 1)
    lane_f = lane.astype(F32)
    far = float(LANES)
    is_g = (lane >= N_EXPERTS) & (lane < N_EXPERTS + N_GROUPS)
    gl = jnp.where(is_g, logits, -jnp.inf)
    gmax = jnp.max(gl, axis=-1, keepdims=True)
    g_sel = jnp.min(jnp.where(is_g & (gl == gmax), lane_f, far), axis=-1, keepdims=True) - float(N_EXPERTS)
    p_g = 1.0 / jnp.sum(jnp.exp(gl - gmax), axis=-1, keepdims=True)

    in_grp = (lane < N_EXPERTS) & ((lane // EXPERTS_PER_GROUP).astype(F32) == g_sel)
    e1 = jnp.where(in_grp, logits, -jnp.inf)
    l1 = jnp.max(e1, axis=-1, keepdims=True)
    i1 = jnp.min(jnp.where(in_grp & (e1 == l1), lane_f, far), axis=-1, keepdims=True)
    rest = in_grp & (lane_f != i1)
    e2 = jnp.where(rest, logits, -jnp.inf)
    l2 = jnp.max(e2, axis=-1, keepdims=True)
    i2 = jnp.min(jnp.where(rest & (e2 == l2), lane_f, far), axis=-1, keepdims=True)
    r = jnp.exp(l2 - l1)
    w1 = 1.0 / (1.0 + r)
    w2 = r / (1.0 + r)
    comb = jnp.where(lane_f == i1, p_g * w1, 0.0) + jnp.where(lane_f == i2, p_g * w2, 0.0)
    comb_ref[...] = comb
    cnt = jnp.sum(jnp.where(comb != 0.0, 1.0, 0.0), axis=0, keepdims=True)
    cnt_ref[...] = jnp.broadcast_to(cnt, cnt_ref.shape)


def _post_call(an, cn, x, mod, n2, woa, woc, wr, br):
    bsz, seq, d = x.shape
    tm = POST_TM
    tok = lambda b, j: (b, j, 0)
    const = lambda b, j: (0, 0)
    return pl.pallas_call(
        _post_kernel,
        out_shape=(jax.ShapeDtypeStruct((bsz, seq, d), F32),
                   jax.ShapeDtypeStruct((bsz, seq, d), BF16),
                   jax.ShapeDtypeStruct((bsz, seq, LANES), F32),
                   jax.ShapeDtypeStruct((bsz, seq // tm, SUBLANES, LANES), F32)),
        grid=(bsz, seq // tm),
        in_specs=[
            pl.BlockSpec((None, tm, ATTN_WIDTH), tok),
            pl.BlockSpec((None, tm, CONV_WIDTH), tok),
            pl.BlockSpec((None, tm, d), tok),
            pl.BlockSpec((None, 6, d), lambda b, j: (b, 0, 0)),
            pl.BlockSpec(n2.shape, const),
            pl.BlockSpec(woa.shape, const),
            pl.BlockSpec(woc.shape, const),
            pl.BlockSpec(wr.shape, const),
            pl.BlockSpec(br.shape, const),
        ],
        out_specs=(pl.BlockSpec((None, tm, d), tok),
                   pl.BlockSpec((None, tm, d), tok),
                   pl.BlockSpec((None, tm, LANES), tok),
                   pl.BlockSpec((None, None, SUBLANES, LANES), lambda b, j: (b, j, 0, 0))),
        compiler_params=pltpu.CompilerParams(dimension_semantics=("arbitrary", "arbitrary"),
                                             vmem_limit_bytes=VMEM_LIMIT_BYTES),
        name="post_router",
    )(an, cn, x, mod, n2, woa, woc, wr, br)


def _strict_tri(n, lower):
    r = lax.broadcasted_iota(jnp.int32, (n, n), 0)
    c = lax.broadcasted_iota(jnp.int32, (n, n), 1)
    return jnp.where((c < r) if lower else (r < c), 1.0, 0.0).astype(BF16)


def _moe_tile_copies(nloc_ref, gtile_ref, blk, local_ref, global_ref, sem, to_global, wait):
    tile = MOE_TILE

    def per_tile(lt, c):
        loc = local_ref.at[pl.ds(pl.multiple_of(lt * tile, tile), tile), :]
        glo = global_ref.at[pl.ds(pl.multiple_of(gtile_ref[blk, lt] * tile, tile), tile), :]
        cp = pltpu.make_async_copy(loc, glo, sem) if to_global else pltpu.make_async_copy(glo, loc, sem)
        if wait:
            cp.wait()
        else:
            cp.start()
        return c

    lax.fori_loop(0, nloc_ref[blk], per_tile, 0)


def _moe_gather_kernel(nloc_ref, gtile_ref, padstart_ref, pad_ref,
                       h2_ref, comb_ref,
                       col_ref, xg_hbm,
                       xg_ref, row_ref, zero_ref, sem):
    blk = pl.program_id(0)
    nb = h2_ref.shape[0]
    tile, chunk = MOE_TILE, MOE_CHUNK
    lane = lax.broadcasted_iota(jnp.int32, (nb, LANES), 1)

    comb = comb_ref[...]
    assigned = comb != 0.0
    a_f = jnp.where(assigned, 1.0, 0.0)
    rank = jnp.dot(_strict_tri(nb, True), a_f.astype(BF16), preferred_element_type=F32)
    cnt = rank[nb - 1:nb, :] + a_f[nb - 1:nb, :]
    ntile = jnp.floor((cnt + float(tile - 1)) * (1.0 / tile))
    first = jnp.dot(jnp.broadcast_to(ntile, (SUBLANES, LANES)).astype(BF16), _strict_tri(LANES, False),
                    preferred_element_type=F32)[0:1, :]
    pos = first * float(tile) + rank
    pos1 = jnp.min(jnp.where(assigned, pos, 1e9), axis=1, keepdims=True)
    pos2 = jnp.max(jnp.where(assigned, pos, -1.0), axis=1, keepdims=True)
    pos2 = jnp.where(pos2 == pos1, -1.0, pos2)
    cw1 = jnp.sum(jnp.where(assigned & (pos == pos1), comb, 0.0), axis=1, keepdims=True)
    cw2 = jnp.sum(jnp.where(assigned & (pos == pos2), comb, 0.0), axis=1, keepdims=True)
    info = jnp.where(lane == 0, pos1, jnp.where(lane == 1, pos2, jnp.where(lane == 2, cw1,
                     jnp.where(lane == 3, cw2, 0.0))))
    col_ref[...] = info
    row_ref[...] = info.T

    n_chunks = (nloc_ref[blk] * tile + (chunk - 1)) // chunk
    p1 = row_ref[0:1, :].astype(jnp.int32)
    p2 = row_ref[1:2, :].astype(jnp.int32)
    sub = lax.broadcasted_iota(jnp.int32, (chunk, nb), 0)

    def gather(c, carry):
        p = sub + c * chunk
        sel = jnp.where((p == p1) | (p == p2), 1.0, 0.0).astype(BF16)
        r0 = pl.multiple_of(c * chunk, chunk)
        xg_ref[pl.ds(r0, chunk), :] = jnp.dot(sel, h2_ref[...], preferred_element_type=F32).astype(BF16)
        return carry

    @pl.when(blk > 0)
    def _():
        _moe_tile_copies(nloc_ref, gtile_ref, blk - 1, xg_ref, xg_hbm, sem, True, True)

    lax.fori_loop(0, n_chunks, gather, 0)

    _moe_tile_copies(nloc_ref, gtile_ref, blk, xg_ref, xg_hbm, sem, True, False)

    is_last = blk == pl.num_programs(0) - 1

    def pad_copies(wait):
        def per_expert(x, carry):
            g0 = padstart_ref[x]

            def per_tile(j, c):
                dst = xg_hbm.at[pl.ds(pl.multiple_of((g0 + j) * tile, tile), tile), :]
                cp = pltpu.make_async_copy(zero_ref, dst, sem)
                if wait:
                    cp.wait()
                else:
                    cp.start()
                return c

            lax.fori_loop(0, pad_ref[x], per_tile, 0)
            return carry

        lax.fori_loop(0, N_EXPERTS, per_expert, 0)

    @pl.when(is_last)
    def _():
        zero_ref[...] = jnp.zeros(zero_ref.shape, BF16)
        pad_copies(False)
        _moe_tile_copies(nloc_ref, gtile_ref, blk, xg_ref, xg_hbm, sem, True, True)
        pad_copies(True)


def _moe_ffn_kernel(texp_ref, nt_ref, x_ref, wgu_ref, wd_ref, y_ref):
    @pl.when(pl.program_id(0) < nt_ref[0])
    def _():
        ab = jnp.dot(x_ref[...], wgu_ref[...], preferred_element_type=F32)
        a = ab[:, :EXPERT_FF]
        hid = ((a * jax.nn.sigmoid(a)) * ab[:, EXPERT_FF:]).astype(BF16)
        y_ref[...] = jnp.dot(hid, wd_ref[...], preferred_element_type=F32).astype(BF16)


def _moe_scatter_kernel(nloc_ref, gtile_ref,
                        col_ref, x1_ref, mod_ref, y_hbm,
                        o_ref,
                        y_ref, sem):
    blk = pl.program_id(0)
    nb = x1_ref.shape[0]
    tile, chunk = MOE_TILE, MOE_CHUNK
    slot = blk & 1

    def copies(b, s, wait):
        _moe_tile_copies(nloc_ref, gtile_ref, b, y_ref.at[s], y_hbm, sem.at[s], False, wait)

    @pl.when(blk == 0)
    def _():
        copies(0, 0, False)

    @pl.when(blk + 1 < pl.num_programs(0))
    def _():
        copies(blk + 1, 1 - slot, False)

    total = nloc_ref[blk]
    n_chunks = (total * tile + (chunk - 1)) // chunk
    max_chunks = y_ref.shape[1] // chunk
    usual = n_chunks <= MOE_USUAL_CHUNKS
    n_static = jnp.where(usual, MOE_USUAL_CHUNKS, max_chunks)

    def clear(t, carry):
        y_ref[slot, pl.ds(pl.multiple_of(t * tile, tile), tile), :] = jnp.zeros((tile, y_ref.shape[2]), BF16)
        return carry

    lax.fori_loop(total, n_static * (chunk // tile), clear, 0)

    p1 = col_ref[:, 0:1].astype(jnp.int32)
    p2 = col_ref[:, 1:2].astype(jnp.int32)
    cw1 = col_ref[:, 2:3]
    cw2 = col_ref[:, 3:4]
    gate = mod_ref[5:6, :]
    lane_c = lax.broadcasted_iota(jnp.int32, (nb, chunk), 1)
    copies(blk, slot, True)

    def scatter(n_unrolled):
        acc = None
        for c in range(n_unrolled):
            p = lane_c + c * chunk
            w = (jnp.where(p == p1, cw1, 0.0) + jnp.where(p == p2, cw2, 0.0)).astype(BF16)
            part = jnp.dot(w, y_ref[slot, c * chunk:(c + 1) * chunk, :], preferred_element_type=F32)
            acc = part if acc is None else acc + part
        o_ref[...] = x1_ref[...] + gate * acc

    lax.cond(usual, lambda: scatter(MOE_USUAL_CHUNKS), lambda: scatter(max_chunks))


def _moe_call(h2, comb, cnt_tiles, x1, mod, wgu, wd):
    bsz, seq, d = x1.shape
    nb, tile, ftm = MOE_TM, MOE_TILE, MOE_FFN_TM
    n_tok = bsz * seq
    n_blk = n_tok // nb
    region = ftm // tile
    rows_local = -(-(2 * nb + N_EXPERTS * tile) // MOE_CHUNK) * MOE_CHUNK
    tiles_global = (2 * n_tok) // tile + n_blk * N_EXPERTS + N_EXPERTS * (region - 1)
    n_ffn_max = -(-tiles_global // region)
    rows_global = n_ffn_max * ftm

    cnt = cnt_tiles[:, :, 0, :N_EXPERTS].reshape(n_blk, nb // POST_TM, N_EXPERTS).sum(axis=1).astype(jnp.int32)
    ntile = (cnt + (tile - 1)) // tile
    lfirst = jnp.cumsum(ntile, axis=1) - ntile
    tot = ntile.sum(axis=0)
    ptot = (tot + (region - 1)) // region * region
    ebase = jnp.cumsum(ptot) - ptot
    gfirst = ebase[None, :] + jnp.cumsum(ntile, axis=0) - ntile
    pad = ptot - tot
    n_ffn = (ptot.sum() // region).reshape(1)
    ends = jnp.cumsum(ptot) // region
    texp = jnp.minimum((jnp.arange(n_ffn_max, dtype=jnp.int32)[:, None] >= ends[None, :]).sum(axis=1),
                       N_EXPERTS - 1).astype(jnp.int32)
    nloc = ntile.sum(axis=1).astype(jnp.int32)
    lt = jnp.arange(rows_local // tile, dtype=jnp.int32)[None, :, None]
    e_of = jnp.minimum((lt >= (lfirst + ntile)[:, None, :]).sum(axis=2), N_EXPERTS - 1)
    gtile = (jnp.take_along_axis(gfirst - lfirst, e_of, axis=1) + lt[:, :, 0]).astype(jnp.int32)
    padstart = (gfirst[-1] + ntile[-1]).astype(jnp.int32)

    h2f = h2.reshape(n_tok, d)
    combf = comb.reshape(n_tok, LANES)
    col, xg = pl.pallas_call(
        _moe_gather_kernel,
        out_shape=(jax.ShapeDtypeStruct((n_tok, LANES), F32),
                   jax.ShapeDtypeStruct((rows_global, d), BF16)),
        grid_spec=pltpu.PrefetchScalarGridSpec(
            num_scalar_prefetch=4,
            grid=(n_blk,),
            in_specs=[pl.BlockSpec((nb, d), lambda j, *_: (j, 0)),
                      pl.BlockSpec((nb, LANES), lambda j, *_: (j, 0))],
            out_specs=(pl.BlockSpec((nb, LANES), lambda j, *_: (j, 0)),
                       pl.BlockSpec(memory_space=pl.ANY)),
            scratch_shapes=[
                pltpu.VMEM((rows_local, d), BF16),
                pltpu.VMEM((LANES, nb), F32),
                pltpu.VMEM((tile, d), BF16),
                pltpu.SemaphoreType.DMA,
            ]),
        compiler_params=pltpu.CompilerParams(dimension_semantics=("arbitrary",),
                                             vmem_limit_bytes=VMEM_LIMIT_BYTES),
        name="moe_gather",
    )(nloc, gtile, padstart, pad, h2f, combf)

    last = lambda t, te, nt: jnp.minimum(t, nt[0] - 1)
    y = pl.pallas_call(
        _moe_ffn_kernel,
        out_shape=jax.ShapeDtypeStruct((rows_global, d), BF16),
        grid_spec=pltpu.PrefetchScalarGridSpec(
            num_scalar_prefetch=2,
            grid=(n_ffn_max,),
            in_specs=[pl.BlockSpec((ftm, d), lambda t, te, nt: (last(t, te, nt), 0)),
                      pl.BlockSpec((None, d, 2 * EXPERT_FF), lambda t, te, nt: (te[last(t, te, nt)], 0, 0)),
                      pl.BlockSpec((None, EXPERT_FF, d), lambda t, te, nt: (te[last(t, te, nt)], 0, 0))],
            out_specs=pl.BlockSpec((ftm, d), lambda t, te, nt: (last(t, te, nt), 0))),
        compiler_params=pltpu.CompilerParams(dimension_semantics=("arbitrary",),
                                             vmem_limit_bytes=VMEM_LIMIT_BYTES),
        name="moe_ffn",
    )(texp, n_ffn, xg, wgu, wd)

    out = pl.pallas_call(
        _moe_scatter_kernel,
        out_shape=jax.ShapeDtypeStruct((n_tok, d), F32),
        grid_spec=pltpu.PrefetchScalarGridSpec(
            num_scalar_prefetch=2,
            grid=(n_blk,),
            in_specs=[pl.BlockSpec((nb, LANES), lambda j, *_: (j, 0)),
                      pl.BlockSpec((nb, d), lambda j, *_: (j, 0)),
                      pl.BlockSpec((None, 6, d), lambda j, *_: ((j * nb) // seq, 0, 0)),
                      pl.BlockSpec(memory_space=pl.ANY)],
            out_specs=pl.BlockSpec((nb, d), lambda j, *_: (j, 0)),
            scratch_shapes=[pltpu.VMEM((2, rows_local, d), BF16),
                            pltpu.SemaphoreType.DMA((2,))]),
        compiler_params=pltpu.CompilerParams(dimension_semantics=("arbitrary",),
                                             vmem_limit_bytes=VMEM_LIMIT_BYTES),
        name="moe_scatter",
    )(nloc, gtile, col, x1.reshape(n_tok, d), mod, y)
    return out.reshape(bsz, seq, d)


def _layer(x, mod, rel_bias, norm1, w_in, q_norm, k_norm, conv_w, attn_out_norm, conv_out_norm, w_out,
           norm2, w_group_router, b_group_router, w_expert_router, b_expert_router, w_gate, w_up, w_down):
    bsz, seq, d = x.shape
    aw = ATTN_WIDTH
    topk = min(TOPK_MAX, seq // 4)

    offs = np.cumsum([0, aw, aw, aw, IDX_HEADS * IDX_DIM, IDX_DIM, IDX_HEADS, CONV_WIDTH, CONV_WIDTH, CONV_WIDTH])
    col = lambda n: w_in[:, int(offs[n]):int(offs[n + 1])]
    wm = jnp.concatenate([col(0), col(1), col(3), col(6), col(7), col(8)], axis=1).astype(BF16)
    wvt = col(2).T.astype(BF16)
    wki = jnp.concatenate([col(4), col(4)], axis=1).astype(BF16)
    wwit = col(5).T.astype(BF16)
    qg = (jnp.tile(q_norm, ATTN_HEADS) * ((HEAD_DIM ** -0.5) * LOG2E))[None, :]
    kg = jnp.tile(k_norm, ATTN_HEADS)[None, :]
    grp = np.arange(aw) // CONV_GROUP_DIM
    gmat = jnp.asarray((grp[:, None] == grp[None, :]).astype(np.float32) / CONV_GROUP_DIM, dtype=BF16)

    q, k, vt, qi, ki, wit, cn = _pre_call(
        x, mod, norm1[None, :], wm, wvt, wki, wwit, qg, kg, conv_w, conv_out_norm.reshape(1, -1), gmat)

    bounds = jnp.asarray(_bucket_boundaries())
    an = _attn_call(rel_bias, bounds, q, qi, wit, k, ki, vt, attn_out_norm.reshape(1, -1), topk)

    wr = jnp.concatenate([w_expert_router, w_group_router,
                          jnp.zeros((d, LANES - N_EXPERTS - N_GROUPS), F32)], axis=1).astype(BF16)
    br = jnp.concatenate([b_expert_router, b_group_router,
                          jnp.zeros((LANES - N_EXPERTS - N_GROUPS,), F32)])[None, :]
    x1, h2, comb, cnt_tiles = _post_call(an, cn, x, mod, norm2[None, :], w_out[:aw].astype(BF16),
                                         w_out[aw:].astype(BF16), wr, br)

    wgu = jnp.concatenate([w_gate, w_up], axis=-1).astype(BF16)
    return _moe_call(h2, comb, cnt_tiles, x1, mod, wgu, w_down.astype(BF16))


def kernel(x, c, rel_bias, w_ada, b_ada, norm1, w_in, q_norm, k_norm, conv_w, attn_out_norm, conv_out_norm,
           w_out, norm2, w_group_router, b_group_router, w_expert_router, b_expert_router, w_gate, w_up,
           w_down):
    bsz, seq, d = x.shape
    assert d == D_MODEL and seq % max(PRE_TM, POST_TM, MOE_TM) == 0 and ATT_TQ == ATT_TK
    depth = w_ada.shape[0]
    for l in range(depth):
        mod = _mod_call(c, w_ada[l], b_ada[l][None, :]).reshape(bsz, 6, d)
        x = _layer(x, mod, rel_bias, norm1[l], w_in[l], q_norm[l], k_norm[l], conv_w[l], attn_out_norm[l],
                   conv_out_norm[l], w_out[l], norm2[l], w_group_router[l], b_group_router[l],
                   w_expert_router[l], b_expert_router[l], w_gate[l], w_up[l], w_down[l])
    return x
```

```python
import functools
import math

import jax
import jax.numpy as jnp
import numpy as np
from jax import lax
from jax.experimental import pallas as pl
from jax.experimental.pallas import tpu as pltpu

F32 = jnp.float32
BF16 = jnp.bfloat16

D_MODEL = 1024
HEAD_DIM = 64
ATTN_HEADS = 8
ATTN_WIDTH = ATTN_HEADS * HEAD_DIM
CONV_WIDTH = D_MODEL - ATTN_WIDTH
CONV_GROUP_DIM = 64
CONV_K = 3
IDX_HEADS = 8
IDX_DIM = 64
TOPK_MAX = 256
IDX_SCALE = (IDX_DIM ** -0.5) * (IDX_HEADS ** -0.5)
N_BUCKETS = 32
MAX_DISTANCE = 128
N_GROUPS = 4
EXPERTS_PER_GROUP = 8
N_EXPERTS = N_GROUPS * EXPERTS_PER_GROUP
EXPERT_FF = 256
EPS = 1e-6
LOG2E = 1.4426950408889634
NEG_BIG = -1e30
COUNT_ACCS = 4
BISECT_GROUP = 4
BISECT_BF16_STEPS = 8
BISECT_VALUE_STEPS = 8
BISECT_MAX_STEPS = 64

LANES = 128
SUBLANES = 8
BF16_SUBLANES = 16
V_SLAB = HEAD_DIM + BF16_SUBLANES
VMEM_LIMIT_BYTES = 56 * 1024 * 1024

PRE_TM = 512
ATT_TQ = 256
ATT_TK = 256
POST_TM = 512
MOE_TM = 512
MOE_TILE = 32
MOE_CHUNK = 512
MOE_USUAL_CHUNKS = 3
MOE_FFN_TM = 1024
MOD_TN = 1536

_NT_DIMS = (((1,), (1,)), ((), ()))


def _tree_sum(parts):
    while len(parts) > 1:
        nxt = [parts[j] + parts[j + 1] for j in range(0, len(parts) - 1, 2)]
        if len(parts) % 2:
            nxt.append(parts[-1])
        parts = nxt
    return parts[0]


def _bucket_boundaries():
    max_exact = N_BUCKETS // 2
    d = np.arange(0, 4 * MAX_DISTANCE, dtype=np.int64)
    nf = np.maximum(d, 1).astype(np.float32)
    large = max_exact + (np.log(nf / np.float32(max_exact)) / np.float32(math.log(MAX_DISTANCE / max_exact))
                         * np.float32(N_BUCKETS - max_exact)).astype(np.int32)
    large = np.minimum(large, N_BUCKETS - 1)
    bucket = np.where(d < max_exact, d, large)
    assert np.all(np.diff(bucket) >= 0) and bucket[-1] == N_BUCKETS - 1
    bounds = [int(np.argmax(bucket >= j)) for j in range(1, N_BUCKETS)]
    return np.asarray([0] + bounds, dtype=np.int32)


def _mod_kernel(c_ref, w_ref, b_ref, o_ref):
    c = c_ref[...]
    act = c * jax.nn.sigmoid(c)
    o_ref[...] = jnp.dot(act, w_ref[...], preferred_element_type=F32,
                         precision=lax.Precision.HIGHEST) + b_ref[...]


def _mod_call(c, w_ada, b_ada):
    bsz, d = c.shape
    n = w_ada.shape[1]
    return pl.pallas_call(
        _mod_kernel,
        out_shape=jax.ShapeDtypeStruct((bsz, n), F32),
        grid=(n // MOD_TN,),
        in_specs=[pl.BlockSpec((bsz, d), lambda j: (0, 0)),
                  pl.BlockSpec((d, MOD_TN), lambda j: (0, j)),
                  pl.BlockSpec((1, MOD_TN), lambda j: (0, j))],
        out_specs=pl.BlockSpec((bsz, MOD_TN), lambda j: (0, j)),
        compiler_params=pltpu.CompilerParams(dimension_semantics=("arbitrary",),
                                             vmem_limit_bytes=VMEM_LIMIT_BYTES),
        name="adaln_mod",
    )(c, w_ada, b_ada)


def _group_rms(y, g_ref):
    ms = jnp.dot((y * y).astype(BF16), g_ref[...], preferred_element_type=F32)
    return y * lax.rsqrt(ms + EPS)


def _pre_kernel(x_ref, mod_ref, n1_ref, wm_ref, wvt_ref, wki_ref, wwit_ref, qg_ref, kg_ref,
                cw_ref, cg_ref, g_ref,
                q_ref, k_ref, vt_ref, qi_ref, ki_ref, wit_ref, cn_ref, carry_ref):
    j = pl.program_id(1)
    tm = x_ref.shape[0]
    aw = ATTN_WIDTH

    x = x_ref[...]
    ms = jnp.mean(x * x, axis=-1, keepdims=True)
    y = x * lax.rsqrt(ms + EPS) * n1_ref[...]
    h = y * (1.0 + mod_ref[1:2, :]) + mod_ref[0:1, :]
    hb = h.astype(BF16)

    def proj(lo):
        return jnp.dot(hb, wm_ref[:, lo:lo + aw], preferred_element_type=F32)

    q = _group_rms(proj(0), g_ref) * qg_ref[...]
    q_ref[...] = q.astype(BF16)
    k = _group_rms(proj(aw), g_ref) * kg_ref[...]
    k_ref[...] = k.astype(BF16)

    vt = lax.dot_general(wvt_ref[...], hb, _NT_DIMS, preferred_element_type=F32).astype(BF16)
    ones = jnp.ones((BF16_SUBLANES, ATT_TK), BF16)
    for cc in range(tm // ATT_TK):
        for hh in range(ATTN_HEADS):
            vt_ref[cc, hh * V_SLAB:hh * V_SLAB + HEAD_DIM, :] = (
                vt[hh * HEAD_DIM:(hh + 1) * HEAD_DIM, cc * ATT_TK:(cc + 1) * ATT_TK])
            vt_ref[cc, hh * V_SLAB + HEAD_DIM:(hh + 1) * V_SLAB, :] = ones

    qi_ref[...] = proj(2 * aw).astype(BF16)
    ki_ref[...] = jnp.dot(hb, wki_ref[...], preferred_element_type=F32).astype(BF16)
    wit_ref[...] = lax.dot_general(wwit_ref[...], hb, _NT_DIMS, preferred_element_type=F32) * IDX_SCALE

    gate_b = proj(3 * aw)
    z = proj(4 * aw) * proj(5 * aw)

    @pl.when(j == 0)
    def _():
        carry_ref[...] = jnp.zeros_like(carry_ref)

    prev = carry_ref[...]
    row = lax.broadcasted_iota(jnp.int32, z.shape, 0)
    z1 = jnp.where(row == 0, prev[SUBLANES - 1:SUBLANES, :], pltpu.roll(z, 1, 0))
    z2 = pltpu.roll(z, 2, 0)
    z2 = jnp.where(row == 0, prev[SUBLANES - 2:SUBLANES - 1, :], z2)
    z2 = jnp.where(row == 1, prev[SUBLANES - 1:SUBLANES, :], z2)
    carry_ref[...] = z[tm - SUBLANES:, :]
    conv = cw_ref[2:3, :] * z + cw_ref[1:2, :] * z1 + cw_ref[0:1, :] * z2
    yc = gate_b * conv
    cn_ref[...] = (_group_rms(yc, g_ref) * cg_ref[...]).astype(BF16)


def _pre_call(x, mod, n1, wm, wvt, wki, wwit, qg, kg, cw, cg, gmat):
    bsz, seq, d = x.shape
    tm = PRE_TM
    nck = tm // ATT_TK
    aw = ATTN_WIDTH
    const = lambda b, j: (0, 0)
    tok = lambda b, j: (b, j, 0)
    out_shape = (
        jax.ShapeDtypeStruct((bsz, seq, aw), BF16),
        jax.ShapeDtypeStruct((bsz, seq, aw), BF16),
        jax.ShapeDtypeStruct((bsz, seq // ATT_TK, ATTN_HEADS * V_SLAB, ATT_TK), BF16),
        jax.ShapeDtypeStruct((bsz, seq, aw), BF16),
        jax.ShapeDtypeStruct((bsz, seq, LANES), BF16),
        jax.ShapeDtypeStruct((bsz, IDX_HEADS, seq), F32),
        jax.ShapeDtypeStruct((bsz, seq, CONV_WIDTH), BF16),
    )
    out_specs = (
        pl.BlockSpec((None, tm, aw), tok),
        pl.BlockSpec((None, tm, aw), tok),
        pl.BlockSpec((None, nck, ATTN_HEADS * V_SLAB, ATT_TK), lambda b, j: (b, j, 0, 0)),
        pl.BlockSpec((None, tm, aw), tok),
        pl.BlockSpec((None, tm, LANES), tok),
        pl.BlockSpec((None, IDX_HEADS, tm), lambda b, j: (b, 0, j)),
        pl.BlockSpec((None, tm, CONV_WIDTH), tok),
    )
    in_specs = [
        pl.BlockSpec((None, tm, d), tok),
        pl.BlockSpec((None, 6, d), lambda b, j: (b, 0, 0)),
        pl.BlockSpec(n1.shape, const),
        pl.BlockSpec(wm.shape, const),
        pl.BlockSpec(wvt.shape, const),
        pl.BlockSpec(wki.shape, const),
        pl.BlockSpec(wwit.shape, const),
        pl.BlockSpec(qg.shape, const),
        pl.BlockSpec(kg.shape, const),
        pl.BlockSpec(cw.shape, const),
        pl.BlockSpec(cg.shape, const),
        pl.BlockSpec(gmat.shape, const),
    ]
    return pl.pallas_call(
        _pre_kernel,
        out_shape=out_shape,
        grid=(bsz, seq // tm),
        in_specs=in_specs,
        out_specs=out_specs,
        scratch_shapes=[pltpu.VMEM((SUBLANES, CONV_WIDTH), F32)],
        compiler_params=pltpu.CompilerParams(dimension_semantics=("arbitrary", "arbitrary"),
                                             vmem_limit_bytes=VMEM_LIMIT_BYTES),
        name="pre_proj",
    )(x, mod, n1, wm, wvt, wki, wwit, qg, kg, cw, cg, gmat)


def _attn_kernel(rb_ref, bnd_ref, q_ref, qi_ref, wit_ref, k_ref, ki_ref, vt_ref, og_ref,
                 o_ref,
                 s_ref, s16_ref, bias_ref, qpad_ref, qipad_ref, lg_ref, acc_ref, out_ref, *, topk):
    b = pl.program_id(0)
    i = pl.program_id(1)
    tq, tk = ATT_TQ, ATT_TK
    nh, hd = ATTN_HEADS, HEAD_DIM

    t_loc = lax.broadcasted_iota(jnp.int32, (tk, tq), 1)
    s_loc = lax.broadcasted_iota(jnp.int32, (tk, tq), 0)

    @pl.when((b == 0) & (i == 0))
    def _():
        for idx in range(2):
            dist = t_loc - s_loc + idx * tq
            for h in range(nh):
                bias_ref[idx, h] = jnp.full((tk, tq), (rb_ref[0, h] - rb_ref[N_BUCKETS - 1, h]) * LOG2E, F32)

            def fill(jb, carry):
                reached = dist >= bnd_ref[jb]
                for h in range(nh):
                    val = (rb_ref[jb, h] - rb_ref[N_BUCKETS - 1, h]) * LOG2E
                    bias_ref[idx, h] = jnp.where(reached, val, bias_ref[idx, h])
                return carry

            lax.fori_loop(1, N_BUCKETS, fill, 0)

    lane = lax.broadcasted_iota(jnp.int32, (tq, LANES), 1)
    for h in range(nh):
        pair = slice((h // 2) * LANES, (h // 2 + 1) * LANES)
        keep = (lane // hd) == (h % 2)
        qpad_ref[h] = jnp.where(keep, q_ref[:, pair], jnp.zeros((), BF16))
        qipad_ref[h] = jnp.where(keep, qi_ref[:, pair], jnp.zeros((), BF16))

    def idx_dots(c, slot):
        kic = ki_ref[pl.ds(pl.multiple_of(c * tk, tk), tk), :]
        for h in range(nh):
            lg_ref[slot, h] = lax.dot_general(kic, qipad_ref[h], _NT_DIMS, preferred_element_type=F32)

    def idx_reduce(c, slot, carry, diagonal):
        rmin, rmax = carry
        sc = _tree_sum([wit_ref[h:h + 1, :] * jnp.maximum(lg_ref[slot, h], 0.0) for h in range(nh)])
        if diagonal:
            causal = s_loc <= t_loc
            lo_c, hi_c = jnp.where(causal, sc, jnp.inf), jnp.where(causal, sc, -jnp.inf)
            sc = hi_c
        else:
            lo_c, hi_c = sc, sc
        s_ref[c] = sc
        s16_ref[c] = sc.astype(BF16)
        return (jnp.minimum(rmin, jnp.min(lo_c, axis=0, keepdims=True)),
                jnp.maximum(rmax, jnp.max(hi_c, axis=0, keepdims=True)))

    def idx_pair(jj, carry):
        idx_dots(2 * jj + 1, 1)
        carry = idx_reduce(2 * jj, 0, carry, False)
        idx_dots(2 * jj + 2, 0)
        return idx_reduce(2 * jj + 1, 1, carry, False)

    def idx_tail_odd(carry):
        idx_dots(i, 1)
        return idx_reduce(i, 1, idx_reduce(i - 1, 0, carry, False), True)

    idx_dots(0, 0)
    carry = (jnp.full((1, tq), jnp.inf, F32), jnp.full((1, tq), -jnp.inf, F32))
    carry = lax.fori_loop(0, i // 2, idx_pair, carry)
    rmin, rmax = lax.cond((i & 1) == 1, idx_tail_odd, lambda cr: idx_reduce(i, 0, cr, True), carry)

    def count_ge(thr):
        def body(c, accs):
            hit = s_ref[c] >= thr
            accs = list(accs)
            for r in range(tk // SUBLANES):
                a = accs[r % COUNT_ACCS]
                accs[r % COUNT_ACCS] = jnp.where(hit[r * SUBLANES:(r + 1) * SUBLANES], a + 1.0, a)
            return tuple(accs)
        accs = lax.fori_loop(0, i + 1, body,
                             tuple(jnp.zeros((SUBLANES, tq), F32) for _ in range(COUNT_ACCS)))
        return jnp.sum(_tree_sum(list(accs)), axis=0, keepdims=True)

    def count16_ge(thr16):
        def body(c, accs):
            hit = s16_ref[c] >= thr16
            accs = list(accs)
            for r in range(tk // BF16_SUBLANES):
                a = accs[r % COUNT_ACCS]
                accs[r % COUNT_ACCS] = jnp.where(hit[r * BF16_SUBLANES:(r + 1) * BF16_SUBLANES], a + 1, a)
            return tuple(accs)
        accs = lax.fori_loop(0, i + 1, body,
                             tuple(jnp.zeros((BF16_SUBLANES, tq), BF16) for _ in range(COUNT_ACCS)))
        return jnp.sum(_tree_sum(list(accs)).astype(F32), axis=0, keepdims=True)

    int_min = jnp.int32(-2 ** 31)

    def order_key(v):
        bits = pltpu.bitcast(v, jnp.int32)
        return jnp.where(bits < 0, -(bits & jnp.int32(0x7FFFFFFF)), bits)

    def from_order_key(key):
        return pltpu.bitcast(jnp.where(key < 0, (-key) | int_min, key), F32)

    t_glob = (i * tq + lax.broadcasted_iota(jnp.int32, (1, tq), 1)).astype(F32)
    n_causal = t_glob + 1.0
    kf = jnp.minimum(float(topk), n_causal)
    all_sel = n_causal <= kf

    lo16 = order_key(rmin) >> 16
    hi16 = (order_key(rmax.astype(BF16).astype(F32) + 0.0) >> 16) + 1
    for _ in range(BISECT_BF16_STEPS):
        lo_v = from_order_key(lo16 << 16)
        hi_v = from_order_key(hi16 << 16)
        mid_val16 = order_key(lo_v + (hi_v - lo_v) * 0.5) >> 16
        mid16 = jnp.where((mid_val16 > lo16) & (mid_val16 < hi16), mid_val16, (lo16 + hi16) >> 1)
        open_ = (hi16 - lo16) > 1
        cm = count16_ge(from_order_key(mid16 << 16).astype(BF16))
        lo16 = jnp.where(open_ & (cm >= kf), mid16, lo16)
        hi16 = jnp.where(open_ & (cm < kf), mid16, hi16)

    active0 = jnp.where(all_sel, 0.0, 1.0)
    thr0 = rmin
    tie0 = jnp.zeros((1, tq), F32)
    lo0 = jnp.maximum(from_order_key((lo16 - 1) << 16), rmin)
    hi0 = from_order_key(hi16 << 16)
    fhi0 = count_ge(hi0)
    hif0 = jnp.full((1, tq), jnp.inf, F32)
    need0 = kf

    def b_cond(st):
        return (jnp.max(st[0]) > 0.0) & (st[8] <= BISECT_MAX_STEPS)

    def b_body(st):
        active, lo, hi, fhi, thr, tie, hif, need, step = st
        lo_key = order_key(lo)
        hi_key = order_key(hi)
        mid_key = (lo_key >> 1) + (hi_key >> 1) + (lo_key & hi_key & 1)
        mid_val = lo + (hi - lo) * 0.5
        use_val = (step < BISECT_VALUE_STEPS) & (mid_val > lo) & (mid_val < hi)
        mid = jnp.where(use_val, mid_val, from_order_key(mid_key))
        collapsed = (mid_key == lo_key) | (step >= BISECT_MAX_STEPS)
        cm = count_ge(mid)
        act = active > 0.0
        live = act & jnp.logical_not(collapsed)
        found = live & (cm == kf)
        go_up = live & (cm > kf)
        go_dn = live & (cm < kf)
        ends_tie = act & collapsed
        thr = jnp.where(found, mid, jnp.where(ends_tie, lo, thr))
        tie = jnp.where(ends_tie, 1.0, tie)
        hif = jnp.where(ends_tie, hi, hif)
        need = jnp.where(ends_tie, kf - fhi, need)
        lo = jnp.where(go_up, mid, lo)
        fhi = jnp.where(go_dn, cm, fhi)
        hi = jnp.where(go_dn, mid, hi)
        active = jnp.where(found | ends_tie, 0.0, active)
        return active, lo, hi, fhi, thr, tie, hif, need, step + 1

    def b_group(st):
        for _ in range(BISECT_GROUP):
            st = b_body(st)
        return st

    _, _, _, _, thr, tie, hif, need, _ = lax.while_loop(
        b_cond, b_group, (active0, lo0, hi0, fhi0, thr0, tie0, hif0, need0, jnp.int32(0)))

    @pl.when(jnp.max(tie) > 0.0)
    def _():
        tri = jnp.where(lax.broadcasted_iota(jnp.int32, (tk, tk), 1)
                        <= lax.broadcasted_iota(jnp.int32, (tk, tk), 0), 1.0, 0.0).astype(BF16)

        def body(c, seen):
            sc_c = s_ref[c]
            tied = (sc_c >= thr) & (sc_c < hif) & (tie > 0.0)
            rank = jnp.dot(tri, jnp.where(tied, 1.0, 0.0).astype(BF16), preferred_element_type=F32) + seen
            s_ref[c] = jnp.where(tied & (rank > need), -jnp.inf, sc_c)
            return rank[tk - 1:tk, :]

        lax.fori_loop(0, i + 1, body, jnp.zeros((1, tq), F32))

    acc_ref[...] = jnp.zeros(acc_ref.shape, F32)

    def store_logits(c, slot, bias_idx):
        masked = jnp.where(s_ref[c] >= thr, 0.0, NEG_BIG)
        row0 = pl.multiple_of(c * tk, tk)
        for h in range(nh):
            kc = k_ref[pl.ds(row0, tk), (h // 2) * LANES:(h // 2 + 1) * LANES]
            lt = lax.dot_general(kc, qpad_ref[h], _NT_DIMS, preferred_element_type=F32) + masked
            if bias_idx is not None:
                lt = lt + bias_ref[bias_idx, h]
            lg_ref[slot, h] = lt

    def softmax_pv(c, slot, m_all):
        m_out = []
        for h in range(nh):
            m_old = m_all[h]
            m_new = jnp.maximum(m_old, jnp.max(lg_ref[slot, h], axis=0, keepdims=True))
            p = jnp.exp2(lg_ref[slot, h] - m_new).astype(BF16)
            alpha = jnp.exp2(m_old - m_new)
            pv = jnp.dot(vt_ref[c, h * V_SLAB:(h + 1) * V_SLAB, :], p, preferred_element_type=F32)
            acc_ref[h] = alpha * acc_ref[h] + pv
            m_out.append(m_new)
        return tuple(m_out)

    def near_step(m_all):
        store_logits(i - 1, 1, 1)
        return softmax_pv(i, 0, m_all)

    def far_step(j, parity, m_all):
        c = i - 2 - j
        store_logits(c, parity, None)
        return softmax_pv(c + 1, 1 - parity, m_all)

    def far_pair(jj, m_all):
        return far_step(2 * jj + 1, 1, far_step(2 * jj, 0, m_all))

    n_far = jnp.maximum(i - 1, 0)
    m_all = tuple(jnp.full((1, tq), NEG_BIG, F32) for _ in range(nh))
    store_logits(i, 0, 0)
    m_all = lax.cond(i >= 1, near_step, lambda m: m, m_all)
    m_all = lax.fori_loop(0, n_far // 2, far_pair, m_all)
    m_all = lax.cond((n_far & 1) == 1, lambda m: far_step(n_far - 1, 0, m), lambda m: m, m_all)
    lax.cond((i & 1) == 0, lambda m: softmax_pv(0, 0, m), lambda m: softmax_pv(0, 1, m), m_all)

    for h in range(nh):
        o = acc_ref[h, :hd, :] / acc_ref[h, hd:hd + 1, :]
        ms = jnp.mean(o * o, axis=0, keepdims=True)
        out_ref[h * hd:(h + 1) * hd, :] = o * lax.rsqrt(ms + EPS)
    o_ref[...] = (out_ref[...].T * og_ref[...]).astype(BF16)


def _attn_call(rel_bias, bounds, q, qi, wit, k, ki, vt, og, topk):
    bsz, seq, aw = q.shape
    tq, tk = ATT_TQ, ATT_TK
    nck = seq // tk
    blk_q = lambda b, i: (b, i, 0)
    whole = lambda b, i: (b, 0, 0)
    smem = pl.BlockSpec(memory_space=pltpu.SMEM)
    return pl.pallas_call(
        functools.partial(_attn_kernel, topk=topk),
        out_shape=jax.ShapeDtypeStruct((bsz, seq, aw), BF16),
        grid=(bsz, seq // tq),
        in_specs=[
            smem, smem,
            pl.BlockSpec((None, tq, aw), blk_q),
            pl.BlockSpec((None, tq, aw), blk_q),
            pl.BlockSpec((None, IDX_HEADS, tq), lambda b, i: (b, 0, i)),
            pl.BlockSpec((None, seq, aw), whole),
            pl.BlockSpec((None, seq, LANES), whole),
            pl.BlockSpec((None, nck, ATTN_HEADS * V_SLAB, tk), lambda b, i: (b, 0, 0, 0)),
            pl.BlockSpec(og.shape, lambda b, i: (0, 0)),
        ],
        out_specs=pl.BlockSpec((None, tq, aw), blk_q),
        scratch_shapes=[
            pltpu.VMEM((nck, tk, tq), F32),
            pltpu.VMEM((nck, tk, tq), BF16),
            pltpu.VMEM((2, ATTN_HEADS, tk, tq), F32),
            pltpu.VMEM((ATTN_HEADS, tq, LANES), BF16),
            pltpu.VMEM((IDX_HEADS, tq, LANES), BF16),
            pltpu.VMEM((2, ATTN_HEADS, tk, tq), F32),
            pltpu.VMEM((ATTN_HEADS, V_SLAB, tq), F32),
            pltpu.VMEM((aw, tq), F32),
        ],
        compiler_params=pltpu.CompilerParams(dimension_semantics=("arbitrary", "arbitrary"),
                                             vmem_limit_bytes=VMEM_LIMIT_BYTES),
        name="dsa_attention",
    )(rel_bias, bounds, q, qi, wit, k, ki, vt, og)


def _post_kernel(an_ref, cn_ref, x_ref, mod_ref, n2_ref, woa_ref, woc_ref, wr_ref, br_ref,
                 x1_ref, h2_ref, comb_ref, cnt_ref):
    mix = (jnp.dot(an_ref[...], woa_ref[...], preferred_element_type=F32)
           + jnp.dot(cn_ref[...], woc_ref[...], preferred_element_type=F32))
    x1 = x_ref[...] + mod_ref[2:3, :] * mix
    x1_ref[...] = x1
    ms = jnp.mean(x1 * x1, axis=-1, keepdims=True)
    h2 = x1 * lax.rsqrt(ms + EPS) * n2_ref[...] * (1.0 + mod_ref[4:5, :]) + mod_ref[3:4, :]
    h2b = h2.astype(BF16)
    h2_ref[...] = h2b

    logits = jnp.dot(h2b, wr_ref[...], preferred_element_type=F32) + br_ref[...]
    lane = lax.broadcasted_iota(jnp.int32, logits.shape, 1)
    lane_f = lane.astype(F32)
    far = float(LANES)
    is_g = (lane >= N_EXPERTS) & (lane < N_EXPERTS + N_GROUPS)
    gl = jnp.where(is_g, logits, -jnp.inf)
    gmax = jnp.max(gl, axis=-1, keepdims=True)
    g_sel = jnp.min(jnp.where(is_g & (gl == gmax), lane_f, far), axis=-1, keepdims=True) - float(N_EXPERTS)
    p_g = 1.0 / jnp.sum(jnp.exp(gl - gmax), axis=-1, keepdims=True)

    in_grp = (lane < N_EXPERTS) & ((lane // EXPERTS_PER_GROUP).astype(F32) == g_sel)
    e1 = jnp.where(in_grp, logits, -jnp.inf)
    l1 = jnp.max(e1, axis=-1, keepdims=True)
    i1 = jnp.min(jnp.where(in_grp & (e1 == l1), lane_f, far), axis=-1, keepdims=True)
    rest = in_grp & (lane_f != i1)
    e2 = jnp.where(rest, logits, -jnp.inf)
    l2 = jnp.max(e2, axis=-1, keepdims=True)
    i2 = jnp.min(jnp.where(rest & (e2 == l2), lane_f, far), axis=-1, keepdims=True)
    r = jnp.exp(l2 - l1)
    w1 = 1.0 / (1.0 + r)
    w2 = r / (1.0 + r)
    comb = jnp.where(lane_f == i1, p_g * w1, 0.0) + jnp.where(lane_f == i2, p_g * w2, 0.0)
    comb_ref[...] = comb
    cnt = jnp.sum(jnp.where(comb != 0.0, 1.0, 0.0), axis=0, keepdims=True)
    cnt_ref[...] = jnp.broadcast_to(cnt, cnt_ref.shape)


def _post_call(an, cn, x, mod, n2, woa, woc, wr, br):
    bsz, seq, d = x.shape
    tm = POST_TM
    tok = lambda b, j: (b, j, 0)
    const = lambda b, j: (0, 0)
    return pl.pallas_call(
        _post_kernel,
        out_shape=(jax.ShapeDtypeStruct((bsz, seq, d), F32),
                   jax.ShapeDtypeStruct((bsz, seq, d), BF16),
                   jax.ShapeDtypeStruct((bsz, seq, LANES), F32),
                   jax.ShapeDtypeStruct((bsz, seq // tm, SUBLANES, LANES), F32)),
        grid=(bsz, seq // tm),
        in_specs=[
            pl.BlockSpec((None, tm, ATTN_WIDTH), tok),
            pl.BlockSpec((None, tm, CONV_WIDTH), tok),
            pl.BlockSpec((None, tm, d), tok),
            pl.BlockSpec((None, 6, d), lambda b, j: (b, 0, 0)),
            pl.BlockSpec(n2.shape, const),
            pl.BlockSpec(woa.shape, const),
            pl.BlockSpec(woc.shape, const),
            pl.BlockSpec(wr.shape, const),
            pl.BlockSpec(br.shape, const),
        ],
        out_specs=(pl.BlockSpec((None, tm, d), tok),
                   pl.BlockSpec((None, tm, d), tok),
                   pl.BlockSpec((None, tm, LANES), tok),
                   pl.BlockSpec((None, None, SUBLANES, LANES), lambda b, j: (b, j, 0, 0))),
        compiler_params=pltpu.CompilerParams(dimension_semantics=("arbitrary", "arbitrary"),
                                             vmem_limit_bytes=VMEM_LIMIT_BYTES),
        name="post_router",
    )(an, cn, x, mod, n2, woa, woc, wr, br)


def _strict_tri(n, lower):
    r = lax.broadcasted_iota(jnp.int32, (n, n), 0)
    c = lax.broadcasted_iota(jnp.int32, (n, n), 1)
    return jnp.where((c < r) if lower else (r < c), 1.0, 0.0).astype(BF16)


def _moe_tile_copies(nloc_ref, gtile_ref, blk, local_ref, global_ref, sem, to_global, wait):
    tile = MOE_TILE

    def per_tile(lt, c):
        loc = local_ref.at[pl.ds(pl.multiple_of(lt * tile, tile), tile), :]
        glo = global_ref.at[pl.ds(pl.multiple_of(gtile_ref[blk, lt] * tile, tile), tile), :]
        cp = pltpu.make_async_copy(loc, glo, sem) if to_global else pltpu.make_async_copy(glo, loc, sem)
        if wait:
            cp.wait()
        else:
            cp.start()
        return c

    lax.fori_loop(0, nloc_ref[blk], per_tile, 0)


def _moe_gather_kernel(nloc_ref, gtile_ref, padstart_ref, pad_ref,
                       h2_ref, comb_ref,
                       col_ref, xg_hbm,
                       xg_ref, row_ref, zero_ref, sem):
    blk = pl.program_id(0)
    nb = h2_ref.shape[0]
    tile, chunk = MOE_TILE, MOE_CHUNK
    lane = lax.broadcasted_iota(jnp.int32, (nb, LANES), 1)

    comb = comb_ref[...]
    assigned = comb != 0.0
    a_f = jnp.where(assigned, 1.0, 0.0)
    rank = jnp.dot(_strict_tri(nb, True), a_f.astype(BF16), preferred_element_type=F32)
    cnt = rank[nb - 1:nb, :] + a_f[nb - 1:nb, :]
    ntile = jnp.floor((cnt + float(tile - 1)) * (1.0 / tile))
    first = jnp.dot(jnp.broadcast_to(ntile, (SUBLANES, LANES)).astype(BF16), _strict_tri(LANES, False),
                    preferred_element_type=F32)[0:1, :]
    pos = first * float(tile) + rank
    pos1 = jnp.min(jnp.where(assigned, pos, 1e9), axis=1, keepdims=True)
    pos2 = jnp.max(jnp.where(assigned, pos, -1.0), axis=1, keepdims=True)
    pos2 = jnp.where(pos2 == pos1, -1.0, pos2)
    cw1 = jnp.sum(jnp.where(assigned & (pos == pos1), comb, 0.0), axis=1, keepdims=True)
    cw2 = jnp.sum(jnp.where(assigned & (pos == pos2), comb, 0.0), axis=1, keepdims=True)
    info = jnp.where(lane == 0, pos1, jnp.where(lane == 1, pos2, jnp.where(lane == 2, cw1,
                     jnp.where(lane == 3, cw2, 0.0))))
    col_ref[...] = info
    row_ref[...] = info.T

    n_chunks = (nloc_ref[blk] * tile + (chunk - 1)) // chunk
    p1 = row_ref[0:1, :].astype(jnp.int32)
    p2 = row_ref[1:2, :].astype(jnp.int32)
    sub = lax.broadcasted_iota(jnp.int32, (chunk, nb), 0)

    def gather(c, carry):
        p = sub + c * chunk
        sel = jnp.where((p == p1) | (p == p2), 1.0, 0.0).astype(BF16)
        r0 = pl.multiple_of(c * chunk, chunk)
        xg_ref[pl.ds(r0, chunk), :] = jnp.dot(sel, h2_ref[...], preferred_element_type=F32).astype(BF16)
        return carry

    @pl.when(blk > 0)
    def _():
        _moe_tile_copies(nloc_ref, gtile_ref, blk - 1, xg_ref, xg_hbm, sem, True, True)

    lax.fori_loop(0, n_chunks, gather, 0)

    _moe_tile_copies(nloc_ref, gtile_ref, blk, xg_ref, xg_hbm, sem, True, False)

    is_last = blk == pl.num_programs(0) - 1

    def pad_copies(wait):
        def per_expert(x, carry):
            g0 = padstart_ref[x]

            def per_tile(j, c):
                dst = xg_hbm.at[pl.ds(pl.multiple_of((g0 + j) * tile, tile), tile), :]
                cp = pltpu.make_async_copy(zero_ref, dst, sem)
                if wait:
                    cp.wait()
                else:
                    cp.start()
                return c

            lax.fori_loop(0, pad_ref[x], per_tile, 0)
            return carry

        lax.fori_loop(0, N_EXPERTS, per_expert, 0)

    @pl.when(is_last)
    def _():
        zero_ref[...] = jnp.zeros(zero_ref.shape, BF16)
        pad_copies(False)
        _moe_tile_copies(nloc_ref, gtile_ref, blk, xg_ref, xg_hbm, sem, True, True)
        pad_copies(True)


def _moe_ffn_kernel(texp_ref, nt_ref, x_ref, wgu_ref, wd_ref, y_ref):
    @pl.when(pl.program_id(0) < nt_ref[0])
    def _():
        ab = jnp.dot(x_ref[...], wgu_ref[...], preferred_element_type=F32)
        a = ab[:, :EXPERT_FF]
        hid = ((a * jax.nn.sigmoid(a)) * ab[:, EXPERT_FF:]).astype(BF16)
        y_ref[...] = jnp.dot(hid, wd_ref[...], preferred_element_type=F32).astype(BF16)


def _moe_scatter_kernel(nloc_ref, gtile_ref,
                        col_ref, x1_ref, mod_ref, y_hbm,
                        o_ref,
                        y_ref, sem):
    blk = pl.program_id(0)
    nb = x1_ref.shape[0]
    tile, chunk = MOE_TILE, MOE_CHUNK
    slot = blk & 1

    def copies(b, s, wait):
        _moe_tile_copies(nloc_ref, gtile_ref, b, y_ref.at[s], y_hbm, sem.at[s], False, wait)

    @pl.when(blk == 0)
    def _():
        copies(0, 0, False)

    @pl.when(blk + 1 < pl.num_programs(0))
    def _():
        copies(blk + 1, 1 - slot, False)

    total = nloc_ref[blk]
    n_chunks = (total * tile + (chunk - 1)) // chunk
    max_chunks = y_ref.shape[1] // chunk
    usual = n_chunks <= MOE_USUAL_CHUNKS
    n_static = jnp.where(usual, MOE_USUAL_CHUNKS, max_chunks)

    def clear(t, carry):
        y_ref[slot, pl.ds(pl.multiple_of(t * tile, tile), tile), :] = jnp.zeros((tile, y_ref.shape[2]), BF16)
        return carry

    lax.fori_loop(total, n_static * (chunk // tile), clear, 0)

    p1 = col_ref[:, 0:1].astype(jnp.int32)
    p2 = col_ref[:, 1:2].astype(jnp.int32)
    cw1 = col_ref[:, 2:3]
    cw2 = col_ref[:, 3:4]
    gate = mod_ref[5:6, :]
    lane_c = lax.broadcasted_iota(jnp.int32, (nb, chunk), 1)
    copies(blk, slot, True)

    def scatter(n_unrolled):
        acc = None
        for c in range(n_unrolled):
            p = lane_c + c * chunk
            w = (jnp.where(p == p1, cw1, 0.0) + jnp.where(p == p2, cw2, 0.0)).astype(BF16)
            part = jnp.dot(w, y_ref[slot, c * chunk:(c + 1) * chunk, :], preferred_element_type=F32)
            acc = part if acc is None else acc + part
        o_ref[...] = x1_ref[...] + gate * acc

    lax.cond(usual, lambda: scatter(MOE_USUAL_CHUNKS), lambda: scatter(max_chunks))


def _moe_call(h2, comb, cnt_tiles, x1, mod, wgu, wd):
    bsz, seq, d = x1.shape
    nb, tile, ftm = MOE_TM, MOE_TILE, MOE_FFN_TM
    n_tok = bsz * seq
    n_blk = n_tok // nb
    region = ftm // tile
    rows_local = -(-(2 * nb + N_EXPERTS * tile) // MOE_CHUNK) * MOE_CHUNK
    tiles_global = (2 * n_tok) // tile + n_blk * N_EXPERTS + N_EXPERTS * (region - 1)
    n_ffn_max = -(-tiles_global // region)
    rows_global = n_ffn_max * ftm

    cnt = cnt_tiles[:, :, 0, :N_EXPERTS].reshape(n_blk, nb // POST_TM, N_EXPERTS).sum(axis=1).astype(jnp.int32)
    ntile = (cnt + (tile - 1)) // tile
    lfirst = jnp.cumsum(ntile, axis=1) - ntile
    tot = ntile.sum(axis=0)
    ptot = (tot + (region - 1)) // region * region
    ebase = jnp.cumsum(ptot) - ptot
    gfirst = ebase[None, :] + jnp.cumsum(ntile, axis=0) - ntile
    pad = ptot - tot
    n_ffn = (ptot.sum() // region).reshape(1)
    ends = jnp.cumsum(ptot) // region
    texp = jnp.minimum((jnp.arange(n_ffn_max, dtype=jnp.int32)[:, None] >= ends[None, :]).sum(axis=1),
                       N_EXPERTS - 1).astype(jnp.int32)
    nloc = ntile.sum(axis=1).astype(jnp.int32)
    lt = jnp.arange(rows_local // tile, dtype=jnp.int32)[None, :, None]
    in_seg = (lt >= lfirst[:, None, :]) & (lt < (lfirst + ntile)[:, None, :])
    gtile = (jnp.where(in_seg, (gfirst - lfirst)[:, None, :], 0).sum(axis=2) + lt[:, :, 0]).astype(jnp.int32)
    padstart = (gfirst[-1] + ntile[-1]).astype(jnp.int32)

    h2f = h2.reshape(n_tok, d)
    combf = comb.reshape(n_tok, LANES)
    col, xg = pl.pallas_call(
        _moe_gather_kernel,
        out_shape=(jax.ShapeDtypeStruct((n_tok, LANES), F32),
                   jax.ShapeDtypeStruct((rows_global, d), BF16)),
        grid_spec=pltpu.PrefetchScalarGridSpec(
            num_scalar_prefetch=4,
            grid=(n_blk,),
            in_specs=[pl.BlockSpec((nb, d), lambda j, *_: (j, 0)),
                      pl.BlockSpec((nb, LANES), lambda j, *_: (j, 0))],
            out_specs=(pl.BlockSpec((nb, LANES), lambda j, *_: (j, 0)),
                       pl.BlockSpec(memory_space=pl.ANY)),
            scratch_shapes=[
                pltpu.VMEM((rows_local, d), BF16),
                pltpu.VMEM((LANES, nb), F32),
                pltpu.VMEM((tile, d), BF16),
                pltpu.SemaphoreType.DMA,
            ]),
        compiler_params=pltpu.CompilerParams(dimension_semantics=("arbitrary",),
                                             vmem_limit_bytes=VMEM_LIMIT_BYTES),
        name="moe_gather",
    )(nloc, gtile, padstart, pad, h2f, combf)

    last = lambda t, te, nt: jnp.minimum(t, nt[0] - 1)
    y = pl.pallas_call(
        _moe_ffn_kernel,
        out_shape=jax.ShapeDtypeStruct((rows_global, d), BF16),
        grid_spec=pltpu.PrefetchScalarGridSpec(
            num_scalar_prefetch=2,
            grid=(n_ffn_max,),
            in_specs=[pl.BlockSpec((ftm, d), lambda t, te, nt: (last(t, te, nt), 0)),
                      pl.BlockSpec((None, d, 2 * EXPERT_FF), lambda t, te, nt: (te[last(t, te, nt)], 0, 0)),
                      pl.BlockSpec((None, EXPERT_FF, d), lambda t, te, nt: (te[last(t, te, nt)], 0, 0))],
            out_specs=pl.BlockSpec((ftm, d), lambda t, te, nt: (last(t, te, nt), 0))),
        compiler_params=pltpu.CompilerParams(dimension_semantics=("arbitrary",),
                                             vmem_limit_bytes=VMEM_LIMIT_BYTES),
        name="moe_ffn",
    )(texp, n_ffn, xg, wgu, wd)

    out = pl.pallas_call(
        _moe_scatter_kernel,
        out_shape=jax.ShapeDtypeStruct((n_tok, d), F32),
        grid_spec=pltpu.PrefetchScalarGridSpec(
            num_scalar_prefetch=2,
            grid=(n_blk,),
            in_specs=[pl.BlockSpec((nb, LANES), lambda j, *_: (j, 0)),
                      pl.BlockSpec((nb, d), lambda j, *_: (j, 0)),
                      pl.BlockSpec((None, 6, d), lambda j, *_: ((j * nb) // seq, 0, 0)),
                      pl.BlockSpec(memory_space=pl.ANY)],
            out_specs=pl.BlockSpec((nb, d), lambda j, *_: (j, 0)),
            scratch_shapes=[pltpu.VMEM((2, rows_local, d), BF16),
                            pltpu.SemaphoreType.DMA((2,))]),
        compiler_params=pltpu.CompilerParams(dimension_semantics=("arbitrary",),
                                             vmem_limit_bytes=VMEM_LIMIT_BYTES),
        name="moe_scatter",
    )(nloc, gtile, col, x1.reshape(n_tok, d), mod, y)
    return out.reshape(bsz, seq, d)


def _layer(x, mod, rel_bias, norm1, w_in, q_norm, k_norm, conv_w, attn_out_norm, conv_out_norm, w_out,
           norm2, w_group_router, b_group_router, w_expert_router, b_expert_router, w_gate, w_up, w_down):
    bsz, seq, d = x.shape
    aw = ATTN_WIDTH
    topk = min(TOPK_MAX, seq // 4)

    offs = np.cumsum([0, aw, aw, aw, IDX_HEADS * IDX_DIM, IDX_DIM, IDX_HEADS, CONV_WIDTH, CONV_WIDTH, CONV_WIDTH])
    col = lambda n: w_in[:, int(offs[n]):int(offs[n + 1])]
    wm = jnp.concatenate([col(0), col(1), col(3), col(6), col(7), col(8)], axis=1).astype(BF16)
    wvt = col(2).T.astype(BF16)
    wki = jnp.concatenate([col(4), col(4)], axis=1).astype(BF16)
    wwit = col(5).T.astype(BF16)
    qg = (jnp.tile(q_norm, ATTN_HEADS) * ((HEAD_DIM ** -0.5) * LOG2E))[None, :]
    kg = jnp.tile(k_norm, ATTN_HEADS)[None, :]
    grp = np.arange(aw) // CONV_GROUP_DIM
    gmat = jnp.asarray((grp[:, None] == grp[None, :]).astype(np.float32) / CONV_GROUP_DIM, dtype=BF16)

    q, k, vt, qi, ki, wit, cn = _pre_call(
        x, mod, norm1[None, :], wm, wvt, wki, wwit, qg, kg, conv_w, conv_out_norm.reshape(1, -1), gmat)

    bounds = jnp.asarray(_bucket_boundaries())
    an = _attn_call(rel_bias, bounds, q, qi, wit, k, ki, vt, attn_out_norm.reshape(1, -1), topk)

    wr = jnp.concatenate([w_expert_router, w_group_router,
                          jnp.zeros((d, LANES - N_EXPERTS - N_GROUPS), F32)], axis=1).astype(BF16)
    br = jnp.concatenate([b_expert_router, b_group_router,
                          jnp.zeros((LANES - N_EXPERTS - N_GROUPS,), F32)])[None, :]
    x1, h2, comb, cnt_tiles = _post_call(an, cn, x, mod, norm2[None, :], w_out[:aw].astype(BF16),
                                         w_out[aw:].astype(BF16), wr, br)

    wgu = jnp.concatenate([w_gate, w_up], axis=-1).astype(BF16)
    return _moe_call(h2, comb, cnt_tiles, x1, mod, wgu, w_down.astype(BF16))


def kernel(x, c, rel_bias, w_ada, b_ada, norm1, w_in, q_norm, k_norm, conv_w, attn_out_norm, conv_out_norm,
           w_out, norm2, w_group_router, b_group_router, w_expert_router, b_expert_router, w_gate, w_up,
           w_down):
    bsz, seq, d = x.shape
    assert d == D_MODEL and seq % max(PRE_TM, POST_TM, MOE_TM) == 0 and ATT_TQ == ATT_TK
    depth = w_ada.shape[0]
    for l in range(depth):
        mod = _mod_call(c, w_ada[l], b_ada[l][None, :]).reshape(bsz, 6, d)
        x = _layer(x, mod, rel_bias, norm1[l], w_in[l], q_norm[l], k_norm[l], conv_w[l], attn_out_norm[l],
                   conv_out_norm[l], w_out[l], norm2[l], w_group_router[l], b_group_router[l],
                   w_expert_router[l], b_expert_router[l], w_gate[l], w_up[l], w_down[l])
    return x
```

```python
import functools
import math

import jax
import jax.numpy as jnp
import numpy as np
from jax import lax
from jax.experimental import pallas as pl
from jax.experimental.pallas import tpu as pltpu

F32 = jnp.float32
BF16 = jnp.bfloat16

D_MODEL = 1024
HEAD_DIM = 64
ATTN_HEADS = 8
ATTN_WIDTH = ATTN_HEADS * HEAD_DIM
CONV_WIDTH = D_MODEL - ATTN_WIDTH
CONV_GROUP_DIM = 64
CONV_K = 3
IDX_HEADS = 8
IDX_DIM = 64
TOPK_MAX = 256
IDX_SCALE = (IDX_DIM ** -0.5) * (IDX_HEADS ** -0.5)
N_BUCKETS = 32
MAX_DISTANCE = 128
N_GROUPS = 4
EXPERTS_PER_GROUP = 8
N_EXPERTS = N_GROUPS * EXPERTS_PER_GROUP
EXPERT_FF = 256
EPS = 1e-6
LOG2E = 1.4426950408889634
NEG_BIG = -1e30
COUNT_ACCS = 4
BISECT_GROUP = 4
BISECT_BF16_STEPS = 8
BISECT_VALUE_STEPS = 8
BISECT_MAX_STEPS = 64

LANES = 128
SUBLANES = 8
BF16_SUBLANES = 16
V_SLAB = HEAD_DIM + BF16_SUBLANES
VMEM_LIMIT_BYTES = 56 * 1024 * 1024

PRE_TM = 512
ATT_TQ = 256
ATT_TK = 256
POST_TM = 512
MOE_TM = 512
MOE_TILE = 32
MOE_CHUNK = 512
MOE_USUAL_CHUNKS = 3
MOE_FFN_TM = 1024
MOD_TN = 1536

_NT_DIMS = (((1,), (1,)), ((), ()))


def _tree_sum(parts):
    while len(parts) > 1:
        nxt = [parts[j] + parts[j + 1] for j in range(0, len(parts) - 1, 2)]
        if len(parts) % 2:
            nxt.append(parts[-1])
        parts = nxt
    return parts[0]


def _bucket_boundaries():
    max_exact = N_BUCKETS // 2
    d = np.arange(0, 4 * MAX_DISTANCE, dtype=np.int64)
    nf = np.maximum(d, 1).astype(np.float32)
    large = max_exact + (np.log(nf / np.float32(max_exact)) / np.float32(math.log(MAX_DISTANCE / max_exact))
                         * np.float32(N_BUCKETS - max_exact)).astype(np.int32)
    large = np.minimum(large, N_BUCKETS - 1)
    bucket = np.where(d < max_exact, d, large)
    assert np.all(np.diff(bucket) >= 0) and bucket[-1] == N_BUCKETS - 1
    bounds = [int(np.argmax(bucket >= j)) for j in range(1, N_BUCKETS)]
    return np.asarray([0] + bounds, dtype=np.int32)


def _mod_kernel(c_ref, w_ref, b_ref, o_ref):
    c = c_ref[...]
    act = c * jax.nn.sigmoid(c)
    o_ref[...] = jnp.dot(act, w_ref[...], preferred_element_type=F32,
                         precision=lax.Precision.HIGHEST) + b_ref[...]


def _mod_call(c, w_ada, b_ada):
    bsz, d = c.shape
    n = w_ada.shape[1]
    return pl.pallas_call(
        _mod_kernel,
        out_shape=jax.ShapeDtypeStruct((bsz, n), F32),
        grid=(n // MOD_TN,),
        in_specs=[pl.BlockSpec((bsz, d), lambda j: (0, 0)),
                  pl.BlockSpec((d, MOD_TN), lambda j: (0, j)),
                  pl.BlockSpec((1, MOD_TN), lambda j: (0, j))],
        out_specs=pl.BlockSpec((bsz, MOD_TN), lambda j: (0, j)),
        compiler_params=pltpu.CompilerParams(dimension_semantics=("arbitrary",),
                                             vmem_limit_bytes=VMEM_LIMIT_BYTES),
        name="adaln_mod",
    )(c, w_ada, b_ada)


def _group_rms(y, g_ref):
    ms = jnp.dot((y * y).astype(BF16), g_ref[...], preferred_element_type=F32)
    return y * lax.rsqrt(ms + EPS)


def _pre_kernel(x_ref, mod_ref, n1_ref, wm_ref, wvt_ref, wki_ref, wwit_ref, qg_ref, kg_ref,
                cw_ref, cg_ref, g_ref,
                q_ref, k_ref, vt_ref, qi_ref, ki_ref, wit_ref, cn_ref, carry_ref):
    j = pl.program_id(1)
    tm = x_ref.shape[0]
    aw = ATTN_WIDTH

    x = x_ref[...]
    ms = jnp.mean(x * x, axis=-1, keepdims=True)
    y = x * lax.rsqrt(ms + EPS) * n1_ref[...]
    h = y * (1.0 + mod_ref[1:2, :]) + mod_ref[0:1, :]
    hb = h.astype(BF16)

    def proj(lo):
        return jnp.dot(hb, wm_ref[:, lo:lo + aw], preferred_element_type=F32)

    q = _group_rms(proj(0), g_ref) * qg_ref[...]
    q_ref[...] = q.astype(BF16)
    k = _group_rms(proj(aw), g_ref) * kg_ref[...]
    k_ref[...] = k.astype(BF16)

    vt = lax.dot_general(wvt_ref[...], hb, _NT_DIMS, preferred_element_type=F32).astype(BF16)
    ones = jnp.ones((BF16_SUBLANES, ATT_TK), BF16)
    for cc in range(tm // ATT_TK):
        for hh in range(ATTN_HEADS):
            vt_ref[cc, hh * V_SLAB:hh * V_SLAB + HEAD_DIM, :] = (
                vt[hh * HEAD_DIM:(hh + 1) * HEAD_DIM, cc * ATT_TK:(cc + 1) * ATT_TK])
            vt_ref[cc, hh * V_SLAB + HEAD_DIM:(hh + 1) * V_SLAB, :] = ones

    qi_ref[...] = proj(2 * aw).astype(BF16)
    ki_ref[...] = jnp.dot(hb, wki_ref[...], preferred_element_type=F32).astype(BF16)
    wit_ref[...] = lax.dot_general(wwit_ref[...], hb, _NT_DIMS, preferred_element_type=F32) * IDX_SCALE

    gate_b = proj(3 * aw)
    z = proj(4 * aw) * proj(5 * aw)

    @pl.when(j == 0)
    def _():
        carry_ref[...] = jnp.zeros_like(carry_ref)

    prev = carry_ref[...]
    row = lax.broadcasted_iota(jnp.int32, z.shape, 0)
    z1 = jnp.where(row == 0, prev[SUBLANES - 1:SUBLANES, :], pltpu.roll(z, 1, 0))
    z2 = pltpu.roll(z, 2, 0)
    z2 = jnp.where(row == 0, prev[SUBLANES - 2:SUBLANES - 1, :], z2)
    z2 = jnp.where(row == 1, prev[SUBLANES - 1:SUBLANES, :], z2)
    carry_ref[...] = z[tm - SUBLANES:, :]
    conv = cw_ref[2:3, :] * z + cw_ref[1:2, :] * z1 + cw_ref[0:1, :] * z2
    yc = gate_b * conv
    cn_ref[...] = (_group_rms(yc, g_ref) * cg_ref[...]).astype(BF16)


def _pre_call(x, mod, n1, wm, wvt, wki, wwit, qg, kg, cw, cg, gmat):
    bsz, seq, d = x.shape
    tm = PRE_TM
    nck = tm // ATT_TK
    aw = ATTN_WIDTH
    const = lambda b, j: (0, 0)
    tok = lambda b, j: (b, j, 0)
    out_shape = (
        jax.ShapeDtypeStruct((bsz, seq, aw), BF16),
        jax.ShapeDtypeStruct((bsz, seq, aw), BF16),
        jax.ShapeDtypeStruct((bsz, seq // ATT_TK, ATTN_HEADS * V_SLAB, ATT_TK), BF16),
        jax.ShapeDtypeStruct((bsz, seq, aw), BF16),
        jax.ShapeDtypeStruct((bsz, seq, LANES), BF16),
        jax.ShapeDtypeStruct((bsz, IDX_HEADS, seq), F32),
        jax.ShapeDtypeStruct((bsz, seq, CONV_WIDTH), BF16),
    )
    out_specs = (
        pl.BlockSpec((None, tm, aw), tok),
        pl.BlockSpec((None, tm, aw), tok),
        pl.BlockSpec((None, nck, ATTN_HEADS * V_SLAB, ATT_TK), lambda b, j: (b, j, 0, 0)),
        pl.BlockSpec((None, tm, aw), tok),
        pl.BlockSpec((None, tm, LANES), tok),
        pl.BlockSpec((None, IDX_HEADS, tm), lambda b, j: (b, 0, j)),
        pl.BlockSpec((None, tm, CONV_WIDTH), tok),
    )
    in_specs = [
        pl.BlockSpec((None, tm, d), tok),
        pl.BlockSpec((None, 6, d), lambda b, j: (b, 0, 0)),
        pl.BlockSpec(n1.shape, const),
        pl.BlockSpec(wm.shape, const),
        pl.BlockSpec(wvt.shape, const),
        pl.BlockSpec(wki.shape, const),
        pl.BlockSpec(wwit.shape, const),
        pl.BlockSpec(qg.shape, const),
        pl.BlockSpec(kg.shape, const),
        pl.BlockSpec(cw.shape, const),
        pl.BlockSpec(cg.shape, const),
        pl.BlockSpec(gmat.shape, const),
    ]
    return pl.pallas_call(
        _pre_kernel,
        out_shape=out_shape,
        grid=(bsz, seq // tm),
        in_specs=in_specs,
        out_specs=out_specs,
        scratch_shapes=[pltpu.VMEM((SUBLANES, CONV_WIDTH), F32)],
        compiler_params=pltpu.CompilerParams(dimension_semantics=("arbitrary", "arbitrary"),
                                             vmem_limit_bytes=VMEM_LIMIT_BYTES),
        name="pre_proj",
    )(x, mod, n1, wm, wvt, wki, wwit, qg, kg, cw, cg, gmat)


def _attn_kernel(rb_ref, bnd_ref, q_ref, qi_ref, wit_ref, k_ref, ki_ref, vt_ref, og_ref,
                 o_ref,
                 s_ref, s16_ref, bias_ref, qpad_ref, qipad_ref, lg_ref, acc_ref, out_ref, *, topk):
    b = pl.program_id(0)
    i = pl.program_id(1)
    tq, tk = ATT_TQ, ATT_TK
    nh, hd = ATTN_HEADS, HEAD_DIM

    t_loc = lax.broadcasted_iota(jnp.int32, (tk, tq), 1)
    s_loc = lax.broadcasted_iota(jnp.int32, (tk, tq), 0)

    @pl.when((b == 0) & (i == 0))
    def _():
        for idx in range(2):
            dist = t_loc - s_loc + idx * tq
            for h in range(nh):
                bias_ref[idx, h] = jnp.full((tk, tq), (rb_ref[0, h] - rb_ref[N_BUCKETS - 1, h]) * LOG2E, F32)

            def fill(jb, carry):
                reached = dist >= bnd_ref[jb]
                for h in range(nh):
                    val = (rb_ref[jb, h] - rb_ref[N_BUCKETS - 1, h]) * LOG2E
                    bias_ref[idx, h] = jnp.where(reached, val, bias_ref[idx, h])
                return carry

            lax.fori_loop(1, N_BUCKETS, fill, 0)

    lane = lax.broadcasted_iota(jnp.int32, (tq, LANES), 1)
    for h in range(nh):
        pair = slice((h // 2) * LANES, (h // 2 + 1) * LANES)
        keep = (lane // hd) == (h % 2)
        qpad_ref[h] = jnp.where(keep, q_ref[:, pair], jnp.zeros((), BF16))
        qipad_ref[h] = jnp.where(keep, qi_ref[:, pair], jnp.zeros((), BF16))

    def idx_dots(c, slot):
        kic = ki_ref[pl.ds(pl.multiple_of(c * tk, tk), tk), :]
        for h in range(nh):
            lg_ref[slot, h] = lax.dot_general(kic, qipad_ref[h], _NT_DIMS, preferred_element_type=F32)

    def idx_reduce(c, slot, carry, diagonal):
        rmin, rmax = carry
        sc = _tree_sum([wit_ref[h:h + 1, :] * jnp.maximum(lg_ref[slot, h], 0.0) for h in range(nh)])
        if diagonal:
            causal = s_loc <= t_loc
            lo_c, hi_c = jnp.where(causal, sc, jnp.inf), jnp.where(causal, sc, -jnp.inf)
            sc = hi_c
        else:
            lo_c, hi_c = sc, sc
        s_ref[c] = sc
        s16_ref[c] = sc.astype(BF16)
        return (jnp.minimum(rmin, jnp.min(lo_c, axis=0, keepdims=True)),
                jnp.maximum(rmax, jnp.max(hi_c, axis=0, keepdims=True)))

    def idx_pair(jj, carry):
        idx_dots(2 * jj + 1, 1)
        carry = idx_reduce(2 * jj, 0, carry, False)
        idx_dots(2 * jj + 2, 0)
        return idx_reduce(2 * jj + 1, 1, carry, False)

    def idx_tail_odd(carry):
        idx_dots(i, 1)
        return idx_reduce(i, 1, idx_reduce(i - 1, 0, carry, False), True)

    idx_dots(0, 0)
    carry = (jnp.full((1, tq), jnp.inf, F32), jnp.full((1, tq), -jnp.inf, F32))
    carry = lax.fori_loop(0, i // 2, idx_pair, carry)
    rmin, rmax = lax.cond((i & 1) == 1, idx_tail_odd, lambda cr: idx_reduce(i, 0, cr, True), carry)

    def count_ge(thr):
        def body(c, accs):
            hit = s_ref[c] >= thr
            accs = list(accs)
            for r in range(tk // SUBLANES):
                a = accs[r % COUNT_ACCS]
                accs[r % COUNT_ACCS] = jnp.where(hit[r * SUBLANES:(r + 1) * SUBLANES], a + 1.0, a)
            return tuple(accs)
        accs = lax.fori_loop(0, i + 1, body,
                             tuple(jnp.zeros((SUBLANES, tq), F32) for _ in range(COUNT_ACCS)))
        return jnp.sum(_tree_sum(list(accs)), axis=0, keepdims=True)

    def count16_ge(thr16):
        def body(c, accs):
            hit = s16_ref[c] >= thr16
            accs = list(accs)
            for r in range(tk // BF16_SUBLANES):
                a = accs[r % COUNT_ACCS]
                accs[r % COUNT_ACCS] = jnp.where(hit[r * BF16_SUBLANES:(r + 1) * BF16_SUBLANES], a + 1, a)
            return tuple(accs)
        accs = lax.fori_loop(0, i + 1, body,
                             tuple(jnp.zeros((BF16_SUBLANES, tq), BF16) for _ in range(COUNT_ACCS)))
        return jnp.sum(_tree_sum(list(accs)).astype(F32), axis=0, keepdims=True)

    int_min = jnp.int32(-2 ** 31)

    def order_key(v):
        bits = pltpu.bitcast(v, jnp.int32)
        return jnp.where(bits < 0, -(bits & jnp.int32(0x7FFFFFFF)), bits)

    def from_order_key(key):
        return pltpu.bitcast(jnp.where(key < 0, (-key) | int_min, key), F32)

    t_glob = (i * tq + lax.broadcasted_iota(jnp.int32, (1, tq), 1)).astype(F32)
    n_causal = t_glob + 1.0
    kf = jnp.minimum(float(topk), n_causal)
    all_sel = n_causal <= kf

    lo16 = order_key(rmin) >> 16
    hi16 = (order_key(rmax.astype(BF16).astype(F32) + 0.0) >> 16) + 1
    for _ in range(BISECT_BF16_STEPS):
        lo_v = from_order_key(lo16 << 16)
        hi_v = from_order_key(hi16 << 16)
        mid_val16 = order_key(lo_v + (hi_v - lo_v) * 0.5) >> 16
        mid16 = jnp.where((mid_val16 > lo16) & (mid_val16 < hi16), mid_val16, (lo16 + hi16) >> 1)
        open_ = (hi16 - lo16) > 1
        cm = count16_ge(from_order_key(mid16 << 16).astype(BF16))
        lo16 = jnp.where(open_ & (cm >= kf), mid16, lo16)
        hi16 = jnp.where(open_ & (cm < kf), mid16, hi16)

    min_normal_key = jnp.int32(0x00800000)

    def snap(key, direction):
        sub = (key > -min_normal_key) & (key < min_normal_key) & (key != 0)
        below = jnp.where(key > 0, 0, -min_normal_key)
        above = jnp.where(key > 0, min_normal_key, 0)
        return jnp.where(sub, {"down": below, "up": above, "zero": jnp.zeros_like(key)}[direction], key)

    active0 = jnp.where(all_sel, 0.0, 1.0)
    thr0 = rmin
    tie0 = jnp.zeros((1, tq), F32)
    lo0 = snap(jnp.maximum((lo16 - 1) << 16, order_key(rmin)), "down")
    hi0 = snap(hi16 << 16, "up")
    fhi0 = count_ge(from_order_key(hi0))
    hif0 = jnp.full((1, tq), jnp.inf, F32)
    need0 = kf

    def b_cond(st):
        return (jnp.max(st[0]) > 0.0) & (st[8] <= BISECT_MAX_STEPS)

    def b_body(st):
        active, lo_key, hi_key, fhi, thr, tie, hif, need, step = st
        lo = from_order_key(lo_key)
        hi = from_order_key(hi_key)
        val_key = snap(order_key(lo + (hi - lo) * 0.5), "zero")
        ord_key = snap((lo_key >> 1) + (hi_key >> 1) + (lo_key & hi_key & 1), "zero")
        use_val = (step < BISECT_VALUE_STEPS) & (val_key > lo_key) & (val_key < hi_key)
        mid_key = jnp.where(use_val, val_key, ord_key)
        mid = from_order_key(mid_key)
        collapsed = (mid_key <= lo_key) | (mid_key >= hi_key) | (step >= BISECT_MAX_STEPS)
        cm = count_ge(mid)
        act = active > 0.0
        live = act & jnp.logical_not(collapsed)
        found = live & (cm == kf)
        go_up = live & (cm > kf)
        go_dn = live & (cm < kf)
        ends_tie = act & collapsed
        thr = jnp.where(found, mid, jnp.where(ends_tie, lo, thr))
        tie = jnp.where(ends_tie, 1.0, tie)
        hif = jnp.where(ends_tie, hi, hif)
        need = jnp.where(ends_tie, kf - fhi, need)
        lo_key = jnp.where(go_up, mid_key, lo_key)
        fhi = jnp.where(go_dn, cm, fhi)
        hi_key = jnp.where(go_dn, mid_key, hi_key)
        active = jnp.where(found | ends_tie, 0.0, active)
        return active, lo_key, hi_key, fhi, thr, tie, hif, need, step + 1

    def b_group(st):
        for _ in range(BISECT_GROUP):
            st = b_body(st)
        return st

    _, _, _, _, thr, tie, hif, need, _ = lax.while_loop(
        b_cond, b_group, (active0, lo0, hi0, fhi0, thr0, tie0, hif0, need0, jnp.int32(0)))

    @pl.when(jnp.max(tie) > 0.0)
    def _():
        tri = jnp.where(lax.broadcasted_iota(jnp.int32, (tk, tk), 1)
                        <= lax.broadcasted_iota(jnp.int32, (tk, tk), 0), 1.0, 0.0).astype(BF16)

        def body(c, seen):
            sc_c = s_ref[c]
            tied = (sc_c >= thr) & (sc_c < hif) & (tie > 0.0)
            rank = jnp.dot(tri, jnp.where(tied, 1.0, 0.0).astype(BF16), preferred_element_type=F32) + seen
            s_ref[c] = jnp.where(tied & (rank > need), -jnp.inf, sc_c)
            return rank[tk - 1:tk, :]

        lax.fori_loop(0, i + 1, body, jnp.zeros((1, tq), F32))

    acc_ref[...] = jnp.zeros(acc_ref.shape, F32)

    def store_logits(c, slot, bias_idx):
        masked = jnp.where(s_ref[c] >= thr, 0.0, NEG_BIG)
        row0 = pl.multiple_of(c * tk, tk)
        for h in range(nh):
            kc = k_ref[pl.ds(row0, tk), (h // 2) * LANES:(h // 2 + 1) * LANES]
            lt = lax.dot_general(kc, qpad_ref[h], _NT_DIMS, preferred_element_type=F32) + masked
            if bias_idx is not None:
                lt = lt + bias_ref[bias_idx, h]
            lg_ref[slot, h] = lt

    def softmax_pv(c, slot, m_all):
        m_out = []
        for h in range(nh):
            m_old = m_all[h]
            m_new = jnp.maximum(m_old, jnp.max(lg_ref[slot, h], axis=0, keepdims=True))
            p = jnp.exp2(lg_ref[slot, h] - m_new).astype(BF16)
            alpha = jnp.exp2(m_old - m_new)
            pv = jnp.dot(vt_ref[c, h * V_SLAB:(h + 1) * V_SLAB, :], p, preferred_element_type=F32)
            acc_ref[h] = alpha * acc_ref[h] + pv
            m_out.append(m_new)
        return tuple(m_out)

    def near_step(m_all):
        store_logits(i - 1, 1, 1)
        return softmax_pv(i, 0, m_all)

    def far_step(j, parity, m_all):
        c = i - 2 - j
        store_logits(c, parity, None)
        return softmax_pv(c + 1, 1 - parity, m_all)

    def far_pair(jj, m_all):
        return far_step(2 * jj + 1, 1, far_step(2 * jj, 0, m_all))

    n_far = jnp.maximum(i - 1, 0)
    m_all = tuple(jnp.full((1, tq), NEG_BIG, F32) for _ in range(nh))
    store_logits(i, 0, 0)
    m_all = lax.cond(i >= 1, near_step, lambda m: m, m_all)
    m_all = lax.fori_loop(0, n_far // 2, far_pair, m_all)
    m_all = lax.cond((n_far & 1) == 1, lambda m: far_step(n_far - 1, 0, m), lambda m: m, m_all)
    lax.cond((i & 1) == 0, lambda m: softmax_pv(0, 0, m), lambda m: softmax_pv(0, 1, m), m_all)

    for h in range(nh):
        o = acc_ref[h, :hd, :] / acc_ref[h, hd:hd + 1, :]
        ms = jnp.mean(o * o, axis=0, keepdims=True)
        out_ref[h * hd:(h + 1) * hd, :] = o * lax.rsqrt(ms + EPS)
    o_ref[...] = (out_ref[...].T * og_ref[...]).astype(BF16)


def _attn_call(rel_bias, bounds, q, qi, wit, k, ki, vt, og, topk):
    bsz, seq, aw = q.shape
    tq, tk = ATT_TQ, ATT_TK
    nck = seq // tk
    blk_q = lambda b, i: (b, i, 0)
    whole = lambda b, i: (b, 0, 0)
    smem = pl.BlockSpec(memory_space=pltpu.SMEM)
    return pl.pallas_call(
        functools.partial(_attn_kernel, topk=topk),
        out_shape=jax.ShapeDtypeStruct((bsz, seq, aw), BF16),
        grid=(bsz, seq // tq),
        in_specs=[
            smem, smem,
            pl.BlockSpec((None, tq, aw), blk_q),
            pl.BlockSpec((None, tq, aw), blk_q),
            pl.BlockSpec((None, IDX_HEADS, tq), lambda b, i: (b, 0, i)),
            pl.BlockSpec((None, seq, aw), whole),
            pl.BlockSpec((None, seq, LANES), whole),
            pl.BlockSpec((None, nck, ATTN_HEADS * V_SLAB, tk), lambda b, i: (b, 0, 0, 0)),
            pl.BlockSpec(og.shape, lambda b, i: (0, 0)),
        ],
        out_specs=pl.BlockSpec((None, tq, aw), blk_q),
        scratch_shapes=[
            pltpu.VMEM((nck, tk, tq), F32),
            pltpu.VMEM((nck, tk, tq), BF16),
            pltpu.VMEM((2, ATTN_HEADS, tk, tq), F32),
            pltpu.VMEM((ATTN_HEADS, tq, LANES), BF16),
            pltpu.VMEM((IDX_HEADS, tq, LANES), BF16),
            pltpu.VMEM((2, ATTN_HEADS, tk, tq), F32),
            pltpu.VMEM((ATTN_HEADS, V_SLAB, tq), F32),
            pltpu.VMEM((aw, tq), F32),
        ],
        compiler_params=pltpu.CompilerParams(dimension_semantics=("arbitrary", "arbitrary"),
                                             vmem_limit_bytes=VMEM_LIMIT_BYTES),
        name="dsa_attention",
    )(rel_bias, bounds, q, qi, wit, k, ki, vt, og)


def _post_kernel(an_ref, cn_ref, x_ref, mod_ref, n2_ref, woa_ref, woc_ref, wr_ref, br_ref,
                 x1_ref, h2_ref, comb_ref, cnt_ref):
    mix = (jnp.dot(an_ref[...], woa_ref[...], preferred_element_type=F32)
           + jnp.dot(cn_ref[...], woc_ref[...], preferred_element_type=F32))
    x1 = x_ref[...] + mod_ref[2:3, :] * mix
    x1_ref[...] = x1
    ms = jnp.mean(x1 * x1, axis=-1, keepdims=True)
    h2 = x1 * lax.rsqrt(ms + EPS) * n2_ref[...] * (1.0 + mod_ref[4:5, :]) + mod_ref[3:4, :]
    h2b = h2.astype(BF16)
    h2_ref[...] = h2b

    logits = jnp.dot(h2b, wr_ref[...], preferred_element_type=F32) + br_ref[...]
    lane = lax.broadcasted_iota(jnp.int32, logits.shape, 1)
    lane_f = lane.astype(F32)
    far = float(LANES)
    is_g = (lane >= N_EXPERTS) & (lane < N_EXPERTS + N_GROUPS)
    gl = jnp.where(is_g, logits, -jnp.inf)
    gmax = jnp.max(gl, axis=-1, keepdims=True)
    g_sel = jnp.min(jnp.where(is_g & (gl == gmax), lane_f, far), axis=-1, keepdims=True) - float(N_EXPERTS)
    p_g = 1.0 / jnp.sum(jnp.exp(gl - gmax), axis=-1, keepdims=True)

    in_grp = (lane < N_EXPERTS) & ((lane // EXPERTS_PER_GROUP).astype(F32) == g_sel)
    e1 = jnp.where(in_grp, logits, -jnp.inf)
    l1 = jnp.max(e1, axis=-1, keepdims=True)
    i1 = jnp.min(jnp.where(in_grp & (e1 == l1), lane_f, far), axis=-1, keepdims=True)
    rest = in_grp & (lane_f != i1)
    e2 = jnp.where(rest, logits, -jnp.inf)
    l2 = jnp.max(e2, axis=-1, keepdims=True)
    i2 = jnp.min(jnp.where(rest & (e2 == l2), lane_f, far), axis=-1, keepdims=True)
    r = jnp.exp(l2 - l1)
    w1 = 1.0 / (1.0 + r)
    w2 = r / (1.0 + r)
    comb = jnp.where(lane_f == i1, p_g * w1, 0.0) + jnp.where(lane_f == i2, p_g * w2, 0.0)
    comb_ref[...] = comb
    cnt = jnp.sum(jnp.where(comb != 0.0, 1.0, 0.0), axis=0, keepdims=True)
    cnt_ref[...] = jnp.broadcast_to(cnt, cnt_ref.shape)


def _post_call(an, cn, x, mod, n2, woa, woc, wr, br):
    bsz, seq, d = x.shape
    tm = POST_TM
    tok = lambda b, j: (b, j, 0)
    const = lambda b, j: (0, 0)
    return pl.pallas_call(
        _post_kernel,
        out_shape=(jax.ShapeDtypeStruct((bsz, seq, d), F32),
                   jax.ShapeDtypeStruct((bsz, seq, d), BF16),
                   jax.ShapeDtypeStruct((bsz, seq, LANES), F32),
                   jax.ShapeDtypeStruct((bsz, seq // tm, SUBLANES, LANES), F32)),
        grid=(bsz, seq // tm),
        in_specs=[
            pl.BlockSpec((None, tm, ATTN_WIDTH), tok),
            pl.BlockSpec((None, tm, CONV_WIDTH), tok),
            pl.BlockSpec((None, tm, d), tok),
            pl.BlockSpec((None, 6, d), lambda b, j: (b, 0, 0)),
            pl.BlockSpec(n2.shape, const),
            pl.BlockSpec(woa.shape, const),
            pl.BlockSpec(woc.shape, const),
            pl.BlockSpec(wr.shape, const),
            pl.BlockSpec(br.shape, const),
        ],
        out_specs=(pl.BlockSpec((None, tm, d), tok),
                   pl.BlockSpec((None, tm, d), tok),
                   pl.BlockSpec((None, tm, LANES), tok),
                   pl.BlockSpec((None, None, SUBLANES, LANES), lambda b, j: (b, j, 0, 0))),
        compiler_params=pltpu.CompilerParams(dimension_semantics=("arbitrary", "arbitrary"),
                                             vmem_limit_bytes=VMEM_LIMIT_BYTES),
        name="post_router",
    )(an, cn, x, mod, n2, woa, woc, wr, br)


def _strict_tri(n, lower):
    r = lax.broadcasted_iota(jnp.int32, (n, n), 0)
    c = lax.broadcasted_iota(jnp.int32, (n, n), 1)
    return jnp.where((c < r) if lower else (r < c), 1.0, 0.0).astype(BF16)


def _moe_tile_copies(nloc_ref, gtile_ref, blk, local_ref, global_ref, sem, to_global, wait):
    tile = MOE_TILE

    def per_tile(lt, c):
        loc = local_ref.at[pl.ds(pl.multiple_of(lt * tile, tile), tile), :]
        glo = global_ref.at[pl.ds(pl.multiple_of(gtile_ref[blk, lt] * tile, tile), tile), :]
        cp = pltpu.make_async_copy(loc, glo, sem) if to_global else pltpu.make_async_copy(glo, loc, sem)
        if wait:
            cp.wait()
        else:
            cp.start()
        return c

    lax.fori_loop(0, nloc_ref[blk], per_tile, 0)


def _moe_gather_kernel(nloc_ref, gtile_ref, padstart_ref, pad_ref,
                       h2_ref, comb_ref,
                       col_ref, xg_hbm,
                       xg_ref, row_ref, zero_ref, sem):
    blk = pl.program_id(0)
    nb = h2_ref.shape[0]
    tile, chunk = MOE_TILE, MOE_CHUNK
    lane = lax.broadcasted_iota(jnp.int32, (nb, LANES), 1)

    comb = comb_ref[...]
    assigned = comb != 0.0
    a_f = jnp.where(assigned, 1.0, 0.0)
    rank = jnp.dot(_strict_tri(nb, True), a_f.astype(BF16), preferred_element_type=F32)
    cnt = rank[nb - 1:nb, :] + a_f[nb - 1:nb, :]
    ntile = jnp.floor((cnt + float(tile - 1)) * (1.0 / tile))
    first = jnp.dot(jnp.broadcast_to(ntile, (SUBLANES, LANES)).astype(BF16), _strict_tri(LANES, False),
                    preferred_element_type=F32)[0:1, :]
    pos = first * float(tile) + rank
    pos1 = jnp.min(jnp.where(assigned, pos, 1e9), axis=1, keepdims=True)
    pos2 = jnp.max(jnp.where(assigned, pos, -1.0), axis=1, keepdims=True)
    pos2 = jnp.where(pos2 == pos1, -1.0, pos2)
    cw1 = jnp.sum(jnp.where(assigned & (pos == pos1), comb, 0.0), axis=1, keepdims=True)
    cw2 = jnp.sum(jnp.where(assigned & (pos == pos2), comb, 0.0), axis=1, keepdims=True)
    info = jnp.where(lane == 0, pos1, jnp.where(lane == 1, pos2, jnp.where(lane == 2, cw1,
                     jnp.where(lane == 3, cw2, 0.0))))
    col_ref[...] = info
    row_ref[...] = info.T

    n_chunks = (nloc_ref[blk] * tile + (chunk - 1)) // chunk
    p1 = row_ref[0:1, :].astype(jnp.int32)
    p2 = row_ref[1:2, :].astype(jnp.int32)
    sub = lax.broadcasted_iota(jnp.int32, (chunk, nb), 0)

    def gather(c, carry):
        p = sub + c * chunk
        sel = jnp.where((p == p1) | (p == p2), 1.0, 0.0).astype(BF16)
        r0 = pl.multiple_of(c * chunk, chunk)
        xg_ref[pl.ds(r0, chunk), :] = jnp.dot(sel, h2_ref[...], preferred_element_type=F32).astype(BF16)
        return carry

    @pl.when(blk > 0)
    def _():
        _moe_tile_copies(nloc_ref, gtile_ref, blk - 1, xg_ref, xg_hbm, sem, True, True)

    lax.fori_loop(0, n_chunks, gather, 0)

    _moe_tile_copies(nloc_ref, gtile_ref, blk, xg_ref, xg_hbm, sem, True, False)

    is_last = blk == pl.num_programs(0) - 1

    def pad_copies(wait):
        def per_expert(x, carry):
            g0 = padstart_ref[x]

            def per_tile(j, c):
                dst = xg_hbm.at[pl.ds(pl.multiple_of((g0 + j) * tile, tile), tile), :]
                cp = pltpu.make_async_copy(zero_ref, dst, sem)
                if wait:
                    cp.wait()
                else:
                    cp.start()
                return c

            lax.fori_loop(0, pad_ref[x], per_tile, 0)
            return carry

        lax.fori_loop(0, N_EXPERTS, per_expert, 0)

    @pl.when(is_last)
    def _():
        zero_ref[...] = jnp.zeros(zero_ref.shape, BF16)
        pad_copies(False)
        _moe_tile_copies(nloc_ref, gtile_ref, blk, xg_ref, xg_hbm, sem, True, True)
        pad_copies(True)


def _moe_ffn_kernel(texp_ref, nt_ref, x_ref, wgu_ref, wd_ref, y_ref):
    @pl.when(pl.program_id(0) < nt_ref[0])
    def _():
        ab = jnp.dot(x_ref[...], wgu_ref[...], preferred_element_type=F32)
        a = ab[:, :EXPERT_FF]
        hid = ((a * jax.nn.sigmoid(a)) * ab[:, EXPERT_FF:]).astype(BF16)
        y_ref[...] = jnp.dot(hid, wd_ref[...], preferred_element_type=F32).astype(BF16)


def _moe_scatter_kernel(nloc_ref, gtile_ref,
                        col_ref, x1_ref, mod_ref, y_hbm,
                        o_ref,
                        y_ref, sem):
    blk = pl.program_id(0)
    nb = x1_ref.shape[0]
    tile, chunk = MOE_TILE, MOE_CHUNK
    slot = blk & 1

    def copies(b, s, wait):
        _moe_tile_copies(nloc_ref, gtile_ref, b, y_ref.at[s], y_hbm, sem.at[s], False, wait)

    @pl.when(blk == 0)
    def _():
        copies(0, 0, False)

    @pl.when(blk + 1 < pl.num_programs(0))
    def _():
        copies(blk + 1, 1 - slot, False)

    total = nloc_ref[blk]
    n_chunks = (total * tile + (chunk - 1)) // chunk
    max_chunks = y_ref.shape[1] // chunk
    usual = n_chunks <= MOE_USUAL_CHUNKS
    n_static = jnp.where(usual, MOE_USUAL_CHUNKS, max_chunks)

    def clear(t, carry):
        y_ref[slot, pl.ds(pl.multiple_of(t * tile, tile), tile), :] = jnp.zeros((tile, y_ref.shape[2]), BF16)
        return carry

    lax.fori_loop(total, n_static * (chunk // tile), clear, 0)

    p1 = col_ref[:, 0:1].astype(jnp.int32)
    p2 = col_ref[:, 1:2].astype(jnp.int32)
    cw1 = col_ref[:, 2:3]
    cw2 = col_ref[:, 3:4]
    gate = mod_ref[5:6, :]
    lane_c = lax.broadcasted_iota(jnp.int32, (nb, chunk), 1)
    copies(blk, slot, True)

    def scatter(n_unrolled):
        acc = None
        for c in range(n_unrolled):
            p = lane_c + c * chunk
            w = (jnp.where(p == p1, cw1, 0.0) + jnp.where(p == p2, cw2, 0.0)).astype(BF16)
            part = jnp.dot(w, y_ref[slot, c * chunk:(c + 1) * chunk, :], preferred_element_type=F32)
            acc = part if acc is None else acc + part
        o_ref[...] = x1_ref[...] + gate * acc

    lax.cond(usual, lambda: scatter(MOE_USUAL_CHUNKS), lambda: scatter(max_chunks))


def _moe_call(h2, comb, cnt_tiles, x1, mod, wgu, wd):
    bsz, seq, d = x1.shape
    nb, tile, ftm = MOE_TM, MOE_TILE, MOE_FFN_TM
    n_tok = bsz * seq
    n_blk = n_tok // nb
    region = ftm // tile
    rows_local = -(-(2 * nb + N_EXPERTS * tile) // MOE_CHUNK) * MOE_CHUNK
    tiles_global = (2 * n_tok) // tile + n_blk * N_EXPERTS + N_EXPERTS * (region - 1)
    n_ffn_max = -(-tiles_global // region)
    rows_global = n_ffn_max * ftm

    cnt = cnt_tiles[:, :, 0, :N_EXPERTS].reshape(n_blk, nb // POST_TM, N_EXPERTS).sum(axis=1).astype(jnp.int32)
    ntile = (cnt + (tile - 1)) // tile
    lfirst = jnp.cumsum(ntile, axis=1) - ntile
    tot = ntile.sum(axis=0)
    ptot = (tot + (region - 1)) // region * region
    ebase = jnp.cumsum(ptot) - ptot
    gfirst = ebase[None, :] + jnp.cumsum(ntile, axis=0) - ntile
    pad = ptot - tot
    n_ffn = (ptot.sum() // region).reshape(1)
    ends = jnp.cumsum(ptot) // region
    texp = jnp.minimum((jnp.arange(n_ffn_max, dtype=jnp.int32)[:, None] >= ends[None, :]).sum(axis=1),
                       N_EXPERTS - 1).astype(jnp.int32)
    nloc = ntile.sum(axis=1).astype(jnp.int32)
    lt = jnp.arange(rows_local // tile, dtype=jnp.int32)[None, :, None]
    in_seg = (lt >= lfirst[:, None, :]) & (lt < (lfirst + ntile)[:, None, :])
    gtile = (jnp.where(in_seg, (gfirst - lfirst)[:, None, :], 0).sum(axis=2) + lt[:, :, 0]).astype(jnp.int32)
    padstart = (gfirst[-1] + ntile[-1]).astype(jnp.int32)

    h2f = h2.reshape(n_tok, d)
    combf = comb.reshape(n_tok, LANES)
    col, xg = pl.pallas_call(
        _moe_gather_kernel,
        out_shape=(jax.ShapeDtypeStruct((n_tok, LANES), F32),
                   jax.ShapeDtypeStruct((rows_global, d), BF16)),
        grid_spec=pltpu.PrefetchScalarGridSpec(
            num_scalar_prefetch=4,
            grid=(n_blk,),
            in_specs=[pl.BlockSpec((nb, d), lambda j, *_: (j, 0)),
                      pl.BlockSpec((nb, LANES), lambda j, *_: (j, 0))],
            out_specs=(pl.BlockSpec((nb, LANES), lambda j, *_: (j, 0)),
                       pl.BlockSpec(memory_space=pl.ANY)),
            scratch_shapes=[
                pltpu.VMEM((rows_local, d), BF16),
                pltpu.VMEM((LANES, nb), F32),
                pltpu.VMEM((tile, d), BF16),
                pltpu.SemaphoreType.DMA,
            ]),
        compiler_params=pltpu.CompilerParams(dimension_semantics=("arbitrary",),
                                             vmem_limit_bytes=VMEM_LIMIT_BYTES),
        name="moe_gather",
    )(nloc, gtile, padstart, pad, h2f, combf)

    last = lambda t, te, nt: jnp.minimum(t, nt[0] - 1)
    y = pl.pallas_call(
        _moe_ffn_kernel,
        out_shape=jax.ShapeDtypeStruct((rows_global, d), BF16),
        grid_spec=pltpu.PrefetchScalarGridSpec(
            num_scalar_prefetch=2,
            grid=(n_ffn_max,),
            in_specs=[pl.BlockSpec((ftm, d), lambda t, te, nt: (last(t, te, nt), 0)),
                      pl.BlockSpec((None, d, 2 * EXPERT_FF), lambda t, te, nt: (te[last(t, te, nt)], 0, 0)),
                      pl.BlockSpec((None, EXPERT_FF, d), lambda t, te, nt: (te[last(t, te, nt)], 0, 0))],
            out_specs=pl.BlockSpec((ftm, d), lambda t, te, nt: (last(t, te, nt), 0))),
        compiler_params=pltpu.CompilerParams(dimension_semantics=("arbitrary",),
                                             vmem_limit_bytes=VMEM_LIMIT_BYTES),
        name="moe_ffn",
    )(texp, n_ffn, xg, wgu, wd)

    out = pl.pallas_call(
        _moe_scatter_kernel,
        out_shape=jax.ShapeDtypeStruct((n_tok, d), F32),
        grid_spec=pltpu.PrefetchScalarGridSpec(
            num_scalar_prefetch=2,
            grid=(n_blk,),
            in_specs=[pl.BlockSpec((nb, LANES), lambda j, *_: (j, 0)),
                      pl.BlockSpec((nb, d), lambda j, *_: (j, 0)),
                      pl.BlockSpec((None, 6, d), lambda j, *_: ((j * nb) // seq, 0, 0)),
                      pl.BlockSpec(memory_space=pl.ANY)],
            out_specs=pl.BlockSpec((nb, d), lambda j, *_: (j, 0)),
            scratch_shapes=[pltpu.VMEM((2, rows_local, d), BF16),
                            pltpu.SemaphoreType.DMA((2,))]),
        compiler_params=pltpu.CompilerParams(dimension_semantics=("arbitrary",),
                                             vmem_limit_bytes=VMEM_LIMIT_BYTES),
        name="moe_scatter",
    )(nloc, gtile, col, x1.reshape(n_tok, d), mod, y)
    return out.reshape(bsz, seq, d)


def _layer(x, mod, rel_bias, norm1, w_in, q_norm, k_norm, conv_w, attn_out_norm, conv_out_norm, w_out,
           norm2, w_group_router, b_group_router, w_expert_router, b_expert_router, w_gate, w_up, w_down):
    bsz, seq, d = x.shape
    aw = ATTN_WIDTH
    topk = min(TOPK_MAX, seq // 4)

    offs = np.cumsum([0, aw, aw, aw, IDX_HEADS * IDX_DIM, IDX_DIM, IDX_HEADS, CONV_WIDTH, CONV_WIDTH, CONV_WIDTH])
    col = lambda n: w_in[:, int(offs[n]):int(offs[n + 1])]
    wm = jnp.concatenate([col(0), col(1), col(3), col(6), col(7), col(8)], axis=1).astype(BF16)
    wvt = col(2).T.astype(BF16)
    wki = jnp.concatenate([col(4), col(4)], axis=1).astype(BF16)
    wwit = col(5).T.astype(BF16)
    qg = (jnp.tile(q_norm, ATTN_HEADS) * ((HEAD_DIM ** -0.5) * LOG2E))[None, :]
    kg = jnp.tile(k_norm, ATTN_HEADS)[None, :]
    grp = np.arange(aw) // CONV_GROUP_DIM
    gmat = jnp.asarray((grp[:, None] == grp[None, :]).astype(np.float32) / CONV_GROUP_DIM, dtype=BF16)

    q, k, vt, qi, ki, wit, cn = _pre_call(
        x, mod, norm1[None, :], wm, wvt, wki, wwit, qg, kg, conv_w, conv_out_norm.reshape(1, -1), gmat)

    bounds = jnp.asarray(_bucket_boundaries())
    an = _attn_call(rel_bias, bounds, q, qi, wit, k, ki, vt, attn_out_norm.reshape(1, -1), topk)

    wr = jnp.concatenate([w_expert_router, w_group_router,
                          jnp.zeros((d, LANES - N_EXPERTS - N_GROUPS), F32)], axis=1).astype(BF16)
    br = jnp.concatenate([b_expert_router, b_group_router,
                          jnp.zeros((LANES - N_EXPERTS - N_GROUPS,), F32)])[None, :]
    x1, h2, comb, cnt_tiles = _post_call(an, cn, x, mod, norm2[None, :], w_out[:aw].astype(BF16),
                                         w_out[aw:].astype(BF16), wr, br)

    wgu = jnp.concatenate([w_gate, w_up], axis=-1).astype(BF16)
    return _moe_call(h2, comb, cnt_tiles, x1, mod, wgu, w_down.astype(BF16))


def kernel(x, c, rel_bias, w_ada, b_ada, norm1, w_in, q_norm, k_norm, conv_w, attn_out_norm, conv_out_norm,
           w_out, norm2, w_group_router, b_group_router, w_expert_router, b_expert_router, w_gate, w_up,
           w_down):
    bsz, seq, d = x.shape
    assert d == D_MODEL and seq % max(PRE_TM, POST_TM, MOE_TM) == 0 and ATT_TQ == ATT_TK
    depth = w_ada.shape[0]
    for l in range(depth):
        mod = _mod_call(c, w_ada[l], b_ada[l][None, :]).reshape(bsz, 6, d)
        x = _layer(x, mod, rel_bias, norm1[l], w_in[l], q_norm[l], k_norm[l], conv_w[l], attn_out_norm[l],
                   conv_out_norm[l], w_out[l], norm2[l], w_group_router[l], b_group_router[l],
                   w_expert_router[l], b_expert_router[l], w_gate[l], w_up[l], w_down[l])
    return x
```

```python
import functools
import math

import jax
import jax.numpy as jnp
import numpy as np
from jax import lax
from jax.experimental import pallas as pl
from jax.experimental.pallas import tpu as pltpu

F32 = jnp.float32
BF16 = jnp.bfloat16

D_MODEL = 1024
HEAD_DIM = 64
ATTN_HEADS = 8
ATTN_WIDTH = ATTN_HEADS * HEAD_DIM
CONV_WIDTH = D_MODEL - ATTN_WIDTH
CONV_GROUP_DIM = 64
CONV_K = 3
IDX_HEADS = 8
IDX_DIM = 64
TOPK_MAX = 256
IDX_SCALE = (IDX_DIM ** -0.5) * (IDX_HEADS ** -0.5)
N_BUCKETS = 32
MAX_DISTANCE = 128
N_GROUPS = 4
EXPERTS_PER_GROUP = 8
N_EXPERTS = N_GROUPS * EXPERTS_PER_GROUP
EXPERT_FF = 256
EPS = 1e-6
LOG2E = 1.4426950408889634
NEG_BIG = -1e30
COUNT_ACCS = 4
BISECT_GROUP = 4
BISECT_BF16_STEPS = 8
BISECT_VALUE_STEPS = 8
BISECT_MAX_STEPS = 64

LANES = 128
SUBLANES = 8
BF16_SUBLANES = 16
V_SLAB = HEAD_DIM + BF16_SUBLANES
VMEM_LIMIT_BYTES = 56 * 1024 * 1024

PRE_TM = 512
ATT_TQ = 256
ATT_TK = 256
POST_TM = 512
MOE_TM = 512
MOE_TILE = 32
MOE_CHUNK = 512
MOE_USUAL_CHUNKS = 3
MOE_FFN_TM = 1024
MOD_TN = 1536

_NT_DIMS = (((1,), (1,)), ((), ()))


def _tree_sum(parts):
    while len(parts) > 1:
        nxt = [parts[j] + parts[j + 1] for j in range(0, len(parts) - 1, 2)]
        if len(parts) % 2:
            nxt.append(parts[-1])
        parts = nxt
    return parts[0]


def _bucket_boundaries():
    max_exact = N_BUCKETS // 2
    d = np.arange(0, 4 * MAX_DISTANCE, dtype=np.int64)
    nf = np.maximum(d, 1).astype(np.float32)
    large = max_exact + (np.log(nf / np.float32(max_exact)) / np.float32(math.log(MAX_DISTANCE / max_exact))
                         * np.float32(N_BUCKETS - max_exact)).astype(np.int32)
    large = np.minimum(large, N_BUCKETS - 1)
    bucket = np.where(d < max_exact, d, large)
    assert np.all(np.diff(bucket) >= 0) and bucket[-1] == N_BUCKETS - 1
    bounds = [int(np.argmax(bucket >= j)) for j in range(1, N_BUCKETS)]
    return np.asarray([0] + bounds, dtype=np.int32)


def _mod_kernel(c_ref, w_ref, b_ref, o_ref):
    c = c_ref[...]
    act = c * jax.nn.sigmoid(c)
    o_ref[...] = jnp.dot(act, w_ref[...], preferred_element_type=F32,
                         precision=lax.Precision.HIGHEST) + b_ref[...]


def _mod_call(c, w_ada, b_ada):
    bsz, d = c.shape
    n = w_ada.shape[1]
    return pl.pallas_call(
        _mod_kernel,
        out_shape=jax.ShapeDtypeStruct((bsz, n), F32),
        grid=(n // MOD_TN,),
        in_specs=[pl.BlockSpec((bsz, d), lambda j: (0, 0)),
                  pl.BlockSpec((d, MOD_TN), lambda j: (0, j)),
                  pl.BlockSpec((1, MOD_TN), lambda j: (0, j))],
        out_specs=pl.BlockSpec((bsz, MOD_TN), lambda j: (0, j)),
        compiler_params=pltpu.CompilerParams(dimension_semantics=("arbitrary",),
                                             vmem_limit_bytes=VMEM_LIMIT_BYTES),
        name="adaln_mod",
    )(c, w_ada, b_ada)


def _group_rms(y, g_ref):
    ms = jnp.dot((y * y).astype(BF16), g_ref[...], preferred_element_type=F32)
    return y * lax.rsqrt(ms + EPS)


def _pre_kernel(x_ref, mod_ref, n1_ref, wm_ref, wvt_ref, wki_ref, wwit_ref, qg_ref, kg_ref,
                cw_ref, cg_ref, g_ref,
                q_ref, k_ref, vt_ref, qi_ref, ki_ref, wit_ref, cn_ref, carry_ref):
    j = pl.program_id(1)
    tm = x_ref.shape[0]
    aw = ATTN_WIDTH

    x = x_ref[...]
    ms = jnp.mean(x * x, axis=-1, keepdims=True)
    y = x * lax.rsqrt(ms + EPS) * n1_ref[...]
    h = y * (1.0 + mod_ref[1:2, :]) + mod_ref[0:1, :]
    hb = h.astype(BF16)

    def proj(lo):
        return jnp.dot(hb, wm_ref[:, lo:lo + aw], preferred_element_type=F32)

    q = _group_rms(proj(0), g_ref) * qg_ref[...]
    q_ref[...] = q.astype(BF16)
    k = _group_rms(proj(aw), g_ref) * kg_ref[...]
    k_ref[...] = k.astype(BF16)

    vt = lax.dot_general(wvt_ref[...], hb, _NT_DIMS, preferred_element_type=F32).astype(BF16)
    ones = jnp.ones((BF16_SUBLANES, ATT_TK), BF16)
    for cc in range(tm // ATT_TK):
        for hh in range(ATTN_HEADS):
            vt_ref[cc, hh * V_SLAB:hh * V_SLAB + HEAD_DIM, :] = (
                vt[hh * HEAD_DIM:(hh + 1) * HEAD_DIM, cc * ATT_TK:(cc + 1) * ATT_TK])
            vt_ref[cc, hh * V_SLAB + HEAD_DIM:(hh + 1) * V_SLAB, :] = ones

    qi_ref[...] = proj(2 * aw).astype(BF16)
    ki_ref[...] = jnp.dot(hb, wki_ref[...], preferred_element_type=F32).astype(BF16)
    wit_ref[...] = lax.dot_general(wwit_ref[...], hb, _NT_DIMS, preferred_element_type=F32) * IDX_SCALE

    gate_b = proj(3 * aw)
    z = proj(4 * aw) * proj(5 * aw)

    @pl.when(j == 0)
    def _():
        carry_ref[...] = jnp.zeros_like(carry_ref)

    prev = carry_ref[...]
    row = lax.broadcasted_iota(jnp.int32, z.shape, 0)
    z1 = jnp.where(row == 0, prev[SUBLANES - 1:SUBLANES, :], pltpu.roll(z, 1, 0))
    z2 = pltpu.roll(z, 2, 0)
    z2 = jnp.where(row == 0, prev[SUBLANES - 2:SUBLANES - 1, :], z2)
    z2 = jnp.where(row == 1, prev[SUBLANES - 1:SUBLANES, :], z2)
    carry_ref[...] = z[tm - SUBLANES:, :]
    conv = cw_ref[2:3, :] * z + cw_ref[1:2, :] * z1 + cw_ref[0:1, :] * z2
    yc = gate_b * conv
    cn_ref[...] = (_group_rms(yc, g_ref) * cg_ref[...]).astype(BF16)


def _pre_call(x, mod, n1, wm, wvt, wki, wwit, qg, kg, cw, cg, gmat):
    bsz, seq, d = x.shape
    tm = PRE_TM
    nck = tm // ATT_TK
    aw = ATTN_WIDTH
    const = lambda b, j: (0, 0)
    tok = lambda b, j: (b, j, 0)
    out_shape = (
        jax.ShapeDtypeStruct((bsz, seq, aw), BF16),
        jax.ShapeDtypeStruct((bsz, seq, aw), BF16),
        jax.ShapeDtypeStruct((bsz, seq // ATT_TK, ATTN_HEADS * V_SLAB, ATT_TK), BF16),
        jax.ShapeDtypeStruct((bsz, seq, aw), BF16),
        jax.ShapeDtypeStruct((bsz, seq, LANES), BF16),
        jax.ShapeDtypeStruct((bsz, IDX_HEADS, seq), F32),
        jax.ShapeDtypeStruct((bsz, seq, CONV_WIDTH), BF16),
    )
    out_specs = (
        pl.BlockSpec((None, tm, aw), tok),
        pl.BlockSpec((None, tm, aw), tok),
        pl.BlockSpec((None, nck, ATTN_HEADS * V_SLAB, ATT_TK), lambda b, j: (b, j, 0, 0)),
        pl.BlockSpec((None, tm, aw), tok),
        pl.BlockSpec((None, tm, LANES), tok),
        pl.BlockSpec((None, IDX_HEADS, tm), lambda b, j: (b, 0, j)),
        pl.BlockSpec((None, tm, CONV_WIDTH), tok),
    )
    in_specs = [
        pl.BlockSpec((None, tm, d), tok),
        pl.BlockSpec((None, 6, d), lambda b, j: (b, 0, 0)),
        pl.BlockSpec(n1.shape, const),
        pl.BlockSpec(wm.shape, const),
        pl.BlockSpec(wvt.shape, const),
        pl.BlockSpec(wki.shape, const),
        pl.BlockSpec(wwit.shape, const),
        pl.BlockSpec(qg.shape, const),
        pl.BlockSpec(kg.shape, const),
        pl.BlockSpec(cw.shape, const),
        pl.BlockSpec(cg.shape, const),
        pl.BlockSpec(gmat.shape, const),
    ]
    return pl.pallas_call(
        _pre_kernel,
        out_shape=out_shape,
        grid=(bsz, seq // tm),
        in_specs=in_specs,
        out_specs=out_specs,
        scratch_shapes=[pltpu.VMEM((SUBLANES, CONV_WIDTH), F32)],
        compiler_params=pltpu.CompilerParams(dimension_semantics=("arbitrary", "arbitrary"),
                                             vmem_limit_bytes=VMEM_LIMIT_BYTES),
        name="pre_proj",
    )(x, mod, n1, wm, wvt, wki, wwit, qg, kg, cw, cg, gmat)


def _attn_kernel(rb_ref, bnd_ref, q_ref, qi_ref, wit_ref, k_ref, ki_ref, vt_ref, og_ref,
                 o_ref,
                 s_ref, s16_ref, bias_ref, qpad_ref, qipad_ref, lg_ref, acc_ref, out_ref, *, topk):
    b = pl.program_id(0)
    i = pl.program_id(1)
    tq, tk = ATT_TQ, ATT_TK
    nh, hd = ATTN_HEADS, HEAD_DIM

    t_loc = lax.broadcasted_iota(jnp.int32, (tk, tq), 1)
    s_loc = lax.broadcasted_iota(jnp.int32, (tk, tq), 0)

    @pl.when((b == 0) & (i == 0))
    def _():
        for idx in range(2):
            dist = t_loc - s_loc + idx * tq
            for h in range(nh):
                bias_ref[idx, h] = jnp.full((tk, tq), (rb_ref[0, h] - rb_ref[N_BUCKETS - 1, h]) * LOG2E, F32)

            def fill(jb, carry):
                reached = dist >= bnd_ref[jb]
                for h in range(nh):
                    val = (rb_ref[jb, h] - rb_ref[N_BUCKETS - 1, h]) * LOG2E
                    bias_ref[idx, h] = jnp.where(reached, val, bias_ref[idx, h])
                return carry

            lax.fori_loop(1, N_BUCKETS, fill, 0)

    lane = lax.broadcasted_iota(jnp.int32, (tq, LANES), 1)
    for h in range(nh):
        pair = slice((h // 2) * LANES, (h // 2 + 1) * LANES)
        keep = (lane // hd) == (h % 2)
        qpad_ref[h] = jnp.where(keep, q_ref[:, pair], jnp.zeros((), BF16))
        qipad_ref[h] = jnp.where(keep, qi_ref[:, pair], jnp.zeros((), BF16))

    def idx_dots(c, slot):
        kic = ki_ref[pl.ds(pl.multiple_of(c * tk, tk), tk), :]
        for h in range(nh):
            lg_ref[slot, h] = lax.dot_general(kic, qipad_ref[h], _NT_DIMS, preferred_element_type=F32)

    def idx_reduce(c, slot, carry, diagonal):
        rmin, rmax = carry
        sc = _tree_sum([wit_ref[h:h + 1, :] * jnp.maximum(lg_ref[slot, h], 0.0) for h in range(nh)])
        if diagonal:
            causal = s_loc <= t_loc
            lo_c, hi_c = jnp.where(causal, sc, jnp.inf), jnp.where(causal, sc, -jnp.inf)
            sc = hi_c
        else:
            lo_c, hi_c = sc, sc
        s_ref[c] = sc
        s16_ref[c] = sc.astype(BF16)
        return (jnp.minimum(rmin, jnp.min(lo_c, axis=0, keepdims=True)),
                jnp.maximum(rmax, jnp.max(hi_c, axis=0, keepdims=True)))

    def idx_pair(jj, carry):
        idx_dots(2 * jj + 1, 1)
        carry = idx_reduce(2 * jj, 0, carry, False)
        idx_dots(2 * jj + 2, 0)
        return idx_reduce(2 * jj + 1, 1, carry, False)

    def idx_tail_odd(carry):
        idx_dots(i, 1)
        return idx_reduce(i, 1, idx_reduce(i - 1, 0, carry, False), True)

    idx_dots(0, 0)
    carry = (jnp.full((1, tq), jnp.inf, F32), jnp.full((1, tq), -jnp.inf, F32))
    carry = lax.fori_loop(0, i // 2, idx_pair, carry)
    rmin, rmax = lax.cond((i & 1) == 1, idx_tail_odd, lambda cr: idx_reduce(i, 0, cr, True), carry)

    def count_ge(thr):
        def body(c, accs):
            hit = s_ref[c] >= thr
            accs = list(accs)
            for r in range(tk // SUBLANES):
                a = accs[r % COUNT_ACCS]
                accs[r % COUNT_ACCS] = jnp.where(hit[r * SUBLANES:(r + 1) * SUBLANES], a + 1.0, a)
            return tuple(accs)
        accs = lax.fori_loop(0, i + 1, body,
                             tuple(jnp.zeros((SUBLANES, tq), F32) for _ in range(COUNT_ACCS)))
        return jnp.sum(_tree_sum(list(accs)), axis=0, keepdims=True)

    def count16_ge(thr16):
        def body(c, accs):
            hit = s16_ref[c] >= thr16
            accs = list(accs)
            for r in range(tk // BF16_SUBLANES):
                a = accs[r % COUNT_ACCS]
                accs[r % COUNT_ACCS] = jnp.where(hit[r * BF16_SUBLANES:(r + 1) * BF16_SUBLANES], a + 1, a)
            return tuple(accs)
        accs = lax.fori_loop(0, i + 1, body,
                             tuple(jnp.zeros((BF16_SUBLANES, tq), BF16) for _ in range(COUNT_ACCS)))
        return jnp.sum(_tree_sum(list(accs)).astype(F32), axis=0, keepdims=True)

    int_min = jnp.int32(-2 ** 31)

    def order_key(v):
        bits = pltpu.bitcast(v, jnp.int32)
        return jnp.where(bits < 0, -(bits & jnp.int32(0x7FFFFFFF)), bits)

    def from_order_key(key):
        return pltpu.bitcast(jnp.where(key < 0, (-key) | int_min, key), F32)

    t_glob = (i * tq + lax.broadcasted_iota(jnp.int32, (1, tq), 1)).astype(F32)
    n_causal = t_glob + 1.0
    kf = jnp.minimum(float(topk), n_causal)
    all_sel = n_causal <= kf

    lo16 = order_key(rmin) >> 16
    hi16 = (order_key(rmax.astype(BF16).astype(F32) + 0.0) >> 16) + 1
    for _ in range(BISECT_BF16_STEPS):
        lo_v = from_order_key(lo16 << 16)
        hi_v = from_order_key(hi16 << 16)
        mid_val16 = order_key(lo_v + (hi_v - lo_v) * 0.5) >> 16
        mid16 = jnp.where((mid_val16 > lo16) & (mid_val16 < hi16), mid_val16, (lo16 + hi16) >> 1)
        open_ = (hi16 - lo16) > 1
        cm = count16_ge(from_order_key(mid16 << 16).astype(BF16))
        lo16 = jnp.where(open_ & (cm >= kf), mid16, lo16)
        hi16 = jnp.where(open_ & (cm < kf), mid16, hi16)

    min_normal_key = jnp.int32(0x00800000)

    def snap(key, direction):
        sub = (key > -min_normal_key) & (key < min_normal_key) & (key != 0)
        below = jnp.where(key > 0, 0, -min_normal_key)
        above = jnp.where(key > 0, min_normal_key, 0)
        return jnp.where(sub, {"down": below, "up": above, "zero": jnp.zeros_like(key)}[direction], key)

    active0 = jnp.where(all_sel, 0.0, 1.0)
    thr0 = rmin
    tie0 = jnp.zeros((1, tq), F32)
    lo0 = snap(jnp.maximum((lo16 - 1) << 16, order_key(rmin)), "down")
    hi0 = snap(hi16 << 16, "up")
    fhi0 = count_ge(from_order_key(hi0))
    hif0 = jnp.full((1, tq), jnp.inf, F32)
    need0 = kf

    def b_cond(st):
        return (jnp.max(st[0]) > 0.0) & (st[8] <= BISECT_MAX_STEPS)

    def b_body(st):
        active, lo_key, hi_key, fhi, thr, tie, hif, need, step = st
        lo = from_order_key(lo_key)
        hi = from_order_key(hi_key)
        val_key = snap(order_key(lo + (hi - lo) * 0.5), "zero")
        ord_key = snap((lo_key >> 1) + (hi_key >> 1) + (lo_key & hi_key & 1), "zero")
        use_val = (step < BISECT_VALUE_STEPS) & (val_key > lo_key) & (val_key < hi_key)
        mid_key = jnp.where(use_val, val_key, ord_key)
        mid = from_order_key(mid_key)
        collapsed = (mid_key <= lo_key) | (mid_key >= hi_key) | (step >= BISECT_MAX_STEPS)
        cm = count_ge(mid)
        act = active > 0.0
        live = act & jnp.logical_not(collapsed)
        found = live & (cm == kf)
        go_up = live & (cm > kf)
        go_dn = live & (cm < kf)
        ends_tie = act & collapsed
        thr = jnp.where(found, mid, jnp.where(ends_tie, lo, thr))
        tie = jnp.where(ends_tie, 1.0, tie)
        hif = jnp.where(ends_tie, hi, hif)
        need = jnp.where(ends_tie, kf - fhi, need)
        lo_key = jnp.where(go_up, mid_key, lo_key)
        fhi = jnp.where(go_dn, cm, fhi)
        hi_key = jnp.where(go_dn, mid_key, hi_key)
        active = jnp.where(found | ends_tie, 0.0, active)
        return active, lo_key, hi_key, fhi, thr, tie, hif, need, step + 1

    def b_group(st):
        for _ in range(BISECT_GROUP):
            st = b_body(st)
        return st

    _, _, _, _, thr, tie, hif, need, _ = lax.while_loop(
        b_cond, b_group, (active0, lo0, hi0, fhi0, thr0, tie0, hif0, need0, jnp.int32(0)))

    @pl.when(jnp.max(tie) > 0.0)
    def _():
        tri = jnp.where(lax.broadcasted_iota(jnp.int32, (tk, tk), 1)
                        <= lax.broadcasted_iota(jnp.int32, (tk, tk), 0), 1.0, 0.0).astype(BF16)

        def body(c, seen):
            sc_c = s_ref[c]
            tied = (sc_c >= thr) & (sc_c < hif) & (tie > 0.0)
            rank = jnp.dot(tri, jnp.where(tied, 1.0, 0.0).astype(BF16), preferred_element_type=F32) + seen
            s_ref[c] = jnp.where(tied & (rank > need), -jnp.inf, sc_c)
            return rank[tk - 1:tk, :]

        lax.fori_loop(0, i + 1, body, jnp.zeros((1, tq), F32))

    acc_ref[...] = jnp.zeros(acc_ref.shape, F32)

    def store_logits(c, slot, bias_idx):
        masked = jnp.where(s_ref[c] >= thr, 0.0, NEG_BIG)
        row0 = pl.multiple_of(c * tk, tk)
        for h in range(nh):
            kc = k_ref[pl.ds(row0, tk), (h // 2) * LANES:(h // 2 + 1) * LANES]
            lt = lax.dot_general(kc, qpad_ref[h], _NT_DIMS, preferred_element_type=F32) + masked
            if bias_idx is not None:
                lt = lt + bias_ref[bias_idx, h]
            lg_ref[slot, h] = lt

    def softmax_pv(c, slot, m_all):
        m_out = []
        for h in range(nh):
            m_old = m_all[h]
            m_new = jnp.maximum(m_old, jnp.max(lg_ref[slot, h], axis=0, keepdims=True))
            p = jnp.exp2(lg_ref[slot, h] - m_new).astype(BF16)
            alpha = jnp.exp2(m_old - m_new)
            pv = jnp.dot(vt_ref[c, h * V_SLAB:(h + 1) * V_SLAB, :], p, preferred_element_type=F32)
            acc_ref[h] = alpha * acc_ref[h] + pv
            m_out.append(m_new)
        return tuple(m_out)

    def near_step(m_all):
        store_logits(i - 1, 1, 1)
        return softmax_pv(i, 0, m_all)

    def far_step(j, parity, m_all):
        c = i - 2 - j
        store_logits(c, parity, None)
        return softmax_pv(c + 1, 1 - parity, m_all)

    def far_pair(jj, m_all):
        return far_step(2 * jj + 1, 1, far_step(2 * jj, 0, m_all))

    n_far = jnp.maximum(i - 1, 0)
    m_all = tuple(jnp.full((1, tq), NEG_BIG, F32) for _ in range(nh))
    store_logits(i, 0, 0)
    m_all = lax.cond(i >= 1, near_step, lambda m: m, m_all)
    m_all = lax.fori_loop(0, n_far // 2, far_pair, m_all)
    m_all = lax.cond((n_far & 1) == 1, lambda m: far_step(n_far - 1, 0, m), lambda m: m, m_all)
    lax.cond((i & 1) == 0, lambda m: softmax_pv(0, 0, m), lambda m: softmax_pv(0, 1, m), m_all)

    for h in range(nh):
        o = acc_ref[h, :hd, :] / acc_ref[h, hd:hd + 1, :]
        ms = jnp.mean(o * o, axis=0, keepdims=True)
        out_ref[h * hd:(h + 1) * hd, :] = o * lax.rsqrt(ms + EPS)
    o_ref[...] = (out_ref[...].T * og_ref[...]).astype(BF16)


def _attn_call(rel_bias, bounds, q, qi, wit, k, ki, vt, og, topk):
    bsz, seq, aw = q.shape
    tq, tk = ATT_TQ, ATT_TK
    nck = seq // tk
    blk_q = lambda b, i: (b, i, 0)
    whole = lambda b, i: (b, 0, 0)
    smem = pl.BlockSpec(memory_space=pltpu.SMEM)
    return pl.pallas_call(
        functools.partial(_attn_kernel, topk=topk),
        out_shape=jax.ShapeDtypeStruct((bsz, seq, aw), BF16),
        grid=(bsz, seq // tq),
        in_specs=[
            smem, smem,
            pl.BlockSpec((None, tq, aw), blk_q),
            pl.BlockSpec((None, tq, aw), blk_q),
            pl.BlockSpec((None, IDX_HEADS, tq), lambda b, i: (b, 0, i)),
            pl.BlockSpec((None, seq, aw), whole),
            pl.BlockSpec((None, seq, LANES), whole),
            pl.BlockSpec((None, nck, ATTN_HEADS * V_SLAB, tk), lambda b, i: (b, 0, 0, 0)),
            pl.BlockSpec(og.shape, lambda b, i: (0, 0)),
        ],
        out_specs=pl.BlockSpec((None, tq, aw), blk_q),
        scratch_shapes=[
            pltpu.VMEM((nck, tk, tq), F32),
            pltpu.VMEM((nck, tk, tq), BF16),
            pltpu.VMEM((2, ATTN_HEADS, tk, tq), F32),
            pltpu.VMEM((ATTN_HEADS, tq, LANES), BF16),
            pltpu.VMEM((IDX_HEADS, tq, LANES), BF16),
            pltpu.VMEM((2, ATTN_HEADS, tk, tq), F32),
            pltpu.VMEM((ATTN_HEADS, V_SLAB, tq), F32),
            pltpu.VMEM((aw, tq), F32),
        ],
        compiler_params=pltpu.CompilerParams(dimension_semantics=("arbitrary", "arbitrary"),
                                             vmem_limit_bytes=VMEM_LIMIT_BYTES),
        name="dsa_attention",
    )(rel_bias, bounds, q, qi, wit, k, ki, vt, og)


def _post_kernel(an_ref, cn_ref, x_ref, mod_ref, n2_ref, woa_ref, woc_ref, wr_ref, br_ref,
                 x1_ref, h2_ref, comb_ref, cnt_ref):
    mix = (jnp.dot(an_ref[...], woa_ref[...], preferred_element_type=F32)
           + jnp.dot(cn_ref[...], woc_ref[...], preferred_element_type=F32))
    x1 = x_ref[...] + mod_ref[2:3, :] * mix
    x1_ref[...] = x1
    ms = jnp.mean(x1 * x1, axis=-1, keepdims=True)
    h2 = x1 * lax.rsqrt(ms + EPS) * n2_ref[...] * (1.0 + mod_ref[4:5, :]) + mod_ref[3:4, :]
    h2b = h2.astype(BF16)
    h2_ref[...] = h2b

    logits = jnp.dot(h2b, wr_ref[...], preferred_element_type=F32) + br_ref[...]
    lane = lax.broadcasted_iota(jnp.int32, logits.shape, 1)
    lane_f = lane.astype(F32)
    far = float(LANES)
    is_g = (lane >= N_EXPERTS) & (lane < N_EXPERTS + N_GROUPS)
    gl = jnp.where(is_g, logits, -jnp.inf)
    gmax = jnp.max(gl, axis=-1, keepdims=True)
    g_sel = jnp.min(jnp.where(is_g & (gl == gmax), lane_f, far), axis=-1, keepdims=True) - float(N_EXPERTS)
    p_g = 1.0 / jnp.sum(jnp.exp(gl - gmax), axis=-1, keepdims=True)

    in_grp = (lane < N_EXPERTS) & ((lane // EXPERTS_PER_GROUP).astype(F32) == g_sel)
    e1 = jnp.where(in_grp, logits, -jnp.inf)
    l1 = jnp.max(e1, axis=-1, keepdims=True)
    i1 = jnp.min(jnp.where(in_grp & (e1 == l1), lane_f, far), axis=-1, keepdims=True)
    rest = in_grp & (lane_f != i1)
    e2 = jnp.where(rest, logits, -jnp.inf)
    l2 = jnp.max(e2, axis=-1, keepdims=True)
    i2 = jnp.min(jnp.where(rest & (e2 == l2), lane_f, far), axis=-1, keepdims=True)
    r = jnp.exp(l2 - l1)
    w1 = 1.0 / (1.0 + r)
    w2 = r / (1.0 + r)
    comb = jnp.where(lane_f == i1, p_g * w1, 0.0) + jnp.where(lane_f == i2, p_g * w2, 0.0)
    comb_ref[...] = comb
    cnt = jnp.sum(jnp.where(comb != 0.0, 1.0, 0.0), axis=0, keepdims=True)
    cnt_ref[...] = jnp.broadcast_to(cnt, cnt_ref.shape)


def _post_call(an, cn, x, mod, n2, woa, woc, wr, br):
    bsz, seq, d = x.shape
    tm = POST_TM
    tok = lambda b, j: (b, j, 0)
    const = lambda b, j: (0, 0)
    return pl.pallas_call(
        _post_kernel,
        out_shape=(jax.ShapeDtypeStruct((bsz, seq, d), F32),
                   jax.ShapeDtypeStruct((bsz, seq, d), BF16),
                   jax.ShapeDtypeStruct((bsz, seq, LANES), F32),
                   jax.ShapeDtypeStruct((bsz, seq // tm, SUBLANES, LANES), F32)),
        grid=(bsz, seq // tm),
        in_specs=[
            pl.BlockSpec((None, tm, ATTN_WIDTH), tok),
            pl.BlockSpec((None, tm, CONV_WIDTH), tok),
            pl.BlockSpec((None, tm, d), tok),
            pl.BlockSpec((None, 6, d), lambda b, j: (b, 0, 0)),
            pl.BlockSpec(n2.shape, const),
            pl.BlockSpec(woa.shape, const),
            pl.BlockSpec(woc.shape, const),
            pl.BlockSpec(wr.shape, const),
            pl.BlockSpec(br.shape, const),
        ],
        out_specs=(pl.BlockSpec((None, tm, d), tok),
                   pl.BlockSpec((None, tm, d), tok),
                   pl.BlockSpec((None, tm, LANES), tok),
                   pl.BlockSpec((None, None, SUBLANES, LANES), lambda b, j: (b, j, 0, 0))),
        compiler_params=pltpu.CompilerParams(dimension_semantics=("arbitrary", "arbitrary"),
                                             vmem_limit_bytes=VMEM_LIMIT_BYTES),
        name="post_router",
    )(an, cn, x, mod, n2, woa, woc, wr, br)


def _strict_tri(n, lower):
    r = lax.broadcasted_iota(jnp.int32, (n, n), 0)
    c = lax.broadcasted_iota(jnp.int32, (n, n), 1)
    return jnp.where((c < r) if lower else (r < c), 1.0, 0.0).astype(BF16)


def _moe_tile_copies(nloc_ref, gtile_ref, blk, local_ref, global_ref, sem, to_global, wait):
    tile = MOE_TILE

    def per_tile(lt, c):
        loc = local_ref.at[pl.ds(pl.multiple_of(lt * tile, tile), tile), :]
        glo = global_ref.at[pl.ds(pl.multiple_of(gtile_ref[blk, lt] * tile, tile), tile), :]
        cp = pltpu.make_async_copy(loc, glo, sem) if to_global else pltpu.make_async_copy(glo, loc, sem)
        if wait:
            cp.wait()
        else:
            cp.start()
        return c

    lax.fori_loop(0, nloc_ref[blk], per_tile, 0)


def _moe_gather_kernel(nloc_ref, gtile_ref, padstart_ref, pad_ref,
                       h2_ref, comb_ref,
                       col_ref, xg_hbm,
                       xg_ref, row_ref, zero_ref, sem):
    blk = pl.program_id(0)
    nb = h2_ref.shape[0]
    tile, chunk = MOE_TILE, MOE_CHUNK
    lane = lax.broadcasted_iota(jnp.int32, (nb, LANES), 1)

    comb = comb_ref[...]
    assigned = comb != 0.0
    a_f = jnp.where(assigned, 1.0, 0.0)
    rank = jnp.dot(_strict_tri(nb, True), a_f.astype(BF16), preferred_element_type=F32)
    cnt = rank[nb - 1:nb, :] + a_f[nb - 1:nb, :]
    ntile = jnp.floor((cnt + float(tile - 1)) * (1.0 / tile))
    first = jnp.dot(jnp.broadcast_to(ntile, (SUBLANES, LANES)).astype(BF16), _strict_tri(LANES, False),
                    preferred_element_type=F32)[0:1, :]
    pos = first * float(tile) + rank
    pos1 = jnp.min(jnp.where(assigned, pos, 1e9), axis=1, keepdims=True)
    pos2 = jnp.max(jnp.where(assigned, pos, -1.0), axis=1, keepdims=True)
    pos2 = jnp.where(pos2 == pos1, -1.0, pos2)
    cw1 = jnp.sum(jnp.where(assigned & (pos == pos1), comb, 0.0), axis=1, keepdims=True)
    cw2 = jnp.sum(jnp.where(assigned & (pos == pos2), comb, 0.0), axis=1, keepdims=True)
    info = jnp.where(lane == 0, pos1, jnp.where(lane == 1, pos2, jnp.where(lane == 2, cw1,
                     jnp.where(lane == 3, cw2, 0.0))))
    col_ref[...] = info
    row_ref[...] = info.T

    n_chunks = (nloc_ref[blk] * tile + (chunk - 1)) // chunk
    p1 = row_ref[0:1, :].astype(jnp.int32)
    p2 = row_ref[1:2, :].astype(jnp.int32)
    sub = lax.broadcasted_iota(jnp.int32, (chunk, nb), 0)

    def gather(c, carry):
        p = sub + c * chunk
        sel = jnp.where((p == p1) | (p == p2), 1.0, 0.0).astype(BF16)
        r0 = pl.multiple_of(c * chunk, chunk)
        xg_ref[pl.ds(r0, chunk), :] = jnp.dot(sel, h2_ref[...], preferred_element_type=F32).astype(BF16)
        return carry

    @pl.when(blk > 0)
    def _():
        _moe_tile_copies(nloc_ref, gtile_ref, blk - 1, xg_ref, xg_hbm, sem, True, True)

    lax.fori_loop(0, n_chunks, gather, 0)

    _moe_tile_copies(nloc_ref, gtile_ref, blk, xg_ref, xg_hbm, sem, True, False)

    is_last = blk == pl.num_programs(0) - 1

    def pad_copies(wait):
        def per_expert(x, carry):
            g0 = padstart_ref[x]

            def per_tile(j, c):
                dst = xg_hbm.at[pl.ds(pl.multiple_of((g0 + j) * tile, tile), tile), :]
                cp = pltpu.make_async_copy(zero_ref, dst, sem)
                if wait:
                    cp.wait()
                else:
                    cp.start()
                return c

            lax.fori_loop(0, pad_ref[x], per_tile, 0)
            return carry

        lax.fori_loop(0, N_EXPERTS, per_expert, 0)

    @pl.when(is_last)
    def _():
        zero_ref[...] = jnp.zeros(zero_ref.shape, BF16)
        pad_copies(False)
        _moe_tile_copies(nloc_ref, gtile_ref, blk, xg_ref, xg_hbm, sem, True, True)
        pad_copies(True)


def _moe_ffn_kernel(texp_ref, nt_ref, x_ref, wg_ref, wu_ref, wd_ref, y_ref):
    @pl.when(pl.program_id(0) < nt_ref[0])
    def _():
        x = x_ref[...]
        a = jnp.dot(x, wg_ref[...].astype(BF16), preferred_element_type=F32)
        up = jnp.dot(x, wu_ref[...].astype(BF16), preferred_element_type=F32)
        hid = ((a * jax.nn.sigmoid(a)) * up).astype(BF16)
        y_ref[...] = jnp.dot(hid, wd_ref[...].astype(BF16), preferred_element_type=F32).astype(BF16)


def _moe_scatter_kernel(nloc_ref, gtile_ref,
                        col_ref, x1_ref, mod_ref, y_hbm,
                        o_ref,
                        y_ref, sem):
    blk = pl.program_id(0)
    nb = x1_ref.shape[0]
    tile, chunk = MOE_TILE, MOE_CHUNK
    slot = blk & 1

    def copies(b, s, wait):
        _moe_tile_copies(nloc_ref, gtile_ref, b, y_ref.at[s], y_hbm, sem.at[s], False, wait)

    @pl.when(blk == 0)
    def _():
        copies(0, 0, False)

    @pl.when(blk + 1 < pl.num_programs(0))
    def _():
        copies(blk + 1, 1 - slot, False)

    total = nloc_ref[blk]
    n_chunks = (total * tile + (chunk - 1)) // chunk
    max_chunks = y_ref.shape[1] // chunk
    usual = n_chunks <= MOE_USUAL_CHUNKS
    n_static = jnp.where(usual, MOE_USUAL_CHUNKS, max_chunks)

    def clear(t, carry):
        y_ref[slot, pl.ds(pl.multiple_of(t * tile, tile), tile), :] = jnp.zeros((tile, y_ref.shape[2]), BF16)
        return carry

    lax.fori_loop(total, n_static * (chunk // tile), clear, 0)

    p1 = col_ref[:, 0:1].astype(jnp.int32)
    p2 = col_ref[:, 1:2].astype(jnp.int32)
    cw1 = col_ref[:, 2:3]
    cw2 = col_ref[:, 3:4]
    gate = mod_ref[5:6, :]
    lane_c = lax.broadcasted_iota(jnp.int32, (nb, chunk), 1)
    copies(blk, slot, True)

    def scatter(n_unrolled):
        acc = None
        for c in range(n_unrolled):
            p = lane_c + c * chunk
            w = (jnp.where(p == p1, cw1, 0.0) + jnp.where(p == p2, cw2, 0.0)).astype(BF16)
            part = jnp.dot(w, y_ref[slot, c * chunk:(c + 1) * chunk, :], preferred_element_type=F32)
            acc = part if acc is None else acc + part
        o_ref[...] = x1_ref[...] + gate * acc

    lax.cond(usual, lambda: scatter(MOE_USUAL_CHUNKS), lambda: scatter(max_chunks))


def _moe_call(h2, comb, cnt_tiles, x1, mod, w_gate, w_up, w_down):
    bsz, seq, d = x1.shape
    nb, tile, ftm = MOE_TM, MOE_TILE, MOE_FFN_TM
    n_tok = bsz * seq
    n_blk = n_tok // nb
    region = ftm // tile
    rows_local = -(-(2 * nb + N_EXPERTS * tile) // MOE_CHUNK) * MOE_CHUNK
    tiles_global = (2 * n_tok) // tile + n_blk * N_EXPERTS + N_EXPERTS * (region - 1)
    n_ffn_max = -(-tiles_global // region)
    rows_global = n_ffn_max * ftm

    cnt = cnt_tiles[:, :, 0, :N_EXPERTS].reshape(n_blk, nb // POST_TM, N_EXPERTS).sum(axis=1).astype(jnp.int32)
    ntile = (cnt + (tile - 1)) // tile
    lfirst = jnp.cumsum(ntile, axis=1) - ntile
    tot = ntile.sum(axis=0)
    ptot = (tot + (region - 1)) // region * region
    ebase = jnp.cumsum(ptot) - ptot
    gfirst = ebase[None, :] + jnp.cumsum(ntile, axis=0) - ntile
    pad = ptot - tot
    n_ffn = (ptot.sum() // region).reshape(1)
    ends = jnp.cumsum(ptot) // region
    texp = jnp.minimum((jnp.arange(n_ffn_max, dtype=jnp.int32)[:, None] >= ends[None, :]).sum(axis=1),
                       N_EXPERTS - 1).astype(jnp.int32)
    nloc = ntile.sum(axis=1).astype(jnp.int32)
    lt = jnp.arange(rows_local // tile, dtype=jnp.int32)[None, :, None]
    in_seg = (lt >= lfirst[:, None, :]) & (lt < (lfirst + ntile)[:, None, :])
    gtile = (jnp.where(in_seg, (gfirst - lfirst)[:, None, :], 0).sum(axis=2) + lt[:, :, 0]).astype(jnp.int32)
    padstart = (gfirst[-1] + ntile[-1]).astype(jnp.int32)

    h2f = h2.reshape(n_tok, d)
    combf = comb.reshape(n_tok, LANES)
    col, xg = pl.pallas_call(
        _moe_gather_kernel,
        out_shape=(jax.ShapeDtypeStruct((n_tok, LANES), F32),
                   jax.ShapeDtypeStruct((rows_global, d), BF16)),
        grid_spec=pltpu.PrefetchScalarGridSpec(
            num_scalar_prefetch=4,
            grid=(n_blk,),
            in_specs=[pl.BlockSpec((nb, d), lambda j, *_: (j, 0)),
                      pl.BlockSpec((nb, LANES), lambda j, *_: (j, 0))],
            out_specs=(pl.BlockSpec((nb, LANES), lambda j, *_: (j, 0)),
                       pl.BlockSpec(memory_space=pl.ANY)),
            scratch_shapes=[
                pltpu.VMEM((rows_local, d), BF16),
                pltpu.VMEM((LANES, nb), F32),
                pltpu.VMEM((tile, d), BF16),
                pltpu.SemaphoreType.DMA,
            ]),
        compiler_params=pltpu.CompilerParams(dimension_semantics=("arbitrary",),
                                             vmem_limit_bytes=VMEM_LIMIT_BYTES),
        name="moe_gather",
    )(nloc, gtile, padstart, pad, h2f, combf)

    last = lambda t, te, nt: jnp.minimum(t, nt[0] - 1)
    y = pl.pallas_call(
        _moe_ffn_kernel,
        out_shape=jax.ShapeDtypeStruct((rows_global, d), BF16),
        grid_spec=pltpu.PrefetchScalarGridSpec(
            num_scalar_prefetch=2,
            grid=(n_ffn_max,),
            in_specs=[pl.BlockSpec((ftm, d), lambda t, te, nt: (last(t, te, nt), 0)),
                      pl.BlockSpec((None, d, EXPERT_FF), lambda t, te, nt: (te[last(t, te, nt)], 0, 0)),
                      pl.BlockSpec((None, d, EXPERT_FF), lambda t, te, nt: (te[last(t, te, nt)], 0, 0)),
                      pl.BlockSpec((None, EXPERT_FF, d), lambda t, te, nt: (te[last(t, te, nt)], 0, 0))],
            out_specs=pl.BlockSpec((ftm, d), lambda t, te, nt: (last(t, te, nt), 0))),
        compiler_params=pltpu.CompilerParams(dimension_semantics=("arbitrary",),
                                             vmem_limit_bytes=VMEM_LIMIT_BYTES),
        name="moe_ffn",
    )(texp, n_ffn, xg, w_gate, w_up, w_down)

    out = pl.pallas_call(
        _moe_scatter_kernel,
        out_shape=jax.ShapeDtypeStruct((n_tok, d), F32),
        grid_spec=pltpu.PrefetchScalarGridSpec(
            num_scalar_prefetch=2,
            grid=(n_blk,),
            in_specs=[pl.BlockSpec((nb, LANES), lambda j, *_: (j, 0)),
                      pl.BlockSpec((nb, d), lambda j, *_: (j, 0)),
                      pl.BlockSpec((None, 6, d), lambda j, *_: ((j * nb) // seq, 0, 0)),
                      pl.BlockSpec(memory_space=pl.ANY)],
            out_specs=pl.BlockSpec((nb, d), lambda j, *_: (j, 0)),
            scratch_shapes=[pltpu.VMEM((2, rows_local, d), BF16),
                            pltpu.SemaphoreType.DMA((2,))]),
        compiler_params=pltpu.CompilerParams(dimension_semantics=("arbitrary",),
                                             vmem_limit_bytes=VMEM_LIMIT_BYTES),
        name="moe_scatter",
    )(nloc, gtile, col, x1.reshape(n_tok, d), mod, y)
    return out.reshape(bsz, seq, d)


def _layer(x, mod, rel_bias, norm1, w_in, q_norm, k_norm, conv_w, attn_out_norm, conv_out_norm, w_out,
           norm2, w_group_router, b_group_router, w_expert_router, b_expert_router, w_gate, w_up, w_down):
    bsz, seq, d = x.shape
    aw = ATTN_WIDTH
    topk = min(TOPK_MAX, seq // 4)

    offs = np.cumsum([0, aw, aw, aw, IDX_HEADS * IDX_DIM, IDX_DIM, IDX_HEADS, CONV_WIDTH, CONV_WIDTH, CONV_WIDTH])
    col = lambda n: w_in[:, int(offs[n]):int(offs[n + 1])]
    wm = jnp.concatenate([col(0), col(1), col(3), col(6), col(7), col(8)], axis=1).astype(BF16)
    wvt = col(2).T.astype(BF16)
    wki = jnp.concatenate([col(4), col(4)], axis=1).astype(BF16)
    wwit = col(5).T.astype(BF16)
    qg = (jnp.tile(q_norm, ATTN_HEADS) * ((HEAD_DIM ** -0.5) * LOG2E))[None, :]
    kg = jnp.tile(k_norm, ATTN_HEADS)[None, :]
    grp = np.arange(aw) // CONV_GROUP_DIM
    gmat = jnp.asarray((grp[:, None] == grp[None, :]).astype(np.float32) / CONV_GROUP_DIM, dtype=BF16)

    q, k, vt, qi, ki, wit, cn = _pre_call(
        x, mod, norm1[None, :], wm, wvt, wki, wwit, qg, kg, conv_w, conv_out_norm.reshape(1, -1), gmat)

    bounds = jnp.asarray(_bucket_boundaries())
    an = _attn_call(rel_bias, bounds, q, qi, wit, k, ki, vt, attn_out_norm.reshape(1, -1), topk)

    wr = jnp.concatenate([w_expert_router, w_group_router,
                          jnp.zeros((d, LANES - N_EXPERTS - N_GROUPS), F32)], axis=1).astype(BF16)
    br = jnp.concatenate([b_expert_router, b_group_router,
                          jnp.zeros((LANES - N_EXPERTS - N_GROUPS,), F32)])[None, :]
    x1, h2, comb, cnt_tiles = _post_call(an, cn, x, mod, norm2[None, :], w_out[:aw].astype(BF16),
                                         w_out[aw:].astype(BF16), wr, br)

    return _moe_call(h2, comb, cnt_tiles, x1, mod, w_gate, w_up, w_down)


def kernel(x, c, rel_bias, w_ada, b_ada, norm1, w_in, q_norm, k_norm, conv_w, attn_out_norm, conv_out_norm,
           w_out, norm2, w_group_router, b_group_router, w_expert_router, b_expert_router, w_gate, w_up,
           w_down):
    bsz, seq, d = x.shape
    assert d == D_MODEL and seq % max(PRE_TM, POST_TM, MOE_TM) == 0 and ATT_TQ == ATT_TK
    depth = w_ada.shape[0]
    for l in range(depth):
        mod = _mod_call(c, w_ada[l], b_ada[l][None, :]).reshape(bsz, 6, d)
        x = _layer(x, mod, rel_bias, norm1[l], w_in[l], q_norm[l], k_norm[l], conv_w[l], attn_out_norm[l],
                   conv_out_norm[l], w_out[l], norm2[l], w_group_router[l], b_group_router[l],
                   w_expert_router[l], b_expert_router[l], w_gate[l], w_up[l], w_down[l])
    return x
```

```python
import functools
import math

import jax
import jax.numpy as jnp
import numpy as np
from jax import lax
from jax.experimental import pallas as pl
from jax.experimental.pallas import tpu as pltpu

F32 = jnp.float32
BF16 = jnp.bfloat16

D_MODEL = 1024
HEAD_DIM = 64
ATTN_HEADS = 8
ATTN_WIDTH = ATTN_HEADS * HEAD_DIM
CONV_WIDTH = D_MODEL - ATTN_WIDTH
CONV_GROUP_DIM = 64
CONV_K = 3
IDX_HEADS = 8
IDX_DIM = 64
TOPK_MAX = 256
IDX_SCALE = (IDX_DIM ** -0.5) * (IDX_HEADS ** -0.5)
N_BUCKETS = 32
MAX_DISTANCE = 128
N_GROUPS = 4
EXPERTS_PER_GROUP = 8
N_EXPERTS = N_GROUPS * EXPERTS_PER_GROUP
EXPERT_FF = 256
EPS = 1e-6
LOG2E = 1.4426950408889634
NEG_BIG = -1e30
COUNT_ACCS = 4
BISECT_GROUP = 4
BISECT_BF16_STEPS = 10
BISECT_VALUE_STEPS = 4
BISECT_MAX_STEPS = 64

LANES = 128
SUBLANES = 8
BF16_SUBLANES = 16
V_SLAB = HEAD_DIM + BF16_SUBLANES
VMEM_LIMIT_BYTES = 56 * 1024 * 1024

PRE_TM = 512
ATT_TQ = 256
ATT_TK = 256
POST_TM = 512
MOE_TM = 512
MOE_TILE = 32
MOE_CHUNK = 512
MOE_USUAL_CHUNKS = 3
MOE_FFN_TM = 1024
MOD_TN = 1536

_NT_DIMS = (((1,), (1,)), ((), ()))


def _tree_sum(parts):
    while len(parts) > 1:
        nxt = [parts[j] + parts[j + 1] for j in range(0, len(parts) - 1, 2)]
        if len(parts) % 2:
            nxt.append(parts[-1])
        parts = nxt
    return parts[0]


def _bucket_boundaries():
    max_exact = N_BUCKETS // 2
    d = np.arange(0, 4 * MAX_DISTANCE, dtype=np.int64)
    nf = np.maximum(d, 1).astype(np.float32)
    large = max_exact + (np.log(nf / np.float32(max_exact)) / np.float32(math.log(MAX_DISTANCE / max_exact))
                         * np.float32(N_BUCKETS - max_exact)).astype(np.int32)
    large = np.minimum(large, N_BUCKETS - 1)
    bucket = np.where(d < max_exact, d, large)
    assert np.all(np.diff(bucket) >= 0) and bucket[-1] == N_BUCKETS - 1
    bounds = [int(np.argmax(bucket >= j)) for j in range(1, N_BUCKETS)]
    return np.asarray([0] + bounds, dtype=np.int32)


def _mod_kernel(c_ref, w_ref, b_ref, o_ref):
    c = c_ref[...]
    act = c * jax.nn.sigmoid(c)
    o_ref[...] = jnp.dot(act, w_ref[...], preferred_element_type=F32,
                         precision=lax.Precision.HIGHEST) + b_ref[...]


def _mod_call(c, w_ada, b_ada):
    bsz, d = c.shape
    n = w_ada.shape[1]
    return pl.pallas_call(
        _mod_kernel,
        out_shape=jax.ShapeDtypeStruct((bsz, n), F32),
        grid=(n // MOD_TN,),
        in_specs=[pl.BlockSpec((bsz, d), lambda j: (0, 0)),
                  pl.BlockSpec((d, MOD_TN), lambda j: (0, j)),
                  pl.BlockSpec((1, MOD_TN), lambda j: (0, j))],
        out_specs=pl.BlockSpec((bsz, MOD_TN), lambda j: (0, j)),
        compiler_params=pltpu.CompilerParams(dimension_semantics=("arbitrary",),
                                             vmem_limit_bytes=VMEM_LIMIT_BYTES),
        name="adaln_mod",
    )(c, w_ada, b_ada)


def _group_rms(y, g_ref):
    ms = jnp.dot((y * y).astype(BF16), g_ref[...], preferred_element_type=F32)
    return y * lax.rsqrt(ms + EPS)


def _pre_kernel(x_ref, mod_ref, n1_ref, wm_ref, wvt_ref, wki_ref, wwit_ref, qg_ref, kg_ref,
                cw_ref, cg_ref, g_ref,
                q_ref, k_ref, vt_ref, qi_ref, ki_ref, wit_ref, cn_ref, carry_ref):
    j = pl.program_id(1)
    tm = x_ref.shape[0]
    aw = ATTN_WIDTH

    x = x_ref[...]
    ms = jnp.mean(x * x, axis=-1, keepdims=True)
    y = x * lax.rsqrt(ms + EPS) * n1_ref[...]
    h = y * (1.0 + mod_ref[1:2, :]) + mod_ref[0:1, :]
    hb = h.astype(BF16)

    def proj(lo):
        return jnp.dot(hb, wm_ref[:, lo:lo + aw], preferred_element_type=F32)

    q = _group_rms(proj(0), g_ref) * qg_ref[...]
    q_ref[...] = q.astype(BF16)
    k = _group_rms(proj(aw), g_ref) * kg_ref[...]
    k_ref[...] = k.astype(BF16)

    vt = lax.dot_general(wvt_ref[...], hb, _NT_DIMS, preferred_element_type=F32).astype(BF16)
    ones = jnp.ones((BF16_SUBLANES, ATT_TK), BF16)
    for cc in range(tm // ATT_TK):
        for hh in range(ATTN_HEADS):
            vt_ref[cc, hh * V_SLAB:hh * V_SLAB + HEAD_DIM, :] = (
                vt[hh * HEAD_DIM:(hh + 1) * HEAD_DIM, cc * ATT_TK:(cc + 1) * ATT_TK])
            vt_ref[cc, hh * V_SLAB + HEAD_DIM:(hh + 1) * V_SLAB, :] = ones

    qi_ref[...] = proj(2 * aw).astype(BF16)
    ki_ref[...] = jnp.dot(hb, wki_ref[...], preferred_element_type=F32).astype(BF16)
    wit_ref[...] = lax.dot_general(wwit_ref[...], hb, _NT_DIMS, preferred_element_type=F32) * IDX_SCALE

    gate_b = proj(3 * aw)
    z = proj(4 * aw) * proj(5 * aw)

    @pl.when(j == 0)
    def _():
        carry_ref[...] = jnp.zeros_like(carry_ref)

    prev = carry_ref[...]
    row = lax.broadcasted_iota(jnp.int32, z.shape, 0)
    z1 = jnp.where(row == 0, prev[SUBLANES - 1:SUBLANES, :], pltpu.roll(z, 1, 0))
    z2 = pltpu.roll(z, 2, 0)
    z2 = jnp.where(row == 0, prev[SUBLANES - 2:SUBLANES - 1, :], z2)
    z2 = jnp.where(row == 1, prev[SUBLANES - 1:SUBLANES, :], z2)
    carry_ref[...] = z[tm - SUBLANES:, :]
    conv = cw_ref[2:3, :] * z + cw_ref[1:2, :] * z1 + cw_ref[0:1, :] * z2
    yc = gate_b * conv
    cn_ref[...] = (_group_rms(yc, g_ref) * cg_ref[...]).astype(BF16)


def _pre_call(x, mod, n1, wm, wvt, wki, wwit, qg, kg, cw, cg, gmat):
    bsz, seq, d = x.shape
    tm = PRE_TM
    nck = tm // ATT_TK
    aw = ATTN_WIDTH
    const = lambda b, j: (0, 0)
    tok = lambda b, j: (b, j, 0)
    out_shape = (
        jax.ShapeDtypeStruct((bsz, seq, aw), BF16),
        jax.ShapeDtypeStruct((bsz, seq, aw), BF16),
        jax.ShapeDtypeStruct((bsz, seq // ATT_TK, ATTN_HEADS * V_SLAB, ATT_TK), BF16),
        jax.ShapeDtypeStruct((bsz, seq, aw), BF16),
        jax.ShapeDtypeStruct((bsz, seq, LANES), BF16),
        jax.ShapeDtypeStruct((bsz, IDX_HEADS, seq), F32),
        jax.ShapeDtypeStruct((bsz, seq, CONV_WIDTH), BF16),
    )
    out_specs = (
        pl.BlockSpec((None, tm, aw), tok),
        pl.BlockSpec((None, tm, aw), tok),
        pl.BlockSpec((None, nck, ATTN_HEADS * V_SLAB, ATT_TK), lambda b, j: (b, j, 0, 0)),
        pl.BlockSpec((None, tm, aw), tok),
        pl.BlockSpec((None, tm, LANES), tok),
        pl.BlockSpec((None, IDX_HEADS, tm), lambda b, j: (b, 0, j)),
        pl.BlockSpec((None, tm, CONV_WIDTH), tok),
    )
    in_specs = [
        pl.BlockSpec((None, tm, d), tok),
        pl.BlockSpec((None, 6, d), lambda b, j: (b, 0, 0)),
        pl.BlockSpec(n1.shape, const),
        pl.BlockSpec(wm.shape, const),
        pl.BlockSpec(wvt.shape, const),
        pl.BlockSpec(wki.shape, const),
        pl.BlockSpec(wwit.shape, const),
        pl.BlockSpec(qg.shape, const),
        pl.BlockSpec(kg.shape, const),
        pl.BlockSpec(cw.shape, const),
        pl.BlockSpec(cg.shape, const),
        pl.BlockSpec(gmat.shape, const),
    ]
    return pl.pallas_call(
        _pre_kernel,
        out_shape=out_shape,
        grid=(bsz, seq // tm),
        in_specs=in_specs,
        out_specs=out_specs,
        scratch_shapes=[pltpu.VMEM((SUBLANES, CONV_WIDTH), F32)],
        compiler_params=pltpu.CompilerParams(dimension_semantics=("arbitrary", "arbitrary"),
                                             vmem_limit_bytes=VMEM_LIMIT_BYTES),
        name="pre_proj",
    )(x, mod, n1, wm, wvt, wki, wwit, qg, kg, cw, cg, gmat)


def _attn_kernel(rb_ref, bnd_ref, q_ref, qi_ref, wit_ref, k_ref, ki_ref, vt_ref, og_ref,
                 o_ref,
                 s_ref, s16_ref, bias_ref, qpad_ref, qipad_ref, lg_ref, acc_ref, out_ref, *, topk):
    b = pl.program_id(0)
    i = pl.program_id(1)
    tq, tk = ATT_TQ, ATT_TK
    nh, hd = ATTN_HEADS, HEAD_DIM

    t_loc = lax.broadcasted_iota(jnp.int32, (tk, tq), 1)
    s_loc = lax.broadcasted_iota(jnp.int32, (tk, tq), 0)

    @pl.when((b == 0) & (i == 0))
    def _():
        for idx in range(2):
            dist = t_loc - s_loc + idx * tq
            for h in range(nh):
                bias_ref[idx, h] = jnp.full((tk, tq), (rb_ref[0, h] - rb_ref[N_BUCKETS - 1, h]) * LOG2E, F32)

            def fill(jb, carry):
                reached = dist >= bnd_ref[jb]
                for h in range(nh):
                    val = (rb_ref[jb, h] - rb_ref[N_BUCKETS - 1, h]) * LOG2E
                    bias_ref[idx, h] = jnp.where(reached, val, bias_ref[idx, h])
                return carry

            lax.fori_loop(1, N_BUCKETS, fill, 0)

    lane = lax.broadcasted_iota(jnp.int32, (tq, LANES), 1)
    for h in range(nh):
        pair = slice((h // 2) * LANES, (h // 2 + 1) * LANES)
        keep = (lane // hd) == (h % 2)
        qpad_ref[h] = jnp.where(keep, q_ref[:, pair], jnp.zeros((), BF16))
        qipad_ref[h] = jnp.where(keep, qi_ref[:, pair], jnp.zeros((), BF16))

    def idx_dots(c, slot):
        kic = ki_ref[pl.ds(pl.multiple_of(c * tk, tk), tk), :]
        for h in range(nh):
            lg_ref[slot, h] = lax.dot_general(kic, qipad_ref[h], _NT_DIMS, preferred_element_type=F32)

    def idx_reduce(c, slot, carry, diagonal):
        rmin, rmax = carry
        sc = _tree_sum([wit_ref[h:h + 1, :] * jnp.maximum(lg_ref[slot, h], 0.0) for h in range(nh)])
        if diagonal:
            causal = s_loc <= t_loc
            lo_c, hi_c = jnp.where(causal, sc, jnp.inf), jnp.where(causal, sc, -jnp.inf)
            sc = hi_c
        else:
            lo_c, hi_c = sc, sc
        s_ref[c] = sc
        s16_ref[c] = sc.astype(BF16)
        return (jnp.minimum(rmin, jnp.min(lo_c, axis=0, keepdims=True)),
                jnp.maximum(rmax, jnp.max(hi_c, axis=0, keepdims=True)))

    def idx_pair(jj, carry):
        idx_dots(2 * jj + 1, 1)
        carry = idx_reduce(2 * jj, 0, carry, False)
        idx_dots(2 * jj + 2, 0)
        return idx_reduce(2 * jj + 1, 1, carry, False)

    def idx_tail_odd(carry):
        idx_dots(i, 1)
        return idx_reduce(i, 1, idx_reduce(i - 1, 0, carry, False), True)

    idx_dots(0, 0)
    carry = (jnp.full((1, tq), jnp.inf, F32), jnp.full((1, tq), -jnp.inf, F32))
    carry = lax.fori_loop(0, i // 2, idx_pair, carry)
    rmin, rmax = lax.cond((i & 1) == 1, idx_tail_odd, lambda cr: idx_reduce(i, 0, cr, True), carry)

    def count_ge(thr):
        def body(c, accs):
            hit = s_ref[c] >= thr
            accs = list(accs)
            for r in range(tk // SUBLANES):
                a = accs[r % COUNT_ACCS]
                accs[r % COUNT_ACCS] = jnp.where(hit[r * SUBLANES:(r + 1) * SUBLANES], a + 1.0, a)
            return tuple(accs)
        accs = lax.fori_loop(0, i + 1, body,
                             tuple(jnp.zeros((SUBLANES, tq), F32) for _ in range(COUNT_ACCS)))
        return jnp.sum(_tree_sum(list(accs)), axis=0, keepdims=True)

    def count16_ge(thr16):
        def body(c, accs):
            hit = s16_ref[c] >= thr16
            accs = list(accs)
            for r in range(tk // BF16_SUBLANES):
                a = accs[r % COUNT_ACCS]
                accs[r % COUNT_ACCS] = jnp.where(hit[r * BF16_SUBLANES:(r + 1) * BF16_SUBLANES], a + 1, a)
            return tuple(accs)
        accs = lax.fori_loop(0, i + 1, body,
                             tuple(jnp.zeros((BF16_SUBLANES, tq), BF16) for _ in range(COUNT_ACCS)))
        return jnp.sum(_tree_sum(list(accs)).astype(F32), axis=0, keepdims=True)

    int_min = jnp.int32(-2 ** 31)

    def order_key(v):
        bits = pltpu.bitcast(v, jnp.int32)
        return jnp.where(bits < 0, -(bits & jnp.int32(0x7FFFFFFF)), bits)

    def from_order_key(key):
        return pltpu.bitcast(jnp.where(key < 0, (-key) | int_min, key), F32)

    t_glob = (i * tq + lax.broadcasted_iota(jnp.int32, (1, tq), 1)).astype(F32)
    n_causal = t_glob + 1.0
    kf = jnp.minimum(float(topk), n_causal)
    all_sel = n_causal <= kf

    lo16 = order_key(rmin) >> 16
    hi16 = (order_key(rmax.astype(BF16).astype(F32) + 0.0) >> 16) + 1
    for _ in range(BISECT_BF16_STEPS):
        lo_v = from_order_key(lo16 << 16)
        hi_v = from_order_key(hi16 << 16)
        mid_val16 = order_key(lo_v + (hi_v - lo_v) * 0.5) >> 16
        mid16 = jnp.where((mid_val16 > lo16) & (mid_val16 < hi16), mid_val16, (lo16 + hi16) >> 1)
        open_ = (hi16 - lo16) > 1
        cm = count16_ge(from_order_key(mid16 << 16).astype(BF16))
        lo16 = jnp.where(open_ & (cm >= kf), mid16, lo16)
        hi16 = jnp.where(open_ & (cm < kf), mid16, hi16)

    min_normal_key = jnp.int32(0x00800000)

    def snap(key, direction):
        sub = (key > -min_normal_key) & (key < min_normal_key) & (key != 0)
        below = jnp.where(key > 0, 0, -min_normal_key)
        above = jnp.where(key > 0, min_normal_key, 0)
        return jnp.where(sub, {"down": below, "up": above, "zero": jnp.zeros_like(key)}[direction], key)

    active0 = jnp.where(all_sel, 0.0, 1.0)
    thr0 = rmin
    tie0 = jnp.zeros((1, tq), F32)
    lo0 = snap(jnp.maximum((lo16 - 1) << 16, order_key(rmin)), "down")
    hi0 = snap(hi16 << 16, "up")
    fhi0 = count_ge(from_order_key(hi0))
    hif0 = jnp.full((1, tq), jnp.inf, F32)
    need0 = kf

    def b_cond(st):
        return (jnp.max(st[0]) > 0.0) & (st[8] <= BISECT_MAX_STEPS)

    def b_body(st):
        active, lo_key, hi_key, fhi, thr, tie, hif, need, step = st
        lo = from_order_key(lo_key)
        hi = from_order_key(hi_key)
        val_key = snap(order_key(lo + (hi - lo) * 0.5), "zero")
        ord_key = snap((lo_key >> 1) + (hi_key >> 1) + (lo_key & hi_key & 1), "zero")
        use_val = (step < BISECT_VALUE_STEPS) & (val_key > lo_key) & (val_key < hi_key)
        mid_key = jnp.where(use_val, val_key, ord_key)
        mid = from_order_key(mid_key)
        collapsed = (mid_key <= lo_key) | (mid_key >= hi_key) | (step >= BISECT_MAX_STEPS)
        cm = count_ge(mid)
        act = active > 0.0
        live = act & jnp.logical_not(collapsed)
        found = live & (cm == kf)
        go_up = live & (cm > kf)
        go_dn = live & (cm < kf)
        ends_tie = act & collapsed
        thr = jnp.where(found, mid, jnp.where(ends_tie, lo, thr))
        tie = jnp.where(ends_tie, 1.0, tie)
        hif = jnp.where(ends_tie, hi, hif)
        need = jnp.where(ends_tie, kf - fhi, need)
        lo_key = jnp.where(go_up, mid_key, lo_key)
        fhi = jnp.where(go_dn, cm, fhi)
        hi_key = jnp.where(go_dn, mid_key, hi_key)
        active = jnp.where(found | ends_tie, 0.0, active)
        return active, lo_key, hi_key, fhi, thr, tie, hif, need, step + 1

    def b_group(st):
        for _ in range(BISECT_GROUP):
            st = b_body(st)
        return st

    _, _, _, _, thr, tie, hif, need, _ = lax.while_loop(
        b_cond, b_group, (active0, lo0, hi0, fhi0, thr0, tie0, hif0, need0, jnp.int32(0)))

    @pl.when(jnp.max(tie) > 0.0)
    def _():
        tri = jnp.where(lax.broadcasted_iota(jnp.int32, (tk, tk), 1)
                        <= lax.broadcasted_iota(jnp.int32, (tk, tk), 0), 1.0, 0.0).astype(BF16)

        def body(c, seen):
            sc_c = s_ref[c]
            tied = (sc_c >= thr) & (sc_c < hif) & (tie > 0.0)
            rank = jnp.dot(tri, jnp.where(tied, 1.0, 0.0).astype(BF16), preferred_element_type=F32) + seen
            s_ref[c] = jnp.where(tied & (rank > need), -jnp.inf, sc_c)
            return rank[tk - 1:tk, :]

        lax.fori_loop(0, i + 1, body, jnp.zeros((1, tq), F32))

    acc_ref[...] = jnp.zeros(acc_ref.shape, F32)

    def store_logits(c, slot, bias_idx):
        masked = jnp.where(s_ref[c] >= thr, 0.0, NEG_BIG)
        row0 = pl.multiple_of(c * tk, tk)
        for h in range(nh):
            kc = k_ref[pl.ds(row0, tk), (h // 2) * LANES:(h // 2 + 1) * LANES]
            lt = lax.dot_general(kc, qpad_ref[h], _NT_DIMS, preferred_element_type=F32) + masked
            if bias_idx is not None:
                lt = lt + bias_ref[bias_idx, h]
            lg_ref[slot, h] = lt

    def softmax_pv(c, slot, m_all):
        m_out = []
        for h in range(nh):
            m_old = m_all[h]
            m_new = jnp.maximum(m_old, jnp.max(lg_ref[slot, h], axis=0, keepdims=True))
            p = jnp.exp2(lg_ref[slot, h] - m_new).astype(BF16)
            alpha = jnp.exp2(m_old - m_new)
            pv = jnp.dot(vt_ref[c, h * V_SLAB:(h + 1) * V_SLAB, :], p, preferred_element_type=F32)
            acc_ref[h] = alpha * acc_ref[h] + pv
            m_out.append(m_new)
        return tuple(m_out)

    def near_step(m_all):
        store_logits(i - 1, 1, 1)
        return softmax_pv(i, 0, m_all)

    def far_step(j, parity, m_all):
        c = i - 2 - j
        store_logits(c, parity, None)
        return softmax_pv(c + 1, 1 - parity, m_all)

    def far_pair(jj, m_all):
        return far_step(2 * jj + 1, 1, far_step(2 * jj, 0, m_all))

    n_far = jnp.maximum(i - 1, 0)
    m_all = tuple(jnp.full((1, tq), NEG_BIG, F32) for _ in range(nh))
    store_logits(i, 0, 0)
    m_all = lax.cond(i >= 1, near_step, lambda m: m, m_all)
    m_all = lax.fori_loop(0, n_far // 2, far_pair, m_all)
    m_all = lax.cond((n_far & 1) == 1, lambda m: far_step(n_far - 1, 0, m), lambda m: m, m_all)
    lax.cond((i & 1) == 0, lambda m: softmax_pv(0, 0, m), lambda m: softmax_pv(0, 1, m), m_all)

    for h in range(nh):
        o = acc_ref[h, :hd, :] / acc_ref[h, hd:hd + 1, :]
        ms = jnp.mean(o * o, axis=0, keepdims=True)
        out_ref[h * hd:(h + 1) * hd, :] = o * lax.rsqrt(ms + EPS)
    o_ref[...] = (out_ref[...].T * og_ref[...]).astype(BF16)


def _attn_call(rel_bias, bounds, q, qi, wit, k, ki, vt, og, topk):
    bsz, seq, aw = q.shape
    tq, tk = ATT_TQ, ATT_TK
    nck = seq // tk
    blk_q = lambda b, i: (b, i, 0)
    whole = lambda b, i: (b, 0, 0)
    smem = pl.BlockSpec(memory_space=pltpu.SMEM)
    return pl.pallas_call(
        functools.partial(_attn_kernel, topk=topk),
        out_shape=jax.ShapeDtypeStruct((bsz, seq, aw), BF16),
        grid=(bsz, seq // tq),
        in_specs=[
            smem, smem,
            pl.BlockSpec((None, tq, aw), blk_q),
            pl.BlockSpec((None, tq, aw), blk_q),
            pl.BlockSpec((None, IDX_HEADS, tq), lambda b, i: (b, 0, i)),
            pl.BlockSpec((None, seq, aw), whole),
            pl.BlockSpec((None, seq, LANES), whole),
            pl.BlockSpec((None, nck, ATTN_HEADS * V_SLAB, tk), lambda b, i: (b, 0, 0, 0)),
            pl.BlockSpec(og.shape, lambda b, i: (0, 0)),
        ],
        out_specs=pl.BlockSpec((None, tq, aw), blk_q),
        scratch_shapes=[
            pltpu.VMEM((nck, tk, tq), F32),
            pltpu.VMEM((nck, tk, tq), BF16),
            pltpu.VMEM((2, ATTN_HEADS, tk, tq), F32),
            pltpu.VMEM((ATTN_HEADS, tq, LANES), BF16),
            pltpu.VMEM((IDX_HEADS, tq, LANES), BF16),
            pltpu.VMEM((2, ATTN_HEADS, tk, tq), F32),
            pltpu.VMEM((ATTN_HEADS, V_SLAB, tq), F32),
            pltpu.VMEM((aw, tq), F32),
        ],
        compiler_params=pltpu.CompilerParams(dimension_semantics=("arbitrary", "arbitrary"),
                                             vmem_limit_bytes=VMEM_LIMIT_BYTES),
        name="dsa_attention",
    )(rel_bias, bounds, q, qi, wit, k, ki, vt, og)


def _post_kernel(an_ref, cn_ref, x_ref, mod_ref, n2_ref, woa_ref, woc_ref, wr_ref, br_ref,
                 x1_ref, h2_ref, comb_ref, cnt_ref):
    mix = (jnp.dot(an_ref[...], woa_ref[...], preferred_element_type=F32)
           + jnp.dot(cn_ref[...], woc_ref[...], preferred_element_type=F32))
    x1 = x_ref[...] + mod_ref[2:3, :] * mix
    x1_ref[...] = x1
    ms = jnp.mean(x1 * x1, axis=-1, keepdims=True)
    h2 = x1 * lax.rsqrt(ms + EPS) * n2_ref[...] * (1.0 + mod_ref[4:5, :]) + mod_ref[3:4, :]
    h2b = h2.astype(BF16)
    h2_ref[...] = h2b

    logits = jnp.dot(h2b, wr_ref[...], preferred_element_type=F32) + br_ref[...]
    lane = lax.broadcasted_iota(jnp.int32, logits.shape, 1)
    lane_f = lane.astype(F32)
    far = float(LANES)
    is_g = (lane >= N_EXPERTS) & (lane < N_EXPERTS + N_GROUPS)
    gl = jnp.where(is_g, logits, -jnp.inf)
    gmax = jnp.max(gl, axis=-1, keepdims=True)
    g_sel = jnp.min(jnp.where(is_g & (gl == gmax), lane_f, far), axis=-1, keepdims=True) - float(N_EXPERTS)
    p_g = 1.0 / jnp.sum(jnp.exp(gl - gmax), axis=-1, keepdims=True)

    in_grp = (lane < N_EXPERTS) & ((lane // EXPERTS_PER_GROUP).astype(F32) == g_sel)
    e1 = jnp.where(in_grp, logits, -jnp.inf)
    l1 = jnp.max(e1, axis=-1, keepdims=True)
    i1 = jnp.min(jnp.where(in_grp & (e1 == l1), lane_f, far), axis=-1, keepdims=True)
    rest = in_grp & (lane_f != i1)
    e2 = jnp.where(rest, logits, -jnp.inf)
    l2 = jnp.max(e2, axis=-1, keepdims=True)
    i2 = jnp.min(jnp.where(rest & (e2 == l2), lane_f, far), axis=-1, keepdims=True)
    r = jnp.exp(l2 - l1)
    w1 = 1.0 / (1.0 + r)
    w2 = r / (1.0 + r)
    comb = jnp.where(lane_f == i1, p_g * w1, 0.0) + jnp.where(lane_f == i2, p_g * w2, 0.0)
    comb_ref[...] = comb
    cnt = jnp.sum(jnp.where(comb != 0.0, 1.0, 0.0), axis=0, keepdims=True)
    cnt_ref[...] = jnp.broadcast_to(cnt, cnt_ref.shape)


def _post_call(an, cn, x, mod, n2, woa, woc, wr, br):
    bsz, seq, d = x.shape
    tm = POST_TM
    tok = lambda b, j: (b, j, 0)
    const = lambda b, j: (0, 0)
    return pl.pallas_call(
        _post_kernel,
        out_shape=(jax.ShapeDtypeStruct((bsz, seq, d), F32),
                   jax.ShapeDtypeStruct((bsz, seq, d), BF16),
                   jax.ShapeDtypeStruct((bsz, seq, LANES), F32),
                   jax.ShapeDtypeStruct((bsz, seq // tm, SUBLANES, LANES), F32)),
        grid=(bsz, seq // tm),
        in_specs=[
            pl.BlockSpec((None, tm, ATTN_WIDTH), tok),
            pl.BlockSpec((None, tm, CONV_WIDTH), tok),
            pl.BlockSpec((None, tm, d), tok),
            pl.BlockSpec((None, 6, d), lambda b, j: (b, 0, 0)),
            pl.BlockSpec(n2.shape, const),
            pl.BlockSpec(woa.shape, const),
            pl.BlockSpec(woc.shape, const),
            pl.BlockSpec(wr.shape, const),
            pl.BlockSpec(br.shape, const),
        ],
        out_specs=(pl.BlockSpec((None, tm, d), tok),
                   pl.BlockSpec((None, tm, d), tok),
                   pl.BlockSpec((None, tm, LANES), tok),
                   pl.BlockSpec((None, None, SUBLANES, LANES), lambda b, j: (b, j, 0, 0))),
        compiler_params=pltpu.CompilerParams(dimension_semantics=("arbitrary", "arbitrary"),
                                             vmem_limit_bytes=VMEM_LIMIT_BYTES),
        name="post_router",
    )(an, cn, x, mod, n2, woa, woc, wr, br)


def _strict_tri(n, lower):
    r = lax.broadcasted_iota(jnp.int32, (n, n), 0)
    c = lax.broadcasted_iota(jnp.int32, (n, n), 1)
    return jnp.where((c < r) if lower else (r < c), 1.0, 0.0).astype(BF16)


def _moe_tile_copies(nloc_ref, gtile_ref, blk, local_ref, global_ref, sem, to_global, wait):
    tile = MOE_TILE

    def per_tile(lt, c):
        loc = local_ref.at[pl.ds(pl.multiple_of(lt * tile, tile), tile), :]
        glo = global_ref.at[pl.ds(pl.multiple_of(gtile_ref[blk, lt] * tile, tile), tile), :]
        cp = pltpu.make_async_copy(loc, glo, sem) if to_global else pltpu.make_async_copy(glo, loc, sem)
        if wait:
            cp.wait()
        else:
            cp.start()
        return c

    lax.fori_loop(0, nloc_ref[blk], per_tile, 0)


def _moe_gather_kernel(nloc_ref, gtile_ref, padstart_ref, pad_ref,
                       h2_ref, comb_ref,
                       col_ref, xg_hbm,
                       xg_ref, row_ref, zero_ref, sem):
    blk = pl.program_id(0)
    nb = h2_ref.shape[0]
    tile, chunk = MOE_TILE, MOE_CHUNK
    lane = lax.broadcasted_iota(jnp.int32, (nb, LANES), 1)

    comb = comb_ref[...]
    assigned = comb != 0.0
    a_f = jnp.where(assigned, 1.0, 0.0)
    rank = jnp.dot(_strict_tri(nb, True), a_f.astype(BF16), preferred_element_type=F32)
    cnt = rank[nb - 1:nb, :] + a_f[nb - 1:nb, :]
    ntile = jnp.floor((cnt + float(tile - 1)) * (1.0 / tile))
    first = jnp.dot(jnp.broadcast_to(ntile, (SUBLANES, LANES)).astype(BF16), _strict_tri(LANES, False),
                    preferred_element_type=F32)[0:1, :]
    pos = first * float(tile) + rank
    pos1 = jnp.min(jnp.where(assigned, pos, 1e9), axis=1, keepdims=True)
    pos2 = jnp.max(jnp.where(assigned, pos, -1.0), axis=1, keepdims=True)
    pos2 = jnp.where(pos2 == pos1, -1.0, pos2)
    cw1 = jnp.sum(jnp.where(assigned & (pos == pos1), comb, 0.0), axis=1, keepdims=True)
    cw2 = jnp.sum(jnp.where(assigned & (pos == pos2), comb, 0.0), axis=1, keepdims=True)
    info = jnp.where(lane == 0, pos1, jnp.where(lane == 1, pos2, jnp.where(lane == 2, cw1,
                     jnp.where(lane == 3, cw2, 0.0))))
    col_ref[...] = info
    row_ref[...] = info.T

    n_chunks = (nloc_ref[blk] * tile + (chunk - 1)) // chunk
    p1 = row_ref[0:1, :].astype(jnp.int32)
    p2 = row_ref[1:2, :].astype(jnp.int32)
    sub = lax.broadcasted_iota(jnp.int32, (chunk, nb), 0)

    def gather(c, carry):
        p = sub + c * chunk
        sel = jnp.where((p == p1) | (p == p2), 1.0, 0.0).astype(BF16)
        r0 = pl.multiple_of(c * chunk, chunk)
        xg_ref[pl.ds(r0, chunk), :] = jnp.dot(sel, h2_ref[...], preferred_element_type=F32).astype(BF16)
        return carry

    @pl.when(blk > 0)
    def _():
        _moe_tile_copies(nloc_ref, gtile_ref, blk - 1, xg_ref, xg_hbm, sem, True, True)

    lax.fori_loop(0, n_chunks, gather, 0)

    _moe_tile_copies(nloc_ref, gtile_ref, blk, xg_ref, xg_hbm, sem, True, False)

    is_last = blk == pl.num_programs(0) - 1

    def pad_copies(wait):
        def per_expert(x, carry):
            g0 = padstart_ref[x]

            def per_tile(j, c):
                dst = xg_hbm.at[pl.ds(pl.multiple_of((g0 + j) * tile, tile), tile), :]
                cp = pltpu.make_async_copy(zero_ref, dst, sem)
                if wait:
                    cp.wait()
                else:
                    cp.start()
                return c

            lax.fori_loop(0, pad_ref[x], per_tile, 0)
            return carry

        lax.fori_loop(0, N_EXPERTS, per_expert, 0)

    @pl.when(is_last)
    def _():
        zero_ref[...] = jnp.zeros(zero_ref.shape, BF16)
        pad_copies(False)
        _moe_tile_copies(nloc_ref, gtile_ref, blk, xg_ref, xg_hbm, sem, True, True)
        pad_copies(True)


def _moe_ffn_kernel(texp_ref, nt_ref, x_ref, wg_ref, wu_ref, wd_ref, y_ref):
    @pl.when(pl.program_id(0) < nt_ref[0])
    def _():
        x = x_ref[...]
        a = jnp.dot(x, wg_ref[...].astype(BF16), preferred_element_type=F32)
        up = jnp.dot(x, wu_ref[...].astype(BF16), preferred_element_type=F32)
        hid = ((a * jax.nn.sigmoid(a)) * up).astype(BF16)
        y_ref[...] = jnp.dot(hid, wd_ref[...].astype(BF16), preferred_element_type=F32).astype(BF16)


def _moe_scatter_kernel(nloc_ref, gtile_ref,
                        col_ref, x1_ref, mod_ref, y_hbm,
                        o_ref,
                        y_ref, sem):
    blk = pl.program_id(0)
    nb = x1_ref.shape[0]
    tile, chunk = MOE_TILE, MOE_CHUNK
    slot = blk & 1

    def copies(b, s, wait):
        _moe_tile_copies(nloc_ref, gtile_ref, b, y_ref.at[s], y_hbm, sem.at[s], False, wait)

    @pl.when(blk == 0)
    def _():
        copies(0, 0, False)

    @pl.when(blk + 1 < pl.num_programs(0))
    def _():
        copies(blk + 1, 1 - slot, False)

    total = nloc_ref[blk]
    n_chunks = (total * tile + (chunk - 1)) // chunk
    max_chunks = y_ref.shape[1] // chunk
    usual = n_chunks <= MOE_USUAL_CHUNKS
    n_static = jnp.where(usual, MOE_USUAL_CHUNKS, max_chunks)

    def clear(t, carry):
        y_ref[slot, pl.ds(pl.multiple_of(t * tile, tile), tile), :] = jnp.zeros((tile, y_ref.shape[2]), BF16)
        return carry

    lax.fori_loop(total, n_static * (chunk // tile), clear, 0)

    p1 = col_ref[:, 0:1].astype(jnp.int32)
    p2 = col_ref[:, 1:2].astype(jnp.int32)
    cw1 = col_ref[:, 2:3]
    cw2 = col_ref[:, 3:4]
    gate = mod_ref[5:6, :]
    lane_c = lax.broadcasted_iota(jnp.int32, (nb, chunk), 1)
    copies(blk, slot, True)

    def scatter(n_unrolled):
        acc = None
        for c in range(n_unrolled):
            p = lane_c + c * chunk
            w = (jnp.where(p == p1, cw1, 0.0) + jnp.where(p == p2, cw2, 0.0)).astype(BF16)
            part = jnp.dot(w, y_ref[slot, c * chunk:(c + 1) * chunk, :], preferred_element_type=F32)
            acc = part if acc is None else acc + part
        o_ref[...] = x1_ref[...] + gate * acc

    lax.cond(usual, lambda: scatter(MOE_USUAL_CHUNKS), lambda: scatter(max_chunks))


def _moe_call(h2, comb, cnt_tiles, x1, mod, w_gate, w_up, w_down):
    bsz, seq, d = x1.shape
    nb, tile, ftm = MOE_TM, MOE_TILE, MOE_FFN_TM
    n_tok = bsz * seq
    n_blk = n_tok // nb
    region = ftm // tile
    rows_local = -(-(2 * nb + N_EXPERTS * tile) // MOE_CHUNK) * MOE_CHUNK
    tiles_global = (2 * n_tok) // tile + n_blk * N_EXPERTS + N_EXPERTS * (region - 1)
    n_ffn_max = -(-tiles_global // region)
    rows_global = n_ffn_max * ftm

    cnt = cnt_tiles[:, :, 0, :N_EXPERTS].reshape(n_blk, nb // POST_TM, N_EXPERTS).sum(axis=1).astype(jnp.int32)
    ntile = (cnt + (tile - 1)) // tile
    lfirst = jnp.cumsum(ntile, axis=1) - ntile
    tot = ntile.sum(axis=0)
    ptot = (tot + (region - 1)) // region * region
    ebase = jnp.cumsum(ptot) - ptot
    gfirst = ebase[None, :] + jnp.cumsum(ntile, axis=0) - ntile
    pad = ptot - tot
    n_ffn = (ptot.sum() // region).reshape(1)
    ends = jnp.cumsum(ptot) // region
    texp = jnp.minimum((jnp.arange(n_ffn_max, dtype=jnp.int32)[:, None] >= ends[None, :]).sum(axis=1),
                       N_EXPERTS - 1).astype(jnp.int32)
    nloc = ntile.sum(axis=1).astype(jnp.int32)
    lt = jnp.arange(rows_local // tile, dtype=jnp.int32)[None, :, None]
    in_seg = (lt >= lfirst[:, None, :]) & (lt < (lfirst + ntile)[:, None, :])
    gtile = (jnp.where(in_seg, (gfirst - lfirst)[:, None, :], 0).sum(axis=2) + lt[:, :, 0]).astype(jnp.int32)
    padstart = (gfirst[-1] + ntile[-1]).astype(jnp.int32)

    h2f = h2.reshape(n_tok, d)
    combf = comb.reshape(n_tok, LANES)
    col, xg = pl.pallas_call(
        _moe_gather_kernel,
        out_shape=(jax.ShapeDtypeStruct((n_tok, LANES), F32),
                   jax.ShapeDtypeStruct((rows_global, d), BF16)),
        grid_spec=pltpu.PrefetchScalarGridSpec(
            num_scalar_prefetch=4,
            grid=(n_blk,),
            in_specs=[pl.BlockSpec((nb, d), lambda j, *_: (j, 0)),
                      pl.BlockSpec((nb, LANES), lambda j, *_: (j, 0))],
            out_specs=(pl.BlockSpec((nb, LANES), lambda j, *_: (j, 0)),
                       pl.BlockSpec(memory_space=pl.ANY)),
            scratch_shapes=[
                pltpu.VMEM((rows_local, d), BF16),
                pltpu.VMEM((LANES, nb), F32),
                pltpu.VMEM((tile, d), BF16),
                pltpu.SemaphoreType.DMA,
            ]),
        compiler_params=pltpu.CompilerParams(dimension_semantics=("arbitrary",),
                                             vmem_limit_bytes=VMEM_LIMIT_BYTES),
        name="moe_gather",
    )(nloc, gtile, padstart, pad, h2f, combf)

    last = lambda t, te, nt: jnp.minimum(t, nt[0] - 1)
    y = pl.pallas_call(
        _moe_ffn_kernel,
        out_shape=jax.ShapeDtypeStruct((rows_global, d), BF16),
        grid_spec=pltpu.PrefetchScalarGridSpec(
            num_scalar_prefetch=2,
            grid=(n_ffn_max,),
            in_specs=[pl.BlockSpec((ftm, d), lambda t, te, nt: (last(t, te, nt), 0)),
                      pl.BlockSpec((None, d, EXPERT_FF), lambda t, te, nt: (te[last(t, te, nt)], 0, 0)),
                      pl.BlockSpec((None, d, EXPERT_FF), lambda t, te, nt: (te[last(t, te, nt)], 0, 0)),
                      pl.BlockSpec((None, EXPERT_FF, d), lambda t, te, nt: (te[last(t, te, nt)], 0, 0))],
            out_specs=pl.BlockSpec((ftm, d), lambda t, te, nt: (last(t, te, nt), 0))),
        compiler_params=pltpu.CompilerParams(dimension_semantics=("arbitrary",),
                                             vmem_limit_bytes=VMEM_LIMIT_BYTES),
        name="moe_ffn",
    )(texp, n_ffn, xg, w_gate, w_up, w_down)

    out = pl.pallas_call(
        _moe_scatter_kernel,
        out_shape=jax.ShapeDtypeStruct((n_tok, d), F32),
        grid_spec=pltpu.PrefetchScalarGridSpec(
            num_scalar_prefetch=2,
            grid=(n_blk,),
            in_specs=[pl.BlockSpec((nb, LANES), lambda j, *_: (j, 0)),
                      pl.BlockSpec((nb, d), lambda j, *_: (j, 0)),
                      pl.BlockSpec((None, 6, d), lambda j, *_: ((j * nb) // seq, 0, 0)),
                      pl.BlockSpec(memory_space=pl.ANY)],
            out_specs=pl.BlockSpec((nb, d), lambda j, *_: (j, 0)),
            scratch_shapes=[pltpu.VMEM((2, rows_local, d), BF16),
                            pltpu.SemaphoreType.DMA((2,))]),
        compiler_params=pltpu.CompilerParams(dimension_semantics=("arbitrary",),
                                             vmem_limit_bytes=VMEM_LIMIT_BYTES),
        name="moe_scatter",
    )(nloc, gtile, col, x1.reshape(n_tok, d), mod, y)
    return out.reshape(bsz, seq, d)


def _layer(x, mod, rel_bias, norm1, w_in, q_norm, k_norm, conv_w, attn_out_norm, conv_out_norm, w_out,
           norm2, w_group_router, b_group_router, w_expert_router, b_expert_router, w_gate, w_up, w_down):
    bsz, seq, d = x.shape
    aw = ATTN_WIDTH
    topk = min(TOPK_MAX, seq // 4)

    offs = np.cumsum([0, aw, aw, aw, IDX_HEADS * IDX_DIM, IDX_DIM, IDX_HEADS, CONV_WIDTH, CONV_WIDTH, CONV_WIDTH])
    col = lambda n: w_in[:, int(offs[n]):int(offs[n + 1])]
    wm = jnp.concatenate([col(0), col(1), col(3), col(6), col(7), col(8)], axis=1).astype(BF16)
    wvt = col(2).T.astype(BF16)
    wki = jnp.concatenate([col(4), col(4)], axis=1).astype(BF16)
    wwit = col(5).T.astype(BF16)
    qg = (jnp.tile(q_norm, ATTN_HEADS) * ((HEAD_DIM ** -0.5) * LOG2E))[None, :]
    kg = jnp.tile(k_norm, ATTN_HEADS)[None, :]
    grp = np.arange(aw) // CONV_GROUP_DIM
    gmat = jnp.asarray((grp[:, None] == grp[None, :]).astype(np.float32) / CONV_GROUP_DIM, dtype=BF16)

    q, k, vt, qi, ki, wit, cn = _pre_call(
        x, mod, norm1[None, :], wm, wvt, wki, wwit, qg, kg, conv_w, conv_out_norm.reshape(1, -1), gmat)

    bounds = jnp.asarray(_bucket_boundaries())
    an = _attn_call(rel_bias, bounds, q, qi, wit, k, ki, vt, attn_out_norm.reshape(1, -1), topk)

    wr = jnp.concatenate([w_expert_router, w_group_router,
                          jnp.zeros((d, LANES - N_EXPERTS - N_GROUPS), F32)], axis=1).astype(BF16)
    br = jnp.concatenate([b_expert_router, b_group_router,
                          jnp.zeros((LANES - N_EXPERTS - N_GROUPS,), F32)])[None, :]
    x1, h2, comb, cnt_tiles = _post_call(an, cn, x, mod, norm2[None, :], w_out[:aw].astype(BF16),
                                         w_out[aw:].astype(BF16), wr, br)

    return _moe_call(h2, comb, cnt_tiles, x1, mod, w_gate, w_up, w_down)


def kernel(x, c, rel_bias, w_ada, b_ada, norm1, w_in, q_norm, k_norm, conv_w, attn_out_norm, conv_out_norm,
           w_out, norm2, w_group_router, b_group_router, w_expert_router, b_expert_router, w_gate, w_up,
           w_down):
    bsz, seq, d = x.shape
    assert d == D_MODEL and seq % max(PRE_TM, POST_TM, MOE_TM) == 0 and ATT_TQ == ATT_TK
    depth = w_ada.shape[0]
    for l in range(depth):
        mod = _mod_call(c, w_ada[l], b_ada[l][None, :]).reshape(bsz, 6, d)
        x = _layer(x, mod, rel_bias, norm1[l], w_in[l], q_norm[l], k_norm[l], conv_w[l], attn_out_norm[l],
                   conv_out_norm[l], w_out[l], norm2[l], w_group_router[l], b_group_router[l],
                   w_expert_router[l], b_expert_router[l], w_gate[l], w_up[l], w_down[l])
    return x
```

```python
import functools
import math

import jax
import jax.numpy as jnp
import numpy as np
from jax import lax
from jax.experimental import pallas as pl
from jax.experimental.pallas import tpu as pltpu

F32 = jnp.float32
BF16 = jnp.bfloat16

D_MODEL = 1024
HEAD_DIM = 64
ATTN_HEADS = 8
ATTN_WIDTH = ATTN_HEADS * HEAD_DIM
CONV_WIDTH = D_MODEL - ATTN_WIDTH
CONV_GROUP_DIM = 64
CONV_K = 3
IDX_HEADS = 8
IDX_DIM = 64
TOPK_MAX = 256
IDX_SCALE = (IDX_DIM ** -0.5) * (IDX_HEADS ** -0.5)
N_BUCKETS = 32
MAX_DISTANCE = 128
N_GROUPS = 4
EXPERTS_PER_GROUP = 8
N_EXPERTS = N_GROUPS * EXPERTS_PER_GROUP
EXPERT_FF = 256
EPS = 1e-6
LOG2E = 1.4426950408889634
NEG_BIG = -1e30
COUNT_ACCS = 4
BISECT_GROUP = 4
BISECT_BF16_STEPS = 10
BISECT_VALUE_STEPS = 4
BISECT_MAX_STEPS = 64

LANES = 128
SUBLANES = 8
BF16_SUBLANES = 16
V_SLAB = HEAD_DIM + BF16_SUBLANES
VMEM_LIMIT_BYTES = 56 * 1024 * 1024

PRE_TM = 512
ATT_TQ = 256
ATT_TK = 256
POST_TM = 512
MOE_TM = 512
MOE_TILE = 32
MOE_CHUNK = 512
MOE_USUAL_CHUNKS = 3
MOE_FFN_TM = 1024
MOD_TN = 1536

_NT_DIMS = (((1,), (1,)), ((), ()))


def _tree_sum(parts):
    while len(parts) > 1:
        nxt = [parts[j] + parts[j + 1] for j in range(0, len(parts) - 1, 2)]
        if len(parts) % 2:
            nxt.append(parts[-1])
        parts = nxt
    return parts[0]


def _bucket_boundaries():
    max_exact = N_BUCKETS // 2
    d = np.arange(0, 4 * MAX_DISTANCE, dtype=np.int64)
    nf = np.maximum(d, 1).astype(np.float32)
    large = max_exact + (np.log(nf / np.float32(max_exact)) / np.float32(math.log(MAX_DISTANCE / max_exact))
                         * np.float32(N_BUCKETS - max_exact)).astype(np.int32)
    large = np.minimum(large, N_BUCKETS - 1)
    bucket = np.where(d < max_exact, d, large)
    assert np.all(np.diff(bucket) >= 0) and bucket[-1] == N_BUCKETS - 1
    bounds = [int(np.argmax(bucket >= j)) for j in range(1, N_BUCKETS)]
    return np.asarray([0] + bounds, dtype=np.int32)


def _mod_kernel(c_ref, w_ref, b_ref, o_ref):
    c = c_ref[...]
    act = c * jax.nn.sigmoid(c)
    o_ref[...] = jnp.dot(act, w_ref[...], preferred_element_type=F32,
                         precision=lax.Precision.HIGHEST) + b_ref[...]


def _mod_call(c, w_ada, b_ada):
    bsz, d = c.shape
    n = w_ada.shape[1]
    return pl.pallas_call(
        _mod_kernel,
        out_shape=jax.ShapeDtypeStruct((bsz, n), F32),
        grid=(n // MOD_TN,),
        in_specs=[pl.BlockSpec((bsz, d), lambda j: (0, 0)),
                  pl.BlockSpec((d, MOD_TN), lambda j: (0, j)),
                  pl.BlockSpec((1, MOD_TN), lambda j: (0, j))],
        out_specs=pl.BlockSpec((bsz, MOD_TN), lambda j: (0, j)),
        compiler_params=pltpu.CompilerParams(dimension_semantics=("arbitrary",),
                                             vmem_limit_bytes=VMEM_LIMIT_BYTES),
        name="adaln_mod",
    )(c, w_ada, b_ada)


def _group_rms(y, g_ref):
    ms = jnp.dot((y * y).astype(BF16), g_ref[...], preferred_element_type=F32)
    return y * lax.rsqrt(ms + EPS)


def _pre_kernel(x_ref, mod_ref, n1_ref, wm_ref, wvt_ref, wki_ref, wwit_ref, qg_ref, kg_ref,
                cw_ref, cg_ref, g_ref,
                q_ref, k_ref, vt_ref, qi_ref, ki_ref, wit_ref, cn_ref, carry_ref):
    j = pl.program_id(1)
    tm = x_ref.shape[0]
    aw = ATTN_WIDTH

    x = x_ref[...]
    ms = jnp.mean(x * x, axis=-1, keepdims=True)
    y = x * lax.rsqrt(ms + EPS) * n1_ref[...]
    h = y * (1.0 + mod_ref[1:2, :]) + mod_ref[0:1, :]
    hb = h.astype(BF16)

    def proj(lo):
        return jnp.dot(hb, wm_ref[:, lo:lo + aw], preferred_element_type=F32)

    q = _group_rms(proj(0), g_ref) * qg_ref[...]
    q_ref[...] = q.astype(BF16)
    k = _group_rms(proj(aw), g_ref) * kg_ref[...]
    k_ref[...] = k.astype(BF16)

    vt = lax.dot_general(wvt_ref[...], hb, _NT_DIMS, preferred_element_type=F32).astype(BF16)
    ones = jnp.ones((BF16_SUBLANES, ATT_TK), BF16)
    for cc in range(tm // ATT_TK):
        for hh in range(ATTN_HEADS):
            vt_ref[cc, hh * V_SLAB:hh * V_SLAB + HEAD_DIM, :] = (
                vt[hh * HEAD_DIM:(hh + 1) * HEAD_DIM, cc * ATT_TK:(cc + 1) * ATT_TK])
            vt_ref[cc, hh * V_SLAB + HEAD_DIM:(hh + 1) * V_SLAB, :] = ones

    qi_ref[...] = proj(2 * aw).astype(BF16)
    ki_ref[...] = jnp.dot(hb, wki_ref[...], preferred_element_type=F32).astype(BF16)
    wit_ref[...] = lax.dot_general(wwit_ref[...], hb, _NT_DIMS, preferred_element_type=F32) * IDX_SCALE

    gate_b = proj(3 * aw)
    z = proj(4 * aw) * proj(5 * aw)

    @pl.when(j == 0)
    def _():
        carry_ref[...] = jnp.zeros_like(carry_ref)

    prev = carry_ref[...]
    row = lax.broadcasted_iota(jnp.int32, z.shape, 0)
    z1 = jnp.where(row == 0, prev[SUBLANES - 1:SUBLANES, :], pltpu.roll(z, 1, 0))
    z2 = pltpu.roll(z, 2, 0)
    z2 = jnp.where(row == 0, prev[SUBLANES - 2:SUBLANES - 1, :], z2)
    z2 = jnp.where(row == 1, prev[SUBLANES - 1:SUBLANES, :], z2)
    carry_ref[...] = z[tm - SUBLANES:, :]
    conv = cw_ref[2:3, :] * z + cw_ref[1:2, :] * z1 + cw_ref[0:1, :] * z2
    yc = gate_b * conv
    cn_ref[...] = (_group_rms(yc, g_ref) * cg_ref[...]).astype(BF16)


def _pre_call(x, mod, n1, wm, wvt, wki, wwit, qg, kg, cw, cg, gmat):
    bsz, seq, d = x.shape
    tm = PRE_TM
    nck = tm // ATT_TK
    aw = ATTN_WIDTH
    const = lambda b, j: (0, 0)
    tok = lambda b, j: (b, j, 0)
    out_shape = (
        jax.ShapeDtypeStruct((bsz, seq, aw), BF16),
        jax.ShapeDtypeStruct((bsz, seq, aw), BF16),
        jax.ShapeDtypeStruct((bsz, seq // ATT_TK, ATTN_HEADS * V_SLAB, ATT_TK), BF16),
        jax.ShapeDtypeStruct((bsz, seq, aw), BF16),
        jax.ShapeDtypeStruct((bsz, seq, LANES), BF16),
        jax.ShapeDtypeStruct((bsz, IDX_HEADS, seq), F32),
        jax.ShapeDtypeStruct((bsz, seq, CONV_WIDTH), BF16),
    )
    out_specs = (
        pl.BlockSpec((None, tm, aw), tok),
        pl.BlockSpec((None, tm, aw), tok),
        pl.BlockSpec((None, nck, ATTN_HEADS * V_SLAB, ATT_TK), lambda b, j: (b, j, 0, 0)),
        pl.BlockSpec((None, tm, aw), tok),
        pl.BlockSpec((None, tm, LANES), tok),
        pl.BlockSpec((None, IDX_HEADS, tm), lambda b, j: (b, 0, j)),
        pl.BlockSpec((None, tm, CONV_WIDTH), tok),
    )
    in_specs = [
        pl.BlockSpec((None, tm, d), tok),
        pl.BlockSpec((None, 6, d), lambda b, j: (b, 0, 0)),
        pl.BlockSpec(n1.shape, const),
        pl.BlockSpec(wm.shape, const),
        pl.BlockSpec(wvt.shape, const),
        pl.BlockSpec(wki.shape, const),
        pl.BlockSpec(wwit.shape, const),
        pl.BlockSpec(qg.shape, const),
        pl.BlockSpec(kg.shape, const),
        pl.BlockSpec(cw.shape, const),
        pl.BlockSpec(cg.shape, const),
        pl.BlockSpec(gmat.shape, const),
    ]
    return pl.pallas_call(
        _pre_kernel,
        out_shape=out_shape,
        grid=(bsz, seq // tm),
        in_specs=in_specs,
        out_specs=out_specs,
        scratch_shapes=[pltpu.VMEM((SUBLANES, CONV_WIDTH), F32)],
        compiler_params=pltpu.CompilerParams(dimension_semantics=("arbitrary", "arbitrary"),
                                             vmem_limit_bytes=VMEM_LIMIT_BYTES),
        name="pre_proj",
    )(x, mod, n1, wm, wvt, wki, wwit, qg, kg, cw, cg, gmat)


def _attn_kernel(rb_ref, bnd_ref, q_ref, qi_ref, wit_ref, k_ref, ki_ref, vt_ref, og_ref,
                 o_ref,
                 s_ref, s16_ref, bias_ref, qpad_ref, qipad_ref, lg_ref, acc_ref, out_ref, *, topk):
    b = pl.program_id(0)
    i = pl.program_id(1)
    tq, tk = ATT_TQ, ATT_TK
    nh, hd = ATTN_HEADS, HEAD_DIM

    t_loc = lax.broadcasted_iota(jnp.int32, (tk, tq), 1)
    s_loc = lax.broadcasted_iota(jnp.int32, (tk, tq), 0)

    @pl.when((b == 0) & (i == 0))
    def _():
        for idx in range(2):
            dist = t_loc - s_loc + idx * tq
            for h in range(nh):
                bias_ref[idx, h] = jnp.full((tk, tq), (rb_ref[0, h] - rb_ref[N_BUCKETS - 1, h]) * LOG2E, F32)

            def fill(jb, carry):
                reached = dist >= bnd_ref[jb]
                for h in range(nh):
                    val = (rb_ref[jb, h] - rb_ref[N_BUCKETS - 1, h]) * LOG2E
                    bias_ref[idx, h] = jnp.where(reached, val, bias_ref[idx, h])
                return carry

            lax.fori_loop(1, N_BUCKETS, fill, 0)

    lane = lax.broadcasted_iota(jnp.int32, (tq, LANES), 1)
    for h in range(nh):
        pair = slice((h // 2) * LANES, (h // 2 + 1) * LANES)
        keep = (lane // hd) == (h % 2)
        qpad_ref[h] = jnp.where(keep, q_ref[:, pair], jnp.zeros((), BF16))
        qipad_ref[h] = jnp.where(keep, qi_ref[:, pair], jnp.zeros((), BF16))

    def idx_dots(c, slot):
        kic = ki_ref[pl.ds(pl.multiple_of(c * tk, tk), tk), :]
        for h in range(nh):
            lg_ref[slot, h] = lax.dot_general(kic, qipad_ref[h], _NT_DIMS, preferred_element_type=F32)

    def idx_reduce(c, slot, carry, diagonal):
        rmin, rmax = carry
        sc = _tree_sum([wit_ref[h:h + 1, :] * jnp.maximum(lg_ref[slot, h], 0.0) for h in range(nh)])
        if diagonal:
            causal = s_loc <= t_loc
            lo_c, hi_c = jnp.where(causal, sc, jnp.inf), jnp.where(causal, sc, -jnp.inf)
            sc = hi_c
        else:
            lo_c, hi_c = sc, sc
        s_ref[c] = sc
        s16_ref[c] = sc.astype(BF16)
        return (jnp.minimum(rmin, jnp.min(lo_c, axis=0, keepdims=True)),
                jnp.maximum(rmax, jnp.max(hi_c, axis=0, keepdims=True)))

    def idx_pair(jj, carry):
        idx_dots(2 * jj + 1, 1)
        carry = idx_reduce(2 * jj, 0, carry, False)
        idx_dots(2 * jj + 2, 0)
        return idx_reduce(2 * jj + 1, 1, carry, False)

    def idx_tail_odd(carry):
        idx_dots(i, 1)
        return idx_reduce(i, 1, idx_reduce(i - 1, 0, carry, False), True)

    idx_dots(0, 0)
    carry = (jnp.full((1, tq), jnp.inf, F32), jnp.full((1, tq), -jnp.inf, F32))
    carry = lax.fori_loop(0, i // 2, idx_pair, carry)
    rmin, rmax = lax.cond((i & 1) == 1, idx_tail_odd, lambda cr: idx_reduce(i, 0, cr, True), carry)

    def count_ge(thr):
        def body(c, accs):
            hit = s_ref[c] >= thr
            accs = list(accs)
            for r in range(tk // SUBLANES):
                a = accs[r % COUNT_ACCS]
                accs[r % COUNT_ACCS] = jnp.where(hit[r * SUBLANES:(r + 1) * SUBLANES], a + 1.0, a)
            return tuple(accs)
        accs = lax.fori_loop(0, i + 1, body,
                             tuple(jnp.zeros((SUBLANES, tq), F32) for _ in range(COUNT_ACCS)))
        return jnp.sum(_tree_sum(list(accs)), axis=0, keepdims=True)

    def count16_ge(thr16):
        def body(c, accs):
            hit = s16_ref[c] >= thr16
            accs = list(accs)
            for r in range(tk // BF16_SUBLANES):
                a = accs[r % COUNT_ACCS]
                accs[r % COUNT_ACCS] = jnp.where(hit[r * BF16_SUBLANES:(r + 1) * BF16_SUBLANES], a + 1, a)
            return tuple(accs)
        accs = lax.fori_loop(0, i + 1, body,
                             tuple(jnp.zeros((BF16_SUBLANES, tq), BF16) for _ in range(COUNT_ACCS)))
        return jnp.sum(_tree_sum(list(accs)).astype(F32), axis=0, keepdims=True)

    int_min = jnp.int32(-2 ** 31)

    def order_key(v):
        bits = pltpu.bitcast(v, jnp.int32)
        return jnp.where(bits < 0, -(bits & jnp.int32(0x7FFFFFFF)), bits)

    def from_order_key(key):
        return pltpu.bitcast(jnp.where(key < 0, (-key) | int_min, key), F32)

    t_glob = (i * tq + lax.broadcasted_iota(jnp.int32, (1, tq), 1)).astype(F32)
    n_causal = t_glob + 1.0
    kf = jnp.minimum(float(topk), n_causal)
    all_sel = n_causal <= kf

    lo16 = order_key(rmin) >> 16
    hi16 = (order_key(rmax.astype(BF16).astype(F32) + 0.0) >> 16) + 1
    for _ in range(BISECT_BF16_STEPS):
        lo_v = from_order_key(lo16 << 16)
        hi_v = from_order_key(hi16 << 16)
        mid_val16 = order_key(lo_v + (hi_v - lo_v) * 0.5) >> 16
        mid16 = jnp.where((mid_val16 > lo16) & (mid_val16 < hi16), mid_val16, (lo16 + hi16) >> 1)
        open_ = (hi16 - lo16) > 1
        cm = count16_ge(from_order_key(mid16 << 16).astype(BF16))
        lo16 = jnp.where(open_ & (cm >= kf), mid16, lo16)
        hi16 = jnp.where(open_ & (cm < kf), mid16, hi16)

    min_normal_key = jnp.int32(0x00800000)

    def snap(key, direction):
        sub = (key > -min_normal_key) & (key < min_normal_key) & (key != 0)
        below = jnp.where(key > 0, 0, -min_normal_key)
        above = jnp.where(key > 0, min_normal_key, 0)
        return jnp.where(sub, {"down": below, "up": above, "zero": jnp.zeros_like(key)}[direction], key)

    lo0 = snap(jnp.maximum((lo16 - 1) << 16, order_key(rmin)), "down")
    hi0 = snap(hi16 << 16, "up")
    state0 = (jnp.where(all_sel, 0.0, 1.0), lo0, hi0, count_ge(from_order_key(hi0)),
              order_key(rmin), jnp.zeros((1, tq), F32), jnp.full((1, tq), 0x7F800000, jnp.int32), kf)

    def bisect_step(st, value_mid, force_end):
        active, lo_key, hi_key, fhi, thr_key, tie, hif_key, need = st
        mid_key = snap((lo_key >> 1) + (hi_key >> 1) + (lo_key & hi_key & 1), "zero")
        if value_mid:
            lo = from_order_key(lo_key)
            hi = from_order_key(hi_key)
            val_key = snap(order_key(lo + (hi - lo) * 0.5), "zero")
            mid_key = jnp.where((val_key > lo_key) & (val_key < hi_key), val_key, mid_key)
        collapsed = (mid_key <= lo_key) | (mid_key >= hi_key) | force_end
        cm = count_ge(from_order_key(mid_key))
        act = active > 0.0
        live = act & jnp.logical_not(collapsed)
        found = live & (cm == kf)
        go_up = live & (cm > kf)
        go_dn = live & (cm < kf)
        ends_tie = act & collapsed
        thr_key = jnp.where(found, mid_key, jnp.where(ends_tie, lo_key, thr_key))
        tie = jnp.where(ends_tie, 1.0, tie)
        hif_key = jnp.where(ends_tie, hi_key, hif_key)
        need = jnp.where(ends_tie, kf - fhi, need)
        lo_key = jnp.where(go_up, mid_key, lo_key)
        fhi = jnp.where(go_dn, cm, fhi)
        hi_key = jnp.where(go_dn, mid_key, hi_key)
        active = jnp.where(found | ends_tie, 0.0, active)
        return active, lo_key, hi_key, fhi, thr_key, tie, hif_key, need

    state = state0
    for _ in range(BISECT_VALUE_STEPS):
        state = bisect_step(state, True, False)

    def b_cond(carry):
        st, step = carry
        return (jnp.max(st[0]) > 0.0) & (step <= BISECT_MAX_STEPS)

    def b_group(carry):
        st, step = carry
        for _ in range(BISECT_GROUP):
            st = bisect_step(st, False, step >= BISECT_MAX_STEPS)
        return st, step + BISECT_GROUP

    (_, _, _, _, thr_key, tie, hif_key, need), _ = lax.while_loop(b_cond, b_group, (state, jnp.int32(0)))
    thr = from_order_key(thr_key)
    hif = from_order_key(hif_key)

    @pl.when(jnp.max(tie) > 0.0)
    def _():
        tri = jnp.where(lax.broadcasted_iota(jnp.int32, (tk, tk), 1)
                        <= lax.broadcasted_iota(jnp.int32, (tk, tk), 0), 1.0, 0.0).astype(BF16)

        def body(c, seen):
            sc_c = s_ref[c]
            tied = (sc_c >= thr) & (sc_c < hif) & (tie > 0.0)
            rank = jnp.dot(tri, jnp.where(tied, 1.0, 0.0).astype(BF16), preferred_element_type=F32) + seen
            s_ref[c] = jnp.where(tied & (rank > need), -jnp.inf, sc_c)
            return rank[tk - 1:tk, :]

        lax.fori_loop(0, i + 1, body, jnp.zeros((1, tq), F32))

    acc_ref[...] = jnp.zeros(acc_ref.shape, F32)

    def store_logits(c, slot, bias_idx):
        masked = jnp.where(s_ref[c] >= thr, 0.0, NEG_BIG)
        row0 = pl.multiple_of(c * tk, tk)
        for h in range(nh):
            kc = k_ref[pl.ds(row0, tk), (h // 2) * LANES:(h // 2 + 1) * LANES]
            lt = lax.dot_general(kc, qpad_ref[h], _NT_DIMS, preferred_element_type=F32) + masked
            if bias_idx is not None:
                lt = lt + bias_ref[bias_idx, h]
            lg_ref[slot, h] = lt

    def softmax_pv(c, slot, m_all):
        m_out = []
        for h in range(nh):
            m_old = m_all[h]
            m_new = jnp.maximum(m_old, jnp.max(lg_ref[slot, h], axis=0, keepdims=True))
            p = jnp.exp2(lg_ref[slot, h] - m_new).astype(BF16)
            alpha = jnp.exp2(m_old - m_new)
            pv = jnp.dot(vt_ref[c, h * V_SLAB:(h + 1) * V_SLAB, :], p, preferred_element_type=F32)
            acc_ref[h] = alpha * acc_ref[h] + pv
            m_out.append(m_new)
        return tuple(m_out)

    def near_step(m_all):
        store_logits(i - 1, 1, 1)
        return softmax_pv(i, 0, m_all)

    def far_step(j, parity, m_all):
        c = i - 2 - j
        store_logits(c, parity, None)
        return softmax_pv(c + 1, 1 - parity, m_all)

    def far_pair(jj, m_all):
        return far_step(2 * jj + 1, 1, far_step(2 * jj, 0, m_all))

    n_far = jnp.maximum(i - 1, 0)
    m_all = tuple(jnp.full((1, tq), NEG_BIG, F32) for _ in range(nh))
    store_logits(i, 0, 0)
    m_all = lax.cond(i >= 1, near_step, lambda m: m, m_all)
    m_all = lax.fori_loop(0, n_far // 2, far_pair, m_all)
    m_all = lax.cond((n_far & 1) == 1, lambda m: far_step(n_far - 1, 0, m), lambda m: m, m_all)
    lax.cond((i & 1) == 0, lambda m: softmax_pv(0, 0, m), lambda m: softmax_pv(0, 1, m), m_all)

    for h in range(nh):
        o = acc_ref[h, :hd, :] / acc_ref[h, hd:hd + 1, :]
        ms = jnp.mean(o * o, axis=0, keepdims=True)
        out_ref[h * hd:(h + 1) * hd, :] = o * lax.rsqrt(ms + EPS)
    o_ref[...] = (out_ref[...].T * og_ref[...]).astype(BF16)


def _attn_call(rel_bias, bounds, q, qi, wit, k, ki, vt, og, topk):
    bsz, seq, aw = q.shape
    tq, tk = ATT_TQ, ATT_TK
    nck = seq // tk
    blk_q = lambda b, i: (b, i, 0)
    whole = lambda b, i: (b, 0, 0)
    smem = pl.BlockSpec(memory_space=pltpu.SMEM)
    return pl.pallas_call(
        functools.partial(_attn_kernel, topk=topk),
        out_shape=jax.ShapeDtypeStruct((bsz, seq, aw), BF16),
        grid=(bsz, seq // tq),
        in_specs=[
            smem, smem,
            pl.BlockSpec((None, tq, aw), blk_q),
            pl.BlockSpec((None, tq, aw), blk_q),
            pl.BlockSpec((None, IDX_HEADS, tq), lambda b, i: (b, 0, i)),
            pl.BlockSpec((None, seq, aw), whole),
            pl.BlockSpec((None, seq, LANES), whole),
            pl.BlockSpec((None, nck, ATTN_HEADS * V_SLAB, tk), lambda b, i: (b, 0, 0, 0)),
            pl.BlockSpec(og.shape, lambda b, i: (0, 0)),
        ],
        out_specs=pl.BlockSpec((None, tq, aw), blk_q),
        scratch_shapes=[
            pltpu.VMEM((nck, tk, tq), F32),
            pltpu.VMEM((nck, tk, tq), BF16),
            pltpu.VMEM((2, ATTN_HEADS, tk, tq), F32),
            pltpu.VMEM((ATTN_HEADS, tq, LANES), BF16),
            pltpu.VMEM((IDX_HEADS, tq, LANES), BF16),
            pltpu.VMEM((2, ATTN_HEADS, tk, tq), F32),
            pltpu.VMEM((ATTN_HEADS, V_SLAB, tq), F32),
            pltpu.VMEM((aw, tq), F32),
        ],
        compiler_params=pltpu.CompilerParams(dimension_semantics=("arbitrary", "arbitrary"),
                                             vmem_limit_bytes=VMEM_LIMIT_BYTES),
        name="dsa_attention",
    )(rel_bias, bounds, q, qi, wit, k, ki, vt, og)


def _post_kernel(an_ref, cn_ref, x_ref, mod_ref, n2_ref, woa_ref, woc_ref, wr_ref, br_ref,
                 x1_ref, h2_ref, comb_ref, cnt_ref):
    mix = (jnp.dot(an_ref[...], woa_ref[...], preferred_element_type=F32)
           + jnp.dot(cn_ref[...], woc_ref[...], preferred_element_type=F32))
    x1 = x_ref[...] + mod_ref[2:3, :] * mix
    x1_ref[...] = x1
    ms = jnp.mean(x1 * x1, axis=-1, keepdims=True)
    h2 = x1 * lax.rsqrt(ms + EPS) * n2_ref[...] * (1.0 + mod_ref[4:5, :]) + mod_ref[3:4, :]
    h2b = h2.astype(BF16)
    h2_ref[...] = h2b

    logits = jnp.dot(h2b, wr_ref[...], preferred_element_type=F32) + br_ref[...]
    lane = lax.broadcasted_iota(jnp.int32, logits.shape, 1)
    lane_f = lane.astype(F32)
    far = float(LANES)
    is_g = (lane >= N_EXPERTS) & (lane < N_EXPERTS + N_GROUPS)
    gl = jnp.where(is_g, logits, -jnp.inf)
    gmax = jnp.max(gl, axis=-1, keepdims=True)
    g_sel = jnp.min(jnp.where(is_g & (gl == gmax), lane_f, far), axis=-1, keepdims=True) - float(N_EXPERTS)
    p_g = 1.0 / jnp.sum(jnp.exp(gl - gmax), axis=-1, keepdims=True)

    in_grp = (lane < N_EXPERTS) & ((lane // EXPERTS_PER_GROUP).astype(F32) == g_sel)
    e1 = jnp.where(in_grp, logits, -jnp.inf)
    l1 = jnp.max(e1, axis=-1, keepdims=True)
    i1 = jnp.min(jnp.where(in_grp & (e1 == l1), lane_f, far), axis=-1, keepdims=True)
    rest = in_grp & (lane_f != i1)
    e2 = jnp.where(rest, logits, -jnp.inf)
    l2 = jnp.max(e2, axis=-1, keepdims=True)
    i2 = jnp.min(jnp.where(rest & (e2 == l2), lane_f, far), axis=-1, keepdims=True)
    r = jnp.exp(l2 - l1)
    w1 = 1.0 / (1.0 + r)
    w2 = r / (1.0 + r)
    comb = jnp.where(lane_f == i1, p_g * w1, 0.0) + jnp.where(lane_f == i2, p_g * w2, 0.0)
    comb_ref[...] = comb
    cnt = jnp.sum(jnp.where(comb != 0.0, 1.0, 0.0), axis=0, keepdims=True)
    cnt_ref[...] = jnp.broadcast_to(cnt, cnt_ref.shape)


def _post_call(an, cn, x, mod, n2, woa, woc, wr, br):
    bsz, seq, d = x.shape
    tm = POST_TM
    tok = lambda b, j: (b, j, 0)
    const = lambda b, j: (0, 0)
    return pl.pallas_call(
        _post_kernel,
        out_shape=(jax.ShapeDtypeStruct((bsz, seq, d), F32),
                   jax.ShapeDtypeStruct((bsz, seq, d), BF16),
                   jax.ShapeDtypeStruct((bsz, seq, LANES), F32),
                   jax.ShapeDtypeStruct((bsz, seq // tm, SUBLANES, LANES), F32)),
        grid=(bsz, seq // tm),
        in_specs=[
            pl.BlockSpec((None, tm, ATTN_WIDTH), tok),
            pl.BlockSpec((None, tm, CONV_WIDTH), tok),
            pl.BlockSpec((None, tm, d), tok),
            pl.BlockSpec((None, 6, d), lambda b, j: (b, 0, 0)),
            pl.BlockSpec(n2.shape, const),
            pl.BlockSpec(woa.shape, const),
            pl.BlockSpec(woc.shape, const),
            pl.BlockSpec(wr.shape, const),
            pl.BlockSpec(br.shape, const),
        ],
        out_specs=(pl.BlockSpec((None, tm, d), tok),
                   pl.BlockSpec((None, tm, d), tok),
                   pl.BlockSpec((None, tm, LANES), tok),
                   pl.BlockSpec((None, None, SUBLANES, LANES), lambda b, j: (b, j, 0, 0))),
        compiler_params=pltpu.CompilerParams(dimension_semantics=("arbitrary", "arbitrary"),
                                             vmem_limit_bytes=VMEM_LIMIT_BYTES),
        name="post_router",
    )(an, cn, x, mod, n2, woa, woc, wr, br)


def _strict_tri(n, lower):
    r = lax.broadcasted_iota(jnp.int32, (n, n), 0)
    c = lax.broadcasted_iota(jnp.int32, (n, n), 1)
    return jnp.where((c < r) if lower else (r < c), 1.0, 0.0).astype(BF16)


def _moe_tile_copies(nloc_ref, gtile_ref, blk, local_ref, global_ref, sem, to_global, wait):
    tile = MOE_TILE

    def per_tile(lt, c):
        loc = local_ref.at[pl.ds(pl.multiple_of(lt * tile, tile), tile), :]
        glo = global_ref.at[pl.ds(pl.multiple_of(gtile_ref[blk, lt] * tile, tile), tile), :]
        cp = pltpu.make_async_copy(loc, glo, sem) if to_global else pltpu.make_async_copy(glo, loc, sem)
        if wait:
            cp.wait()
        else:
            cp.start()
        return c

    lax.fori_loop(0, nloc_ref[blk], per_tile, 0)


def _moe_gather_kernel(nloc_ref, gtile_ref, padstart_ref, pad_ref,
                       h2_ref, comb_ref,
                       col_ref, xg_hbm,
                       xg_ref, row_ref, zero_ref, sem):
    blk = pl.program_id(0)
    nb = h2_ref.shape[0]
    tile, chunk = MOE_TILE, MOE_CHUNK
    lane = lax.broadcasted_iota(jnp.int32, (nb, LANES), 1)

    comb = comb_ref[...]
    assigned = comb != 0.0
    a_f = jnp.where(assigned, 1.0, 0.0)
    rank = jnp.dot(_strict_tri(nb, True), a_f.astype(BF16), preferred_element_type=F32)
    cnt = rank[nb - 1:nb, :] + a_f[nb - 1:nb, :]
    ntile = jnp.floor((cnt + float(tile - 1)) * (1.0 / tile))
    first = jnp.dot(jnp.broadcast_to(ntile, (SUBLANES, LANES)).astype(BF16), _strict_tri(LANES, False),
                    preferred_element_type=F32)[0:1, :]
    pos = first * float(tile) + rank
    pos1 = jnp.min(jnp.where(assigned, pos, 1e9), axis=1, keepdims=True)
    pos2 = jnp.max(jnp.where(assigned, pos, -1.0), axis=1, keepdims=True)
    pos2 = jnp.where(pos2 == pos1, -1.0, pos2)
    cw1 = jnp.sum(jnp.where(assigned & (pos == pos1), comb, 0.0), axis=1, keepdims=True)
    cw2 = jnp.sum(jnp.where(assigned & (pos == pos2), comb, 0.0), axis=1, keepdims=True)
    info = jnp.where(lane == 0, pos1, jnp.where(lane == 1, pos2, jnp.where(lane == 2, cw1,
                     jnp.where(lane == 3, cw2, 0.0))))
    col_ref[...] = info
    row_ref[...] = info.T

    n_chunks = (nloc_ref[blk] * tile + (chunk - 1)) // chunk
    p1 = row_ref[0:1, :].astype(jnp.int32)
    p2 = row_ref[1:2, :].astype(jnp.int32)
    sub = lax.broadcasted_iota(jnp.int32, (chunk, nb), 0)

    def gather(c, carry):
        p = sub + c * chunk
        sel = jnp.where((p == p1) | (p == p2), 1.0, 0.0).astype(BF16)
        r0 = pl.multiple_of(c * chunk, chunk)
        xg_ref[pl.ds(r0, chunk), :] = jnp.dot(sel, h2_ref[...], preferred_element_type=F32).astype(BF16)
        return carry

    @pl.when(blk > 0)
    def _():
        _moe_tile_copies(nloc_ref, gtile_ref, blk - 1, xg_ref, xg_hbm, sem, True, True)

    lax.fori_loop(0, n_chunks, gather, 0)

    _moe_tile_copies(nloc_ref, gtile_ref, blk, xg_ref, xg_hbm, sem, True, False)

    is_last = blk == pl.num_programs(0) - 1

    def pad_copies(wait):
        def per_expert(x, carry):
            g0 = padstart_ref[x]

            def per_tile(j, c):
                dst = xg_hbm.at[pl.ds(pl.multiple_of((g0 + j) * tile, tile), tile), :]
                cp = pltpu.make_async_copy(zero_ref, dst, sem)
                if wait:
                    cp.wait()
                else:
                    cp.start()
                return c

            lax.fori_loop(0, pad_ref[x], per_tile, 0)
            return carry

        lax.fori_loop(0, N_EXPERTS, per_expert, 0)

    @pl.when(is_last)
    def _():
        zero_ref[...] = jnp.zeros(zero_ref.shape, BF16)
        pad_copies(False)
        _moe_tile_copies(nloc_ref, gtile_ref, blk, xg_ref, xg_hbm, sem, True, True)
        pad_copies(True)


def _moe_ffn_kernel(texp_ref, nt_ref, x_ref, wg_ref, wu_ref, wd_ref, y_ref):
    @pl.when(pl.program_id(0) < nt_ref[0])
    def _():
        x = x_ref[...]
        a = jnp.dot(x, wg_ref[...].astype(BF16), preferred_element_type=F32)
        up = jnp.dot(x, wu_ref[...].astype(BF16), preferred_element_type=F32)
        hid = ((a * jax.nn.sigmoid(a)) * up).astype(BF16)
        y_ref[...] = jnp.dot(hid, wd_ref[...].astype(BF16), preferred_element_type=F32).astype(BF16)


def _moe_scatter_kernel(nloc_ref, gtile_ref,
                        col_ref, x1_ref, mod_ref, y_hbm,
                        o_ref,
                        y_ref, sem):
    blk = pl.program_id(0)
    nb = x1_ref.shape[0]
    tile, chunk = MOE_TILE, MOE_CHUNK
    slot = blk & 1

    def copies(b, s, wait):
        _moe_tile_copies(nloc_ref, gtile_ref, b, y_ref.at[s], y_hbm, sem.at[s], False, wait)

    @pl.when(blk == 0)
    def _():
        copies(0, 0, False)

    @pl.when(blk + 1 < pl.num_programs(0))
    def _():
        copies(blk + 1, 1 - slot, False)

    total = nloc_ref[blk]
    n_chunks = (total * tile + (chunk - 1)) // chunk
    max_chunks = y_ref.shape[1] // chunk
    usual = n_chunks <= MOE_USUAL_CHUNKS
    n_static = jnp.where(usual, MOE_USUAL_CHUNKS, max_chunks)

    def clear(t, carry):
        y_ref[slot, pl.ds(pl.multiple_of(t * tile, tile), tile), :] = jnp.zeros((tile, y_ref.shape[2]), BF16)
        return carry

    lax.fori_loop(total, n_static * (chunk // tile), clear, 0)

    p1 = col_ref[:, 0:1].astype(jnp.int32)
    p2 = col_ref[:, 1:2].astype(jnp.int32)
    cw1 = col_ref[:, 2:3]
    cw2 = col_ref[:, 3:4]
    gate = mod_ref[5:6, :]
    lane_c = lax.broadcasted_iota(jnp.int32, (nb, chunk), 1)
    copies(blk, slot, True)

    def scatter(n_unrolled):
        acc = None
        for c in range(n_unrolled):
            p = lane_c + c * chunk
            w = (jnp.where(p == p1, cw1, 0.0) + jnp.where(p == p2, cw2, 0.0)).astype(BF16)
            part = jnp.dot(w, y_ref[slot, c * chunk:(c + 1) * chunk, :], preferred_element_type=F32)
            acc = part if acc is None else acc + part
        o_ref[...] = x1_ref[...] + gate * acc

    lax.cond(usual, lambda: scatter(MOE_USUAL_CHUNKS), lambda: scatter(max_chunks))


def _moe_call(h2, comb, cnt_tiles, x1, mod, w_gate, w_up, w_down):
    bsz, seq, d = x1.shape
    nb, tile, ftm = MOE_TM, MOE_TILE, MOE_FFN_TM
    n_tok = bsz * seq
    n_blk = n_tok // nb
    region = ftm // tile
    rows_local = -(-(2 * nb + N_EXPERTS * tile) // MOE_CHUNK) * MOE_CHUNK
    tiles_global = (2 * n_tok) // tile + n_blk * N_EXPERTS + N_EXPERTS * (region - 1)
    n_ffn_max = -(-tiles_global // region)
    rows_global = n_ffn_max * ftm

    cnt = cnt_tiles[:, :, 0, :N_EXPERTS].reshape(n_blk, nb // POST_TM, N_EXPERTS).sum(axis=1).astype(jnp.int32)
    ntile = (cnt + (tile - 1)) // tile
    lfirst = jnp.cumsum(ntile, axis=1) - ntile
    tot = ntile.sum(axis=0)
    ptot = (tot + (region - 1)) // region * region
    ebase = jnp.cumsum(ptot) - ptot
    gfirst = ebase[None, :] + jnp.cumsum(ntile, axis=0) - ntile
    pad = ptot - tot
    n_ffn = (ptot.sum() // region).reshape(1)
    ends = jnp.cumsum(ptot) // region
    texp = jnp.minimum((jnp.arange(n_ffn_max, dtype=jnp.int32)[:, None] >= ends[None, :]).sum(axis=1),
                       N_EXPERTS - 1).astype(jnp.int32)
    nloc = ntile.sum(axis=1).astype(jnp.int32)
    lt = jnp.arange(rows_local // tile, dtype=jnp.int32)[None, :, None]
    in_seg = (lt >= lfirst[:, None, :]) & (lt < (lfirst + ntile)[:, None, :])
    gtile = (jnp.where(in_seg, (gfirst - lfirst)[:, None, :], 0).sum(axis=2) + lt[:, :, 0]).astype(jnp.int32)
    padstart = (gfirst[-1] + ntile[-1]).astype(jnp.int32)

    h2f = h2.reshape(n_tok, d)
    combf = comb.reshape(n_tok, LANES)
    col, xg = pl.pallas_call(
        _moe_gather_kernel,
        out_shape=(jax.ShapeDtypeStruct((n_tok, LANES), F32),
                   jax.ShapeDtypeStruct((rows_global, d), BF16)),
        grid_spec=pltpu.PrefetchScalarGridSpec(
            num_scalar_prefetch=4,
            grid=(n_blk,),
            in_specs=[pl.BlockSpec((nb, d), lambda j, *_: (j, 0)),
                      pl.BlockSpec((nb, LANES), lambda j, *_: (j, 0))],
            out_specs=(pl.BlockSpec((nb, LANES), lambda j, *_: (j, 0)),
                       pl.BlockSpec(memory_space=pl.ANY)),
            scratch_shapes=[
                pltpu.VMEM((rows_local, d), BF16),
                pltpu.VMEM((LANES, nb), F32),
                pltpu.VMEM((tile, d), BF16),
                pltpu.SemaphoreType.DMA,
            ]),
        compiler_params=pltpu.CompilerParams(dimension_semantics=("arbitrary",),
                                             vmem_limit_bytes=VMEM_LIMIT_BYTES),
        name="moe_gather",
    )(nloc, gtile, padstart, pad, h2f, combf)

    last = lambda t, te, nt: jnp.minimum(t, nt[0] - 1)
    y = pl.pallas_call(
        _moe_ffn_kernel,
        out_shape=jax.ShapeDtypeStruct((rows_global, d), BF16),
        grid_spec=pltpu.PrefetchScalarGridSpec(
            num_scalar_prefetch=2,
            grid=(n_ffn_max,),
            in_specs=[pl.BlockSpec((ftm, d), lambda t, te, nt: (last(t, te, nt), 0)),
                      pl.BlockSpec((None, d, EXPERT_FF), lambda t, te, nt: (te[last(t, te, nt)], 0, 0)),
                      pl.BlockSpec((None, d, EXPERT_FF), lambda t, te, nt: (te[last(t, te, nt)], 0, 0)),
                      pl.BlockSpec((None, EXPERT_FF, d), lambda t, te, nt: (te[last(t, te, nt)], 0, 0))],
            out_specs=pl.BlockSpec((ftm, d), lambda t, te, nt: (last(t, te, nt), 0))),
        compiler_params=pltpu.CompilerParams(dimension_semantics=("arbitrary",),
                                             vmem_limit_bytes=VMEM_LIMIT_BYTES),
        name="moe_ffn",
    )(texp, n_ffn, xg, w_gate, w_up, w_down)

    out = pl.pallas_call(
        _moe_scatter_kernel,
        out_shape=jax.ShapeDtypeStruct((n_tok, d), F32),
        grid_spec=pltpu.PrefetchScalarGridSpec(
            num_scalar_prefetch=2,
            grid=(n_blk,),
            in_specs=[pl.BlockSpec((nb, LANES), lambda j, *_: (j, 0)),
                      pl.BlockSpec((nb, d), lambda j, *_: (j, 0)),
                      pl.BlockSpec((None, 6, d), lambda j, *_: ((j * nb) // seq, 0, 0)),
                      pl.BlockSpec(memory_space=pl.ANY)],
            out_specs=pl.BlockSpec((nb, d), lambda j, *_: (j, 0)),
            scratch_shapes=[pltpu.VMEM((2, rows_local, d), BF16),
                            pltpu.SemaphoreType.DMA((2,))]),
        compiler_params=pltpu.CompilerParams(dimension_semantics=("arbitrary",),
                                             vmem_limit_bytes=VMEM_LIMIT_BYTES),
        name="moe_scatter",
    )(nloc, gtile, col, x1.reshape(n_tok, d), mod, y)
    return out.reshape(bsz, seq, d)


def _layer(x, mod, rel_bias, norm1, w_in, q_norm, k_norm, conv_w, attn_out_norm, conv_out_norm, w_out,
           norm2, w_group_router, b_group_router, w_expert_router, b_expert_router, w_gate, w_up, w_down):
    bsz, seq, d = x.shape
    aw = ATTN_WIDTH
    topk = min(TOPK_MAX, seq // 4)

    offs = np.cumsum([0, aw, aw, aw, IDX_HEADS * IDX_DIM, IDX_DIM, IDX_HEADS, CONV_WIDTH, CONV_WIDTH, CONV_WIDTH])
    col = lambda n: w_in[:, int(offs[n]):int(offs[n + 1])]
    wm = jnp.concatenate([col(0), col(1), col(3), col(6), col(7), col(8)], axis=1).astype(BF16)
    wvt = col(2).T.astype(BF16)
    wki = jnp.concatenate([col(4), col(4)], axis=1).astype(BF16)
    wwit = col(5).T.astype(BF16)
    qg = (jnp.tile(q_norm, ATTN_HEADS) * ((HEAD_DIM ** -0.5) * LOG2E))[None, :]
    kg = jnp.tile(k_norm, ATTN_HEADS)[None, :]
    grp = np.arange(aw) // CONV_GROUP_DIM
    gmat = jnp.asarray((grp[:, None] == grp[None, :]).astype(np.float32) / CONV_GROUP_DIM, dtype=BF16)

    q, k, vt, qi, ki, wit, cn = _pre_call(
        x, mod, norm1[None, :], wm, wvt, wki, wwit, qg, kg, conv_w, conv_out_norm.reshape(1, -1), gmat)

    bounds = jnp.asarray(_bucket_boundaries())
    an = _attn_call(rel_bias, bounds, q, qi, wit, k, ki, vt, attn_out_norm.reshape(1, -1), topk)

    wr = jnp.concatenate([w_expert_router, w_group_router,
                          jnp.zeros((d, LANES - N_EXPERTS - N_GROUPS), F32)], axis=1).astype(BF16)
    br = jnp.concatenate([b_expert_router, b_group_router,
                          jnp.zeros((LANES - N_EXPERTS - N_GROUPS,), F32)])[None, :]
    x1, h2, comb, cnt_tiles = _post_call(an, cn, x, mod, norm2[None, :], w_out[:aw].astype(BF16),
                                         w_out[aw:].astype(BF16), wr, br)

    return _moe_call(h2, comb, cnt_tiles, x1, mod, w_gate, w_up, w_down)


def kernel(x, c, rel_bias, w_ada, b_ada, norm1, w_in, q_norm, k_norm, conv_w, attn_out_norm, conv_out_norm,
           w_out, norm2, w_group_router, b_group_router, w_expert_router, b_expert_router, w_gate, w_up,
           w_down):
    bsz, seq, d = x.shape
    assert d == D_MODEL and seq % max(PRE_TM, POST_TM, MOE_TM) == 0 and ATT_TQ == ATT_TK
    depth = w_ada.shape[0]
    for l in range(depth):
        mod = _mod_call(c, w_ada[l], b_ada[l][None, :]).reshape(bsz, 6, d)
        x = _layer(x, mod, rel_bias, norm1[l], w_in[l], q_norm[l], k_norm[l], conv_w[l], attn_out_norm[l],
                   conv_out_norm[l], w_out[l], norm2[l], w_group_router[l], b_group_router[l],
                   w_expert_router[l], b_expert_router[l], w_gate[l], w_up[l], w_down[l])
    return x
```

```python
import functools
import math

import jax
import jax.numpy as jnp
import numpy as np
from jax import lax
from jax.experimental import pallas as pl
from jax.experimental.pallas import tpu as pltpu

F32 = jnp.float32
BF16 = jnp.bfloat16

D_MODEL = 1024
HEAD_DIM = 64
ATTN_HEADS = 8
ATTN_WIDTH = ATTN_HEADS * HEAD_DIM
CONV_WIDTH = D_MODEL - ATTN_WIDTH
CONV_GROUP_DIM = 64
CONV_K = 3
IDX_HEADS = 8
IDX_DIM = 64
TOPK_MAX = 256
IDX_SCALE = (IDX_DIM ** -0.5) * (IDX_HEADS ** -0.5)
N_BUCKETS = 32
MAX_DISTANCE = 128
N_GROUPS = 4
EXPERTS_PER_GROUP = 8
N_EXPERTS = N_GROUPS * EXPERTS_PER_GROUP
EXPERT_FF = 256
EPS = 1e-6
LOG2E = 1.4426950408889634
NEG_BIG = -1e30
COUNT_ACCS = 4
BISECT_GROUP = 2
BISECT_BF16_STEPS = 10
BISECT_VALUE_STEPS = 4
BISECT_MAX_STEPS = 64

LANES = 128
SUBLANES = 8
BF16_SUBLANES = 16
V_SLAB = HEAD_DIM + BF16_SUBLANES
VMEM_LIMIT_BYTES = 56 * 1024 * 1024

PRE_TM = 512
ATT_TQ = 256
ATT_TK = 256
POST_TM = 512
MOE_TM = 512
MOE_TILE = 32
MOE_CHUNK = 512
MOE_USUAL_CHUNKS = 3
MOE_FFN_TM = 1024
MOD_TN = 1536

_NT_DIMS = (((1,), (1,)), ((), ()))


def _tree_sum(parts):
    while len(parts) > 1:
        nxt = [parts[j] + parts[j + 1] for j in range(0, len(parts) - 1, 2)]
        if len(parts) % 2:
            nxt.append(parts[-1])
        parts = nxt
    return parts[0]


def _bucket_boundaries():
    max_exact = N_BUCKETS // 2
    d = np.arange(0, 4 * MAX_DISTANCE, dtype=np.int64)
    nf = np.maximum(d, 1).astype(np.float32)
    large = max_exact + (np.log(nf / np.float32(max_exact)) / np.float32(math.log(MAX_DISTANCE / max_exact))
                         * np.float32(N_BUCKETS - max_exact)).astype(np.int32)
    large = np.minimum(large, N_BUCKETS - 1)
    bucket = np.where(d < max_exact, d, large)
    assert np.all(np.diff(bucket) >= 0) and bucket[-1] == N_BUCKETS - 1
    bounds = [int(np.argmax(bucket >= j)) for j in range(1, N_BUCKETS)]
    return np.asarray([0] + bounds, dtype=np.int32)


def _mod_kernel(c_ref, w_ref, b_ref, o_ref):
    c = c_ref[...]
    act = c * jax.nn.sigmoid(c)
    o_ref[...] = jnp.dot(act, w_ref[...], preferred_element_type=F32,
                         precision=lax.Precision.HIGHEST) + b_ref[...]


def _mod_call(c, w_ada, b_ada):
    bsz, d = c.shape
    n = w_ada.shape[1]
    return pl.pallas_call(
        _mod_kernel,
        out_shape=jax.ShapeDtypeStruct((bsz, n), F32),
        grid=(n // MOD_TN,),
        in_specs=[pl.BlockSpec((bsz, d), lambda j: (0, 0)),
                  pl.BlockSpec((d, MOD_TN), lambda j: (0, j)),
                  pl.BlockSpec((1, MOD_TN), lambda j: (0, j))],
        out_specs=pl.BlockSpec((bsz, MOD_TN), lambda j: (0, j)),
        compiler_params=pltpu.CompilerParams(dimension_semantics=("arbitrary",),
                                             vmem_limit_bytes=VMEM_LIMIT_BYTES),
        name="adaln_mod",
    )(c, w_ada, b_ada)


def _group_rms(y, g_ref):
    ms = jnp.dot((y * y).astype(BF16), g_ref[...], preferred_element_type=F32)
    return y * lax.rsqrt(ms + EPS)


def _pre_kernel(x_ref, mod_ref, n1_ref, wm_ref, wvt_ref, wki_ref, wwit_ref, qg_ref, kg_ref,
                cw_ref, cg_ref, g_ref,
                q_ref, k_ref, vt_ref, qi_ref, ki_ref, wit_ref, cn_ref, carry_ref):
    j = pl.program_id(1)
    tm = x_ref.shape[0]
    aw = ATTN_WIDTH

    x = x_ref[...]
    ms = jnp.mean(x * x, axis=-1, keepdims=True)
    y = x * lax.rsqrt(ms + EPS) * n1_ref[...]
    h = y * (1.0 + mod_ref[1:2, :]) + mod_ref[0:1, :]
    hb = h.astype(BF16)

    def proj(lo):
        return jnp.dot(hb, wm_ref[:, lo:lo + aw], preferred_element_type=F32)

    q = _group_rms(proj(0), g_ref) * qg_ref[...]
    q_ref[...] = q.astype(BF16)
    k = _group_rms(proj(aw), g_ref) * kg_ref[...]
    k_ref[...] = k.astype(BF16)

    vt = lax.dot_general(wvt_ref[...], hb, _NT_DIMS, preferred_element_type=F32).astype(BF16)
    ones = jnp.ones((BF16_SUBLANES, ATT_TK), BF16)
    for cc in range(tm // ATT_TK):
        for hh in range(ATTN_HEADS):
            vt_ref[cc, hh * V_SLAB:hh * V_SLAB + HEAD_DIM, :] = (
                vt[hh * HEAD_DIM:(hh + 1) * HEAD_DIM, cc * ATT_TK:(cc + 1) * ATT_TK])
            vt_ref[cc, hh * V_SLAB + HEAD_DIM:(hh + 1) * V_SLAB, :] = ones

    qi_ref[...] = proj(2 * aw).astype(BF16)
    ki_ref[...] = jnp.dot(hb, wki_ref[...], preferred_element_type=F32).astype(BF16)
    wit_ref[...] = lax.dot_general(wwit_ref[...], hb, _NT_DIMS, preferred_element_type=F32) * IDX_SCALE

    assert CONV_K == 3 and cw_ref.shape[0] == CONV_K
    gate_b = proj(3 * aw)
    z = proj(4 * aw) * proj(5 * aw)

    @pl.when(j == 0)
    def _():
        carry_ref[...] = jnp.zeros_like(carry_ref)

    prev = carry_ref[...]
    row = lax.broadcasted_iota(jnp.int32, z.shape, 0)
    z1 = jnp.where(row == 0, prev[SUBLANES - 1:SUBLANES, :], pltpu.roll(z, 1, 0))
    z2 = pltpu.roll(z, 2, 0)
    z2 = jnp.where(row == 0, prev[SUBLANES - 2:SUBLANES - 1, :], z2)
    z2 = jnp.where(row == 1, prev[SUBLANES - 1:SUBLANES, :], z2)
    carry_ref[...] = z[tm - SUBLANES:, :]
    conv = cw_ref[2:3, :] * z + cw_ref[1:2, :] * z1 + cw_ref[0:1, :] * z2
    yc = gate_b * conv
    cn_ref[...] = (_group_rms(yc, g_ref) * cg_ref[...]).astype(BF16)


def _pre_call(x, mod, n1, wm, wvt, wki, wwit, qg, kg, cw, cg, gmat):
    bsz, seq, d = x.shape
    tm = PRE_TM
    nck = tm // ATT_TK
    aw = ATTN_WIDTH
    const = lambda b, j: (0, 0)
    tok = lambda b, j: (b, j, 0)
    out_shape = (
        jax.ShapeDtypeStruct((bsz, seq, aw), BF16),
        jax.ShapeDtypeStruct((bsz, seq, aw), BF16),
        jax.ShapeDtypeStruct((bsz, seq // ATT_TK, ATTN_HEADS * V_SLAB, ATT_TK), BF16),
        jax.ShapeDtypeStruct((bsz, seq, aw), BF16),
        jax.ShapeDtypeStruct((bsz, seq, LANES), BF16),
        jax.ShapeDtypeStruct((bsz, IDX_HEADS, seq), F32),
        jax.ShapeDtypeStruct((bsz, seq, CONV_WIDTH), BF16),
    )
    out_specs = (
        pl.BlockSpec((None, tm, aw), tok),
        pl.BlockSpec((None, tm, aw), tok),
        pl.BlockSpec((None, nck, ATTN_HEADS * V_SLAB, ATT_TK), lambda b, j: (b, j, 0, 0)),
        pl.BlockSpec((None, tm, aw), tok),
        pl.BlockSpec((None, tm, LANES), tok),
        pl.BlockSpec((None, IDX_HEADS, tm), lambda b, j: (b, 0, j)),
        pl.BlockSpec((None, tm, CONV_WIDTH), tok),
    )
    in_specs = [
        pl.BlockSpec((None, tm, d), tok),
        pl.BlockSpec((None, 6, d), lambda b, j: (b, 0, 0)),
        pl.BlockSpec(n1.shape, const),
        pl.BlockSpec(wm.shape, const),
        pl.BlockSpec(wvt.shape, const),
        pl.BlockSpec(wki.shape, const),
        pl.BlockSpec(wwit.shape, const),
        pl.BlockSpec(qg.shape, const),
        pl.BlockSpec(kg.shape, const),
        pl.BlockSpec(cw.shape, const),
        pl.BlockSpec(cg.shape, const),
        pl.BlockSpec(gmat.shape, const),
    ]
    return pl.pallas_call(
        _pre_kernel,
        out_shape=out_shape,
        grid=(bsz, seq // tm),
        in_specs=in_specs,
        out_specs=out_specs,
        scratch_shapes=[pltpu.VMEM((SUBLANES, CONV_WIDTH), F32)],
        compiler_params=pltpu.CompilerParams(dimension_semantics=("arbitrary", "arbitrary"),
                                             vmem_limit_bytes=VMEM_LIMIT_BYTES),
        name="pre_proj",
    )(x, mod, n1, wm, wvt, wki, wwit, qg, kg, cw, cg, gmat)


def _attn_kernel(rb_ref, bnd_ref, q_ref, qi_ref, wit_ref, k_ref, ki_ref, vt_ref, og_ref,
                 o_ref,
                 s_ref, s16_ref, bias_ref, qpad_ref, qipad_ref, lg_ref, acc_ref, out_ref, *, topk):
    b = pl.program_id(0)
    i = pl.program_id(1)
    tq, tk = ATT_TQ, ATT_TK
    nh, hd = ATTN_HEADS, HEAD_DIM

    t_loc = lax.broadcasted_iota(jnp.int32, (tk, tq), 1)
    s_loc = lax.broadcasted_iota(jnp.int32, (tk, tq), 0)

    @pl.when((b == 0) & (i == 0))
    def _():
        for idx in range(2):
            dist = t_loc - s_loc + idx * tq
            for h in range(nh):
                bias_ref[idx, h] = jnp.full((tk, tq), (rb_ref[0, h] - rb_ref[N_BUCKETS - 1, h]) * LOG2E, F32)

            def fill(jb, carry):
                reached = dist >= bnd_ref[jb]
                for h in range(nh):
                    val = (rb_ref[jb, h] - rb_ref[N_BUCKETS - 1, h]) * LOG2E
                    bias_ref[idx, h] = jnp.where(reached, val, bias_ref[idx, h])
                return carry

            lax.fori_loop(1, N_BUCKETS, fill, 0)

    lane = lax.broadcasted_iota(jnp.int32, (tq, LANES), 1)
    for h in range(nh):
        pair = slice((h // 2) * LANES, (h // 2 + 1) * LANES)
        keep = (lane // hd) == (h % 2)
        qpad_ref[h] = jnp.where(keep, q_ref[:, pair], jnp.zeros((), BF16))
        qipad_ref[h] = jnp.where(keep, qi_ref[:, pair], jnp.zeros((), BF16))

    def idx_dots(c, slot):
        kic = ki_ref[pl.ds(pl.multiple_of(c * tk, tk), tk), :]
        for h in range(nh):
            lg_ref[slot, h] = lax.dot_general(kic, qipad_ref[h], _NT_DIMS, preferred_element_type=F32)

    def idx_reduce(c, slot, carry, diagonal):
        rmin, rmax = carry
        sc = _tree_sum([wit_ref[h:h + 1, :] * jnp.maximum(lg_ref[slot, h], 0.0) for h in range(nh)])
        if diagonal:
            causal = s_loc <= t_loc
            lo_c, hi_c = jnp.where(causal, sc, jnp.inf), jnp.where(causal, sc, -jnp.inf)
            sc = hi_c
        else:
            lo_c, hi_c = sc, sc
        s_ref[c] = sc
        s16_ref[c] = sc.astype(BF16)
        return (jnp.minimum(rmin, jnp.min(lo_c, axis=0, keepdims=True)),
                jnp.maximum(rmax, jnp.max(hi_c, axis=0, keepdims=True)))

    def idx_pair(jj, carry):
        idx_dots(2 * jj + 1, 1)
        carry = idx_reduce(2 * jj, 0, carry, False)
        idx_dots(2 * jj + 2, 0)
        return idx_reduce(2 * jj + 1, 1, carry, False)

    def idx_tail_odd(carry):
        idx_dots(i, 1)
        return idx_reduce(i, 1, idx_reduce(i - 1, 0, carry, False), True)

    idx_dots(0, 0)
    carry = (jnp.full((1, tq), jnp.inf, F32), jnp.full((1, tq), -jnp.inf, F32))
    carry = lax.fori_loop(0, i // 2, idx_pair, carry)
    rmin, rmax = lax.cond((i & 1) == 1, idx_tail_odd, lambda cr: idx_reduce(i, 0, cr, True), carry)

    def count_ge(thr):
        def body(c, accs):
            hit = s_ref[c] >= thr
            accs = list(accs)
            for r in range(tk // SUBLANES):
                a = accs[r % COUNT_ACCS]
                accs[r % COUNT_ACCS] = jnp.where(hit[r * SUBLANES:(r + 1) * SUBLANES], a + 1.0, a)
            return tuple(accs)
        accs = lax.fori_loop(0, i + 1, body,
                             tuple(jnp.zeros((SUBLANES, tq), F32) for _ in range(COUNT_ACCS)))
        return jnp.sum(_tree_sum(list(accs)), axis=0, keepdims=True)

    def count16_ge(thr16):
        def body(c, accs):
            hit = s16_ref[c] >= thr16
            accs = list(accs)
            for r in range(tk // BF16_SUBLANES):
                a = accs[r % COUNT_ACCS]
                accs[r % COUNT_ACCS] = jnp.where(hit[r * BF16_SUBLANES:(r + 1) * BF16_SUBLANES], a + 1, a)
            return tuple(accs)
        accs = lax.fori_loop(0, i + 1, body,
                             tuple(jnp.zeros((BF16_SUBLANES, tq), BF16) for _ in range(COUNT_ACCS)))
        return jnp.sum(_tree_sum(list(accs)).astype(F32), axis=0, keepdims=True)

    int_min = jnp.int32(-2 ** 31)

    def order_key(v):
        bits = pltpu.bitcast(v, jnp.int32)
        return jnp.where(bits < 0, -(bits & jnp.int32(0x7FFFFFFF)), bits)

    def from_order_key(key):
        return pltpu.bitcast(jnp.where(key < 0, (-key) | int_min, key), F32)

    t_glob = (i * tq + lax.broadcasted_iota(jnp.int32, (1, tq), 1)).astype(F32)
    n_causal = t_glob + 1.0
    kf = jnp.minimum(float(topk), n_causal)
    all_sel = n_causal <= kf

    lo16 = order_key(rmin) >> 16
    hi16 = (order_key(rmax.astype(BF16).astype(F32) + 0.0) >> 16) + 1
    for _ in range(BISECT_BF16_STEPS):
        lo_v = from_order_key(lo16 << 16)
        hi_v = from_order_key(hi16 << 16)
        mid_val16 = order_key(lo_v + (hi_v - lo_v) * 0.5) >> 16
        mid16 = jnp.where((mid_val16 > lo16) & (mid_val16 < hi16), mid_val16, (lo16 + hi16) >> 1)
        open_ = (hi16 - lo16) > 1
        cm = count16_ge(from_order_key(mid16 << 16).astype(BF16))
        lo16 = jnp.where(open_ & (cm >= kf), mid16, lo16)
        hi16 = jnp.where(open_ & (cm < kf), mid16, hi16)

    min_normal_key = jnp.int32(0x00800000)

    def snap(key, direction):
        sub = (key > -min_normal_key) & (key < min_normal_key) & (key != 0)
        below = jnp.where(key > 0, 0, -min_normal_key)
        above = jnp.where(key > 0, min_normal_key, 0)
        return jnp.where(sub, {"down": below, "up": above, "zero": jnp.zeros_like(key)}[direction], key)

    lo0 = snap(jnp.maximum((lo16 - 1) << 16, order_key(rmin)), "down")
    hi0 = snap(hi16 << 16, "up")
    state0 = (jnp.where(all_sel, 0.0, 1.0), lo0, hi0, count_ge(from_order_key(hi0)),
              order_key(rmin), jnp.zeros((1, tq), F32), jnp.full((1, tq), 0x7F800000, jnp.int32), kf)

    def bisect_step(st, value_mid, force_end):
        active, lo_key, hi_key, fhi, thr_key, tie, hif_key, need = st
        mid_key = snap((lo_key >> 1) + (hi_key >> 1) + (lo_key & hi_key & 1), "zero")
        if value_mid:
            lo = from_order_key(lo_key)
            hi = from_order_key(hi_key)
            val_key = snap(order_key(lo + (hi - lo) * 0.5), "zero")
            mid_key = jnp.where((val_key > lo_key) & (val_key < hi_key), val_key, mid_key)
        collapsed = (mid_key <= lo_key) | (mid_key >= hi_key) | force_end
        cm = count_ge(from_order_key(mid_key))
        act = active > 0.0
        live = act & jnp.logical_not(collapsed)
        found = live & (cm == kf)
        go_up = live & (cm > kf)
        go_dn = live & (cm < kf)
        ends_tie = act & collapsed
        thr_key = jnp.where(found, mid_key, jnp.where(ends_tie, lo_key, thr_key))
        tie = jnp.where(ends_tie, 1.0, tie)
        hif_key = jnp.where(ends_tie, hi_key, hif_key)
        need = jnp.where(ends_tie, kf - fhi, need)
        lo_key = jnp.where(go_up, mid_key, lo_key)
        fhi = jnp.where(go_dn, cm, fhi)
        hi_key = jnp.where(go_dn, mid_key, hi_key)
        active = jnp.where(found | ends_tie, 0.0, active)
        return active, lo_key, hi_key, fhi, thr_key, tie, hif_key, need

    state = state0
    for _ in range(BISECT_VALUE_STEPS):
        state = bisect_step(state, True, False)

    def b_cond(carry):
        st, step = carry
        return (jnp.max(st[0]) > 0.0) & (step <= BISECT_MAX_STEPS)

    def b_group(carry):
        st, step = carry
        for _ in range(BISECT_GROUP):
            st = bisect_step(st, False, step >= BISECT_MAX_STEPS)
        return st, step + BISECT_GROUP

    (_, _, _, _, thr_key, tie, hif_key, need), _ = lax.while_loop(b_cond, b_group, (state, jnp.int32(0)))
    thr = from_order_key(thr_key)
    hif = from_order_key(hif_key)

    @pl.when(jnp.max(tie) > 0.0)
    def _():
        tri = jnp.where(lax.broadcasted_iota(jnp.int32, (tk, tk), 1)
                        <= lax.broadcasted_iota(jnp.int32, (tk, tk), 0), 1.0, 0.0).astype(BF16)

        def body(c, seen):
            sc_c = s_ref[c]
            tied = (sc_c >= thr) & (sc_c < hif) & (tie > 0.0)
            rank = jnp.dot(tri, jnp.where(tied, 1.0, 0.0).astype(BF16), preferred_element_type=F32) + seen
            s_ref[c] = jnp.where(tied & (rank > need), -jnp.inf, sc_c)
            return rank[tk - 1:tk, :]

        lax.fori_loop(0, i + 1, body, jnp.zeros((1, tq), F32))

    acc_ref[...] = jnp.zeros(acc_ref.shape, F32)

    def store_logits(c, slot, bias_idx):
        masked = jnp.where(s_ref[c] >= thr, 0.0, NEG_BIG)
        row0 = pl.multiple_of(c * tk, tk)
        for h in range(nh):
            kc = k_ref[pl.ds(row0, tk), (h // 2) * LANES:(h // 2 + 1) * LANES]
            lt = lax.dot_general(kc, qpad_ref[h], _NT_DIMS, preferred_element_type=F32) + masked
            if bias_idx is not None:
                lt = lt + bias_ref[bias_idx, h]
            lg_ref[slot, h] = lt

    def softmax_pv(c, slot, m_all):
        m_out = []
        for h in range(nh):
            m_old = m_all[h]
            m_new = jnp.maximum(m_old, jnp.max(lg_ref[slot, h], axis=0, keepdims=True))
            p = jnp.exp2(lg_ref[slot, h] - m_new).astype(BF16)
            alpha = jnp.exp2(m_old - m_new)
            pv = jnp.dot(vt_ref[c, h * V_SLAB:(h + 1) * V_SLAB, :], p, preferred_element_type=F32)
            acc_ref[h] = alpha * acc_ref[h] + pv
            m_out.append(m_new)
        return tuple(m_out)

    def near_step(m_all):
        store_logits(i - 1, 1, 1)
        return softmax_pv(i, 0, m_all)

    def far_step(j, parity, m_all):
        c = i - 2 - j
        store_logits(c, parity, None)
        return softmax_pv(c + 1, 1 - parity, m_all)

    def far_pair(jj, m_all):
        return far_step(2 * jj + 1, 1, far_step(2 * jj, 0, m_all))

    n_far = jnp.maximum(i - 1, 0)
    m_all = tuple(jnp.full((1, tq), NEG_BIG, F32) for _ in range(nh))
    store_logits(i, 0, 0)
    m_all = lax.cond(i >= 1, near_step, lambda m: m, m_all)
    m_all = lax.fori_loop(0, n_far // 2, far_pair, m_all)
    m_all = lax.cond((n_far & 1) == 1, lambda m: far_step(n_far - 1, 0, m), lambda m: m, m_all)
    lax.cond((i & 1) == 0, lambda m: softmax_pv(0, 0, m), lambda m: softmax_pv(0, 1, m), m_all)

    for h in range(nh):
        o = acc_ref[h, :hd, :] / acc_ref[h, hd:hd + 1, :]
        ms = jnp.mean(o * o, axis=0, keepdims=True)
        out_ref[h * hd:(h + 1) * hd, :] = o * lax.rsqrt(ms + EPS)
    o_ref[...] = (out_ref[...].T * og_ref[...]).astype(BF16)


def _attn_call(rel_bias, bounds, q, qi, wit, k, ki, vt, og, topk):
    bsz, seq, aw = q.shape
    tq, tk = ATT_TQ, ATT_TK
    nck = seq // tk
    blk_q = lambda b, i: (b, i, 0)
    whole = lambda b, i: (b, 0, 0)
    smem = pl.BlockSpec(memory_space=pltpu.SMEM)
    return pl.pallas_call(
        functools.partial(_attn_kernel, topk=topk),
        out_shape=jax.ShapeDtypeStruct((bsz, seq, aw), BF16),
        grid=(bsz, seq // tq),
        in_specs=[
            smem, smem,
            pl.BlockSpec((None, tq, aw), blk_q),
            pl.BlockSpec((None, tq, aw), blk_q),
            pl.BlockSpec((None, IDX_HEADS, tq), lambda b, i: (b, 0, i)),
            pl.BlockSpec((None, seq, aw), whole),
            pl.BlockSpec((None, seq, LANES), whole),
            pl.BlockSpec((None, nck, ATTN_HEADS * V_SLAB, tk), lambda b, i: (b, 0, 0, 0)),
            pl.BlockSpec(og.shape, lambda b, i: (0, 0)),
        ],
        out_specs=pl.BlockSpec((None, tq, aw), blk_q),
        scratch_shapes=[
            pltpu.VMEM((nck, tk, tq), F32),
            pltpu.VMEM((nck, tk, tq), BF16),
            pltpu.VMEM((2, ATTN_HEADS, tk, tq), F32),
            pltpu.VMEM((ATTN_HEADS, tq, LANES), BF16),
            pltpu.VMEM((IDX_HEADS, tq, LANES), BF16),
            pltpu.VMEM((2, ATTN_HEADS, tk, tq), F32),
            pltpu.VMEM((ATTN_HEADS, V_SLAB, tq), F32),
            pltpu.VMEM((aw, tq), F32),
        ],
        compiler_params=pltpu.CompilerParams(dimension_semantics=("arbitrary", "arbitrary"),
                                             vmem_limit_bytes=VMEM_LIMIT_BYTES),
        name="dsa_attention",
    )(rel_bias, bounds, q, qi, wit, k, ki, vt, og)


def _post_kernel(an_ref, cn_ref, x_ref, mod_ref, n2_ref, woa_ref, woc_ref, wr_ref, br_ref,
                 x1_ref, h2_ref, comb_ref, cnt_ref):
    mix = (jnp.dot(an_ref[...], woa_ref[...], preferred_element_type=F32)
           + jnp.dot(cn_ref[...], woc_ref[...], preferred_element_type=F32))
    x1 = x_ref[...] + mod_ref[2:3, :] * mix
    x1_ref[...] = x1
    ms = jnp.mean(x1 * x1, axis=-1, keepdims=True)
    h2 = x1 * lax.rsqrt(ms + EPS) * n2_ref[...] * (1.0 + mod_ref[4:5, :]) + mod_ref[3:4, :]
    h2b = h2.astype(BF16)
    h2_ref[...] = h2b

    logits = jnp.dot(h2b, wr_ref[...], preferred_element_type=F32) + br_ref[...]
    lane = lax.broadcasted_iota(jnp.int32, logits.shape, 1)
    lane_f = lane.astype(F32)
    far = float(LANES)
    is_g = (lane >= N_EXPERTS) & (lane < N_EXPERTS + N_GROUPS)
    gl = jnp.where(is_g, logits, -jnp.inf)
    gmax = jnp.max(gl, axis=-1, keepdims=True)
    g_sel = jnp.min(jnp.where(is_g & (gl == gmax), lane_f, far), axis=-1, keepdims=True) - float(N_EXPERTS)
    p_g = 1.0 / jnp.sum(jnp.exp(gl - gmax), axis=-1, keepdims=True)

    in_grp = (lane < N_EXPERTS) & ((lane // EXPERTS_PER_GROUP).astype(F32) == g_sel)
    e1 = jnp.where(in_grp, logits, -jnp.inf)
    l1 = jnp.max(e1, axis=-1, keepdims=True)
    i1 = jnp.min(jnp.where(in_grp & (e1 == l1), lane_f, far), axis=-1, keepdims=True)
    rest = in_grp & (lane_f != i1)
    e2 = jnp.where(rest, logits, -jnp.inf)
    l2 = jnp.max(e2, axis=-1, keepdims=True)
    i2 = jnp.min(jnp.where(rest & (e2 == l2), lane_f, far), axis=-1, keepdims=True)
    r = jnp.exp(l2 - l1)
    w1 = 1.0 / (1.0 + r)
    w2 = r / (1.0 + r)
    comb = jnp.where(lane_f == i1, p_g * w1, 0.0) + jnp.where(lane_f == i2, p_g * w2, 0.0)
    comb_ref[...] = comb
    cnt = jnp.sum(jnp.where(comb != 0.0, 1.0, 0.0), axis=0, keepdims=True)
    cnt_ref[...] = jnp.broadcast_to(cnt, cnt_ref.shape)


def _post_call(an, cn, x, mod, n2, woa, woc, wr, br):
    bsz, seq, d = x.shape
    tm = POST_TM
    tok = lambda b, j: (b, j, 0)
    const = lambda b, j: (0, 0)
    return pl.pallas_call(
        _post_kernel,
        out_shape=(jax.ShapeDtypeStruct((bsz, seq, d), F32),
                   jax.ShapeDtypeStruct((bsz, seq, d), BF16),
                   jax.ShapeDtypeStruct((bsz, seq, LANES), F32),
                   jax.ShapeDtypeStruct((bsz, seq // tm, SUBLANES, LANES), F32)),
        grid=(bsz, seq // tm),
        in_specs=[
            pl.BlockSpec((None, tm, ATTN_WIDTH), tok),
            pl.BlockSpec((None, tm, CONV_WIDTH), tok),
            pl.BlockSpec((None, tm, d), tok),
            pl.BlockSpec((None, 6, d), lambda b, j: (b, 0, 0)),
            pl.BlockSpec(n2.shape, const),
            pl.BlockSpec(woa.shape, const),
            pl.BlockSpec(woc.shape, const),
            pl.BlockSpec(wr.shape, const),
            pl.BlockSpec(br.shape, const),
        ],
        out_specs=(pl.BlockSpec((None, tm, d), tok),
                   pl.BlockSpec((None, tm, d), tok),
                   pl.BlockSpec((None, tm, LANES), tok),
                   pl.BlockSpec((None, None, SUBLANES, LANES), lambda b, j: (b, j, 0, 0))),
        compiler_params=pltpu.CompilerParams(dimension_semantics=("arbitrary", "arbitrary"),
                                             vmem_limit_bytes=VMEM_LIMIT_BYTES),
        name="post_router",
    )(an, cn, x, mod, n2, woa, woc, wr, br)


def _strict_tri(n, lower):
    r = lax.broadcasted_iota(jnp.int32, (n, n), 0)
    c = lax.broadcasted_iota(jnp.int32, (n, n), 1)
    return jnp.where((c < r) if lower else (r < c), 1.0, 0.0).astype(BF16)


def _moe_tile_copies(nloc_ref, gtile_ref, blk, local_ref, global_ref, sem, to_global, wait):
    tile = MOE_TILE

    def per_tile(lt, c):
        loc = local_ref.at[pl.ds(pl.multiple_of(lt * tile, tile), tile), :]
        glo = global_ref.at[pl.ds(pl.multiple_of(gtile_ref[blk, lt] * tile, tile), tile), :]
        cp = pltpu.make_async_copy(loc, glo, sem) if to_global else pltpu.make_async_copy(glo, loc, sem)
        if wait:
            cp.wait()
        else:
            cp.start()
        return c

    lax.fori_loop(0, nloc_ref[blk], per_tile, 0)


def _moe_gather_kernel(nloc_ref, gtile_ref, padstart_ref, pad_ref,
                       h2_ref, comb_ref,
                       col_ref, xg_hbm,
                       xg_ref, row_ref, zero_ref, sem):
    blk = pl.program_id(0)
    nb = h2_ref.shape[0]
    tile, chunk = MOE_TILE, MOE_CHUNK
    lane = lax.broadcasted_iota(jnp.int32, (nb, LANES), 1)

    comb = comb_ref[...]
    assigned = comb != 0.0
    a_f = jnp.where(assigned, 1.0, 0.0)
    rank = jnp.dot(_strict_tri(nb, True), a_f.astype(BF16), preferred_element_type=F32)
    cnt = rank[nb - 1:nb, :] + a_f[nb - 1:nb, :]
    ntile = jnp.floor((cnt + float(tile - 1)) * (1.0 / tile))
    first = jnp.dot(jnp.broadcast_to(ntile, (SUBLANES, LANES)).astype(BF16), _strict_tri(LANES, False),
                    preferred_element_type=F32)[0:1, :]
    pos = first * float(tile) + rank
    pos1 = jnp.min(jnp.where(assigned, pos, 1e9), axis=1, keepdims=True)
    pos2 = jnp.max(jnp.where(assigned, pos, -1.0), axis=1, keepdims=True)
    pos2 = jnp.where(pos2 == pos1, -1.0, pos2)
    cw1 = jnp.sum(jnp.where(assigned & (pos == pos1), comb, 0.0), axis=1, keepdims=True)
    cw2 = jnp.sum(jnp.where(assigned & (pos == pos2), comb, 0.0), axis=1, keepdims=True)
    info = jnp.where(lane == 0, pos1, jnp.where(lane == 1, pos2, jnp.where(lane == 2, cw1,
                     jnp.where(lane == 3, cw2, 0.0))))
    col_ref[...] = info
    row_ref[...] = info.T

    n_chunks = (nloc_ref[blk] * tile + (chunk - 1)) // chunk
    p1 = row_ref[0:1, :].astype(jnp.int32)
    p2 = row_ref[1:2, :].astype(jnp.int32)
    sub = lax.broadcasted_iota(jnp.int32, (chunk, nb), 0)

    def gather(c, carry):
        p = sub + c * chunk
        sel = jnp.where((p == p1) | (p == p2), 1.0, 0.0).astype(BF16)
        r0 = pl.multiple_of(c * chunk, chunk)
        xg_ref[pl.ds(r0, chunk), :] = jnp.dot(sel, h2_ref[...], preferred_element_type=F32).astype(BF16)
        return carry

    @pl.when(blk > 0)
    def _():
        _moe_tile_copies(nloc_ref, gtile_ref, blk - 1, xg_ref, xg_hbm, sem, True, True)

    lax.fori_loop(0, n_chunks, gather, 0)

    _moe_tile_copies(nloc_ref, gtile_ref, blk, xg_ref, xg_hbm, sem, True, False)

    is_last = blk == pl.num_programs(0) - 1

    def pad_copies(wait):
        def per_expert(x, carry):
            g0 = padstart_ref[x]

            def per_tile(j, c):
                dst = xg_hbm.at[pl.ds(pl.multiple_of((g0 + j) * tile, tile), tile), :]
                cp = pltpu.make_async_copy(zero_ref, dst, sem)
                if wait:
                    cp.wait()
                else:
                    cp.start()
                return c

            lax.fori_loop(0, pad_ref[x], per_tile, 0)
            return carry

        lax.fori_loop(0, N_EXPERTS, per_expert, 0)

    @pl.when(is_last)
    def _():
        zero_ref[...] = jnp.zeros(zero_ref.shape, BF16)
        pad_copies(False)
        _moe_tile_copies(nloc_ref, gtile_ref, blk, xg_ref, xg_hbm, sem, True, True)
        pad_copies(True)


def _moe_ffn_kernel(texp_ref, nt_ref, x_ref, wg_ref, wu_ref, wd_ref, y_ref):
    @pl.when(pl.program_id(0) < nt_ref[0])
    def _():
        x = x_ref[...]
        a = jnp.dot(x, wg_ref[...].astype(BF16), preferred_element_type=F32)
        up = jnp.dot(x, wu_ref[...].astype(BF16), preferred_element_type=F32)
        hid = ((a * jax.nn.sigmoid(a)) * up).astype(BF16)
        y_ref[...] = jnp.dot(hid, wd_ref[...].astype(BF16), preferred_element_type=F32).astype(BF16)


def _moe_scatter_kernel(nloc_ref, gtile_ref,
                        col_ref, x1_ref, mod_ref, y_hbm,
                        o_ref,
                        y_ref, sem):
    blk = pl.program_id(0)
    nb = x1_ref.shape[0]
    tile, chunk = MOE_TILE, MOE_CHUNK
    slot = blk & 1

    def copies(b, s, wait):
        _moe_tile_copies(nloc_ref, gtile_ref, b, y_ref.at[s], y_hbm, sem.at[s], False, wait)

    @pl.when(blk == 0)
    def _():
        copies(0, 0, False)

    @pl.when(blk + 1 < pl.num_programs(0))
    def _():
        copies(blk + 1, 1 - slot, False)

    total = nloc_ref[blk]
    n_chunks = (total * tile + (chunk - 1)) // chunk
    max_chunks = y_ref.shape[1] // chunk
    usual = n_chunks <= MOE_USUAL_CHUNKS
    n_static = jnp.where(usual, MOE_USUAL_CHUNKS, max_chunks)

    def clear(t, carry):
        y_ref[slot, pl.ds(pl.multiple_of(t * tile, tile), tile), :] = jnp.zeros((tile, y_ref.shape[2]), BF16)
        return carry

    lax.fori_loop(total, n_static * (chunk // tile), clear, 0)

    p1 = col_ref[:, 0:1].astype(jnp.int32)
    p2 = col_ref[:, 1:2].astype(jnp.int32)
    cw1 = col_ref[:, 2:3]
    cw2 = col_ref[:, 3:4]
    gate = mod_ref[5:6, :]
    lane_c = lax.broadcasted_iota(jnp.int32, (nb, chunk), 1)
    copies(blk, slot, True)

    def scatter(n_unrolled):
        acc = None
        for c in range(n_unrolled):
            p = lane_c + c * chunk
            w = (jnp.where(p == p1, cw1, 0.0) + jnp.where(p == p2, cw2, 0.0)).astype(BF16)
            part = jnp.dot(w, y_ref[slot, c * chunk:(c + 1) * chunk, :], preferred_element_type=F32)
            acc = part if acc is None else acc + part
        o_ref[...] = x1_ref[...] + gate * acc

    lax.cond(usual, lambda: scatter(MOE_USUAL_CHUNKS), lambda: scatter(max_chunks))


def _moe_call(h2, comb, cnt_tiles, x1, mod, w_gate, w_up, w_down):
    bsz, seq, d = x1.shape
    nb, tile, ftm = MOE_TM, MOE_TILE, MOE_FFN_TM
    n_tok = bsz * seq
    n_blk = n_tok // nb
    region = ftm // tile
    rows_local = -(-(2 * nb + N_EXPERTS * tile) // MOE_CHUNK) * MOE_CHUNK
    tiles_global = (2 * n_tok) // tile + n_blk * N_EXPERTS + N_EXPERTS * (region - 1)
    n_ffn_max = -(-tiles_global // region)
    rows_global = n_ffn_max * ftm

    cnt = cnt_tiles[:, :, 0, :N_EXPERTS].reshape(n_blk, nb // POST_TM, N_EXPERTS).sum(axis=1).astype(jnp.int32)
    ntile = (cnt + (tile - 1)) // tile
    lfirst = jnp.cumsum(ntile, axis=1) - ntile
    tot = ntile.sum(axis=0)
    ptot = (tot + (region - 1)) // region * region
    ebase = jnp.cumsum(ptot) - ptot
    gfirst = ebase[None, :] + jnp.cumsum(ntile, axis=0) - ntile
    pad = ptot - tot
    n_ffn = (ptot.sum() // region).reshape(1)
    ends = jnp.cumsum(ptot) // region
    texp = jnp.minimum((jnp.arange(n_ffn_max, dtype=jnp.int32)[:, None] >= ends[None, :]).sum(axis=1),
                       N_EXPERTS - 1).astype(jnp.int32)
    nloc = ntile.sum(axis=1).astype(jnp.int32)
    lt = jnp.arange(rows_local // tile, dtype=jnp.int32)[None, :, None]
    in_seg = (lt >= lfirst[:, None, :]) & (lt < (lfirst + ntile)[:, None, :])
    gtile = (jnp.where(in_seg, (gfirst - lfirst)[:, None, :], 0).sum(axis=2) + lt[:, :, 0]).astype(jnp.int32)
    padstart = (gfirst[-1] + ntile[-1]).astype(jnp.int32)

    h2f = h2.reshape(n_tok, d)
    combf = comb.reshape(n_tok, LANES)
    col, xg = pl.pallas_call(
        _moe_gather_kernel,
        out_shape=(jax.ShapeDtypeStruct((n_tok, LANES), F32),
                   jax.ShapeDtypeStruct((rows_global, d), BF16)),
        grid_spec=pltpu.PrefetchScalarGridSpec(
            num_scalar_prefetch=4,
            grid=(n_blk,),
            in_specs=[pl.BlockSpec((nb, d), lambda j, *_: (j, 0)),
                      pl.BlockSpec((nb, LANES), lambda j, *_: (j, 0))],
            out_specs=(pl.BlockSpec((nb, LANES), lambda j, *_: (j, 0)),
                       pl.BlockSpec(memory_space=pl.ANY)),
            scratch_shapes=[
                pltpu.VMEM((rows_local, d), BF16),
                pltpu.VMEM((LANES, nb), F32),
                pltpu.VMEM((tile, d), BF16),
                pltpu.SemaphoreType.DMA,
            ]),
        compiler_params=pltpu.CompilerParams(dimension_semantics=("arbitrary",),
                                             vmem_limit_bytes=VMEM_LIMIT_BYTES),
        name="moe_gather",
    )(nloc, gtile, padstart, pad, h2f, combf)

    last = lambda t, te, nt: jnp.minimum(t, nt[0] - 1)
    y = pl.pallas_call(
        _moe_ffn_kernel,
        out_shape=jax.ShapeDtypeStruct((rows_global, d), BF16),
        grid_spec=pltpu.PrefetchScalarGridSpec(
            num_scalar_prefetch=2,
            grid=(n_ffn_max,),
            in_specs=[pl.BlockSpec((ftm, d), lambda t, te, nt: (last(t, te, nt), 0)),
                      pl.BlockSpec((None, d, EXPERT_FF), lambda t, te, nt: (te[last(t, te, nt)], 0, 0)),
                      pl.BlockSpec((None, d, EXPERT_FF), lambda t, te, nt: (te[last(t, te, nt)], 0, 0)),
                      pl.BlockSpec((None, EXPERT_FF, d), lambda t, te, nt: (te[last(t, te, nt)], 0, 0))],
            out_specs=pl.BlockSpec((ftm, d), lambda t, te, nt: (last(t, te, nt), 0))),
        compiler_params=pltpu.CompilerParams(dimension_semantics=("arbitrary",),
                                             vmem_limit_bytes=VMEM_LIMIT_BYTES),
        name="moe_ffn",
    )(texp, n_ffn, xg, w_gate, w_up, w_down)

    out = pl.pallas_call(
        _moe_scatter_kernel,
        out_shape=jax.ShapeDtypeStruct((n_tok, d), F32),
        grid_spec=pltpu.PrefetchScalarGridSpec(
            num_scalar_prefetch=2,
            grid=(n_blk,),
            in_specs=[pl.BlockSpec((nb, LANES), lambda j, *_: (j, 0)),
                      pl.BlockSpec((nb, d), lambda j, *_: (j, 0)),
                      pl.BlockSpec((None, 6, d), lambda j, *_: ((j * nb) // seq, 0, 0)),
                      pl.BlockSpec(memory_space=pl.ANY)],
            out_specs=pl.BlockSpec((nb, d), lambda j, *_: (j, 0)),
            scratch_shapes=[pltpu.VMEM((2, rows_local, d), BF16),
                            pltpu.SemaphoreType.DMA((2,))]),
        compiler_params=pltpu.CompilerParams(dimension_semantics=("arbitrary",),
                                             vmem_limit_bytes=VMEM_LIMIT_BYTES),
        name="moe_scatter",
    )(nloc, gtile, col, x1.reshape(n_tok, d), mod, y)
    return out.reshape(bsz, seq, d)


def _layer(x, mod, rel_bias, norm1, w_in, q_norm, k_norm, conv_w, attn_out_norm, conv_out_norm, w_out,
           norm2, w_group_router, b_group_router, w_expert_router, b_expert_router, w_gate, w_up, w_down):
    bsz, seq, d = x.shape
    aw = ATTN_WIDTH
    topk = min(TOPK_MAX, seq // 4)

    offs = np.cumsum([0, aw, aw, aw, IDX_HEADS * IDX_DIM, IDX_DIM, IDX_HEADS, CONV_WIDTH, CONV_WIDTH, CONV_WIDTH])
    col = lambda n: w_in[:, int(offs[n]):int(offs[n + 1])]
    wm = jnp.concatenate([col(0), col(1), col(3), col(6), col(7), col(8)], axis=1).astype(BF16)
    wvt = col(2).T.astype(BF16)
    wki = jnp.concatenate([col(4), col(4)], axis=1).astype(BF16)
    wwit = col(5).T.astype(BF16)
    qg = (jnp.tile(q_norm, ATTN_HEADS) * ((HEAD_DIM ** -0.5) * LOG2E))[None, :]
    kg = jnp.tile(k_norm, ATTN_HEADS)[None, :]
    grp = np.arange(aw) // CONV_GROUP_DIM
    gmat = jnp.asarray((grp[:, None] == grp[None, :]).astype(np.float32) / CONV_GROUP_DIM, dtype=BF16)

    q, k, vt, qi, ki, wit, cn = _pre_call(
        x, mod, norm1[None, :], wm, wvt, wki, wwit, qg, kg, conv_w, conv_out_norm.reshape(1, -1), gmat)

    bounds = jnp.asarray(_bucket_boundaries())
    an = _attn_call(rel_bias, bounds, q, qi, wit, k, ki, vt, attn_out_norm.reshape(1, -1), topk)

    wr = jnp.concatenate([w_expert_router, w_group_router,
                          jnp.zeros((d, LANES - N_EXPERTS - N_GROUPS), F32)], axis=1).astype(BF16)
    br = jnp.concatenate([b_expert_router, b_group_router,
                          jnp.zeros((LANES - N_EXPERTS - N_GROUPS,), F32)])[None, :]
    x1, h2, comb, cnt_tiles = _post_call(an, cn, x, mod, norm2[None, :], w_out[:aw].astype(BF16),
                                         w_out[aw:].astype(BF16), wr, br)

    return _moe_call(h2, comb, cnt_tiles, x1, mod, w_gate, w_up, w_down)


def kernel(x, c, rel_bias, w_ada, b_ada, norm1, w_in, q_norm, k_norm, conv_w, attn_out_norm, conv_out_norm,
           w_out, norm2, w_group_router, b_group_router, w_expert_router, b_expert_router, w_gate, w_up,
           w_down):
    bsz, seq, d = x.shape
    assert d == D_MODEL and seq % max(PRE_TM, POST_TM, MOE_TM) == 0 and ATT_TQ == ATT_TK
    depth = w_ada.shape[0]
    for l in range(depth):
        mod = _mod_call(c, w_ada[l], b_ada[l][None, :]).reshape(bsz, 6, d)
        x = _layer(x, mod, rel_bias, norm1[l], w_in[l], q_norm[l], k_norm[l], conv_w[l], attn_out_norm[l],
                   conv_out_norm[l], w_out[l], norm2[l], w_group_router[l], b_group_router[l],
                   w_expert_router[l], b_expert_router[l], w_gate[l], w_up[l], w_down[l])
    return x
```

```python
import functools
import math

import jax
import jax.numpy as jnp
import numpy as np
from jax import lax
from jax.experimental import pallas as pl
from jax.experimental.pallas import tpu as pltpu

F32 = jnp.float32
BF16 = jnp.bfloat16

D_MODEL = 1024
HEAD_DIM = 64
ATTN_HEADS = 8
ATTN_WIDTH = ATTN_HEADS * HEAD_DIM
CONV_WIDTH = D_MODEL - ATTN_WIDTH
CONV_GROUP_DIM = 64
CONV_K = 3
IDX_HEADS = 8
IDX_DIM = 64
TOPK_MAX = 256
IDX_SCALE = (IDX_DIM ** -0.5) * (IDX_HEADS ** -0.5)
N_BUCKETS = 32
MAX_DISTANCE = 128
N_GROUPS = 4
EXPERTS_PER_GROUP = 8
N_EXPERTS = N_GROUPS * EXPERTS_PER_GROUP
EXPERT_FF = 256
EPS = 1e-6
LOG2E = 1.4426950408889634
NEG_BIG = -1e30
COUNT_ACCS = 4
BISECT_GROUP = 2
BISECT_BF16_STEPS = 10
BISECT_VALUE_STEPS = 4
BISECT_MAX_STEPS = 64

LANES = 128
SUBLANES = 8
BF16_SUBLANES = 16
V_SLAB = HEAD_DIM + BF16_SUBLANES
VMEM_LIMIT_BYTES = 56 * 1024 * 1024

PRE_TM = 512
ATT_TQ = 256
ATT_TK = 256
POST_TM = 512
POST_SLABS = 2
MOE_TM = 512
MOE_TILE = 32
MOE_CHUNK = 512
MOE_USUAL_CHUNKS = 3
MOE_FFN_TM = 1024
MOD_TN = 1536

_NT_DIMS = (((1,), (1,)), ((), ()))


def _tree_sum(parts):
    while len(parts) > 1:
        nxt = [parts[j] + parts[j + 1] for j in range(0, len(parts) - 1, 2)]
        if len(parts) % 2:
            nxt.append(parts[-1])
        parts = nxt
    return parts[0]


def _bucket_boundaries():
    max_exact = N_BUCKETS // 2
    d = np.arange(0, 4 * MAX_DISTANCE, dtype=np.int64)
    nf = np.maximum(d, 1).astype(np.float32)
    large = max_exact + (np.log(nf / np.float32(max_exact)) / np.float32(math.log(MAX_DISTANCE / max_exact))
                         * np.float32(N_BUCKETS - max_exact)).astype(np.int32)
    large = np.minimum(large, N_BUCKETS - 1)
    bucket = np.where(d < max_exact, d, large)
    assert np.all(np.diff(bucket) >= 0) and bucket[-1] == N_BUCKETS - 1
    bounds = [int(np.argmax(bucket >= j)) for j in range(1, N_BUCKETS)]
    return np.asarray([0] + bounds, dtype=np.int32)


def _mod_kernel(c_ref, w_ref, b_ref, o_ref):
    c = c_ref[...]
    act = c * jax.nn.sigmoid(c)
    o_ref[...] = jnp.dot(act, w_ref[...], preferred_element_type=F32,
                         precision=lax.Precision.HIGHEST) + b_ref[...]


def _mod_call(c, w_ada, b_ada):
    bsz, d = c.shape
    n = w_ada.shape[1]
    return pl.pallas_call(
        _mod_kernel,
        out_shape=jax.ShapeDtypeStruct((bsz, n), F32),
        grid=(n // MOD_TN,),
        in_specs=[pl.BlockSpec((bsz, d), lambda j: (0, 0)),
                  pl.BlockSpec((d, MOD_TN), lambda j: (0, j)),
                  pl.BlockSpec((1, MOD_TN), lambda j: (0, j))],
        out_specs=pl.BlockSpec((bsz, MOD_TN), lambda j: (0, j)),
        compiler_params=pltpu.CompilerParams(dimension_semantics=("arbitrary",),
                                             vmem_limit_bytes=VMEM_LIMIT_BYTES),
        name="adaln_mod",
    )(c, w_ada, b_ada)


def _group_rms(y, g_ref):
    ms = jnp.dot((y * y).astype(BF16), g_ref[...], preferred_element_type=F32)
    return y * lax.rsqrt(ms + EPS)


def _pre_kernel(x_ref, mod_ref, n1_ref, wm_ref, wvt_ref, wki_ref, wwit_ref, qg_ref, kg_ref,
                cw_ref, cg_ref, g_ref,
                q_ref, k_ref, vt_ref, qi_ref, ki_ref, wit_ref, cn_ref, carry_ref):
    j = pl.program_id(1)
    tm = x_ref.shape[0]
    aw = ATTN_WIDTH

    x = x_ref[...]
    ms = jnp.mean(x * x, axis=-1, keepdims=True)
    y = x * lax.rsqrt(ms + EPS) * n1_ref[...]
    h = y * (1.0 + mod_ref[1:2, :]) + mod_ref[0:1, :]
    hb = h.astype(BF16)

    def proj(lo):
        return jnp.dot(hb, wm_ref[:, lo:lo + aw], preferred_element_type=F32)

    q = _group_rms(proj(0), g_ref) * qg_ref[...]
    q_ref[...] = q.astype(BF16)
    k = _group_rms(proj(aw), g_ref) * kg_ref[...]
    k_ref[...] = k.astype(BF16)

    vt = lax.dot_general(wvt_ref[...], hb, _NT_DIMS, preferred_element_type=F32).astype(BF16)
    ones = jnp.ones((BF16_SUBLANES, ATT_TK), BF16)
    for cc in range(tm // ATT_TK):
        for hh in range(ATTN_HEADS):
            vt_ref[cc, hh * V_SLAB:hh * V_SLAB + HEAD_DIM, :] = (
                vt[hh * HEAD_DIM:(hh + 1) * HEAD_DIM, cc * ATT_TK:(cc + 1) * ATT_TK])
            vt_ref[cc, hh * V_SLAB + HEAD_DIM:(hh + 1) * V_SLAB, :] = ones

    qi_ref[...] = proj(2 * aw).astype(BF16)
    ki_ref[...] = jnp.dot(hb, wki_ref[...], preferred_element_type=F32).astype(BF16)
    wit_ref[...] = lax.dot_general(wwit_ref[...], hb, _NT_DIMS, preferred_element_type=F32) * IDX_SCALE

    assert CONV_K == 3 and cw_ref.shape[0] == CONV_K
    gate_b = proj(3 * aw)
    z = proj(4 * aw) * proj(5 * aw)

    @pl.when(j == 0)
    def _():
        carry_ref[...] = jnp.zeros_like(carry_ref)

    prev = carry_ref[...]
    row = lax.broadcasted_iota(jnp.int32, z.shape, 0)
    z1 = jnp.where(row == 0, prev[SUBLANES - 1:SUBLANES, :], pltpu.roll(z, 1, 0))
    z2 = pltpu.roll(z, 2, 0)
    z2 = jnp.where(row == 0, prev[SUBLANES - 2:SUBLANES - 1, :], z2)
    z2 = jnp.where(row == 1, prev[SUBLANES - 1:SUBLANES, :], z2)
    carry_ref[...] = z[tm - SUBLANES:, :]
    conv = cw_ref[2:3, :] * z + cw_ref[1:2, :] * z1 + cw_ref[0:1, :] * z2
    yc = gate_b * conv
    cn_ref[...] = (_group_rms(yc, g_ref) * cg_ref[...]).astype(BF16)


def _pre_call(x, mod, n1, wm, wvt, wki, wwit, qg, kg, cw, cg, gmat):
    bsz, seq, d = x.shape
    tm = PRE_TM
    nck = tm // ATT_TK
    aw = ATTN_WIDTH
    const = lambda b, j: (0, 0)
    tok = lambda b, j: (b, j, 0)
    out_shape = (
        jax.ShapeDtypeStruct((bsz, seq, aw), BF16),
        jax.ShapeDtypeStruct((bsz, seq, aw), BF16),
        jax.ShapeDtypeStruct((bsz, seq // ATT_TK, ATTN_HEADS * V_SLAB, ATT_TK), BF16),
        jax.ShapeDtypeStruct((bsz, seq, aw), BF16),
        jax.ShapeDtypeStruct((bsz, seq, LANES), BF16),
        jax.ShapeDtypeStruct((bsz, IDX_HEADS, seq), F32),
        jax.ShapeDtypeStruct((bsz, seq, CONV_WIDTH), BF16),
    )
    out_specs = (
        pl.BlockSpec((None, tm, aw), tok),
        pl.BlockSpec((None, tm, aw), tok),
        pl.BlockSpec((None, nck, ATTN_HEADS * V_SLAB, ATT_TK), lambda b, j: (b, j, 0, 0)),
        pl.BlockSpec((None, tm, aw), tok),
        pl.BlockSpec((None, tm, LANES), tok),
        pl.BlockSpec((None, IDX_HEADS, tm), lambda b, j: (b, 0, j)),
        pl.BlockSpec((None, tm, CONV_WIDTH), tok),
    )
    in_specs = [
        pl.BlockSpec((None, tm, d), tok),
        pl.BlockSpec((None, 6, d), lambda b, j: (b, 0, 0)),
        pl.BlockSpec(n1.shape, const),
        pl.BlockSpec(wm.shape, const),
        pl.BlockSpec(wvt.shape, const),
        pl.BlockSpec(wki.shape, const),
        pl.BlockSpec(wwit.shape, const),
        pl.BlockSpec(qg.shape, const),
        pl.BlockSpec(kg.shape, const),
        pl.BlockSpec(cw.shape, const),
        pl.BlockSpec(cg.shape, const),
        pl.BlockSpec(gmat.shape, const),
    ]
    return pl.pallas_call(
        _pre_kernel,
        out_shape=out_shape,
        grid=(bsz, seq // tm),
        in_specs=in_specs,
        out_specs=out_specs,
        scratch_shapes=[pltpu.VMEM((SUBLANES, CONV_WIDTH), F32)],
        compiler_params=pltpu.CompilerParams(dimension_semantics=("arbitrary", "arbitrary"),
                                             vmem_limit_bytes=VMEM_LIMIT_BYTES),
        name="pre_proj",
    )(x, mod, n1, wm, wvt, wki, wwit, qg, kg, cw, cg, gmat)


def _attn_kernel(rb_ref, bnd_ref, q_ref, qi_ref, wit_ref, k_ref, ki_ref, vt_ref, og_ref,
                 o_ref,
                 s_ref, s16_ref, bias_ref, qpad_ref, qipad_ref, lg_ref, acc_ref, out_ref, *, topk):
    b = pl.program_id(0)
    i = pl.program_id(1)
    tq, tk = ATT_TQ, ATT_TK
    nh, hd = ATTN_HEADS, HEAD_DIM

    t_loc = lax.broadcasted_iota(jnp.int32, (tk, tq), 1)
    s_loc = lax.broadcasted_iota(jnp.int32, (tk, tq), 0)

    @pl.when((b == 0) & (i == 0))
    def _():
        for idx in range(2):
            dist = t_loc - s_loc + idx * tq
            for h in range(nh):
                bias_ref[idx, h] = jnp.full((tk, tq), (rb_ref[0, h] - rb_ref[N_BUCKETS - 1, h]) * LOG2E, F32)

            def fill(jb, carry):
                reached = dist >= bnd_ref[jb]
                for h in range(nh):
                    val = (rb_ref[jb, h] - rb_ref[N_BUCKETS - 1, h]) * LOG2E
                    bias_ref[idx, h] = jnp.where(reached, val, bias_ref[idx, h])
                return carry

            lax.fori_loop(1, N_BUCKETS, fill, 0)

    lane = lax.broadcasted_iota(jnp.int32, (tq, LANES), 1)
    for h in range(nh):
        pair = slice((h // 2) * LANES, (h // 2 + 1) * LANES)
        keep = (lane // hd) == (h % 2)
        qpad_ref[h] = jnp.where(keep, q_ref[:, pair], jnp.zeros((), BF16))
        qipad_ref[h] = jnp.where(keep, qi_ref[:, pair], jnp.zeros((), BF16))

    def idx_dots(c, slot):
        kic = ki_ref[pl.ds(pl.multiple_of(c * tk, tk), tk), :]
        for h in range(nh):
            lg_ref[slot, h] = lax.dot_general(kic, qipad_ref[h], _NT_DIMS, preferred_element_type=F32)

    def idx_reduce(c, slot, carry, diagonal):
        rmin, rmax = carry
        sc = _tree_sum([wit_ref[h:h + 1, :] * jnp.maximum(lg_ref[slot, h], 0.0) for h in range(nh)])
        if diagonal:
            causal = s_loc <= t_loc
            lo_c, hi_c = jnp.where(causal, sc, jnp.inf), jnp.where(causal, sc, -jnp.inf)
            sc = hi_c
        else:
            lo_c, hi_c = sc, sc
        s_ref[c] = sc
        s16_ref[c] = sc.astype(BF16)
        return (jnp.minimum(rmin, jnp.min(lo_c, axis=0, keepdims=True)),
                jnp.maximum(rmax, jnp.max(hi_c, axis=0, keepdims=True)))

    def idx_pair(jj, carry):
        idx_dots(2 * jj + 1, 1)
        carry = idx_reduce(2 * jj, 0, carry, False)
        idx_dots(2 * jj + 2, 0)
        return idx_reduce(2 * jj + 1, 1, carry, False)

    def idx_tail_odd(carry):
        idx_dots(i, 1)
        return idx_reduce(i, 1, idx_reduce(i - 1, 0, carry, False), True)

    idx_dots(0, 0)
    carry = (jnp.full((1, tq), jnp.inf, F32), jnp.full((1, tq), -jnp.inf, F32))
    carry = lax.fori_loop(0, i // 2, idx_pair, carry)
    rmin, rmax = lax.cond((i & 1) == 1, idx_tail_odd, lambda cr: idx_reduce(i, 0, cr, True), carry)

    def count_ge(thr):
        def body(c, accs):
            hit = s_ref[c] >= thr
            accs = list(accs)
            for r in range(tk // SUBLANES):
                a = accs[r % COUNT_ACCS]
                accs[r % COUNT_ACCS] = jnp.where(hit[r * SUBLANES:(r + 1) * SUBLANES], a + 1.0, a)
            return tuple(accs)
        accs = lax.fori_loop(0, i + 1, body,
                             tuple(jnp.zeros((SUBLANES, tq), F32) for _ in range(COUNT_ACCS)))
        return jnp.sum(_tree_sum(list(accs)), axis=0, keepdims=True)

    def count16_ge(thr16):
        def body(c, accs):
            hit = s16_ref[c] >= thr16
            accs = list(accs)
            for r in range(tk // BF16_SUBLANES):
                a = accs[r % COUNT_ACCS]
                accs[r % COUNT_ACCS] = jnp.where(hit[r * BF16_SUBLANES:(r + 1) * BF16_SUBLANES], a + 1, a)
            return tuple(accs)
        accs = lax.fori_loop(0, i + 1, body,
                             tuple(jnp.zeros((BF16_SUBLANES, tq), BF16) for _ in range(COUNT_ACCS)))
        return jnp.sum(_tree_sum(list(accs)).astype(F32), axis=0, keepdims=True)

    int_min = jnp.int32(-2 ** 31)

    def order_key(v):
        bits = pltpu.bitcast(v, jnp.int32)
        return jnp.where(bits < 0, -(bits & jnp.int32(0x7FFFFFFF)), bits)

    def from_order_key(key):
        return pltpu.bitcast(jnp.where(key < 0, (-key) | int_min, key), F32)

    t_glob = (i * tq + lax.broadcasted_iota(jnp.int32, (1, tq), 1)).astype(F32)
    n_causal = t_glob + 1.0
    kf = jnp.minimum(float(topk), n_causal)
    all_sel = n_causal <= kf

    lo16 = order_key(rmin) >> 16
    hi16 = (order_key(rmax.astype(BF16).astype(F32) + 0.0) >> 16) + 1
    for _ in range(BISECT_BF16_STEPS):
        lo_v = from_order_key(lo16 << 16)
        hi_v = from_order_key(hi16 << 16)
        mid_val16 = order_key(lo_v + (hi_v - lo_v) * 0.5) >> 16
        mid16 = jnp.where((mid_val16 > lo16) & (mid_val16 < hi16), mid_val16, (lo16 + hi16) >> 1)
        open_ = (hi16 - lo16) > 1
        cm = count16_ge(from_order_key(mid16 << 16).astype(BF16))
        lo16 = jnp.where(open_ & (cm >= kf), mid16, lo16)
        hi16 = jnp.where(open_ & (cm < kf), mid16, hi16)

    min_normal_key = jnp.int32(0x00800000)

    def snap(key, direction):
        sub = (key > -min_normal_key) & (key < min_normal_key) & (key != 0)
        below = jnp.where(key > 0, 0, -min_normal_key)
        above = jnp.where(key > 0, min_normal_key, 0)
        return jnp.where(sub, {"down": below, "up": above, "zero": jnp.zeros_like(key)}[direction], key)

    lo0 = snap(jnp.maximum((lo16 - 1) << 16, order_key(rmin)), "down")
    hi0 = snap(hi16 << 16, "up")
    state0 = (jnp.where(all_sel, 0.0, 1.0), lo0, hi0, count_ge(from_order_key(hi0)),
              order_key(rmin), jnp.zeros((1, tq), F32), jnp.full((1, tq), 0x7F800000, jnp.int32), kf)

    def bisect_step(st, value_mid, force_end):
        active, lo_key, hi_key, fhi, thr_key, tie, hif_key, need = st
        mid_key = snap((lo_key >> 1) + (hi_key >> 1) + (lo_key & hi_key & 1), "zero")
        if value_mid:
            lo = from_order_key(lo_key)
            hi = from_order_key(hi_key)
            val_key = snap(order_key(lo + (hi - lo) * 0.5), "zero")
            mid_key = jnp.where((val_key > lo_key) & (val_key < hi_key), val_key, mid_key)
        collapsed = (mid_key <= lo_key) | (mid_key >= hi_key) | force_end
        cm = count_ge(from_order_key(mid_key))
        act = active > 0.0
        live = act & jnp.logical_not(collapsed)
        found = live & (cm == kf)
        go_up = live & (cm > kf)
        go_dn = live & (cm < kf)
        ends_tie = act & collapsed
        thr_key = jnp.where(found, mid_key, jnp.where(ends_tie, lo_key, thr_key))
        tie = jnp.where(ends_tie, 1.0, tie)
        hif_key = jnp.where(ends_tie, hi_key, hif_key)
        need = jnp.where(ends_tie, kf - fhi, need)
        lo_key = jnp.where(go_up, mid_key, lo_key)
        fhi = jnp.where(go_dn, cm, fhi)
        hi_key = jnp.where(go_dn, mid_key, hi_key)
        active = jnp.where(found | ends_tie, 0.0, active)
        return active, lo_key, hi_key, fhi, thr_key, tie, hif_key, need

    state = state0
    for _ in range(BISECT_VALUE_STEPS):
        state = bisect_step(state, True, False)

    def b_cond(carry):
        st, step = carry
        return (jnp.max(st[0]) > 0.0) & (step <= BISECT_MAX_STEPS)

    def b_group(carry):
        st, step = carry
        for _ in range(BISECT_GROUP):
            st = bisect_step(st, False, step >= BISECT_MAX_STEPS)
        return st, step + BISECT_GROUP

    (_, _, _, _, thr_key, tie, hif_key, need), _ = lax.while_loop(b_cond, b_group, (state, jnp.int32(0)))
    thr = from_order_key(thr_key)
    hif = from_order_key(hif_key)

    @pl.when(jnp.max(tie) > 0.0)
    def _():
        tri = jnp.where(lax.broadcasted_iota(jnp.int32, (tk, tk), 1)
                        <= lax.broadcasted_iota(jnp.int32, (tk, tk), 0), 1.0, 0.0).astype(BF16)

        def body(c, seen):
            sc_c = s_ref[c]
            tied = (sc_c >= thr) & (sc_c < hif) & (tie > 0.0)
            rank = jnp.dot(tri, jnp.where(tied, 1.0, 0.0).astype(BF16), preferred_element_type=F32) + seen
            s_ref[c] = jnp.where(tied & (rank > need), -jnp.inf, sc_c)
            return rank[tk - 1:tk, :]

        lax.fori_loop(0, i + 1, body, jnp.zeros((1, tq), F32))

    acc_ref[...] = jnp.zeros(acc_ref.shape, F32)

    def store_logits(c, slot, bias_idx):
        masked = jnp.where(s_ref[c] >= thr, 0.0, NEG_BIG)
        row0 = pl.multiple_of(c * tk, tk)
        for h in range(nh):
            kc = k_ref[pl.ds(row0, tk), (h // 2) * LANES:(h // 2 + 1) * LANES]
            lt = lax.dot_general(kc, qpad_ref[h], _NT_DIMS, preferred_element_type=F32) + masked
            if bias_idx is not None:
                lt = lt + bias_ref[bias_idx, h]
            lg_ref[slot, h] = lt

    def softmax_pv(c, slot, m_all):
        m_out = []
        for h in range(nh):
            m_old = m_all[h]
            m_new = jnp.maximum(m_old, jnp.max(lg_ref[slot, h], axis=0, keepdims=True))
            p = jnp.exp2(lg_ref[slot, h] - m_new).astype(BF16)
            alpha = jnp.exp2(m_old - m_new)
            pv = jnp.dot(vt_ref[c, h * V_SLAB:(h + 1) * V_SLAB, :], p, preferred_element_type=F32)
            acc_ref[h] = alpha * acc_ref[h] + pv
            m_out.append(m_new)
        return tuple(m_out)

    def near_step(m_all):
        store_logits(i - 1, 1, 1)
        return softmax_pv(i, 0, m_all)

    def far_step(j, parity, m_all):
        c = i - 2 - j
        store_logits(c, parity, None)
        return softmax_pv(c + 1, 1 - parity, m_all)

    def far_pair(jj, m_all):
        return far_step(2 * jj + 1, 1, far_step(2 * jj, 0, m_all))

    n_far = jnp.maximum(i - 1, 0)
    m_all = tuple(jnp.full((1, tq), NEG_BIG, F32) for _ in range(nh))
    store_logits(i, 0, 0)
    m_all = lax.cond(i >= 1, near_step, lambda m: m, m_all)
    m_all = lax.fori_loop(0, n_far // 2, far_pair, m_all)
    m_all = lax.cond((n_far & 1) == 1, lambda m: far_step(n_far - 1, 0, m), lambda m: m, m_all)
    lax.cond((i & 1) == 0, lambda m: softmax_pv(0, 0, m), lambda m: softmax_pv(0, 1, m), m_all)

    for h in range(nh):
        o = acc_ref[h, :hd, :] / acc_ref[h, hd:hd + 1, :]
        ms = jnp.mean(o * o, axis=0, keepdims=True)
        out_ref[h * hd:(h + 1) * hd, :] = o * lax.rsqrt(ms + EPS)
    o_ref[...] = (out_ref[...].T * og_ref[...]).astype(BF16)


def _attn_call(rel_bias, bounds, q, qi, wit, k, ki, vt, og, topk):
    bsz, seq, aw = q.shape
    tq, tk = ATT_TQ, ATT_TK
    nck = seq // tk
    blk_q = lambda b, i: (b, i, 0)
    whole = lambda b, i: (b, 0, 0)
    smem = pl.BlockSpec(memory_space=pltpu.SMEM)
    return pl.pallas_call(
        functools.partial(_attn_kernel, topk=topk),
        out_shape=jax.ShapeDtypeStruct((bsz, seq, aw), BF16),
        grid=(bsz, seq // tq),
        in_specs=[
            smem, smem,
            pl.BlockSpec((None, tq, aw), blk_q),
            pl.BlockSpec((None, tq, aw), blk_q),
            pl.BlockSpec((None, IDX_HEADS, tq), lambda b, i: (b, 0, i)),
            pl.BlockSpec((None, seq, aw), whole),
            pl.BlockSpec((None, seq, LANES), whole),
            pl.BlockSpec((None, nck, ATTN_HEADS * V_SLAB, tk), lambda b, i: (b, 0, 0, 0)),
            pl.BlockSpec(og.shape, lambda b, i: (0, 0)),
        ],
        out_specs=pl.BlockSpec((None, tq, aw), blk_q),
        scratch_shapes=[
            pltpu.VMEM((nck, tk, tq), F32),
            pltpu.VMEM((nck, tk, tq), BF16),
            pltpu.VMEM((2, ATTN_HEADS, tk, tq), F32),
            pltpu.VMEM((ATTN_HEADS, tq, LANES), BF16),
            pltpu.VMEM((IDX_HEADS, tq, LANES), BF16),
            pltpu.VMEM((2, ATTN_HEADS, tk, tq), F32),
            pltpu.VMEM((ATTN_HEADS, V_SLAB, tq), F32),
            pltpu.VMEM((aw, tq), F32),
        ],
        compiler_params=pltpu.CompilerParams(dimension_semantics=("arbitrary", "arbitrary"),
                                             vmem_limit_bytes=VMEM_LIMIT_BYTES),
        name="dsa_attention",
    )(rel_bias, bounds, q, qi, wit, k, ki, vt, og)


def _post_kernel(an_ref, cn_ref, x_ref, mod_ref, n2_ref, woa_ref, woc_ref, wr_ref, br_ref,
                 x1_ref, h2_ref, comb_ref, cnt_ref):
    slab = x_ref.shape[0] // POST_SLABS
    cnt = jnp.zeros((1, LANES), F32)
    for s in range(POST_SLABS):
        cnt = cnt + _post_slab(slice(s * slab, (s + 1) * slab), an_ref, cn_ref, x_ref, mod_ref, n2_ref,
                               woa_ref, woc_ref, wr_ref, br_ref, x1_ref, h2_ref, comb_ref)
    cnt_ref[...] = jnp.broadcast_to(cnt, cnt_ref.shape)


def _post_slab(rows, an_ref, cn_ref, x_ref, mod_ref, n2_ref, woa_ref, woc_ref, wr_ref, br_ref,
               x1_ref, h2_ref, comb_ref):
    mix = (jnp.dot(an_ref[rows, :], woa_ref[...], preferred_element_type=F32)
           + jnp.dot(cn_ref[rows, :], woc_ref[...], preferred_element_type=F32))
    x1 = x_ref[rows, :] + mod_ref[2:3, :] * mix
    x1_ref[rows, :] = x1
    ms = jnp.mean(x1 * x1, axis=-1, keepdims=True)
    h2 = x1 * lax.rsqrt(ms + EPS) * n2_ref[...] * (1.0 + mod_ref[4:5, :]) + mod_ref[3:4, :]
    h2b = h2.astype(BF16)
    h2_ref[rows, :] = h2b

    logits = jnp.dot(h2b, wr_ref[...], preferred_element_type=F32) + br_ref[...]
    lane = lax.broadcasted_iota(jnp.int32, logits.shape, 1)
    lane_f = lane.astype(F32)
    far = float(LANES)
    is_g = (lane >= N_EXPERTS) & (lane < N_EXPERTS + N_GROUPS)
    gl = jnp.where(is_g, logits, -jnp.inf)
    gmax = jnp.max(gl, axis=-1, keepdims=True)
    g_sel = jnp.min(jnp.where(is_g & (gl == gmax), lane_f, far), axis=-1, keepdims=True) - float(N_EXPERTS)
    p_g = 1.0 / jnp.sum(jnp.exp(gl - gmax), axis=-1, keepdims=True)

    in_grp = (lane < N_EXPERTS) & ((lane // EXPERTS_PER_GROUP).astype(F32) == g_sel)
    e1 = jnp.where(in_grp, logits, -jnp.inf)
    l1 = jnp.max(e1, axis=-1, keepdims=True)
    i1 = jnp.min(jnp.where(in_grp & (e1 == l1), lane_f, far), axis=-1, keepdims=True)
    rest = in_grp & (lane_f != i1)
    e2 = jnp.where(rest, logits, -jnp.inf)
    l2 = jnp.max(e2, axis=-1, keepdims=True)
    i2 = jnp.min(jnp.where(rest & (e2 == l2), lane_f, far), axis=-1, keepdims=True)
    r = jnp.exp(l2 - l1)
    w1 = 1.0 / (1.0 + r)
    w2 = r / (1.0 + r)
    comb = jnp.where(lane_f == i1, p_g * w1, 0.0) + jnp.where(lane_f == i2, p_g * w2, 0.0)
    comb_ref[rows, :] = comb
    return jnp.sum(jnp.where(comb != 0.0, 1.0, 0.0), axis=0, keepdims=True)


def _post_call(an, cn, x, mod, n2, woa, woc, wr, br):
    bsz, seq, d = x.shape
    tm = POST_TM
    tok = lambda b, j: (b, j, 0)
    const = lambda b, j: (0, 0)
    return pl.pallas_call(
        _post_kernel,
        out_shape=(jax.ShapeDtypeStruct((bsz, seq, d), F32),
                   jax.ShapeDtypeStruct((bsz, seq, d), BF16),
                   jax.ShapeDtypeStruct((bsz, seq, LANES), F32),
                   jax.ShapeDtypeStruct((bsz, seq // tm, SUBLANES, LANES), F32)),
        grid=(bsz, seq // tm),
        in_specs=[
            pl.BlockSpec((None, tm, ATTN_WIDTH), tok),
            pl.BlockSpec((None, tm, CONV_WIDTH), tok),
            pl.BlockSpec((None, tm, d), tok),
            pl.BlockSpec((None, 6, d), lambda b, j: (b, 0, 0)),
            pl.BlockSpec(n2.shape, const),
            pl.BlockSpec(woa.shape, const),
            pl.BlockSpec(woc.shape, const),
            pl.BlockSpec(wr.shape, const),
            pl.BlockSpec(br.shape, const),
        ],
        out_specs=(pl.BlockSpec((None, tm, d), tok),
                   pl.BlockSpec((None, tm, d), tok),
                   pl.BlockSpec((None, tm, LANES), tok),
                   pl.BlockSpec((None, None, SUBLANES, LANES), lambda b, j: (b, j, 0, 0))),
        compiler_params=pltpu.CompilerParams(dimension_semantics=("arbitrary", "arbitrary"),
                                             vmem_limit_bytes=VMEM_LIMIT_BYTES),
        name="post_router",
    )(an, cn, x, mod, n2, woa, woc, wr, br)


def _strict_tri(n, lower):
    r = lax.broadcasted_iota(jnp.int32, (n, n), 0)
    c = lax.broadcasted_iota(jnp.int32, (n, n), 1)
    return jnp.where((c < r) if lower else (r < c), 1.0, 0.0).astype(BF16)


def _moe_tile_copies(nloc_ref, gtile_ref, blk, local_ref, global_ref, sem, to_global, wait):
    tile = MOE_TILE

    def per_tile(lt, c):
        loc = local_ref.at[pl.ds(pl.multiple_of(lt * tile, tile), tile), :]
        glo = global_ref.at[pl.ds(pl.multiple_of(gtile_ref[blk, lt] * tile, tile), tile), :]
        cp = pltpu.make_async_copy(loc, glo, sem) if to_global else pltpu.make_async_copy(glo, loc, sem)
        if wait:
            cp.wait()
        else:
            cp.start()
        return c

    lax.fori_loop(0, nloc_ref[blk], per_tile, 0)


def _moe_gather_kernel(nloc_ref, gtile_ref, padstart_ref, pad_ref,
                       h2_ref, comb_ref,
                       col_ref, xg_hbm,
                       xg_ref, row_ref, zero_ref, sem):
    blk = pl.program_id(0)
    nb = h2_ref.shape[0]
    tile, chunk = MOE_TILE, MOE_CHUNK
    lane = lax.broadcasted_iota(jnp.int32, (nb, LANES), 1)

    comb = comb_ref[...]
    assigned = comb != 0.0
    a_f = jnp.where(assigned, 1.0, 0.0)
    rank = jnp.dot(_strict_tri(nb, True), a_f.astype(BF16), preferred_element_type=F32)
    cnt = rank[nb - 1:nb, :] + a_f[nb - 1:nb, :]
    ntile = jnp.floor((cnt + float(tile - 1)) * (1.0 / tile))
    first = jnp.dot(jnp.broadcast_to(ntile, (SUBLANES, LANES)).astype(BF16), _strict_tri(LANES, False),
                    preferred_element_type=F32)[0:1, :]
    pos = first * float(tile) + rank
    pos1 = jnp.min(jnp.where(assigned, pos, 1e9), axis=1, keepdims=True)
    pos2 = jnp.max(jnp.where(assigned, pos, -1.0), axis=1, keepdims=True)
    pos2 = jnp.where(pos2 == pos1, -1.0, pos2)
    cw1 = jnp.sum(jnp.where(assigned & (pos == pos1), comb, 0.0), axis=1, keepdims=True)
    cw2 = jnp.sum(jnp.where(assigned & (pos == pos2), comb, 0.0), axis=1, keepdims=True)
    info = jnp.where(lane == 0, pos1, jnp.where(lane == 1, pos2, jnp.where(lane == 2, cw1,
                     jnp.where(lane == 3, cw2, 0.0))))
    col_ref[...] = info
    row_ref[...] = info.T

    n_chunks = (nloc_ref[blk] * tile + (chunk - 1)) // chunk
    p1 = row_ref[0:1, :].astype(jnp.int32)
    p2 = row_ref[1:2, :].astype(jnp.int32)
    sub = lax.broadcasted_iota(jnp.int32, (chunk, nb), 0)

    def gather(c, carry):
        p = sub + c * chunk
        sel = jnp.where((p == p1) | (p == p2), 1.0, 0.0).astype(BF16)
        r0 = pl.multiple_of(c * chunk, chunk)
        xg_ref[pl.ds(r0, chunk), :] = jnp.dot(sel, h2_ref[...], preferred_element_type=F32).astype(BF16)
        return carry

    @pl.when(blk > 0)
    def _():
        _moe_tile_copies(nloc_ref, gtile_ref, blk - 1, xg_ref, xg_hbm, sem, True, True)

    lax.fori_loop(0, n_chunks, gather, 0)

    _moe_tile_copies(nloc_ref, gtile_ref, blk, xg_ref, xg_hbm, sem, True, False)

    is_last = blk == pl.num_programs(0) - 1

    def pad_copies(wait):
        def per_expert(x, carry):
            g0 = padstart_ref[x]

            def per_tile(j, c):
                dst = xg_hbm.at[pl.ds(pl.multiple_of((g0 + j) * tile, tile), tile), :]
                cp = pltpu.make_async_copy(zero_ref, dst, sem)
                if wait:
                    cp.wait()
                else:
                    cp.start()
                return c

            lax.fori_loop(0, pad_ref[x], per_tile, 0)
            return carry

        lax.fori_loop(0, N_EXPERTS, per_expert, 0)

    @pl.when(is_last)
    def _():
        zero_ref[...] = jnp.zeros(zero_ref.shape, BF16)
        pad_copies(False)
        _moe_tile_copies(nloc_ref, gtile_ref, blk, xg_ref, xg_hbm, sem, True, True)
        pad_copies(True)


def _moe_ffn_kernel(texp_ref, nt_ref, x_ref, wg_ref, wu_ref, wd_ref, y_ref):
    @pl.when(pl.program_id(0) < nt_ref[0])
    def _():
        x = x_ref[...]
        a = jnp.dot(x, wg_ref[...].astype(BF16), preferred_element_type=F32)
        up = jnp.dot(x, wu_ref[...].astype(BF16), preferred_element_type=F32)
        hid = ((a * jax.nn.sigmoid(a)) * up).astype(BF16)
        y_ref[...] = jnp.dot(hid, wd_ref[...].astype(BF16), preferred_element_type=F32).astype(BF16)


def _moe_scatter_kernel(nloc_ref, gtile_ref,
                        col_ref, x1_ref, mod_ref, y_hbm,
                        o_ref,
                        y_ref, sem):
    blk = pl.program_id(0)
    nb = x1_ref.shape[0]
    tile, chunk = MOE_TILE, MOE_CHUNK
    slot = blk & 1

    def copies(b, s, wait):
        _moe_tile_copies(nloc_ref, gtile_ref, b, y_ref.at[s], y_hbm, sem.at[s], False, wait)

    @pl.when(blk == 0)
    def _():
        copies(0, 0, False)

    @pl.when(blk + 1 < pl.num_programs(0))
    def _():
        copies(blk + 1, 1 - slot, False)

    total = nloc_ref[blk]
    n_chunks = (total * tile + (chunk - 1)) // chunk
    max_chunks = y_ref.shape[1] // chunk
    usual = n_chunks <= MOE_USUAL_CHUNKS
    n_static = jnp.where(usual, MOE_USUAL_CHUNKS, max_chunks)

    def clear(t, carry):
        y_ref[slot, pl.ds(pl.multiple_of(t * tile, tile), tile), :] = jnp.zeros((tile, y_ref.shape[2]), BF16)
        return carry

    lax.fori_loop(total, n_static * (chunk // tile), clear, 0)

    p1 = col_ref[:, 0:1].astype(jnp.int32)
    p2 = col_ref[:, 1:2].astype(jnp.int32)
    cw1 = col_ref[:, 2:3]
    cw2 = col_ref[:, 3:4]
    gate = mod_ref[5:6, :]
    lane_c = lax.broadcasted_iota(jnp.int32, (nb, chunk), 1)
    copies(blk, slot, True)

    def scatter(n_unrolled):
        acc = None
        for c in range(n_unrolled):
            p = lane_c + c * chunk
            w = (jnp.where(p == p1, cw1, 0.0) + jnp.where(p == p2, cw2, 0.0)).astype(BF16)
            part = jnp.dot(w, y_ref[slot, c * chunk:(c + 1) * chunk, :], preferred_element_type=F32)
            acc = part if acc is None else acc + part
        o_ref[...] = x1_ref[...] + gate * acc

    lax.cond(usual, lambda: scatter(MOE_USUAL_CHUNKS), lambda: scatter(max_chunks))


def _moe_call(h2, comb, cnt_tiles, x1, mod, w_gate, w_up, w_down):
    bsz, seq, d = x1.shape
    nb, tile, ftm = MOE_TM, MOE_TILE, MOE_FFN_TM
    n_tok = bsz * seq
    n_blk = n_tok // nb
    region = ftm // tile
    rows_local = -(-(2 * nb + N_EXPERTS * tile) // MOE_CHUNK) * MOE_CHUNK
    tiles_global = (2 * n_tok) // tile + n_blk * N_EXPERTS + N_EXPERTS * (region - 1)
    n_ffn_max = -(-tiles_global // region)
    rows_global = n_ffn_max * ftm

    cnt = cnt_tiles[:, :, 0, :N_EXPERTS].reshape(n_blk, nb // POST_TM, N_EXPERTS).sum(axis=1).astype(jnp.int32)
    ntile = (cnt + (tile - 1)) // tile
    lfirst = jnp.cumsum(ntile, axis=1) - ntile
    tot = ntile.sum(axis=0)
    ptot = (tot + (region - 1)) // region * region
    ebase = jnp.cumsum(ptot) - ptot
    gfirst = ebase[None, :] + jnp.cumsum(ntile, axis=0) - ntile
    pad = ptot - tot
    n_ffn = (ptot.sum() // region).reshape(1)
    ends = jnp.cumsum(ptot) // region
    texp = jnp.minimum((jnp.arange(n_ffn_max, dtype=jnp.int32)[:, None] >= ends[None, :]).sum(axis=1),
                       N_EXPERTS - 1).astype(jnp.int32)
    nloc = ntile.sum(axis=1).astype(jnp.int32)
    lt = jnp.arange(rows_local // tile, dtype=jnp.int32)[None, :, None]
    in_seg = (lt >= lfirst[:, None, :]) & (lt < (lfirst + ntile)[:, None, :])
    gtile = (jnp.where(in_seg, (gfirst - lfirst)[:, None, :], 0).sum(axis=2) + lt[:, :, 0]).astype(jnp.int32)
    padstart = (gfirst[-1] + ntile[-1]).astype(jnp.int32)

    h2f = h2.reshape(n_tok, d)
    combf = comb.reshape(n_tok, LANES)
    col, xg = pl.pallas_call(
        _moe_gather_kernel,
        out_shape=(jax.ShapeDtypeStruct((n_tok, LANES), F32),
                   jax.ShapeDtypeStruct((rows_global, d), BF16)),
        grid_spec=pltpu.PrefetchScalarGridSpec(
            num_scalar_prefetch=4,
            grid=(n_blk,),
            in_specs=[pl.BlockSpec((nb, d), lambda j, *_: (j, 0)),
                      pl.BlockSpec((nb, LANES), lambda j, *_: (j, 0))],
            out_specs=(pl.BlockSpec((nb, LANES), lambda j, *_: (j, 0)),
                       pl.BlockSpec(memory_space=pl.ANY)),
            scratch_shapes=[
                pltpu.VMEM((rows_local, d), BF16),
                pltpu.VMEM((LANES, nb), F32),
                pltpu.VMEM((tile, d), BF16),
                pltpu.SemaphoreType.DMA,
            ]),
        compiler_params=pltpu.CompilerParams(dimension_semantics=("arbitrary",),
                                             vmem_limit_bytes=VMEM_LIMIT_BYTES),
        name="moe_gather",
    )(nloc, gtile, padstart, pad, h2f, combf)

    last = lambda t, te, nt: jnp.minimum(t, nt[0] - 1)
    y = pl.pallas_call(
        _moe_ffn_kernel,
        out_shape=jax.ShapeDtypeStruct((rows_global, d), BF16),
        grid_spec=pltpu.PrefetchScalarGridSpec(
            num_scalar_prefetch=2,
            grid=(n_ffn_max,),
            in_specs=[pl.BlockSpec((ftm, d), lambda t, te, nt: (last(t, te, nt), 0)),
                      pl.BlockSpec((None, d, EXPERT_FF), lambda t, te, nt: (te[last(t, te, nt)], 0, 0)),
                      pl.BlockSpec((None, d, EXPERT_FF), lambda t, te, nt: (te[last(t, te, nt)], 0, 0)),
                      pl.BlockSpec((None, EXPERT_FF, d), lambda t, te, nt: (te[last(t, te, nt)], 0, 0))],
            out_specs=pl.BlockSpec((ftm, d), lambda t, te, nt: (last(t, te, nt), 0))),
        compiler_params=pltpu.CompilerParams(dimension_semantics=("arbitrary",),
                                             vmem_limit_bytes=VMEM_LIMIT_BYTES),
        name="moe_ffn",
    )(texp, n_ffn, xg, w_gate, w_up, w_down)

    out = pl.pallas_call(
        _moe_scatter_kernel,
        out_shape=jax.ShapeDtypeStruct((n_tok, d), F32),
        grid_spec=pltpu.PrefetchScalarGridSpec(
            num_scalar_prefetch=2,
            grid=(n_blk,),
            in_specs=[pl.BlockSpec((nb, LANES), lambda j, *_: (j, 0)),
                      pl.BlockSpec((nb, d), lambda j, *_: (j, 0)),
                      pl.BlockSpec((None, 6, d), lambda j, *_: ((j * nb) // seq, 0, 0)),
                      pl.BlockSpec(memory_space=pl.ANY)],
            out_specs=pl.BlockSpec((nb, d), lambda j, *_: (j, 0)),
            scratch_shapes=[pltpu.VMEM((2, rows_local, d), BF16),
                            pltpu.SemaphoreType.DMA((2,))]),
        compiler_params=pltpu.CompilerParams(dimension_semantics=("arbitrary",),
                                             vmem_limit_bytes=VMEM_LIMIT_BYTES),
        name="moe_scatter",
    )(nloc, gtile, col, x1.reshape(n_tok, d), mod, y)
    return out.reshape(bsz, seq, d)


def _layer(x, mod, rel_bias, norm1, w_in, q_norm, k_norm, conv_w, attn_out_norm, conv_out_norm, w_out,
           norm2, w_group_router, b_group_router, w_expert_router, b_expert_router, w_gate, w_up, w_down):
    bsz, seq, d = x.shape
    aw = ATTN_WIDTH
    topk = min(TOPK_MAX, seq // 4)

    offs = np.cumsum([0, aw, aw, aw, IDX_HEADS * IDX_DIM, IDX_DIM, IDX_HEADS, CONV_WIDTH, CONV_WIDTH, CONV_WIDTH])
    col = lambda n: w_in[:, int(offs[n]):int(offs[n + 1])]
    wm = jnp.concatenate([col(0), col(1), col(3), col(6), col(7), col(8)], axis=1).astype(BF16)
    wvt = col(2).T.astype(BF16)
    wki = jnp.concatenate([col(4), col(4)], axis=1).astype(BF16)
    wwit = col(5).T.astype(BF16)
    qg = (jnp.tile(q_norm, ATTN_HEADS) * ((HEAD_DIM ** -0.5) * LOG2E))[None, :]
    kg = jnp.tile(k_norm, ATTN_HEADS)[None, :]
    grp = np.arange(aw) // CONV_GROUP_DIM
    gmat = jnp.asarray((grp[:, None] == grp[None, :]).astype(np.float32) / CONV_GROUP_DIM, dtype=BF16)

    q, k, vt, qi, ki, wit, cn = _pre_call(
        x, mod, norm1[None, :], wm, wvt, wki, wwit, qg, kg, conv_w, conv_out_norm.reshape(1, -1), gmat)

    bounds = jnp.asarray(_bucket_boundaries())
    an = _attn_call(rel_bias, bounds, q, qi, wit, k, ki, vt, attn_out_norm.reshape(1, -1), topk)

    wr = jnp.concatenate([w_expert_router, w_group_router,
                          jnp.zeros((d, LANES - N_EXPERTS - N_GROUPS), F32)], axis=1).astype(BF16)
    br = jnp.concatenate([b_expert_router, b_group_router,
                          jnp.zeros((LANES - N_EXPERTS - N_GROUPS,), F32)])[None, :]
    x1, h2, comb, cnt_tiles = _post_call(an, cn, x, mod, norm2[None, :], w_out[:aw].astype(BF16),
                                         w_out[aw:].astype(BF16), wr, br)

    return _moe_call(h2, comb, cnt_tiles, x1, mod, w_gate, w_up, w_down)


def kernel(x, c, rel_bias, w_ada, b_ada, norm1, w_in, q_norm, k_norm, conv_w, attn_out_norm, conv_out_norm,
           w_out, norm2, w_group_router, b_group_router, w_expert_router, b_expert_router, w_gate, w_up,
           w_down):
    bsz, seq, d = x.shape
    assert d == D_MODEL and seq % max(PRE_TM, POST_TM, MOE_TM) == 0 and ATT_TQ == ATT_TK
    depth = w_ada.shape[0]
    for l in range(depth):
        mod = _mod_call(c, w_ada[l], b_ada[l][None, :]).reshape(bsz, 6, d)
        x = _layer(x, mod, rel_bias, norm1[l], w_in[l], q_norm[l], k_norm[l], conv_w[l], attn_out_norm[l],
                   conv_out_norm[l], w_out[l], norm2[l], w_group_router[l], b_group_router[l],
                   w_expert_router[l], b_expert_router[l], w_gate[l], w_up[l], w_down[l])
    return x
```

```python
import functools
import math

import jax
import jax.numpy as jnp
import numpy as np
from jax import lax
from jax.experimental import pallas as pl
from jax.experimental.pallas import tpu as pltpu

F32 = jnp.float32
BF16 = jnp.bfloat16

D_MODEL = 1024
HEAD_DIM = 64
ATTN_HEADS = 8
ATTN_WIDTH = ATTN_HEADS * HEAD_DIM
CONV_WIDTH = D_MODEL - ATTN_WIDTH
CONV_GROUP_DIM = 64
CONV_K = 3
IDX_HEADS = 8
IDX_DIM = 64
TOPK_MAX = 256
IDX_SCALE = (IDX_DIM ** -0.5) * (IDX_HEADS ** -0.5)
N_BUCKETS = 32
MAX_DISTANCE = 128
N_GROUPS = 4
EXPERTS_PER_GROUP = 8
N_EXPERTS = N_GROUPS * EXPERTS_PER_GROUP
EXPERT_FF = 256
EPS = 1e-6
LOG2E = 1.4426950408889634
NEG_BIG = -1e30
COUNT_ACCS = 4
BISECT_GROUP = 2
BISECT_BF16_STEPS = 10
BISECT_VALUE_STEPS = 4
BISECT_MAX_STEPS = 64

LANES = 128
SUBLANES = 8
BF16_SUBLANES = 16
V_SLAB = HEAD_DIM + BF16_SUBLANES
VMEM_LIMIT_BYTES = 56 * 1024 * 1024

PRE_TM = 512
ATT_TQ = 256
ATT_TK = 256
POST_TM = 512
POST_SLABS = 2
MOE_TM = 512
MOE_TILE = 32
MOE_CHUNK = 512
MOE_USUAL_CHUNKS = 3
MOE_FFN_TM = 1024
MOD_TN = 1536

_NT_DIMS = (((1,), (1,)), ((), ()))


def _tree_sum(parts):
    while len(parts) > 1:
        nxt = [parts[j] + parts[j + 1] for j in range(0, len(parts) - 1, 2)]
        if len(parts) % 2:
            nxt.append(parts[-1])
        parts = nxt
    return parts[0]


def _bucket_boundaries():
    max_exact = N_BUCKETS // 2
    d = np.arange(0, 4 * MAX_DISTANCE, dtype=np.int64)
    nf = np.maximum(d, 1).astype(np.float32)
    large = max_exact + (np.log(nf / np.float32(max_exact)) / np.float32(math.log(MAX_DISTANCE / max_exact))
                         * np.float32(N_BUCKETS - max_exact)).astype(np.int32)
    large = np.minimum(large, N_BUCKETS - 1)
    bucket = np.where(d < max_exact, d, large)
    assert np.all(np.diff(bucket) >= 0) and bucket[-1] == N_BUCKETS - 1
    bounds = [int(np.argmax(bucket >= j)) for j in range(1, N_BUCKETS)]
    return np.asarray([0] + bounds, dtype=np.int32)


def _mod_kernel(c_ref, w_ref, b_ref, o_ref):
    c = c_ref[...]
    act = c * jax.nn.sigmoid(c)
    o_ref[...] = jnp.dot(act, w_ref[...], preferred_element_type=F32,
                         precision=lax.Precision.HIGHEST) + b_ref[...]


def _mod_call(c, w_ada, b_ada):
    bsz, d = c.shape
    n = w_ada.shape[1]
    return pl.pallas_call(
        _mod_kernel,
        out_shape=jax.ShapeDtypeStruct((bsz, n), F32),
        grid=(n // MOD_TN,),
        in_specs=[pl.BlockSpec((bsz, d), lambda j: (0, 0)),
                  pl.BlockSpec((d, MOD_TN), lambda j: (0, j)),
                  pl.BlockSpec((1, MOD_TN), lambda j: (0, j))],
        out_specs=pl.BlockSpec((bsz, MOD_TN), lambda j: (0, j)),
        compiler_params=pltpu.CompilerParams(dimension_semantics=("arbitrary",),
                                             vmem_limit_bytes=VMEM_LIMIT_BYTES),
        name="adaln_mod",
    )(c, w_ada, b_ada)


def _group_rms(y, g_ref):
    ms = jnp.dot((y * y).astype(BF16), g_ref[...], preferred_element_type=F32)
    return y * lax.rsqrt(ms + EPS)


def _pre_kernel(x_ref, mod_ref, n1_ref, wm_ref, wvt_ref, wki_ref, wwit_ref, qg_ref, kg_ref,
                cw_ref, cg_ref, g_ref,
                q_ref, k_ref, vt_ref, qi_ref, ki_ref, wit_ref, cn_ref, carry_ref):
    j = pl.program_id(1)
    tm = x_ref.shape[0]
    aw = ATTN_WIDTH

    x = x_ref[...]
    ms = jnp.mean(x * x, axis=-1, keepdims=True)
    y = x * lax.rsqrt(ms + EPS) * n1_ref[...]
    h = y * (1.0 + mod_ref[1:2, :]) + mod_ref[0:1, :]
    hb = h.astype(BF16)

    def proj(lo):
        return jnp.dot(hb, wm_ref[:, lo:lo + aw], preferred_element_type=F32)

    q = _group_rms(proj(0), g_ref) * qg_ref[...]
    q_ref[...] = q.astype(BF16)
    k = _group_rms(proj(aw), g_ref) * kg_ref[...]
    k_ref[...] = k.astype(BF16)

    vt = lax.dot_general(wvt_ref[...], hb, _NT_DIMS, preferred_element_type=F32).astype(BF16)
    ones = jnp.ones((BF16_SUBLANES, ATT_TK), BF16)
    for cc in range(tm // ATT_TK):
        for hh in range(ATTN_HEADS):
            vt_ref[cc, hh * V_SLAB:hh * V_SLAB + HEAD_DIM, :] = (
                vt[hh * HEAD_DIM:(hh + 1) * HEAD_DIM, cc * ATT_TK:(cc + 1) * ATT_TK])
            vt_ref[cc, hh * V_SLAB + HEAD_DIM:(hh + 1) * V_SLAB, :] = ones

    qi_ref[...] = proj(2 * aw).astype(BF16)
    ki_ref[...] = jnp.dot(hb, wki_ref[...], preferred_element_type=F32).astype(BF16)
    wit_ref[...] = lax.dot_general(wwit_ref[...], hb, _NT_DIMS, preferred_element_type=F32) * IDX_SCALE

    assert CONV_K == 3 and cw_ref.shape[0] == CONV_K
    gate_b = proj(3 * aw)
    z = proj(4 * aw) * proj(5 * aw)

    @pl.when(j == 0)
    def _():
        carry_ref[...] = jnp.zeros_like(carry_ref)

    prev = carry_ref[...]
    row = lax.broadcasted_iota(jnp.int32, z.shape, 0)
    z1 = jnp.where(row == 0, prev[SUBLANES - 1:SUBLANES, :], pltpu.roll(z, 1, 0))
    z2 = pltpu.roll(z, 2, 0)
    z2 = jnp.where(row == 0, prev[SUBLANES - 2:SUBLANES - 1, :], z2)
    z2 = jnp.where(row == 1, prev[SUBLANES - 1:SUBLANES, :], z2)
    carry_ref[...] = z[tm - SUBLANES:, :]
    conv = cw_ref[2:3, :] * z + cw_ref[1:2, :] * z1 + cw_ref[0:1, :] * z2
    yc = gate_b * conv
    cn_ref[...] = (_group_rms(yc, g_ref) * cg_ref[...]).astype(BF16)


def _pre_call(x, mod, n1, wm, wvt, wki, wwit, qg, kg, cw, cg, gmat):
    bsz, seq, d = x.shape
    tm = PRE_TM
    nck = tm // ATT_TK
    aw = ATTN_WIDTH
    const = lambda b, j: (0, 0)
    tok = lambda b, j: (b, j, 0)
    out_shape = (
        jax.ShapeDtypeStruct((bsz, seq, aw), BF16),
        jax.ShapeDtypeStruct((bsz, seq, aw), BF16),
        jax.ShapeDtypeStruct((bsz, seq // ATT_TK, ATTN_HEADS * V_SLAB, ATT_TK), BF16),
        jax.ShapeDtypeStruct((bsz, seq, aw), BF16),
        jax.ShapeDtypeStruct((bsz, seq, LANES), BF16),
        jax.ShapeDtypeStruct((bsz, IDX_HEADS, seq), F32),
        jax.ShapeDtypeStruct((bsz, seq, CONV_WIDTH), BF16),
    )
    out_specs = (
        pl.BlockSpec((None, tm, aw), tok),
        pl.BlockSpec((None, tm, aw), tok),
        pl.BlockSpec((None, nck, ATTN_HEADS * V_SLAB, ATT_TK), lambda b, j: (b, j, 0, 0)),
        pl.BlockSpec((None, tm, aw), tok),
        pl.BlockSpec((None, tm, LANES), tok),
        pl.BlockSpec((None, IDX_HEADS, tm), lambda b, j: (b, 0, j)),
        pl.BlockSpec((None, tm, CONV_WIDTH), tok),
    )
    in_specs = [
        pl.BlockSpec((None, tm, d), tok),
        pl.BlockSpec((None, 6, d), lambda b, j: (b, 0, 0)),
        pl.BlockSpec(n1.shape, const),
        pl.BlockSpec(wm.shape, const),
        pl.BlockSpec(wvt.shape, const),
        pl.BlockSpec(wki.shape, const),
        pl.BlockSpec(wwit.shape, const),
        pl.BlockSpec(qg.shape, const),
        pl.BlockSpec(kg.shape, const),
        pl.BlockSpec(cw.shape, const),
        pl.BlockSpec(cg.shape, const),
        pl.BlockSpec(gmat.shape, const),
    ]
    return pl.pallas_call(
        _pre_kernel,
        out_shape=out_shape,
        grid=(bsz, seq // tm),
        in_specs=in_specs,
        out_specs=out_specs,
        scratch_shapes=[pltpu.VMEM((SUBLANES, CONV_WIDTH), F32)],
        compiler_params=pltpu.CompilerParams(dimension_semantics=("arbitrary", "arbitrary"),
                                             vmem_limit_bytes=VMEM_LIMIT_BYTES),
        name="pre_proj",
    )(x, mod, n1, wm, wvt, wki, wwit, qg, kg, cw, cg, gmat)


def _attn_kernel(rb_ref, bnd_ref, q_ref, qi_ref, wit_ref, k_ref, ki_ref, vt_ref, og_ref,
                 o_ref,
                 s_ref, s16_ref, bias_ref, qpad_ref, qipad_ref, lg_ref, acc_ref, out_ref, *, topk):
    b = pl.program_id(0)
    i = pl.program_id(1)
    tq, tk = ATT_TQ, ATT_TK
    nh, hd = ATTN_HEADS, HEAD_DIM

    t_loc = lax.broadcasted_iota(jnp.int32, (tk, tq), 1)
    s_loc = lax.broadcasted_iota(jnp.int32, (tk, tq), 0)

    @pl.when((b == 0) & (i == 0))
    def _():
        for idx in range(2):
            dist = t_loc - s_loc + idx * tq
            for h in range(nh):
                bias_ref[idx, h] = jnp.full((tk, tq), (rb_ref[0, h] - rb_ref[N_BUCKETS - 1, h]) * LOG2E, F32)

            def fill(jb, carry):
                reached = dist >= bnd_ref[jb]
                for h in range(nh):
                    val = (rb_ref[jb, h] - rb_ref[N_BUCKETS - 1, h]) * LOG2E
                    bias_ref[idx, h] = jnp.where(reached, val, bias_ref[idx, h])
                return carry

            lax.fori_loop(1, N_BUCKETS, fill, 0)

    lane = lax.broadcasted_iota(jnp.int32, (tq, LANES), 1)
    for h in range(nh):
        pair = slice((h // 2) * LANES, (h // 2 + 1) * LANES)
        keep = (lane // hd) == (h % 2)
        qpad_ref[h] = jnp.where(keep, q_ref[:, pair], jnp.zeros((), BF16))
        qipad_ref[h] = jnp.where(keep, qi_ref[:, pair], jnp.zeros((), BF16))

    def idx_dots(c, slot):
        kic = ki_ref[pl.ds(pl.multiple_of(c * tk, tk), tk), :]
        for h in range(nh):
            lg_ref[slot, h] = lax.dot_general(kic, qipad_ref[h], _NT_DIMS, preferred_element_type=F32)

    def idx_reduce(c, slot, carry, diagonal):
        rmin, rmax = carry
        sc = _tree_sum([wit_ref[h:h + 1, :] * jnp.maximum(lg_ref[slot, h], 0.0) for h in range(nh)])
        if diagonal:
            causal = s_loc <= t_loc
            lo_c, hi_c = jnp.where(causal, sc, jnp.inf), jnp.where(causal, sc, -jnp.inf)
            sc = hi_c
        else:
            lo_c, hi_c = sc, sc
        s_ref[c] = sc
        s16_ref[c] = sc.astype(BF16)
        return (jnp.minimum(rmin, jnp.min(lo_c, axis=0, keepdims=True)),
                jnp.maximum(rmax, jnp.max(hi_c, axis=0, keepdims=True)))

    def idx_pair(jj, carry):
        idx_dots(2 * jj + 1, 1)
        carry = idx_reduce(2 * jj, 0, carry, False)
        idx_dots(2 * jj + 2, 0)
        return idx_reduce(2 * jj + 1, 1, carry, False)

    def idx_tail_odd(carry):
        idx_dots(i, 1)
        return idx_reduce(i, 1, idx_reduce(i - 1, 0, carry, False), True)

    idx_dots(0, 0)
    carry = (jnp.full((1, tq), jnp.inf, F32), jnp.full((1, tq), -jnp.inf, F32))
    carry = lax.fori_loop(0, i // 2, idx_pair, carry)
    rmin, rmax = lax.cond((i & 1) == 1, idx_tail_odd, lambda cr: idx_reduce(i, 0, cr, True), carry)

    def count_ge(thr):
        def body(c, accs):
            hit = s_ref[c] >= thr
            accs = list(accs)
            for r in range(tk // SUBLANES):
                a = accs[r % COUNT_ACCS]
                accs[r % COUNT_ACCS] = jnp.where(hit[r * SUBLANES:(r + 1) * SUBLANES], a + 1.0, a)
            return tuple(accs)
        accs = lax.fori_loop(0, i + 1, body,
                             tuple(jnp.zeros((SUBLANES, tq), F32) for _ in range(COUNT_ACCS)))
        return jnp.sum(_tree_sum(list(accs)), axis=0, keepdims=True)

    def count16_ge(thr16):
        def body(c, accs):
            hit = s16_ref[c] >= thr16
            accs = list(accs)
            for r in range(tk // BF16_SUBLANES):
                a = accs[r % COUNT_ACCS]
                accs[r % COUNT_ACCS] = jnp.where(hit[r * BF16_SUBLANES:(r + 1) * BF16_SUBLANES], a + 1, a)
            return tuple(accs)
        accs = lax.fori_loop(0, i + 1, body,
                             tuple(jnp.zeros((BF16_SUBLANES, tq), BF16) for _ in range(COUNT_ACCS)))
        return jnp.sum(_tree_sum(list(accs)).astype(F32), axis=0, keepdims=True)

    int_min = jnp.int32(-2 ** 31)

    def order_key(v):
        bits = pltpu.bitcast(v, jnp.int32)
        return jnp.where(bits < 0, -(bits & jnp.int32(0x7FFFFFFF)), bits)

    def from_order_key(key):
        return pltpu.bitcast(jnp.where(key < 0, (-key) | int_min, key), F32)

    t_glob = (i * tq + lax.broadcasted_iota(jnp.int32, (1, tq), 1)).astype(F32)
    n_causal = t_glob + 1.0
    kf = jnp.minimum(float(topk), n_causal)
    all_sel = n_causal <= kf

    lo16 = order_key(rmin) >> 16
    hi16 = (order_key(rmax.astype(BF16).astype(F32) + 0.0) >> 16) + 1
    for _ in range(BISECT_BF16_STEPS):
        lo_v = from_order_key(lo16 << 16)
        hi_v = from_order_key(hi16 << 16)
        mid_val16 = order_key(lo_v + (hi_v - lo_v) * 0.5) >> 16
        mid16 = jnp.where((mid_val16 > lo16) & (mid_val16 < hi16), mid_val16, (lo16 + hi16) >> 1)
        open_ = (hi16 - lo16) > 1
        cm = count16_ge(from_order_key(mid16 << 16).astype(BF16))
        lo16 = jnp.where(open_ & (cm >= kf), mid16, lo16)
        hi16 = jnp.where(open_ & (cm < kf), mid16, hi16)

    min_normal_key = jnp.int32(0x00800000)

    def snap(key, direction):
        sub = (key > -min_normal_key) & (key < min_normal_key) & (key != 0)
        below = jnp.where(key > 0, 0, -min_normal_key)
        above = jnp.where(key > 0, min_normal_key, 0)
        return jnp.where(sub, {"down": below, "up": above, "zero": jnp.zeros_like(key)}[direction], key)

    lo0 = snap(jnp.maximum((lo16 - 1) << 16, order_key(rmin)), "down")
    hi0 = snap(hi16 << 16, "up")
    state0 = (jnp.where(all_sel, 0.0, 1.0), lo0, hi0, count_ge(from_order_key(hi0)),
              order_key(rmin), jnp.zeros((1, tq), F32), jnp.full((1, tq), 0x7F800000, jnp.int32), kf)

    def bisect_step(st, value_mid, force_end):
        active, lo_key, hi_key, fhi, thr_key, tie, hif_key, need = st
        mid_key = snap((lo_key >> 1) + (hi_key >> 1) + (lo_key & hi_key & 1), "zero")
        if value_mid:
            lo = from_order_key(lo_key)
            hi = from_order_key(hi_key)
            val_key = snap(order_key(lo + (hi - lo) * 0.5), "zero")
            mid_key = jnp.where((val_key > lo_key) & (val_key < hi_key), val_key, mid_key)
        collapsed = (mid_key <= lo_key) | (mid_key >= hi_key) | force_end
        cm = count_ge(from_order_key(mid_key))
        act = active > 0.0
        live = act & jnp.logical_not(collapsed)
        found = live & (cm == kf)
        go_up = live & (cm > kf)
        go_dn = live & (cm < kf)
        ends_tie = act & collapsed
        thr_key = jnp.where(found, mid_key, jnp.where(ends_tie, lo_key, thr_key))
        tie = jnp.where(ends_tie, 1.0, tie)
        hif_key = jnp.where(ends_tie, hi_key, hif_key)
        need = jnp.where(ends_tie, kf - fhi, need)
        lo_key = jnp.where(go_up, mid_key, lo_key)
        fhi = jnp.where(go_dn, cm, fhi)
        hi_key = jnp.where(go_dn, mid_key, hi_key)
        active = jnp.where(found | ends_tie, 0.0, active)
        return active, lo_key, hi_key, fhi, thr_key, tie, hif_key, need

    state = state0
    for _ in range(BISECT_VALUE_STEPS):
        state = bisect_step(state, True, False)

    def b_cond(carry):
        st, step = carry
        return (jnp.max(st[0]) > 0.0) & (step <= BISECT_MAX_STEPS)

    def b_group(carry):
        st, step = carry
        for _ in range(BISECT_GROUP):
            st = bisect_step(st, False, step >= BISECT_MAX_STEPS)
        return st, step + BISECT_GROUP

    (_, _, _, _, thr_key, tie, hif_key, need), _ = lax.while_loop(b_cond, b_group, (state, jnp.int32(0)))
    thr = from_order_key(thr_key)
    hif = from_order_key(hif_key)

    @pl.when(jnp.max(tie) > 0.0)
    def _():
        tri = jnp.where(lax.broadcasted_iota(jnp.int32, (tk, tk), 1)
                        <= lax.broadcasted_iota(jnp.int32, (tk, tk), 0), 1.0, 0.0).astype(BF16)

        def body(c, seen):
            sc_c = s_ref[c]
            tied = (sc_c >= thr) & (sc_c < hif) & (tie > 0.0)
            rank = jnp.dot(tri, jnp.where(tied, 1.0, 0.0).astype(BF16), preferred_element_type=F32) + seen
            s_ref[c] = jnp.where(tied & (rank > need), -jnp.inf, sc_c)
            return rank[tk - 1:tk, :]

        lax.fori_loop(0, i + 1, body, jnp.zeros((1, tq), F32))

    acc_ref[...] = jnp.zeros(acc_ref.shape, F32)

    def store_logits(c, slot, bias_idx):
        masked = jnp.where(s_ref[c] >= thr, 0.0, NEG_BIG)
        row0 = pl.multiple_of(c * tk, tk)
        for h in range(nh):
            kc = k_ref[pl.ds(row0, tk), (h // 2) * LANES:(h // 2 + 1) * LANES]
            lt = lax.dot_general(kc, qpad_ref[h], _NT_DIMS, preferred_element_type=F32) + masked
            if bias_idx is not None:
                lt = lt + bias_ref[bias_idx, h]
            lg_ref[slot, h] = lt

    def softmax_pv(c, slot, m_all):
        m_out = []
        for h in range(nh):
            m_old = m_all[h]
            m_new = jnp.maximum(m_old, jnp.max(lg_ref[slot, h], axis=0, keepdims=True))
            p = jnp.exp2(lg_ref[slot, h] - m_new).astype(BF16)
            alpha = jnp.exp2(m_old - m_new)
            pv = jnp.dot(vt_ref[c, h * V_SLAB:(h + 1) * V_SLAB, :], p, preferred_element_type=F32)
            acc_ref[h] = alpha * acc_ref[h] + pv
            m_out.append(m_new)
        return tuple(m_out)

    def near_step(m_all):
        store_logits(i - 1, 1, 1)
        return softmax_pv(i, 0, m_all)

    def far_step(j, parity, m_all):
        c = i - 2 - j
        store_logits(c, parity, None)
        return softmax_pv(c + 1, 1 - parity, m_all)

    def far_pair(jj, m_all):
        return far_step(2 * jj + 1, 1, far_step(2 * jj, 0, m_all))

    n_far = jnp.maximum(i - 1, 0)
    m_all = tuple(jnp.full((1, tq), NEG_BIG, F32) for _ in range(nh))
    store_logits(i, 0, 0)
    m_all = lax.cond(i >= 1, near_step, lambda m: m, m_all)
    m_all = lax.fori_loop(0, n_far // 2, far_pair, m_all)
    m_all = lax.cond((n_far & 1) == 1, lambda m: far_step(n_far - 1, 0, m), lambda m: m, m_all)
    lax.cond((i & 1) == 0, lambda m: softmax_pv(0, 0, m), lambda m: softmax_pv(0, 1, m), m_all)

    for h in range(nh):
        o = acc_ref[h, :hd, :] / acc_ref[h, hd:hd + 1, :]
        ms = jnp.mean(o * o, axis=0, keepdims=True)
        out_ref[h * hd:(h + 1) * hd, :] = o * lax.rsqrt(ms + EPS)
    o_ref[...] = (out_ref[...].T * og_ref[...]).astype(BF16)


def _attn_call(rel_bias, bounds, q, qi, wit, k, ki, vt, og, topk):
    bsz, seq, aw = q.shape
    tq, tk = ATT_TQ, ATT_TK
    nck = seq // tk
    blk_q = lambda b, i: (b, i, 0)
    whole = lambda b, i: (b, 0, 0)
    smem = pl.BlockSpec(memory_space=pltpu.SMEM)
    return pl.pallas_call(
        functools.partial(_attn_kernel, topk=topk),
        out_shape=jax.ShapeDtypeStruct((bsz, seq, aw), BF16),
        grid=(bsz, seq // tq),
        in_specs=[
            smem, smem,
            pl.BlockSpec((None, tq, aw), blk_q),
            pl.BlockSpec((None, tq, aw), blk_q),
            pl.BlockSpec((None, IDX_HEADS, tq), lambda b, i: (b, 0, i)),
            pl.BlockSpec((None, seq, aw), whole),
            pl.BlockSpec((None, seq, LANES), whole),
            pl.BlockSpec((None, nck, ATTN_HEADS * V_SLAB, tk), lambda b, i: (b, 0, 0, 0)),
            pl.BlockSpec(og.shape, lambda b, i: (0, 0)),
        ],
        out_specs=pl.BlockSpec((None, tq, aw), blk_q),
        scratch_shapes=[
            pltpu.VMEM((nck, tk, tq), F32),
            pltpu.VMEM((nck, tk, tq), BF16),
            pltpu.VMEM((2, ATTN_HEADS, tk, tq), F32),
            pltpu.VMEM((ATTN_HEADS, tq, LANES), BF16),
            pltpu.VMEM((IDX_HEADS, tq, LANES), BF16),
            pltpu.VMEM((2, ATTN_HEADS, tk, tq), F32),
            pltpu.VMEM((ATTN_HEADS, V_SLAB, tq), F32),
            pltpu.VMEM((aw, tq), F32),
        ],
        compiler_params=pltpu.CompilerParams(dimension_semantics=("arbitrary", "arbitrary"),
                                             vmem_limit_bytes=VMEM_LIMIT_BYTES),
        name="dsa_attention",
    )(rel_bias, bounds, q, qi, wit, k, ki, vt, og)


def _post_kernel(an_ref, cn_ref, x_ref, mod_ref, n2_ref, woa_ref, woc_ref, wr_ref, br_ref,
                 x1_ref, h2_ref, comb_ref, cnt_ref):
    slab = x_ref.shape[0] // POST_SLABS
    cnt = jnp.zeros((1, LANES), F32)
    for s in range(POST_SLABS):
        cnt = cnt + _post_slab(slice(s * slab, (s + 1) * slab), an_ref, cn_ref, x_ref, mod_ref, n2_ref,
                               woa_ref, woc_ref, wr_ref, br_ref, x1_ref, h2_ref, comb_ref)
    cnt_ref[...] = jnp.broadcast_to(cnt, cnt_ref.shape)


def _post_slab(rows, an_ref, cn_ref, x_ref, mod_ref, n2_ref, woa_ref, woc_ref, wr_ref, br_ref,
               x1_ref, h2_ref, comb_ref):
    mix = (jnp.dot(an_ref[rows, :], woa_ref[...], preferred_element_type=F32)
           + jnp.dot(cn_ref[rows, :], woc_ref[...], preferred_element_type=F32))
    x1 = x_ref[rows, :] + mod_ref[2:3, :] * mix
    x1_ref[rows, :] = x1
    ms = jnp.mean(x1 * x1, axis=-1, keepdims=True)
    h2 = x1 * lax.rsqrt(ms + EPS) * n2_ref[...] * (1.0 + mod_ref[4:5, :]) + mod_ref[3:4, :]
    h2b = h2.astype(BF16)
    h2_ref[rows, :] = h2b

    logits = jnp.dot(h2b, wr_ref[...], preferred_element_type=F32) + br_ref[...]
    lane = lax.broadcasted_iota(jnp.int32, logits.shape, 1)
    lane_f = lane.astype(F32)
    far = float(LANES)
    is_g = (lane >= N_EXPERTS) & (lane < N_EXPERTS + N_GROUPS)
    gl = jnp.where(is_g, logits, -jnp.inf)
    gmax = jnp.max(gl, axis=-1, keepdims=True)
    g_sel = jnp.min(jnp.where(is_g & (gl == gmax), lane_f, far), axis=-1, keepdims=True) - float(N_EXPERTS)
    p_g = 1.0 / jnp.sum(jnp.exp(gl - gmax), axis=-1, keepdims=True)

    in_grp = (lane < N_EXPERTS) & ((lane // EXPERTS_PER_GROUP).astype(F32) == g_sel)
    e1 = jnp.where(in_grp, logits, -jnp.inf)
    l1 = jnp.max(e1, axis=-1, keepdims=True)
    i1 = jnp.min(jnp.where(in_grp & (e1 == l1), lane_f, far), axis=-1, keepdims=True)
    rest = in_grp & (lane_f != i1)
    e2 = jnp.where(rest, logits, -jnp.inf)
    l2 = jnp.max(e2, axis=-1, keepdims=True)
    i2 = jnp.min(jnp.where(rest & (e2 == l2), lane_f, far), axis=-1, keepdims=True)
    r = jnp.exp(l2 - l1)
    w1 = 1.0 / (1.0 + r)
    w2 = r / (1.0 + r)
    comb = jnp.where(lane_f == i1, p_g * w1, 0.0) + jnp.where(lane_f == i2, p_g * w2, 0.0)
    comb_ref[rows, :] = comb
    return jnp.sum(jnp.where(comb != 0.0, 1.0, 0.0), axis=0, keepdims=True)


def _post_call(an, cn, x, mod, n2, woa, woc, wr, br):
    bsz, seq, d = x.shape
    tm = POST_TM
    tok = lambda b, j: (b, j, 0)
    const = lambda b, j: (0, 0)
    return pl.pallas_call(
        _post_kernel,
        out_shape=(jax.ShapeDtypeStruct((bsz, seq, d), F32),
                   jax.ShapeDtypeStruct((bsz, seq, d), BF16),
                   jax.ShapeDtypeStruct((bsz, seq, LANES), F32),
                   jax.ShapeDtypeStruct((bsz, seq // tm, SUBLANES, LANES), F32)),
        grid=(bsz, seq // tm),
        in_specs=[
            pl.BlockSpec((None, tm, ATTN_WIDTH), tok),
            pl.BlockSpec((None, tm, CONV_WIDTH), tok),
            pl.BlockSpec((None, tm, d), tok),
            pl.BlockSpec((None, 6, d), lambda b, j: (b, 0, 0)),
            pl.BlockSpec(n2.shape, const),
            pl.BlockSpec(woa.shape, const),
            pl.BlockSpec(woc.shape, const),
            pl.BlockSpec(wr.shape, const),
            pl.BlockSpec(br.shape, const),
        ],
        out_specs=(pl.BlockSpec((None, tm, d), tok),
                   pl.BlockSpec((None, tm, d), tok),
                   pl.BlockSpec((None, tm, LANES), tok),
                   pl.BlockSpec((None, None, SUBLANES, LANES), lambda b, j: (b, j, 0, 0))),
        compiler_params=pltpu.CompilerParams(dimension_semantics=("arbitrary", "arbitrary"),
                                             vmem_limit_bytes=VMEM_LIMIT_BYTES),
        name="post_router",
    )(an, cn, x, mod, n2, woa, woc, wr, br)


def _strict_tri(n, lower):
    r = lax.broadcasted_iota(jnp.int32, (n, n), 0)
    c = lax.broadcasted_iota(jnp.int32, (n, n), 1)
    return jnp.where((c < r) if lower else (r < c), 1.0, 0.0).astype(BF16)


def _moe_tile_copies(nloc_ref, gtile_ref, blk, local_ref, global_ref, sem, to_global, wait):
    tile = MOE_TILE

    def one(lt, priority):
        loc = local_ref.at[pl.ds(pl.multiple_of(lt * tile, tile), tile), :]
        glo = global_ref.at[pl.ds(pl.multiple_of(gtile_ref[blk, lt] * tile, tile), tile), :]
        cp = pltpu.make_async_copy(loc, glo, sem) if to_global else pltpu.make_async_copy(glo, loc, sem)
        if wait:
            cp.wait()
        else:
            cp.start(priority=priority)

    def per_pair(pr, c):
        one(2 * pr, 0)
        one(2 * pr + 1, 1)
        return c

    n = nloc_ref[blk]
    lax.fori_loop(0, n // 2, per_pair, 0)

    @pl.when((n & 1) == 1)
    def _():
        one(n - 1, 0)


def _moe_gather_kernel(nloc_ref, gtile_ref, padstart_ref, pad_ref,
                       h2_ref, comb_ref,
                       col_ref, xg_hbm,
                       xg_ref, row_ref, zero_ref, sem):
    blk = pl.program_id(0)
    nb = h2_ref.shape[0]
    tile, chunk = MOE_TILE, MOE_CHUNK
    lane = lax.broadcasted_iota(jnp.int32, (nb, LANES), 1)

    comb = comb_ref[...]
    assigned = comb != 0.0
    a_f = jnp.where(assigned, 1.0, 0.0)
    rank = jnp.dot(_strict_tri(nb, True), a_f.astype(BF16), preferred_element_type=F32)
    cnt = rank[nb - 1:nb, :] + a_f[nb - 1:nb, :]
    ntile = jnp.floor((cnt + float(tile - 1)) * (1.0 / tile))
    first = jnp.dot(jnp.broadcast_to(ntile, (SUBLANES, LANES)).astype(BF16), _strict_tri(LANES, False),
                    preferred_element_type=F32)[0:1, :]
    pos = first * float(tile) + rank
    pos1 = jnp.min(jnp.where(assigned, pos, 1e9), axis=1, keepdims=True)
    pos2 = jnp.max(jnp.where(assigned, pos, -1.0), axis=1, keepdims=True)
    pos2 = jnp.where(pos2 == pos1, -1.0, pos2)
    cw1 = jnp.sum(jnp.where(assigned & (pos == pos1), comb, 0.0), axis=1, keepdims=True)
    cw2 = jnp.sum(jnp.where(assigned & (pos == pos2), comb, 0.0), axis=1, keepdims=True)
    info = jnp.where(lane == 0, pos1, jnp.where(lane == 1, pos2, jnp.where(lane == 2, cw1,
                     jnp.where(lane == 3, cw2, 0.0))))
    col_ref[...] = info
    row_ref[...] = info.T

    n_chunks = (nloc_ref[blk] * tile + (chunk - 1)) // chunk
    p1 = row_ref[0:1, :].astype(jnp.int32)
    p2 = row_ref[1:2, :].astype(jnp.int32)
    sub = lax.broadcasted_iota(jnp.int32, (chunk, nb), 0)

    def gather(c, carry):
        p = sub + c * chunk
        sel = jnp.where((p == p1) | (p == p2), 1.0, 0.0).astype(BF16)
        r0 = pl.multiple_of(c * chunk, chunk)
        xg_ref[pl.ds(r0, chunk), :] = jnp.dot(sel, h2_ref[...], preferred_element_type=F32).astype(BF16)
        return carry

    @pl.when(blk > 0)
    def _():
        _moe_tile_copies(nloc_ref, gtile_ref, blk - 1, xg_ref, xg_hbm, sem, True, True)

    lax.fori_loop(0, n_chunks, gather, 0)

    _moe_tile_copies(nloc_ref, gtile_ref, blk, xg_ref, xg_hbm, sem, True, False)

    is_last = blk == pl.num_programs(0) - 1

    def pad_copies(wait):
        def per_expert(x, carry):
            g0 = padstart_ref[x]

            def per_tile(j, c):
                dst = xg_hbm.at[pl.ds(pl.multiple_of((g0 + j) * tile, tile), tile), :]
                cp = pltpu.make_async_copy(zero_ref, dst, sem)
                if wait:
                    cp.wait()
                else:
                    cp.start()
                return c

            lax.fori_loop(0, pad_ref[x], per_tile, 0)
            return carry

        lax.fori_loop(0, N_EXPERTS, per_expert, 0)

    @pl.when(is_last)
    def _():
        zero_ref[...] = jnp.zeros(zero_ref.shape, BF16)
        pad_copies(False)
        _moe_tile_copies(nloc_ref, gtile_ref, blk, xg_ref, xg_hbm, sem, True, True)
        pad_copies(True)


def _moe_ffn_kernel(texp_ref, nt_ref, x_ref, wg_ref, wu_ref, wd_ref, y_ref):
    @pl.when(pl.program_id(0) < nt_ref[0])
    def _():
        x = x_ref[...]
        a = jnp.dot(x, wg_ref[...].astype(BF16), preferred_element_type=F32)
        up = jnp.dot(x, wu_ref[...].astype(BF16), preferred_element_type=F32)
        hid = ((a * jax.nn.sigmoid(a)) * up).astype(BF16)
        y_ref[...] = jnp.dot(hid, wd_ref[...].astype(BF16), preferred_element_type=F32).astype(BF16)


def _moe_scatter_kernel(nloc_ref, gtile_ref,
                        col_ref, x1_ref, mod_ref, y_hbm,
                        o_ref,
                        y_ref, sem):
    blk = pl.program_id(0)
    nb = x1_ref.shape[0]
    tile, chunk = MOE_TILE, MOE_CHUNK
    slot = blk & 1

    def copies(b, s, wait):
        _moe_tile_copies(nloc_ref, gtile_ref, b, y_ref.at[s], y_hbm, sem.at[s], False, wait)

    @pl.when(blk == 0)
    def _():
        copies(0, 0, False)

    @pl.when(blk + 1 < pl.num_programs(0))
    def _():
        copies(blk + 1, 1 - slot, False)

    total = nloc_ref[blk]
    n_chunks = (total * tile + (chunk - 1)) // chunk
    max_chunks = y_ref.shape[1] // chunk
    usual = n_chunks <= MOE_USUAL_CHUNKS
    n_static = jnp.where(usual, MOE_USUAL_CHUNKS, max_chunks)

    def clear(t, carry):
        y_ref[slot, pl.ds(pl.multiple_of(t * tile, tile), tile), :] = jnp.zeros((tile, y_ref.shape[2]), BF16)
        return carry

    lax.fori_loop(total, n_static * (chunk // tile), clear, 0)

    p1 = col_ref[:, 0:1].astype(jnp.int32)
    p2 = col_ref[:, 1:2].astype(jnp.int32)
    cw1 = col_ref[:, 2:3]
    cw2 = col_ref[:, 3:4]
    gate = mod_ref[5:6, :]
    lane_c = lax.broadcasted_iota(jnp.int32, (nb, chunk), 1)
    copies(blk, slot, True)

    def scatter(n_unrolled):
        acc = None
        for c in range(n_unrolled):
            p = lane_c + c * chunk
            w = (jnp.where(p == p1, cw1, 0.0) + jnp.where(p == p2, cw2, 0.0)).astype(BF16)
            part = jnp.dot(w, y_ref[slot, c * chunk:(c + 1) * chunk, :], preferred_element_type=F32)
            acc = part if acc is None else acc + part
        o_ref[...] = x1_ref[...] + gate * acc

    lax.cond(usual, lambda: scatter(MOE_USUAL_CHUNKS), lambda: scatter(max_chunks))


def _moe_call(h2, comb, cnt_tiles, x1, mod, w_gate, w_up, w_down):
    bsz, seq, d = x1.shape
    nb, tile, ftm = MOE_TM, MOE_TILE, MOE_FFN_TM
    n_tok = bsz * seq
    n_blk = n_tok // nb
    region = ftm // tile
    rows_local = -(-(2 * nb + N_EXPERTS * tile) // MOE_CHUNK) * MOE_CHUNK
    tiles_global = (2 * n_tok) // tile + n_blk * N_EXPERTS + N_EXPERTS * (region - 1)
    n_ffn_max = -(-tiles_global // region)
    rows_global = n_ffn_max * ftm

    cnt = cnt_tiles[:, :, 0, :N_EXPERTS].reshape(n_blk, nb // POST_TM, N_EXPERTS).sum(axis=1).astype(jnp.int32)
    ntile = (cnt + (tile - 1)) // tile
    lfirst = jnp.cumsum(ntile, axis=1) - ntile
    tot = ntile.sum(axis=0)
    ptot = (tot + (region - 1)) // region * region
    ebase = jnp.cumsum(ptot) - ptot
    gfirst = ebase[None, :] + jnp.cumsum(ntile, axis=0) - ntile
    pad = ptot - tot
    n_ffn = (ptot.sum() // region).reshape(1)
    ends = jnp.cumsum(ptot) // region
    texp = jnp.minimum((jnp.arange(n_ffn_max, dtype=jnp.int32)[:, None] >= ends[None, :]).sum(axis=1),
                       N_EXPERTS - 1).astype(jnp.int32)
    nloc = ntile.sum(axis=1).astype(jnp.int32)
    lt = jnp.arange(rows_local // tile, dtype=jnp.int32)[None, :, None]
    in_seg = (lt >= lfirst[:, None, :]) & (lt < (lfirst + ntile)[:, None, :])
    gtile = (jnp.where(in_seg, (gfirst - lfirst)[:, None, :], 0).sum(axis=2) + lt[:, :, 0]).astype(jnp.int32)
    padstart = (gfirst[-1] + ntile[-1]).astype(jnp.int32)

    h2f = h2.reshape(n_tok, d)
    combf = comb.reshape(n_tok, LANES)
    col, xg = pl.pallas_call(
        _moe_gather_kernel,
        out_shape=(jax.ShapeDtypeStruct((n_tok, LANES), F32),
                   jax.ShapeDtypeStruct((rows_global, d), BF16)),
        grid_spec=pltpu.PrefetchScalarGridSpec(
            num_scalar_prefetch=4,
            grid=(n_blk,),
            in_specs=[pl.BlockSpec((nb, d), lambda j, *_: (j, 0)),
                      pl.BlockSpec((nb, LANES), lambda j, *_: (j, 0))],
            out_specs=(pl.BlockSpec((nb, LANES), lambda j, *_: (j, 0)),
                       pl.BlockSpec(memory_space=pl.ANY)),
            scratch_shapes=[
                pltpu.VMEM((rows_local, d), BF16),
                pltpu.VMEM((LANES, nb), F32),
                pltpu.VMEM((tile, d), BF16),
                pltpu.SemaphoreType.DMA,
            ]),
        compiler_params=pltpu.CompilerParams(dimension_semantics=("arbitrary",),
                                             vmem_limit_bytes=VMEM_LIMIT_BYTES),
        name="moe_gather",
    )(nloc, gtile, padstart, pad, h2f, combf)

    last = lambda t, te, nt: jnp.minimum(t, nt[0] - 1)
    y = pl.pallas_call(
        _moe_ffn_kernel,
        out_shape=jax.ShapeDtypeStruct((rows_global, d), BF16),
        grid_spec=pltpu.PrefetchScalarGridSpec(
            num_scalar_prefetch=2,
            grid=(n_ffn_max,),
            in_specs=[pl.BlockSpec((ftm, d), lambda t, te, nt: (last(t, te, nt), 0)),
                      pl.BlockSpec((None, d, EXPERT_FF), lambda t, te, nt: (te[last(t, te, nt)], 0, 0)),
                      pl.BlockSpec((None, d, EXPERT_FF), lambda t, te, nt: (te[last(t, te, nt)], 0, 0)),
                      pl.BlockSpec((None, EXPERT_FF, d), lambda t, te, nt: (te[last(t, te, nt)], 0, 0))],
            out_specs=pl.BlockSpec((ftm, d), lambda t, te, nt: (last(t, te, nt), 0))),
        compiler_params=pltpu.CompilerParams(dimension_semantics=("arbitrary",),
                                             vmem_limit_bytes=VMEM_LIMIT_BYTES),
        name="moe_ffn",
    )(texp, n_ffn, xg, w_gate, w_up, w_down)

    out = pl.pallas_call(
        _moe_scatter_kernel,
        out_shape=jax.ShapeDtypeStruct((n_tok, d), F32),
        grid_spec=pltpu.PrefetchScalarGridSpec(
            num_scalar_prefetch=2,
            grid=(n_blk,),
            in_specs=[pl.BlockSpec((nb, LANES), lambda j, *_: (j, 0)),
                      pl.BlockSpec((nb, d), lambda j, *_: (j, 0)),
                      pl.BlockSpec((None, 6, d), lambda j, *_: ((j * nb) // seq, 0, 0)),
                      pl.BlockSpec(memory_space=pl.ANY)],
            out_specs=pl.BlockSpec((nb, d), lambda j, *_: (j, 0)),
            scratch_shapes=[pltpu.VMEM((2, rows_local, d), BF16),
                            pltpu.SemaphoreType.DMA((2,))]),
        compiler_params=pltpu.CompilerParams(dimension_semantics=("arbitrary",),
                                             vmem_limit_bytes=VMEM_LIMIT_BYTES),
        name="moe_scatter",
    )(nloc, gtile, col, x1.reshape(n_tok, d), mod, y)
    return out.reshape(bsz, seq, d)


def _layer(x, mod, rel_bias, norm1, w_in, q_norm, k_norm, conv_w, attn_out_norm, conv_out_norm, w_out,
           norm2, w_group_router, b_group_router, w_expert_router, b_expert_router, w_gate, w_up, w_down):
    bsz, seq, d = x.shape
    aw = ATTN_WIDTH
    topk = min(TOPK_MAX, seq // 4)

    offs = np.cumsum([0, aw, aw, aw, IDX_HEADS * IDX_DIM, IDX_DIM, IDX_HEADS, CONV_WIDTH, CONV_WIDTH, CONV_WIDTH])
    col = lambda n: w_in[:, int(offs[n]):int(offs[n + 1])]
    wm = jnp.concatenate([col(0), col(1), col(3), col(6), col(7), col(8)], axis=1).astype(BF16)
    wvt = col(2).T.astype(BF16)
    wki = jnp.concatenate([col(4), col(4)], axis=1).astype(BF16)
    wwit = col(5).T.astype(BF16)
    qg = (jnp.tile(q_norm, ATTN_HEADS) * ((HEAD_DIM ** -0.5) * LOG2E))[None, :]
    kg = jnp.tile(k_norm, ATTN_HEADS)[None, :]
    grp = np.arange(aw) // CONV_GROUP_DIM
    gmat = jnp.asarray((grp[:, None] == grp[None, :]).astype(np.float32) / CONV_GROUP_DIM, dtype=BF16)

    q, k, vt, qi, ki, wit, cn = _pre_call(
        x, mod, norm1[None, :], wm, wvt, wki, wwit, qg, kg, conv_w, conv_out_norm.reshape(1, -1), gmat)

    bounds = jnp.asarray(_bucket_boundaries())
    an = _attn_call(rel_bias, bounds, q, qi, wit, k, ki, vt, attn_out_norm.reshape(1, -1), topk)

    wr = jnp.concatenate([w_expert_router, w_group_router,
                          jnp.zeros((d, LANES - N_EXPERTS - N_GROUPS), F32)], axis=1).astype(BF16)
    br = jnp.concatenate([b_expert_router, b_group_router,
                          jnp.zeros((LANES - N_EXPERTS - N_GROUPS,), F32)])[None, :]
    x1, h2, comb, cnt_tiles = _post_call(an, cn, x, mod, norm2[None, :], w_out[:aw].astype(BF16),
                                         w_out[aw:].astype(BF16), wr, br)

    return _moe_call(h2, comb, cnt_tiles, x1, mod, w_gate, w_up, w_down)


def kernel(x, c, rel_bias, w_ada, b_ada, norm1, w_in, q_norm, k_norm, conv_w, attn_out_norm, conv_out_norm,
           w_out, norm2, w_group_router, b_group_router, w_expert_router, b_expert_router, w_gate, w_up,
           w_down):
    bsz, seq, d = x.shape
    assert d == D_MODEL and seq % max(PRE_TM, POST_TM, MOE_TM) == 0 and ATT_TQ == ATT_TK
    depth = w_ada.shape[0]
    for l in range(depth):
        mod = _mod_call(c, w_ada[l], b_ada[l][None, :]).reshape(bsz, 6, d)
        x = _layer(x, mod, rel_bias, norm1[l], w_in[l], q_norm[l], k_norm[l], conv_w[l], attn_out_norm[l],
                   conv_out_norm[l], w_out[l], norm2[l], w_group_router[l], b_group_router[l],
                   w_expert_router[l], b_expert_router[l], w_gate[l], w_up[l], w_down[l])
    return x
```

```python
import functools
import math

import jax
import jax.numpy as jnp
import numpy as np
from jax import lax
from jax.experimental import pallas as pl
from jax.experimental.pallas import tpu as pltpu

F32 = jnp.float32
BF16 = jnp.bfloat16

D_MODEL = 1024
HEAD_DIM = 64
ATTN_HEADS = 8
ATTN_WIDTH = ATTN_HEADS * HEAD_DIM
CONV_WIDTH = D_MODEL - ATTN_WIDTH
CONV_GROUP_DIM = 64
CONV_K = 3
IDX_HEADS = 8
IDX_DIM = 64
TOPK_MAX = 256
IDX_SCALE = (IDX_DIM ** -0.5) * (IDX_HEADS ** -0.5)
N_BUCKETS = 32
MAX_DISTANCE = 128
N_GROUPS = 4
EXPERTS_PER_GROUP = 8
N_EXPERTS = N_GROUPS * EXPERTS_PER_GROUP
EXPERT_FF = 256
EPS = 1e-6
LOG2E = 1.4426950408889634
NEG_BIG = -1e30
COUNT_ACCS = 4
BISECT_GROUP = 2
BISECT_BF16_STEPS = 10
BISECT_VALUE_STEPS = 4
BISECT_MAX_STEPS = 64

LANES = 128
SUBLANES = 8
BF16_SUBLANES = 16
V_SLAB = HEAD_DIM + BF16_SUBLANES
VMEM_LIMIT_BYTES = 56 * 1024 * 1024

PRE_TM = 512
ATT_TQ = 256
ATT_TK = 256
POST_TM = 512
POST_SLABS = 2
MOE_TM = 512
MOE_TILE = 32
MOE_CHUNK = 512
MOE_USUAL_CHUNKS = 3
MOE_FFN_TM = 1024
MOD_TN = 1536

_NT_DIMS = (((1,), (1,)), ((), ()))


def _tree_sum(parts):
    while len(parts) > 1:
        nxt = [parts[j] + parts[j + 1] for j in range(0, len(parts) - 1, 2)]
        if len(parts) % 2:
            nxt.append(parts[-1])
        parts = nxt
    return parts[0]


def _bucket_boundaries():
    max_exact = N_BUCKETS // 2
    d = np.arange(0, 4 * MAX_DISTANCE, dtype=np.int64)
    nf = np.maximum(d, 1).astype(np.float32)
    large = max_exact + (np.log(nf / np.float32(max_exact)) / np.float32(math.log(MAX_DISTANCE / max_exact))
                         * np.float32(N_BUCKETS - max_exact)).astype(np.int32)
    large = np.minimum(large, N_BUCKETS - 1)
    bucket = np.where(d < max_exact, d, large)
    assert np.all(np.diff(bucket) >= 0) and bucket[-1] == N_BUCKETS - 1
    bounds = [int(np.argmax(bucket >= j)) for j in range(1, N_BUCKETS)]
    return np.asarray([0] + bounds, dtype=np.int32)


def _mod_kernel(c_ref, w_ref, b_ref, o_ref):
    c = c_ref[...]
    act = c * jax.nn.sigmoid(c)
    o_ref[...] = jnp.dot(act, w_ref[...], preferred_element_type=F32,
                         precision=lax.Precision.HIGHEST) + b_ref[...]


def _mod_call(c, w_ada, b_ada):
    bsz, d = c.shape
    n = w_ada.shape[1]
    return pl.pallas_call(
        _mod_kernel,
        out_shape=jax.ShapeDtypeStruct((bsz, n), F32),
        grid=(n // MOD_TN,),
        in_specs=[pl.BlockSpec((bsz, d), lambda j: (0, 0)),
                  pl.BlockSpec((d, MOD_TN), lambda j: (0, j)),
                  pl.BlockSpec((1, MOD_TN), lambda j: (0, j))],
        out_specs=pl.BlockSpec((bsz, MOD_TN), lambda j: (0, j)),
        compiler_params=pltpu.CompilerParams(dimension_semantics=("arbitrary",),
                                             vmem_limit_bytes=VMEM_LIMIT_BYTES),
        name="adaln_mod",
    )(c, w_ada, b_ada)


def _group_rms(y, g_ref):
    ms = jnp.dot((y * y).astype(BF16), g_ref[...], preferred_element_type=F32)
    return y * lax.rsqrt(ms + EPS)


def _pre_kernel(x_ref, mod_ref, n1_ref, wm_ref, wvt_ref, wki_ref, wwit_ref, qg_ref, kg_ref,
                cw_ref, cg_ref, g_ref,
                q_ref, k_ref, vt_ref, qi_ref, ki_ref, wit_ref, cn_ref, carry_ref):
    j = pl.program_id(1)
    tm = x_ref.shape[0]
    aw = ATTN_WIDTH

    x = x_ref[...]
    ms = jnp.mean(x * x, axis=-1, keepdims=True)
    y = x * lax.rsqrt(ms + EPS) * n1_ref[...]
    h = y * (1.0 + mod_ref[1:2, :]) + mod_ref[0:1, :]
    hb = h.astype(BF16)

    def proj(lo):
        return jnp.dot(hb, wm_ref[:, lo:lo + aw], preferred_element_type=F32)

    q = _group_rms(proj(0), g_ref) * qg_ref[...]
    q_ref[...] = q.astype(BF16)
    k = _group_rms(proj(aw), g_ref) * kg_ref[...]
    k_ref[...] = k.astype(BF16)

    vt = lax.dot_general(wvt_ref[...], hb, _NT_DIMS, preferred_element_type=F32).astype(BF16)
    ones = jnp.ones((BF16_SUBLANES, ATT_TK), BF16)
    for cc in range(tm // ATT_TK):
        for hh in range(ATTN_HEADS):
            vt_ref[cc, hh * V_SLAB:hh * V_SLAB + HEAD_DIM, :] = (
                vt[hh * HEAD_DIM:(hh + 1) * HEAD_DIM, cc * ATT_TK:(cc + 1) * ATT_TK])
            vt_ref[cc, hh * V_SLAB + HEAD_DIM:(hh + 1) * V_SLAB, :] = ones

    qi_ref[...] = proj(2 * aw).astype(BF16)
    ki_ref[...] = jnp.dot(hb, wki_ref[...], preferred_element_type=F32).astype(BF16)
    wit_ref[...] = lax.dot_general(wwit_ref[...], hb, _NT_DIMS, preferred_element_type=F32) * IDX_SCALE

    assert CONV_K == 3 and cw_ref.shape[0] == CONV_K
    gate_b = proj(3 * aw)
    z = proj(4 * aw) * proj(5 * aw)

    @pl.when(j == 0)
    def _():
        carry_ref[...] = jnp.zeros_like(carry_ref)

    prev = carry_ref[...]
    row = lax.broadcasted_iota(jnp.int32, z.shape, 0)
    z1 = jnp.where(row == 0, prev[SUBLANES - 1:SUBLANES, :], pltpu.roll(z, 1, 0))
    z2 = pltpu.roll(z, 2, 0)
    z2 = jnp.where(row == 0, prev[SUBLANES - 2:SUBLANES - 1, :], z2)
    z2 = jnp.where(row == 1, prev[SUBLANES - 1:SUBLANES, :], z2)
    carry_ref[...] = z[tm - SUBLANES:, :]
    conv = cw_ref[2:3, :] * z + cw_ref[1:2, :] * z1 + cw_ref[0:1, :] * z2
    yc = gate_b * conv
    cn_ref[...] = (_group_rms(yc, g_ref) * cg_ref[...]).astype(BF16)


def _pre_call(x, mod, n1, wm, wvt, wki, wwit, qg, kg, cw, cg, gmat):
    bsz, seq, d = x.shape
    tm = PRE_TM
    nck = tm // ATT_TK
    aw = ATTN_WIDTH
    const = lambda b, j: (0, 0)
    tok = lambda b, j: (b, j, 0)
    out_shape = (
        jax.ShapeDtypeStruct((bsz, seq, aw), BF16),
        jax.ShapeDtypeStruct((bsz, seq, aw), BF16),
        jax.ShapeDtypeStruct((bsz, seq // ATT_TK, ATTN_HEADS * V_SLAB, ATT_TK), BF16),
        jax.ShapeDtypeStruct((bsz, seq, aw), BF16),
        jax.ShapeDtypeStruct((bsz, seq, LANES), BF16),
        jax.ShapeDtypeStruct((bsz, IDX_HEADS, seq), F32),
        jax.ShapeDtypeStruct((bsz, seq, CONV_WIDTH), BF16),
    )
    out_specs = (
        pl.BlockSpec((None, tm, aw), tok),
        pl.BlockSpec((None, tm, aw), tok),
        pl.BlockSpec((None, nck, ATTN_HEADS * V_SLAB, ATT_TK), lambda b, j: (b, j, 0, 0)),
        pl.BlockSpec((None, tm, aw), tok),
        pl.BlockSpec((None, tm, LANES), tok),
        pl.BlockSpec((None, IDX_HEADS, tm), lambda b, j: (b, 0, j)),
        pl.BlockSpec((None, tm, CONV_WIDTH), tok),
    )
    in_specs = [
        pl.BlockSpec((None, tm, d), tok),
        pl.BlockSpec((None, 6, d), lambda b, j: (b, 0, 0)),
        pl.BlockSpec(n1.shape, const),
        pl.BlockSpec(wm.shape, const),
        pl.BlockSpec(wvt.shape, const),
        pl.BlockSpec(wki.shape, const),
        pl.BlockSpec(wwit.shape, const),
        pl.BlockSpec(qg.shape, const),
        pl.BlockSpec(kg.shape, const),
        pl.BlockSpec(cw.shape, const),
        pl.BlockSpec(cg.shape, const),
        pl.BlockSpec(gmat.shape, const),
    ]
    return pl.pallas_call(
        _pre_kernel,
        out_shape=out_shape,
        grid=(bsz, seq // tm),
        in_specs=in_specs,
        out_specs=out_specs,
        scratch_shapes=[pltpu.VMEM((SUBLANES, CONV_WIDTH), F32)],
        compiler_params=pltpu.CompilerParams(dimension_semantics=("arbitrary", "arbitrary"),
                                             vmem_limit_bytes=VMEM_LIMIT_BYTES),
        name="pre_proj",
    )(x, mod, n1, wm, wvt, wki, wwit, qg, kg, cw, cg, gmat)


def _attn_kernel(rb_ref, bnd_ref, q_ref, qi_ref, wit_ref, k_ref, ki_ref, vt_ref, og_ref,
                 o_ref,
                 s_ref, s16_ref, bias_ref, qpad_ref, qipad_ref, lg_ref, acc_ref, out_ref, *, topk):
    b = pl.program_id(0)
    i = pl.program_id(1)
    tq, tk = ATT_TQ, ATT_TK
    nh, hd = ATTN_HEADS, HEAD_DIM

    t_loc = lax.broadcasted_iota(jnp.int32, (tk, tq), 1)
    s_loc = lax.broadcasted_iota(jnp.int32, (tk, tq), 0)

    @pl.when((b == 0) & (i == 0))
    def _():
        for idx in range(2):
            dist = t_loc - s_loc + idx * tq
            for h in range(nh):
                bias_ref[idx, h] = jnp.full((tk, tq), (rb_ref[0, h] - rb_ref[N_BUCKETS - 1, h]) * LOG2E, F32)

            def fill(jb, carry):
                reached = dist >= bnd_ref[jb]
                for h in range(nh):
                    val = (rb_ref[jb, h] - rb_ref[N_BUCKETS - 1, h]) * LOG2E
                    bias_ref[idx, h] = jnp.where(reached, val, bias_ref[idx, h])
                return carry

            lax.fori_loop(1, N_BUCKETS, fill, 0)

    lane = lax.broadcasted_iota(jnp.int32, (tq, LANES), 1)
    for h in range(nh):
        pair = slice((h // 2) * LANES, (h // 2 + 1) * LANES)
        keep = (lane // hd) == (h % 2)
        qpad_ref[h] = jnp.where(keep, q_ref[:, pair], jnp.zeros((), BF16))
        qipad_ref[h] = jnp.where(keep, qi_ref[:, pair], jnp.zeros((), BF16))

    def idx_dots(c, slot):
        kic = ki_ref[pl.ds(pl.multiple_of(c * tk, tk), tk), :]
        for h in range(nh):
            lg_ref[slot, h] = lax.dot_general(kic, qipad_ref[h], _NT_DIMS, preferred_element_type=F32)

    def idx_reduce(c, slot, carry, diagonal):
        rmin, rmax = carry
        sc = _tree_sum([wit_ref[h:h + 1, :] * jnp.maximum(lg_ref[slot, h], 0.0) for h in range(nh)])
        if diagonal:
            causal = s_loc <= t_loc
            lo_c, hi_c = jnp.where(causal, sc, jnp.inf), jnp.where(causal, sc, -jnp.inf)
            sc = hi_c
        else:
            lo_c, hi_c = sc, sc
        s_ref[c] = sc
        s16_ref[c] = sc.astype(BF16)
        return (jnp.minimum(rmin, jnp.min(lo_c, axis=0, keepdims=True)),
                jnp.maximum(rmax, jnp.max(hi_c, axis=0, keepdims=True)))

    def idx_pair(jj, carry):
        idx_dots(2 * jj + 1, 1)
        carry = idx_reduce(2 * jj, 0, carry, False)
        idx_dots(2 * jj + 2, 0)
        return idx_reduce(2 * jj + 1, 1, carry, False)

    def idx_tail_odd(carry):
        idx_dots(i, 1)
        return idx_reduce(i, 1, idx_reduce(i - 1, 0, carry, False), True)

    idx_dots(0, 0)
    carry = (jnp.full((1, tq), jnp.inf, F32), jnp.full((1, tq), -jnp.inf, F32))
    carry = lax.fori_loop(0, i // 2, idx_pair, carry)
    rmin, rmax = lax.cond((i & 1) == 1, idx_tail_odd, lambda cr: idx_reduce(i, 0, cr, True), carry)

    def count_ge(thr):
        def body(c, accs):
            hit = s_ref[c] >= thr
            accs = list(accs)
            for r in range(tk // SUBLANES):
                a = accs[r % COUNT_ACCS]
                accs[r % COUNT_ACCS] = jnp.where(hit[r * SUBLANES:(r + 1) * SUBLANES], a + 1.0, a)
            return tuple(accs)
        accs = lax.fori_loop(0, i + 1, body,
                             tuple(jnp.zeros((SUBLANES, tq), F32) for _ in range(COUNT_ACCS)))
        return jnp.sum(_tree_sum(list(accs)), axis=0, keepdims=True)

    def count16_ge(thr16):
        def body(c, accs):
            hit = s16_ref[c] >= thr16
            accs = list(accs)
            for r in range(tk // BF16_SUBLANES):
                a = accs[r % COUNT_ACCS]
                accs[r % COUNT_ACCS] = jnp.where(hit[r * BF16_SUBLANES:(r + 1) * BF16_SUBLANES], a + 1, a)
            return tuple(accs)
        accs = lax.fori_loop(0, i + 1, body,
                             tuple(jnp.zeros((BF16_SUBLANES, tq), BF16) for _ in range(COUNT_ACCS)))
        return jnp.sum(_tree_sum(list(accs)).astype(F32), axis=0, keepdims=True)

    int_min = jnp.int32(-2 ** 31)

    def order_key(v):
        bits = pltpu.bitcast(v, jnp.int32)
        return jnp.where(bits < 0, -(bits & jnp.int32(0x7FFFFFFF)), bits)

    def from_order_key(key):
        return pltpu.bitcast(jnp.where(key < 0, (-key) | int_min, key), F32)

    t_glob = (i * tq + lax.broadcasted_iota(jnp.int32, (1, tq), 1)).astype(F32)
    n_causal = t_glob + 1.0
    kf = jnp.minimum(float(topk), n_causal)
    all_sel = n_causal <= kf

    lo16 = order_key(rmin) >> 16
    hi16 = (order_key(rmax.astype(BF16).astype(F32) + 0.0) >> 16) + 1
    for _ in range(BISECT_BF16_STEPS):
        lo_v = from_order_key(lo16 << 16)
        hi_v = from_order_key(hi16 << 16)
        mid_val16 = order_key(lo_v + (hi_v - lo_v) * 0.5) >> 16
        mid16 = jnp.where((mid_val16 > lo16) & (mid_val16 < hi16), mid_val16, (lo16 + hi16) >> 1)
        open_ = (hi16 - lo16) > 1
        cm = count16_ge(from_order_key(mid16 << 16).astype(BF16))
        lo16 = jnp.where(open_ & (cm >= kf), mid16, lo16)
        hi16 = jnp.where(open_ & (cm < kf), mid16, hi16)

    min_normal_key = jnp.int32(0x00800000)

    def snap(key, direction):
        sub = (key > -min_normal_key) & (key < min_normal_key) & (key != 0)
        below = jnp.where(key > 0, 0, -min_normal_key)
        above = jnp.where(key > 0, min_normal_key, 0)
        return jnp.where(sub, {"down": below, "up": above, "zero": jnp.zeros_like(key)}[direction], key)

    lo0 = snap(jnp.maximum((lo16 - 1) << 16, order_key(rmin)), "down")
    hi0 = snap(hi16 << 16, "up")
    state0 = (jnp.where(all_sel, 0.0, 1.0), lo0, hi0, count_ge(from_order_key(hi0)),
              order_key(rmin), jnp.zeros((1, tq), F32), jnp.full((1, tq), 0x7F800000, jnp.int32), kf)

    def bisect_step(st, value_mid, force_end):
        active, lo_key, hi_key, fhi, thr_key, tie, hif_key, need = st
        mid_key = snap((lo_key >> 1) + (hi_key >> 1) + (lo_key & hi_key & 1), "zero")
        if value_mid:
            lo = from_order_key(lo_key)
            hi = from_order_key(hi_key)
            val_key = snap(order_key(lo + (hi - lo) * 0.5), "zero")
            mid_key = jnp.where((val_key > lo_key) & (val_key < hi_key), val_key, mid_key)
        collapsed = (mid_key <= lo_key) | (mid_key >= hi_key) | force_end
        cm = count_ge(from_order_key(mid_key))
        act = active > 0.0
        live = act & jnp.logical_not(collapsed)
        found = live & (cm == kf)
        go_up = live & (cm > kf)
        go_dn = live & (cm < kf)
        ends_tie = act & collapsed
        thr_key = jnp.where(found, mid_key, jnp.where(ends_tie, lo_key, thr_key))
        tie = jnp.where(ends_tie, 1.0, tie)
        hif_key = jnp.where(ends_tie, hi_key, hif_key)
        need = jnp.where(ends_tie, kf - fhi, need)
        lo_key = jnp.where(go_up, mid_key, lo_key)
        fhi = jnp.where(go_dn, cm, fhi)
        hi_key = jnp.where(go_dn, mid_key, hi_key)
        active = jnp.where(found | ends_tie, 0.0, active)
        return active, lo_key, hi_key, fhi, thr_key, tie, hif_key, need

    state = state0
    for _ in range(BISECT_VALUE_STEPS):
        state = bisect_step(state, True, False)

    def b_cond(carry):
        st, step = carry
        return (jnp.max(st[0]) > 0.0) & (step <= BISECT_MAX_STEPS)

    def b_group(carry):
        st, step = carry
        for _ in range(BISECT_GROUP):
            st = bisect_step(st, False, step >= BISECT_MAX_STEPS)
        return st, step + BISECT_GROUP

    (_, _, _, _, thr_key, tie, hif_key, need), _ = lax.while_loop(b_cond, b_group, (state, jnp.int32(0)))
    thr = from_order_key(thr_key)
    hif = from_order_key(hif_key)

    @pl.when(jnp.max(tie) > 0.0)
    def _():
        tri = jnp.where(lax.broadcasted_iota(jnp.int32, (tk, tk), 1)
                        <= lax.broadcasted_iota(jnp.int32, (tk, tk), 0), 1.0, 0.0).astype(BF16)

        def body(c, seen):
            sc_c = s_ref[c]
            tied = (sc_c >= thr) & (sc_c < hif) & (tie > 0.0)
            rank = jnp.dot(tri, jnp.where(tied, 1.0, 0.0).astype(BF16), preferred_element_type=F32) + seen
            s_ref[c] = jnp.where(tied & (rank > need), -jnp.inf, sc_c)
            return rank[tk - 1:tk, :]

        lax.fori_loop(0, i + 1, body, jnp.zeros((1, tq), F32))

    acc_ref[...] = jnp.zeros(acc_ref.shape, F32)

    def store_logits(c, slot, bias_idx):
        masked = jnp.where(s_ref[c] >= thr, 0.0, NEG_BIG)
        row0 = pl.multiple_of(c * tk, tk)
        for h in range(nh):
            kc = k_ref[pl.ds(row0, tk), (h // 2) * LANES:(h // 2 + 1) * LANES]
            lt = lax.dot_general(kc, qpad_ref[h], _NT_DIMS, preferred_element_type=F32) + masked
            if bias_idx is not None:
                lt = lt + bias_ref[bias_idx, h]
            lg_ref[slot, h] = lt

    def softmax_pv(c, slot, m_all):
        m_out = []
        for h in range(nh):
            m_old = m_all[h]
            m_new = jnp.maximum(m_old, jnp.max(lg_ref[slot, h], axis=0, keepdims=True))
            p = jnp.exp2(lg_ref[slot, h] - m_new).astype(BF16)
            alpha = jnp.exp2(m_old - m_new)
            pv = jnp.dot(vt_ref[c, h * V_SLAB:(h + 1) * V_SLAB, :], p, preferred_element_type=F32)
            acc_ref[h] = alpha * acc_ref[h] + pv
            m_out.append(m_new)
        return tuple(m_out)

    def near_step(m_all):
        store_logits(i - 1, 1, 1)
        return softmax_pv(i, 0, m_all)

    def far_step(j, parity, m_all):
        c = i - 2 - j
        store_logits(c, parity, None)
        return softmax_pv(c + 1, 1 - parity, m_all)

    def far_pair(jj, m_all):
        return far_step(2 * jj + 1, 1, far_step(2 * jj, 0, m_all))

    n_far = jnp.maximum(i - 1, 0)
    m_all = tuple(jnp.full((1, tq), NEG_BIG, F32) for _ in range(nh))
    store_logits(i, 0, 0)
    m_all = lax.cond(i >= 1, near_step, lambda m: m, m_all)
    m_all = lax.fori_loop(0, n_far // 2, far_pair, m_all)
    m_all = lax.cond((n_far & 1) == 1, lambda m: far_step(n_far - 1, 0, m), lambda m: m, m_all)
    lax.cond((i & 1) == 0, lambda m: softmax_pv(0, 0, m), lambda m: softmax_pv(0, 1, m), m_all)

    for h in range(nh):
        o = acc_ref[h, :hd, :] / acc_ref[h, hd:hd + 1, :]
        ms = jnp.mean(o * o, axis=0, keepdims=True)
        out_ref[h * hd:(h + 1) * hd, :] = o * lax.rsqrt(ms + EPS)
    o_ref[...] = (out_ref[...].T * og_ref[...]).astype(BF16)


def _attn_call(rel_bias, bounds, q, qi, wit, k, ki, vt, og, topk):
    bsz, seq, aw = q.shape
    tq, tk = ATT_TQ, ATT_TK
    nck = seq // tk
    blk_q = lambda b, i: (b, i, 0)
    whole = lambda b, i: (b, 0, 0)
    smem = pl.BlockSpec(memory_space=pltpu.SMEM)
    return pl.pallas_call(
        functools.partial(_attn_kernel, topk=topk),
        out_shape=jax.ShapeDtypeStruct((bsz, seq, aw), BF16),
        grid=(bsz, seq // tq),
        in_specs=[
            smem, smem,
            pl.BlockSpec((None, tq, aw), blk_q),
            pl.BlockSpec((None, tq, aw), blk_q),
            pl.BlockSpec((None, IDX_HEADS, tq), lambda b, i: (b, 0, i)),
            pl.BlockSpec((None, seq, aw), whole),
            pl.BlockSpec((None, seq, LANES), whole),
            pl.BlockSpec((None, nck, ATTN_HEADS * V_SLAB, tk), lambda b, i: (b, 0, 0, 0)),
            pl.BlockSpec(og.shape, lambda b, i: (0, 0)),
        ],
        out_specs=pl.BlockSpec((None, tq, aw), blk_q),
        scratch_shapes=[
            pltpu.VMEM((nck, tk, tq), F32),
            pltpu.VMEM((nck, tk, tq), BF16),
            pltpu.VMEM((2, ATTN_HEADS, tk, tq), F32),
            pltpu.VMEM((ATTN_HEADS, tq, LANES), BF16),
            pltpu.VMEM((IDX_HEADS, tq, LANES), BF16),
            pltpu.VMEM((2, ATTN_HEADS, tk, tq), F32),
            pltpu.VMEM((ATTN_HEADS, V_SLAB, tq), F32),
            pltpu.VMEM((aw, tq), F32),
        ],
        compiler_params=pltpu.CompilerParams(dimension_semantics=("arbitrary", "arbitrary"),
                                             vmem_limit_bytes=VMEM_LIMIT_BYTES),
        name="dsa_attention",
    )(rel_bias, bounds, q, qi, wit, k, ki, vt, og)


def _post_kernel(an_ref, cn_ref, x_ref, mod_ref, n2_ref, woa_ref, woc_ref, wr_ref, br_ref,
                 x1_ref, h2_ref, comb_ref, cnt_ref):
    slab = x_ref.shape[0] // POST_SLABS
    cnt = jnp.zeros((1, LANES), F32)
    for s in range(POST_SLABS):
        cnt = cnt + _post_slab(slice(s * slab, (s + 1) * slab), an_ref, cn_ref, x_ref, mod_ref, n2_ref,
                               woa_ref, woc_ref, wr_ref, br_ref, x1_ref, h2_ref, comb_ref)
    cnt_ref[...] = jnp.broadcast_to(cnt, cnt_ref.shape)


def _post_slab(rows, an_ref, cn_ref, x_ref, mod_ref, n2_ref, woa_ref, woc_ref, wr_ref, br_ref,
               x1_ref, h2_ref, comb_ref):
    mix = (jnp.dot(an_ref[rows, :], woa_ref[...], preferred_element_type=F32)
           + jnp.dot(cn_ref[rows, :], woc_ref[...], preferred_element_type=F32))
    x1 = x_ref[rows, :] + mod_ref[2:3, :] * mix
    x1_ref[rows, :] = x1
    ms = jnp.mean(x1 * x1, axis=-1, keepdims=True)
    h2 = x1 * lax.rsqrt(ms + EPS) * n2_ref[...] * (1.0 + mod_ref[4:5, :]) + mod_ref[3:4, :]
    h2b = h2.astype(BF16)
    h2_ref[rows, :] = h2b

    logits = jnp.dot(h2b, wr_ref[...], preferred_element_type=F32) + br_ref[...]
    lane = lax.broadcasted_iota(jnp.int32, logits.shape, 1)
    lane_f = lane.astype(F32)
    far = float(LANES)
    is_g = (lane >= N_EXPERTS) & (lane < N_EXPERTS + N_GROUPS)
    gl = jnp.where(is_g, logits, -jnp.inf)
    gmax = jnp.max(gl, axis=-1, keepdims=True)
    g_sel = jnp.min(jnp.where(is_g & (gl == gmax), lane_f, far), axis=-1, keepdims=True) - float(N_EXPERTS)
    p_g = 1.0 / jnp.sum(jnp.exp(gl - gmax), axis=-1, keepdims=True)

    in_grp = (lane < N_EXPERTS) & ((lane // EXPERTS_PER_GROUP).astype(F32) == g_sel)
    e1 = jnp.where(in_grp, logits, -jnp.inf)
    l1 = jnp.max(e1, axis=-1, keepdims=True)
    i1 = jnp.min(jnp.where(in_grp & (e1 == l1), lane_f, far), axis=-1, keepdims=True)
    rest = in_grp & (lane_f != i1)
    e2 = jnp.where(rest, logits, -jnp.inf)
    l2 = jnp.max(e2, axis=-1, keepdims=True)
    i2 = jnp.min(jnp.where(rest & (e2 == l2), lane_f, far), axis=-1, keepdims=True)
    r = jnp.exp(l2 - l1)
    w1 = 1.0 / (1.0 + r)
    w2 = r / (1.0 + r)
    comb = jnp.where(lane_f == i1, p_g * w1, 0.0) + jnp.where(lane_f == i2, p_g * w2, 0.0)
    comb_ref[rows, :] = comb
    return jnp.sum(jnp.where(comb != 0.0, 1.0, 0.0), axis=0, keepdims=True)


def _post_call(an, cn, x, mod, n2, woa, woc, wr, br):
    bsz, seq, d = x.shape
    tm = POST_TM
    tok = lambda b, j: (b, j, 0)
    const = lambda b, j: (0, 0)
    return pl.pallas_call(
        _post_kernel,
        out_shape=(jax.ShapeDtypeStruct((bsz, seq, d), F32),
                   jax.ShapeDtypeStruct((bsz, seq, d), BF16),
                   jax.ShapeDtypeStruct((bsz, seq, LANES), F32),
                   jax.ShapeDtypeStruct((bsz, seq // tm, SUBLANES, LANES), F32)),
        grid=(bsz, seq // tm),
        in_specs=[
            pl.BlockSpec((None, tm, ATTN_WIDTH), tok),
            pl.BlockSpec((None, tm, CONV_WIDTH), tok),
            pl.BlockSpec((None, tm, d), tok),
            pl.BlockSpec((None, 6, d), lambda b, j: (b, 0, 0)),
            pl.BlockSpec(n2.shape, const),
            pl.BlockSpec(woa.shape, const),
            pl.BlockSpec(woc.shape, const),
            pl.BlockSpec(wr.shape, const),
            pl.BlockSpec(br.shape, const),
        ],
        out_specs=(pl.BlockSpec((None, tm, d), tok),
                   pl.BlockSpec((None, tm, d), tok),
                   pl.BlockSpec((None, tm, LANES), tok),
                   pl.BlockSpec((None, None, SUBLANES, LANES), lambda b, j: (b, j, 0, 0))),
        compiler_params=pltpu.CompilerParams(dimension_semantics=("arbitrary", "arbitrary"),
                                             vmem_limit_bytes=VMEM_LIMIT_BYTES),
        name="post_router",
    )(an, cn, x, mod, n2, woa, woc, wr, br)


def _strict_tri(n, lower):
    r = lax.broadcasted_iota(jnp.int32, (n, n), 0)
    c = lax.broadcasted_iota(jnp.int32, (n, n), 1)
    return jnp.where((c < r) if lower else (r < c), 1.0, 0.0).astype(BF16)


def _moe_tile_copies(nloc_ref, gtile_ref, blk, local_ref, global_ref, sem, to_global, wait):
    tile = MOE_TILE

    def per_tile(lt, c):
        loc = local_ref.at[pl.ds(pl.multiple_of(lt * tile, tile), tile), :]
        glo = global_ref.at[pl.ds(pl.multiple_of(gtile_ref[blk, lt] * tile, tile), tile), :]
        cp = pltpu.make_async_copy(loc, glo, sem) if to_global else pltpu.make_async_copy(glo, loc, sem)
        if wait:
            cp.wait()
        else:
            cp.start()
        return c

    lax.fori_loop(0, nloc_ref[blk], per_tile, 0)


def _moe_gather_kernel(nloc_ref, gtile_ref, padstart_ref, pad_ref,
                       h2_ref, comb_ref,
                       col_ref, xg_hbm,
                       xg_ref, row_ref, zero_ref, sem):
    blk = pl.program_id(0)
    nb = h2_ref.shape[0]
    tile, chunk = MOE_TILE, MOE_CHUNK
    lane = lax.broadcasted_iota(jnp.int32, (nb, LANES), 1)

    comb = comb_ref[...]
    assigned = comb != 0.0
    a_f = jnp.where(assigned, 1.0, 0.0)
    rank = jnp.dot(_strict_tri(nb, True), a_f.astype(BF16), preferred_element_type=F32)
    cnt = rank[nb - 1:nb, :] + a_f[nb - 1:nb, :]
    ntile = jnp.floor((cnt + float(tile - 1)) * (1.0 / tile))
    first = jnp.dot(jnp.broadcast_to(ntile, (SUBLANES, LANES)).astype(BF16), _strict_tri(LANES, False),
                    preferred_element_type=F32)[0:1, :]
    pos = first * float(tile) + rank
    pos1 = jnp.min(jnp.where(assigned, pos, 1e9), axis=1, keepdims=True)
    pos2 = jnp.max(jnp.where(assigned, pos, -1.0), axis=1, keepdims=True)
    pos2 = jnp.where(pos2 == pos1, -1.0, pos2)
    cw1 = jnp.sum(jnp.where(assigned & (pos == pos1), comb, 0.0), axis=1, keepdims=True)
    cw2 = jnp.sum(jnp.where(assigned & (pos == pos2), comb, 0.0), axis=1, keepdims=True)
    info = jnp.where(lane == 0, pos1, jnp.where(lane == 1, pos2, jnp.where(lane == 2, cw1,
                     jnp.where(lane == 3, cw2, 0.0))))
    col_ref[...] = info
    row_ref[...] = info.T

    n_chunks = (nloc_ref[blk] * tile + (chunk - 1)) // chunk
    p1 = row_ref[0:1, :].astype(jnp.int32)
    p2 = row_ref[1:2, :].astype(jnp.int32)
    sub = lax.broadcasted_iota(jnp.int32, (chunk, nb), 0)

    def gather(c, carry):
        p = sub + c * chunk
        sel = jnp.where((p == p1) | (p == p2), 1.0, 0.0).astype(BF16)
        r0 = pl.multiple_of(c * chunk, chunk)
        xg_ref[pl.ds(r0, chunk), :] = jnp.dot(sel, h2_ref[...], preferred_element_type=F32).astype(BF16)
        return carry

    @pl.when(blk > 0)
    def _():
        _moe_tile_copies(nloc_ref, gtile_ref, blk - 1, xg_ref, xg_hbm, sem, True, True)

    lax.fori_loop(0, n_chunks, gather, 0)

    _moe_tile_copies(nloc_ref, gtile_ref, blk, xg_ref, xg_hbm, sem, True, False)

    is_last = blk == pl.num_programs(0) - 1

    def pad_copies(wait):
        def per_expert(x, carry):
            g0 = padstart_ref[x]

            def per_tile(j, c):
                dst = xg_hbm.at[pl.ds(pl.multiple_of((g0 + j) * tile, tile), tile), :]
                cp = pltpu.make_async_copy(zero_ref, dst, sem)
                if wait:
                    cp.wait()
                else:
                    cp.start()
                return c

            lax.fori_loop(0, pad_ref[x], per_tile, 0)
            return carry

        lax.fori_loop(0, N_EXPERTS, per_expert, 0)

    @pl.when(is_last)
    def _():
        zero_ref[...] = jnp.zeros(zero_ref.shape, BF16)
        pad_copies(False)
        _moe_tile_copies(nloc_ref, gtile_ref, blk, xg_ref, xg_hbm, sem, True, True)
        pad_copies(True)


def _moe_ffn_kernel(texp_ref, nt_ref, x_ref, wg_ref, wu_ref, wd_ref, y_ref, wg16_ref, wu16_ref, wd16_ref):
    t = pl.program_id(0)

    @pl.when(t < nt_ref[0])
    def _():
        @pl.when((t == 0) | (texp_ref[t] != texp_ref[jnp.maximum(t - 1, 0)]))
        def _():
            wg16_ref[...] = wg_ref[...].astype(BF16)
            wu16_ref[...] = wu_ref[...].astype(BF16)
            wd16_ref[...] = wd_ref[...].astype(BF16)

        x = x_ref[...]
        a = jnp.dot(x, wg16_ref[...], preferred_element_type=F32)
        up = jnp.dot(x, wu16_ref[...], preferred_element_type=F32)
        hid = ((a * jax.nn.sigmoid(a)) * up).astype(BF16)
        y_ref[...] = jnp.dot(hid, wd16_ref[...], preferred_element_type=F32).astype(BF16)


def _moe_scatter_kernel(nloc_ref, gtile_ref,
                        col_ref, x1_ref, mod_ref, y_hbm,
                        o_ref,
                        y_ref, sem):
    blk = pl.program_id(0)
    nb = x1_ref.shape[0]
    tile, chunk = MOE_TILE, MOE_CHUNK
    slot = blk & 1

    def copies(b, s, wait):
        _moe_tile_copies(nloc_ref, gtile_ref, b, y_ref.at[s], y_hbm, sem.at[s], False, wait)

    @pl.when(blk == 0)
    def _():
        copies(0, 0, False)

    @pl.when(blk + 1 < pl.num_programs(0))
    def _():
        copies(blk + 1, 1 - slot, False)

    total = nloc_ref[blk]
    n_chunks = (total * tile + (chunk - 1)) // chunk
    max_chunks = y_ref.shape[1] // chunk
    usual = n_chunks <= MOE_USUAL_CHUNKS
    n_static = jnp.where(usual, MOE_USUAL_CHUNKS, max_chunks)

    def clear(t, carry):
        y_ref[slot, pl.ds(pl.multiple_of(t * tile, tile), tile), :] = jnp.zeros((tile, y_ref.shape[2]), BF16)
        return carry

    lax.fori_loop(total, n_static * (chunk // tile), clear, 0)

    p1 = col_ref[:, 0:1].astype(jnp.int32)
    p2 = col_ref[:, 1:2].astype(jnp.int32)
    cw1 = col_ref[:, 2:3]
    cw2 = col_ref[:, 3:4]
    gate = mod_ref[5:6, :]
    lane_c = lax.broadcasted_iota(jnp.int32, (nb, chunk), 1)
    copies(blk, slot, True)

    def scatter(n_unrolled):
        acc = None
        for c in range(n_unrolled):
            p = lane_c + c * chunk
            w = (jnp.where(p == p1, cw1, 0.0) + jnp.where(p == p2, cw2, 0.0)).astype(BF16)
            part = jnp.dot(w, y_ref[slot, c * chunk:(c + 1) * chunk, :], preferred_element_type=F32)
            acc = part if acc is None else acc + part
        o_ref[...] = x1_ref[...] + gate * acc

    lax.cond(usual, lambda: scatter(MOE_USUAL_CHUNKS), lambda: scatter(max_chunks))


def _moe_call(h2, comb, cnt_tiles, x1, mod, w_gate, w_up, w_down):
    bsz, seq, d = x1.shape
    nb, tile, ftm = MOE_TM, MOE_TILE, MOE_FFN_TM
    n_tok = bsz * seq
    n_blk = n_tok // nb
    region = ftm // tile
    rows_local = -(-(2 * nb + N_EXPERTS * tile) // MOE_CHUNK) * MOE_CHUNK
    tiles_global = (2 * n_tok) // tile + n_blk * N_EXPERTS + N_EXPERTS * (region - 1)
    n_ffn_max = -(-tiles_global // region)
    rows_global = n_ffn_max * ftm

    cnt = cnt_tiles[:, :, 0, :N_EXPERTS].reshape(n_blk, nb // POST_TM, N_EXPERTS).sum(axis=1).astype(jnp.int32)
    ntile = (cnt + (tile - 1)) // tile
    lfirst = jnp.cumsum(ntile, axis=1) - ntile
    tot = ntile.sum(axis=0)
    ptot = (tot + (region - 1)) // region * region
    ebase = jnp.cumsum(ptot) - ptot
    gfirst = ebase[None, :] + jnp.cumsum(ntile, axis=0) - ntile
    pad = ptot - tot
    n_ffn = (ptot.sum() // region).reshape(1)
    ends = jnp.cumsum(ptot) // region
    texp = jnp.minimum((jnp.arange(n_ffn_max, dtype=jnp.int32)[:, None] >= ends[None, :]).sum(axis=1),
                       N_EXPERTS - 1).astype(jnp.int32)
    nloc = ntile.sum(axis=1).astype(jnp.int32)
    lt = jnp.arange(rows_local // tile, dtype=jnp.int32)[None, :, None]
    in_seg = (lt >= lfirst[:, None, :]) & (lt < (lfirst + ntile)[:, None, :])
    gtile = (jnp.where(in_seg, (gfirst - lfirst)[:, None, :], 0).sum(axis=2) + lt[:, :, 0]).astype(jnp.int32)
    padstart = (gfirst[-1] + ntile[-1]).astype(jnp.int32)

    h2f = h2.reshape(n_tok, d)
    combf = comb.reshape(n_tok, LANES)
    col, xg = pl.pallas_call(
        _moe_gather_kernel,
        out_shape=(jax.ShapeDtypeStruct((n_tok, LANES), F32),
                   jax.ShapeDtypeStruct((rows_global, d), BF16)),
        grid_spec=pltpu.PrefetchScalarGridSpec(
            num_scalar_prefetch=4,
            grid=(n_blk,),
            in_specs=[pl.BlockSpec((nb, d), lambda j, *_: (j, 0)),
                      pl.BlockSpec((nb, LANES), lambda j, *_: (j, 0))],
            out_specs=(pl.BlockSpec((nb, LANES), lambda j, *_: (j, 0)),
                       pl.BlockSpec(memory_space=pl.ANY)),
            scratch_shapes=[
                pltpu.VMEM((rows_local, d), BF16),
                pltpu.VMEM((LANES, nb), F32),
                pltpu.VMEM((tile, d), BF16),
                pltpu.SemaphoreType.DMA,
            ]),
        compiler_params=pltpu.CompilerParams(dimension_semantics=("arbitrary",),
                                             vmem_limit_bytes=VMEM_LIMIT_BYTES),
        name="moe_gather",
    )(nloc, gtile, padstart, pad, h2f, combf)

    last = lambda t, te, nt: jnp.minimum(t, nt[0] - 1)
    y = pl.pallas_call(
        _moe_ffn_kernel,
        out_shape=jax.ShapeDtypeStruct((rows_global, d), BF16),
        grid_spec=pltpu.PrefetchScalarGridSpec(
            num_scalar_prefetch=2,
            grid=(n_ffn_max,),
            in_specs=[pl.BlockSpec((ftm, d), lambda t, te, nt: (last(t, te, nt), 0)),
                      pl.BlockSpec((None, d, EXPERT_FF), lambda t, te, nt: (te[last(t, te, nt)], 0, 0)),
                      pl.BlockSpec((None, d, EXPERT_FF), lambda t, te, nt: (te[last(t, te, nt)], 0, 0)),
                      pl.BlockSpec((None, EXPERT_FF, d), lambda t, te, nt: (te[last(t, te, nt)], 0, 0))],
            out_specs=pl.BlockSpec((ftm, d), lambda t, te, nt: (last(t, te, nt), 0)),
            scratch_shapes=[pltpu.VMEM((d, EXPERT_FF), BF16), pltpu.VMEM((d, EXPERT_FF), BF16),
                            pltpu.VMEM((EXPERT_FF, d), BF16)]),
        compiler_params=pltpu.CompilerParams(dimension_semantics=("arbitrary",),
                                             vmem_limit_bytes=VMEM_LIMIT_BYTES),
        name="moe_ffn",
    )(texp, n_ffn, xg, w_gate, w_up, w_down)

    out = pl.pallas_call(
        _moe_scatter_kernel,
        out_shape=jax.ShapeDtypeStruct((n_tok, d), F32),
        grid_spec=pltpu.PrefetchScalarGridSpec(
            num_scalar_prefetch=2,
            grid=(n_blk,),
            in_specs=[pl.BlockSpec((nb, LANES), lambda j, *_: (j, 0)),
                      pl.BlockSpec((nb, d), lambda j, *_: (j, 0)),
                      pl.BlockSpec((None, 6, d), lambda j, *_: ((j * nb) // seq, 0, 0)),
                      pl.BlockSpec(memory_space=pl.ANY)],
            out_specs=pl.BlockSpec((nb, d), lambda j, *_: (j, 0)),
            scratch_shapes=[pltpu.VMEM((2, rows_local, d), BF16),
                            pltpu.SemaphoreType.DMA((2,))]),
        compiler_params=pltpu.CompilerParams(dimension_semantics=("arbitrary",),
                                             vmem_limit_bytes=VMEM_LIMIT_BYTES),
        name="moe_scatter",
    )(nloc, gtile, col, x1.reshape(n_tok, d), mod, y)
    return out.reshape(bsz, seq, d)


def _layer(x, mod, rel_bias, norm1, w_in, q_norm, k_norm, conv_w, attn_out_norm, conv_out_norm, w_out,
           norm2, w_group_router, b_group_router, w_expert_router, b_expert_router, w_gate, w_up, w_down):
    bsz, seq, d = x.shape
    aw = ATTN_WIDTH
    topk = min(TOPK_MAX, seq // 4)

    offs = np.cumsum([0, aw, aw, aw, IDX_HEADS * IDX_DIM, IDX_DIM, IDX_HEADS, CONV_WIDTH, CONV_WIDTH, CONV_WIDTH])
    col = lambda n: w_in[:, int(offs[n]):int(offs[n + 1])]
    wm = jnp.concatenate([col(0), col(1), col(3), col(6), col(7), col(8)], axis=1).astype(BF16)
    wvt = col(2).T.astype(BF16)
    wki = jnp.concatenate([col(4), col(4)], axis=1).astype(BF16)
    wwit = col(5).T.astype(BF16)
    qg = (jnp.tile(q_norm, ATTN_HEADS) * ((HEAD_DIM ** -0.5) * LOG2E))[None, :]
    kg = jnp.tile(k_norm, ATTN_HEADS)[None, :]
    grp = np.arange(aw) // CONV_GROUP_DIM
    gmat = jnp.asarray((grp[:, None] == grp[None, :]).astype(np.float32) / CONV_GROUP_DIM, dtype=BF16)

    q, k, vt, qi, ki, wit, cn = _pre_call(
        x, mod, norm1[None, :], wm, wvt, wki, wwit, qg, kg, conv_w, conv_out_norm.reshape(1, -1), gmat)

    bounds = jnp.asarray(_bucket_boundaries())
    an = _attn_call(rel_bias, bounds, q, qi, wit, k, ki, vt, attn_out_norm.reshape(1, -1), topk)

    wr = jnp.concatenate([w_expert_router, w_group_router,
                          jnp.zeros((d, LANES - N_EXPERTS - N_GROUPS), F32)], axis=1).astype(BF16)
    br = jnp.concatenate([b_expert_router, b_group_router,
                          jnp.zeros((LANES - N_EXPERTS - N_GROUPS,), F32)])[None, :]
    x1, h2, comb, cnt_tiles = _post_call(an, cn, x, mod, norm2[None, :], w_out[:aw].astype(BF16),
                                         w_out[aw:].astype(BF16), wr, br)

    return _moe_call(h2, comb, cnt_tiles, x1, mod, w_gate, w_up, w_down)


def kernel(x, c, rel_bias, w_ada, b_ada, norm1, w_in, q_norm, k_norm, conv_w, attn_out_norm, conv_out_norm,
           w_out, norm2, w_group_router, b_group_router, w_expert_router, b_expert_router, w_gate, w_up,
           w_down):
    bsz, seq, d = x.shape
    assert d == D_MODEL and seq % max(PRE_TM, POST_TM, MOE_TM) == 0 and ATT_TQ == ATT_TK
    depth = w_ada.shape[0]
    for l in range(depth):
        mod = _mod_call(c, w_ada[l], b_ada[l][None, :]).reshape(bsz, 6, d)
        x = _layer(x, mod, rel_bias, norm1[l], w_in[l], q_norm[l], k_norm[l], conv_w[l], attn_out_norm[l],
                   conv_out_norm[l], w_out[l], norm2[l], w_group_router[l], b_group_router[l],
                   w_expert_router[l], b_expert_router[l], w_gate[l], w_up[l], w_down[l])
    return x
```
